```python
import math
import jax, jax.numpy as jnp
from jax import lax
import numpy as np

D_MODEL = 1024
BATCH = 4
SEQ = 4096
DEPTH = 1
DEC_BATCH = 128
DEC_SEQ = 8
PAST_LEN = 2048
PAGE_SIZE = 128

SSM_WIDTH = D_MODEL // 2
SSM_GROUP = 16
N_SSM_GROUPS = SSM_WIDTH // SSM_GROUP
SSM_STATE = 64
DT_MIN = 1e-3
DT_MAX = 1e-1
HEAD_DIM = 64
N_Q_HEADS = (D_MODEL // 2) // HEAD_DIM
N_KV_HEADS = 2
GQA = N_Q_HEADS // N_KV_HEADS
CMP_LEN = 32
CMP_STRIDE = 16
SLC_BLOCK = 64
TOP_N = 8
WINDOW = 512
Q_BLOCK = 128
NEG_INF = -1e30
FORCE_BONUS = 1e4
Q_W = N_Q_HEADS * HEAD_DIM
KV_W = 2 * N_KV_HEADS * HEAD_DIM
NSA_GATE_W = 3 * N_Q_HEADS
D_IN_PROJ = SSM_WIDTH + Q_W + 3 * KV_W + NSA_GATE_W + 2 * D_MODEL
N_EXPERT_GROUPS = 4
EXPERTS_PER_GROUP = 4
N_EXPERTS = N_EXPERT_GROUPS * EXPERTS_PER_GROUP
TOP_K_INNER = 2
D_FF_EXPERT = 256
MOE_CHUNK = 2048
PLE_DIM = 256
RMS_EPS = 1e-6

kernel_name = "hybrid_s5_nsa_hmoe_decode_step"


def rms_norm(x, g):
    xf = x.astype(jnp.float32)
    inv = lax.rsqrt(jnp.mean(xf * xf, axis=-1, keepdims=True) + RMS_EPS)
    return (xf * inv * g.astype(jnp.float32)).astype(x.dtype)


def gather_pages(pool, page_table):
    rows = pool[page_table]
    return rows.reshape((rows.shape[0], rows.shape[1] * rows.shape[2]) + rows.shape[3:])


def s5_scan(u, h0, lam_re, lam_im, log_dt, b_re, b_im, c_re, c_im, d_skip):
    f32 = jnp.float32
    n, t, _ = u.shape
    lam = lax.complex(lam_re.astype(f32), lam_im.astype(f32))
    dt = jnp.exp(log_dt.astype(f32))[:, None]
    a_bar = jnp.exp(lam * dt)
    b = lax.complex(b_re.astype(f32), b_im.astype(f32))
    b_bar = ((a_bar - 1.0) / lam)[..., None] * b
    uf = u.astype(f32)
    ug = uf.reshape(n, t, N_SSM_GROUPS, SSM_GROUP)
    bu = lax.complex(jnp.einsum('ntgc,gpc->ntgp', ug, b_bar.real),
                     jnp.einsum('ntgc,gpc->ntgp', ug, b_bar.imag))
    a_seq = jnp.broadcast_to(a_bar, bu.shape)

    def combine(left, right):
        return (left[0] * right[0], right[0] * left[1] + right[1])

    a_cum, h = lax.associative_scan(combine, (a_seq, bu), axis=1)
    if h0 is not None:
        h = h + a_cum * lax.complex(h0[..., 0].astype(f32), h0[..., 1].astype(f32))[:, None]
    y = (jnp.einsum('ntgp,gcp->ntgc', h.real, c_re.astype(f32))
         - jnp.einsum('ntgp,gcp->ntgc', h.imag, c_im.astype(f32)))
    y = y.reshape(n, t, SSM_WIDTH) + d_skip.astype(f32) * uf
    h_last = h[:, -1]
    return y, jnp.stack([h_last.real, h_last.imag], axis=-1)


def compress_blocks(kv, pe, w1, w2):
    n, tp = kv.shape[:2]
    r = CMP_LEN // CMP_STRIDE
    n_chunk = tp // CMP_STRIDE
    n_cmp = n_chunk - r + 1
    chunks = kv.reshape(n, n_chunk, CMP_STRIDE, 2, N_KV_HEADS, HEAD_DIM)
    pre = None
    for s in range(r):
        sl = slice(s * CMP_STRIDE, (s + 1) * CMP_STRIDE)
        pe_s = jnp.swapaxes(pe[:, sl], 0, 1)[:, :, None, :].astype(kv.dtype)
        part = jnp.einsum('ncikhd,kide->nckhe', chunks + pe_s, w1[:, sl].astype(kv.dtype))
        part = part[:, s:s + n_cmp]
        pre = part if pre is None else pre + part
    return jnp.einsum('njkhe,kef->njkhf', jax.nn.gelu(pre), w2.astype(kv.dtype))


def nsa_attention(q, gates, kv_cmp, kv_slc, win_ext, q_pos0, n_pre, cmp_pe, cmp_w1, cmp_w2):
    f32 = jnp.float32
    n, tq = q.shape[:2]
    tk = kv_cmp.shape[1]
    tpad = -(-tk // SLC_BLOCK) * SLC_BLOCK
    padw = ((0, 0), (0, tpad - tk), (0, 0), (0, 0), (0, 0))
    kv_cmp = jnp.pad(kv_cmp.astype(f32), padw)
    kv_slc = jnp.pad(kv_slc.astype(f32), padw)
    cmp = compress_blocks(kv_cmp, cmp_pe, cmp_w1, cmp_w2)
    kc, vc = cmp[:, :, 0], cmp[:, :, 1]
    n_cmp = cmp.shape[1]
    n_slc = tpad // SLC_BLOCK
    ks = kv_slc[:, :, 0].reshape(n, n_slc, SLC_BLOCK, N_KV_HEADS, HEAD_DIM).transpose(0, 3, 1, 2, 4)
    vs = kv_slc[:, :, 1].reshape(n, n_slc, SLC_BLOCK, N_KV_HEADS, HEAD_DIM).transpose(0, 3, 1, 2, 4)
    cmp_start = jnp.arange(n_cmp, dtype=jnp.int32) * CMP_STRIDE
    cmp_end = cmp_start + (CMP_LEN - 1)
    slc_start = jnp.arange(n_slc, dtype=jnp.int32) * SLC_BLOCK
    overlap = ((cmp_start[:, None] <= slc_start[None, :] + (SLC_BLOCK - 1))
               & (cmp_end[:, None] >= slc_start[None, :])).astype(f32)
    k_sel = min(TOP_N, n_slc)
    qb = min(Q_BLOCK, tq)
    n_blk = tq // qb
    lw = n_pre + qb
    win_ext = win_ext.astype(f32)
    b_idx = jnp.arange(n)[:, None, None, None]
    h_idx = jnp.arange(N_KV_HEADS)[None, :, None, None]
    sid = jnp.arange(n_slc, dtype=jnp.int32)

    def block(args):
        q_b, g_b, pos, i = args
        qf = q_b.astype(f32) * (HEAD_DIM ** -0.5)
        s_c = jnp.einsum('nqhgd,njhd->nhgqj', qf, kc)
        m_c = cmp_end[None, :] <= pos[:, None]
        p_c = jax.nn.softmax(jnp.where(m_c, s_c, NEG_INF), axis=-1) * m_c
        o_c = jnp.einsum('nhgqj,njhd->nqhgd', p_c, vc)
        imp = jnp.einsum('nhgqj,js->nhqs', p_c, overlap)
        cur = pos // SLC_BLOCK
        forced = (sid[None, :] == 0) | (sid[None, :] == cur[:, None]) | (sid[None, :] == cur[:, None] - 1)
        imp = jnp.where(forced, imp + FORCE_BONUS, imp)
        imp = jnp.where(slc_start[None, :] <= pos[:, None], imp, NEG_INF)
        _, idx = lax.top_k(imp, k_sel)
        k_g = ks[b_idx, h_idx, idx]
        v_g = vs[b_idx, h_idx, idx]
        kpos = slc_start[idx][..., None] + jnp.arange(SLC_BLOCK, dtype=jnp.int32)
        m_s = (kpos <= pos[None, None, :, None, None]).reshape(n, N_KV_HEADS, 1, qb, k_sel * SLC_BLOCK)
        s_s = jnp.einsum('nqhgd,nhqkld->nhgqkl', qf, k_g).reshape(n, N_KV_HEADS, GQA, qb, k_sel * SLC_BLOCK)
        p_s = jax.nn.softmax(jnp.where(m_s, s_s, NEG_INF), axis=-1)
        p_s = p_s.reshape(n, N_KV_HEADS, GQA, qb, k_sel, SLC_BLOCK)
        o_s = jnp.einsum('nhgqkl,nhqkld->nqhgd', p_s, v_g)
        kw = lax.dynamic_slice_in_dim(win_ext, i * qb, lw, axis=1)
        wpos = q_pos0 - n_pre + i * qb + jnp.arange(lw, dtype=jnp.int32)
        dlt = pos[:, None] - wpos[None, :]
        m_w = (dlt >= 0) & (dlt < WINDOW) & (wpos[None, :] >= 0)
        s_w = jnp.einsum('nqhgd,nkhd->nhgqk', qf, kw[:, :, 0])
        p_w = jax.nn.softmax(jnp.where(m_w, s_w, NEG_INF), axis=-1)
        o_w = jnp.einsum('nhgqk,nkhd->nqhgd', p_w, kw[:, :, 1])
        return g_b[..., 0:1] * o_c + g_b[..., 1:2] * o_s + g_b[..., 2:3] * o_w

    xs = (q.reshape(n, n_blk, qb, N_KV_HEADS, GQA, HEAD_DIM).swapaxes(0, 1),
          gates.reshape(n, n_blk, qb, N_KV_HEADS, GQA, 3).swapaxes(0, 1),
          (q_pos0 + jnp.arange(tq, dtype=jnp.int32)).reshape(n_blk, qb),
          jnp.arange(n_blk, dtype=jnp.int32))
    out = lax.map(block, xs)
    return out.swapaxes(0, 1).reshape(n, tq, N_KV_HEADS, GQA, HEAD_DIM)


def token_mixer(h, q_pos0, past_cmp, past_slc, past_win, h0,
                w_in, lam_re, lam_im, log_dt, b_re, b_im, c_re, c_im, d_skip, w_glu, b_glu,
                cmp_pe, cmp_w1, cmp_w2, w_ssm_out, w_nsa_out, w_o):
    n, t, _ = h.shape
    z = h @ w_in
    cuts = np.cumsum([SSM_WIDTH, Q_W, KV_W, KV_W, KV_W, NSA_GATE_W, D_MODEL]).tolist()
    u, q, kvc, kvs, kvw, g_nsa, g_a, g_b = jnp.split(z, cuts, axis=-1)
    y_ssm, ssm_state = s5_scan(u, h0, lam_re, lam_im, log_dt, b_re, b_im, c_re, c_im, d_skip)
    zg = jax.nn.gelu(y_ssm)
    glu = zg * jax.nn.sigmoid(zg @ w_glu.astype(jnp.float32) + b_glu.astype(jnp.float32))
    kv_shape = (n, t, 2, N_KV_HEADS, HEAD_DIM)
    kvc = kvc.reshape(kv_shape)
    kvs = kvs.reshape(kv_shape)
    kvw = kvw.reshape(kv_shape)
    full_c = kvc if past_cmp is None else jnp.concatenate([past_cmp.astype(kvc.dtype), kvc], axis=1)
    full_s = kvs if past_slc is None else jnp.concatenate([past_slc.astype(kvs.dtype), kvs], axis=1)
    if past_win is None:
        n_pre = WINDOW
        win_ext = jnp.concatenate([jnp.zeros((n, WINDOW, 2, N_KV_HEADS, HEAD_DIM), kvw.dtype), kvw], axis=1)
        new_win = kvw[:, t - min(WINDOW, t):]
    else:
        n_pre = past_win.shape[1]
        win_ext = jnp.concatenate([past_win.astype(kvw.dtype), kvw], axis=1)
        keep = min(WINDOW, n_pre + t)
        new_win = win_ext[:, n_pre + t - keep:]
    qh = q.reshape(n, t, N_KV_HEADS, GQA, HEAD_DIM)
    gh = jax.nn.sigmoid(g_nsa.astype(jnp.float32)).reshape(n, t, N_KV_HEADS, GQA, 3)
    o_nsa = nsa_attention(qh, gh, full_c, full_s, win_ext, q_pos0, n_pre, cmp_pe, cmp_w1, cmp_w2)
    o_nsa = o_nsa.reshape(n, t, Q_W).astype(h.dtype)
    a_br = glu.astype(h.dtype) @ w_ssm_out
    b_br = o_nsa @ w_nsa_out
    merged = jax.nn.sigmoid(g_a) * a_br + jax.nn.sigmoid(g_b) * b_br
    return merged @ w_o, kvc, kvs, new_win, ssm_state


def hier_moe(h, w_rg, b_rg, w_re, b_re, w_gate, w_up, w_down):
    f32 = jnp.float32
    n, t, dm = h.shape
    m = n * t
    c = math.gcd(m, MOE_CHUNK)

    def chunk(hf):
        glog = (hf @ w_rg + b_rg).astype(f32)
        gsel = jnp.argmax(glog, axis=-1)
        gw = jnp.take_along_axis(jax.nn.softmax(glog, axis=-1), gsel[:, None], axis=-1)
        elog = (hf @ w_re + b_re).astype(f32).reshape(c, N_EXPERT_GROUPS, EXPERTS_PER_GROUP)
        elog = jnp.take_along_axis(elog, gsel[:, None, None], axis=1)[:, 0]
        top_v, top_i = lax.top_k(elog, TOP_K_INNER)
        wts = jax.nn.softmax(top_v, axis=-1) * gw
        eid = gsel[:, None] * EXPERTS_PER_GROUP + top_i
        comb = jnp.einsum('cke,ck->ce', jax.nn.one_hot(eid, N_EXPERTS, dtype=f32), wts)
        a = jnp.einsum('cd,edf->cef', hf, w_gate)
        b = jnp.einsum('cd,edf->cef', hf, w_up)
        act = jax.nn.silu(a) * b * comb[..., None].astype(hf.dtype)
        return jnp.einsum('cef,efd->cd', act, w_down)

    return lax.map(chunk, h.reshape(m // c, c, dm)).reshape(n, t, dm)


def setup_inputs(seed: int = 0) -> dict:
    key = jax.random.key(seed)
    keys = iter(jax.random.split(key, 48))

    def nrm(shape, scale):
        return jax.random.normal(next(keys), shape, jnp.float32) * scale

    n_pages = PAST_LEN // PAGE_SIZE
    n_used = DEC_BATCH * n_pages
    n_pool = n_used + n_used // 4
    win_buf = min(WINDOW, PAST_LEN)
    kvt = (2, N_KV_HEADS, HEAD_DIM)
    x_prompt = nrm((BATCH, SEQ, D_MODEL), 1.0)
    x_sample = nrm((DEC_BATCH, DEC_SEQ, D_MODEL), 1.0)
    p_prompt = nrm((DEPTH, BATCH, SEQ, PLE_DIM), 1.0)
    p_sample = nrm((DEPTH, DEC_BATCH, DEC_SEQ, PLE_DIM), 1.0)
    cache_cmp_kv = nrm((DEPTH, n_pool, PAGE_SIZE) + kvt, 1.0)
    cache_slc_kv = nrm((DEPTH, n_pool, PAGE_SIZE) + kvt, 1.0)
    cache_win_kv = nrm((DEPTH, DEC_BATCH, win_buf) + kvt, 1.0)
    state_ssm = nrm((DEPTH, DEC_BATCH, N_SSM_GROUPS, SSM_STATE, 2), 1.0)
    page_table = jax.random.permutation(next(keys), n_pool)[:n_used].reshape(DEC_BATCH, n_pages).astype(jnp.int32)
    norm1_g = 1.0 + nrm((DEPTH, D_MODEL), 0.02)
    w_in = nrm((DEPTH, D_MODEL, D_IN_PROJ), D_MODEL ** -0.5)
    ssm_lam_re = -0.5 + nrm((DEPTH, N_SSM_GROUPS, SSM_STATE), 0.01)
    ssm_lam_im = jnp.pi * jnp.arange(SSM_STATE, dtype=jnp.float32) + nrm((DEPTH, N_SSM_GROUPS, SSM_STATE), 0.01)
    ssm_log_dt = jax.random.uniform(next(keys), (DEPTH, N_SSM_GROUPS), jnp.float32, math.log(DT_MIN), math.log(DT_MAX))
    ssm_b_re = nrm((DEPTH, N_SSM_GROUPS, SSM_STATE, SSM_GROUP), (2 * SSM_GROUP) ** -0.5)
    ssm_b_im = nrm((DEPTH, N_SSM_GROUPS, SSM_STATE, SSM_GROUP), (2 * SSM_GROUP) ** -0.5)
    ssm_c_re = nrm((DEPTH, N_SSM_GROUPS, SSM_GROUP, SSM_STATE), (2 * SSM_STATE) ** -0.5)
    ssm_c_im = nrm((DEPTH, N_SSM_GROUPS, SSM_GROUP, SSM_STATE), (2 * SSM_STATE) ** -0.5)
    ssm_d = nrm((DEPTH, SSM_WIDTH), 1.0)
    w_glu = nrm((DEPTH, SSM_WIDTH, SSM_WIDTH), SSM_WIDTH ** -0.5)
    b_glu = nrm((DEPTH, SSM_WIDTH), 0.02)
    cmp_pe = nrm((DEPTH, 2, CMP_LEN, HEAD_DIM), 0.1)
    cmp_w1 = nrm((DEPTH, 2, CMP_LEN, HEAD_DIM, HEAD_DIM), (CMP_LEN * HEAD_DIM) ** -0.5)
    cmp_w2 = nrm((DEPTH, 2, HEAD_DIM, HEAD_DIM), HEAD_DIM ** -0.5)
    w_ssm_out = nrm((DEPTH, SSM_WIDTH, D_MODEL), SSM_WIDTH ** -0.5)
    w_nsa_out = nrm((DEPTH, Q_W, D_MODEL), Q_W ** -0.5)
    w_o = nrm((DEPTH, D_MODEL, D_MODEL), D_MODEL ** -0.5)
    norm2_g = 1.0 + nrm((DEPTH, D_MODEL), 0.02)
    w_route_group = nrm((DEPTH, D_MODEL, N_EXPERT_GROUPS), D_MODEL ** -0.5)
    b_route_group = nrm((DEPTH, N_EXPERT_GROUPS), 0.01)
    w_route_expert = nrm((DEPTH, D_MODEL, N_EXPERTS), D_MODEL ** -0.5)
    b_route_expert = nrm((DEPTH, N_EXPERTS), 0.01)
    w_exp_gate = nrm((DEPTH, N_EXPERTS, D_MODEL, D_FF_EXPERT), D_MODEL ** -0.5)
    w_exp_up = nrm((DEPTH, N_EXPERTS, D_MODEL, D_FF_EXPERT), D_MODEL ** -0.5)
    w_exp_down = nrm((DEPTH, N_EXPERTS, D_FF_EXPERT, D_MODEL), D_FF_EXPERT ** -0.5)
    w_ple = nrm((DEPTH, PLE_DIM, D_MODEL), PLE_DIM ** -0.5)
    w_ple_gate = nrm((DEPTH, D_MODEL, D_MODEL), D_MODEL ** -0.5)
    final_norm_g = 1.0 + nrm((D_MODEL,), 0.02)
    return {"x_prompt": x_prompt, "x_sample": x_sample, "p_prompt": p_prompt, "p_sample": p_sample,
            "cache_cmp_kv": cache_cmp_kv, "cache_slc_kv": cache_slc_kv, "cache_win_kv": cache_win_kv,
            "state_ssm": state_ssm, "page_table": page_table,
            "norm1_g": norm1_g, "w_in": w_in, "ssm_lam_re": ssm_lam_re, "ssm_lam_im": ssm_lam_im,
            "ssm_log_dt": ssm_log_dt, "ssm_b_re": ssm_b_re, "ssm_b_im": ssm_b_im, "ssm_c_re": ssm_c_re,
            "ssm_c_im": ssm_c_im, "ssm_d": ssm_d, "w_glu": w_glu, "b_glu": b_glu,
            "cmp_pe": cmp_pe, "cmp_w1": cmp_w1, "cmp_w2": cmp_w2,
            "w_ssm_out": w_ssm_out, "w_nsa_out": w_nsa_out, "w_o": w_o, "norm2_g": norm2_g,
            "w_route_group": w_route_group, "b_route_group": b_route_group,
            "w_route_expert": w_route_expert, "b_route_expert": b_route_expert,
            "w_exp_gate": w_exp_gate, "w_exp_up": w_exp_up, "w_exp_down": w_exp_down,
            "w_ple": w_ple, "w_ple_gate": w_ple_gate, "final_norm_g": final_norm_g}


def reference(x_prompt, x_sample, p_prompt, p_sample, cache_cmp_kv, cache_slc_kv, cache_win_kv, state_ssm,
              page_table, norm1_g, w_in, ssm_lam_re, ssm_lam_im, ssm_log_dt, ssm_b_re, ssm_b_im, ssm_c_re,
              ssm_c_im, ssm_d, w_glu, b_glu, cmp_pe, cmp_w1, cmp_w2, w_ssm_out, w_nsa_out, w_o, norm2_g,
              w_route_group, b_route_group, w_route_expert, b_route_expert, w_exp_gate, w_exp_up, w_exp_down,
              w_ple, w_ple_gate, final_norm_g):
    past_len = page_table.shape[1] * PAGE_SIZE

    def layer(x, p_l, l, q_pos0, past_cmp, past_slc, past_win, h0):
        h = rms_norm(x, norm1_g[l])
        mix, kvc, kvs, new_win, ssm_st = token_mixer(
            h, q_pos0, past_cmp, past_slc, past_win, h0,
            w_in[l], ssm_lam_re[l], ssm_lam_im[l], ssm_log_dt[l], ssm_b_re[l], ssm_b_im[l],
            ssm_c_re[l], ssm_c_im[l], ssm_d[l], w_glu[l], b_glu[l], cmp_pe[l], cmp_w1[l], cmp_w2[l],
            w_ssm_out[l], w_nsa_out[l], w_o[l])
        x = x + mix
        x = x + hier_moe(rms_norm(x, norm2_g[l]), w_route_group[l], b_route_group[l], w_route_expert[l],
                         b_route_expert[l], w_exp_gate[l], w_exp_up[l], w_exp_down[l])
        x = x + jax.nn.sigmoid(x @ w_ple_gate[l]) * (p_l @ w_ple[l])
        return x, kvc, kvs, new_win, ssm_st

    xp = x_prompt
    st_p = []
    for l in range(DEPTH):
        xp, *st = layer(xp, p_prompt[l], l, 0, None, None, None, None)
        st_p.append(st)
    xs = x_sample
    st_s = []
    for l in range(DEPTH):
        xs, *st = layer(xs, p_sample[l], l, past_len,
                        gather_pages(cache_cmp_kv[l], page_table),
                        gather_pages(cache_slc_kv[l], page_table),
                        cache_win_kv[l], state_ssm[l])
        st_s.append(st)
    y_prompt = rms_norm(xp, final_norm_g)
    y_sample = rms_norm(xs, final_norm_g)
    new_cmp_kv_prompt = jnp.stack([s[0] for s in st_p])
    new_slc_kv_prompt = jnp.stack([s[1] for s in st_p])
    new_win_kv_prompt = jnp.stack([s[2] for s in st_p])
    new_ssm_prompt = jnp.stack([s[3] for s in st_p])
    new_cmp_kv_sample = jnp.stack([s[0] for s in st_s])
    new_slc_kv_sample = jnp.stack([s[1] for s in st_s])
    new_win_kv_sample = jnp.stack([s[2] for s in st_s])
    new_ssm_sample = jnp.stack([s[3] for s in st_s])
    return (y_prompt, y_sample, new_cmp_kv_prompt, new_slc_kv_prompt, new_win_kv_prompt, new_ssm_prompt,
            new_cmp_kv_sample, new_slc_kv_sample, new_win_kv_sample, new_ssm_sample)
```

```python
import functools
import math

import jax
import jax.numpy as jnp
import numpy as np
from jax import lax
from jax.experimental import pallas as pl
from jax.experimental.pallas import tpu as pltpu

F32 = jnp.float32
BF16 = jnp.bfloat16

D_MODEL = 1024
SSM_WIDTH = 512
SSM_GROUP = 16
N_SSM_GROUPS = 32
SSM_STATE = 64
HEAD_DIM = 64
N_Q_HEADS = 8
N_KV_HEADS = 2
GQA = 4
CMP_LEN = 32
CMP_STRIDE = 16
SLC_BLOCK = 64
TOP_N = 8
WINDOW = 512
Q_BLOCK = 128
NEG_INF = -1e30
FORCE_BONUS = 1e4
Q_W = 512
KV_W = 256
NSA_GATE_W = 24
N_EXPERT_GROUPS = 4
EXPERTS_PER_GROUP = 4
N_EXPERTS = 16
D_FF_EXPERT = 256
PLE_DIM = 256
RMS_EPS = 1e-6
PAGE_SIZE = 128

LANES = 128
SUBLANES = 8
N_STATE = N_SSM_GROUPS * SSM_STATE
MIB = 2 ** 20


def _cparams(sem, vmem_mib):
    return pltpu.CompilerParams(dimension_semantics=sem, vmem_limit_bytes=vmem_mib * MIB)


def _full(shape):
    nd = len(shape)
    return pl.BlockSpec(shape, lambda *_: (0,) * nd)


def _prompt_layout(nseq, t, tm):
    nb = t // tm
    return {
        "grid": (nseq, nb), "tm": tm,
        "a": lambda w: ((nseq * t, w), pl.BlockSpec((tm, w), lambda s, b: (s * nb + b, 0))),
        "b": lambda w: ((t, nseq * w), pl.BlockSpec((tm, w), lambda s, b: (b, s))),
    }


def _sample_layout(nseq, t):
    return {
        "grid": (1, t), "tm": nseq,
        "a": lambda w: ((nseq, t * w), pl.BlockSpec((nseq, w), lambda s, b: (0, b))),
        "b": lambda w: ((t * nseq, w), pl.BlockSpec((nseq, w), lambda s, b: (b, 0))),
    }


TK_SLC = 512
TK_WIN = 128


def _inproj_body(x_ref, g_ref, wa_ref, wgn_ref, wgab_ref,
                 u_ref, q_ref, kvc_ref, kvs_ref, kvw_ref, gn_ref, ga_ref, gb_ref, *attn_refs):
    x = x_ref[...]
    inv = lax.rsqrt(jnp.mean(x * x, axis=-1, keepdims=True) + RMS_EPS)
    h = (x * inv * g_ref[...]).astype(BF16)

    def mm(w):
        return jnp.dot(h, w, preferred_element_type=F32)

    u_ref[...] = mm(wa_ref[:, 0:512])
    q_ref[...] = mm(wa_ref[:, 512:1024]) * (HEAD_DIM ** -0.5)
    kvc_ref[...] = mm(wa_ref[:, 1024:1280])
    kvs = mm(wa_ref[:, 1280:1536])
    kvs_ref[...] = kvs
    kvw = mm(wa_ref[:, 1536:1792])
    kvw_ref[...] = kvw
    gn_ref[...] = jax.nn.sigmoid(mm(wgn_ref[...]))
    ga_ref[...] = jax.nn.sigmoid(mm(wgab_ref[:, 0:D_MODEL]))
    gb_ref[...] = jax.nn.sigmoid(mm(wgab_ref[:, D_MODEL:2 * D_MODEL]))
    if attn_refs:
        ksb_ref, kwb_ref, vst_ref, vwt_ref = attn_refs
        ksb_ref[...] = kvs[:, 0:LANES].astype(BF16)
        kwb_ref[...] = kvw[:, 0:LANES].astype(BF16)
        tm = kvs.shape[0]
        for c in range(tm // TK_SLC):
            vst_ref[0, c] = kvs[c * TK_SLC:(c + 1) * TK_SLC, LANES:2 * LANES].T.astype(BF16)
        for c in range(tm // TK_WIN):
            vwt_ref[0, c] = kvw[c * TK_WIN:(c + 1) * TK_WIN, LANES:2 * LANES].T.astype(BF16)


def _inproj(x2d, lay, g, wa, wgn, wgab, attn_operands):
    tm = lay["tm"]
    names = ["u", "q", "kvc", "kvs", "kvw", "gn", "ga", "gb"]
    widths = [SSM_WIDTH, Q_W, KV_W, KV_W, KV_W, LANES, D_MODEL, D_MODEL]
    out_shapes, out_specs = [], []
    for n, w in zip(names, widths):
        shp, spec = lay["b" if n == "u" else "a"](w)
        out_shapes.append(jax.ShapeDtypeStruct(shp, F32))
        out_specs.append(spec)
    if attn_operands:
        nseq, nb = lay["grid"]
        t = nb * tm
        for n in ("ksb", "kwb"):
            shp, spec = lay["a"](LANES)
            out_shapes.append(jax.ShapeDtypeStruct(shp, BF16))
            out_specs.append(spec)
        for n, tk in (("vst", TK_SLC), ("vwt", TK_WIN)):
            out_shapes.append(jax.ShapeDtypeStruct((nseq, t // tk, LANES, tk), BF16))
            out_specs.append(pl.BlockSpec((1, tm // tk, LANES, tk), lambda s, b: (s, b, 0, 0)))
        names = names + ["ksb", "kwb", "vst", "vwt"]
    x_shape, x_spec = lay["a"](D_MODEL)
    outs = pl.pallas_call(
        _inproj_body,
        grid=lay["grid"],
        in_specs=[x_spec, _full(g.shape), _full(wa.shape), _full(wgn.shape), _full(wgab.shape)],
        out_specs=out_specs,
        out_shape=out_shapes,
        compiler_params=_cparams(("arbitrary",) * len(lay["grid"]), 56),
        name="inproj",
    )(x2d.reshape(x_shape), g, wa, wgn, wgab)
    return dict(zip(names, outs))


def _s5_prompt_body(u_ref, wb_ref, ar_ref, ai_ref, cw_ref, d_ref, wglu_ref, bglu_ref, wso_ref,
                    abr_ref, hlast_ref, lhs_ref, bu_ref, h8_ref, p_ref, us_ref, hstate_ref):
    c = pl.program_id(0)
    nseq = 4
    r4 = u_ref.shape[0]
    tc = r4 // nseq
    half = tc // 2

    @pl.when(c == 0)
    def _():
        hstate_ref[...] = jnp.zeros_like(hstate_ref)

    u = u_ref[...]
    row2 = lax.broadcasted_iota(jnp.int32, (r4, SSM_WIDTH), 0)
    lo2 = (row2 % SUBLANES) < nseq
    up = pltpu.roll(u, r4 - nseq, axis=0)
    dn = pltpu.roll(u, nseq, axis=0)
    swapped = jnp.where(lo2, up, dn)
    zero = jnp.zeros_like(u)
    ev_re = jnp.where(lo2, u, zero).astype(BF16).reshape(half, SUBLANES, SSM_WIDTH)
    ev_im = jnp.where(lo2, zero, swapped).astype(BF16).reshape(half, SUBLANES, SSM_WIDTH)
    od_re = jnp.where(lo2, swapped, zero).astype(BF16).reshape(half, SUBLANES, SSM_WIDTH)
    od_im = jnp.where(lo2, zero, u).astype(BF16).reshape(half, SUBLANES, SSM_WIDTH)
    for j in range(4):
        sl = slice(LANES * j, LANES * (j + 1))
        lhs_ref[:, 0:8, 256 * j:256 * j + LANES] = ev_re[:, :, sl]
        lhs_ref[:, 0:8, 256 * j + LANES:256 * (j + 1)] = ev_im[:, :, sl]
        lhs_ref[:, 8:16, 256 * j:256 * j + LANES] = od_re[:, :, sl]
        lhs_ref[:, 8:16, 256 * j + LANES:256 * (j + 1)] = od_im[:, :, sl]
    for j in range(4):
        lhs = lhs_ref[:, :, 256 * j:256 * (j + 1)].reshape(tc * SUBLANES, 256)
        bu_ref[:, 512 * j:512 * (j + 1)] = jnp.dot(lhs, wb_ref[j], preferred_element_type=F32)

    for lc in range(4):
        sl = slice(512 * lc, 512 * (lc + 1))
        ar = ar_ref[:, sl]
        ai = ai_ref[:, sl]

        def step(t, h, sl=sl, ar=ar, ai=ai):
            r0 = pl.multiple_of(t * SUBLANES, SUBLANES)
            h = ar * h + ai * pltpu.roll(h, nseq, axis=0) + bu_ref[pl.ds(r0, SUBLANES), sl]
            h8_ref[pl.ds(r0, SUBLANES), sl] = h
            return h

        hstate_ref[:, sl] = lax.fori_loop(0, tc, step, hstate_ref[:, sl], unroll=8)
    hlast_ref[...] = hstate_ref[...]

    for j in range(4):
        pj = jnp.dot(h8_ref[:, 512 * j:512 * (j + 1)].astype(BF16), cw_ref[j], preferred_element_type=F32)
        p_ref[2 * j] = pj[:, 0:LANES]
        p_ref[2 * j + 1] = pj[:, LANES:2 * LANES]
        us_ref[j] = u[:, LANES * j:LANES * (j + 1)]
    ys = []
    for s in range(nseq):
        parts = []
        for j in range(4):
            re = p_ref[2 * j, pl.ds(s, tc, stride=SUBLANES), :]
            im = p_ref[2 * j + 1, pl.ds(nseq + s, tc, stride=SUBLANES), :]
            us = us_ref[j, pl.ds(s, tc, stride=nseq), :]
            parts.append(re + im + d_ref[:, LANES * j:LANES * (j + 1)] * us)
        ys.append(jnp.concatenate(parts, axis=1))
    y = jnp.concatenate(ys, axis=0)
    zg = jax.nn.gelu(y)
    gate = jnp.dot(zg.astype(BF16), wglu_ref[...], preferred_element_type=F32) + bglu_ref[...]
    glu = (zg * jax.nn.sigmoid(gate)).astype(BF16)
    abr = jnp.dot(glu, wso_ref[...], preferred_element_type=F32)
    for s in range(nseq):
        abr_ref[s] = abr[s * tc:(s + 1) * tc]


def _s5_prompt(u_ts, sp, wglu, bglu, wso, t_total, tc):
    nseq = 4
    grid = (t_total // tc,)
    abr, hlast = pl.pallas_call(
        _s5_prompt_body,
        grid=grid,
        in_specs=[pl.BlockSpec((tc * nseq, SSM_WIDTH), lambda c: (c, 0)),
                  _full(sp["wb8"].shape), _full(sp["ar8"].shape), _full(sp["ai8"].shape), _full(sp["cw8"].shape),
                  _full(sp["d"].shape), _full(wglu.shape), _full(bglu.shape), _full(wso.shape)],
        out_specs=[pl.BlockSpec((nseq, tc, D_MODEL), lambda c: (0, c, 0)),
                   pl.BlockSpec((SUBLANES, N_STATE), lambda c: (0, 0))],
        out_shape=[jax.ShapeDtypeStruct((nseq, t_total, D_MODEL), F32),
                   jax.ShapeDtypeStruct((SUBLANES, N_STATE), F32)],
        scratch_shapes=[pltpu.VMEM((tc // 2, 2 * SUBLANES, 1024), BF16),
                        pltpu.VMEM((tc * SUBLANES, N_STATE), F32),
                        pltpu.VMEM((tc * SUBLANES, N_STATE), F32),
                        pltpu.VMEM((8, tc * SUBLANES, LANES), F32),
                        pltpu.VMEM((4, tc * nseq, LANES), F32),
                        pltpu.VMEM((SUBLANES, N_STATE), F32)],
        compiler_params=_cparams(("arbitrary",), 56),
        name="s5_prompt",
    )(u_ts, sp["wb8"], sp["ar8"], sp["ai8"], sp["cw8"], sp["d"], wglu, bglu, wso)
    return abr, hlast


def _s5_params(lam_re, lam_im, log_dt, b_re, b_im, c_re, c_im, d_skip):
    lam = lax.complex(lam_re.astype(F32), lam_im.astype(F32))
    dt = jnp.exp(log_dt.astype(F32))[:, None]
    a_bar = jnp.exp(lam * dt)
    b = lax.complex(b_re.astype(F32), b_im.astype(F32))
    b_bar = ((a_bar - 1.0) / lam)[..., None] * b
    eye8 = jnp.eye(8, dtype=F32)

    def bd_b(m):
        return jnp.einsum("ab,jbpc->jacbp", eye8, m.reshape(4, 8, SSM_STATE, SSM_GROUP)).reshape(4, 128, 512)

    def bd_c(m):
        return jnp.einsum("ab,jbcp->japbc", eye8, m.reshape(4, 8, SSM_GROUP, SSM_STATE)).reshape(4, 512, 128)

    wre, wim = bd_b(b_bar.real), bd_b(b_bar.imag)
    cre, cim = bd_c(c_re.astype(F32)), bd_c(c_im.astype(F32))
    ar = a_bar.real.reshape(1, N_STATE)
    ai = a_bar.imag.reshape(1, N_STATE)
    sign = jnp.concatenate([-jnp.ones((4, 1), F32), jnp.ones((4, 1), F32)], axis=0)
    return {
        "wb8": jnp.concatenate([wre, wim], axis=1).astype(BF16),
        "cw8": jnp.concatenate([cre, -cim], axis=2).astype(BF16),
        "ar8": jnp.broadcast_to(ar, (SUBLANES, N_STATE)),
        "ai8": sign * ai,
        "wre": wre.astype(BF16), "wim": wim.astype(BF16),
        "cre": cre.astype(BF16), "cim": cim.astype(BF16),
        "ar": ar, "ai": ai,
        "d": d_skip.astype(F32).reshape(1, SSM_WIDTH),
    }


def _cmp_params(cmp_pe, cmp_w1, cmp_w2):
    eye2 = jnp.eye(2, dtype=F32)
    w1s, pes = [], []
    for s in range(CMP_LEN // CMP_STRIDE):
        w = cmp_w1[:, s * CMP_STRIDE:(s + 1) * CMP_STRIDE].astype(F32)
        w1s.append(jnp.einsum("pk,qh,kide->ipqdkhe", eye2, eye2, w).reshape(CMP_STRIDE * KV_W, KV_W))
        pe = cmp_pe[:, s * CMP_STRIDE:(s + 1) * CMP_STRIDE].astype(F32)
        pes.append(jnp.broadcast_to(pe.transpose(1, 0, 2)[:, :, None, :],
                                    (CMP_STRIDE, 2, N_KV_HEADS, HEAD_DIM)).reshape(1, CMP_STRIDE * KV_W))
    w2 = jnp.einsum("pk,qh,kef->pqekhf", eye2, eye2, cmp_w2.astype(F32)).reshape(KV_W, KV_W)
    return {"w1": jnp.stack(w1s).astype(BF16),
            "pe": jnp.concatenate(pes, axis=0),
            "w2": w2.astype(BF16)}


def _compress_rows(x, pe_ref, w1_ref, w2_ref):
    nch = x.shape[0]
    p0 = jnp.dot((x + pe_ref[0:1, :]).astype(BF16), w1_ref[0], preferred_element_type=F32)
    p1 = jnp.dot((x + pe_ref[1:2, :]).astype(BF16), w1_ref[1], preferred_element_type=F32)
    pre = p0 + pltpu.roll(p1, nch - 1, axis=0)
    return jnp.dot(jax.nn.gelu(pre).astype(BF16), w2_ref[...], preferred_element_type=F32)


def _compress_prompt_body(x_ref, pe_ref, w1_ref, w2_ref, ck_ref, cvt_ref):
    out = _compress_rows(x_ref[0], pe_ref, w1_ref, w2_ref)
    ck_ref[0] = out[:, 0:LANES].astype(BF16)
    cvt_ref[0] = out[:, LANES:2 * LANES].T.astype(BF16)


def _compress_prompt(kvc, cp, nseq, t):
    nch = t // CMP_STRIDE
    x = kvc.reshape(nseq, nch, CMP_STRIDE * KV_W)
    return pl.pallas_call(
        _compress_prompt_body,
        grid=(nseq,),
        in_specs=[pl.BlockSpec((1, nch, CMP_STRIDE * KV_W), lambda n: (n, 0, 0)),
                  _full(cp["pe"].shape), _full(cp["w1"].shape), _full(cp["w2"].shape)],
        out_specs=[pl.BlockSpec((1, nch, LANES), lambda n: (n, 0, 0)),
                   pl.BlockSpec((1, LANES, nch), lambda n: (n, 0, 0))],
        out_shape=[jax.ShapeDtypeStruct((nseq, nch, LANES), BF16),
                   jax.ShapeDtypeStruct((nseq, LANES, nch), BF16)],
        compiler_params=_cparams(("arbitrary",), 48),
        name="compress_prompt",
    )(x, cp["pe"], cp["w1"], cp["w2"])


def _overlap_t(n_cmp_pad, n_slc_pad):
    j = np.arange(n_cmp_pad)[None, :]
    s = np.arange(n_slc_pad)[:, None]
    ov = (j * CMP_STRIDE <= s * SLC_BLOCK + SLC_BLOCK - 1) & (j * CMP_STRIDE + CMP_LEN - 1 >= s * SLC_BLOCK)
    return jnp.asarray(ov, dtype=BF16)


def _softmax_cols(s, valid):
    sm = jnp.where(valid, s, NEG_INF)
    mx = jnp.max(sm, axis=0, keepdims=True)
    e = jnp.where(valid, jnp.exp(sm - mx), 0.0)
    l = jnp.sum(e, axis=0, keepdims=True)
    return e * (1.0 / jnp.maximum(l, 1e-30))


def _select_blocks(imp, blk, pos, nblk, axis=0):
    cur = pos // SLC_BLOCK
    forced = (blk == 0) | (blk == cur) | (blk == cur - 1)
    v = jnp.where(forced, imp + FORCE_BONUS, imp)
    v = jnp.where(blk * SLC_BLOCK <= pos, v, NEG_INF)
    v = jnp.where(blk < nblk, v, -3e38)
    blk_f = blk.astype(F32)
    neg = jnp.full(imp.shape, NEG_INF, F32)
    for _ in range(min(TOP_N, nblk)):
        mx = jnp.max(v, axis=axis, keepdims=True)
        first = jnp.min(jnp.where(v == mx, blk_f, float(imp.shape[axis])), axis=axis, keepdims=True)
        pick = blk_f == first
        neg = jnp.where(pick, 0.0, neg)
        v = jnp.where(pick, -3e38, v)
    return neg


def _attn_prompt_body(q_ref, gn_ref, ck_ref, cvt_ref, ks_ref, vst_ref, kw_ref, vwt_ref, ovt_ref,
                      o_ref, kaug_ref, qaug_ref, acc_ref):
    i = pl.program_id(1)
    t = ks_ref.shape[1]
    nch = ck_ref.shape[1]
    nslc = t // SLC_BLOCK
    qb = Q_BLOCK
    ncol = N_Q_HEADS * qb
    q0 = i * qb

    @pl.when(i == 0)
    def _():
        kaug_ref[:, 0:LANES] = ks_ref[0]
        blk = lax.broadcasted_iota(jnp.int32, (t, LANES), 0) // SLC_BLOCK
        col = lax.broadcasted_iota(jnp.int32, (t, LANES), 1)
        kaug_ref[:, LANES:2 * LANES] = jnp.where(blk == col, 1.0, 0.0).astype(BF16)

    zeros64 = jnp.zeros((HEAD_DIM, qb), BF16)
    for c in range(N_Q_HEADS // 2):
        qt = q_ref[:, LANES * c:LANES * (c + 1)].T
        for hh in range(2):
            j = 2 * c + hh
            dst = j // GQA
            qaug_ref[HEAD_DIM * dst:HEAD_DIM * (dst + 1), qb * j:qb * (j + 1)] = \
                qt[HEAD_DIM * hh:HEAD_DIM * (hh + 1), :].astype(BF16)
            qaug_ref[HEAD_DIM * (1 - dst):HEAD_DIM * (2 - dst), qb * j:qb * (j + 1)] = zeros64
    qaug_ref[2 * LANES - HEAD_DIM:2 * LANES, :] = jnp.zeros((HEAD_DIM, ncol), BF16)
    qst = qaug_ref[0:LANES, :]

    pos_c = q0 + (lax.broadcasted_iota(jnp.int32, (nch, ncol), 1) & (qb - 1))
    cend = lax.broadcasted_iota(jnp.int32, (nch, ncol), 0) * CMP_STRIDE + (CMP_LEN - 1)
    sc = jnp.dot(ck_ref[0], qst, preferred_element_type=F32)
    pc = _softmax_cols(sc, cend <= pos_c).astype(BF16)
    oc = jnp.dot(cvt_ref[0], pc, preferred_element_type=F32)
    imp = jnp.dot(ovt_ref[...], pc, preferred_element_type=F32)
    blk = lax.broadcasted_iota(jnp.int32, (nslc, qb), 0)
    pos_q = q0 + lax.broadcasted_iota(jnp.int32, (nslc, qb), 1)
    for h in range(N_KV_HEADS):
        v = imp[0:nslc, qb * GQA * h:qb * GQA * h + qb]
        for g in range(1, GQA):
            v = v + imp[0:nslc, qb * (GQA * h + g):qb * (GQA * h + g + 1)]
        neg = _select_blocks(v, blk, pos_q, nslc).astype(BF16)
        for g in range(GQA):
            j = GQA * h + g
            qaug_ref[LANES:LANES + nslc, qb * j:qb * (j + 1)] = neg
    if nslc < HEAD_DIM:
        qaug_ref[LANES + nslc:LANES + HEAD_DIM, :] = jnp.zeros((HEAD_DIM - nslc, ncol), BF16)

    pos_t = q0 + (lax.broadcasted_iota(jnp.int32, (TK_SLC, ncol), 1) & (qb - 1))
    krow = lax.broadcasted_iota(jnp.int32, (TK_SLC, ncol), 0)
    acc_ref[...] = jnp.zeros_like(acc_ref)

    def tile(kt, carry, diag):
        m, l = carry
        k0 = pl.multiple_of(kt * TK_SLC, TK_SLC)
        s = jnp.dot(kaug_ref[pl.ds(k0, TK_SLC), :], qaug_ref[...], preferred_element_type=F32)
        if diag:
            s = jnp.where(krow + k0 <= pos_t, s, NEG_INF)
        mn = jnp.maximum(m, jnp.max(s, axis=0, keepdims=True))
        alpha = jnp.exp(m - mn)
        p = jnp.exp(s - mn)
        l = alpha * l + jnp.sum(p, axis=0, keepdims=True)
        acc_ref[...] = alpha * acc_ref[...] + jnp.dot(vst_ref[0, kt], p.astype(BF16), preferred_element_type=F32)
        return mn, l

    last = (q0 + qb - 1) // TK_SLC
    carry = (jnp.full((1, ncol), NEG_INF, F32), jnp.zeros((1, ncol), F32))
    carry = lax.fori_loop(0, last, functools.partial(tile, diag=False), carry)
    _, l = tile(last, carry, True)
    osel = acc_ref[...] * (1.0 / l)

    npiece = (WINDOW + qb) // TK_WIN
    start = jnp.maximum(q0 - WINDOW, 0)
    pb0 = start // TK_WIN
    pos_w = q0 + (lax.broadcasted_iota(jnp.int32, (TK_WIN, ncol), 1) & (qb - 1))
    wrow = lax.broadcasted_iota(jnp.int32, (TK_WIN, ncol), 0)
    sw, mw = [], []
    for w in range(npiece):
        k0 = pl.multiple_of(start + w * TK_WIN, TK_WIN)
        sw.append(jnp.dot(kw_ref[0, pl.ds(k0, TK_WIN), :], qst, preferred_element_type=F32))
        dlt = pos_w - (wrow + k0)
        mw.append((dlt >= 0) & (dlt < WINDOW))
    pw = _softmax_cols(jnp.concatenate(sw, axis=0), jnp.concatenate(mw, axis=0)).astype(BF16)
    ow = jnp.zeros((LANES, ncol), F32)
    for w in range(npiece):
        ow = ow + jnp.dot(vwt_ref[0, pb0 + w], pw[w * TK_WIN:(w + 1) * TK_WIN], preferred_element_type=F32)

    gt = gn_ref[...].T
    for c in range(N_Q_HEADS // 2):
        rows = []
        for hh in range(2):
            j = 2 * c + hh
            rs = slice(HEAD_DIM * (j // GQA), HEAD_DIM * (j // GQA + 1))
            cs = slice(qb * j, qb * (j + 1))
            rows.append(gt[3 * j:3 * j + 1, :] * oc[rs, cs] + gt[3 * j + 1:3 * j + 2, :] * osel[rs, cs]
                        + gt[3 * j + 2:3 * j + 3, :] * ow[rs, cs])
        o_ref[:, LANES * c:LANES * (c + 1)] = jnp.concatenate(rows, axis=0).T.astype(o_ref.dtype)


def _attn_prompt(q, gn, ck, cvt, ksb, vst, kwb, vwt, nseq, t):
    nb = t // Q_BLOCK
    nch = t // CMP_STRIDE
    nslc = t // SLC_BLOCK
    ovt = _overlap_t(nch, max(nslc, SUBLANES))
    row = lambda n, i: (n * nb + i, 0)
    seq3 = lambda n, i: (n, 0, 0)
    seq4 = lambda n, i: (n, 0, 0, 0)
    return pl.pallas_call(
        _attn_prompt_body,
        grid=(nseq, nb),
        in_specs=[pl.BlockSpec((Q_BLOCK, Q_W), row), pl.BlockSpec((Q_BLOCK, LANES), row),
                  pl.BlockSpec((1, nch, LANES), seq3), pl.BlockSpec((1, LANES, nch), seq3),
                  pl.BlockSpec((1, t, LANES), seq3), pl.BlockSpec((1, t // TK_SLC, LANES, TK_SLC), seq4),
                  pl.BlockSpec((1, t, LANES), seq3), pl.BlockSpec((1, t // TK_WIN, LANES, TK_WIN), seq4),
                  _full(ovt.shape)],
        out_specs=pl.BlockSpec((Q_BLOCK, Q_W), row),
        out_shape=jax.ShapeDtypeStruct((nseq * t, Q_W), BF16),
        scratch_shapes=[pltpu.VMEM((t, 2 * LANES), BF16),
                        pltpu.VMEM((2 * LANES, N_Q_HEADS * Q_BLOCK), BF16),
                        pltpu.VMEM((LANES, N_Q_HEADS * Q_BLOCK), F32)],
        compiler_params=_cparams(("arbitrary", "arbitrary"), 56),
        name="attn_prompt",
    )(q, gn, ck, cvt, ksb.reshape(nseq, t, LANES), vst, kwb.reshape(nseq, t, LANES), vwt, ovt)


def _post_body(x_ref, abr_ref, on_ref, ga_ref, gb_ref, wno_ref, wo_ref, g2_ref, x1_ref, h2_ref):
    bbr = jnp.dot(on_ref[...].astype(BF16), wno_ref[...], preferred_element_type=F32)
    merged = ga_ref[...] * abr_ref[...] + gb_ref[...] * bbr
    x1 = x_ref[...] + jnp.dot(merged.astype(BF16), wo_ref[...], preferred_element_type=F32)
    x1_ref[...] = x1
    inv = lax.rsqrt(jnp.mean(x1 * x1, axis=-1, keepdims=True) + RMS_EPS)
    h2_ref[...] = (x1 * inv * g2_ref[...]).astype(BF16)


def _post(x2d, abr, abr_lay, onsa, ga, gb, wno, wo, g2, lay, out_lay):
    x_shape, x_spec = lay["a"](D_MODEL)
    abr_shape, abr_spec = lay[abr_lay](D_MODEL)
    on_shape, on_spec = lay["a"](Q_W)
    o_shape, o_spec = lay[out_lay](D_MODEL)
    return pl.pallas_call(
        _post_body,
        grid=lay["grid"],
        in_specs=[x_spec, abr_spec, on_spec, x_spec, x_spec, _full(wno.shape), _full(wo.shape), _full(g2.shape)],
        out_specs=[o_spec, o_spec],
        out_shape=[jax.ShapeDtypeStruct(o_shape, F32), jax.ShapeDtypeStruct(o_shape, BF16)],
        compiler_params=_cparams(("arbitrary",) * len(lay["grid"]), 48),
        name="post",
    )(x2d.reshape(x_shape), abr.reshape(abr_shape), onsa.reshape(on_shape), ga, gb, wno, wo, g2)


def _route(logits):
    lane = lax.broadcasted_iota(jnp.int32, logits.shape, 1).astype(F32)
    big = float(LANES)
    glog = jnp.where(lane < N_EXPERT_GROUPS, logits, -jnp.inf)
    gmax = jnp.max(glog, axis=1, keepdims=True)
    gsel = jnp.min(jnp.where(glog == gmax, lane, big), axis=1, keepdims=True)
    gw = 1.0 / jnp.sum(jnp.exp(glog - gmax), axis=1, keepdims=True)
    lo = N_EXPERT_GROUPS + EXPERTS_PER_GROUP * gsel
    el = jnp.where((lane >= lo) & (lane < lo + EXPERTS_PER_GROUP), logits, -jnp.inf)
    v1 = jnp.max(el, axis=1, keepdims=True)
    i1 = jnp.min(jnp.where(el == v1, lane, big), axis=1, keepdims=True)
    el2 = jnp.where(lane == i1, -jnp.inf, el)
    v2 = jnp.max(el2, axis=1, keepdims=True)
    i2 = jnp.min(jnp.where(el2 == v2, lane, big), axis=1, keepdims=True)
    e2 = jnp.exp(v2 - v1)
    w1 = gw / (1.0 + e2)
    return jnp.where(lane == i1, w1, 0.0) + jnp.where(lane == i2, w1 * e2, 0.0)


def _moe_body(x1_ref, h2_ref, p_ref, wr_ref, br_ref, wg_ref, wu_ref, wd_ref, wpg_ref, wp_ref, gf_ref,
              y_ref, acc_ref, comb_ref, *, tsplit):
    g = pl.program_id(1)
    h2 = h2_ref[...]

    @pl.when(g == 0)
    def _():
        logits = jnp.dot(h2, wr_ref[...], preferred_element_type=F32) + br_ref[...]
        comb_ref[...] = _route(logits)
        acc_ref[...] = jnp.zeros_like(acc_ref)

    comb = comb_ref[...]
    lane = lax.broadcasted_iota(jnp.int32, comb.shape, 1)
    acc = acc_ref[...]
    for k in range(EXPERTS_PER_GROUP):
        e_lane = N_EXPERT_GROUPS + EXPERTS_PER_GROUP * g + k
        ce = jnp.sum(jnp.where(lane == e_lane, comb, 0.0), axis=1, keepdims=True)
        a = jnp.dot(h2, wg_ref[k], preferred_element_type=F32)
        b = jnp.dot(h2, wu_ref[k], preferred_element_type=F32)
        act = (jax.nn.silu(a) * b * ce).astype(BF16)
        acc = acc + jnp.dot(act, wd_ref[k], preferred_element_type=F32)
    acc_ref[...] = acc

    @pl.when(g == N_EXPERT_GROUPS - 1)
    def _():
        x2 = x1_ref[...] + acc_ref[...]
        rows = x2.shape[0] // tsplit
        if tsplit == 1:
            p = p_ref[...]
        else:
            p = jnp.concatenate([p_ref[:, PLE_DIM * t:PLE_DIM * (t + 1)] for t in range(tsplit)], axis=0)
        gate = jax.nn.sigmoid(jnp.dot(x2.astype(BF16), wpg_ref[...], preferred_element_type=F32))
        x3 = x2 + gate * jnp.dot(p.astype(BF16), wp_ref[...], preferred_element_type=F32)
        inv = lax.rsqrt(jnp.mean(x3 * x3, axis=-1, keepdims=True) + RMS_EPS)
        y = x3 * inv * gf_ref[...]
        if tsplit == 1:
            y_ref[...] = y
        else:
            for t in range(tsplit):
                y_ref[:, D_MODEL * t:D_MODEL * (t + 1)] = y[rows * t:rows * (t + 1)]


def _moe(x1, h2, p, mp, tm, tsplit):
    rows = x1.shape[0]
    nrb = rows // tm
    rb = lambda r, g: (r, 0)
    grp = lambda r, g: (g, 0, 0)
    if tsplit == 1:
        p_spec = pl.BlockSpec((tm, PLE_DIM), rb)
        y_spec = pl.BlockSpec((tm, D_MODEL), rb)
        y_shape = (rows, D_MODEL)
    else:
        assert nrb == 1
        p_spec = _full(p.shape)
        y_shape = (rows // tsplit, tsplit * D_MODEL)
        y_spec = _full(y_shape)
    return pl.pallas_call(
        functools.partial(_moe_body, tsplit=tsplit),
        grid=(nrb, N_EXPERT_GROUPS),
        in_specs=[pl.BlockSpec((tm, D_MODEL), rb), pl.BlockSpec((tm, D_MODEL), rb), p_spec,
                  _full(mp["wr"].shape), _full(mp["br"].shape),
                  pl.BlockSpec((EXPERTS_PER_GROUP, D_MODEL, D_FF_EXPERT), grp),
                  pl.BlockSpec((EXPERTS_PER_GROUP, D_MODEL, D_FF_EXPERT), grp),
                  pl.BlockSpec((EXPERTS_PER_GROUP, D_FF_EXPERT, D_MODEL), grp),
                  _full(mp["wpg"].shape), _full(mp["wp"].shape), _full(mp["gf"].shape)],
        out_specs=y_spec,
        out_shape=jax.ShapeDtypeStruct(y_shape, F32),
        scratch_shapes=[pltpu.VMEM((tm, D_MODEL), F32), pltpu.VMEM((tm, LANES), F32)],
        compiler_params=_cparams(("arbitrary", "arbitrary"), 56),
        name="moe_ple",
    )(x1, h2, p, mp["wr"], mp["br"], mp["wg"], mp["wu"], mp["wd"], mp["wpg"], mp["wp"], mp["gf"])


def _s5_sample_body(u_ref, h0re_ref, h0im_ref, wre_ref, wim_ref, ar_ref, ai_ref, cre_ref, cim_ref, d_ref,
                    wglu_ref, bglu_ref, wso_ref, abr_ref, hre_out_ref, him_out_ref,
                    bure_ref, buim_ref, hre_ref, him_ref, *, nseq, nstep):
    u = u_ref[...]
    ub = u.astype(BF16)
    for j in range(4):
        lhs = ub[:, LANES * j:LANES * (j + 1)]
        bure_ref[:, 512 * j:512 * (j + 1)] = jnp.dot(lhs, wre_ref[j], preferred_element_type=F32)
        buim_ref[:, 512 * j:512 * (j + 1)] = jnp.dot(lhs, wim_ref[j], preferred_element_type=F32)
    for lc in range(4):
        sl = slice(512 * lc, 512 * (lc + 1))
        ar = jnp.broadcast_to(ar_ref[:, sl], (SUBLANES, 512))
        ai = jnp.broadcast_to(ai_ref[:, sl], (SUBLANES, 512))

        def body(rc, carry, sl=sl, ar=ar, ai=ai):
            r0 = pl.multiple_of(rc * SUBLANES, SUBLANES)
            hr = h0re_ref[pl.ds(r0, SUBLANES), sl]
            hi = h0im_ref[pl.ds(r0, SUBLANES), sl]
            for t in range(nstep):
                rr = pl.multiple_of(t * nseq + rc * SUBLANES, SUBLANES)
                hr, hi = (ar * hr - ai * hi + bure_ref[pl.ds(rr, SUBLANES), sl],
                          ar * hi + ai * hr + buim_ref[pl.ds(rr, SUBLANES), sl])
                hre_ref[pl.ds(rr, SUBLANES), sl] = hr
                him_ref[pl.ds(rr, SUBLANES), sl] = hi
            hre_out_ref[pl.ds(r0, SUBLANES), sl] = hr
            him_out_ref[pl.ds(r0, SUBLANES), sl] = hi
            return carry

        lax.fori_loop(0, nseq // SUBLANES, body, 0)
    parts = []
    for j in range(4):
        sl = slice(512 * j, 512 * (j + 1))
        parts.append(jnp.dot(hre_ref[:, sl].astype(BF16), cre_ref[j], preferred_element_type=F32)
                     - jnp.dot(him_ref[:, sl].astype(BF16), cim_ref[j], preferred_element_type=F32))
    y = jnp.concatenate(parts, axis=1) + d_ref[...] * u
    zg = jax.nn.gelu(y)
    gate = jnp.dot(zg.astype(BF16), wglu_ref[...], preferred_element_type=F32) + bglu_ref[...]
    glu = (zg * jax.nn.sigmoid(gate)).astype(BF16)
    abr_ref[...] = jnp.dot(glu, wso_ref[...], preferred_element_type=F32)


def _s5_sample(u_ts, h0re, h0im, sp, wglu, bglu, wso, nseq, nstep):
    rows = nseq * nstep
    ops = [u_ts, h0re, h0im, sp["wre"], sp["wim"], sp["ar"], sp["ai"], sp["cre"], sp["cim"], sp["d"], wglu, bglu, wso]
    return pl.pallas_call(
        functools.partial(_s5_sample_body, nseq=nseq, nstep=nstep),
        grid=(1,),
        in_specs=[_full(o.shape) for o in ops],
        out_specs=[_full((rows, D_MODEL)), _full((nseq, N_STATE)), _full((nseq, N_STATE))],
        out_shape=[jax.ShapeDtypeStruct((rows, D_MODEL), F32),
                   jax.ShapeDtypeStruct((nseq, N_STATE), F32), jax.ShapeDtypeStruct((nseq, N_STATE), F32)],
        scratch_shapes=[pltpu.VMEM((rows, N_STATE), F32) for _ in range(4)],
        compiler_params=_cparams(("arbitrary",), 56),
        name="s5_sample",
    )(*ops)


def _softmax_rows(s, valid):
    sm = jnp.where(valid, s, NEG_INF)
    mx = jnp.max(sm, axis=1, keepdims=True)
    e = jnp.where(valid, jnp.exp(sm - mx), 0.0)
    l = jnp.sum(e, axis=1, keepdims=True)
    return e * (1.0 / jnp.maximum(l, 1e-30))


def _dot_t(a, b):
    return lax.dot_general(a, b, (((1,), (1,)), ((), ())), preferred_element_type=F32)


def _attn_sample_body(pt_ref, q_ref, gn_ref, nks_ref, nkw_ref, win_ref, pe_ref, w1_ref, w2_ref, ov_ref, *rest,
                      npage, past_len):
    cmp_pages = rest[:npage]
    slc_pages = rest[npage:2 * npage]
    o_ref, nwin_ref, kaug_ref, vall_ref, kw_ref, vw_ref = rest[2 * npage:]
    n = pl.program_id(0)
    tq = q_ref.shape[0]
    nrow = N_Q_HEADS * tq
    nkeys = kaug_ref.shape[0]
    nwin = win_ref.shape[1]
    nslc = -(-(past_len + tq) // SLC_BLOCK)

    @pl.when(n == 0)
    def _():
        blk = lax.broadcasted_iota(jnp.int32, (nkeys, LANES), 0) // SLC_BLOCK
        col = lax.broadcasted_iota(jnp.int32, (nkeys, LANES), 1)
        kaug_ref[:, LANES:2 * LANES] = jnp.where(blk == col, 1.0, 0.0).astype(BF16)

    for p in range(npage):
        page = slc_pages[p][0]
        kaug_ref[PAGE_SIZE * p:PAGE_SIZE * (p + 1), 0:LANES] = page[:, 0:LANES].astype(BF16)
        vall_ref[PAGE_SIZE * p:PAGE_SIZE * (p + 1), :] = page[:, LANES:2 * LANES].astype(BF16)
    tail = nkeys - past_len
    nks = jnp.concatenate([nks_ref[...], jnp.zeros((tail - tq, KV_W), F32)], axis=0)
    kaug_ref[past_len:nkeys, 0:LANES] = nks[:, 0:LANES].astype(BF16)
    vall_ref[past_len:nkeys, :] = nks[:, LANES:2 * LANES].astype(BF16)
    win = win_ref[0]
    nkw = nkw_ref[...]
    wtail = kw_ref.shape[0] - nwin
    nkw_pad = jnp.concatenate([nkw, jnp.zeros((wtail - tq, KV_W), F32)], axis=0)
    kw_ref[0:nwin, :] = win[:, 0:LANES].astype(BF16)
    vw_ref[0:nwin, :] = win[:, LANES:2 * LANES].astype(BF16)
    kw_ref[nwin:nwin + wtail, :] = nkw_pad[:, 0:LANES].astype(BF16)
    vw_ref[nwin:nwin + wtail, :] = nkw_pad[:, LANES:2 * LANES].astype(BF16)
    nwin_ref[0, 0:nwin - tq, :] = win[tq:nwin, :]
    nwin_ref[0, nwin - tq:nwin, :] = nkw

    q = q_ref[...]
    lane8 = lax.broadcasted_iota(jnp.int32, (tq, LANES), 1)
    qrows = []
    for j in range(N_Q_HEADS):
        chunk = q[:, LANES * (j // 2):LANES * (j // 2 + 1)]
        dst = j // GQA
        if (j % 2) != dst:
            chunk = pltpu.roll(chunk, HEAD_DIM, axis=1)
        keep = (lane8 < HEAD_DIM) if dst == 0 else (lane8 >= HEAD_DIM)
        qrows.append(jnp.where(keep, chunk, 0.0))
    qs = jnp.concatenate(qrows, axis=0).astype(BF16)
    pos = past_len + (lax.broadcasted_iota(jnp.int32, (nrow, LANES), 0) & (tq - 1))
    lane = lax.broadcasted_iota(jnp.int32, (nrow, LANES), 1)

    x = jnp.concatenate([cmp_pages[p][0] for p in range(npage)], axis=0)
    cmp = _compress_rows(x, pe_ref, w1_ref, w2_ref)
    ck = cmp[:, 0:LANES].astype(BF16)
    cv = cmp[:, LANES:2 * LANES].astype(BF16)
    sc = _dot_t(qs, ck)
    pc = _softmax_rows(sc, lane * CMP_STRIDE + (CMP_LEN - 1) <= pos).astype(BF16)
    oc = jnp.dot(pc, cv, preferred_element_type=F32)
    imp = jnp.dot(pc, ov_ref[...], preferred_element_type=F32)
    negs = []
    for h in range(N_KV_HEADS):
        v = imp[tq * GQA * h:tq * GQA * h + tq]
        for g in range(1, GQA):
            v = v + imp[tq * (GQA * h + g):tq * (GQA * h + g + 1)]
        neg = _select_blocks(v, lane[0:tq], pos[0:tq], nslc, axis=1)
        negs.extend([neg] * GQA)
    qaug = jnp.concatenate([qs, jnp.concatenate(negs, axis=0).astype(BF16)], axis=1)

    ss = _dot_t(qaug, kaug_ref[...])
    kpos = lax.broadcasted_iota(jnp.int32, (nrow, nkeys), 1)
    pos_k = past_len + (lax.broadcasted_iota(jnp.int32, (nrow, nkeys), 0) & (tq - 1))
    ps = _softmax_rows(ss, kpos <= pos_k).astype(BF16)
    osel = jnp.dot(ps, vall_ref[...], preferred_element_type=F32)

    nw = kw_ref.shape[0]
    sw = _dot_t(qs, kw_ref[...])
    widx = lax.broadcasted_iota(jnp.int32, (nrow, nw), 1)
    pos_w = past_len + (lax.broadcasted_iota(jnp.int32, (nrow, nw), 0) & (tq - 1))
    dlt = pos_w - (past_len - nwin + widx)
    pw = _softmax_rows(sw, (dlt >= 0) & (dlt < WINDOW) & (widx < nwin + tq)).astype(BF16)
    ow = jnp.dot(pw, vw_ref[...], preferred_element_type=F32)

    gn = gn_ref[...]
    for c in range(N_Q_HEADS // 2):
        halves = []
        for hh in range(2):
            j = 2 * c + hh
            rs = slice(tq * j, tq * (j + 1))
            oj = (gn[:, 3 * j:3 * j + 1] * oc[rs] + gn[:, 3 * j + 1:3 * j + 2] * osel[rs]
                  + gn[:, 3 * j + 2:3 * j + 3] * ow[rs])
            if (j // GQA) != hh:
                oj = pltpu.roll(oj, HEAD_DIM, axis=1)
            halves.append(oj)
        o_ref[:, LANES * c:LANES * (c + 1)] = jnp.where(lane8 < HEAD_DIM, halves[0], halves[1])


def _attn_sample(q, gn, nks, nkw, cache_cmp, cache_slc, cache_win, page_table, cp, nseq, tq, past_len):
    assert tq <= CMP_STRIDE and past_len % PAGE_SIZE == 0
    npage = past_len // PAGE_SIZE
    n_pool = cache_cmp.shape[0]
    nwin = cache_win.shape[1]
    nkeys = -(-(past_len + tq) // LANES) * LANES
    nw = -(-(nwin + tq) // LANES) * LANES
    chunks = past_len // CMP_STRIDE
    ov = _overlap_t(chunks, LANES).T
    cmp_v = cache_cmp.reshape(n_pool, PAGE_SIZE // CMP_STRIDE, CMP_STRIDE * KV_W)
    slc_v = cache_slc.reshape(n_pool, PAGE_SIZE, KV_W)
    win_v = cache_win.reshape(nseq, nwin, KV_W)
    row = lambda n, pt: (n, 0)
    seq3 = lambda n, pt: (n, 0, 0)
    page = lambda p: (lambda n, pt: (pt[n, p], 0, 0))
    in_specs = [pl.BlockSpec((tq, Q_W), row), pl.BlockSpec((tq, LANES), row),
                pl.BlockSpec((tq, KV_W), row), pl.BlockSpec((tq, KV_W), row),
                pl.BlockSpec((1, nwin, KV_W), seq3),
                pl.BlockSpec(cp["pe"].shape, lambda n, pt: (0, 0)), pl.BlockSpec(cp["w1"].shape, lambda n, pt: (0, 0, 0)),
                pl.BlockSpec(cp["w2"].shape, lambda n, pt: (0, 0)), pl.BlockSpec(ov.shape, lambda n, pt: (0, 0))]
    in_specs += [pl.BlockSpec((1, PAGE_SIZE // CMP_STRIDE, CMP_STRIDE * KV_W), page(p)) for p in range(npage)]
    in_specs += [pl.BlockSpec((1, PAGE_SIZE, KV_W), page(p)) for p in range(npage)]
    grid_spec = pltpu.PrefetchScalarGridSpec(
        num_scalar_prefetch=1,
        grid=(nseq,),
        in_specs=in_specs,
        out_specs=[pl.BlockSpec((tq, Q_W), row), pl.BlockSpec((1, nwin, KV_W), seq3)],
        scratch_shapes=[pltpu.VMEM((nkeys, 2 * LANES), BF16), pltpu.VMEM((nkeys, LANES), BF16),
                        pltpu.VMEM((nw, LANES), BF16), pltpu.VMEM((nw, LANES), BF16)],
    )
    return pl.pallas_call(
        functools.partial(_attn_sample_body, npage=npage, past_len=past_len),
        grid_spec=grid_spec,
        out_shape=[jax.ShapeDtypeStruct((nseq * tq, Q_W), F32), jax.ShapeDtypeStruct((nseq, nwin, KV_W), F32)],
        compiler_params=_cparams(("arbitrary",), 56),
        name="attn_sample",
    )(page_table, q, gn, nks, nkw, win_v, cp["pe"], cp["w1"], cp["w2"], ov,
      *([cmp_v] * npage), *([slc_v] * npage))


def _moe_params(w_rg, b_rg, w_re, b_re, w_gate, w_up, w_down, w_ple, w_ple_gate, gf):
    pad = LANES - N_EXPERT_GROUPS - N_EXPERTS
    return {"wr": jnp.pad(jnp.concatenate([w_rg, w_re], axis=1), ((0, 0), (0, pad))).astype(BF16),
            "br": jnp.pad(jnp.concatenate([b_rg, b_re]), (0, pad)).astype(F32).reshape(1, LANES),
            "wg": w_gate.astype(BF16), "wu": w_up.astype(BF16), "wd": w_down.astype(BF16),
            "wpg": w_ple_gate.astype(BF16), "wp": w_ple.astype(BF16), "gf": gf.astype(F32).reshape(1, D_MODEL)}


TM_PROMPT = 512
TC_S5 = 128


def kernel(x_prompt, x_sample, p_prompt, p_sample, cache_cmp_kv, cache_slc_kv, cache_win_kv, state_ssm, page_table, norm1_g, w_in, ssm_lam_re, ssm_lam_im, ssm_log_dt, ssm_b_re, ssm_b_im, ssm_c_re, ssm_c_im, ssm_d, w_glu, b_glu, cmp_pe, cmp_w1, cmp_w2, w_ssm_out, w_nsa_out, w_o, norm2_g, w_route_group, b_route_group, w_route_expert, b_route_expert, w_exp_gate, w_exp_up, w_exp_down, w_ple, w_ple_gate, final_norm_g):
    assert w_in.shape[0] == 1, "one layer"
    l = 0
    nb, t = x_prompt.shape[:2]
    ns, ts = x_sample.shape[:2]
    past_len = page_table.shape[1] * PAGE_SIZE
    kvt = (2, N_KV_HEADS, HEAD_DIM)

    w_in0 = w_in[l]
    n_main = SSM_WIDTH + Q_W + 3 * KV_W
    wa = w_in0[:, :n_main].astype(BF16)
    wgn = jnp.pad(w_in0[:, n_main:n_main + NSA_GATE_W], ((0, 0), (0, LANES - NSA_GATE_W))).astype(BF16)
    wgab = w_in0[:, n_main + NSA_GATE_W:].astype(BF16)
    g1 = norm1_g[l].astype(F32).reshape(1, D_MODEL)
    g2 = norm2_g[l].astype(F32).reshape(1, D_MODEL)
    sp = _s5_params(ssm_lam_re[l], ssm_lam_im[l], ssm_log_dt[l], ssm_b_re[l], ssm_b_im[l], ssm_c_re[l], ssm_c_im[l],
                    ssm_d[l])
    cp = _cmp_params(cmp_pe[l], cmp_w1[l], cmp_w2[l])
    mp = _moe_params(w_route_group[l], b_route_group[l], w_route_expert[l], b_route_expert[l], w_exp_gate[l],
                     w_exp_up[l], w_exp_down[l], w_ple[l], w_ple_gate[l], final_norm_g)
    wglu = w_glu[l].astype(BF16)
    bglu = b_glu[l].astype(F32).reshape(1, SSM_WIDTH)
    wso = w_ssm_out[l].astype(BF16)
    wno = w_nsa_out[l].astype(BF16)
    wo = w_o[l].astype(BF16)

    lay = _prompt_layout(nb, t, TM_PROMPT)
    xp = x_prompt.reshape(nb * t, D_MODEL)
    r = _inproj(xp, lay, g1, wa, wgn, wgab, True)
    abr, hlast = _s5_prompt(r["u"].reshape(t * nb, SSM_WIDTH), sp, wglu, bglu, wso, t, TC_S5)
    ck, cvt = _compress_prompt(r["kvc"], cp, nb, t)
    onsa = _attn_prompt(r["q"], r["gn"], ck, cvt, r["ksb"], r["vst"], r["kwb"], r["vwt"], nb, t)
    x1, h2 = _post(xp, abr, "a", onsa, r["ga"], r["gb"], wno, wo, g2, lay, "a")
    y_prompt = _moe(x1, h2, p_prompt[l].reshape(nb * t, PLE_DIM), mp, TM_PROMPT, 1).reshape(nb, t, D_MODEL)
    keep = min(WINDOW, t)
    new_cmp_p = r["kvc"].reshape((1, nb, t) + kvt)
    new_slc_p = r["kvs"].reshape((1, nb, t) + kvt)
    new_win_p = r["kvw"].reshape(nb, t, KV_W)[:, t - keep:].reshape((1, nb, keep) + kvt)
    new_ssm_p = jnp.stack([hlast[0:nb], hlast[nb:2 * nb]], axis=-1).reshape(1, nb, N_SSM_GROUPS, SSM_STATE, 2)

    lays = _sample_layout(ns, ts)
    xs = x_sample.reshape(ns * ts, D_MODEL)
    rs = _inproj(xs, lays, g1, wa, wgn, wgab, False)
    h0 = state_ssm[l].astype(F32).reshape(ns, N_STATE, 2)
    abr_s, hre, him = _s5_sample(rs["u"], h0[..., 0], h0[..., 1], sp, wglu, bglu, wso, ns, ts)
    onsa_s, new_win = _attn_sample(rs["q"].reshape(ns * ts, Q_W), rs["gn"].reshape(ns * ts, LANES),
                                   rs["kvs"].reshape(ns * ts, KV_W), rs["kvw"].reshape(ns * ts, KV_W),
                                   cache_cmp_kv[l], cache_slc_kv[l], cache_win_kv[l], page_table, cp, ns, ts, past_len)
    x1s, h2s = _post(xs, abr_s, "b", onsa_s, rs["ga"], rs["gb"], wno, wo, g2, lays, "b")
    y_sample = _moe(x1s, h2s, p_sample[l].reshape(ns, ts * PLE_DIM), mp, ns * ts, ts).reshape(ns, ts, D_MODEL)
    new_cmp_s = rs["kvc"].reshape((1, ns, ts) + kvt)
    new_slc_s = rs["kvs"].reshape((1, ns, ts) + kvt)
    new_win_s = new_win.reshape((1, ns, new_win.shape[1]) + kvt)
    new_ssm_s = jnp.stack([hre, him], axis=-1).reshape(1, ns, N_SSM_GROUPS, SSM_STATE, 2)
    return (y_prompt, y_sample, new_cmp_p, new_slc_p, new_win_p, new_ssm_p,
            new_cmp_s, new_slc_s, new_win_s, new_ssm_s)
```

```python
import functools
import math

import jax
import jax.numpy as jnp
import numpy as np
from jax import lax
from jax.experimental import pallas as pl
from jax.experimental.pallas import tpu as pltpu

F32 = jnp.float32
BF16 = jnp.bfloat16

D_MODEL = 1024
SSM_WIDTH = 512
SSM_GROUP = 16
N_SSM_GROUPS = 32
SSM_STATE = 64
HEAD_DIM = 64
N_Q_HEADS = 8
N_KV_HEADS = 2
GQA = 4
CMP_LEN = 32
CMP_STRIDE = 16
SLC_BLOCK = 64
TOP_N = 8
WINDOW = 512
Q_BLOCK = 128
NEG_INF = -1e30
FORCE_BONUS = 1e4
Q_W = 512
KV_W = 256
NSA_GATE_W = 24
N_EXPERT_GROUPS = 4
EXPERTS_PER_GROUP = 4
N_EXPERTS = 16
D_FF_EXPERT = 256
PLE_DIM = 256
RMS_EPS = 1e-6
PAGE_SIZE = 128

LANES = 128
SUBLANES = 8
N_STATE = N_SSM_GROUPS * SSM_STATE
MIB = 2 ** 20


def _cparams(sem, vmem_mib):
    return pltpu.CompilerParams(dimension_semantics=sem, vmem_limit_bytes=vmem_mib * MIB)


def _full(shape):
    nd = len(shape)
    return pl.BlockSpec(shape, lambda *_: (0,) * nd)


def _prompt_layout(nseq, t, tm):
    nb = t // tm
    return {
        "grid": (nseq, nb), "tm": tm,
        "a": lambda w: ((nseq * t, w), pl.BlockSpec((tm, w), lambda s, b: (s * nb + b, 0))),
        "b": lambda w: ((t, nseq * w), pl.BlockSpec((tm, w), lambda s, b: (b, s))),
    }


def _sample_layout(nseq, t):
    return {
        "grid": (1, t), "tm": nseq,
        "a": lambda w: ((nseq, t * w), pl.BlockSpec((nseq, w), lambda s, b: (0, b))),
        "b": lambda w: ((t * nseq, w), pl.BlockSpec((nseq, w), lambda s, b: (b, 0))),
    }


TK_SLC = 512
TK_WIN = 128


C_U, C_Q, C_KVC, C_KVS, C_KVW = 0, 512, 1024, 1280, 1536
N_MAIN = 1792


def _dot_t(a, b):
    return lax.dot_general(a, b, (((1,), (1,)), ((), ())), preferred_element_type=F32)


def _inproj_prompt_body(x_ref, g_ref, wa_ref, wat_ref, wgnt_ref, wgab_ref,
                        u_ref, kvc_ref, ksb_ref, kwb_ref, ga_ref, gb_ref,
                        qt_ref, kvct_ref, kvst_ref, kvwt_ref, gnt_ref, vst_ref, vwt_ref):
    x = x_ref[...]
    inv = lax.rsqrt(jnp.mean(x * x, axis=-1, keepdims=True) + RMS_EPS)
    h = (x * inv * g_ref[...]).astype(BF16)
    tm = h.shape[0]

    def mm(w):
        return jnp.dot(h, w, preferred_element_type=F32)

    u_ref[...] = mm(wa_ref[:, C_U:C_U + SSM_WIDTH])
    kvc_ref[...] = mm(wa_ref[:, C_KVC:C_KVC + KV_W])
    ksb_ref[...] = mm(wa_ref[:, C_KVS:C_KVS + LANES]).astype(BF16)
    kwb_ref[...] = mm(wa_ref[:, C_KVW:C_KVW + LANES]).astype(BF16)
    ga_ref[...] = jax.nn.sigmoid(mm(wgab_ref[:, 0:D_MODEL]))
    gb_ref[...] = jax.nn.sigmoid(mm(wgab_ref[:, D_MODEL:2 * D_MODEL]))
    qt_ref[0] = (_dot_t(wat_ref[C_Q:C_Q + Q_W, :], h) * (HEAD_DIM ** -0.5)).astype(BF16)
    kvct_ref[0] = _dot_t(wat_ref[C_KVC:C_KVC + KV_W, :], h)
    kvst = _dot_t(wat_ref[C_KVS:C_KVS + KV_W, :], h)
    kvst_ref[0] = kvst
    kvwt = _dot_t(wat_ref[C_KVW:C_KVW + KV_W, :], h)
    kvwt_ref[0] = kvwt
    gnt_ref[0] = jax.nn.sigmoid(_dot_t(wgnt_ref[...], h))
    for c in range(tm // TK_SLC):
        vst_ref[0, c] = kvst[LANES:2 * LANES, c * TK_SLC:(c + 1) * TK_SLC].astype(BF16)
    for c in range(tm // TK_WIN):
        vwt_ref[0, c] = kvwt[LANES:2 * LANES, c * TK_WIN:(c + 1) * TK_WIN].astype(BF16)


def _inproj_prompt(x2d, lay, g, w):
    tm = lay["tm"]
    nseq, nb = lay["grid"]
    t = nb * tm
    out_shapes, out_specs, names = [], [], []

    def add(name, shape_spec, dt):
        names.append(name)
        out_shapes.append(jax.ShapeDtypeStruct(shape_spec[0], dt))
        out_specs.append(shape_spec[1])

    def tr(rows):
        return (nseq, rows, t), pl.BlockSpec((1, rows, tm), lambda s, b: (s, 0, b))

    add("u", lay["b"](SSM_WIDTH), F32)
    add("kvc", lay["a"](KV_W), F32)
    add("ksb", lay["a"](LANES), BF16)
    add("kwb", lay["a"](LANES), BF16)
    add("ga", lay["a"](D_MODEL), F32)
    add("gb", lay["a"](D_MODEL), F32)
    add("qt", tr(Q_W), BF16)
    add("kvct", tr(KV_W), F32)
    add("kvst", tr(KV_W), F32)
    add("kvwt", tr(KV_W), F32)
    add("gnt", tr(w["wgnt"].shape[0]), F32)
    for name, tk in (("vst", TK_SLC), ("vwt", TK_WIN)):
        add(name, ((nseq, t // tk, LANES, tk), pl.BlockSpec((1, tm // tk, LANES, tk), lambda s, b: (s, b, 0, 0))), BF16)
    x_shape, x_spec = lay["a"](D_MODEL)
    ops = [g, w["wa"], w["wat"], w["wgnt"], w["wgab"]]
    outs = pl.pallas_call(
        _inproj_prompt_body,
        grid=lay["grid"],
        in_specs=[x_spec] + [_full(o.shape) for o in ops],
        out_specs=out_specs,
        out_shape=out_shapes,
        compiler_params=_cparams(("arbitrary",) * 2, 56),
        name="inproj_prompt",
    )(x2d.reshape(x_shape), *ops)
    return dict(zip(names, outs))


def _inproj_sample_body(x_ref, g_ref, wa_ref, wat_ref, wgn_ref, wgab_ref,
                        u_ref, q_ref, kvs_ref, kvw_ref, gn_ref, ga_ref, gb_ref, kvct_ref, kvst_ref, kvwt_ref):
    x = x_ref[...]
    inv = lax.rsqrt(jnp.mean(x * x, axis=-1, keepdims=True) + RMS_EPS)
    h = (x * inv * g_ref[...]).astype(BF16)

    def mm(w):
        return jnp.dot(h, w, preferred_element_type=F32)

    u_ref[...] = mm(wa_ref[:, C_U:C_U + SSM_WIDTH])
    q_ref[...] = mm(wa_ref[:, C_Q:C_Q + Q_W]) * (HEAD_DIM ** -0.5)
    kvs_ref[...] = mm(wa_ref[:, C_KVS:C_KVS + KV_W])
    kvw_ref[...] = mm(wa_ref[:, C_KVW:C_KVW + KV_W])
    gn_ref[...] = jax.nn.sigmoid(mm(wgn_ref[...]))
    ga_ref[...] = jax.nn.sigmoid(mm(wgab_ref[:, 0:D_MODEL]))
    gb_ref[...] = jax.nn.sigmoid(mm(wgab_ref[:, D_MODEL:2 * D_MODEL]))
    kvct_ref[0] = _dot_t(wat_ref[C_KVC:C_KVC + KV_W, :], h)
    kvst_ref[0] = _dot_t(wat_ref[C_KVS:C_KVS + KV_W, :], h)
    kvwt_ref[0] = _dot_t(wat_ref[C_KVW:C_KVW + KV_W, :], h)


def _inproj_sample(x2d, lay, g, w):
    nseq = lay["tm"]
    ts = lay["grid"][1]
    names = ["u", "q", "kvs", "kvw", "gn", "ga", "gb"]
    widths = [SSM_WIDTH, Q_W, KV_W, KV_W, LANES, D_MODEL, D_MODEL]
    out_shapes, out_specs = [], []
    for n, wd in zip(names, widths):
        shp, spec = lay["b" if n == "u" else "a"](wd)
        out_shapes.append(jax.ShapeDtypeStruct(shp, F32))
        out_specs.append(spec)
    for n in ("kvct", "kvst", "kvwt"):
        names.append(n)
        out_shapes.append(jax.ShapeDtypeStruct((ts, KV_W, nseq), F32))
        out_specs.append(pl.BlockSpec((1, KV_W, nseq), lambda s, b: (b, 0, 0)))
    x_shape, x_spec = lay["a"](D_MODEL)
    ops = [g, w["wa"], w["wat"], w["wgn"], w["wgab"]]
    outs = pl.pallas_call(
        _inproj_sample_body,
        grid=lay["grid"],
        in_specs=[x_spec] + [_full(o.shape) for o in ops],
        out_specs=out_specs,
        out_shape=out_shapes,
        compiler_params=_cparams(("arbitrary",) * 2, 56),
        name="inproj_sample",
    )(x2d.reshape(x_shape), *ops)
    return dict(zip(names, outs))


def _inproj_params(w_in0):
    wt = w_in0.T
    gn_rows = 2 * 16
    return {"wa": w_in0[:, :N_MAIN].astype(BF16),
            "wat": wt[:N_MAIN].astype(BF16),
            "wgn": jnp.pad(w_in0[:, N_MAIN:N_MAIN + NSA_GATE_W], ((0, 0), (0, LANES - NSA_GATE_W))).astype(BF16),
            "wgnt": jnp.pad(wt[N_MAIN:N_MAIN + NSA_GATE_W], ((0, gn_rows - NSA_GATE_W), (0, 0))).astype(BF16),
            "wgab": w_in0[:, N_MAIN + NSA_GATE_W:].astype(BF16)}


def _s5_prompt_body(u_ref, wb_ref, ar_ref, ai_ref, cw_ref, d_ref, wglu_ref, bglu_ref, wso_ref,
                    abr_ref, hlast_ref, lhs_ref, bu_ref, h8_ref, p_ref, us_ref, hstate_ref):
    c = pl.program_id(0)
    nseq = 4
    r4 = u_ref.shape[0]
    tc = r4 // nseq
    half = tc // 2

    @pl.when(c == 0)
    def _():
        hstate_ref[...] = jnp.zeros_like(hstate_ref)

    u = u_ref[...]
    row2 = lax.broadcasted_iota(jnp.int32, (r4, SSM_WIDTH), 0)
    lo2 = (row2 % SUBLANES) < nseq
    up = pltpu.roll(u, r4 - nseq, axis=0)
    dn = pltpu.roll(u, nseq, axis=0)
    swapped = jnp.where(lo2, up, dn)
    zero = jnp.zeros_like(u)
    ev_re = jnp.where(lo2, u, zero).astype(BF16).reshape(half, SUBLANES, SSM_WIDTH)
    ev_im = jnp.where(lo2, zero, swapped).astype(BF16).reshape(half, SUBLANES, SSM_WIDTH)
    od_re = jnp.where(lo2, swapped, zero).astype(BF16).reshape(half, SUBLANES, SSM_WIDTH)
    od_im = jnp.where(lo2, zero, u).astype(BF16).reshape(half, SUBLANES, SSM_WIDTH)
    for j in range(4):
        sl = slice(LANES * j, LANES * (j + 1))
        lhs_ref[:, 0:8, 256 * j:256 * j + LANES] = ev_re[:, :, sl]
        lhs_ref[:, 0:8, 256 * j + LANES:256 * (j + 1)] = ev_im[:, :, sl]
        lhs_ref[:, 8:16, 256 * j:256 * j + LANES] = od_re[:, :, sl]
        lhs_ref[:, 8:16, 256 * j + LANES:256 * (j + 1)] = od_im[:, :, sl]
    for j in range(4):
        lhs = lhs_ref[:, :, 256 * j:256 * (j + 1)].reshape(tc * SUBLANES, 256)
        bu_ref[:, 512 * j:512 * (j + 1)] = jnp.dot(lhs, wb_ref[j], preferred_element_type=F32)

    for lc in range(4):
        sl = slice(512 * lc, 512 * (lc + 1))
        ar = ar_ref[:, sl]
        ai = ai_ref[:, sl]

        def step(t, h, sl=sl, ar=ar, ai=ai):
            r0 = pl.multiple_of(t * SUBLANES, SUBLANES)
            h = ar * h + ai * pltpu.roll(h, nseq, axis=0) + bu_ref[pl.ds(r0, SUBLANES), sl]
            h8_ref[pl.ds(r0, SUBLANES), sl] = h
            return h

        hstate_ref[:, sl] = lax.fori_loop(0, tc, step, hstate_ref[:, sl], unroll=8)
    hlast_ref[...] = hstate_ref[...]

    for j in range(4):
        pj = jnp.dot(h8_ref[:, 512 * j:512 * (j + 1)].astype(BF16), cw_ref[j], preferred_element_type=F32)
        p_ref[2 * j] = pj[:, 0:LANES]
        p_ref[2 * j + 1] = pj[:, LANES:2 * LANES]
        us_ref[j] = u[:, LANES * j:LANES * (j + 1)]
    ys = []
    for s in range(nseq):
        parts = []
        for j in range(4):
            re = p_ref[2 * j, pl.ds(s, tc, stride=SUBLANES), :]
            im = p_ref[2 * j + 1, pl.ds(nseq + s, tc, stride=SUBLANES), :]
            us = us_ref[j, pl.ds(s, tc, stride=nseq), :]
            parts.append(re + im + d_ref[:, LANES * j:LANES * (j + 1)] * us)
        ys.append(jnp.concatenate(parts, axis=1))
    y = jnp.concatenate(ys, axis=0)
    zg = jax.nn.gelu(y)
    gate = jnp.dot(zg.astype(BF16), wglu_ref[...], preferred_element_type=F32) + bglu_ref[...]
    glu = (zg * jax.nn.sigmoid(gate)).astype(BF16)
    abr = jnp.dot(glu, wso_ref[...], preferred_element_type=F32)
    for s in range(nseq):
        abr_ref[s] = abr[s * tc:(s + 1) * tc]


def _s5_prompt(u_ts, sp, wglu, bglu, wso, t_total, tc):
    nseq = 4
    grid = (t_total // tc,)
    abr, hlast = pl.pallas_call(
        _s5_prompt_body,
        grid=grid,
        in_specs=[pl.BlockSpec((tc * nseq, SSM_WIDTH), lambda c: (c, 0)),
                  _full(sp["wb8"].shape), _full(sp["ar8"].shape), _full(sp["ai8"].shape), _full(sp["cw8"].shape),
                  _full(sp["d"].shape), _full(wglu.shape), _full(bglu.shape), _full(wso.shape)],
        out_specs=[pl.BlockSpec((nseq, tc, D_MODEL), lambda c: (0, c, 0)),
                   pl.BlockSpec((SUBLANES, N_STATE), lambda c: (0, 0))],
        out_shape=[jax.ShapeDtypeStruct((nseq, t_total, D_MODEL), F32),
                   jax.ShapeDtypeStruct((SUBLANES, N_STATE), F32)],
        scratch_shapes=[pltpu.VMEM((tc // 2, 2 * SUBLANES, 1024), BF16),
                        pltpu.VMEM((tc * SUBLANES, N_STATE), F32),
                        pltpu.VMEM((tc * SUBLANES, N_STATE), F32),
                        pltpu.VMEM((8, tc * SUBLANES, LANES), F32),
                        pltpu.VMEM((4, tc * nseq, LANES), F32),
                        pltpu.VMEM((SUBLANES, N_STATE), F32)],
        compiler_params=_cparams(("arbitrary",), 56),
        name="s5_prompt",
    )(u_ts, sp["wb8"], sp["ar8"], sp["ai8"], sp["cw8"], sp["d"], wglu, bglu, wso)
    return abr, hlast


def _s5_params(lam_re, lam_im, log_dt, b_re, b_im, c_re, c_im, d_skip):
    lam = lax.complex(lam_re.astype(F32), lam_im.astype(F32))
    dt = jnp.exp(log_dt.astype(F32))[:, None]
    a_bar = jnp.exp(lam * dt)
    b = lax.complex(b_re.astype(F32), b_im.astype(F32))
    b_bar = ((a_bar - 1.0) / lam)[..., None] * b
    eye8 = jnp.eye(8, dtype=F32)

    def bd_b(m):
        return jnp.einsum("ab,jbpc->jacbp", eye8, m.reshape(4, 8, SSM_STATE, SSM_GROUP)).reshape(4, 128, 512)

    def bd_c(m):
        return jnp.einsum("ab,jbcp->japbc", eye8, m.reshape(4, 8, SSM_GROUP, SSM_STATE)).reshape(4, 512, 128)

    wre, wim = bd_b(b_bar.real), bd_b(b_bar.imag)
    cre, cim = bd_c(c_re.astype(F32)), bd_c(c_im.astype(F32))
    ar = a_bar.real.reshape(1, N_STATE)
    ai = a_bar.imag.reshape(1, N_STATE)
    sign = jnp.concatenate([-jnp.ones((4, 1), F32), jnp.ones((4, 1), F32)], axis=0)
    return {
        "wb8": jnp.concatenate([wre, wim], axis=1).astype(BF16),
        "cw8": jnp.concatenate([cre, -cim], axis=2).astype(BF16),
        "ar8": jnp.broadcast_to(ar, (SUBLANES, N_STATE)),
        "ai8": sign * ai,
        "wre": wre.astype(BF16), "wim": wim.astype(BF16),
        "cre": cre.astype(BF16), "cim": cim.astype(BF16),
        "ar": ar, "ai": ai,
        "d": d_skip.astype(F32).reshape(1, SSM_WIDTH),
    }


def _cmp_params(cmp_pe, cmp_w1, cmp_w2):
    eye2 = jnp.eye(2, dtype=F32)
    w1s, pes = [], []
    for s in range(CMP_LEN // CMP_STRIDE):
        w = cmp_w1[:, s * CMP_STRIDE:(s + 1) * CMP_STRIDE].astype(F32)
        w1s.append(jnp.einsum("pk,qh,kide->ipqdkhe", eye2, eye2, w).reshape(CMP_STRIDE * KV_W, KV_W))
        pe = cmp_pe[:, s * CMP_STRIDE:(s + 1) * CMP_STRIDE].astype(F32)
        pes.append(jnp.broadcast_to(pe.transpose(1, 0, 2)[:, :, None, :],
                                    (CMP_STRIDE, 2, N_KV_HEADS, HEAD_DIM)).reshape(1, CMP_STRIDE * KV_W))
    w2 = jnp.einsum("pk,qh,kef->pqekhf", eye2, eye2, cmp_w2.astype(F32)).reshape(KV_W, KV_W)
    nhalf = CMP_LEN // CMP_STRIDE
    w1r = cmp_w1.astype(F32).reshape(2, nhalf, CMP_STRIDE, HEAD_DIM, HEAD_DIM)
    wk = jnp.einsum("kside,ph->kipdshe", w1r, eye2).reshape(2, CMP_STRIDE * LANES, nhalf * LANES)
    bk = jnp.einsum("kld,klde->ke", cmp_pe.astype(F32), cmp_w1.astype(F32), precision=lax.Precision.HIGHEST)
    w2k = jnp.einsum("kef,ph->kpehf", cmp_w2.astype(F32), eye2).reshape(2, LANES, LANES)
    return {"w1": jnp.stack(w1s).astype(BF16),
            "pe": jnp.concatenate(pes, axis=0),
            "w2": w2.astype(BF16),
            "wk": wk.astype(BF16), "bk": jnp.tile(bk, (1, N_KV_HEADS)), "w2k": w2k.astype(BF16)}


def _compress_rows(x, pe_ref, w1_ref, w2_ref):
    nch = x.shape[0]
    p0 = jnp.dot((x + pe_ref[0:1, :]).astype(BF16), w1_ref[0], preferred_element_type=F32)
    p1 = jnp.dot((x + pe_ref[1:2, :]).astype(BF16), w1_ref[1], preferred_element_type=F32)
    pre = p0 + pltpu.roll(p1, nch - 1, axis=0)
    return jnp.dot(jax.nn.gelu(pre).astype(BF16), w2_ref[...], preferred_element_type=F32)


def _compress_prompt_body(x_ref, pe_ref, w1_ref, w2_ref, ck_ref, cvt_ref):
    out = _compress_rows(x_ref[0], pe_ref, w1_ref, w2_ref)
    ck_ref[0] = out[:, 0:LANES].astype(BF16)
    cvt_ref[0] = out[:, LANES:2 * LANES].T.astype(BF16)


def _compress_prompt(kvc, cp, nseq, t):
    nch = t // CMP_STRIDE
    x = kvc.reshape(nseq, nch, CMP_STRIDE * KV_W)
    return pl.pallas_call(
        _compress_prompt_body,
        grid=(nseq,),
        in_specs=[pl.BlockSpec((1, nch, CMP_STRIDE * KV_W), lambda n: (n, 0, 0)),
                  _full(cp["pe"].shape), _full(cp["w1"].shape), _full(cp["w2"].shape)],
        out_specs=[pl.BlockSpec((1, nch, LANES), lambda n: (n, 0, 0)),
                   pl.BlockSpec((1, LANES, nch), lambda n: (n, 0, 0))],
        out_shape=[jax.ShapeDtypeStruct((nseq, nch, LANES), BF16),
                   jax.ShapeDtypeStruct((nseq, LANES, nch), BF16)],
        compiler_params=_cparams(("arbitrary",), 48),
        name="compress_prompt",
    )(x, cp["pe"], cp["w1"], cp["w2"])


def _overlap_t(n_cmp_pad, n_slc_pad):
    j = np.arange(n_cmp_pad)[None, :]
    s = np.arange(n_slc_pad)[:, None]
    ov = (j * CMP_STRIDE <= s * SLC_BLOCK + SLC_BLOCK - 1) & (j * CMP_STRIDE + CMP_LEN - 1 >= s * SLC_BLOCK)
    return jnp.asarray(ov, dtype=BF16)


def _softmax_cols(s, valid):
    sm = jnp.where(valid, s, NEG_INF)
    mx = jnp.max(sm, axis=0, keepdims=True)
    e = jnp.where(valid, jnp.exp(sm - mx), 0.0)
    l = jnp.sum(e, axis=0, keepdims=True)
    return e * (1.0 / jnp.maximum(l, 1e-30))


def _select_blocks(imp, blk, pos, nblk, axis=0):
    cur = pos // SLC_BLOCK
    forced = (blk == 0) | (blk == cur) | (blk == cur - 1)
    v = jnp.where(forced, imp + FORCE_BONUS, imp)
    v = jnp.where(blk * SLC_BLOCK <= pos, v, NEG_INF)
    v = jnp.where(blk < nblk, v, -3e38)
    blk_f = blk.astype(F32)
    neg = jnp.full(imp.shape, NEG_INF, F32)
    for _ in range(min(TOP_N, nblk)):
        mx = jnp.max(v, axis=axis, keepdims=True)
        first = jnp.min(jnp.where(v == mx, blk_f, float(imp.shape[axis])), axis=axis, keepdims=True)
        pick = blk_f == first
        neg = jnp.where(pick, 0.0, neg)
        v = jnp.where(pick, -3e38, v)
    return neg


def _attn_prompt_body(q_ref, gn_ref, ck_ref, cvt_ref, ks_ref, vst_ref, kw_ref, vwt_ref, ovt_ref,
                      o_ref, kaug_ref, qaug_ref, acc_ref):
    i = pl.program_id(1)
    t = ks_ref.shape[1]
    nch = ck_ref.shape[1]
    nslc = t // SLC_BLOCK
    qb = Q_BLOCK
    ncol = N_Q_HEADS * qb
    q0 = i * qb

    @pl.when(i == 0)
    def _():
        kaug_ref[:, 0:LANES] = ks_ref[0]
        blk = lax.broadcasted_iota(jnp.int32, (t, LANES), 0) // SLC_BLOCK
        col = lax.broadcasted_iota(jnp.int32, (t, LANES), 1)
        kaug_ref[:, LANES:2 * LANES] = jnp.where(blk == col, 1.0, 0.0).astype(BF16)

    zeros64 = jnp.zeros((HEAD_DIM, qb), BF16)
    for j in range(N_Q_HEADS):
        dst = j // GQA
        qaug_ref[HEAD_DIM * dst:HEAD_DIM * (dst + 1), qb * j:qb * (j + 1)] = q_ref[0, HEAD_DIM * j:HEAD_DIM * (j + 1), :]
        qaug_ref[HEAD_DIM * (1 - dst):HEAD_DIM * (2 - dst), qb * j:qb * (j + 1)] = zeros64
    qaug_ref[2 * LANES - HEAD_DIM:2 * LANES, :] = jnp.zeros((HEAD_DIM, ncol), BF16)
    qst = qaug_ref[0:LANES, :]

    pos_c = q0 + (lax.broadcasted_iota(jnp.int32, (nch, ncol), 1) & (qb - 1))
    cend = lax.broadcasted_iota(jnp.int32, (nch, ncol), 0) * CMP_STRIDE + (CMP_LEN - 1)
    sc = jnp.dot(ck_ref[0], qst, preferred_element_type=F32)
    pc = _softmax_cols(sc, cend <= pos_c).astype(BF16)
    oc = jnp.dot(cvt_ref[0], pc, preferred_element_type=F32)
    imp = jnp.dot(ovt_ref[...], pc, preferred_element_type=F32)
    blk = lax.broadcasted_iota(jnp.int32, (nslc, qb), 0)
    pos_q = q0 + lax.broadcasted_iota(jnp.int32, (nslc, qb), 1)
    for h in range(N_KV_HEADS):
        v = imp[0:nslc, qb * GQA * h:qb * GQA * h + qb]
        for g in range(1, GQA):
            v = v + imp[0:nslc, qb * (GQA * h + g):qb * (GQA * h + g + 1)]
        neg = _select_blocks(v, blk, pos_q, nslc).astype(BF16)
        for g in range(GQA):
            j = GQA * h + g
            qaug_ref[LANES:LANES + nslc, qb * j:qb * (j + 1)] = neg
    if nslc < HEAD_DIM:
        qaug_ref[LANES + nslc:LANES + HEAD_DIM, :] = jnp.zeros((HEAD_DIM - nslc, ncol), BF16)

    pos_t = q0 + (lax.broadcasted_iota(jnp.int32, (TK_SLC, ncol), 1) & (qb - 1))
    krow = lax.broadcasted_iota(jnp.int32, (TK_SLC, ncol), 0)
    acc_ref[...] = jnp.zeros_like(acc_ref)

    def tile(kt, carry, diag):
        m, l = carry
        k0 = pl.multiple_of(kt * TK_SLC, TK_SLC)
        s = jnp.dot(kaug_ref[pl.ds(k0, TK_SLC), :], qaug_ref[...], preferred_element_type=F32)
        if diag:
            s = jnp.where(krow + k0 <= pos_t, s, NEG_INF)
        mn = jnp.maximum(m, jnp.max(s, axis=0, keepdims=True))
        alpha = jnp.exp(m - mn)
        p = jnp.exp(s - mn)
        l = alpha * l + jnp.sum(p, axis=0, keepdims=True)
        acc_ref[...] = alpha * acc_ref[...] + jnp.dot(vst_ref[0, kt], p.astype(BF16), preferred_element_type=F32)
        return mn, l

    last = (q0 + qb - 1) // TK_SLC
    carry = (jnp.full((1, ncol), NEG_INF, F32), jnp.zeros((1, ncol), F32))
    carry = lax.fori_loop(0, last, functools.partial(tile, diag=False), carry)
    _, l = tile(last, carry, True)
    osel = acc_ref[...] * (1.0 / l)

    npiece = (WINDOW + qb) // TK_WIN
    start = jnp.maximum(q0 - WINDOW, 0)
    pb0 = start // TK_WIN
    pos_w = q0 + (lax.broadcasted_iota(jnp.int32, (TK_WIN, ncol), 1) & (qb - 1))
    wrow = lax.broadcasted_iota(jnp.int32, (TK_WIN, ncol), 0)
    sw, mw = [], []
    for w in range(npiece):
        k0 = pl.multiple_of(start + w * TK_WIN, TK_WIN)
        sw.append(jnp.dot(kw_ref[0, pl.ds(k0, TK_WIN), :], qst, preferred_element_type=F32))
        dlt = pos_w - (wrow + k0)
        mw.append((dlt >= 0) & (dlt < WINDOW))
    pw = _softmax_cols(jnp.concatenate(sw, axis=0), jnp.concatenate(mw, axis=0)).astype(BF16)
    ow = jnp.zeros((LANES, ncol), F32)
    for w in range(npiece):
        ow = ow + jnp.dot(vwt_ref[0, pb0 + w], pw[w * TK_WIN:(w + 1) * TK_WIN], preferred_element_type=F32)

    gt = gn_ref[0]
    for c in range(N_Q_HEADS // 2):
        rows = []
        for hh in range(2):
            j = 2 * c + hh
            rs = slice(HEAD_DIM * (j // GQA), HEAD_DIM * (j // GQA + 1))
            cs = slice(qb * j, qb * (j + 1))
            rows.append(gt[3 * j:3 * j + 1, :] * oc[rs, cs] + gt[3 * j + 1:3 * j + 2, :] * osel[rs, cs]
                        + gt[3 * j + 2:3 * j + 3, :] * ow[rs, cs])
        o_ref[:, LANES * c:LANES * (c + 1)] = jnp.concatenate(rows, axis=0).T.astype(o_ref.dtype)


def _attn_prompt(q, gn, ck, cvt, ksb, vst, kwb, vwt, nseq, t):
    nb = t // Q_BLOCK
    nch = t // CMP_STRIDE
    nslc = t // SLC_BLOCK
    ovt = _overlap_t(nch, max(nslc, SUBLANES))
    row = lambda n, i: (n * nb + i, 0)
    seq3 = lambda n, i: (n, 0, 0)
    seq4 = lambda n, i: (n, 0, 0, 0)
    col3 = lambda n, i: (n, 0, i)
    return pl.pallas_call(
        _attn_prompt_body,
        grid=(nseq, nb),
        in_specs=[pl.BlockSpec((1, Q_W, Q_BLOCK), col3), pl.BlockSpec((1, gn.shape[1], Q_BLOCK), col3),
                  pl.BlockSpec((1, nch, LANES), seq3), pl.BlockSpec((1, LANES, nch), seq3),
                  pl.BlockSpec((1, t, LANES), seq3), pl.BlockSpec((1, t // TK_SLC, LANES, TK_SLC), seq4),
                  pl.BlockSpec((1, t, LANES), seq3), pl.BlockSpec((1, t // TK_WIN, LANES, TK_WIN), seq4),
                  _full(ovt.shape)],
        out_specs=pl.BlockSpec((Q_BLOCK, Q_W), row),
        out_shape=jax.ShapeDtypeStruct((nseq * t, Q_W), BF16),
        scratch_shapes=[pltpu.VMEM((t, 2 * LANES), BF16),
                        pltpu.VMEM((2 * LANES, N_Q_HEADS * Q_BLOCK), BF16),
                        pltpu.VMEM((LANES, N_Q_HEADS * Q_BLOCK), F32)],
        compiler_params=_cparams(("arbitrary", "arbitrary"), 56),
        name="attn_prompt",
    )(q, gn, ck, cvt, ksb.reshape(nseq, t, LANES), vst, kwb.reshape(nseq, t, LANES), vwt, ovt)


def _post_body(x_ref, abr_ref, on_ref, ga_ref, gb_ref, wno_ref, wo_ref, g2_ref, x1_ref, h2_ref):
    bbr = jnp.dot(on_ref[...].astype(BF16), wno_ref[...], preferred_element_type=F32)
    merged = ga_ref[...] * abr_ref[...] + gb_ref[...] * bbr
    x1 = x_ref[...] + jnp.dot(merged.astype(BF16), wo_ref[...], preferred_element_type=F32)
    x1_ref[...] = x1
    inv = lax.rsqrt(jnp.mean(x1 * x1, axis=-1, keepdims=True) + RMS_EPS)
    h2_ref[...] = (x1 * inv * g2_ref[...]).astype(BF16)


def _post(x2d, abr, abr_lay, onsa, ga, gb, wno, wo, g2, lay, out_lay):
    x_shape, x_spec = lay["a"](D_MODEL)
    abr_shape, abr_spec = lay[abr_lay](D_MODEL)
    on_shape, on_spec = lay["a"](Q_W)
    o_shape, o_spec = lay[out_lay](D_MODEL)
    return pl.pallas_call(
        _post_body,
        grid=lay["grid"],
        in_specs=[x_spec, abr_spec, on_spec, x_spec, x_spec, _full(wno.shape), _full(wo.shape), _full(g2.shape)],
        out_specs=[o_spec, o_spec],
        out_shape=[jax.ShapeDtypeStruct(o_shape, F32), jax.ShapeDtypeStruct(o_shape, BF16)],
        compiler_params=_cparams(("arbitrary",) * len(lay["grid"]), 48),
        name="post",
    )(x2d.reshape(x_shape), abr.reshape(abr_shape), onsa.reshape(on_shape), ga, gb, wno, wo, g2)


def _route(logits):
    lane = lax.broadcasted_iota(jnp.int32, logits.shape, 1).astype(F32)
    big = float(LANES)
    glog = jnp.where(lane < N_EXPERT_GROUPS, logits, -jnp.inf)
    gmax = jnp.max(glog, axis=1, keepdims=True)
    gsel = jnp.min(jnp.where(glog == gmax, lane, big), axis=1, keepdims=True)
    gw = 1.0 / jnp.sum(jnp.exp(glog - gmax), axis=1, keepdims=True)
    lo = N_EXPERT_GROUPS + EXPERTS_PER_GROUP * gsel
    el = jnp.where((lane >= lo) & (lane < lo + EXPERTS_PER_GROUP), logits, -jnp.inf)
    v1 = jnp.max(el, axis=1, keepdims=True)
    i1 = jnp.min(jnp.where(el == v1, lane, big), axis=1, keepdims=True)
    el2 = jnp.where(lane == i1, -jnp.inf, el)
    v2 = jnp.max(el2, axis=1, keepdims=True)
    i2 = jnp.min(jnp.where(el2 == v2, lane, big), axis=1, keepdims=True)
    e2 = jnp.exp(v2 - v1)
    w1 = gw / (1.0 + e2)
    return jnp.where(lane == i1, w1, 0.0) + jnp.where(lane == i2, w1 * e2, 0.0)


def _moe_body(x1_ref, h2_ref, p_ref, wr_ref, br_ref, wg_ref, wu_ref, wd_ref, wpg_ref, wp_ref, gf_ref,
              y_ref, acc_ref, comb_ref, *, tsplit):
    g = pl.program_id(1)
    h2 = h2_ref[...]

    @pl.when(g == 0)
    def _():
        logits = jnp.dot(h2, wr_ref[...], preferred_element_type=F32) + br_ref[...]
        comb_ref[...] = _route(logits)
        acc_ref[...] = jnp.zeros_like(acc_ref)

    comb = comb_ref[...]
    lane = lax.broadcasted_iota(jnp.int32, comb.shape, 1)
    acc = acc_ref[...]
    for k in range(EXPERTS_PER_GROUP):
        e_lane = N_EXPERT_GROUPS + EXPERTS_PER_GROUP * g + k
        ce = jnp.sum(jnp.where(lane == e_lane, comb, 0.0), axis=1, keepdims=True)
        a = jnp.dot(h2, wg_ref[k], preferred_element_type=F32)
        b = jnp.dot(h2, wu_ref[k], preferred_element_type=F32)
        act = (jax.nn.silu(a) * b * ce).astype(BF16)
        acc = acc + jnp.dot(act, wd_ref[k], preferred_element_type=F32)
    acc_ref[...] = acc

    @pl.when(g == N_EXPERT_GROUPS - 1)
    def _():
        x2 = x1_ref[...] + acc_ref[...]
        rows = x2.shape[0] // tsplit
        if tsplit == 1:
            p = p_ref[...]
        else:
            p = jnp.concatenate([p_ref[:, PLE_DIM * t:PLE_DIM * (t + 1)] for t in range(tsplit)], axis=0)
        gate = jax.nn.sigmoid(jnp.dot(x2.astype(BF16), wpg_ref[...], preferred_element_type=F32))
        x3 = x2 + gate * jnp.dot(p.astype(BF16), wp_ref[...], preferred_element_type=F32)
        inv = lax.rsqrt(jnp.mean(x3 * x3, axis=-1, keepdims=True) + RMS_EPS)
        y = x3 * inv * gf_ref[...]
        if tsplit == 1:
            y_ref[...] = y
        else:
            for t in range(tsplit):
                y_ref[:, D_MODEL * t:D_MODEL * (t + 1)] = y[rows * t:rows * (t + 1)]


def _moe(x1, h2, p, mp, tm, tsplit):
    rows = x1.shape[0]
    nrb = rows // tm
    rb = lambda r, g: (r, 0)
    grp = lambda r, g: (g, 0, 0)
    if tsplit == 1:
        p_spec = pl.BlockSpec((tm, PLE_DIM), rb)
        y_spec = pl.BlockSpec((tm, D_MODEL), rb)
        y_shape = (rows, D_MODEL)
    else:
        assert nrb == 1
        p_spec = _full(p.shape)
        y_shape = (rows // tsplit, tsplit * D_MODEL)
        y_spec = _full(y_shape)
    return pl.pallas_call(
        functools.partial(_moe_body, tsplit=tsplit),
        grid=(nrb, N_EXPERT_GROUPS),
        in_specs=[pl.BlockSpec((tm, D_MODEL), rb), pl.BlockSpec((tm, D_MODEL), rb), p_spec,
                  _full(mp["wr"].shape), _full(mp["br"].shape),
                  pl.BlockSpec((EXPERTS_PER_GROUP, D_MODEL, D_FF_EXPERT), grp),
                  pl.BlockSpec((EXPERTS_PER_GROUP, D_MODEL, D_FF_EXPERT), grp),
                  pl.BlockSpec((EXPERTS_PER_GROUP, D_FF_EXPERT, D_MODEL), grp),
                  _full(mp["wpg"].shape), _full(mp["wp"].shape), _full(mp["gf"].shape)],
        out_specs=y_spec,
        out_shape=jax.ShapeDtypeStruct(y_shape, F32),
        scratch_shapes=[pltpu.VMEM((tm, D_MODEL), F32), pltpu.VMEM((tm, LANES), F32)],
        compiler_params=_cparams(("arbitrary", "arbitrary"), 56),
        name="moe_ple",
    )(x1, h2, p, mp["wr"], mp["br"], mp["wg"], mp["wu"], mp["wd"], mp["wpg"], mp["wp"], mp["gf"])


def _s5_sample_body(u_ref, h0re_ref, h0im_ref, wre_ref, wim_ref, ar_ref, ai_ref, cre_ref, cim_ref, d_ref,
                    wglu_ref, bglu_ref, wso_ref, abr_ref, hre_out_ref, him_out_ref,
                    bure_ref, buim_ref, hre_ref, him_ref, *, nseq, nstep):
    u = u_ref[...]
    ub = u.astype(BF16)
    for j in range(4):
        lhs = ub[:, LANES * j:LANES * (j + 1)]
        bure_ref[:, 512 * j:512 * (j + 1)] = jnp.dot(lhs, wre_ref[j], preferred_element_type=F32)
        buim_ref[:, 512 * j:512 * (j + 1)] = jnp.dot(lhs, wim_ref[j], preferred_element_type=F32)
    for lc in range(4):
        sl = slice(512 * lc, 512 * (lc + 1))
        ar = jnp.broadcast_to(ar_ref[:, sl], (SUBLANES, 512))
        ai = jnp.broadcast_to(ai_ref[:, sl], (SUBLANES, 512))

        def body(rc, carry, sl=sl, ar=ar, ai=ai):
            r0 = pl.multiple_of(rc * SUBLANES, SUBLANES)
            hr = h0re_ref[pl.ds(r0, SUBLANES), sl]
            hi = h0im_ref[pl.ds(r0, SUBLANES), sl]
            for t in range(nstep):
                rr = pl.multiple_of(t * nseq + rc * SUBLANES, SUBLANES)
                hr, hi = (ar * hr - ai * hi + bure_ref[pl.ds(rr, SUBLANES), sl],
                          ar * hi + ai * hr + buim_ref[pl.ds(rr, SUBLANES), sl])
                hre_ref[pl.ds(rr, SUBLANES), sl] = hr
                him_ref[pl.ds(rr, SUBLANES), sl] = hi
            hre_out_ref[pl.ds(r0, SUBLANES), sl] = hr
            him_out_ref[pl.ds(r0, SUBLANES), sl] = hi
            return carry

        lax.fori_loop(0, nseq // SUBLANES, body, 0)
    parts = []
    for j in range(4):
        sl = slice(512 * j, 512 * (j + 1))
        parts.append(jnp.dot(hre_ref[:, sl].astype(BF16), cre_ref[j], preferred_element_type=F32)
                     - jnp.dot(him_ref[:, sl].astype(BF16), cim_ref[j], preferred_element_type=F32))
    y = jnp.concatenate(parts, axis=1) + d_ref[...] * u
    zg = jax.nn.gelu(y)
    gate = jnp.dot(zg.astype(BF16), wglu_ref[...], preferred_element_type=F32) + bglu_ref[...]
    glu = (zg * jax.nn.sigmoid(gate)).astype(BF16)
    abr_ref[...] = jnp.dot(glu, wso_ref[...], preferred_element_type=F32)


def _s5_sample(u_ts, h0re, h0im, sp, wglu, bglu, wso, nseq, nstep):
    rows = nseq * nstep
    ops = [u_ts, h0re, h0im, sp["wre"], sp["wim"], sp["ar"], sp["ai"], sp["cre"], sp["cim"], sp["d"], wglu, bglu, wso]
    return pl.pallas_call(
        functools.partial(_s5_sample_body, nseq=nseq, nstep=nstep),
        grid=(1,),
        in_specs=[_full(o.shape) for o in ops],
        out_specs=[_full((rows, D_MODEL)), _full((nseq, N_STATE)), _full((nseq, N_STATE))],
        out_shape=[jax.ShapeDtypeStruct((rows, D_MODEL), F32),
                   jax.ShapeDtypeStruct((nseq, N_STATE), F32), jax.ShapeDtypeStruct((nseq, N_STATE), F32)],
        scratch_shapes=[pltpu.VMEM((rows, N_STATE), F32) for _ in range(4)],
        compiler_params=_cparams(("arbitrary",), 56),
        name="s5_sample",
    )(*ops)


def _softmax_rows(s, valid):
    sm = jnp.where(valid, s, NEG_INF)
    mx = jnp.max(sm, axis=1, keepdims=True)
    e = jnp.where(valid, jnp.exp(sm - mx), 0.0)
    l = jnp.sum(e, axis=1, keepdims=True)
    return e * (1.0 / jnp.maximum(l, 1e-30))


def _attn_sample_body(pt_ref, q_ref, gn_ref, nks_ref, nkw_ref, wint_ref, wk_ref, bk_ref, w2k_ref, ov_ref, e_ref, *rest,
                      npage, past_len):
    cmp_pages = rest[:npage]
    slc_pages = rest[npage:2 * npage]
    o_ref, nwint_ref, xrow_ref = rest[2 * npage:]
    tq = q_ref.shape[0]
    nrow = N_Q_HEADS * tq
    nwin = wint_ref.shape[2]
    nslc = -(-(past_len + tq) // SLC_BLOCK)
    nch = past_len // CMP_STRIDE

    nks = jnp.concatenate([nks_ref[...], jnp.zeros((LANES - tq, KV_W), F32)], axis=0)
    nkw = jnp.concatenate([nkw_ref[...], jnp.zeros((LANES - tq, KV_W), F32)], axis=0)
    wint = wint_ref[0]
    lane_w = lax.broadcasted_iota(jnp.int32, (KV_W, LANES), 1)
    shifted = pltpu.roll(wint, nwin - tq, axis=1)
    new_t = pltpu.roll(nkw.T, LANES - tq, axis=1)
    nwint_ref[0, :, 0:nwin - LANES] = shifted[:, 0:nwin - LANES]
    nwint_ref[0, :, nwin - LANES:nwin] = jnp.where(lane_w >= LANES - tq, new_t, shifted[:, nwin - LANES:nwin])

    q = q_ref[...]
    lane8 = lax.broadcasted_iota(jnp.int32, (tq, LANES), 1)
    qrows = []
    for j in range(N_Q_HEADS):
        chunk = q[:, LANES * (j // 2):LANES * (j // 2 + 1)]
        dst = j // GQA
        if (j % 2) != dst:
            chunk = pltpu.roll(chunk, HEAD_DIM, axis=1)
        keep = (lane8 < HEAD_DIM) if dst == 0 else (lane8 >= HEAD_DIM)
        qrows.append(jnp.where(keep, chunk, 0.0))
    qs = jnp.concatenate(qrows, axis=0).astype(BF16)
    pos = past_len + (lax.broadcasted_iota(jnp.int32, (nrow, LANES), 0) & (tq - 1))
    lane = lax.broadcasted_iota(jnp.int32, (nrow, LANES), 1)

    for p in range(npage):
        pg = cmp_pages[p][0]
        xrow_ref[0, PAGE_SIZE * p:PAGE_SIZE * (p + 1), :] = pg[0:LANES].T
        xrow_ref[1, PAGE_SIZE * p:PAGE_SIZE * (p + 1), :] = pg[LANES:2 * LANES].T
    cmp = []
    for kv in range(2):
        taps = [xrow_ref[kv, pl.ds(i, nch, stride=CMP_STRIDE), :] for i in range(CMP_STRIDE)]
        pp = jnp.dot(jnp.concatenate(taps, axis=1).astype(BF16), wk_ref[kv], preferred_element_type=F32)
        pre = pp[:, 0:LANES] + pltpu.roll(pp[:, LANES:2 * LANES], nch - 1, axis=0) + bk_ref[kv:kv + 1, :]
        cmp.append(jnp.dot(jax.nn.gelu(pre).astype(BF16), w2k_ref[kv], preferred_element_type=F32).astype(BF16))
    ck, cv = cmp
    sc = _dot_t(qs, ck)
    pc = _softmax_rows(sc, lane * CMP_STRIDE + (CMP_LEN - 1) <= pos).astype(BF16)
    oc = jnp.dot(pc, cv, preferred_element_type=F32)
    imp = jnp.dot(pc, ov_ref[...], preferred_element_type=F32)
    negs = []
    for h in range(N_KV_HEADS):
        v = imp[tq * GQA * h:tq * GQA * h + tq]
        for g in range(1, GQA):
            v = v + imp[tq * (GQA * h + g):tq * (GQA * h + g + 1)]
        neg = _select_blocks(v, lane[0:tq], pos[0:tq], nslc, axis=1)
        negs.extend([neg] * GQA)
    negsel = jnp.concatenate(negs, axis=0)
    qaug = jnp.concatenate([qs, negsel.astype(BF16)], axis=1)

    parts = []
    for p in range(0, npage, 2):
        kt = jnp.concatenate([slc_pages[p][0][0:LANES], slc_pages[p + 1][0][0:LANES]], axis=1).astype(BF16)
        et = jnp.concatenate([e_ref[p], e_ref[p + 1]], axis=1)
        parts.append(jnp.dot(qaug, jnp.concatenate([kt, et], axis=0), preferred_element_type=F32))
    new_blk = past_len // SLC_BLOCK
    parts.append(_dot_t(qs, nks[:, 0:LANES].astype(BF16)) + negsel[:, new_blk:new_blk + 1])
    ss = jnp.concatenate(parts, axis=1)
    nkeys = ss.shape[1]
    kpos = lax.broadcasted_iota(jnp.int32, (nrow, nkeys), 1)
    pos_k = past_len + (lax.broadcasted_iota(jnp.int32, (nrow, nkeys), 0) & (tq - 1))
    ps = _softmax_rows(ss, kpos <= pos_k).astype(BF16)
    osel = jnp.dot(ps[:, past_len:nkeys], nks[:, LANES:2 * LANES].astype(BF16), preferred_element_type=F32)
    for p in range(0, npage, 2):
        vt = jnp.concatenate([slc_pages[p][0][LANES:2 * LANES], slc_pages[p + 1][0][LANES:2 * LANES]],
                             axis=1).astype(BF16)
        osel = osel + _dot_t(ps[:, PAGE_SIZE * p:PAGE_SIZE * (p + 2)], vt)

    sw = jnp.concatenate([jnp.dot(qs, wint[0:LANES].astype(BF16), preferred_element_type=F32),
                          _dot_t(qs, nkw[:, 0:LANES].astype(BF16))], axis=1)
    nw = sw.shape[1]
    widx = lax.broadcasted_iota(jnp.int32, (nrow, nw), 1)
    pos_w = past_len + (lax.broadcasted_iota(jnp.int32, (nrow, nw), 0) & (tq - 1))
    dlt = pos_w - (past_len - nwin + widx)
    pw = _softmax_rows(sw, (dlt >= 0) & (dlt < WINDOW) & (widx < nwin + tq)).astype(BF16)
    ow = (_dot_t(pw[:, 0:nwin], wint[LANES:2 * LANES].astype(BF16))
          + jnp.dot(pw[:, nwin:nw], nkw[:, LANES:2 * LANES].astype(BF16), preferred_element_type=F32))

    gn = gn_ref[...]
    for c in range(N_Q_HEADS // 2):
        halves = []
        for hh in range(2):
            j = 2 * c + hh
            rs = slice(tq * j, tq * (j + 1))
            oj = (gn[:, 3 * j:3 * j + 1] * oc[rs] + gn[:, 3 * j + 1:3 * j + 2] * osel[rs]
                  + gn[:, 3 * j + 2:3 * j + 3] * ow[rs])
            if (j // GQA) != hh:
                oj = pltpu.roll(oj, HEAD_DIM, axis=1)
            halves.append(oj)
        o_ref[:, LANES * c:LANES * (c + 1)] = jnp.where(lane8 < HEAD_DIM, halves[0], halves[1])


def _attn_sample(q, gn, nks, nkw, cache_cmp, cache_slc, cache_win, page_table, cp, nseq, tq, past_len):
    assert tq <= CMP_STRIDE and past_len % PAGE_SIZE == 0
    npage = past_len // PAGE_SIZE
    assert npage % 2 == 0 and PAGE_SIZE == LANES
    n_pool = cache_cmp.shape[0]
    nwin = cache_win.shape[1]
    chunks = past_len // CMP_STRIDE
    ov = _overlap_t(chunks, LANES).T
    key = np.arange(past_len).reshape(npage, 1, PAGE_SIZE)
    e = jnp.asarray(np.arange(LANES).reshape(1, LANES, 1) == key // SLC_BLOCK, dtype=BF16)
    to_t = lambda c: jnp.transpose(c, (0, 2, 3, 4, 1)).reshape(c.shape[0], KV_W, c.shape[1])
    cmp_t, slc_t, win_t = to_t(cache_cmp), to_t(cache_slc), to_t(cache_win)
    row = lambda n, pt: (n, 0)
    seq3 = lambda n, pt: (n, 0, 0)
    page = lambda p: (lambda n, pt: (pt[n, p], 0, 0))
    consts = [cp["wk"], cp["bk"], cp["w2k"], ov, e]
    in_specs = [pl.BlockSpec((tq, Q_W), row), pl.BlockSpec((tq, LANES), row),
                pl.BlockSpec((tq, KV_W), row), pl.BlockSpec((tq, KV_W), row),
                pl.BlockSpec((1, KV_W, nwin), seq3)]
    in_specs += [pl.BlockSpec(c.shape, (lambda nd: lambda n, pt: (0,) * nd)(c.ndim)) for c in consts]
    in_specs += [pl.BlockSpec((1, KV_W, PAGE_SIZE), page(p)) for p in range(npage)] * 2
    grid_spec = pltpu.PrefetchScalarGridSpec(
        num_scalar_prefetch=1,
        grid=(nseq,),
        in_specs=in_specs,
        out_specs=[pl.BlockSpec((tq, Q_W), row), pl.BlockSpec((1, KV_W, nwin), seq3)],
        scratch_shapes=[pltpu.VMEM((2, past_len, LANES), F32)],
    )
    return pl.pallas_call(
        functools.partial(_attn_sample_body, npage=npage, past_len=past_len),
        grid_spec=grid_spec,
        out_shape=[jax.ShapeDtypeStruct((nseq * tq, Q_W), F32), jax.ShapeDtypeStruct((nseq, KV_W, nwin), F32)],
        compiler_params=_cparams(("arbitrary",), 56),
        name="attn_sample",
    )(page_table, q, gn, nks, nkw, win_t, *consts, *([cmp_t] * npage), *([slc_t] * npage))


def _moe_params(w_rg, b_rg, w_re, b_re, w_gate, w_up, w_down, w_ple, w_ple_gate, gf):
    pad = LANES - N_EXPERT_GROUPS - N_EXPERTS
    return {"wr": jnp.pad(jnp.concatenate([w_rg, w_re], axis=1), ((0, 0), (0, pad))).astype(BF16),
            "br": jnp.pad(jnp.concatenate([b_rg, b_re]), (0, pad)).astype(F32).reshape(1, LANES),
            "wg": w_gate.astype(BF16), "wu": w_up.astype(BF16), "wd": w_down.astype(BF16),
            "wpg": w_ple_gate.astype(BF16), "wp": w_ple.astype(BF16), "gf": gf.astype(F32).reshape(1, D_MODEL)}


TM_PROMPT = 512
TC_S5 = 128


def kernel(x_prompt, x_sample, p_prompt, p_sample, cache_cmp_kv, cache_slc_kv, cache_win_kv, state_ssm, page_table, norm1_g, w_in, ssm_lam_re, ssm_lam_im, ssm_log_dt, ssm_b_re, ssm_b_im, ssm_c_re, ssm_c_im, ssm_d, w_glu, b_glu, cmp_pe, cmp_w1, cmp_w2, w_ssm_out, w_nsa_out, w_o, norm2_g, w_route_group, b_route_group, w_route_expert, b_route_expert, w_exp_gate, w_exp_up, w_exp_down, w_ple, w_ple_gate, final_norm_g):
    assert w_in.shape[0] == 1, "one layer"
    l = 0
    nb, t = x_prompt.shape[:2]
    ns, ts = x_sample.shape[:2]
    past_len = page_table.shape[1] * PAGE_SIZE
    kvt = (2, N_KV_HEADS, HEAD_DIM)

    wi = _inproj_params(w_in[l])
    g1 = norm1_g[l].astype(F32).reshape(1, D_MODEL)
    g2 = norm2_g[l].astype(F32).reshape(1, D_MODEL)
    sp = _s5_params(ssm_lam_re[l], ssm_lam_im[l], ssm_log_dt[l], ssm_b_re[l], ssm_b_im[l], ssm_c_re[l], ssm_c_im[l],
                    ssm_d[l])
    cp = _cmp_params(cmp_pe[l], cmp_w1[l], cmp_w2[l])
    mp = _moe_params(w_route_group[l], b_route_group[l], w_route_expert[l], b_route_expert[l], w_exp_gate[l],
                     w_exp_up[l], w_exp_down[l], w_ple[l], w_ple_gate[l], final_norm_g)
    wglu = w_glu[l].astype(BF16)
    bglu = b_glu[l].astype(F32).reshape(1, SSM_WIDTH)
    wso = w_ssm_out[l].astype(BF16)
    wno = w_nsa_out[l].astype(BF16)
    wo = w_o[l].astype(BF16)

    lay = _prompt_layout(nb, t, TM_PROMPT)
    xp = x_prompt.reshape(nb * t, D_MODEL)
    r = _inproj_prompt(xp, lay, g1, wi)
    abr, hlast = _s5_prompt(r["u"].reshape(t * nb, SSM_WIDTH), sp, wglu, bglu, wso, t, TC_S5)
    ck, cvt = _compress_prompt(r["kvc"], cp, nb, t)
    onsa = _attn_prompt(r["qt"], r["gnt"], ck, cvt, r["ksb"], r["vst"], r["kwb"], r["vwt"], nb, t)
    x1, h2 = _post(xp, abr, "a", onsa, r["ga"], r["gb"], wno, wo, g2, lay, "a")
    y_prompt = _moe(x1, h2, p_prompt[l].reshape(nb * t, PLE_DIM), mp, TM_PROMPT, 1).reshape(nb, t, D_MODEL)
    keep = min(WINDOW, t)

    def rows_last(a):
        return jnp.transpose(a.reshape((a.shape[0],) + kvt + (a.shape[2],)), (0, 4, 1, 2, 3))[None]

    new_cmp_p = rows_last(r["kvct"])
    new_slc_p = rows_last(r["kvst"])
    new_win_p = rows_last(r["kvwt"][:, :, t - keep:])
    new_ssm_p = jnp.stack([hlast[0:nb], hlast[nb:2 * nb]], axis=-1).reshape(1, nb, N_SSM_GROUPS, SSM_STATE, 2)

    lays = _sample_layout(ns, ts)
    xs = x_sample.reshape(ns * ts, D_MODEL)
    rs = _inproj_sample(xs, lays, g1, wi)
    h0 = state_ssm[l].astype(F32).reshape(ns, N_STATE, 2)
    abr_s, hre, him = _s5_sample(rs["u"], h0[..., 0], h0[..., 1], sp, wglu, bglu, wso, ns, ts)
    onsa_s, new_win = _attn_sample(rs["q"].reshape(ns * ts, Q_W), rs["gn"].reshape(ns * ts, LANES),
                                   rs["kvs"].reshape(ns * ts, KV_W), rs["kvw"].reshape(ns * ts, KV_W),
                                   cache_cmp_kv[l], cache_slc_kv[l], cache_win_kv[l], page_table, cp, ns, ts, past_len)
    x1s, h2s = _post(xs, abr_s, "b", onsa_s, rs["ga"], rs["gb"], wno, wo, g2, lays, "b")
    y_sample = _moe(x1s, h2s, p_sample[l].reshape(ns, ts * PLE_DIM), mp, ns * ts, ts).reshape(ns, ts, D_MODEL)
    steps_first = lambda a: jnp.transpose(a.reshape((ts,) + kvt + (ns,)), (4, 0, 1, 2, 3))[None]
    new_cmp_s = steps_first(rs["kvct"])
    new_slc_s = steps_first(rs["kvst"])
    new_win_s = rows_last(new_win)
    new_ssm_s = jnp.stack([hre, him], axis=-1).reshape(1, ns, N_SSM_GROUPS, SSM_STATE, 2)
    return (y_prompt, y_sample, new_cmp_p, new_slc_p, new_win_p, new_ssm_p,
            new_cmp_s, new_slc_s, new_win_s, new_ssm_s)
```

```python
import functools
import math

import jax
import jax.numpy as jnp
import numpy as np
from jax import lax
from jax.experimental import pallas as pl
from jax.experimental.pallas import tpu as pltpu

F32 = jnp.float32
BF16 = jnp.bfloat16

D_MODEL = 1024
SSM_WIDTH = 512
SSM_GROUP = 16
N_SSM_GROUPS = 32
SSM_STATE = 64
HEAD_DIM = 64
N_Q_HEADS = 8
N_KV_HEADS = 2
GQA = 4
CMP_LEN = 32
CMP_STRIDE = 16
SLC_BLOCK = 64
TOP_N = 8
WINDOW = 512
Q_BLOCK = 128
NEG_INF = -1e30
FORCE_BONUS = 1e4
Q_W = 512
KV_W = 256
NSA_GATE_W = 24
N_EXPERT_GROUPS = 4
EXPERTS_PER_GROUP = 4
N_EXPERTS = 16
D_FF_EXPERT = 256
PLE_DIM = 256
RMS_EPS = 1e-6
PAGE_SIZE = 128

LANES = 128
SUBLANES = 8
N_STATE = N_SSM_GROUPS * SSM_STATE
MIB = 2 ** 20


def _cparams(sem, vmem_mib):
    return pltpu.CompilerParams(dimension_semantics=sem, vmem_limit_bytes=vmem_mib * MIB)


def _full(shape):
    nd = len(shape)
    return pl.BlockSpec(shape, lambda *_: (0,) * nd)


def _prompt_layout(nseq, t, tm):
    nb = t // tm
    return {
        "grid": (nseq, nb), "tm": tm,
        "a": lambda w: ((nseq * t, w), pl.BlockSpec((tm, w), lambda s, b: (s * nb + b, 0))),
        "b": lambda w: ((t, nseq * w), pl.BlockSpec((tm, w), lambda s, b: (b, s))),
    }


def _sample_layout(nseq, t):
    return {
        "grid": (1, t), "tm": nseq,
        "a": lambda w: ((nseq, t * w), pl.BlockSpec((nseq, w), lambda s, b: (0, b))),
        "b": lambda w: ((t * nseq, w), pl.BlockSpec((nseq, w), lambda s, b: (b, 0))),
    }


TK_SLC = 512
TK_WIN = 128


C_U, C_Q, C_KVC, C_KVS, C_KVW = 0, 512, 1024, 1280, 1536
N_MAIN = 1792


def _dot_t(a, b):
    return lax.dot_general(a, b, (((1,), (1,)), ((), ())), preferred_element_type=F32)


def _inproj_prompt_body(x_ref, g_ref, wa_ref, wat_ref, wgnt_ref, wgab_ref,
                        u_ref, kvc_ref, ksb_ref, kwb_ref, ga_ref, gb_ref,
                        qt_ref, kvct_ref, kvst_ref, kvwt_ref, gnt_ref, vst_ref, vwt_ref):
    x = x_ref[...]
    inv = lax.rsqrt(jnp.mean(x * x, axis=-1, keepdims=True) + RMS_EPS)
    h = (x * inv * g_ref[...]).astype(BF16)
    tm = h.shape[0]

    def mm(w):
        return jnp.dot(h, w, preferred_element_type=F32)

    u_ref[...] = mm(wa_ref[:, C_U:C_U + SSM_WIDTH])
    kvc_ref[...] = mm(wa_ref[:, C_KVC:C_KVC + KV_W])
    ksb_ref[...] = mm(wa_ref[:, C_KVS:C_KVS + LANES]).astype(BF16)
    kwb_ref[...] = mm(wa_ref[:, C_KVW:C_KVW + LANES]).astype(BF16)
    ga_ref[...] = jax.nn.sigmoid(mm(wgab_ref[:, 0:D_MODEL]))
    gb_ref[...] = jax.nn.sigmoid(mm(wgab_ref[:, D_MODEL:2 * D_MODEL]))
    qt_ref[0] = (_dot_t(wat_ref[C_Q:C_Q + Q_W, :], h) * (HEAD_DIM ** -0.5)).astype(BF16)
    kvct_ref[0] = _dot_t(wat_ref[C_KVC:C_KVC + KV_W, :], h)
    kvst = _dot_t(wat_ref[C_KVS:C_KVS + KV_W, :], h)
    kvst_ref[0] = kvst
    kvwt = _dot_t(wat_ref[C_KVW:C_KVW + KV_W, :], h)
    kvwt_ref[0] = kvwt
    gnt_ref[0] = jax.nn.sigmoid(_dot_t(wgnt_ref[...], h))
    for c in range(tm // TK_WIN):
        vst_ref[0, c] = kvst[LANES:2 * LANES, c * TK_WIN:(c + 1) * TK_WIN].astype(BF16)
        vwt_ref[0, c] = kvwt[LANES:2 * LANES, c * TK_WIN:(c + 1) * TK_WIN].astype(BF16)


def _inproj_prompt(x2d, lay, g, w):
    tm = lay["tm"]
    nseq, nb = lay["grid"]
    t = nb * tm
    out_shapes, out_specs, names = [], [], []

    def add(name, shape_spec, dt):
        names.append(name)
        out_shapes.append(jax.ShapeDtypeStruct(shape_spec[0], dt))
        out_specs.append(shape_spec[1])

    def tr(rows):
        return (nseq, rows, t), pl.BlockSpec((1, rows, tm), lambda s, b: (s, 0, b))

    add("u", lay["b"](SSM_WIDTH), F32)
    add("kvc", lay["a"](KV_W), F32)
    add("ksb", lay["a"](LANES), BF16)
    add("kwb", lay["a"](LANES), BF16)
    add("ga", lay["a"](D_MODEL), F32)
    add("gb", lay["a"](D_MODEL), F32)
    add("qt", tr(Q_W), BF16)
    add("kvct", tr(KV_W), F32)
    add("kvst", tr(KV_W), F32)
    add("kvwt", tr(KV_W), F32)
    add("gnt", tr(w["wgnt"].shape[0]), F32)
    for name, tk in (("vst", TK_WIN), ("vwt", TK_WIN)):
        add(name, ((nseq, t // tk, LANES, tk), pl.BlockSpec((1, tm // tk, LANES, tk), lambda s, b: (s, b, 0, 0))), BF16)
    x_shape, x_spec = lay["a"](D_MODEL)
    ops = [g, w["wa"], w["wat"], w["wgnt"], w["wgab"]]
    outs = pl.pallas_call(
        _inproj_prompt_body,
        grid=lay["grid"],
        in_specs=[x_spec] + [_full(o.shape) for o in ops],
        out_specs=out_specs,
        out_shape=out_shapes,
        compiler_params=_cparams(("arbitrary",) * 2, 56),
        name="inproj_prompt",
    )(x2d.reshape(x_shape), *ops)
    return dict(zip(names, outs))


def _inproj_sample_body(x_ref, g_ref, wa_ref, wat_ref, wgn_ref, wgab_ref,
                        u_ref, q_ref, kvs_ref, kvw_ref, gn_ref, ga_ref, gb_ref, kvct_ref, kvst_ref, kvwt_ref):
    x = x_ref[...]
    inv = lax.rsqrt(jnp.mean(x * x, axis=-1, keepdims=True) + RMS_EPS)
    h = (x * inv * g_ref[...]).astype(BF16)

    def mm(w):
        return jnp.dot(h, w, preferred_element_type=F32)

    u_ref[...] = mm(wa_ref[:, C_U:C_U + SSM_WIDTH])
    q_ref[...] = mm(wa_ref[:, C_Q:C_Q + Q_W]) * (HEAD_DIM ** -0.5)
    kvs_ref[...] = mm(wa_ref[:, C_KVS:C_KVS + KV_W])
    kvw_ref[...] = mm(wa_ref[:, C_KVW:C_KVW + KV_W])
    gn_ref[...] = jax.nn.sigmoid(mm(wgn_ref[...]))
    ga_ref[...] = jax.nn.sigmoid(mm(wgab_ref[:, 0:D_MODEL]))
    gb_ref[...] = jax.nn.sigmoid(mm(wgab_ref[:, D_MODEL:2 * D_MODEL]))
    kvct_ref[0] = _dot_t(wat_ref[C_KVC:C_KVC + KV_W, :], h)
    kvst_ref[0] = _dot_t(wat_ref[C_KVS:C_KVS + KV_W, :], h)
    kvwt_ref[0] = _dot_t(wat_ref[C_KVW:C_KVW + KV_W, :], h)


def _inproj_sample(x2d, lay, g, w):
    nseq = lay["tm"]
    ts = lay["grid"][1]
    names = ["u", "q", "kvs", "kvw", "gn", "ga", "gb"]
    widths = [SSM_WIDTH, Q_W, KV_W, KV_W, LANES, D_MODEL, D_MODEL]
    out_shapes, out_specs = [], []
    for n, wd in zip(names, widths):
        shp, spec = lay["b" if n == "u" else "a"](wd)
        out_shapes.append(jax.ShapeDtypeStruct(shp, F32))
        out_specs.append(spec)
    for n in ("kvct", "kvst", "kvwt"):
        names.append(n)
        out_shapes.append(jax.ShapeDtypeStruct((ts, KV_W, nseq), F32))
        out_specs.append(pl.BlockSpec((1, KV_W, nseq), lambda s, b: (b, 0, 0)))
    x_shape, x_spec = lay["a"](D_MODEL)
    ops = [g, w["wa"], w["wat"], w["wgn"], w["wgab"]]
    outs = pl.pallas_call(
        _inproj_sample_body,
        grid=lay["grid"],
        in_specs=[x_spec] + [_full(o.shape) for o in ops],
        out_specs=out_specs,
        out_shape=out_shapes,
        compiler_params=_cparams(("arbitrary",) * 2, 56),
        name="inproj_sample",
    )(x2d.reshape(x_shape), *ops)
    return dict(zip(names, outs))


def _inproj_params(w_in0):
    wt = w_in0.T
    gn_rows = 2 * 16
    return {"wa": w_in0[:, :N_MAIN].astype(BF16),
            "wat": wt[:N_MAIN].astype(BF16),
            "wgn": jnp.pad(w_in0[:, N_MAIN:N_MAIN + NSA_GATE_W], ((0, 0), (0, LANES - NSA_GATE_W))).astype(BF16),
            "wgnt": jnp.pad(wt[N_MAIN:N_MAIN + NSA_GATE_W], ((0, gn_rows - NSA_GATE_W), (0, 0))).astype(BF16),
            "wgab": w_in0[:, N_MAIN + NSA_GATE_W:].astype(BF16)}


def _s5_prompt_body(u_ref, wb_ref, ar_ref, ai_ref, cw_ref, d_ref, wglu_ref, bglu_ref, wso_ref,
                    abr_ref, hlast_ref, lhs_ref, bu_ref, h8_ref, p_ref, us_ref, hstate_ref):
    c = pl.program_id(0)
    nseq = 4
    r4 = u_ref.shape[0]
    tc = r4 // nseq
    half = tc // 2

    @pl.when(c == 0)
    def _():
        hstate_ref[...] = jnp.zeros_like(hstate_ref)

    u = u_ref[...]
    row2 = lax.broadcasted_iota(jnp.int32, (r4, SSM_WIDTH), 0)
    lo2 = (row2 % SUBLANES) < nseq
    up = pltpu.roll(u, r4 - nseq, axis=0)
    dn = pltpu.roll(u, nseq, axis=0)
    swapped = jnp.where(lo2, up, dn)
    zero = jnp.zeros_like(u)
    ev_re = jnp.where(lo2, u, zero).astype(BF16).reshape(half, SUBLANES, SSM_WIDTH)
    ev_im = jnp.where(lo2, zero, swapped).astype(BF16).reshape(half, SUBLANES, SSM_WIDTH)
    od_re = jnp.where(lo2, swapped, zero).astype(BF16).reshape(half, SUBLANES, SSM_WIDTH)
    od_im = jnp.where(lo2, zero, u).astype(BF16).reshape(half, SUBLANES, SSM_WIDTH)
    for j in range(4):
        sl = slice(LANES * j, LANES * (j + 1))
        lhs_ref[:, 0:8, 256 * j:256 * j + LANES] = ev_re[:, :, sl]
        lhs_ref[:, 0:8, 256 * j + LANES:256 * (j + 1)] = ev_im[:, :, sl]
        lhs_ref[:, 8:16, 256 * j:256 * j + LANES] = od_re[:, :, sl]
        lhs_ref[:, 8:16, 256 * j + LANES:256 * (j + 1)] = od_im[:, :, sl]
    for j in range(4):
        lhs = lhs_ref[:, :, 256 * j:256 * (j + 1)].reshape(tc * SUBLANES, 256)
        bu_ref[:, 512 * j:512 * (j + 1)] = jnp.dot(lhs, wb_ref[j], preferred_element_type=F32)

    for lc in range(4):
        sl = slice(512 * lc, 512 * (lc + 1))
        ar = ar_ref[:, sl]
        ai = ai_ref[:, sl]

        def step(t, h, sl=sl, ar=ar, ai=ai):
            r0 = pl.multiple_of(t * SUBLANES, SUBLANES)
            h = ar * h + ai * pltpu.roll(h, nseq, axis=0) + bu_ref[pl.ds(r0, SUBLANES), sl]
            h8_ref[pl.ds(r0, SUBLANES), sl] = h
            return h

        hstate_ref[:, sl] = lax.fori_loop(0, tc, step, hstate_ref[:, sl], unroll=8)
    hlast_ref[...] = hstate_ref[...]

    for j in range(4):
        pj = jnp.dot(h8_ref[:, 512 * j:512 * (j + 1)].astype(BF16), cw_ref[j], preferred_element_type=F32)
        p_ref[2 * j] = pj[:, 0:LANES]
        p_ref[2 * j + 1] = pj[:, LANES:2 * LANES]
        us_ref[j] = u[:, LANES * j:LANES * (j + 1)]
    ys = []
    for s in range(nseq):
        parts = []
        for j in range(4):
            re = p_ref[2 * j, pl.ds(s, tc, stride=SUBLANES), :]
            im = p_ref[2 * j + 1, pl.ds(nseq + s, tc, stride=SUBLANES), :]
            us = us_ref[j, pl.ds(s, tc, stride=nseq), :]
            parts.append(re + im + d_ref[:, LANES * j:LANES * (j + 1)] * us)
        ys.append(jnp.concatenate(parts, axis=1))
    y = jnp.concatenate(ys, axis=0)
    zg = jax.nn.gelu(y)
    gate = jnp.dot(zg.astype(BF16), wglu_ref[...], preferred_element_type=F32) + bglu_ref[...]
    glu = (zg * jax.nn.sigmoid(gate)).astype(BF16)
    abr = jnp.dot(glu, wso_ref[...], preferred_element_type=F32)
    for s in range(nseq):
        abr_ref[s] = abr[s * tc:(s + 1) * tc]


def _s5_prompt(u_ts, sp, wglu, bglu, wso, t_total, tc):
    nseq = 4
    grid = (t_total // tc,)
    abr, hlast = pl.pallas_call(
        _s5_prompt_body,
        grid=grid,
        in_specs=[pl.BlockSpec((tc * nseq, SSM_WIDTH), lambda c: (c, 0)),
                  _full(sp["wb8"].shape), _full(sp["ar8"].shape), _full(sp["ai8"].shape), _full(sp["cw8"].shape),
                  _full(sp["d"].shape), _full(wglu.shape), _full(bglu.shape), _full(wso.shape)],
        out_specs=[pl.BlockSpec((nseq, tc, D_MODEL), lambda c: (0, c, 0)),
                   pl.BlockSpec((SUBLANES, N_STATE), lambda c: (0, 0))],
        out_shape=[jax.ShapeDtypeStruct((nseq, t_total, D_MODEL), F32),
                   jax.ShapeDtypeStruct((SUBLANES, N_STATE), F32)],
        scratch_shapes=[pltpu.VMEM((tc // 2, 2 * SUBLANES, 1024), BF16),
                        pltpu.VMEM((tc * SUBLANES, N_STATE), F32),
                        pltpu.VMEM((tc * SUBLANES, N_STATE), F32),
                        pltpu.VMEM((8, tc * SUBLANES, LANES), F32),
                        pltpu.VMEM((4, tc * nseq, LANES), F32),
                        pltpu.VMEM((SUBLANES, N_STATE), F32)],
        compiler_params=_cparams(("arbitrary",), 56),
        name="s5_prompt",
    )(u_ts, sp["wb8"], sp["ar8"], sp["ai8"], sp["cw8"], sp["d"], wglu, bglu, wso)
    return abr, hlast


def _s5_params(lam_re, lam_im, log_dt, b_re, b_im, c_re, c_im, d_skip):
    lam = lax.complex(lam_re.astype(F32), lam_im.astype(F32))
    dt = jnp.exp(log_dt.astype(F32))[:, None]
    a_bar = jnp.exp(lam * dt)
    b = lax.complex(b_re.astype(F32), b_im.astype(F32))
    b_bar = ((a_bar - 1.0) / lam)[..., None] * b
    eye8 = jnp.eye(8, dtype=F32)

    def bd_b(m):
        return jnp.einsum("ab,jbpc->jacbp", eye8, m.reshape(4, 8, SSM_STATE, SSM_GROUP)).reshape(4, 128, 512)

    def bd_c(m):
        return jnp.einsum("ab,jbcp->japbc", eye8, m.reshape(4, 8, SSM_GROUP, SSM_STATE)).reshape(4, 512, 128)

    wre, wim = bd_b(b_bar.real), bd_b(b_bar.imag)
    cre, cim = bd_c(c_re.astype(F32)), bd_c(c_im.astype(F32))
    ar = a_bar.real.reshape(1, N_STATE)
    ai = a_bar.imag.reshape(1, N_STATE)
    sign = jnp.concatenate([-jnp.ones((4, 1), F32), jnp.ones((4, 1), F32)], axis=0)
    return {
        "wb8": jnp.concatenate([wre, wim], axis=1).astype(BF16),
        "cw8": jnp.concatenate([cre, -cim], axis=2).astype(BF16),
        "ar8": jnp.broadcast_to(ar, (SUBLANES, N_STATE)),
        "ai8": sign * ai,
        "wre": wre.astype(BF16), "wim": wim.astype(BF16),
        "cre": cre.astype(BF16), "cim": cim.astype(BF16),
        "ar": ar, "ai": ai,
        "d": d_skip.astype(F32).reshape(1, SSM_WIDTH),
    }


def _cmp_params(cmp_pe, cmp_w1, cmp_w2):
    eye2 = jnp.eye(2, dtype=F32)
    w1s, pes = [], []
    for s in range(CMP_LEN // CMP_STRIDE):
        w = cmp_w1[:, s * CMP_STRIDE:(s + 1) * CMP_STRIDE].astype(F32)
        w1s.append(jnp.einsum("pk,qh,kide->ipqdkhe", eye2, eye2, w).reshape(CMP_STRIDE * KV_W, KV_W))
        pe = cmp_pe[:, s * CMP_STRIDE:(s + 1) * CMP_STRIDE].astype(F32)
        pes.append(jnp.broadcast_to(pe.transpose(1, 0, 2)[:, :, None, :],
                                    (CMP_STRIDE, 2, N_KV_HEADS, HEAD_DIM)).reshape(1, CMP_STRIDE * KV_W))
    w2 = jnp.einsum("pk,qh,kef->pqekhf", eye2, eye2, cmp_w2.astype(F32)).reshape(KV_W, KV_W)
    nhalf = CMP_LEN // CMP_STRIDE
    w1r = cmp_w1.astype(F32).reshape(2, nhalf, CMP_STRIDE, HEAD_DIM, HEAD_DIM)
    wk = jnp.einsum("kside,ph->kipdshe", w1r, eye2).reshape(2, CMP_STRIDE * LANES, nhalf * LANES)
    bk = jnp.einsum("kld,klde->ke", cmp_pe.astype(F32), cmp_w1.astype(F32), precision=lax.Precision.HIGHEST)
    w2k = jnp.einsum("kef,ph->kpehf", cmp_w2.astype(F32), eye2).reshape(2, LANES, LANES)
    return {"w1": jnp.stack(w1s).astype(BF16),
            "pe": jnp.concatenate(pes, axis=0),
            "w2": w2.astype(BF16),
            "wk": wk.astype(BF16), "bk": jnp.tile(bk, (1, N_KV_HEADS)), "w2k": w2k.astype(BF16)}


def _compress_rows(x, pe_ref, w1_ref, w2_ref):
    nch = x.shape[0]
    p0 = jnp.dot((x + pe_ref[0:1, :]).astype(BF16), w1_ref[0], preferred_element_type=F32)
    p1 = jnp.dot((x + pe_ref[1:2, :]).astype(BF16), w1_ref[1], preferred_element_type=F32)
    pre = p0 + pltpu.roll(p1, nch - 1, axis=0)
    return jnp.dot(jax.nn.gelu(pre).astype(BF16), w2_ref[...], preferred_element_type=F32)


def _compress_prompt_body(x_ref, pe_ref, w1_ref, w2_ref, ck_ref, cvt_ref):
    out = _compress_rows(x_ref[0], pe_ref, w1_ref, w2_ref)
    ck_ref[0] = out[:, 0:LANES].astype(BF16)
    cvt_ref[0] = out[:, LANES:2 * LANES].T.astype(BF16)


def _compress_prompt(kvc, cp, nseq, t):
    nch = t // CMP_STRIDE
    x = kvc.reshape(nseq, nch, CMP_STRIDE * KV_W)
    return pl.pallas_call(
        _compress_prompt_body,
        grid=(nseq,),
        in_specs=[pl.BlockSpec((1, nch, CMP_STRIDE * KV_W), lambda n: (n, 0, 0)),
                  _full(cp["pe"].shape), _full(cp["w1"].shape), _full(cp["w2"].shape)],
        out_specs=[pl.BlockSpec((1, nch, LANES), lambda n: (n, 0, 0)),
                   pl.BlockSpec((1, LANES, nch), lambda n: (n, 0, 0))],
        out_shape=[jax.ShapeDtypeStruct((nseq, nch, LANES), BF16),
                   jax.ShapeDtypeStruct((nseq, LANES, nch), BF16)],
        compiler_params=_cparams(("arbitrary",), 48),
        name="compress_prompt",
    )(x, cp["pe"], cp["w1"], cp["w2"])


def _overlap_t(n_cmp_pad, n_slc_pad):
    j = np.arange(n_cmp_pad)[None, :]
    s = np.arange(n_slc_pad)[:, None]
    ov = (j * CMP_STRIDE <= s * SLC_BLOCK + SLC_BLOCK - 1) & (j * CMP_STRIDE + CMP_LEN - 1 >= s * SLC_BLOCK)
    return jnp.asarray(ov, dtype=BF16)


def _softmax_cols(s, valid):
    sm = jnp.where(valid, s, NEG_INF)
    mx = jnp.max(sm, axis=0, keepdims=True)
    e = jnp.where(valid, jnp.exp(sm - mx), 0.0)
    l = jnp.sum(e, axis=0, keepdims=True)
    return e * (1.0 / jnp.maximum(l, 1e-30))


def _select_blocks(imp, blk, pos, nblk, axis=0):
    cur = pos // SLC_BLOCK
    forced = (blk == 0) | (blk == cur) | (blk == cur - 1)
    v = jnp.where(forced, imp + FORCE_BONUS, imp)
    v = jnp.where(blk * SLC_BLOCK <= pos, v, NEG_INF)
    v = jnp.where(blk < nblk, v, -3e38)
    blk_f = blk.astype(F32)
    neg = jnp.full(imp.shape, NEG_INF, F32)
    for _ in range(min(TOP_N, nblk)):
        mx = jnp.max(v, axis=axis, keepdims=True)
        first = jnp.min(jnp.where(v == mx, blk_f, float(imp.shape[axis])), axis=axis, keepdims=True)
        pick = blk_f == first
        neg = jnp.where(pick, 0.0, neg)
        v = jnp.where(pick, -3e38, v)
    return neg


CB = 2 * LANES


def _attn_prompt_body(q_ref, gn_ref, ck_ref, cvt_ref, ks_ref, vst_ref, kw_ref, vwt_ref, ovt_ref,
                      o_ref, kaug_ref, kwaug_ref, qaug_ref, acc_ref):
    i = pl.program_id(1)
    t = ks_ref.shape[1]
    nch = ck_ref.shape[1]
    nslc = t // SLC_BLOCK
    qb = Q_BLOCK
    ncol = N_Q_HEADS * qb
    ncb = ncol // CB
    q0 = i * qb
    one_row = 2 * LANES - HEAD_DIM

    @pl.when(i == 0)
    def _():
        kaug_ref[:, 0:LANES] = ks_ref[0]
        blk = lax.broadcasted_iota(jnp.int32, (t, LANES), 0) // SLC_BLOCK
        col = lax.broadcasted_iota(jnp.int32, (t, LANES), 1)
        kaug_ref[:, LANES:2 * LANES] = jnp.where(blk == col, 1.0, 0.0).astype(BF16)
        padcol = lax.broadcasted_iota(jnp.int32, (WINDOW, 2 * LANES), 1)
        kwaug_ref[0:WINDOW, :] = jnp.where(padcol == one_row, NEG_INF, 0.0).astype(BF16)
        kwaug_ref[WINDOW:WINDOW + t, 0:LANES] = kw_ref[0]
        kwaug_ref[WINDOW:WINDOW + t, LANES:2 * LANES] = jnp.zeros((t, LANES), BF16)

    zeros64 = jnp.zeros((HEAD_DIM, qb), BF16)
    for j in range(N_Q_HEADS):
        dst = j // GQA
        qaug_ref[HEAD_DIM * dst:HEAD_DIM * (dst + 1), qb * j:qb * (j + 1)] = q_ref[0, HEAD_DIM * j:HEAD_DIM * (j + 1), :]
        qaug_ref[HEAD_DIM * (1 - dst):HEAD_DIM * (2 - dst), qb * j:qb * (j + 1)] = zeros64
    tail_row = lax.broadcasted_iota(jnp.int32, (HEAD_DIM, ncol), 0)
    qaug_ref[one_row:2 * LANES, :] = jnp.where(tail_row == 0, 1.0, 0.0).astype(BF16)

    pos_c = q0 + (lax.broadcasted_iota(jnp.int32, (nch, CB), 1) & (qb - 1))
    cvalid = lax.broadcasted_iota(jnp.int32, (nch, CB), 0) * CMP_STRIDE + (CMP_LEN - 1) <= pos_c
    ocs, imps = [], []
    for cb in range(ncb):
        sc = jnp.dot(ck_ref[0], qaug_ref[0:LANES, CB * cb:CB * (cb + 1)], preferred_element_type=F32)
        pc = _softmax_cols(sc, cvalid).astype(BF16)
        ocs.append(jnp.dot(cvt_ref[0], pc, preferred_element_type=F32))
        imps.append(jnp.dot(ovt_ref[...], pc, preferred_element_type=F32))
    oc = jnp.concatenate(ocs, axis=1)
    imp = jnp.concatenate(imps, axis=1)
    blk = lax.broadcasted_iota(jnp.int32, (nslc, qb), 0)
    pos_q = q0 + lax.broadcasted_iota(jnp.int32, (nslc, qb), 1)
    for h in range(N_KV_HEADS):
        v = imp[0:nslc, qb * GQA * h:qb * GQA * h + qb]
        for g in range(1, GQA):
            v = v + imp[0:nslc, qb * (GQA * h + g):qb * (GQA * h + g + 1)]
        neg = _select_blocks(v, blk, pos_q, nslc).astype(BF16)
        for g in range(GQA):
            j = GQA * h + g
            qaug_ref[LANES:LANES + nslc, qb * j:qb * (j + 1)] = neg
    if nslc < HEAD_DIM:
        qaug_ref[LANES + nslc:LANES + HEAD_DIM, :] = jnp.zeros((HEAD_DIM - nslc, ncol), BF16)

    brow = lax.broadcasted_iota(jnp.int32, (qb, CB), 0)
    bcol = lax.broadcasted_iota(jnp.int32, (qb, CB), 1) & (qb - 1)
    tri_lo = jnp.where(brow <= bcol, 0.0, NEG_INF)
    tri_up = jnp.where(brow > bcol, 0.0, NEG_INF)

    acc_ref[...] = jnp.zeros_like(acc_ref)

    def sel_tile(k0, nk, vt, carry, bias):
        m, l = carry
        ka = kaug_ref[pl.ds(k0, nk), :]
        css = [slice(CB * cb, CB * (cb + 1)) for cb in range(ncb)]
        ss = [jnp.dot(ka, qaug_ref[:, cs], preferred_element_type=F32) for cs in css]
        ms, ls, ps, alphas = [], [], [], []
        for cs, s in zip(css, ss):
            if bias is not None:
                s = s + bias
            mn = jnp.maximum(m[:, cs], jnp.max(s, axis=0, keepdims=True))
            alpha = jnp.exp(m[:, cs] - mn)
            p = jnp.exp(s - mn)
            ms.append(mn)
            ls.append(alpha * l[:, cs] + jnp.sum(p, axis=0, keepdims=True))
            ps.append(p.astype(BF16))
            alphas.append(alpha)
        pvs = [jnp.dot(vt, p, preferred_element_type=F32) for p in ps]
        for cs, alpha, pv in zip(css, alphas, pvs):
            acc_ref[:, cs] = alpha * acc_ref[:, cs] + pv
        return jnp.concatenate(ms, axis=1), jnp.concatenate(ls, axis=1)

    per_tile = TK_SLC // TK_WIN

    def big_tile(kt, carry):
        vt = jnp.concatenate([vst_ref[0, kt * per_tile + j] for j in range(per_tile)], axis=1)
        return sel_tile(pl.multiple_of(kt * TK_SLC, TK_SLC), TK_SLC, vt, carry, None)

    def small_tile(kb, carry):
        return sel_tile(pl.multiple_of(kb * TK_WIN, TK_WIN), TK_WIN, vst_ref[0, kb], carry, None)

    carry = (jnp.full((1, ncol), NEG_INF, F32), jnp.zeros((1, ncol), F32))
    nbig = i // per_tile
    carry = lax.fori_loop(0, nbig, big_tile, carry)
    carry = lax.fori_loop(nbig * per_tile, i, small_tile, carry)
    _, l = sel_tile(pl.multiple_of(q0, qb), qb, vst_ref[0, i], carry, tri_lo)
    osel = acc_ref[...] * (1.0 / l)

    npiece = (WINDOW + qb) // TK_WIN
    kws = [kwaug_ref[pl.ds(pl.multiple_of(q0 + w * TK_WIN, TK_WIN), TK_WIN), :] for w in range(npiece)]
    vwt = jnp.concatenate([vwt_ref[0, jnp.maximum(i + w - (npiece - 1), 0)] for w in range(npiece)], axis=1)
    ows = []
    for cb in range(ncb):
        qa = qaug_ref[:, CB * cb:CB * (cb + 1)]
        sw = [jnp.dot(kws[w], qa, preferred_element_type=F32) for w in range(npiece)]
        sw[0] = sw[0] + tri_up
        sw[-1] = sw[-1] + tri_lo
        s = jnp.concatenate(sw, axis=0)
        e = jnp.exp(s - jnp.max(s, axis=0, keepdims=True))
        pw = (e * (1.0 / jnp.sum(e, axis=0, keepdims=True))).astype(BF16)
        ows.append(jnp.dot(vwt, pw, preferred_element_type=F32))
    ow = jnp.concatenate(ows, axis=1)

    gt = gn_ref[0]
    for c in range(N_Q_HEADS // 2):
        rows = []
        for hh in range(2):
            j = 2 * c + hh
            rs = slice(HEAD_DIM * (j // GQA), HEAD_DIM * (j // GQA + 1))
            cs = slice(qb * j, qb * (j + 1))
            rows.append(gt[3 * j:3 * j + 1, :] * oc[rs, cs] + gt[3 * j + 1:3 * j + 2, :] * osel[rs, cs]
                        + gt[3 * j + 2:3 * j + 3, :] * ow[rs, cs])
        o_ref[:, LANES * c:LANES * (c + 1)] = jnp.concatenate(rows, axis=0).T.astype(o_ref.dtype)


def _attn_prompt(q, gn, ck, cvt, ksb, vst, kwb, vwt, nseq, t):
    nb = t // Q_BLOCK
    nch = t // CMP_STRIDE
    nslc = t // SLC_BLOCK
    ovt = _overlap_t(nch, max(nslc, SUBLANES))
    row = lambda n, i: (n * nb + i, 0)
    seq3 = lambda n, i: (n, 0, 0)
    seq4 = lambda n, i: (n, 0, 0, 0)
    col3 = lambda n, i: (n, 0, i)
    return pl.pallas_call(
        _attn_prompt_body,
        grid=(nseq, nb),
        in_specs=[pl.BlockSpec((1, Q_W, Q_BLOCK), col3), pl.BlockSpec((1, gn.shape[1], Q_BLOCK), col3),
                  pl.BlockSpec((1, nch, LANES), seq3), pl.BlockSpec((1, LANES, nch), seq3),
                  pl.BlockSpec((1, t, LANES), seq3), pl.BlockSpec((1, t // TK_WIN, LANES, TK_WIN), seq4),
                  pl.BlockSpec((1, t, LANES), seq3), pl.BlockSpec((1, t // TK_WIN, LANES, TK_WIN), seq4),
                  _full(ovt.shape)],
        out_specs=pl.BlockSpec((Q_BLOCK, Q_W), row),
        out_shape=jax.ShapeDtypeStruct((nseq * t, Q_W), BF16),
        scratch_shapes=[pltpu.VMEM((t, 2 * LANES), BF16),
                        pltpu.VMEM((WINDOW + t, 2 * LANES), BF16),
                        pltpu.VMEM((2 * LANES, N_Q_HEADS * Q_BLOCK), BF16),
                        pltpu.VMEM((LANES, N_Q_HEADS * Q_BLOCK), F32)],
        compiler_params=_cparams(("arbitrary", "arbitrary"), 56),
        name="attn_prompt",
    )(q, gn, ck, cvt, ksb.reshape(nseq, t, LANES), vst, kwb.reshape(nseq, t, LANES), vwt, ovt)


def _post_body(x_ref, abr_ref, on_ref, ga_ref, gb_ref, wno_ref, wo_ref, g2_ref, x1_ref, h2_ref):
    bbr = jnp.dot(on_ref[...].astype(BF16), wno_ref[...], preferred_element_type=F32)
    merged = ga_ref[...] * abr_ref[...] + gb_ref[...] * bbr
    x1 = x_ref[...] + jnp.dot(merged.astype(BF16), wo_ref[...], preferred_element_type=F32)
    x1_ref[...] = x1
    inv = lax.rsqrt(jnp.mean(x1 * x1, axis=-1, keepdims=True) + RMS_EPS)
    h2_ref[...] = (x1 * inv * g2_ref[...]).astype(BF16)


def _post(x2d, abr, abr_lay, onsa, ga, gb, wno, wo, g2, lay, out_lay):
    x_shape, x_spec = lay["a"](D_MODEL)
    abr_shape, abr_spec = lay[abr_lay](D_MODEL)
    on_shape, on_spec = lay["a"](Q_W)
    o_shape, o_spec = lay[out_lay](D_MODEL)
    return pl.pallas_call(
        _post_body,
        grid=lay["grid"],
        in_specs=[x_spec, abr_spec, on_spec, x_spec, x_spec, _full(wno.shape), _full(wo.shape), _full(g2.shape)],
        out_specs=[o_spec, o_spec],
        out_shape=[jax.ShapeDtypeStruct(o_shape, F32), jax.ShapeDtypeStruct(o_shape, BF16)],
        compiler_params=_cparams(("arbitrary",) * len(lay["grid"]), 48),
        name="post",
    )(x2d.reshape(x_shape), abr.reshape(abr_shape), onsa.reshape(on_shape), ga, gb, wno, wo, g2)


def _route(logits):
    lane = lax.broadcasted_iota(jnp.int32, logits.shape, 1).astype(F32)
    big = float(LANES)
    glog = jnp.where(lane < N_EXPERT_GROUPS, logits, -jnp.inf)
    gmax = jnp.max(glog, axis=1, keepdims=True)
    gsel = jnp.min(jnp.where(glog == gmax, lane, big), axis=1, keepdims=True)
    gw = 1.0 / jnp.sum(jnp.exp(glog - gmax), axis=1, keepdims=True)
    lo = N_EXPERT_GROUPS + EXPERTS_PER_GROUP * gsel
    el = jnp.where((lane >= lo) & (lane < lo + EXPERTS_PER_GROUP), logits, -jnp.inf)
    v1 = jnp.max(el, axis=1, keepdims=True)
    i1 = jnp.min(jnp.where(el == v1, lane, big), axis=1, keepdims=True)
    el2 = jnp.where(lane == i1, -jnp.inf, el)
    v2 = jnp.max(el2, axis=1, keepdims=True)
    i2 = jnp.min(jnp.where(el2 == v2, lane, big), axis=1, keepdims=True)
    e2 = jnp.exp(v2 - v1)
    w1 = gw / (1.0 + e2)
    return jnp.where(lane == i1, w1, 0.0) + jnp.where(lane == i2, w1 * e2, 0.0)


def _moe_body(x1_ref, h2_ref, p_ref, wr_ref, br_ref, wg_ref, wu_ref, wd_ref, wpg_ref, wp_ref, gf_ref,
              y_ref, acc_ref, comb_ref, *, tsplit):
    g = pl.program_id(1)
    h2 = h2_ref[...]

    @pl.when(g == 0)
    def _():
        logits = jnp.dot(h2, wr_ref[...], preferred_element_type=F32) + br_ref[...]
        comb_ref[...] = _route(logits)
        acc_ref[...] = jnp.zeros_like(acc_ref)

    comb = comb_ref[...]
    lane = lax.broadcasted_iota(jnp.int32, comb.shape, 1)
    acc = acc_ref[...]
    for k in range(EXPERTS_PER_GROUP):
        e_lane = N_EXPERT_GROUPS + EXPERTS_PER_GROUP * g + k
        ce = jnp.sum(jnp.where(lane == e_lane, comb, 0.0), axis=1, keepdims=True)
        a = jnp.dot(h2, wg_ref[k], preferred_element_type=F32)
        b = jnp.dot(h2, wu_ref[k], preferred_element_type=F32)
        act = (jax.nn.silu(a) * b * ce).astype(BF16)
        acc = acc + jnp.dot(act, wd_ref[k], preferred_element_type=F32)
    acc_ref[...] = acc

    @pl.when(g == N_EXPERT_GROUPS - 1)
    def _():
        x2 = x1_ref[...] + acc_ref[...]
        rows = x2.shape[0] // tsplit
        if tsplit == 1:
            p = p_ref[...]
        else:
            p = jnp.concatenate([p_ref[:, PLE_DIM * t:PLE_DIM * (t + 1)] for t in range(tsplit)], axis=0)
        gate = jax.nn.sigmoid(jnp.dot(x2.astype(BF16), wpg_ref[...], preferred_element_type=F32))
        x3 = x2 + gate * jnp.dot(p.astype(BF16), wp_ref[...], preferred_element_type=F32)
        inv = lax.rsqrt(jnp.mean(x3 * x3, axis=-1, keepdims=True) + RMS_EPS)
        y = x3 * inv * gf_ref[...]
        if tsplit == 1:
            y_ref[...] = y
        else:
            for t in range(tsplit):
                y_ref[:, D_MODEL * t:D_MODEL * (t + 1)] = y[rows * t:rows * (t + 1)]


def _moe(x1, h2, p, mp, tm, tsplit):
    rows = x1.shape[0]
    nrb = rows // tm
    rb = lambda r, g: (r, 0)
    grp = lambda r, g: (g, 0, 0)
    if tsplit == 1:
        p_spec = pl.BlockSpec((tm, PLE_DIM), rb)
        y_spec = pl.BlockSpec((tm, D_MODEL), rb)
        y_shape = (rows, D_MODEL)
    else:
        assert nrb == 1
        p_spec = _full(p.shape)
        y_shape = (rows // tsplit, tsplit * D_MODEL)
        y_spec = _full(y_shape)
    return pl.pallas_call(
        functools.partial(_moe_body, tsplit=tsplit),
        grid=(nrb, N_EXPERT_GROUPS),
        in_specs=[pl.BlockSpec((tm, D_MODEL), rb), pl.BlockSpec((tm, D_MODEL), rb), p_spec,
                  _full(mp["wr"].shape), _full(mp["br"].shape),
                  pl.BlockSpec((EXPERTS_PER_GROUP, D_MODEL, D_FF_EXPERT), grp),
                  pl.BlockSpec((EXPERTS_PER_GROUP, D_MODEL, D_FF_EXPERT), grp),
                  pl.BlockSpec((EXPERTS_PER_GROUP, D_FF_EXPERT, D_MODEL), grp),
                  _full(mp["wpg"].shape), _full(mp["wp"].shape), _full(mp["gf"].shape)],
        out_specs=y_spec,
        out_shape=jax.ShapeDtypeStruct(y_shape, F32),
        scratch_shapes=[pltpu.VMEM((tm, D_MODEL), F32), pltpu.VMEM((tm, LANES), F32)],
        compiler_params=_cparams(("arbitrary", "arbitrary"), 56),
        name="moe_ple",
    )(x1, h2, p, mp["wr"], mp["br"], mp["wg"], mp["wu"], mp["wd"], mp["wpg"], mp["wp"], mp["gf"])


def _s5_sample_body(u_ref, h0re_ref, h0im_ref, wre_ref, wim_ref, ar_ref, ai_ref, cre_ref, cim_ref, d_ref,
                    wglu_ref, bglu_ref, wso_ref, abr_ref, hre_out_ref, him_out_ref,
                    bure_ref, buim_ref, hre_ref, him_ref, *, nseq, nstep):
    u = u_ref[...]
    ub = u.astype(BF16)
    for j in range(4):
        lhs = ub[:, LANES * j:LANES * (j + 1)]
        bure_ref[:, 512 * j:512 * (j + 1)] = jnp.dot(lhs, wre_ref[j], preferred_element_type=F32)
        buim_ref[:, 512 * j:512 * (j + 1)] = jnp.dot(lhs, wim_ref[j], preferred_element_type=F32)
    for lc in range(4):
        sl = slice(512 * lc, 512 * (lc + 1))
        ar = jnp.broadcast_to(ar_ref[:, sl], (SUBLANES, 512))
        ai = jnp.broadcast_to(ai_ref[:, sl], (SUBLANES, 512))

        def body(rc, carry, sl=sl, ar=ar, ai=ai):
            r0 = pl.multiple_of(rc * SUBLANES, SUBLANES)
            hr = h0re_ref[pl.ds(r0, SUBLANES), sl]
            hi = h0im_ref[pl.ds(r0, SUBLANES), sl]
            for t in range(nstep):
                rr = pl.multiple_of(t * nseq + rc * SUBLANES, SUBLANES)
                hr, hi = (ar * hr - ai * hi + bure_ref[pl.ds(rr, SUBLANES), sl],
                          ar * hi + ai * hr + buim_ref[pl.ds(rr, SUBLANES), sl])
                hre_ref[pl.ds(rr, SUBLANES), sl] = hr
                him_ref[pl.ds(rr, SUBLANES), sl] = hi
            hre_out_ref[pl.ds(r0, SUBLANES), sl] = hr
            him_out_ref[pl.ds(r0, SUBLANES), sl] = hi
            return carry

        lax.fori_loop(0, nseq // SUBLANES, body, 0)
    parts = []
    for j in range(4):
        sl = slice(512 * j, 512 * (j + 1))
        parts.append(jnp.dot(hre_ref[:, sl].astype(BF16), cre_ref[j], preferred_element_type=F32)
                     - jnp.dot(him_ref[:, sl].astype(BF16), cim_ref[j], preferred_element_type=F32))
    y = jnp.concatenate(parts, axis=1) + d_ref[...] * u
    zg = jax.nn.gelu(y)
    gate = jnp.dot(zg.astype(BF16), wglu_ref[...], preferred_element_type=F32) + bglu_ref[...]
    glu = (zg * jax.nn.sigmoid(gate)).astype(BF16)
    abr_ref[...] = jnp.dot(glu, wso_ref[...], preferred_element_type=F32)


def _s5_sample(u_ts, h0re, h0im, sp, wglu, bglu, wso, nseq, nstep):
    rows = nseq * nstep
    ops = [u_ts, h0re, h0im, sp["wre"], sp["wim"], sp["ar"], sp["ai"], sp["cre"], sp["cim"], sp["d"], wglu, bglu, wso]
    return pl.pallas_call(
        functools.partial(_s5_sample_body, nseq=nseq, nstep=nstep),
        grid=(1,),
        in_specs=[_full(o.shape) for o in ops],
        out_specs=[_full((rows, D_MODEL)), _full((nseq, N_STATE)), _full((nseq, N_STATE))],
        out_shape=[jax.ShapeDtypeStruct((rows, D_MODEL), F32),
                   jax.ShapeDtypeStruct((nseq, N_STATE), F32), jax.ShapeDtypeStruct((nseq, N_STATE), F32)],
        scratch_shapes=[pltpu.VMEM((rows, N_STATE), F32) for _ in range(4)],
        compiler_params=_cparams(("arbitrary",), 56),
        name="s5_sample",
    )(*ops)


def _softmax_rows(s, valid):
    sm = jnp.where(valid, s, NEG_INF)
    mx = jnp.max(sm, axis=1, keepdims=True)
    e = jnp.where(valid, jnp.exp(sm - mx), 0.0)
    l = jnp.sum(e, axis=1, keepdims=True)
    return e * (1.0 / jnp.maximum(l, 1e-30))


SAMPLE_SEQS_PER_STEP = 4
CMP_PITCH = 24


def _attn_sample_body(pt_ref, q_ref, gn_ref, nks_ref, nkw_ref, wint_ref, wk_ref, bk_ref, w2k_ref, ov_ref, e_ref, *rest,
                      npage, past_len, nsub, tq):
    o_ref, nwint_ref, xrow_ref = rest[2 * nsub * npage:]
    nrow = N_Q_HEADS * tq
    nwin = wint_ref.shape[2]
    nslc = -(-(past_len + tq) // SLC_BLOCK)
    nch = past_len // CMP_STRIDE
    per_page = PAGE_SIZE // CMP_STRIDE

    cmp = []
    for kv in range(2):
        for s in range(nsub):
            for p in range(npage):
                rows = rest[s * npage + p][0, LANES * kv:LANES * (kv + 1), :].T
                for c in range(per_page):
                    r0 = CMP_PITCH * (per_page * p + c)
                    xrow_ref[s, r0:r0 + CMP_STRIDE, :] = rows[CMP_STRIDE * c:CMP_STRIDE * (c + 1)]
        x = jnp.concatenate(
            [jnp.concatenate([xrow_ref[s, pl.ds(i, nch, stride=CMP_PITCH), :] for i in range(CMP_STRIDE)], axis=1)
             for s in range(nsub)], axis=0)
        pp = jnp.dot(x.astype(BF16), wk_ref[kv], preferred_element_type=F32)
        pre = pp[:, 0:LANES] + pltpu.roll(pp[:, LANES:2 * LANES], nsub * nch - 1, axis=0) + bk_ref[kv:kv + 1, :]
        cmp.append(jnp.dot(jax.nn.gelu(pre).astype(BF16), w2k_ref[kv], preferred_element_type=F32).astype(BF16))
    cks = [cmp[0][nch * s:nch * (s + 1)] for s in range(nsub)]
    cvs = [cmp[1][nch * s:nch * (s + 1)] for s in range(nsub)]
    slc_pages = [rest[(nsub + s) * npage:(nsub + s + 1) * npage] for s in range(nsub)]
    seqs = range(nsub)
    rcat = lambda parts: jnp.concatenate(parts, axis=0)

    lane_w = lax.broadcasted_iota(jnp.int32, (KV_W, LANES), 1)
    lane8 = lax.broadcasted_iota(jnp.int32, (tq, LANES), 1)
    nks_l, nkw_l, wint_l, qs_l = [], [], [], []
    for s in seqs:
        rows_s = slice(tq * s, tq * (s + 1))
        nks_l.append(jnp.concatenate([nks_ref[rows_s, :], jnp.zeros((LANES - tq, KV_W), F32)], axis=0))
        nkw = jnp.concatenate([nkw_ref[rows_s, :], jnp.zeros((LANES - tq, KV_W), F32)], axis=0)
        nkw_l.append(nkw)
        wint = wint_ref[s]
        wint_l.append(wint)
        shifted = pltpu.roll(wint, nwin - tq, axis=1)
        new_t = pltpu.roll(nkw.T, LANES - tq, axis=1)
        nwint_ref[s, :, 0:nwin - LANES] = shifted[:, 0:nwin - LANES]
        nwint_ref[s, :, nwin - LANES:nwin] = jnp.where(lane_w >= LANES - tq, new_t, shifted[:, nwin - LANES:nwin])
        q = q_ref[rows_s, :]
        qrows = []
        for j in range(N_Q_HEADS):
            chunk = q[:, LANES * (j // 2):LANES * (j // 2 + 1)]
            dst = j // GQA
            if (j % 2) != dst:
                chunk = pltpu.roll(chunk, HEAD_DIM, axis=1)
            keep = (lane8 < HEAD_DIM) if dst == 0 else (lane8 >= HEAD_DIM)
            qrows.append(jnp.where(keep, chunk, 0.0))
        qs_l.append(jnp.concatenate(qrows, axis=0).astype(BF16))

    rtot = nsub * nrow
    seq_rows = [slice(nrow * s, nrow * (s + 1)) for s in seqs]
    pos = past_len + (lax.broadcasted_iota(jnp.int32, (rtot, LANES), 0) & (tq - 1))
    lane = lax.broadcasted_iota(jnp.int32, (rtot, LANES), 1)

    sc = rcat([_dot_t(qs_l[s], cks[s]) for s in seqs])
    pc = _softmax_rows(sc, lane * CMP_STRIDE + (CMP_LEN - 1) <= pos).astype(BF16)
    oc = rcat([jnp.dot(pc[seq_rows[s]], cvs[s], preferred_element_type=F32) for s in seqs])
    imp = jnp.dot(pc, ov_ref[...], preferred_element_type=F32)
    vs = []
    for s in seqs:
        for h in range(N_KV_HEADS):
            r0 = nrow * s + tq * GQA * h
            v = imp[r0:r0 + tq]
            for g in range(1, GQA):
                v = v + imp[r0 + tq * g:r0 + tq * (g + 1)]
            vs.append(v)
    nsel = len(vs) * tq
    vt = rcat(vs + [jnp.zeros((LANES - nsel, LANES), F32)]).T
    nblk_pad = -(-nslc // SUBLANES) * SUBLANES
    blk_t = lax.broadcasted_iota(jnp.int32, (nblk_pad, LANES), 0)
    pos_t = past_len + (lax.broadcasted_iota(jnp.int32, (nblk_pad, LANES), 1) & (tq - 1))
    neg_t = _select_blocks(vt[0:nblk_pad], blk_t, pos_t, nslc, axis=0)
    neg = rcat([neg_t, jnp.zeros((LANES - nblk_pad, LANES), F32)]).T
    negsel = rcat([neg[tq * (N_KV_HEADS * s + j // GQA):tq * (N_KV_HEADS * s + j // GQA + 1)]
                   for s in seqs for j in range(N_Q_HEADS)])
    negsel_b = negsel.astype(BF16)

    new_blk = past_len // SLC_BLOCK
    ss_l = []
    for s in seqs:
        qaug = jnp.concatenate([qs_l[s], negsel_b[seq_rows[s]]], axis=1)
        parts = []
        for p in range(0, npage, 2):
            kt = jnp.concatenate([slc_pages[s][p][0][0:LANES], slc_pages[s][p + 1][0][0:LANES]], axis=1).astype(BF16)
            et = jnp.concatenate([e_ref[p], e_ref[p + 1]], axis=1)
            parts.append(jnp.dot(qaug, jnp.concatenate([kt, et], axis=0), preferred_element_type=F32))
        parts.append(_dot_t(qs_l[s], nks_l[s][:, 0:LANES].astype(BF16)) + negsel[seq_rows[s], new_blk:new_blk + 1])
        ss_l.append(jnp.concatenate(parts, axis=1))
    ss = rcat(ss_l)
    nkeys = ss.shape[1]
    kpos = lax.broadcasted_iota(jnp.int32, (rtot, nkeys), 1)
    pos_k = past_len + (lax.broadcasted_iota(jnp.int32, (rtot, nkeys), 0) & (tq - 1))
    ps = _softmax_rows(ss, kpos <= pos_k).astype(BF16)
    osel_l = []
    for s in seqs:
        psq = ps[seq_rows[s]]
        o = jnp.dot(psq[:, past_len:nkeys], nks_l[s][:, LANES:2 * LANES].astype(BF16), preferred_element_type=F32)
        for p in range(0, npage, 2):
            vtp = jnp.concatenate([slc_pages[s][p][0][LANES:2 * LANES], slc_pages[s][p + 1][0][LANES:2 * LANES]],
                                  axis=1).astype(BF16)
            o = o + _dot_t(psq[:, PAGE_SIZE * p:PAGE_SIZE * (p + 2)], vtp)
        osel_l.append(o)
    osel = rcat(osel_l)

    sw = rcat([jnp.concatenate([jnp.dot(qs_l[s], wint_l[s][0:LANES].astype(BF16), preferred_element_type=F32),
                                _dot_t(qs_l[s], nkw_l[s][:, 0:LANES].astype(BF16))], axis=1) for s in seqs])
    nw = sw.shape[1]
    widx = lax.broadcasted_iota(jnp.int32, (rtot, nw), 1)
    pos_w = past_len + (lax.broadcasted_iota(jnp.int32, (rtot, nw), 0) & (tq - 1))
    dlt = pos_w - (past_len - nwin + widx)
    pw = _softmax_rows(sw, (dlt >= 0) & (dlt < WINDOW) & (widx < nwin + tq)).astype(BF16)
    ow = rcat([_dot_t(pw[seq_rows[s], 0:nwin], wint_l[s][LANES:2 * LANES].astype(BF16))
               + jnp.dot(pw[seq_rows[s], nwin:nw], nkw_l[s][:, LANES:2 * LANES].astype(BF16),
                         preferred_element_type=F32) for s in seqs])

    for s in seqs:
        rows_s = slice(tq * s, tq * (s + 1))
        gn = gn_ref[rows_s, :]
        for c in range(N_Q_HEADS // 2):
            halves = []
            for hh in range(2):
                j = 2 * c + hh
                rs = slice(nrow * s + tq * j, nrow * s + tq * (j + 1))
                oj = (gn[:, 3 * j:3 * j + 1] * oc[rs] + gn[:, 3 * j + 1:3 * j + 2] * osel[rs]
                      + gn[:, 3 * j + 2:3 * j + 3] * ow[rs])
                if (j // GQA) != hh:
                    oj = pltpu.roll(oj, HEAD_DIM, axis=1)
                halves.append(oj)
            o_ref[rows_s, LANES * c:LANES * (c + 1)] = jnp.where(lane8 < HEAD_DIM, halves[0], halves[1])


def _attn_sample(q, gn, nks, nkw, cache_cmp, cache_slc, cache_win, page_table, cp, nseq, tq, past_len):
    assert tq <= CMP_STRIDE and past_len % PAGE_SIZE == 0
    npage = past_len // PAGE_SIZE
    assert npage % 2 == 0 and PAGE_SIZE == LANES
    n_pool = cache_cmp.shape[0]
    nwin = cache_win.shape[1]
    chunks = past_len // CMP_STRIDE
    ov = _overlap_t(chunks, LANES).T
    key = np.arange(past_len).reshape(npage, 1, PAGE_SIZE)
    e = jnp.asarray(np.arange(LANES).reshape(1, LANES, 1) == key // SLC_BLOCK, dtype=BF16)
    to_t = lambda c: jnp.transpose(c, (0, 2, 3, 4, 1)).reshape(c.shape[0], KV_W, c.shape[1])
    cmp_t, slc_t, win_t = to_t(cache_cmp), to_t(cache_slc), to_t(cache_win)
    nsub = SAMPLE_SEQS_PER_STEP
    assert nseq % nsub == 0
    row = lambda n, pt: (n, 0)
    seq3 = lambda n, pt: (n, 0, 0)
    page = lambda s, p: (lambda n, pt: (pt[n * nsub + s, p], 0, 0))
    consts = [cp["wk"], cp["bk"], cp["w2k"], ov, e]
    in_specs = [pl.BlockSpec((nsub * tq, Q_W), row), pl.BlockSpec((nsub * tq, LANES), row),
                pl.BlockSpec((nsub * tq, KV_W), row), pl.BlockSpec((nsub * tq, KV_W), row),
                pl.BlockSpec((nsub, KV_W, nwin), seq3)]
    in_specs += [pl.BlockSpec(c.shape, (lambda nd: lambda n, pt: (0,) * nd)(c.ndim)) for c in consts]
    pages = [pl.BlockSpec((1, KV_W, PAGE_SIZE), page(s, p)) for s in range(nsub) for p in range(npage)]
    in_specs += pages * 2
    grid_spec = pltpu.PrefetchScalarGridSpec(
        num_scalar_prefetch=1,
        grid=(nseq // nsub,),
        in_specs=in_specs,
        out_specs=[pl.BlockSpec((nsub * tq, Q_W), row), pl.BlockSpec((nsub, KV_W, nwin), seq3)],
        scratch_shapes=[pltpu.VMEM((nsub, chunks * CMP_PITCH, LANES), F32)],
    )
    return pl.pallas_call(
        functools.partial(_attn_sample_body, npage=npage, past_len=past_len, nsub=nsub, tq=tq),
        grid_spec=grid_spec,
        out_shape=[jax.ShapeDtypeStruct((nseq * tq, Q_W), F32), jax.ShapeDtypeStruct((nseq, KV_W, nwin), F32)],
        compiler_params=_cparams(("arbitrary",), 56),
        name="attn_sample",
    )(page_table, q, gn, nks, nkw, win_t, *consts, *([cmp_t] * (nsub * npage)), *([slc_t] * (nsub * npage)))


def _moe_params(w_rg, b_rg, w_re, b_re, w_gate, w_up, w_down, w_ple, w_ple_gate, gf):
    pad = LANES - N_EXPERT_GROUPS - N_EXPERTS
    return {"wr": jnp.pad(jnp.concatenate([w_rg, w_re], axis=1), ((0, 0), (0, pad))).astype(BF16),
            "br": jnp.pad(jnp.concatenate([b_rg, b_re]), (0, pad)).astype(F32).reshape(1, LANES),
            "wg": w_gate.astype(BF16), "wu": w_up.astype(BF16), "wd": w_down.astype(BF16),
            "wpg": w_ple_gate.astype(BF16), "wp": w_ple.astype(BF16), "gf": gf.astype(F32).reshape(1, D_MODEL)}


TM_PROMPT = 512
TC_S5 = 128


def kernel(x_prompt, x_sample, p_prompt, p_sample, cache_cmp_kv, cache_slc_kv, cache_win_kv, state_ssm, page_table, norm1_g, w_in, ssm_lam_re, ssm_lam_im, ssm_log_dt, ssm_b_re, ssm_b_im, ssm_c_re, ssm_c_im, ssm_d, w_glu, b_glu, cmp_pe, cmp_w1, cmp_w2, w_ssm_out, w_nsa_out, w_o, norm2_g, w_route_group, b_route_group, w_route_expert, b_route_expert, w_exp_gate, w_exp_up, w_exp_down, w_ple, w_ple_gate, final_norm_g):
    assert w_in.shape[0] == 1, "one layer"
    l = 0
    nb, t = x_prompt.shape[:2]
    ns, ts = x_sample.shape[:2]
    past_len = page_table.shape[1] * PAGE_SIZE
    kvt = (2, N_KV_HEADS, HEAD_DIM)

    wi = _inproj_params(w_in[l])
    g1 = norm1_g[l].astype(F32).reshape(1, D_MODEL)
    g2 = norm2_g[l].astype(F32).reshape(1, D_MODEL)
    sp = _s5_params(ssm_lam_re[l], ssm_lam_im[l], ssm_log_dt[l], ssm_b_re[l], ssm_b_im[l], ssm_c_re[l], ssm_c_im[l],
                    ssm_d[l])
    cp = _cmp_params(cmp_pe[l], cmp_w1[l], cmp_w2[l])
    mp = _moe_params(w_route_group[l], b_route_group[l], w_route_expert[l], b_route_expert[l], w_exp_gate[l],
                     w_exp_up[l], w_exp_down[l], w_ple[l], w_ple_gate[l], final_norm_g)
    wglu = w_glu[l].astype(BF16)
    bglu = b_glu[l].astype(F32).reshape(1, SSM_WIDTH)
    wso = w_ssm_out[l].astype(BF16)
    wno = w_nsa_out[l].astype(BF16)
    wo = w_o[l].astype(BF16)

    lay = _prompt_layout(nb, t, TM_PROMPT)
    xp = x_prompt.reshape(nb * t, D_MODEL)
    r = _inproj_prompt(xp, lay, g1, wi)
    abr, hlast = _s5_prompt(r["u"].reshape(t * nb, SSM_WIDTH), sp, wglu, bglu, wso, t, TC_S5)
    ck, cvt = _compress_prompt(r["kvc"], cp, nb, t)
    onsa = _attn_prompt(r["qt"], r["gnt"], ck, cvt, r["ksb"], r["vst"], r["kwb"], r["vwt"], nb, t)
    x1, h2 = _post(xp, abr, "a", onsa, r["ga"], r["gb"], wno, wo, g2, lay, "a")
    y_prompt = _moe(x1, h2, p_prompt[l].reshape(nb * t, PLE_DIM), mp, TM_PROMPT, 1).reshape(nb, t, D_MODEL)
    keep = min(WINDOW, t)

    def rows_last(a):
        return jnp.transpose(a.reshape((a.shape[0],) + kvt + (a.shape[2],)), (0, 4, 1, 2, 3))[None]

    new_cmp_p = rows_last(r["kvct"])
    new_slc_p = rows_last(r["kvst"])
    new_win_p = rows_last(r["kvwt"][:, :, t - keep:])
    new_ssm_p = jnp.stack([hlast[0:nb], hlast[nb:2 * nb]], axis=-1).reshape(1, nb, N_SSM_GROUPS, SSM_STATE, 2)

    lays = _sample_layout(ns, ts)
    xs = x_sample.reshape(ns * ts, D_MODEL)
    rs = _inproj_sample(xs, lays, g1, wi)
    h0 = state_ssm[l].astype(F32).reshape(ns, N_STATE, 2)
    abr_s, hre, him = _s5_sample(rs["u"], h0[..., 0], h0[..., 1], sp, wglu, bglu, wso, ns, ts)
    onsa_s, new_win = _attn_sample(rs["q"].reshape(ns * ts, Q_W), rs["gn"].reshape(ns * ts, LANES),
                                   rs["kvs"].reshape(ns * ts, KV_W), rs["kvw"].reshape(ns * ts, KV_W),
                                   cache_cmp_kv[l], cache_slc_kv[l], cache_win_kv[l], page_table, cp, ns, ts, past_len)
    x1s, h2s = _post(xs, abr_s, "b", onsa_s, rs["ga"], rs["gb"], wno, wo, g2, lays, "b")
    y_sample = _moe(x1s, h2s, p_sample[l].reshape(ns, ts * PLE_DIM), mp, ns * ts, ts).reshape(ns, ts, D_MODEL)
    steps_first = lambda a: jnp.transpose(a.reshape((ts,) + kvt + (ns,)), (4, 0, 1, 2, 3))[None]
    new_cmp_s = steps_first(rs["kvct"])
    new_slc_s = steps_first(rs["kvst"])
    new_win_s = rows_last(new_win)
    new_ssm_s = jnp.stack([hre, him], axis=-1).reshape(1, ns, N_SSM_GROUPS, SSM_STATE, 2)
    return (y_prompt, y_sample, new_cmp_p, new_slc_p, new_win_p, new_ssm_p,
            new_cmp_s, new_slc_s, new_win_s, new_ssm_s)
```

```python
import functools
import math

import jax
import jax.numpy as jnp
import numpy as np
from jax import lax
from jax.experimental import pallas as pl
from jax.experimental.pallas import tpu as pltpu

F32 = jnp.float32
BF16 = jnp.bfloat16

D_MODEL = 1024
SSM_WIDTH = 512
SSM_GROUP = 16
N_SSM_GROUPS = 32
SSM_STATE = 64
HEAD_DIM = 64
N_Q_HEADS = 8
N_KV_HEADS = 2
GQA = 4
CMP_LEN = 32
CMP_STRIDE = 16
SLC_BLOCK = 64
TOP_N = 8
WINDOW = 512
Q_BLOCK = 256
NEG_INF = -1e30
FORCE_BONUS = 1e4
Q_W = 512
KV_W = 256
NSA_GATE_W = 24
N_EXPERT_GROUPS = 4
EXPERTS_PER_GROUP = 4
N_EXPERTS = 16
D_FF_EXPERT = 256
PLE_DIM = 256
RMS_EPS = 1e-6
PAGE_SIZE = 128

LANES = 128
SUBLANES = 8
N_STATE = N_SSM_GROUPS * SSM_STATE
MIB = 2 ** 20


def _cparams(sem, vmem_mib):
    return pltpu.CompilerParams(dimension_semantics=sem, vmem_limit_bytes=vmem_mib * MIB)


def _full(shape):
    nd = len(shape)
    return pl.BlockSpec(shape, lambda *_: (0,) * nd)


def _prompt_layout(nseq, t, tm):
    nb = t // tm
    return {
        "grid": (nseq, nb), "tm": tm,
        "a": lambda w: ((nseq * t, w), pl.BlockSpec((tm, w), lambda s, b: (s * nb + b, 0))),
        "b": lambda w: ((t, nseq * w), pl.BlockSpec((tm, w), lambda s, b: (b, s))),
    }


def _sample_layout(nseq, t):
    return {
        "grid": (1, t), "tm": nseq,
        "a": lambda w: ((nseq, t * w), pl.BlockSpec((nseq, w), lambda s, b: (0, b))),
        "b": lambda w: ((t * nseq, w), pl.BlockSpec((nseq, w), lambda s, b: (b, 0))),
    }


TK_SLC = 512
TK_WIN = 128


Q_SCALE = HEAD_DIM ** -0.5 * math.log2(math.e)
C_U, C_Q, C_KVC, C_KVS, C_KVW = 0, 512, 1024, 1280, 1536
N_MAIN = 1792


def _dot_t(a, b):
    return lax.dot_general(a, b, (((1,), (1,)), ((), ())), preferred_element_type=F32)


def _inproj_prompt_body(x_ref, g_ref, wa_ref, wat_ref, wgnt_ref, wgab_ref,
                        u_ref, kvc_ref, ksb_ref, kwb_ref, ga_ref, gb_ref,
                        qt_ref, kvct_ref, kvst_ref, kvwt_ref, gnt_ref, vst_ref, vwt_ref):
    x = x_ref[...]
    inv = lax.rsqrt(jnp.mean(x * x, axis=-1, keepdims=True) + RMS_EPS)
    h = (x * inv * g_ref[...]).astype(BF16)
    tm = h.shape[0]

    def mm(w):
        return jnp.dot(h, w, preferred_element_type=F32)

    u_ref[...] = mm(wa_ref[:, C_U:C_U + SSM_WIDTH])
    kvc_ref[...] = mm(wa_ref[:, C_KVC:C_KVC + KV_W])
    ksb_ref[...] = mm(wa_ref[:, C_KVS:C_KVS + LANES]).astype(BF16)
    kwb_ref[...] = mm(wa_ref[:, C_KVW:C_KVW + LANES]).astype(BF16)
    ga_ref[...] = jax.nn.sigmoid(mm(wgab_ref[:, 0:D_MODEL]))
    gb_ref[...] = jax.nn.sigmoid(mm(wgab_ref[:, D_MODEL:2 * D_MODEL]))
    qt_ref[0] = (_dot_t(wat_ref[C_Q:C_Q + Q_W, :], h) * Q_SCALE).astype(BF16)
    kvct_ref[0] = _dot_t(wat_ref[C_KVC:C_KVC + KV_W, :], h)
    kvst = _dot_t(wat_ref[C_KVS:C_KVS + KV_W, :], h)
    kvst_ref[0] = kvst
    kvwt = _dot_t(wat_ref[C_KVW:C_KVW + KV_W, :], h)
    kvwt_ref[0] = kvwt
    gnt_ref[0] = jax.nn.sigmoid(_dot_t(wgnt_ref[...], h))
    for c in range(tm // TK_WIN):
        vst_ref[0, c] = kvst[LANES:2 * LANES, c * TK_WIN:(c + 1) * TK_WIN].astype(BF16)
        vwt_ref[0, c] = kvwt[LANES:2 * LANES, c * TK_WIN:(c + 1) * TK_WIN].astype(BF16)


def _inproj_prompt(x2d, lay, g, w):
    tm = lay["tm"]
    nseq, nb = lay["grid"]
    t = nb * tm
    out_shapes, out_specs, names = [], [], []

    def add(name, shape_spec, dt):
        names.append(name)
        out_shapes.append(jax.ShapeDtypeStruct(shape_spec[0], dt))
        out_specs.append(shape_spec[1])

    def tr(rows):
        return (nseq, rows, t), pl.BlockSpec((1, rows, tm), lambda s, b: (s, 0, b))

    add("u", lay["b"](SSM_WIDTH), F32)
    add("kvc", lay["a"](KV_W), F32)
    add("ksb", lay["a"](LANES), BF16)
    add("kwb", lay["a"](LANES), BF16)
    add("ga", lay["a"](D_MODEL), F32)
    add("gb", lay["a"](D_MODEL), F32)
    add("qt", tr(Q_W), BF16)
    add("kvct", tr(KV_W), F32)
    add("kvst", tr(KV_W), F32)
    add("kvwt", tr(KV_W), F32)
    add("gnt", tr(w["wgnt"].shape[0]), F32)
    for name, tk in (("vst", TK_WIN), ("vwt", TK_WIN)):
        add(name, ((nseq, t // tk, LANES, tk), pl.BlockSpec((1, tm // tk, LANES, tk), lambda s, b: (s, b, 0, 0))), BF16)
    x_shape, x_spec = lay["a"](D_MODEL)
    ops = [g, w["wa"], w["wat"], w["wgnt"], w["wgab"]]
    outs = pl.pallas_call(
        _inproj_prompt_body,
        grid=lay["grid"],
        in_specs=[x_spec] + [_full(o.shape) for o in ops],
        out_specs=out_specs,
        out_shape=out_shapes,
        compiler_params=_cparams(("arbitrary",) * 2, 56),
        name="inproj_prompt",
    )(x2d.reshape(x_shape), *ops)
    return dict(zip(names, outs))


def _inproj_sample_body(x_ref, g_ref, wa_ref, wat_ref, wgn_ref, wgab_ref,
                        u_ref, q_ref, kvs_ref, kvw_ref, gn_ref, ga_ref, gb_ref, kvct_ref, kvst_ref, kvwt_ref):
    x = x_ref[...]
    inv = lax.rsqrt(jnp.mean(x * x, axis=-1, keepdims=True) + RMS_EPS)
    h = (x * inv * g_ref[...]).astype(BF16)

    def mm(w):
        return jnp.dot(h, w, preferred_element_type=F32)

    u_ref[...] = mm(wa_ref[:, C_U:C_U + SSM_WIDTH])
    q_ref[...] = mm(wa_ref[:, C_Q:C_Q + Q_W]) * Q_SCALE
    kvs_ref[...] = mm(wa_ref[:, C_KVS:C_KVS + KV_W])
    kvw_ref[...] = mm(wa_ref[:, C_KVW:C_KVW + KV_W])
    gn_ref[...] = jax.nn.sigmoid(mm(wgn_ref[...]))
    ga_ref[...] = jax.nn.sigmoid(mm(wgab_ref[:, 0:D_MODEL]))
    gb_ref[...] = jax.nn.sigmoid(mm(wgab_ref[:, D_MODEL:2 * D_MODEL]))
    kvct_ref[0] = _dot_t(wat_ref[C_KVC:C_KVC + KV_W, :], h)
    kvst_ref[0] = _dot_t(wat_ref[C_KVS:C_KVS + KV_W, :], h)
    kvwt_ref[0] = _dot_t(wat_ref[C_KVW:C_KVW + KV_W, :], h)


def _inproj_sample(x2d, lay, g, w):
    nseq = lay["tm"]
    ts = lay["grid"][1]
    names = ["u", "q", "kvs", "kvw", "gn", "ga", "gb"]
    widths = [SSM_WIDTH, Q_W, KV_W, KV_W, LANES, D_MODEL, D_MODEL]
    out_shapes, out_specs = [], []
    for n, wd in zip(names, widths):
        shp, spec = lay["b" if n == "u" else "a"](wd)
        out_shapes.append(jax.ShapeDtypeStruct(shp, F32))
        out_specs.append(spec)
    for n in ("kvct", "kvst", "kvwt"):
        names.append(n)
        out_shapes.append(jax.ShapeDtypeStruct((ts, KV_W, nseq), F32))
        out_specs.append(pl.BlockSpec((1, KV_W, nseq), lambda s, b: (b, 0, 0)))
    x_shape, x_spec = lay["a"](D_MODEL)
    ops = [g, w["wa"], w["wat"], w["wgn"], w["wgab"]]
    outs = pl.pallas_call(
        _inproj_sample_body,
        grid=lay["grid"],
        in_specs=[x_spec] + [_full(o.shape) for o in ops],
        out_specs=out_specs,
        out_shape=out_shapes,
        compiler_params=_cparams(("arbitrary",) * 2, 56),
        name="inproj_sample",
    )(x2d.reshape(x_shape), *ops)
    return dict(zip(names, outs))


def _inproj_params(w_in0):
    wt = w_in0.T
    gn_rows = 2 * 16
    return {"wa": w_in0[:, :N_MAIN].astype(BF16),
            "wat": wt[:N_MAIN].astype(BF16),
            "wgn": jnp.pad(w_in0[:, N_MAIN:N_MAIN + NSA_GATE_W], ((0, 0), (0, LANES - NSA_GATE_W))).astype(BF16),
            "wgnt": jnp.pad(wt[N_MAIN:N_MAIN + NSA_GATE_W], ((0, gn_rows - NSA_GATE_W), (0, 0))).astype(BF16),
            "wgab": w_in0[:, N_MAIN + NSA_GATE_W:].astype(BF16)}


def _s5_prompt_body(u_ref, wb_ref, ar_ref, ai_ref, cw_ref, d_ref, wglu_ref, bglu_ref, wso_ref,
                    abr_ref, hlast_ref, lhs_ref, bu_ref, h8_ref, p_ref, us_ref, hstate_ref):
    c = pl.program_id(0)
    nseq = 4
    r4 = u_ref.shape[0]
    tc = r4 // nseq
    half = tc // 2

    @pl.when(c == 0)
    def _():
        hstate_ref[...] = jnp.zeros_like(hstate_ref)

    u = u_ref[...]
    row2 = lax.broadcasted_iota(jnp.int32, (r4, SSM_WIDTH), 0)
    lo2 = (row2 % SUBLANES) < nseq
    up = pltpu.roll(u, r4 - nseq, axis=0)
    dn = pltpu.roll(u, nseq, axis=0)
    swapped = jnp.where(lo2, up, dn)
    zero = jnp.zeros_like(u)
    ev_re = jnp.where(lo2, u, zero).astype(BF16).reshape(half, SUBLANES, SSM_WIDTH)
    ev_im = jnp.where(lo2, zero, swapped).astype(BF16).reshape(half, SUBLANES, SSM_WIDTH)
    od_re = jnp.where(lo2, swapped, zero).astype(BF16).reshape(half, SUBLANES, SSM_WIDTH)
    od_im = jnp.where(lo2, zero, u).astype(BF16).reshape(half, SUBLANES, SSM_WIDTH)
    for j in range(4):
        sl = slice(LANES * j, LANES * (j + 1))
        lhs_ref[:, 0:8, 256 * j:256 * j + LANES] = ev_re[:, :, sl]
        lhs_ref[:, 0:8, 256 * j + LANES:256 * (j + 1)] = ev_im[:, :, sl]
        lhs_ref[:, 8:16, 256 * j:256 * j + LANES] = od_re[:, :, sl]
        lhs_ref[:, 8:16, 256 * j + LANES:256 * (j + 1)] = od_im[:, :, sl]
    for j in range(4):
        lhs = lhs_ref[:, :, 256 * j:256 * (j + 1)].reshape(tc * SUBLANES, 256)
        bu_ref[:, 512 * j:512 * (j + 1)] = jnp.dot(lhs, wb_ref[j], preferred_element_type=F32)

    for lc in range(4):
        sl = slice(512 * lc, 512 * (lc + 1))
        ar = ar_ref[:, sl]
        ai = ai_ref[:, sl]

        def step(t, h, sl=sl, ar=ar, ai=ai):
            r0 = pl.multiple_of(t * SUBLANES, SUBLANES)
            h = ar * h + ai * pltpu.roll(h, nseq, axis=0) + bu_ref[pl.ds(r0, SUBLANES), sl]
            h8_ref[pl.ds(r0, SUBLANES), sl] = h
            return h

        hstate_ref[:, sl] = lax.fori_loop(0, tc, step, hstate_ref[:, sl], unroll=8)
    hlast_ref[...] = hstate_ref[...]

    for j in range(4):
        pj = jnp.dot(h8_ref[:, 512 * j:512 * (j + 1)].astype(BF16), cw_ref[j], preferred_element_type=F32)
        p_ref[2 * j] = pj[:, 0:LANES]
        p_ref[2 * j + 1] = pj[:, LANES:2 * LANES]
        us_ref[j] = u[:, LANES * j:LANES * (j + 1)]
    ys = []
    for s in range(nseq):
        parts = []
        for j in range(4):
            re = p_ref[2 * j, pl.ds(s, tc, stride=SUBLANES), :]
            im = p_ref[2 * j + 1, pl.ds(nseq + s, tc, stride=SUBLANES), :]
            us = us_ref[j, pl.ds(s, tc, stride=nseq), :]
            parts.append(re + im + d_ref[:, LANES * j:LANES * (j + 1)] * us)
        ys.append(jnp.concatenate(parts, axis=1))
    y = jnp.concatenate(ys, axis=0)
    zg = jax.nn.gelu(y)
    gate = jnp.dot(zg.astype(BF16), wglu_ref[...], preferred_element_type=F32) + bglu_ref[...]
    glu = (zg * jax.nn.sigmoid(gate)).astype(BF16)
    abr = jnp.dot(glu, wso_ref[...], preferred_element_type=F32)
    for s in range(nseq):
        abr_ref[s] = abr[s * tc:(s + 1) * tc]


def _s5_prompt(u_ts, sp, wglu, bglu, wso, t_total, tc):
    nseq = 4
    grid = (t_total // tc,)
    abr, hlast = pl.pallas_call(
        _s5_prompt_body,
        grid=grid,
        in_specs=[pl.BlockSpec((tc * nseq, SSM_WIDTH), lambda c: (c, 0)),
                  _full(sp["wb8"].shape), _full(sp["ar8"].shape), _full(sp["ai8"].shape), _full(sp["cw8"].shape),
                  _full(sp["d"].shape), _full(wglu.shape), _full(bglu.shape), _full(wso.shape)],
        out_specs=[pl.BlockSpec((nseq, tc, D_MODEL), lambda c: (0, c, 0)),
                   pl.BlockSpec((SUBLANES, N_STATE), lambda c: (0, 0))],
        out_shape=[jax.ShapeDtypeStruct((nseq, t_total, D_MODEL), F32),
                   jax.ShapeDtypeStruct((SUBLANES, N_STATE), F32)],
        scratch_shapes=[pltpu.VMEM((tc // 2, 2 * SUBLANES, 1024), BF16),
                        pltpu.VMEM((tc * SUBLANES, N_STATE), F32),
                        pltpu.VMEM((tc * SUBLANES, N_STATE), F32),
                        pltpu.VMEM((8, tc * SUBLANES, LANES), F32),
                        pltpu.VMEM((4, tc * nseq, LANES), F32),
                        pltpu.VMEM((SUBLANES, N_STATE), F32)],
        compiler_params=_cparams(("arbitrary",), 56),
        name="s5_prompt",
    )(u_ts, sp["wb8"], sp["ar8"], sp["ai8"], sp["cw8"], sp["d"], wglu, bglu, wso)
    return abr, hlast


def _s5_params(lam_re, lam_im, log_dt, b_re, b_im, c_re, c_im, d_skip):
    lam = lax.complex(lam_re.astype(F32), lam_im.astype(F32))
    dt = jnp.exp(log_dt.astype(F32))[:, None]
    a_bar = jnp.exp(lam * dt)
    b = lax.complex(b_re.astype(F32), b_im.astype(F32))
    b_bar = ((a_bar - 1.0) / lam)[..., None] * b
    eye8 = jnp.eye(8, dtype=F32)

    def bd_b(m):
        return jnp.einsum("ab,jbpc->jacbp", eye8, m.reshape(4, 8, SSM_STATE, SSM_GROUP)).reshape(4, 128, 512)

    def bd_c(m):
        return jnp.einsum("ab,jbcp->japbc", eye8, m.reshape(4, 8, SSM_GROUP, SSM_STATE)).reshape(4, 512, 128)

    wre, wim = bd_b(b_bar.real), bd_b(b_bar.imag)
    cre, cim = bd_c(c_re.astype(F32)), bd_c(c_im.astype(F32))
    ar = a_bar.real.reshape(1, N_STATE)
    ai = a_bar.imag.reshape(1, N_STATE)
    sign = jnp.concatenate([-jnp.ones((4, 1), F32), jnp.ones((4, 1), F32)], axis=0)
    return {
        "wb8": jnp.concatenate([wre, wim], axis=1).astype(BF16),
        "cw8": jnp.concatenate([cre, -cim], axis=2).astype(BF16),
        "ar8": jnp.broadcast_to(ar, (SUBLANES, N_STATE)),
        "ai8": sign * ai,
        "wre": wre.astype(BF16), "wim": wim.astype(BF16),
        "cre": cre.astype(BF16), "cim": cim.astype(BF16),
        "ar": ar, "ai": ai,
        "d": d_skip.astype(F32).reshape(1, SSM_WIDTH),
    }


def _cmp_params(cmp_pe, cmp_w1, cmp_w2):
    eye2 = jnp.eye(2, dtype=F32)
    w1s, pes = [], []
    for s in range(CMP_LEN // CMP_STRIDE):
        w = cmp_w1[:, s * CMP_STRIDE:(s + 1) * CMP_STRIDE].astype(F32)
        w1s.append(jnp.einsum("pk,qh,kide->ipqdkhe", eye2, eye2, w).reshape(CMP_STRIDE * KV_W, KV_W))
        pe = cmp_pe[:, s * CMP_STRIDE:(s + 1) * CMP_STRIDE].astype(F32)
        pes.append(jnp.broadcast_to(pe.transpose(1, 0, 2)[:, :, None, :],
                                    (CMP_STRIDE, 2, N_KV_HEADS, HEAD_DIM)).reshape(1, CMP_STRIDE * KV_W))
    w2 = jnp.einsum("pk,qh,kef->pqekhf", eye2, eye2, cmp_w2.astype(F32)).reshape(KV_W, KV_W)
    nhalf = CMP_LEN // CMP_STRIDE
    w1r = cmp_w1.astype(F32).reshape(2, nhalf, CMP_STRIDE, HEAD_DIM, HEAD_DIM)
    wk = jnp.einsum("kside,ph->kipdshe", w1r, eye2).reshape(2, CMP_STRIDE * LANES, nhalf * LANES)
    bk = jnp.einsum("kld,klde->ke", cmp_pe.astype(F32), cmp_w1.astype(F32), precision=lax.Precision.HIGHEST)
    w2k = jnp.einsum("kef,ph->kpehf", cmp_w2.astype(F32), eye2).reshape(2, LANES, LANES)
    return {"w1": jnp.stack(w1s).astype(BF16),
            "pe": jnp.concatenate(pes, axis=0),
            "w2": w2.astype(BF16),
            "wk": wk.astype(BF16), "bk": jnp.tile(bk, (1, N_KV_HEADS)), "w2k": w2k.astype(BF16)}


def _compress_rows(x, pe_ref, w1_ref, w2_ref):
    nch = x.shape[0]
    p0 = jnp.dot((x + pe_ref[0:1, :]).astype(BF16), w1_ref[0], preferred_element_type=F32)
    p1 = jnp.dot((x + pe_ref[1:2, :]).astype(BF16), w1_ref[1], preferred_element_type=F32)
    pre = p0 + pltpu.roll(p1, nch - 1, axis=0)
    return jnp.dot(jax.nn.gelu(pre).astype(BF16), w2_ref[...], preferred_element_type=F32)


def _compress_prompt_body(x_ref, pe_ref, w1_ref, w2_ref, ck_ref, cvt_ref):
    out = _compress_rows(x_ref[0], pe_ref, w1_ref, w2_ref)
    ck_ref[0] = out[:, 0:LANES].astype(BF16)
    cvt_ref[0] = out[:, LANES:2 * LANES].T.astype(BF16)


def _compress_prompt(kvc, cp, nseq, t):
    nch = t // CMP_STRIDE
    x = kvc.reshape(nseq, nch, CMP_STRIDE * KV_W)
    return pl.pallas_call(
        _compress_prompt_body,
        grid=(nseq,),
        in_specs=[pl.BlockSpec((1, nch, CMP_STRIDE * KV_W), lambda n: (n, 0, 0)),
                  _full(cp["pe"].shape), _full(cp["w1"].shape), _full(cp["w2"].shape)],
        out_specs=[pl.BlockSpec((1, nch, LANES), lambda n: (n, 0, 0)),
                   pl.BlockSpec((1, LANES, nch), lambda n: (n, 0, 0))],
        out_shape=[jax.ShapeDtypeStruct((nseq, nch, LANES), BF16),
                   jax.ShapeDtypeStruct((nseq, LANES, nch), BF16)],
        compiler_params=_cparams(("arbitrary",), 48),
        name="compress_prompt",
    )(x, cp["pe"], cp["w1"], cp["w2"])


def _overlap_t(n_cmp_pad, n_slc_pad):
    j = np.arange(n_cmp_pad)[None, :]
    s = np.arange(n_slc_pad)[:, None]
    ov = (j * CMP_STRIDE <= s * SLC_BLOCK + SLC_BLOCK - 1) & (j * CMP_STRIDE + CMP_LEN - 1 >= s * SLC_BLOCK)
    return jnp.asarray(ov, dtype=BF16)


def _softmax_cols(s, valid):
    sm = jnp.where(valid, s, NEG_INF)
    mx = jnp.max(sm, axis=0, keepdims=True)
    e = jnp.where(valid, jnp.exp2(sm - mx), 0.0)
    l = jnp.sum(e, axis=0, keepdims=True)
    return e * (1.0 / jnp.maximum(l, 1e-30))


def _select_blocks(imp, blk, pos, nblk, axis=0):
    cur = pos // SLC_BLOCK
    forced = (blk == 0) | (blk == cur) | (blk == cur - 1)
    v = jnp.where(forced, imp + FORCE_BONUS, imp)
    v = jnp.where(blk * SLC_BLOCK <= pos, v, NEG_INF)
    v = jnp.where(blk < nblk, v, -3e38)
    blk_f = blk.astype(F32)
    neg = jnp.full(imp.shape, NEG_INF, F32)
    for _ in range(min(TOP_N, nblk)):
        mx = jnp.max(v, axis=axis, keepdims=True)
        first = jnp.min(jnp.where(v == mx, blk_f, float(imp.shape[axis])), axis=axis, keepdims=True)
        pick = blk_f == first
        neg = jnp.where(pick, 0.0, neg)
        v = jnp.where(pick, -3e38, v)
    return neg


CB = 2 * LANES


def _attn_prompt_body(q_ref, gn_ref, ck_ref, cvt_ref, ks_ref, vst_ref, kw_ref, vwt_ref, ovt_ref,
                      o_ref, kaug_ref, kwaug_ref, qaug_ref, acc_ref):
    i = pl.program_id(1)
    t = ks_ref.shape[1]
    nch = ck_ref.shape[1]
    nslc = t // SLC_BLOCK
    qb = Q_BLOCK
    ncol = N_Q_HEADS * qb
    ncb = ncol // CB
    q0 = i * qb
    one_row = 2 * LANES - HEAD_DIM

    @pl.when(i == 0)
    def _():
        kaug_ref[:, 0:LANES] = ks_ref[0]
        blk = lax.broadcasted_iota(jnp.int32, (t, LANES), 0) // SLC_BLOCK
        col = lax.broadcasted_iota(jnp.int32, (t, LANES), 1)
        kaug_ref[:, LANES:2 * LANES] = jnp.where(blk == col, 1.0, 0.0).astype(BF16)
        padcol = lax.broadcasted_iota(jnp.int32, (WINDOW, 2 * LANES), 1)
        kwaug_ref[0:WINDOW, :] = jnp.where(padcol == one_row, NEG_INF, 0.0).astype(BF16)
        kwaug_ref[WINDOW:WINDOW + t, 0:LANES] = kw_ref[0]
        kwaug_ref[WINDOW:WINDOW + t, LANES:2 * LANES] = jnp.zeros((t, LANES), BF16)

    zeros64 = jnp.zeros((HEAD_DIM, qb), BF16)
    for j in range(N_Q_HEADS):
        dst = j // GQA
        qaug_ref[HEAD_DIM * dst:HEAD_DIM * (dst + 1), qb * j:qb * (j + 1)] = q_ref[0, HEAD_DIM * j:HEAD_DIM * (j + 1), :]
        qaug_ref[HEAD_DIM * (1 - dst):HEAD_DIM * (2 - dst), qb * j:qb * (j + 1)] = zeros64
    tail_row = lax.broadcasted_iota(jnp.int32, (HEAD_DIM, ncol), 0)
    qaug_ref[one_row:2 * LANES, :] = jnp.where(tail_row == 0, 1.0, 0.0).astype(BF16)

    pos_c = q0 + (lax.broadcasted_iota(jnp.int32, (nch, CB), 1) & (qb - 1))
    cvalid = lax.broadcasted_iota(jnp.int32, (nch, CB), 0) * CMP_STRIDE + (CMP_LEN - 1) <= pos_c
    ocs, imps = [], []
    for cb in range(ncb):
        sc = jnp.dot(ck_ref[0], qaug_ref[0:LANES, CB * cb:CB * (cb + 1)], preferred_element_type=F32)
        pc = _softmax_cols(sc, cvalid).astype(BF16)
        ocs.append(jnp.dot(cvt_ref[0], pc, preferred_element_type=F32))
        imps.append(jnp.dot(ovt_ref[...], pc, preferred_element_type=F32))
    oc = jnp.concatenate(ocs, axis=1)
    imp = jnp.concatenate(imps, axis=1)
    blk = lax.broadcasted_iota(jnp.int32, (nslc, qb), 0)
    pos_q = q0 + lax.broadcasted_iota(jnp.int32, (nslc, qb), 1)
    for h in range(N_KV_HEADS):
        v = imp[0:nslc, qb * GQA * h:qb * GQA * h + qb]
        for g in range(1, GQA):
            v = v + imp[0:nslc, qb * (GQA * h + g):qb * (GQA * h + g + 1)]
        neg = _select_blocks(v, blk, pos_q, nslc).astype(BF16)
        for g in range(GQA):
            j = GQA * h + g
            qaug_ref[LANES:LANES + nslc, qb * j:qb * (j + 1)] = neg
    if nslc < HEAD_DIM:
        qaug_ref[LANES + nslc:LANES + HEAD_DIM, :] = jnp.zeros((HEAD_DIM - nslc, ncol), BF16)

    brow = lax.broadcasted_iota(jnp.int32, (qb, CB), 0)
    bcol = lax.broadcasted_iota(jnp.int32, (qb, CB), 1) & (qb - 1)
    tri_lo = jnp.where(brow <= bcol, 0.0, NEG_INF)

    acc_ref[...] = jnp.zeros_like(acc_ref)

    def sel_tile(k0, nk, vt, carry, bias):
        m, l = carry
        ka = kaug_ref[pl.ds(k0, nk), :]
        css = [slice(CB * cb, CB * (cb + 1)) for cb in range(ncb)]
        ss = [jnp.dot(ka, qaug_ref[:, cs], preferred_element_type=F32) for cs in css]
        ms, ls, ps, alphas = [], [], [], []
        for cs, s in zip(css, ss):
            if bias is not None:
                s = s + bias
            mn = jnp.maximum(m[:, cs], jnp.max(s, axis=0, keepdims=True))
            alpha = jnp.exp2(m[:, cs] - mn)
            p = jnp.exp2(s - mn)
            ms.append(mn)
            ls.append(alpha * l[:, cs] + jnp.sum(p, axis=0, keepdims=True))
            ps.append(p.astype(BF16))
            alphas.append(alpha)
        pvs = [jnp.dot(vt, p, preferred_element_type=F32) for p in ps]
        for cs, alpha, pv in zip(css, alphas, pvs):
            acc_ref[:, cs] = alpha * acc_ref[:, cs] + pv
        return jnp.concatenate(ms, axis=1), jnp.concatenate(ls, axis=1)

    def vt_blocks(ref, b0, n):
        return jnp.concatenate([ref[0, b0 + j] for j in range(n)], axis=1) if n > 1 else ref[0, b0]

    big_blocks, q_blocks = TK_SLC // TK_WIN, qb // TK_WIN

    def big_tile(kt, carry):
        return sel_tile(pl.multiple_of(kt * TK_SLC, TK_SLC), TK_SLC, vt_blocks(vst_ref, kt * big_blocks, big_blocks),
                        carry, None)

    def small_tile(kb, carry):
        return sel_tile(pl.multiple_of(kb * qb, qb), qb, vt_blocks(vst_ref, kb * q_blocks, q_blocks), carry, None)

    carry = (jnp.full((1, ncol), NEG_INF, F32), jnp.zeros((1, ncol), F32))
    nbig = q0 // TK_SLC
    carry = lax.fori_loop(0, nbig, big_tile, carry)
    carry = lax.fori_loop(nbig * (TK_SLC // qb), i, small_tile, carry)
    _, l = sel_tile(pl.multiple_of(q0, qb), qb, vt_blocks(vst_ref, i * q_blocks, q_blocks), carry, tri_lo)
    osel = acc_ref[...] * (1.0 / l)

    npiece = (WINDOW + qb) // TK_WIN
    kws = [kwaug_ref[pl.ds(pl.multiple_of(q0 + w * TK_WIN, TK_WIN), TK_WIN), :] for w in range(npiece)]
    vwt = jnp.concatenate([vwt_ref[0, jnp.maximum(i * q_blocks + w - WINDOW // TK_WIN, 0)] for w in range(npiece)],
                          axis=1)
    wrow = lax.broadcasted_iota(jnp.int32, (TK_WIN, CB), 0)
    wcol = lax.broadcasted_iota(jnp.int32, (TK_WIN, CB), 1) & (qb - 1)
    wbias = []
    for w in range(npiece):
        lo, hi = w * TK_WIN - WINDOW, w * TK_WIN - WINDOW + TK_WIN - 1
        if hi <= 0 and qb - 1 - lo < WINDOW:
            wbias.append(None)
        else:
            dlt = wcol - wrow - lo
            wbias.append(jnp.where((dlt >= 0) & (dlt < WINDOW), 0.0, NEG_INF))
    ows = []
    for cb in range(ncb):
        qa = qaug_ref[:, CB * cb:CB * (cb + 1)]
        sw = [jnp.dot(kws[w], qa, preferred_element_type=F32) for w in range(npiece)]
        sw = [x if b is None else x + b for x, b in zip(sw, wbias)]
        s = jnp.concatenate(sw, axis=0)
        e = jnp.exp2(s - jnp.max(s, axis=0, keepdims=True))
        ows.append(jnp.dot(vwt, e.astype(BF16), preferred_element_type=F32) * (1.0 / jnp.sum(e, axis=0, keepdims=True)))
    ow = jnp.concatenate(ows, axis=1)

    gt = gn_ref[0]
    for c in range(N_Q_HEADS // 2):
        rows = []
        for hh in range(2):
            j = 2 * c + hh
            rs = slice(HEAD_DIM * (j // GQA), HEAD_DIM * (j // GQA + 1))
            cs = slice(qb * j, qb * (j + 1))
            rows.append(gt[3 * j:3 * j + 1, :] * oc[rs, cs] + gt[3 * j + 1:3 * j + 2, :] * osel[rs, cs]
                        + gt[3 * j + 2:3 * j + 3, :] * ow[rs, cs])
        o_ref[:, LANES * c:LANES * (c + 1)] = jnp.concatenate(rows, axis=0).T.astype(o_ref.dtype)


def _attn_prompt(q, gn, ck, cvt, ksb, vst, kwb, vwt, nseq, t):
    nb = t // Q_BLOCK
    nch = t // CMP_STRIDE
    nslc = t // SLC_BLOCK
    ovt = _overlap_t(nch, max(nslc, SUBLANES))
    row = lambda n, i: (n * nb + i, 0)
    seq3 = lambda n, i: (n, 0, 0)
    seq4 = lambda n, i: (n, 0, 0, 0)
    col3 = lambda n, i: (n, 0, i)
    return pl.pallas_call(
        _attn_prompt_body,
        grid=(nseq, nb),
        in_specs=[pl.BlockSpec((1, Q_W, Q_BLOCK), col3), pl.BlockSpec((1, gn.shape[1], Q_BLOCK), col3),
                  pl.BlockSpec((1, nch, LANES), seq3), pl.BlockSpec((1, LANES, nch), seq3),
                  pl.BlockSpec((1, t, LANES), seq3), pl.BlockSpec((1, t // TK_WIN, LANES, TK_WIN), seq4),
                  pl.BlockSpec((1, t, LANES), seq3), pl.BlockSpec((1, t // TK_WIN, LANES, TK_WIN), seq4),
                  _full(ovt.shape)],
        out_specs=pl.BlockSpec((Q_BLOCK, Q_W), row),
        out_shape=jax.ShapeDtypeStruct((nseq * t, Q_W), BF16),
        scratch_shapes=[pltpu.VMEM((t, 2 * LANES), BF16),
                        pltpu.VMEM((WINDOW + t, 2 * LANES), BF16),
                        pltpu.VMEM((2 * LANES, N_Q_HEADS * Q_BLOCK), BF16),
                        pltpu.VMEM((LANES, N_Q_HEADS * Q_BLOCK), F32)],
        compiler_params=_cparams(("arbitrary", "arbitrary"), 56),
        name="attn_prompt",
    )(q, gn, ck, cvt, ksb.reshape(nseq, t, LANES), vst, kwb.reshape(nseq, t, LANES), vwt, ovt)


def _post_body(x_ref, abr_ref, on_ref, ga_ref, gb_ref, wno_ref, wo_ref, g2_ref, x1_ref, h2_ref):
    bbr = jnp.dot(on_ref[...].astype(BF16), wno_ref[...], preferred_element_type=F32)
    merged = ga_ref[...] * abr_ref[...] + gb_ref[...] * bbr
    x1 = x_ref[...] + jnp.dot(merged.astype(BF16), wo_ref[...], preferred_element_type=F32)
    x1_ref[...] = x1
    inv = lax.rsqrt(jnp.mean(x1 * x1, axis=-1, keepdims=True) + RMS_EPS)
    h2_ref[...] = (x1 * inv * g2_ref[...]).astype(BF16)


def _post(x2d, abr, abr_lay, onsa, ga, gb, wno, wo, g2, lay, out_lay):
    x_shape, x_spec = lay["a"](D_MODEL)
    abr_shape, abr_spec = lay[abr_lay](D_MODEL)
    on_shape, on_spec = lay["a"](Q_W)
    o_shape, o_spec = lay[out_lay](D_MODEL)
    return pl.pallas_call(
        _post_body,
        grid=lay["grid"],
        in_specs=[x_spec, abr_spec, on_spec, x_spec, x_spec, _full(wno.shape), _full(wo.shape), _full(g2.shape)],
        out_specs=[o_spec, o_spec],
        out_shape=[jax.ShapeDtypeStruct(o_shape, F32), jax.ShapeDtypeStruct(o_shape, BF16)],
        compiler_params=_cparams(("arbitrary",) * len(lay["grid"]), 48),
        name="post",
    )(x2d.reshape(x_shape), abr.reshape(abr_shape), onsa.reshape(on_shape), ga, gb, wno, wo, g2)


def _route(logits):
    lane = lax.broadcasted_iota(jnp.int32, logits.shape, 1).astype(F32)
    big = float(LANES)
    glog = jnp.where(lane < N_EXPERT_GROUPS, logits, -jnp.inf)
    gmax = jnp.max(glog, axis=1, keepdims=True)
    gsel = jnp.min(jnp.where(glog == gmax, lane, big), axis=1, keepdims=True)
    gw = 1.0 / jnp.sum(jnp.exp(glog - gmax), axis=1, keepdims=True)
    lo = N_EXPERT_GROUPS + EXPERTS_PER_GROUP * gsel
    el = jnp.where((lane >= lo) & (lane < lo + EXPERTS_PER_GROUP), logits, -jnp.inf)
    v1 = jnp.max(el, axis=1, keepdims=True)
    i1 = jnp.min(jnp.where(el == v1, lane, big), axis=1, keepdims=True)
    el2 = jnp.where(lane == i1, -jnp.inf, el)
    v2 = jnp.max(el2, axis=1, keepdims=True)
    i2 = jnp.min(jnp.where(el2 == v2, lane, big), axis=1, keepdims=True)
    e2 = jnp.exp(v2 - v1)
    w1 = gw / (1.0 + e2)
    return jnp.where(lane == i1, w1, 0.0) + jnp.where(lane == i2, w1 * e2, 0.0)


def _moe_body(x1_ref, h2_ref, p_ref, wr_ref, br_ref, wg_ref, wu_ref, wd_ref, wpg_ref, wp_ref, gf_ref,
              y_ref, acc_ref, comb_ref, *, tsplit):
    g = pl.program_id(1)
    h2 = h2_ref[...]

    @pl.when(g == 0)
    def _():
        logits = jnp.dot(h2, wr_ref[...], preferred_element_type=F32) + br_ref[...]
        comb_ref[...] = _route(logits)
        acc_ref[...] = jnp.zeros_like(acc_ref)

    comb = comb_ref[...]
    lane = lax.broadcasted_iota(jnp.int32, comb.shape, 1)
    acc = acc_ref[...]
    for k in range(EXPERTS_PER_GROUP):
        e_lane = N_EXPERT_GROUPS + EXPERTS_PER_GROUP * g + k
        ce = jnp.sum(jnp.where(lane == e_lane, comb, 0.0), axis=1, keepdims=True)
        a = jnp.dot(h2, wg_ref[k], preferred_element_type=F32)
        b = jnp.dot(h2, wu_ref[k], preferred_element_type=F32)
        act = (jax.nn.silu(a) * b * ce).astype(BF16)
        acc = acc + jnp.dot(act, wd_ref[k], preferred_element_type=F32)
    acc_ref[...] = acc

    @pl.when(g == N_EXPERT_GROUPS - 1)
    def _():
        x2 = x1_ref[...] + acc_ref[...]
        rows = x2.shape[0] // tsplit
        if tsplit == 1:
            p = p_ref[...]
        else:
            p = jnp.concatenate([p_ref[:, PLE_DIM * t:PLE_DIM * (t + 1)] for t in range(tsplit)], axis=0)
        gate = jax.nn.sigmoid(jnp.dot(x2.astype(BF16), wpg_ref[...], preferred_element_type=F32))
        x3 = x2 + gate * jnp.dot(p.astype(BF16), wp_ref[...], preferred_element_type=F32)
        inv = lax.rsqrt(jnp.mean(x3 * x3, axis=-1, keepdims=True) + RMS_EPS)
        y = x3 * inv * gf_ref[...]
        if tsplit == 1:
            y_ref[...] = y
        else:
            for t in range(tsplit):
                y_ref[:, D_MODEL * t:D_MODEL * (t + 1)] = y[rows * t:rows * (t + 1)]


def _moe(x1, h2, p, mp, tm, tsplit):
    rows = x1.shape[0]
    nrb = rows // tm
    rb = lambda r, g: (r, 0)
    grp = lambda r, g: (g, 0, 0)
    if tsplit == 1:
        p_spec = pl.BlockSpec((tm, PLE_DIM), rb)
        y_spec = pl.BlockSpec((tm, D_MODEL), rb)
        y_shape = (rows, D_MODEL)
    else:
        assert nrb == 1
        p_spec = _full(p.shape)
        y_shape = (rows // tsplit, tsplit * D_MODEL)
        y_spec = _full(y_shape)
    return pl.pallas_call(
        functools.partial(_moe_body, tsplit=tsplit),
        grid=(nrb, N_EXPERT_GROUPS),
        in_specs=[pl.BlockSpec((tm, D_MODEL), rb), pl.BlockSpec((tm, D_MODEL), rb), p_spec,
                  _full(mp["wr"].shape), _full(mp["br"].shape),
                  pl.BlockSpec((EXPERTS_PER_GROUP, D_MODEL, D_FF_EXPERT), grp),
                  pl.BlockSpec((EXPERTS_PER_GROUP, D_MODEL, D_FF_EXPERT), grp),
                  pl.BlockSpec((EXPERTS_PER_GROUP, D_FF_EXPERT, D_MODEL), grp),
                  _full(mp["wpg"].shape), _full(mp["wp"].shape), _full(mp["gf"].shape)],
        out_specs=y_spec,
        out_shape=jax.ShapeDtypeStruct(y_shape, F32),
        scratch_shapes=[pltpu.VMEM((tm, D_MODEL), F32), pltpu.VMEM((tm, LANES), F32)],
        compiler_params=_cparams(("arbitrary", "arbitrary"), 56),
        name="moe_ple",
    )(x1, h2, p, mp["wr"], mp["br"], mp["wg"], mp["wu"], mp["wd"], mp["wpg"], mp["wp"], mp["gf"])


def _s5_sample_body(u_ref, h0re_ref, h0im_ref, wre_ref, wim_ref, ar_ref, ai_ref, cre_ref, cim_ref, d_ref,
                    wglu_ref, bglu_ref, wso_ref, abr_ref, hre_out_ref, him_out_ref,
                    bure_ref, buim_ref, hre_ref, him_ref, *, nseq, nstep):
    u = u_ref[...]
    ub = u.astype(BF16)
    for j in range(4):
        lhs = ub[:, LANES * j:LANES * (j + 1)]
        bure_ref[:, 512 * j:512 * (j + 1)] = jnp.dot(lhs, wre_ref[j], preferred_element_type=F32)
        buim_ref[:, 512 * j:512 * (j + 1)] = jnp.dot(lhs, wim_ref[j], preferred_element_type=F32)
    for lc in range(4):
        sl = slice(512 * lc, 512 * (lc + 1))
        ar = jnp.broadcast_to(ar_ref[:, sl], (SUBLANES, 512))
        ai = jnp.broadcast_to(ai_ref[:, sl], (SUBLANES, 512))

        def body(rc, carry, sl=sl, ar=ar, ai=ai):
            r0 = pl.multiple_of(rc * SUBLANES, SUBLANES)
            hr = h0re_ref[pl.ds(r0, SUBLANES), sl]
            hi = h0im_ref[pl.ds(r0, SUBLANES), sl]
            for t in range(nstep):
                rr = pl.multiple_of(t * nseq + rc * SUBLANES, SUBLANES)
                hr, hi = (ar * hr - ai * hi + bure_ref[pl.ds(rr, SUBLANES), sl],
                          ar * hi + ai * hr + buim_ref[pl.ds(rr, SUBLANES), sl])
                hre_ref[pl.ds(rr, SUBLANES), sl] = hr
                him_ref[pl.ds(rr, SUBLANES), sl] = hi
            hre_out_ref[pl.ds(r0, SUBLANES), sl] = hr
            him_out_ref[pl.ds(r0, SUBLANES), sl] = hi
            return carry

        lax.fori_loop(0, nseq // SUBLANES, body, 0)
    parts = []
    for j in range(4):
        sl = slice(512 * j, 512 * (j + 1))
        parts.append(jnp.dot(hre_ref[:, sl].astype(BF16), cre_ref[j], preferred_element_type=F32)
                     - jnp.dot(him_ref[:, sl].astype(BF16), cim_ref[j], preferred_element_type=F32))
    y = jnp.concatenate(parts, axis=1) + d_ref[...] * u
    zg = jax.nn.gelu(y)
    gate = jnp.dot(zg.astype(BF16), wglu_ref[...], preferred_element_type=F32) + bglu_ref[...]
    glu = (zg * jax.nn.sigmoid(gate)).astype(BF16)
    abr_ref[...] = jnp.dot(glu, wso_ref[...], preferred_element_type=F32)


def _s5_sample(u_ts, h0re, h0im, sp, wglu, bglu, wso, nseq, nstep):
    rows = nseq * nstep
    ops = [u_ts, h0re, h0im, sp["wre"], sp["wim"], sp["ar"], sp["ai"], sp["cre"], sp["cim"], sp["d"], wglu, bglu, wso]
    return pl.pallas_call(
        functools.partial(_s5_sample_body, nseq=nseq, nstep=nstep),
        grid=(1,),
        in_specs=[_full(o.shape) for o in ops],
        out_specs=[_full((rows, D_MODEL)), _full((nseq, N_STATE)), _full((nseq, N_STATE))],
        out_shape=[jax.ShapeDtypeStruct((rows, D_MODEL), F32),
                   jax.ShapeDtypeStruct((nseq, N_STATE), F32), jax.ShapeDtypeStruct((nseq, N_STATE), F32)],
        scratch_shapes=[pltpu.VMEM((rows, N_STATE), F32) for _ in range(4)],
        compiler_params=_cparams(("arbitrary",), 56),
        name="s5_sample",
    )(*ops)


def _softmax_rows(s, valid):
    sm = jnp.where(valid, s, NEG_INF)
    mx = jnp.max(sm, axis=1, keepdims=True)
    e = jnp.where(valid, jnp.exp2(sm - mx), 0.0)
    l = jnp.sum(e, axis=1, keepdims=True)
    return e * (1.0 / jnp.maximum(l, 1e-30))


SAMPLE_SEQS_PER_STEP = 4
CMP_PITCH = 24


def _attn_sample_body(pt_ref, q_ref, gn_ref, nks_ref, nkw_ref, wint_ref, wk_ref, bk_ref, w2k_ref, ov_ref, e_ref, *rest,
                      npage, past_len, nsub, tq):
    o_ref, nwint_ref, xrow_ref = rest[2 * nsub * npage:]
    nrow = N_Q_HEADS * tq
    nwin = wint_ref.shape[2]
    nslc = -(-(past_len + tq) // SLC_BLOCK)
    nch = past_len // CMP_STRIDE
    per_page = PAGE_SIZE // CMP_STRIDE

    cmp = []
    for kv in range(2):
        for s in range(nsub):
            for p in range(npage):
                rows = rest[s * npage + p][0, LANES * kv:LANES * (kv + 1), :].T
                for c in range(per_page):
                    r0 = CMP_PITCH * (per_page * p + c)
                    xrow_ref[s, r0:r0 + CMP_STRIDE, :] = rows[CMP_STRIDE * c:CMP_STRIDE * (c + 1)]
        x = jnp.concatenate(
            [jnp.concatenate([xrow_ref[s, pl.ds(i, nch, stride=CMP_PITCH), :] for i in range(CMP_STRIDE)], axis=1)
             for s in range(nsub)], axis=0)
        pp = jnp.dot(x.astype(BF16), wk_ref[kv], preferred_element_type=F32)
        pre = pp[:, 0:LANES] + pltpu.roll(pp[:, LANES:2 * LANES], nsub * nch - 1, axis=0) + bk_ref[kv:kv + 1, :]
        cmp.append(jnp.dot(jax.nn.gelu(pre).astype(BF16), w2k_ref[kv], preferred_element_type=F32).astype(BF16))
    cks = [cmp[0][nch * s:nch * (s + 1)] for s in range(nsub)]
    cvs = [cmp[1][nch * s:nch * (s + 1)] for s in range(nsub)]
    slc_pages = [rest[(nsub + s) * npage:(nsub + s + 1) * npage] for s in range(nsub)]
    seqs = range(nsub)
    rcat = lambda parts: jnp.concatenate(parts, axis=0)

    lane_w = lax.broadcasted_iota(jnp.int32, (KV_W, LANES), 1)
    lane8 = lax.broadcasted_iota(jnp.int32, (tq, LANES), 1)
    nks_l, nkw_l, wint_l, qs_l = [], [], [], []
    for s in seqs:
        rows_s = slice(tq * s, tq * (s + 1))
        nks_l.append(jnp.concatenate([nks_ref[rows_s, :], jnp.zeros((LANES - tq, KV_W), F32)], axis=0))
        nkw = jnp.concatenate([nkw_ref[rows_s, :], jnp.zeros((LANES - tq, KV_W), F32)], axis=0)
        nkw_l.append(nkw)
        wint = wint_ref[s]
        wint_l.append(wint)
        shifted = pltpu.roll(wint, nwin - tq, axis=1)
        new_t = pltpu.roll(nkw.T, LANES - tq, axis=1)
        nwint_ref[s, :, 0:nwin - LANES] = shifted[:, 0:nwin - LANES]
        nwint_ref[s, :, nwin - LANES:nwin] = jnp.where(lane_w >= LANES - tq, new_t, shifted[:, nwin - LANES:nwin])
        q = q_ref[rows_s, :]
        qrows = []
        for j in range(N_Q_HEADS):
            chunk = q[:, LANES * (j // 2):LANES * (j // 2 + 1)]
            dst = j // GQA
            if (j % 2) != dst:
                chunk = pltpu.roll(chunk, HEAD_DIM, axis=1)
            keep = (lane8 < HEAD_DIM) if dst == 0 else (lane8 >= HEAD_DIM)
            qrows.append(jnp.where(keep, chunk, 0.0))
        qs_l.append(jnp.concatenate(qrows, axis=0).astype(BF16))

    rtot = nsub * nrow
    seq_rows = [slice(nrow * s, nrow * (s + 1)) for s in seqs]
    pos = past_len + (lax.broadcasted_iota(jnp.int32, (rtot, LANES), 0) & (tq - 1))
    lane = lax.broadcasted_iota(jnp.int32, (rtot, LANES), 1)

    sc = rcat([_dot_t(qs_l[s], cks[s]) for s in seqs])
    pc = _softmax_rows(sc, lane * CMP_STRIDE + (CMP_LEN - 1) <= pos).astype(BF16)
    oc = rcat([jnp.dot(pc[seq_rows[s]], cvs[s], preferred_element_type=F32) for s in seqs])
    imp = jnp.dot(pc, ov_ref[...], preferred_element_type=F32)
    vs = []
    for s in seqs:
        for h in range(N_KV_HEADS):
            r0 = nrow * s + tq * GQA * h
            v = imp[r0:r0 + tq]
            for g in range(1, GQA):
                v = v + imp[r0 + tq * g:r0 + tq * (g + 1)]
            vs.append(v)
    nsel = len(vs) * tq
    vt = rcat(vs + [jnp.zeros((LANES - nsel, LANES), F32)]).T
    nblk_pad = -(-nslc // SUBLANES) * SUBLANES
    blk_t = lax.broadcasted_iota(jnp.int32, (nblk_pad, LANES), 0)
    pos_t = past_len + (lax.broadcasted_iota(jnp.int32, (nblk_pad, LANES), 1) & (tq - 1))
    neg_t = _select_blocks(vt[0:nblk_pad], blk_t, pos_t, nslc, axis=0)
    neg = rcat([neg_t, jnp.zeros((LANES - nblk_pad, LANES), F32)]).T
    negsel = rcat([neg[tq * (N_KV_HEADS * s + j // GQA):tq * (N_KV_HEADS * s + j // GQA + 1)]
                   for s in seqs for j in range(N_Q_HEADS)])
    negsel_b = negsel.astype(BF16)

    new_blk = past_len // SLC_BLOCK
    ss_l = []
    for s in seqs:
        qaug = jnp.concatenate([qs_l[s], negsel_b[seq_rows[s]]], axis=1)
        parts = []
        for p in range(0, npage, 2):
            kt = jnp.concatenate([slc_pages[s][p][0][0:LANES], slc_pages[s][p + 1][0][0:LANES]], axis=1).astype(BF16)
            et = jnp.concatenate([e_ref[p], e_ref[p + 1]], axis=1)
            parts.append(jnp.dot(qaug, jnp.concatenate([kt, et], axis=0), preferred_element_type=F32))
        parts.append(_dot_t(qs_l[s], nks_l[s][:, 0:LANES].astype(BF16)) + negsel[seq_rows[s], new_blk:new_blk + 1])
        ss_l.append(jnp.concatenate(parts, axis=1))
    ss = rcat(ss_l)
    nkeys = ss.shape[1]
    kpos = lax.broadcasted_iota(jnp.int32, (rtot, nkeys), 1)
    pos_k = past_len + (lax.broadcasted_iota(jnp.int32, (rtot, nkeys), 0) & (tq - 1))
    ps = _softmax_rows(ss, kpos <= pos_k).astype(BF16)
    osel_l = []
    for s in seqs:
        psq = ps[seq_rows[s]]
        o = jnp.dot(psq[:, past_len:nkeys], nks_l[s][:, LANES:2 * LANES].astype(BF16), preferred_element_type=F32)
        for p in range(0, npage, 2):
            vtp = jnp.concatenate([slc_pages[s][p][0][LANES:2 * LANES], slc_pages[s][p + 1][0][LANES:2 * LANES]],
                                  axis=1).astype(BF16)
            o = o + _dot_t(psq[:, PAGE_SIZE * p:PAGE_SIZE * (p + 2)], vtp)
        osel_l.append(o)
    osel = rcat(osel_l)

    sw = rcat([jnp.concatenate([jnp.dot(qs_l[s], wint_l[s][0:LANES].astype(BF16), preferred_element_type=F32),
                                _dot_t(qs_l[s], nkw_l[s][:, 0:LANES].astype(BF16))], axis=1) for s in seqs])
    nw = sw.shape[1]
    widx = lax.broadcasted_iota(jnp.int32, (rtot, nw), 1)
    pos_w = past_len + (lax.broadcasted_iota(jnp.int32, (rtot, nw), 0) & (tq - 1))
    dlt = pos_w - (past_len - nwin + widx)
    pw = _softmax_rows(sw, (dlt >= 0) & (dlt < WINDOW) & (widx < nwin + tq)).astype(BF16)
    ow = rcat([_dot_t(pw[seq_rows[s], 0:nwin], wint_l[s][LANES:2 * LANES].astype(BF16))
               + jnp.dot(pw[seq_rows[s], nwin:nw], nkw_l[s][:, LANES:2 * LANES].astype(BF16),
                         preferred_element_type=F32) for s in seqs])

    for s in seqs:
        rows_s = slice(tq * s, tq * (s + 1))
        gn = gn_ref[rows_s, :]
        for c in range(N_Q_HEADS // 2):
            halves = []
            for hh in range(2):
                j = 2 * c + hh
                rs = slice(nrow * s + tq * j, nrow * s + tq * (j + 1))
                oj = (gn[:, 3 * j:3 * j + 1] * oc[rs] + gn[:, 3 * j + 1:3 * j + 2] * osel[rs]
                      + gn[:, 3 * j + 2:3 * j + 3] * ow[rs])
                if (j // GQA) != hh:
                    oj = pltpu.roll(oj, HEAD_DIM, axis=1)
                halves.append(oj)
            o_ref[rows_s, LANES * c:LANES * (c + 1)] = jnp.where(lane8 < HEAD_DIM, halves[0], halves[1])


def _attn_sample(q, gn, nks, nkw, cache_cmp, cache_slc, cache_win, page_table, cp, nseq, tq, past_len):
    assert tq <= CMP_STRIDE and past_len % PAGE_SIZE == 0
    npage = past_len // PAGE_SIZE
    assert npage % 2 == 0 and PAGE_SIZE == LANES
    n_pool = cache_cmp.shape[0]
    nwin = cache_win.shape[1]
    chunks = past_len // CMP_STRIDE
    ov = _overlap_t(chunks, LANES).T
    key = np.arange(past_len).reshape(npage, 1, PAGE_SIZE)
    e = jnp.asarray(np.arange(LANES).reshape(1, LANES, 1) == key // SLC_BLOCK, dtype=BF16)
    to_t = lambda c: jnp.transpose(c, (0, 2, 3, 4, 1)).reshape(c.shape[0], KV_W, c.shape[1])
    cmp_t, slc_t, win_t = to_t(cache_cmp), to_t(cache_slc), to_t(cache_win)
    nsub = SAMPLE_SEQS_PER_STEP
    assert nseq % nsub == 0
    row = lambda n, pt: (n, 0)
    seq3 = lambda n, pt: (n, 0, 0)
    page = lambda s, p: (lambda n, pt: (pt[n * nsub + s, p], 0, 0))
    consts = [cp["wk"], cp["bk"], cp["w2k"], ov, e]
    in_specs = [pl.BlockSpec((nsub * tq, Q_W), row), pl.BlockSpec((nsub * tq, LANES), row),
                pl.BlockSpec((nsub * tq, KV_W), row), pl.BlockSpec((nsub * tq, KV_W), row),
                pl.BlockSpec((nsub, KV_W, nwin), seq3)]
    in_specs += [pl.BlockSpec(c.shape, (lambda nd: lambda n, pt: (0,) * nd)(c.ndim)) for c in consts]
    pages = [pl.BlockSpec((1, KV_W, PAGE_SIZE), page(s, p)) for s in range(nsub) for p in range(npage)]
    in_specs += pages * 2
    grid_spec = pltpu.PrefetchScalarGridSpec(
        num_scalar_prefetch=1,
        grid=(nseq // nsub,),
        in_specs=in_specs,
        out_specs=[pl.BlockSpec((nsub * tq, Q_W), row), pl.BlockSpec((nsub, KV_W, nwin), seq3)],
        scratch_shapes=[pltpu.VMEM((nsub, chunks * CMP_PITCH, LANES), F32)],
    )
    return pl.pallas_call(
        functools.partial(_attn_sample_body, npage=npage, past_len=past_len, nsub=nsub, tq=tq),
        grid_spec=grid_spec,
        out_shape=[jax.ShapeDtypeStruct((nseq * tq, Q_W), F32), jax.ShapeDtypeStruct((nseq, KV_W, nwin), F32)],
        compiler_params=_cparams(("arbitrary",), 56),
        name="attn_sample",
    )(page_table, q, gn, nks, nkw, win_t, *consts, *([cmp_t] * (nsub * npage)), *([slc_t] * (nsub * npage)))


def _moe_params(w_rg, b_rg, w_re, b_re, w_gate, w_up, w_down, w_ple, w_ple_gate, gf):
    pad = LANES - N_EXPERT_GROUPS - N_EXPERTS
    return {"wr": jnp.pad(jnp.concatenate([w_rg, w_re], axis=1), ((0, 0), (0, pad))).astype(BF16),
            "br": jnp.pad(jnp.concatenate([b_rg, b_re]), (0, pad)).astype(F32).reshape(1, LANES),
            "wg": w_gate.astype(BF16), "wu": w_up.astype(BF16), "wd": w_down.astype(BF16),
            "wpg": w_ple_gate.astype(BF16), "wp": w_ple.astype(BF16), "gf": gf.astype(F32).reshape(1, D_MODEL)}


TM_PROMPT = 512
TC_S5 = 128


def kernel(x_prompt, x_sample, p_prompt, p_sample, cache_cmp_kv, cache_slc_kv, cache_win_kv, state_ssm, page_table, norm1_g, w_in, ssm_lam_re, ssm_lam_im, ssm_log_dt, ssm_b_re, ssm_b_im, ssm_c_re, ssm_c_im, ssm_d, w_glu, b_glu, cmp_pe, cmp_w1, cmp_w2, w_ssm_out, w_nsa_out, w_o, norm2_g, w_route_group, b_route_group, w_route_expert, b_route_expert, w_exp_gate, w_exp_up, w_exp_down, w_ple, w_ple_gate, final_norm_g):
    assert w_in.shape[0] == 1, "one layer"
    l = 0
    nb, t = x_prompt.shape[:2]
    ns, ts = x_sample.shape[:2]
    past_len = page_table.shape[1] * PAGE_SIZE
    kvt = (2, N_KV_HEADS, HEAD_DIM)

    wi = _inproj_params(w_in[l])
    g1 = norm1_g[l].astype(F32).reshape(1, D_MODEL)
    g2 = norm2_g[l].astype(F32).reshape(1, D_MODEL)
    sp = _s5_params(ssm_lam_re[l], ssm_lam_im[l], ssm_log_dt[l], ssm_b_re[l], ssm_b_im[l], ssm_c_re[l], ssm_c_im[l],
                    ssm_d[l])
    cp = _cmp_params(cmp_pe[l], cmp_w1[l], cmp_w2[l])
    mp = _moe_params(w_route_group[l], b_route_group[l], w_route_expert[l], b_route_expert[l], w_exp_gate[l],
                     w_exp_up[l], w_exp_down[l], w_ple[l], w_ple_gate[l], final_norm_g)
    wglu = w_glu[l].astype(BF16)
    bglu = b_glu[l].astype(F32).reshape(1, SSM_WIDTH)
    wso = w_ssm_out[l].astype(BF16)
    wno = w_nsa_out[l].astype(BF16)
    wo = w_o[l].astype(BF16)

    lay = _prompt_layout(nb, t, TM_PROMPT)
    xp = x_prompt.reshape(nb * t, D_MODEL)
    r = _inproj_prompt(xp, lay, g1, wi)
    abr, hlast = _s5_prompt(r["u"].reshape(t * nb, SSM_WIDTH), sp, wglu, bglu, wso, t, TC_S5)
    ck, cvt = _compress_prompt(r["kvc"], cp, nb, t)
    onsa = _attn_prompt(r["qt"], r["gnt"], ck, cvt, r["ksb"], r["vst"], r["kwb"], r["vwt"], nb, t)
    x1, h2 = _post(xp, abr, "a", onsa, r["ga"], r["gb"], wno, wo, g2, lay, "a")
    y_prompt = _moe(x1, h2, p_prompt[l].reshape(nb * t, PLE_DIM), mp, TM_PROMPT, 1).reshape(nb, t, D_MODEL)
    keep = min(WINDOW, t)

    def rows_last(a):
        return jnp.transpose(a.reshape((a.shape[0],) + kvt + (a.shape[2],)), (0, 4, 1, 2, 3))[None]

    new_cmp_p = rows_last(r["kvct"])
    new_slc_p = rows_last(r["kvst"])
    new_win_p = rows_last(r["kvwt"][:, :, t - keep:])
    new_ssm_p = jnp.stack([hlast[0:nb], hlast[nb:2 * nb]], axis=-1).reshape(1, nb, N_SSM_GROUPS, SSM_STATE, 2)

    lays = _sample_layout(ns, ts)
    xs = x_sample.reshape(ns * ts, D_MODEL)
    rs = _inproj_sample(xs, lays, g1, wi)
    h0 = state_ssm[l].astype(F32).reshape(ns, N_STATE, 2)
    abr_s, hre, him = _s5_sample(rs["u"], h0[..., 0], h0[..., 1], sp, wglu, bglu, wso, ns, ts)
    onsa_s, new_win = _attn_sample(rs["q"].reshape(ns * ts, Q_W), rs["gn"].reshape(ns * ts, LANES),
                                   rs["kvs"].reshape(ns * ts, KV_W), rs["kvw"].reshape(ns * ts, KV_W),
                                   cache_cmp_kv[l], cache_slc_kv[l], cache_win_kv[l], page_table, cp, ns, ts, past_len)
    x1s, h2s = _post(xs, abr_s, "b", onsa_s, rs["ga"], rs["gb"], wno, wo, g2, lays, "b")
    y_sample = _moe(x1s, h2s, p_sample[l].reshape(ns, ts * PLE_DIM), mp, ns * ts, ts).reshape(ns, ts, D_MODEL)
    steps_first = lambda a: jnp.transpose(a.reshape((ts,) + kvt + (ns,)), (4, 0, 1, 2, 3))[None]
    new_cmp_s = steps_first(rs["kvct"])
    new_slc_s = steps_first(rs["kvst"])
    new_win_s = rows_last(new_win)
    new_ssm_s = jnp.stack([hre, him], axis=-1).reshape(1, ns, N_SSM_GROUPS, SSM_STATE, 2)
    return (y_prompt, y_sample, new_cmp_p, new_slc_p, new_win_p, new_ssm_p,
            new_cmp_s, new_slc_s, new_win_s, new_ssm_s)
```

```python
import functools
import math

import jax
import jax.numpy as jnp
import numpy as np
from jax import lax
from jax.experimental import pallas as pl
from jax.experimental.pallas import tpu as pltpu

F32 = jnp.float32
BF16 = jnp.bfloat16

D_MODEL = 1024
SSM_WIDTH = 512
SSM_GROUP = 16
N_SSM_GROUPS = 32
SSM_STATE = 64
HEAD_DIM = 64
N_Q_HEADS = 8
N_KV_HEADS = 2
GQA = 4
CMP_LEN = 32
CMP_STRIDE = 16
SLC_BLOCK = 64
TOP_N = 8
WINDOW = 512
Q_BLOCK = 256
NEG_INF = -1e30
FORCE_BONUS = 1e4
Q_W = 512
KV_W = 256
NSA_GATE_W = 24
N_EXPERT_GROUPS = 4
EXPERTS_PER_GROUP = 4
N_EXPERTS = 16
D_FF_EXPERT = 256
PLE_DIM = 256
RMS_EPS = 1e-6
PAGE_SIZE = 128

LANES = 128
SUBLANES = 8
N_STATE = N_SSM_GROUPS * SSM_STATE
MIB = 2 ** 20


def _cparams(sem, vmem_mib):
    return pltpu.CompilerParams(dimension_semantics=sem, vmem_limit_bytes=vmem_mib * MIB)


def _full(shape):
    nd = len(shape)
    return pl.BlockSpec(shape, lambda *_: (0,) * nd)


def _prompt_layout(nseq, t, tm):
    nb = t // tm
    return {
        "grid": (nb, nseq), "tm": tm, "nseq": nseq, "t": t,
        "a": lambda w: ((nseq * t, w), pl.BlockSpec((tm, w), lambda b, s: (s * nb + b, 0))),
    }


def _sample_layout(nseq, t):
    return {
        "grid": (1, t), "tm": nseq,
        "a": lambda w: ((nseq, t * w), pl.BlockSpec((nseq, w), lambda s, b: (0, b))),
        "b": lambda w: ((t * nseq, w), pl.BlockSpec((nseq, w), lambda s, b: (b, 0))),
    }


TK_SLC = 512
TK_WIN = 128


Q_SCALE = HEAD_DIM ** -0.5 * math.log2(math.e)
C_U, C_Q, C_KVC, C_KVS, C_KVW = 0, 512, 1024, 1280, 1536
N_MAIN = 1792


def _dot_t(a, b):
    return lax.dot_general(a, b, (((1,), (1,)), ((), ())), preferred_element_type=F32)


GN_ROWS = 32


def _inproj_prompt_body(x_ref, g_ref, wa_ref, wgn_ref, wgab_ref,
                        u_ref, kvc_ref, ksb_ref, kwb_ref, ga_ref, gb_ref,
                        qt_ref, kvct_ref, kvst_ref, kvwt_ref, gnt_ref, vst_ref, vwt_ref, *, nseq):
    s = pl.program_id(1)
    x = x_ref[...]
    inv = lax.rsqrt(jnp.mean(x * x, axis=-1, keepdims=True) + RMS_EPS)
    h = (x * inv * g_ref[...]).astype(BF16)
    tm = h.shape[0]

    def mm(w):
        return jnp.dot(h, w, preferred_element_type=F32)

    u = mm(wa_ref[:, C_U:C_U + SSM_WIDTH])
    for j in range(SSM_WIDTH // LANES):
        u_ref[j, pl.ds(s, tm, stride=nseq), :] = u[:, LANES * j:LANES * (j + 1)]
    ga_ref[...] = jax.nn.sigmoid(mm(wgab_ref[:, 0:D_MODEL]))
    gb_ref[...] = jax.nn.sigmoid(mm(wgab_ref[:, D_MODEL:2 * D_MODEL]))
    kvc = mm(wa_ref[:, C_KVC:C_KVC + KV_W])
    kvc_ref[0] = kvc[:, 0:LANES]
    kvc_ref[1] = kvc[:, LANES:2 * LANES]
    kvct_ref[0] = kvc.T
    kvs = mm(wa_ref[:, C_KVS:C_KVS + KV_W])
    ksb_ref[...] = kvs[:, 0:LANES].astype(BF16)
    kvst = kvs.T
    kvst_ref[0] = kvst
    kvw = mm(wa_ref[:, C_KVW:C_KVW + KV_W])
    kwb_ref[...] = kvw[:, 0:LANES].astype(BF16)
    kvwt = kvw.T
    kvwt_ref[0] = kvwt
    for c in range(tm // TK_WIN):
        vst_ref[0, c] = kvst[LANES:2 * LANES, c * TK_WIN:(c + 1) * TK_WIN].astype(BF16)
        vwt_ref[0, c] = kvwt[LANES:2 * LANES, c * TK_WIN:(c + 1) * TK_WIN].astype(BF16)
    qt_ref[0] = (mm(wa_ref[:, C_Q:C_Q + Q_W]) * Q_SCALE).T.astype(BF16)
    gnt_ref[0] = jax.nn.sigmoid(mm(wgn_ref[...])).T[0:GN_ROWS]


def _inproj_prompt(x2d, lay, g, w):
    tm, nseq, t = lay["tm"], lay["nseq"], lay["t"]
    nb = t // tm
    out_shapes, out_specs, names = [], [], []

    def add(name, shape_spec, dt):
        names.append(name)
        out_shapes.append(jax.ShapeDtypeStruct(shape_spec[0], dt))
        out_specs.append(shape_spec[1])

    def tr(rows):
        return (nseq, rows, t), pl.BlockSpec((1, rows, tm), lambda b, s: (s, 0, b))

    nu = SSM_WIDTH // LANES
    add("u", ((nu, t * nseq, LANES), pl.BlockSpec((nu, tm * nseq, LANES), lambda b, s: (0, b, 0))), F32)
    add("kvc", ((2, nseq * t, LANES), pl.BlockSpec((2, tm, LANES), lambda b, s: (0, s * nb + b, 0))), F32)
    add("ksb", lay["a"](LANES), BF16)
    add("kwb", lay["a"](LANES), BF16)
    add("ga", lay["a"](D_MODEL), F32)
    add("gb", lay["a"](D_MODEL), F32)
    add("qt", tr(Q_W), BF16)
    add("kvct", tr(KV_W), F32)
    add("kvst", tr(KV_W), F32)
    add("kvwt", tr(KV_W), F32)
    add("gnt", tr(GN_ROWS), F32)
    for name in ("vst", "vwt"):
        add(name, ((nseq, t // TK_WIN, LANES, TK_WIN),
                   pl.BlockSpec((1, tm // TK_WIN, LANES, TK_WIN), lambda b, s: (s, b, 0, 0))), BF16)
    x_shape, x_spec = lay["a"](D_MODEL)
    ops = [g, w["wa"], w["wgn"], w["wgab"]]
    outs = pl.pallas_call(
        functools.partial(_inproj_prompt_body, nseq=nseq),
        grid=lay["grid"],
        in_specs=[x_spec] + [_full(o.shape) for o in ops],
        out_specs=out_specs,
        out_shape=out_shapes,
        compiler_params=_cparams(("arbitrary",) * 2, 56),
        name="inproj_prompt",
    )(x2d.reshape(x_shape), *ops)
    return dict(zip(names, outs))


def _inproj_sample_body(x_ref, g_ref, wa_ref, wgn_ref, wgab_ref,
                        u_ref, q_ref, kvs_ref, kvw_ref, gn_ref, ga_ref, gb_ref, kvct_ref, kvst_ref, kvwt_ref):
    x = x_ref[...]
    inv = lax.rsqrt(jnp.mean(x * x, axis=-1, keepdims=True) + RMS_EPS)
    h = (x * inv * g_ref[...]).astype(BF16)

    def mm(w):
        return jnp.dot(h, w, preferred_element_type=F32)

    u_ref[...] = mm(wa_ref[:, C_U:C_U + SSM_WIDTH])
    q_ref[...] = mm(wa_ref[:, C_Q:C_Q + Q_W]) * Q_SCALE
    kvs = mm(wa_ref[:, C_KVS:C_KVS + KV_W])
    kvs_ref[...] = kvs
    kvw = mm(wa_ref[:, C_KVW:C_KVW + KV_W])
    kvw_ref[...] = kvw
    gn_ref[...] = jax.nn.sigmoid(mm(wgn_ref[...]))
    ga_ref[...] = jax.nn.sigmoid(mm(wgab_ref[:, 0:D_MODEL]))
    gb_ref[...] = jax.nn.sigmoid(mm(wgab_ref[:, D_MODEL:2 * D_MODEL]))
    kvct_ref[0] = mm(wa_ref[:, C_KVC:C_KVC + KV_W]).T
    kvst_ref[0] = kvs.T
    kvwt_ref[0] = kvw.T


def _inproj_sample(x2d, lay, g, w):
    nseq = lay["tm"]
    ts = lay["grid"][1]
    names = ["u", "q", "kvs", "kvw", "gn", "ga", "gb"]
    widths = [SSM_WIDTH, Q_W, KV_W, KV_W, LANES, D_MODEL, D_MODEL]
    out_shapes, out_specs = [], []
    for n, wd in zip(names, widths):
        shp, spec = lay["b" if n == "u" else "a"](wd)
        out_shapes.append(jax.ShapeDtypeStruct(shp, F32))
        out_specs.append(spec)
    for n in ("kvct", "kvst", "kvwt"):
        names.append(n)
        out_shapes.append(jax.ShapeDtypeStruct((ts, KV_W, nseq), F32))
        out_specs.append(pl.BlockSpec((1, KV_W, nseq), lambda s, b: (b, 0, 0)))
    x_shape, x_spec = lay["a"](D_MODEL)
    ops = [g, w["wa"], w["wgn"], w["wgab"]]
    outs = pl.pallas_call(
        _inproj_sample_body,
        grid=lay["grid"],
        in_specs=[x_spec] + [_full(o.shape) for o in ops],
        out_specs=out_specs,
        out_shape=out_shapes,
        compiler_params=_cparams(("arbitrary",) * 2, 56),
        name="inproj_sample",
    )(x2d.reshape(x_shape), *ops)
    return dict(zip(names, outs))


def _inproj_params(w_in0):
    return {"wa": w_in0[:, :N_MAIN].astype(BF16),
            "wgn": jnp.pad(w_in0[:, N_MAIN:N_MAIN + NSA_GATE_W], ((0, 0), (0, LANES - NSA_GATE_W))).astype(BF16),
            "wgab": w_in0[:, N_MAIN + NSA_GATE_W:].astype(BF16)}


def _s5_prompt_body(u_ref, wb_ref, ar_ref, ai_ref, cw_ref, d_ref, wglu_ref, bglu_ref, wso_ref,
                    abr_ref, hlast_ref, lhs_ref, bu_ref, h8_ref, p_ref, hstate_ref):
    c = pl.program_id(0)
    nseq = 4
    r4 = u_ref.shape[1]
    tc = r4 // nseq
    half = tc // 2

    @pl.when(c == 0)
    def _():
        hstate_ref[...] = jnp.zeros_like(hstate_ref)

    u = jnp.concatenate([u_ref[j] for j in range(SSM_WIDTH // LANES)], axis=1)
    row2 = lax.broadcasted_iota(jnp.int32, (r4, SSM_WIDTH), 0)
    lo2 = (row2 % SUBLANES) < nseq
    up = pltpu.roll(u, r4 - nseq, axis=0)
    dn = pltpu.roll(u, nseq, axis=0)
    swapped = jnp.where(lo2, up, dn)
    zero = jnp.zeros_like(u)
    ev_re = jnp.where(lo2, u, zero).astype(BF16).reshape(half, SUBLANES, SSM_WIDTH)
    ev_im = jnp.where(lo2, zero, swapped).astype(BF16).reshape(half, SUBLANES, SSM_WIDTH)
    od_re = jnp.where(lo2, swapped, zero).astype(BF16).reshape(half, SUBLANES, SSM_WIDTH)
    od_im = jnp.where(lo2, zero, u).astype(BF16).reshape(half, SUBLANES, SSM_WIDTH)
    for j in range(4):
        sl = slice(LANES * j, LANES * (j + 1))
        lhs_ref[:, 0:8, 256 * j:256 * j + LANES] = ev_re[:, :, sl]
        lhs_ref[:, 0:8, 256 * j + LANES:256 * (j + 1)] = ev_im[:, :, sl]
        lhs_ref[:, 8:16, 256 * j:256 * j + LANES] = od_re[:, :, sl]
        lhs_ref[:, 8:16, 256 * j + LANES:256 * (j + 1)] = od_im[:, :, sl]
    for j in range(4):
        lhs = lhs_ref[:, :, 256 * j:256 * (j + 1)].reshape(tc * SUBLANES, 256)
        bu_ref[:, 512 * j:512 * (j + 1)] = jnp.dot(lhs, wb_ref[j], preferred_element_type=F32)

    for lc in range(4):
        sl = slice(512 * lc, 512 * (lc + 1))
        ar = ar_ref[:, sl]
        ai = ai_ref[:, sl]

        def step(t, h, sl=sl, ar=ar, ai=ai):
            r0 = pl.multiple_of(t * SUBLANES, SUBLANES)
            h = ar * h + ai * pltpu.roll(h, nseq, axis=0) + bu_ref[pl.ds(r0, SUBLANES), sl]
            h8_ref[pl.ds(r0, SUBLANES), sl] = h
            return h

        hstate_ref[:, sl] = lax.fori_loop(0, tc, step, hstate_ref[:, sl], unroll=8)
    hlast_ref[...] = hstate_ref[...]

    for j in range(4):
        pj = jnp.dot(h8_ref[:, 512 * j:512 * (j + 1)].astype(BF16), cw_ref[j], preferred_element_type=F32)
        p_ref[2 * j] = pj[:, 0:LANES]
        p_ref[2 * j + 1] = pj[:, LANES:2 * LANES]
    ys = []
    for s in range(nseq):
        parts = []
        for j in range(4):
            re = p_ref[2 * j, pl.ds(s, tc, stride=SUBLANES), :]
            im = p_ref[2 * j + 1, pl.ds(nseq + s, tc, stride=SUBLANES), :]
            us = u_ref[j, pl.ds(s, tc, stride=nseq), :]
            parts.append(re + im + d_ref[:, LANES * j:LANES * (j + 1)] * us)
        ys.append(jnp.concatenate(parts, axis=1))
    y = jnp.concatenate(ys, axis=0)
    zg = jax.nn.gelu(y)
    gate = jnp.dot(zg.astype(BF16), wglu_ref[...], preferred_element_type=F32) + bglu_ref[...]
    glu = (zg * jax.nn.sigmoid(gate)).astype(BF16)
    abr = jnp.dot(glu, wso_ref[...], preferred_element_type=F32)
    for s in range(nseq):
        abr_ref[s] = abr[s * tc:(s + 1) * tc]


def _s5_prompt(u_ts, sp, wglu, bglu, wso, t_total, tc):
    nseq = 4
    grid = (t_total // tc,)
    abr, hlast = pl.pallas_call(
        _s5_prompt_body,
        grid=grid,
        in_specs=[pl.BlockSpec((SSM_WIDTH // LANES, tc * nseq, LANES), lambda c: (0, c, 0)),
                  _full(sp["wb8"].shape), _full(sp["ar8"].shape), _full(sp["ai8"].shape), _full(sp["cw8"].shape),
                  _full(sp["d"].shape), _full(wglu.shape), _full(bglu.shape), _full(wso.shape)],
        out_specs=[pl.BlockSpec((nseq, tc, D_MODEL), lambda c: (0, c, 0)),
                   pl.BlockSpec((SUBLANES, N_STATE), lambda c: (0, 0))],
        out_shape=[jax.ShapeDtypeStruct((nseq, t_total, D_MODEL), F32),
                   jax.ShapeDtypeStruct((SUBLANES, N_STATE), F32)],
        scratch_shapes=[pltpu.VMEM((tc // 2, 2 * SUBLANES, 1024), BF16),
                        pltpu.VMEM((tc * SUBLANES, N_STATE), F32),
                        pltpu.VMEM((tc * SUBLANES, N_STATE), F32),
                        pltpu.VMEM((8, tc * SUBLANES, LANES), F32),
                        pltpu.VMEM((SUBLANES, N_STATE), F32)],
        compiler_params=_cparams(("arbitrary",), 56),
        name="s5_prompt",
    )(u_ts, sp["wb8"], sp["ar8"], sp["ai8"], sp["cw8"], sp["d"], wglu, bglu, wso)
    return abr, hlast


def _s5_params(lam_re, lam_im, log_dt, b_re, b_im, c_re, c_im, d_skip):
    lam = lax.complex(lam_re.astype(F32), lam_im.astype(F32))
    dt = jnp.exp(log_dt.astype(F32))[:, None]
    a_bar = jnp.exp(lam * dt)
    b = lax.complex(b_re.astype(F32), b_im.astype(F32))
    b_bar = ((a_bar - 1.0) / lam)[..., None] * b
    eye8 = jnp.eye(8, dtype=F32)

    def bd_b(m):
        return jnp.einsum("ab,jbpc->jacbp", eye8, m.reshape(4, 8, SSM_STATE, SSM_GROUP)).reshape(4, 128, 512)

    def bd_c(m):
        return jnp.einsum("ab,jbcp->japbc", eye8, m.reshape(4, 8, SSM_GROUP, SSM_STATE)).reshape(4, 512, 128)

    wre, wim = bd_b(b_bar.real), bd_b(b_bar.imag)
    cre, cim = bd_c(c_re.astype(F32)), bd_c(c_im.astype(F32))
    ar = a_bar.real.reshape(1, N_STATE)
    ai = a_bar.imag.reshape(1, N_STATE)
    sign = jnp.concatenate([-jnp.ones((4, 1), F32), jnp.ones((4, 1), F32)], axis=0)
    return {
        "wb8": jnp.concatenate([wre, wim], axis=1).astype(BF16),
        "cw8": jnp.concatenate([cre, -cim], axis=2).astype(BF16),
        "ar8": jnp.broadcast_to(ar, (SUBLANES, N_STATE)),
        "ai8": sign * ai,
        "wre": wre.astype(BF16), "wim": wim.astype(BF16),
        "cre": cre.astype(BF16), "cim": cim.astype(BF16),
        "ar": ar, "ai": ai,
        "d": d_skip.astype(F32).reshape(1, SSM_WIDTH),
    }


def _cmp_params(cmp_pe, cmp_w1, cmp_w2):
    eye2 = jnp.eye(2, dtype=F32)
    nhalf = CMP_LEN // CMP_STRIDE
    w1r = cmp_w1.astype(F32).reshape(2, nhalf, CMP_STRIDE, HEAD_DIM, HEAD_DIM)
    wk = jnp.einsum("kside,ph->kipdshe", w1r, eye2).reshape(2, CMP_STRIDE * LANES, nhalf * LANES)
    bk = jnp.einsum("kld,klde->ke", cmp_pe.astype(F32), cmp_w1.astype(F32), precision=lax.Precision.HIGHEST)
    w2k = jnp.einsum("kef,ph->kpehf", cmp_w2.astype(F32), eye2).reshape(2, LANES, LANES)
    return {"wk": wk.astype(BF16), "bk": jnp.tile(bk, (1, N_KV_HEADS)), "w2k": w2k.astype(BF16),
            "w2kt": jnp.swapaxes(w2k, 1, 2).astype(BF16)}


def _compress_hidden(tap, nch, kv, wk_ref, bk_ref):
    x = jnp.concatenate([tap(i).astype(BF16) for i in range(CMP_STRIDE)], axis=1)
    pp = jnp.dot(x, wk_ref[kv], preferred_element_type=F32)
    pre = pp[:, 0:LANES] + pltpu.roll(pp[:, LANES:2 * LANES], nch - 1, axis=0) + bk_ref[kv:kv + 1, :]
    return jax.nn.gelu(pre).astype(BF16)


def _compress_prompt_body(x_ref, wk_ref, bk_ref, w2k_ref, w2kt_ref, ck_ref, cvt_ref):
    nch = x_ref.shape[1] // CMP_STRIDE
    hid = [_compress_hidden(lambda i, kv=kv: x_ref[kv, pl.ds(i, nch, stride=CMP_STRIDE), :], nch, kv, wk_ref, bk_ref)
           for kv in range(2)]
    ck_ref[0] = jnp.dot(hid[0], w2k_ref[0], preferred_element_type=F32).astype(BF16)
    cvt_ref[0] = _dot_t(w2kt_ref[1], hid[1]).astype(BF16)


def _compress_prompt(kvc2, cp, nseq, t):
    nch = t // CMP_STRIDE
    return pl.pallas_call(
        _compress_prompt_body,
        grid=(nseq,),
        in_specs=[pl.BlockSpec((2, t, LANES), lambda n: (0, n, 0)),
                  _full(cp["wk"].shape), _full(cp["bk"].shape), _full(cp["w2k"].shape), _full(cp["w2kt"].shape)],
        out_specs=[pl.BlockSpec((1, nch, LANES), lambda n: (n, 0, 0)),
                   pl.BlockSpec((1, LANES, nch), lambda n: (n, 0, 0))],
        out_shape=[jax.ShapeDtypeStruct((nseq, nch, LANES), BF16),
                   jax.ShapeDtypeStruct((nseq, LANES, nch), BF16)],
        compiler_params=_cparams(("arbitrary",), 48),
        name="compress_prompt",
    )(kvc2, cp["wk"], cp["bk"], cp["w2k"], cp["w2kt"])


def _overlap_t(n_cmp_pad, n_slc_pad):
    j = np.arange(n_cmp_pad)[None, :]
    s = np.arange(n_slc_pad)[:, None]
    ov = (j * CMP_STRIDE <= s * SLC_BLOCK + SLC_BLOCK - 1) & (j * CMP_STRIDE + CMP_LEN - 1 >= s * SLC_BLOCK)
    return jnp.asarray(ov, dtype=BF16)


def _softmax_cols(s, valid):
    sm = jnp.where(valid, s, NEG_INF)
    mx = jnp.max(sm, axis=0, keepdims=True)
    e = jnp.where(valid, jnp.exp2(sm - mx), 0.0)
    l = jnp.sum(e, axis=0, keepdims=True)
    return e * (1.0 / jnp.maximum(l, 1e-30))


def _select_blocks(imp, blk, pos, nblk, axis=0):
    cur = pos // SLC_BLOCK
    forced = (blk == 0) | (blk == cur) | (blk == cur - 1)
    v = jnp.where(forced, imp + FORCE_BONUS, imp)
    v = jnp.where(blk * SLC_BLOCK <= pos, v, NEG_INF)
    v = jnp.where(blk < nblk, v, -3e38)
    blk_f = blk.astype(F32)
    neg = jnp.full(imp.shape, NEG_INF, F32)
    for _ in range(min(TOP_N, nblk)):
        mx = jnp.max(v, axis=axis, keepdims=True)
        first = jnp.min(jnp.where(v == mx, blk_f, float(imp.shape[axis])), axis=axis, keepdims=True)
        pick = blk_f == first
        neg = jnp.where(pick, 0.0, neg)
        v = jnp.where(pick, -3e38, v)
    return neg


CB = 2 * LANES


def _attn_prompt_body(q_ref, gn_ref, ck_ref, cvt_ref, ks_ref, vst_ref, kw_ref, vwt_ref, ovt_ref,
                      o_ref, kaug_ref, kwaug_ref, qaug_ref, acc_ref):
    i = pl.program_id(1)
    t = ks_ref.shape[1]
    nch = ck_ref.shape[1]
    nslc = t // SLC_BLOCK
    qb = Q_BLOCK
    ncol = N_Q_HEADS * qb
    ncb = ncol // CB
    q0 = i * qb
    one_row = 2 * LANES - HEAD_DIM

    @pl.when(i == 0)
    def _():
        kaug_ref[:, 0:LANES] = ks_ref[0]
        blk = lax.broadcasted_iota(jnp.int32, (t, LANES), 0) // SLC_BLOCK
        col = lax.broadcasted_iota(jnp.int32, (t, LANES), 1)
        kaug_ref[:, LANES:2 * LANES] = jnp.where(blk == col, 1.0, 0.0).astype(BF16)
        padcol = lax.broadcasted_iota(jnp.int32, (WINDOW, 2 * LANES), 1)
        kwaug_ref[0:WINDOW, :] = jnp.where(padcol == one_row, NEG_INF, 0.0).astype(BF16)
        kwaug_ref[WINDOW:WINDOW + t, 0:LANES] = kw_ref[0]
        kwaug_ref[WINDOW:WINDOW + t, LANES:2 * LANES] = jnp.zeros((t, LANES), BF16)

    zeros64 = jnp.zeros((HEAD_DIM, qb), BF16)
    for j in range(N_Q_HEADS):
        dst = j // GQA
        qaug_ref[HEAD_DIM * dst:HEAD_DIM * (dst + 1), qb * j:qb * (j + 1)] = q_ref[0, HEAD_DIM * j:HEAD_DIM * (j + 1), :]
        qaug_ref[HEAD_DIM * (1 - dst):HEAD_DIM * (2 - dst), qb * j:qb * (j + 1)] = zeros64
    tail_row = lax.broadcasted_iota(jnp.int32, (HEAD_DIM, ncol), 0)
    qaug_ref[one_row:2 * LANES, :] = jnp.where(tail_row == 0, 1.0, 0.0).astype(BF16)

    pos_c = q0 + (lax.broadcasted_iota(jnp.int32, (nch, CB), 1) & (qb - 1))
    cvalid = lax.broadcasted_iota(jnp.int32, (nch, CB), 0) * CMP_STRIDE + (CMP_LEN - 1) <= pos_c
    ocs, imps = [], []
    for cb in range(ncb):
        sc = jnp.dot(ck_ref[0], qaug_ref[0:LANES, CB * cb:CB * (cb + 1)], preferred_element_type=F32)
        pc = _softmax_cols(sc, cvalid).astype(BF16)
        ocs.append(jnp.dot(cvt_ref[0], pc, preferred_element_type=F32))
        imps.append(jnp.dot(ovt_ref[...], pc, preferred_element_type=F32))
    oc = jnp.concatenate(ocs, axis=1)
    imp = jnp.concatenate(imps, axis=1)
    blk = lax.broadcasted_iota(jnp.int32, (nslc, qb), 0)
    pos_q = q0 + lax.broadcasted_iota(jnp.int32, (nslc, qb), 1)
    for h in range(N_KV_HEADS):
        v = imp[0:nslc, qb * GQA * h:qb * GQA * h + qb]
        for g in range(1, GQA):
            v = v + imp[0:nslc, qb * (GQA * h + g):qb * (GQA * h + g + 1)]
        neg = _select_blocks(v, blk, pos_q, nslc).astype(BF16)
        for g in range(GQA):
            j = GQA * h + g
            qaug_ref[LANES:LANES + nslc, qb * j:qb * (j + 1)] = neg
    if nslc < HEAD_DIM:
        qaug_ref[LANES + nslc:LANES + HEAD_DIM, :] = jnp.zeros((HEAD_DIM - nslc, ncol), BF16)

    brow = lax.broadcasted_iota(jnp.int32, (qb, CB), 0)
    bcol = lax.broadcasted_iota(jnp.int32, (qb, CB), 1) & (qb - 1)
    tri_lo = jnp.where(brow <= bcol, 0.0, NEG_INF)

    acc_ref[...] = jnp.zeros_like(acc_ref)

    def sel_tile(k0, nk, vt, carry, bias):
        m, l = carry
        ka = kaug_ref[pl.ds(k0, nk), :]
        css = [slice(CB * cb, CB * (cb + 1)) for cb in range(ncb)]
        ss = [jnp.dot(ka, qaug_ref[:, cs], preferred_element_type=F32) for cs in css]
        ms, ls, ps, alphas = [], [], [], []
        for cs, s in zip(css, ss):
            if bias is not None:
                s = s + bias
            mn = jnp.maximum(m[:, cs], jnp.max(s, axis=0, keepdims=True))
            alpha = jnp.exp2(m[:, cs] - mn)
            p = jnp.exp2(s - mn)
            ms.append(mn)
            ls.append(alpha * l[:, cs] + jnp.sum(p, axis=0, keepdims=True))
            ps.append(p.astype(BF16))
            alphas.append(alpha)
        pvs = [jnp.dot(vt, p, preferred_element_type=F32) for p in ps]
        for cs, alpha, pv in zip(css, alphas, pvs):
            acc_ref[:, cs] = alpha * acc_ref[:, cs] + pv
        return jnp.concatenate(ms, axis=1), jnp.concatenate(ls, axis=1)

    def vt_blocks(ref, b0, n):
        return jnp.concatenate([ref[0, b0 + j] for j in range(n)], axis=1) if n > 1 else ref[0, b0]

    big_blocks, q_blocks = TK_SLC // TK_WIN, qb // TK_WIN

    def big_tile(kt, carry):
        return sel_tile(pl.multiple_of(kt * TK_SLC, TK_SLC), TK_SLC, vt_blocks(vst_ref, kt * big_blocks, big_blocks),
                        carry, None)

    def small_tile(kb, carry):
        return sel_tile(pl.multiple_of(kb * qb, qb), qb, vt_blocks(vst_ref, kb * q_blocks, q_blocks), carry, None)

    carry = (jnp.full((1, ncol), NEG_INF, F32), jnp.zeros((1, ncol), F32))
    nbig = q0 // TK_SLC
    carry = lax.fori_loop(0, nbig, big_tile, carry)
    carry = lax.fori_loop(nbig * (TK_SLC // qb), i, small_tile, carry)
    _, l = sel_tile(pl.multiple_of(q0, qb), qb, vt_blocks(vst_ref, i * q_blocks, q_blocks), carry, tri_lo)
    osel = acc_ref[...] * (1.0 / l)

    npiece = (WINDOW + qb) // TK_WIN
    kws = [kwaug_ref[pl.ds(pl.multiple_of(q0 + w * TK_WIN, TK_WIN), TK_WIN), :] for w in range(npiece)]
    vwt = jnp.concatenate([vwt_ref[0, jnp.maximum(i * q_blocks + w - WINDOW // TK_WIN, 0)] for w in range(npiece)],
                          axis=1)
    wrow = lax.broadcasted_iota(jnp.int32, (TK_WIN, CB), 0)
    wcol = lax.broadcasted_iota(jnp.int32, (TK_WIN, CB), 1) & (qb - 1)
    wbias = []
    for w in range(npiece):
        lo, hi = w * TK_WIN - WINDOW, w * TK_WIN - WINDOW + TK_WIN - 1
        if hi <= 0 and qb - 1 - lo < WINDOW:
            wbias.append(None)
        else:
            dlt = wcol - wrow - lo
            wbias.append(jnp.where((dlt >= 0) & (dlt < WINDOW), 0.0, NEG_INF))
    ows = []
    for cb in range(ncb):
        qa = qaug_ref[:, CB * cb:CB * (cb + 1)]
        sw = [jnp.dot(kws[w], qa, preferred_element_type=F32) for w in range(npiece)]
        sw = [x if b is None else x + b for x, b in zip(sw, wbias)]
        s = jnp.concatenate(sw, axis=0)
        e = jnp.exp2(s - jnp.max(s, axis=0, keepdims=True))
        ows.append(jnp.dot(vwt, e.astype(BF16), preferred_element_type=F32) * (1.0 / jnp.sum(e, axis=0, keepdims=True)))
    ow = jnp.concatenate(ows, axis=1)

    gt = gn_ref[0]
    for c in range(N_Q_HEADS // 2):
        rows = []
        for hh in range(2):
            j = 2 * c + hh
            rs = slice(HEAD_DIM * (j // GQA), HEAD_DIM * (j // GQA + 1))
            cs = slice(qb * j, qb * (j + 1))
            rows.append(gt[3 * j:3 * j + 1, :] * oc[rs, cs] + gt[3 * j + 1:3 * j + 2, :] * osel[rs, cs]
                        + gt[3 * j + 2:3 * j + 3, :] * ow[rs, cs])
        o_ref[:, LANES * c:LANES * (c + 1)] = jnp.concatenate(rows, axis=0).T.astype(o_ref.dtype)


def _attn_prompt(q, gn, ck, cvt, ksb, vst, kwb, vwt, nseq, t):
    nb = t // Q_BLOCK
    nch = t // CMP_STRIDE
    nslc = t // SLC_BLOCK
    ovt = _overlap_t(nch, max(nslc, SUBLANES))
    row = lambda n, i: (n * nb + i, 0)
    seq3 = lambda n, i: (n, 0, 0)
    seq4 = lambda n, i: (n, 0, 0, 0)
    col3 = lambda n, i: (n, 0, i)
    return pl.pallas_call(
        _attn_prompt_body,
        grid=(nseq, nb),
        in_specs=[pl.BlockSpec((1, Q_W, Q_BLOCK), col3), pl.BlockSpec((1, gn.shape[1], Q_BLOCK), col3),
                  pl.BlockSpec((1, nch, LANES), seq3), pl.BlockSpec((1, LANES, nch), seq3),
                  pl.BlockSpec((1, t, LANES), seq3), pl.BlockSpec((1, t // TK_WIN, LANES, TK_WIN), seq4),
                  pl.BlockSpec((1, t, LANES), seq3), pl.BlockSpec((1, t // TK_WIN, LANES, TK_WIN), seq4),
                  _full(ovt.shape)],
        out_specs=pl.BlockSpec((Q_BLOCK, Q_W), row),
        out_shape=jax.ShapeDtypeStruct((nseq * t, Q_W), BF16),
        scratch_shapes=[pltpu.VMEM((t, 2 * LANES), BF16),
                        pltpu.VMEM((WINDOW + t, 2 * LANES), BF16),
                        pltpu.VMEM((2 * LANES, N_Q_HEADS * Q_BLOCK), BF16),
                        pltpu.VMEM((LANES, N_Q_HEADS * Q_BLOCK), F32)],
        compiler_params=_cparams(("arbitrary", "arbitrary"), 56),
        name="attn_prompt",
    )(q, gn, ck, cvt, ksb.reshape(nseq, t, LANES), vst, kwb.reshape(nseq, t, LANES), vwt, ovt)


def _post_body(x_ref, abr_ref, on_ref, ga_ref, gb_ref, wno_ref, wo_ref, g2_ref, x1_ref, h2_ref):
    bbr = jnp.dot(on_ref[...].astype(BF16), wno_ref[...], preferred_element_type=F32)
    merged = ga_ref[...] * abr_ref[...] + gb_ref[...] * bbr
    x1 = x_ref[...] + jnp.dot(merged.astype(BF16), wo_ref[...], preferred_element_type=F32)
    x1_ref[...] = x1
    inv = lax.rsqrt(jnp.mean(x1 * x1, axis=-1, keepdims=True) + RMS_EPS)
    h2_ref[...] = (x1 * inv * g2_ref[...]).astype(BF16)


def _post(x2d, abr, abr_lay, onsa, ga, gb, wno, wo, g2, lay, out_lay):
    x_shape, x_spec = lay["a"](D_MODEL)
    abr_shape, abr_spec = lay[abr_lay](D_MODEL)
    on_shape, on_spec = lay["a"](Q_W)
    o_shape, o_spec = lay[out_lay](D_MODEL)
    return pl.pallas_call(
        _post_body,
        grid=lay["grid"],
        in_specs=[x_spec, abr_spec, on_spec, x_spec, x_spec, _full(wno.shape), _full(wo.shape), _full(g2.shape)],
        out_specs=[o_spec, o_spec],
        out_shape=[jax.ShapeDtypeStruct(o_shape, F32), jax.ShapeDtypeStruct(o_shape, BF16)],
        compiler_params=_cparams(("arbitrary",) * len(lay["grid"]), 48),
        name="post",
    )(x2d.reshape(x_shape), abr.reshape(abr_shape), onsa.reshape(on_shape), ga, gb, wno, wo, g2)


def _route(logits):
    lane = lax.broadcasted_iota(jnp.int32, logits.shape, 1).astype(F32)
    big = float(LANES)
    glog = jnp.where(lane < N_EXPERT_GROUPS, logits, -jnp.inf)
    gmax = jnp.max(glog, axis=1, keepdims=True)
    gsel = jnp.min(jnp.where(glog == gmax, lane, big), axis=1, keepdims=True)
    gw = 1.0 / jnp.sum(jnp.exp(glog - gmax), axis=1, keepdims=True)
    lo = N_EXPERT_GROUPS + EXPERTS_PER_GROUP * gsel
    el = jnp.where((lane >= lo) & (lane < lo + EXPERTS_PER_GROUP), logits, -jnp.inf)
    v1 = jnp.max(el, axis=1, keepdims=True)
    i1 = jnp.min(jnp.where(el == v1, lane, big), axis=1, keepdims=True)
    el2 = jnp.where(lane == i1, -jnp.inf, el)
    v2 = jnp.max(el2, axis=1, keepdims=True)
    i2 = jnp.min(jnp.where(el2 == v2, lane, big), axis=1, keepdims=True)
    e2 = jnp.exp(v2 - v1)
    w1 = gw / (1.0 + e2)
    return jnp.where(lane == i1, w1, 0.0) + jnp.where(lane == i2, w1 * e2, 0.0)


def _moe_body(x1_ref, h2_ref, p_ref, wr_ref, br_ref, wg_ref, wu_ref, wd_ref, wpg_ref, wp_ref, gf_ref,
              y_ref, acc_ref, comb_ref, *, tsplit):
    g = pl.program_id(1)
    h2 = h2_ref[...]

    @pl.when(g == 0)
    def _():
        logits = jnp.dot(h2, wr_ref[...], preferred_element_type=F32) + br_ref[...]
        comb_ref[...] = _route(logits)
        acc_ref[...] = jnp.zeros_like(acc_ref)

    comb = comb_ref[...]
    lane = lax.broadcasted_iota(jnp.int32, comb.shape, 1)
    acc = acc_ref[...]
    for k in range(EXPERTS_PER_GROUP):
        e_lane = N_EXPERT_GROUPS + EXPERTS_PER_GROUP * g + k
        ce = jnp.sum(jnp.where(lane == e_lane, comb, 0.0), axis=1, keepdims=True)
        a = jnp.dot(h2, wg_ref[k], preferred_element_type=F32)
        b = jnp.dot(h2, wu_ref[k], preferred_element_type=F32)
        act = (jax.nn.silu(a) * b * ce).astype(BF16)
        acc = acc + jnp.dot(act, wd_ref[k], preferred_element_type=F32)
    acc_ref[...] = acc

    @pl.when(g == N_EXPERT_GROUPS - 1)
    def _():
        x2 = x1_ref[...] + acc_ref[...]
        rows = x2.shape[0] // tsplit
        if tsplit == 1:
            p = p_ref[...]
        else:
            p = jnp.concatenate([p_ref[:, PLE_DIM * t:PLE_DIM * (t + 1)] for t in range(tsplit)], axis=0)
        gate = jax.nn.sigmoid(jnp.dot(x2.astype(BF16), wpg_ref[...], preferred_element_type=F32))
        x3 = x2 + gate * jnp.dot(p.astype(BF16), wp_ref[...], preferred_element_type=F32)
        inv = lax.rsqrt(jnp.mean(x3 * x3, axis=-1, keepdims=True) + RMS_EPS)
        y = x3 * inv * gf_ref[...]
        if tsplit == 1:
            y_ref[...] = y
        else:
            for t in range(tsplit):
                y_ref[:, D_MODEL * t:D_MODEL * (t + 1)] = y[rows * t:rows * (t + 1)]


def _moe(x1, h2, p, mp, tm, tsplit):
    rows = x1.shape[0]
    nrb = rows // tm
    rb = lambda r, g: (r, 0)
    grp = lambda r, g: (g, 0, 0)
    if tsplit == 1:
        p_spec = pl.BlockSpec((tm, PLE_DIM), rb)
        y_spec = pl.BlockSpec((tm, D_MODEL), rb)
        y_shape = (rows, D_MODEL)
    else:
        assert nrb == 1
        p_spec = _full(p.shape)
        y_shape = (rows // tsplit, tsplit * D_MODEL)
        y_spec = _full(y_shape)
    return pl.pallas_call(
        functools.partial(_moe_body, tsplit=tsplit),
        grid=(nrb, N_EXPERT_GROUPS),
        in_specs=[pl.BlockSpec((tm, D_MODEL), rb), pl.BlockSpec((tm, D_MODEL), rb), p_spec,
                  _full(mp["wr"].shape), _full(mp["br"].shape),
                  pl.BlockSpec((EXPERTS_PER_GROUP, D_MODEL, D_FF_EXPERT), grp),
                  pl.BlockSpec((EXPERTS_PER_GROUP, D_MODEL, D_FF_EXPERT), grp),
                  pl.BlockSpec((EXPERTS_PER_GROUP, D_FF_EXPERT, D_MODEL), grp),
                  _full(mp["wpg"].shape), _full(mp["wp"].shape), _full(mp["gf"].shape)],
        out_specs=y_spec,
        out_shape=jax.ShapeDtypeStruct(y_shape, F32),
        scratch_shapes=[pltpu.VMEM((tm, D_MODEL), F32), pltpu.VMEM((tm, LANES), F32)],
        compiler_params=_cparams(("arbitrary", "arbitrary"), 56),
        name="moe_ple",
    )(x1, h2, p, mp["wr"], mp["br"], mp["wg"], mp["wu"], mp["wd"], mp["wpg"], mp["wp"], mp["gf"])


def _s5_sample_body(u_ref, h0re_ref, h0im_ref, wre_ref, wim_ref, ar_ref, ai_ref, cre_ref, cim_ref, d_ref,
                    wglu_ref, bglu_ref, wso_ref, abr_ref, hre_out_ref, him_out_ref,
                    bure_ref, buim_ref, hre_ref, him_ref, *, nseq, nstep):
    u = u_ref[...]
    ub = u.astype(BF16)
    for j in range(4):
        lhs = ub[:, LANES * j:LANES * (j + 1)]
        bure_ref[:, 512 * j:512 * (j + 1)] = jnp.dot(lhs, wre_ref[j], preferred_element_type=F32)
        buim_ref[:, 512 * j:512 * (j + 1)] = jnp.dot(lhs, wim_ref[j], preferred_element_type=F32)
    for lc in range(4):
        sl = slice(512 * lc, 512 * (lc + 1))
        ar = jnp.broadcast_to(ar_ref[:, sl], (SUBLANES, 512))
        ai = jnp.broadcast_to(ai_ref[:, sl], (SUBLANES, 512))

        def body(rc, carry, sl=sl, ar=ar, ai=ai):
            r0 = pl.multiple_of(rc * SUBLANES, SUBLANES)
            hr = h0re_ref[pl.ds(r0, SUBLANES), sl]
            hi = h0im_ref[pl.ds(r0, SUBLANES), sl]
            for t in range(nstep):
                rr = pl.multiple_of(t * nseq + rc * SUBLANES, SUBLANES)
                hr, hi = (ar * hr - ai * hi + bure_ref[pl.ds(rr, SUBLANES), sl],
                          ar * hi + ai * hr + buim_ref[pl.ds(rr, SUBLANES), sl])
                hre_ref[pl.ds(rr, SUBLANES), sl] = hr
                him_ref[pl.ds(rr, SUBLANES), sl] = hi
            hre_out_ref[pl.ds(r0, SUBLANES), sl] = hr
            him_out_ref[pl.ds(r0, SUBLANES), sl] = hi
            return carry

        lax.fori_loop(0, nseq // SUBLANES, body, 0)
    parts = []
    for j in range(4):
        sl = slice(512 * j, 512 * (j + 1))
        parts.append(jnp.dot(hre_ref[:, sl].astype(BF16), cre_ref[j], preferred_element_type=F32)
                     - jnp.dot(him_ref[:, sl].astype(BF16), cim_ref[j], preferred_element_type=F32))
    y = jnp.concatenate(parts, axis=1) + d_ref[...] * u
    zg = jax.nn.gelu(y)
    gate = jnp.dot(zg.astype(BF16), wglu_ref[...], preferred_element_type=F32) + bglu_ref[...]
    glu = (zg * jax.nn.sigmoid(gate)).astype(BF16)
    abr_ref[...] = jnp.dot(glu, wso_ref[...], preferred_element_type=F32)


def _s5_sample(u_ts, h0re, h0im, sp, wglu, bglu, wso, nseq, nstep):
    rows = nseq * nstep
    ops = [u_ts, h0re, h0im, sp["wre"], sp["wim"], sp["ar"], sp["ai"], sp["cre"], sp["cim"], sp["d"], wglu, bglu, wso]
    return pl.pallas_call(
        functools.partial(_s5_sample_body, nseq=nseq, nstep=nstep),
        grid=(1,),
        in_specs=[_full(o.shape) for o in ops],
        out_specs=[_full((rows, D_MODEL)), _full((nseq, N_STATE)), _full((nseq, N_STATE))],
        out_shape=[jax.ShapeDtypeStruct((rows, D_MODEL), F32),
                   jax.ShapeDtypeStruct((nseq, N_STATE), F32), jax.ShapeDtypeStruct((nseq, N_STATE), F32)],
        scratch_shapes=[pltpu.VMEM((rows, N_STATE), F32) for _ in range(4)],
        compiler_params=_cparams(("arbitrary",), 56),
        name="s5_sample",
    )(*ops)


def _softmax_rows(s, valid):
    sm = jnp.where(valid, s, NEG_INF)
    mx = jnp.max(sm, axis=1, keepdims=True)
    e = jnp.where(valid, jnp.exp2(sm - mx), 0.0)
    l = jnp.sum(e, axis=1, keepdims=True)
    return e * (1.0 / jnp.maximum(l, 1e-30))


SAMPLE_SEQS_PER_STEP = 4
CMP_PITCH = 24


def _attn_sample_body(pt_ref, q_ref, gn_ref, nks_ref, nkw_ref, wint_ref, wk_ref, bk_ref, w2k_ref, ov_ref, e_ref, *rest,
                      npage, past_len, nsub, tq):
    o_ref, nwint_ref, xrow_ref = rest[2 * nsub * npage:]
    nrow = N_Q_HEADS * tq
    nwin = wint_ref.shape[2]
    nslc = -(-(past_len + tq) // SLC_BLOCK)
    nch = past_len // CMP_STRIDE
    per_page = PAGE_SIZE // CMP_STRIDE

    cmp = []
    for kv in range(2):
        for s in range(nsub):
            for p in range(npage):
                rows = rest[s * npage + p][0, LANES * kv:LANES * (kv + 1), :].T
                for c in range(per_page):
                    r0 = CMP_PITCH * (per_page * p + c)
                    xrow_ref[s, r0:r0 + CMP_STRIDE, :] = rows[CMP_STRIDE * c:CMP_STRIDE * (c + 1)]
        x = jnp.concatenate(
            [jnp.concatenate([xrow_ref[s, pl.ds(i, nch, stride=CMP_PITCH), :] for i in range(CMP_STRIDE)], axis=1)
             for s in range(nsub)], axis=0)
        pp = jnp.dot(x.astype(BF16), wk_ref[kv], preferred_element_type=F32)
        pre = pp[:, 0:LANES] + pltpu.roll(pp[:, LANES:2 * LANES], nsub * nch - 1, axis=0) + bk_ref[kv:kv + 1, :]
        cmp.append(jnp.dot(jax.nn.gelu(pre).astype(BF16), w2k_ref[kv], preferred_element_type=F32).astype(BF16))
    cks = [cmp[0][nch * s:nch * (s + 1)] for s in range(nsub)]
    cvs = [cmp[1][nch * s:nch * (s + 1)] for s in range(nsub)]
    slc_pages = [rest[(nsub + s) * npage:(nsub + s + 1) * npage] for s in range(nsub)]
    seqs = range(nsub)
    rcat = lambda parts: jnp.concatenate(parts, axis=0)

    lane_w = lax.broadcasted_iota(jnp.int32, (KV_W, LANES), 1)
    lane8 = lax.broadcasted_iota(jnp.int32, (tq, LANES), 1)
    nks_l, nkw_l, wint_l, qs_l = [], [], [], []
    for s in seqs:
        rows_s = slice(tq * s, tq * (s + 1))
        nks_l.append(jnp.concatenate([nks_ref[rows_s, :], jnp.zeros((LANES - tq, KV_W), F32)], axis=0))
        nkw = jnp.concatenate([nkw_ref[rows_s, :], jnp.zeros((LANES - tq, KV_W), F32)], axis=0)
        nkw_l.append(nkw)
        wint = wint_ref[s]
        wint_l.append(wint)
        shifted = pltpu.roll(wint, nwin - tq, axis=1)
        new_t = pltpu.roll(nkw.T, LANES - tq, axis=1)
        nwint_ref[s, :, 0:nwin - LANES] = shifted[:, 0:nwin - LANES]
        nwint_ref[s, :, nwin - LANES:nwin] = jnp.where(lane_w >= LANES - tq, new_t, shifted[:, nwin - LANES:nwin])
        q = q_ref[rows_s, :]
        qrows = []
        for j in range(N_Q_HEADS):
            chunk = q[:, LANES * (j // 2):LANES * (j // 2 + 1)]
            dst = j // GQA
            if (j % 2) != dst:
                chunk = pltpu.roll(chunk, HEAD_DIM, axis=1)
            keep = (lane8 < HEAD_DIM) if dst == 0 else (lane8 >= HEAD_DIM)
            qrows.append(jnp.where(keep, chunk, 0.0))
        qs_l.append(jnp.concatenate(qrows, axis=0).astype(BF16))

    rtot = nsub * nrow
    seq_rows = [slice(nrow * s, nrow * (s + 1)) for s in seqs]
    pos = past_len + (lax.broadcasted_iota(jnp.int32, (rtot, LANES), 0) & (tq - 1))
    lane = lax.broadcasted_iota(jnp.int32, (rtot, LANES), 1)

    sc = rcat([_dot_t(qs_l[s], cks[s]) for s in seqs])
    pc = _softmax_rows(sc, lane * CMP_STRIDE + (CMP_LEN - 1) <= pos).astype(BF16)
    oc = rcat([jnp.dot(pc[seq_rows[s]], cvs[s], preferred_element_type=F32) for s in seqs])
    imp = jnp.dot(pc, ov_ref[...], preferred_element_type=F32)
    vs = []
    for s in seqs:
        for h in range(N_KV_HEADS):
            r0 = nrow * s + tq * GQA * h
            v = imp[r0:r0 + tq]
            for g in range(1, GQA):
                v = v + imp[r0 + tq * g:r0 + tq * (g + 1)]
            vs.append(v)
    nsel = len(vs) * tq
    vt = rcat(vs + [jnp.zeros((LANES - nsel, LANES), F32)]).T
    nblk_pad = -(-nslc // SUBLANES) * SUBLANES
    blk_t = lax.broadcasted_iota(jnp.int32, (nblk_pad, LANES), 0)
    pos_t = past_len + (lax.broadcasted_iota(jnp.int32, (nblk_pad, LANES), 1) & (tq - 1))
    neg_t = _select_blocks(vt[0:nblk_pad], blk_t, pos_t, nslc, axis=0)
    neg = rcat([neg_t, jnp.zeros((LANES - nblk_pad, LANES), F32)]).T
    negsel = rcat([neg[tq * (N_KV_HEADS * s + j // GQA):tq * (N_KV_HEADS * s + j // GQA + 1)]
                   for s in seqs for j in range(N_Q_HEADS)])
    negsel_b = negsel.astype(BF16)

    new_blk = past_len // SLC_BLOCK
    ss_l = []
    for s in seqs:
        qaug = jnp.concatenate([qs_l[s], negsel_b[seq_rows[s]]], axis=1)
        parts = []
        for p in range(0, npage, 2):
            kt = jnp.concatenate([slc_pages[s][p][0][0:LANES], slc_pages[s][p + 1][0][0:LANES]], axis=1).astype(BF16)
            et = jnp.concatenate([e_ref[p], e_ref[p + 1]], axis=1)
            parts.append(jnp.dot(qaug, jnp.concatenate([kt, et], axis=0), preferred_element_type=F32))
        parts.append(_dot_t(qs_l[s], nks_l[s][:, 0:LANES].astype(BF16)) + negsel[seq_rows[s], new_blk:new_blk + 1])
        ss_l.append(jnp.concatenate(parts, axis=1))
    ss = rcat(ss_l)
    nkeys = ss.shape[1]
    kpos = lax.broadcasted_iota(jnp.int32, (rtot, nkeys), 1)
    pos_k = past_len + (lax.broadcasted_iota(jnp.int32, (rtot, nkeys), 0) & (tq - 1))
    ps = _softmax_rows(ss, kpos <= pos_k).astype(BF16)
    osel_l = []
    for s in seqs:
        psq = ps[seq_rows[s]]
        o = jnp.dot(psq[:, past_len:nkeys], nks_l[s][:, LANES:2 * LANES].astype(BF16), preferred_element_type=F32)
        for p in range(0, npage, 2):
            vtp = jnp.concatenate([slc_pages[s][p][0][LANES:2 * LANES], slc_pages[s][p + 1][0][LANES:2 * LANES]],
                                  axis=1).astype(BF16)
            o = o + _dot_t(psq[:, PAGE_SIZE * p:PAGE_SIZE * (p + 2)], vtp)
        osel_l.append(o)
    osel = rcat(osel_l)

    sw = rcat([jnp.concatenate([jnp.dot(qs_l[s], wint_l[s][0:LANES].astype(BF16), preferred_element_type=F32),
                                _dot_t(qs_l[s], nkw_l[s][:, 0:LANES].astype(BF16))], axis=1) for s in seqs])
    nw = sw.shape[1]
    widx = lax.broadcasted_iota(jnp.int32, (rtot, nw), 1)
    pos_w = past_len + (lax.broadcasted_iota(jnp.int32, (rtot, nw), 0) & (tq - 1))
    dlt = pos_w - (past_len - nwin + widx)
    pw = _softmax_rows(sw, (dlt >= 0) & (dlt < WINDOW) & (widx < nwin + tq)).astype(BF16)
    ow = rcat([_dot_t(pw[seq_rows[s], 0:nwin], wint_l[s][LANES:2 * LANES].astype(BF16))
               + jnp.dot(pw[seq_rows[s], nwin:nw], nkw_l[s][:, LANES:2 * LANES].astype(BF16),
                         preferred_element_type=F32) for s in seqs])

    for s in seqs:
        rows_s = slice(tq * s, tq * (s + 1))
        gn = gn_ref[rows_s, :]
        for c in range(N_Q_HEADS // 2):
            halves = []
            for hh in range(2):
                j = 2 * c + hh
                rs = slice(nrow * s + tq * j, nrow * s + tq * (j + 1))
                oj = (gn[:, 3 * j:3 * j + 1] * oc[rs] + gn[:, 3 * j + 1:3 * j + 2] * osel[rs]
                      + gn[:, 3 * j + 2:3 * j + 3] * ow[rs])
                if (j // GQA) != hh:
                    oj = pltpu.roll(oj, HEAD_DIM, axis=1)
                halves.append(oj)
            o_ref[rows_s, LANES * c:LANES * (c + 1)] = jnp.where(lane8 < HEAD_DIM, halves[0], halves[1])


def _attn_sample(q, gn, nks, nkw, cache_cmp, cache_slc, cache_win, page_table, cp, nseq, tq, past_len):
    assert tq <= CMP_STRIDE and past_len % PAGE_SIZE == 0
    npage = past_len // PAGE_SIZE
    assert npage % 2 == 0 and PAGE_SIZE == LANES
    n_pool = cache_cmp.shape[0]
    nwin = cache_win.shape[1]
    chunks = past_len // CMP_STRIDE
    ov = _overlap_t(chunks, LANES).T
    key = np.arange(past_len).reshape(npage, 1, PAGE_SIZE)
    e = jnp.asarray(np.arange(LANES).reshape(1, LANES, 1) == key // SLC_BLOCK, dtype=BF16)
    to_t = lambda c: jnp.transpose(c, (0, 2, 3, 4, 1)).reshape(c.shape[0], KV_W, c.shape[1])
    cmp_t, slc_t, win_t = to_t(cache_cmp), to_t(cache_slc), to_t(cache_win)
    nsub = SAMPLE_SEQS_PER_STEP
    assert nseq % nsub == 0
    row = lambda n, pt: (n, 0)
    seq3 = lambda n, pt: (n, 0, 0)
    page = lambda s, p: (lambda n, pt: (pt[n * nsub + s, p], 0, 0))
    consts = [cp["wk"], cp["bk"], cp["w2k"], ov, e]
    in_specs = [pl.BlockSpec((nsub * tq, Q_W), row), pl.BlockSpec((nsub * tq, LANES), row),
                pl.BlockSpec((nsub * tq, KV_W), row), pl.BlockSpec((nsub * tq, KV_W), row),
                pl.BlockSpec((nsub, KV_W, nwin), seq3)]
    in_specs += [pl.BlockSpec(c.shape, (lambda nd: lambda n, pt: (0,) * nd)(c.ndim)) for c in consts]
    pages = [pl.BlockSpec((1, KV_W, PAGE_SIZE), page(s, p)) for s in range(nsub) for p in range(npage)]
    in_specs += pages * 2
    grid_spec = pltpu.PrefetchScalarGridSpec(
        num_scalar_prefetch=1,
        grid=(nseq // nsub,),
        in_specs=in_specs,
        out_specs=[pl.BlockSpec((nsub * tq, Q_W), row), pl.BlockSpec((nsub, KV_W, nwin), seq3)],
        scratch_shapes=[pltpu.VMEM((nsub, chunks * CMP_PITCH, LANES), F32)],
    )
    return pl.pallas_call(
        functools.partial(_attn_sample_body, npage=npage, past_len=past_len, nsub=nsub, tq=tq),
        grid_spec=grid_spec,
        out_shape=[jax.ShapeDtypeStruct((nseq * tq, Q_W), F32), jax.ShapeDtypeStruct((nseq, KV_W, nwin), F32)],
        compiler_params=_cparams(("arbitrary",), 56),
        name="attn_sample",
    )(page_table, q, gn, nks, nkw, win_t, *consts, *([cmp_t] * (nsub * npage)), *([slc_t] * (nsub * npage)))


def _moe_params(w_rg, b_rg, w_re, b_re, w_gate, w_up, w_down, w_ple, w_ple_gate, gf):
    pad = LANES - N_EXPERT_GROUPS - N_EXPERTS
    return {"wr": jnp.pad(jnp.concatenate([w_rg, w_re], axis=1), ((0, 0), (0, pad))).astype(BF16),
            "br": jnp.pad(jnp.concatenate([b_rg, b_re]), (0, pad)).astype(F32).reshape(1, LANES),
            "wg": w_gate.astype(BF16), "wu": w_up.astype(BF16), "wd": w_down.astype(BF16),
            "wpg": w_ple_gate.astype(BF16), "wp": w_ple.astype(BF16), "gf": gf.astype(F32).reshape(1, D_MODEL)}


TM_PROMPT = 512
TC_S5 = 128


def kernel(x_prompt, x_sample, p_prompt, p_sample, cache_cmp_kv, cache_slc_kv, cache_win_kv, state_ssm, page_table, norm1_g, w_in, ssm_lam_re, ssm_lam_im, ssm_log_dt, ssm_b_re, ssm_b_im, ssm_c_re, ssm_c_im, ssm_d, w_glu, b_glu, cmp_pe, cmp_w1, cmp_w2, w_ssm_out, w_nsa_out, w_o, norm2_g, w_route_group, b_route_group, w_route_expert, b_route_expert, w_exp_gate, w_exp_up, w_exp_down, w_ple, w_ple_gate, final_norm_g):
    assert w_in.shape[0] == 1, "one layer"
    l = 0
    nb, t = x_prompt.shape[:2]
    ns, ts = x_sample.shape[:2]
    past_len = page_table.shape[1] * PAGE_SIZE
    kvt = (2, N_KV_HEADS, HEAD_DIM)

    wi = _inproj_params(w_in[l])
    g1 = norm1_g[l].astype(F32).reshape(1, D_MODEL)
    g2 = norm2_g[l].astype(F32).reshape(1, D_MODEL)
    sp = _s5_params(ssm_lam_re[l], ssm_lam_im[l], ssm_log_dt[l], ssm_b_re[l], ssm_b_im[l], ssm_c_re[l], ssm_c_im[l],
                    ssm_d[l])
    cp = _cmp_params(cmp_pe[l], cmp_w1[l], cmp_w2[l])
    mp = _moe_params(w_route_group[l], b_route_group[l], w_route_expert[l], b_route_expert[l], w_exp_gate[l],
                     w_exp_up[l], w_exp_down[l], w_ple[l], w_ple_gate[l], final_norm_g)
    wglu = w_glu[l].astype(BF16)
    bglu = b_glu[l].astype(F32).reshape(1, SSM_WIDTH)
    wso = w_ssm_out[l].astype(BF16)
    wno = w_nsa_out[l].astype(BF16)
    wo = w_o[l].astype(BF16)

    lay = _prompt_layout(nb, t, TM_PROMPT)
    xp = x_prompt.reshape(nb * t, D_MODEL)
    r = _inproj_prompt(xp, lay, g1, wi)
    abr, hlast = _s5_prompt(r["u"], sp, wglu, bglu, wso, t, TC_S5)
    ck, cvt = _compress_prompt(r["kvc"], cp, nb, t)
    onsa = _attn_prompt(r["qt"], r["gnt"], ck, cvt, r["ksb"], r["vst"], r["kwb"], r["vwt"], nb, t)
    x1, h2 = _post(xp, abr, "a", onsa, r["ga"], r["gb"], wno, wo, g2, lay, "a")
    y_prompt = _moe(x1, h2, p_prompt[l].reshape(nb * t, PLE_DIM), mp, TM_PROMPT, 1).reshape(nb, t, D_MODEL)
    keep = min(WINDOW, t)

    def rows_last(a):
        return jnp.transpose(a.reshape((a.shape[0],) + kvt + (a.shape[2],)), (0, 4, 1, 2, 3))[None]

    new_cmp_p = rows_last(r["kvct"])
    new_slc_p = rows_last(r["kvst"])
    new_win_p = rows_last(r["kvwt"][:, :, t - keep:])
    new_ssm_p = jnp.stack([hlast[0:nb], hlast[nb:2 * nb]], axis=-1).reshape(1, nb, N_SSM_GROUPS, SSM_STATE, 2)

    lays = _sample_layout(ns, ts)
    xs = x_sample.reshape(ns * ts, D_MODEL)
    rs = _inproj_sample(xs, lays, g1, wi)
    h0 = state_ssm[l].astype(F32).reshape(ns, N_STATE, 2)
    abr_s, hre, him = _s5_sample(rs["u"], h0[..., 0], h0[..., 1], sp, wglu, bglu, wso, ns, ts)
    onsa_s, new_win = _attn_sample(rs["q"].reshape(ns * ts, Q_W), rs["gn"].reshape(ns * ts, LANES),
                                   rs["kvs"].reshape(ns * ts, KV_W), rs["kvw"].reshape(ns * ts, KV_W),
                                   cache_cmp_kv[l], cache_slc_kv[l], cache_win_kv[l], page_table, cp, ns, ts, past_len)
    x1s, h2s = _post(xs, abr_s, "b", onsa_s, rs["ga"], rs["gb"], wno, wo, g2, lays, "b")
    y_sample = _moe(x1s, h2s, p_sample[l].reshape(ns, ts * PLE_DIM), mp, ns * ts, ts).reshape(ns, ts, D_MODEL)
    steps_first = lambda a: jnp.transpose(a.reshape((ts,) + kvt + (ns,)), (4, 0, 1, 2, 3))[None]
    new_cmp_s = steps_first(rs["kvct"])
    new_slc_s = steps_first(rs["kvst"])
    new_win_s = rows_last(new_win)
    new_ssm_s = jnp.stack([hre, him], axis=-1).reshape(1, ns, N_SSM_GROUPS, SSM_STATE, 2)
    return (y_prompt, y_sample, new_cmp_p, new_slc_p, new_win_p, new_ssm_p,
            new_cmp_s, new_slc_s, new_win_s, new_ssm_s)
```

```python
import functools
import math

import jax
import jax.numpy as jnp
import numpy as np
from jax import lax
from jax.experimental import pallas as pl
from jax.experimental.pallas import tpu as pltpu

F32 = jnp.float32
BF16 = jnp.bfloat16

D_MODEL = 1024
SSM_WIDTH = 512
SSM_GROUP = 16
N_SSM_GROUPS = 32
SSM_STATE = 64
HEAD_DIM = 64
N_Q_HEADS = 8
N_KV_HEADS = 2
GQA = 4
CMP_LEN = 32
CMP_STRIDE = 16
SLC_BLOCK = 64
TOP_N = 8
WINDOW = 512
Q_BLOCK = 256
NEG_INF = -1e30
FORCE_BONUS = 1e4
Q_W = 512
KV_W = 256
NSA_GATE_W = 24
N_EXPERT_GROUPS = 4
EXPERTS_PER_GROUP = 4
N_EXPERTS = 16
D_FF_EXPERT = 256
PLE_DIM = 256
RMS_EPS = 1e-6
PAGE_SIZE = 128

LANES = 128
SUBLANES = 8
N_STATE = N_SSM_GROUPS * SSM_STATE
MIB = 2 ** 20


def _cparams(sem, vmem_mib):
    return pltpu.CompilerParams(dimension_semantics=sem, vmem_limit_bytes=vmem_mib * MIB)


def _full(shape, single_buffer=False):
    nd = len(shape)
    if single_buffer:
        return pl.BlockSpec(shape, lambda *_: (0,) * nd, pipeline_mode=pl.Buffered(1))
    return pl.BlockSpec(shape, lambda *_: (0,) * nd)


def _prompt_layout(nseq, t, tm):
    nb = t // tm
    return {
        "grid": (nb, nseq), "tm": tm, "nseq": nseq, "t": t,
        "a": lambda w: ((nseq * t, w), pl.BlockSpec((tm, w), lambda b, s: (s * nb + b, 0))),
    }


def _sample_layout(nseq, t):
    return {
        "grid": (1, t), "tm": nseq,
        "a": lambda w: ((nseq, t * w), pl.BlockSpec((nseq, w), lambda s, b: (0, b))),
        "b": lambda w: ((t * nseq, w), pl.BlockSpec((nseq, w), lambda s, b: (b, 0))),
    }


TK_SLC = 512
TK_WIN = 128


Q_SCALE = HEAD_DIM ** -0.5 * math.log2(math.e)
C_U, C_Q, C_KVC, C_KVS, C_KVW = 0, 512, 1024, 1280, 1536
N_MAIN = 1792


def _dot_t(a, b):
    return lax.dot_general(a, b, (((1,), (1,)), ((), ())), preferred_element_type=F32)


GN_ROWS = 32


def _inproj_prompt_body(x_ref, g_ref, wa_ref, wgn_ref, wgab_ref,
                        u_ref, kvc_ref, ksb_ref, kwb_ref, ga_ref, gb_ref,
                        qt_ref, kvct_ref, kvst_ref, kvwt_ref, gnt_ref, vst_ref, vwt_ref, *, nseq):
    s = pl.program_id(1)
    x = x_ref[...]
    inv = lax.rsqrt(jnp.mean(x * x, axis=-1, keepdims=True) + RMS_EPS)
    h = (x * inv * g_ref[...]).astype(BF16)
    tm = h.shape[0]

    def mm(w):
        return jnp.dot(h, w, preferred_element_type=F32)

    u = mm(wa_ref[:, C_U:C_U + SSM_WIDTH])
    for j in range(SSM_WIDTH // LANES):
        u_ref[j, pl.ds(s, tm, stride=nseq), :] = u[:, LANES * j:LANES * (j + 1)]
    ga_ref[...] = jax.nn.sigmoid(mm(wgab_ref[:, 0:D_MODEL]))
    gb_ref[...] = jax.nn.sigmoid(mm(wgab_ref[:, D_MODEL:2 * D_MODEL]))
    kvc = mm(wa_ref[:, C_KVC:C_KVC + KV_W])
    kvc_ref[0] = kvc[:, 0:LANES]
    kvc_ref[1] = kvc[:, LANES:2 * LANES]
    kvct_ref[0] = kvc.T
    kvs = mm(wa_ref[:, C_KVS:C_KVS + KV_W])
    ksb_ref[...] = kvs[:, 0:LANES].astype(BF16)
    kvst = kvs.T
    kvst_ref[0] = kvst
    kvw = mm(wa_ref[:, C_KVW:C_KVW + KV_W])
    kwb_ref[...] = kvw[:, 0:LANES].astype(BF16)
    kvwt = kvw.T
    kvwt_ref[0] = kvwt
    for c in range(tm // TK_WIN):
        vst_ref[0, c] = kvst[LANES:2 * LANES, c * TK_WIN:(c + 1) * TK_WIN].astype(BF16)
        vwt_ref[0, c] = kvwt[LANES:2 * LANES, c * TK_WIN:(c + 1) * TK_WIN].astype(BF16)
    qt_ref[0] = (mm(wa_ref[:, C_Q:C_Q + Q_W]) * Q_SCALE).T.astype(BF16)
    gnt_ref[0] = jax.nn.sigmoid(mm(wgn_ref[...])).T[0:GN_ROWS]


def _inproj_prompt(x2d, lay, g, w):
    tm, nseq, t = lay["tm"], lay["nseq"], lay["t"]
    nb = t // tm
    out_shapes, out_specs, names = [], [], []

    def add(name, shape_spec, dt):
        names.append(name)
        out_shapes.append(jax.ShapeDtypeStruct(shape_spec[0], dt))
        out_specs.append(shape_spec[1])

    def tr(rows):
        return (nseq, rows, t), pl.BlockSpec((1, rows, tm), lambda b, s: (s, 0, b))

    nu = SSM_WIDTH // LANES
    add("u", ((nu, t * nseq, LANES), pl.BlockSpec((nu, tm * nseq, LANES), lambda b, s: (0, b, 0))), F32)
    add("kvc", ((2, nseq * t, LANES), pl.BlockSpec((2, tm, LANES), lambda b, s: (0, s * nb + b, 0))), F32)
    add("ksb", lay["a"](LANES), BF16)
    add("kwb", lay["a"](LANES), BF16)
    add("ga", lay["a"](D_MODEL), F32)
    add("gb", lay["a"](D_MODEL), F32)
    add("qt", tr(Q_W), BF16)
    add("kvct", tr(KV_W), F32)
    add("kvst", tr(KV_W), F32)
    add("kvwt", tr(KV_W), F32)
    add("gnt", tr(GN_ROWS), F32)
    for name in ("vst", "vwt"):
        add(name, ((nseq, t // TK_WIN, LANES, TK_WIN),
                   pl.BlockSpec((1, tm // TK_WIN, LANES, TK_WIN), lambda b, s: (s, b, 0, 0))), BF16)
    x_shape, x_spec = lay["a"](D_MODEL)
    ops = [g, w["wa"], w["wgn"], w["wgab"]]
    outs = pl.pallas_call(
        functools.partial(_inproj_prompt_body, nseq=nseq),
        grid=lay["grid"],
        in_specs=[x_spec] + [_full(o.shape) for o in ops],
        out_specs=out_specs,
        out_shape=out_shapes,
        compiler_params=_cparams(("arbitrary",) * 2, 56),
        name="inproj_prompt",
    )(x2d.reshape(x_shape), *ops)
    return dict(zip(names, outs))


def _inproj_sample_body(x_ref, g_ref, wa_ref, wgn_ref, wgab_ref,
                        u_ref, q_ref, kvs_ref, kvw_ref, gn_ref, ga_ref, gb_ref, kvct_ref, kvst_ref, kvwt_ref):
    x = x_ref[...]
    inv = lax.rsqrt(jnp.mean(x * x, axis=-1, keepdims=True) + RMS_EPS)
    h = (x * inv * g_ref[...]).astype(BF16)

    def mm(w):
        return jnp.dot(h, w, preferred_element_type=F32)

    u_ref[...] = mm(wa_ref[:, C_U:C_U + SSM_WIDTH])
    q_ref[...] = mm(wa_ref[:, C_Q:C_Q + Q_W]) * Q_SCALE
    kvs = mm(wa_ref[:, C_KVS:C_KVS + KV_W])
    kvs_ref[...] = kvs
    kvw = mm(wa_ref[:, C_KVW:C_KVW + KV_W])
    kvw_ref[...] = kvw
    gn_ref[...] = jax.nn.sigmoid(mm(wgn_ref[...]))
    ga_ref[...] = jax.nn.sigmoid(mm(wgab_ref[:, 0:D_MODEL]))
    gb_ref[...] = jax.nn.sigmoid(mm(wgab_ref[:, D_MODEL:2 * D_MODEL]))
    kvct_ref[0] = mm(wa_ref[:, C_KVC:C_KVC + KV_W]).T
    kvst_ref[0] = kvs.T
    kvwt_ref[0] = kvw.T


def _inproj_sample(x2d, lay, g, w):
    nseq = lay["tm"]
    ts = lay["grid"][1]
    names = ["u", "q", "kvs", "kvw", "gn", "ga", "gb"]
    widths = [SSM_WIDTH, Q_W, KV_W, KV_W, LANES, D_MODEL, D_MODEL]
    out_shapes, out_specs = [], []
    for n, wd in zip(names, widths):
        shp, spec = lay["b" if n == "u" else "a"](wd)
        out_shapes.append(jax.ShapeDtypeStruct(shp, F32))
        out_specs.append(spec)
    for n in ("kvct", "kvst", "kvwt"):
        names.append(n)
        out_shapes.append(jax.ShapeDtypeStruct((ts, KV_W, nseq), F32))
        out_specs.append(pl.BlockSpec((1, KV_W, nseq), lambda s, b: (b, 0, 0)))
    x_shape, x_spec = lay["a"](D_MODEL)
    ops = [g, w["wa"], w["wgn"], w["wgab"]]
    outs = pl.pallas_call(
        _inproj_sample_body,
        grid=lay["grid"],
        in_specs=[x_spec] + [_full(o.shape) for o in ops],
        out_specs=out_specs,
        out_shape=out_shapes,
        compiler_params=_cparams(("arbitrary",) * 2, 56),
        name="inproj_sample",
    )(x2d.reshape(x_shape), *ops)
    return dict(zip(names, outs))


def _inproj_params(w_in0):
    return {"wa": w_in0[:, :N_MAIN].astype(BF16),
            "wgn": jnp.pad(w_in0[:, N_MAIN:N_MAIN + NSA_GATE_W], ((0, 0), (0, LANES - NSA_GATE_W))).astype(BF16),
            "wgab": w_in0[:, N_MAIN + NSA_GATE_W:].astype(BF16)}


def _s5_prompt_body(u_ref, wb_ref, ar_ref, ai_ref, cw_ref, d_ref, wglu_ref, bglu_ref, wso_ref,
                    abr_ref, hlast_ref, lhs_ref, bu_ref, h8_ref, p_ref, hstate_ref):
    c = pl.program_id(0)
    nseq = 4
    r4 = u_ref.shape[1]
    tc = r4 // nseq
    half = tc // 2

    @pl.when(c == 0)
    def _():
        hstate_ref[...] = jnp.zeros_like(hstate_ref)

    u = jnp.concatenate([u_ref[j] for j in range(SSM_WIDTH // LANES)], axis=1)
    row2 = lax.broadcasted_iota(jnp.int32, (r4, SSM_WIDTH), 0)
    lo2 = (row2 % SUBLANES) < nseq
    up = pltpu.roll(u, r4 - nseq, axis=0)
    dn = pltpu.roll(u, nseq, axis=0)
    swapped = jnp.where(lo2, up, dn)
    zero = jnp.zeros_like(u)
    ev_re = jnp.where(lo2, u, zero).astype(BF16).reshape(half, SUBLANES, SSM_WIDTH)
    ev_im = jnp.where(lo2, zero, swapped).astype(BF16).reshape(half, SUBLANES, SSM_WIDTH)
    od_re = jnp.where(lo2, swapped, zero).astype(BF16).reshape(half, SUBLANES, SSM_WIDTH)
    od_im = jnp.where(lo2, zero, u).astype(BF16).reshape(half, SUBLANES, SSM_WIDTH)
    for j in range(4):
        sl = slice(LANES * j, LANES * (j + 1))
        lhs_ref[:, 0:8, 256 * j:256 * j + LANES] = ev_re[:, :, sl]
        lhs_ref[:, 0:8, 256 * j + LANES:256 * (j + 1)] = ev_im[:, :, sl]
        lhs_ref[:, 8:16, 256 * j:256 * j + LANES] = od_re[:, :, sl]
        lhs_ref[:, 8:16, 256 * j + LANES:256 * (j + 1)] = od_im[:, :, sl]
    for j in range(4):
        lhs = lhs_ref[:, :, 256 * j:256 * (j + 1)].reshape(tc * SUBLANES, 256)
        bu_ref[:, 512 * j:512 * (j + 1)] = jnp.dot(lhs, wb_ref[j], preferred_element_type=F32)

    for lc in range(4):
        sl = slice(512 * lc, 512 * (lc + 1))
        ar = ar_ref[:, sl]
        ai = ai_ref[:, sl]

        def step(t, h, sl=sl, ar=ar, ai=ai):
            r0 = pl.multiple_of(t * SUBLANES, SUBLANES)
            h = ar * h + ai * pltpu.roll(h, nseq, axis=0) + bu_ref[pl.ds(r0, SUBLANES), sl]
            h8_ref[pl.ds(r0, SUBLANES), sl] = h
            return h

        hstate_ref[:, sl] = lax.fori_loop(0, tc, step, hstate_ref[:, sl], unroll=8)
    hlast_ref[...] = hstate_ref[...]

    for j in range(4):
        pj = jnp.dot(h8_ref[:, 512 * j:512 * (j + 1)].astype(BF16), cw_ref[j], preferred_element_type=F32)
        p_ref[2 * j] = pj[:, 0:LANES]
        p_ref[2 * j + 1] = pj[:, LANES:2 * LANES]
    ys = []
    for s in range(nseq):
        parts = []
        for j in range(4):
            re = p_ref[2 * j, pl.ds(s, tc, stride=SUBLANES), :]
            im = p_ref[2 * j + 1, pl.ds(nseq + s, tc, stride=SUBLANES), :]
            us = u_ref[j, pl.ds(s, tc, stride=nseq), :]
            parts.append(re + im + d_ref[:, LANES * j:LANES * (j + 1)] * us)
        ys.append(jnp.concatenate(parts, axis=1))
    y = jnp.concatenate(ys, axis=0)
    zg = jax.nn.gelu(y)
    gate = jnp.dot(zg.astype(BF16), wglu_ref[...], preferred_element_type=F32) + bglu_ref[...]
    glu = (zg * jax.nn.sigmoid(gate)).astype(BF16)
    abr = jnp.dot(glu, wso_ref[...], preferred_element_type=F32)
    for s in range(nseq):
        abr_ref[s] = abr[s * tc:(s + 1) * tc]


def _s5_prompt(u_ts, sp, wglu, bglu, wso, t_total, tc):
    nseq = 4
    grid = (t_total // tc,)
    abr, hlast = pl.pallas_call(
        _s5_prompt_body,
        grid=grid,
        in_specs=[pl.BlockSpec((SSM_WIDTH // LANES, tc * nseq, LANES), lambda c: (0, c, 0)),
                  _full(sp["wb8"].shape), _full(sp["ar8"].shape), _full(sp["ai8"].shape), _full(sp["cw8"].shape),
                  _full(sp["d"].shape), _full(wglu.shape), _full(bglu.shape), _full(wso.shape)],
        out_specs=[pl.BlockSpec((nseq, tc, D_MODEL), lambda c: (0, c, 0)),
                   pl.BlockSpec((SUBLANES, N_STATE), lambda c: (0, 0))],
        out_shape=[jax.ShapeDtypeStruct((nseq, t_total, D_MODEL), F32),
                   jax.ShapeDtypeStruct((SUBLANES, N_STATE), F32)],
        scratch_shapes=[pltpu.VMEM((tc // 2, 2 * SUBLANES, 1024), BF16),
                        pltpu.VMEM((tc * SUBLANES, N_STATE), F32),
                        pltpu.VMEM((tc * SUBLANES, N_STATE), F32),
                        pltpu.VMEM((8, tc * SUBLANES, LANES), F32),
                        pltpu.VMEM((SUBLANES, N_STATE), F32)],
        compiler_params=_cparams(("arbitrary",), 56),
        name="s5_prompt",
    )(u_ts, sp["wb8"], sp["ar8"], sp["ai8"], sp["cw8"], sp["d"], wglu, bglu, wso)
    return abr, hlast


def _s5_params(lam_re, lam_im, log_dt, b_re, b_im, c_re, c_im, d_skip):
    lam = lax.complex(lam_re.astype(F32), lam_im.astype(F32))
    dt = jnp.exp(log_dt.astype(F32))[:, None]
    a_bar = jnp.exp(lam * dt)
    b = lax.complex(b_re.astype(F32), b_im.astype(F32))
    b_bar = ((a_bar - 1.0) / lam)[..., None] * b
    eye8 = jnp.eye(8, dtype=F32)

    def bd_b(m):
        return jnp.einsum("ab,jbpc->jacbp", eye8, m.reshape(4, 8, SSM_STATE, SSM_GROUP)).reshape(4, 128, 512)

    def bd_c(m):
        return jnp.einsum("ab,jbcp->japbc", eye8, m.reshape(4, 8, SSM_GROUP, SSM_STATE)).reshape(4, 512, 128)

    wre, wim = bd_b(b_bar.real), bd_b(b_bar.imag)
    cre, cim = bd_c(c_re.astype(F32)), bd_c(c_im.astype(F32))
    ar = a_bar.real.reshape(1, N_STATE)
    ai = a_bar.imag.reshape(1, N_STATE)
    sign = jnp.concatenate([-jnp.ones((4, 1), F32), jnp.ones((4, 1), F32)], axis=0)
    return {
        "wb8": jnp.concatenate([wre, wim], axis=1).astype(BF16),
        "cw8": jnp.concatenate([cre, -cim], axis=2).astype(BF16),
        "ar8": jnp.broadcast_to(ar, (SUBLANES, N_STATE)),
        "ai8": sign * ai,
        "wre": wre.astype(BF16), "wim": wim.astype(BF16),
        "cre": cre.astype(BF16), "cim": cim.astype(BF16),
        "ar": ar, "ai": ai,
        "d": d_skip.astype(F32).reshape(1, SSM_WIDTH),
    }


def _cmp_params(cmp_pe, cmp_w1, cmp_w2):
    eye2 = jnp.eye(2, dtype=F32)
    nhalf = CMP_LEN // CMP_STRIDE
    w1r = cmp_w1.astype(F32).reshape(2, nhalf, CMP_STRIDE, HEAD_DIM, HEAD_DIM)
    wk = jnp.einsum("kside,ph->kipdshe", w1r, eye2).reshape(2, CMP_STRIDE * LANES, nhalf * LANES)
    bk = jnp.einsum("kld,klde->ke", cmp_pe.astype(F32), cmp_w1.astype(F32), precision=lax.Precision.HIGHEST)
    w2k = jnp.einsum("kef,ph->kpehf", cmp_w2.astype(F32), eye2).reshape(2, LANES, LANES)
    return {"wk": wk.astype(BF16), "bk": jnp.tile(bk, (1, N_KV_HEADS)), "w2k": w2k.astype(BF16),
            "w2kt": jnp.swapaxes(w2k, 1, 2).astype(BF16)}


def _compress_hidden(tap, nch, kv, wk_ref, bk_ref):
    x = jnp.concatenate([tap(i).astype(BF16) for i in range(CMP_STRIDE)], axis=1)
    pp = jnp.dot(x, wk_ref[kv], preferred_element_type=F32)
    pre = pp[:, 0:LANES] + pltpu.roll(pp[:, LANES:2 * LANES], nch - 1, axis=0) + bk_ref[kv:kv + 1, :]
    return jax.nn.gelu(pre).astype(BF16)


def _compress_prompt_body(x_ref, wk_ref, bk_ref, w2k_ref, w2kt_ref, ck_ref, cvt_ref):
    nch = x_ref.shape[1] // CMP_STRIDE
    hid = [_compress_hidden(lambda i, kv=kv: x_ref[kv, pl.ds(i, nch, stride=CMP_STRIDE), :], nch, kv, wk_ref, bk_ref)
           for kv in range(2)]
    ck_ref[0] = jnp.dot(hid[0], w2k_ref[0], preferred_element_type=F32).astype(BF16)
    cvt_ref[0] = _dot_t(w2kt_ref[1], hid[1]).astype(BF16)


def _compress_prompt(kvc2, cp, nseq, t):
    nch = t // CMP_STRIDE
    return pl.pallas_call(
        _compress_prompt_body,
        grid=(nseq,),
        in_specs=[pl.BlockSpec((2, t, LANES), lambda n: (0, n, 0)),
                  _full(cp["wk"].shape), _full(cp["bk"].shape), _full(cp["w2k"].shape), _full(cp["w2kt"].shape)],
        out_specs=[pl.BlockSpec((1, nch, LANES), lambda n: (n, 0, 0)),
                   pl.BlockSpec((1, LANES, nch), lambda n: (n, 0, 0))],
        out_shape=[jax.ShapeDtypeStruct((nseq, nch, LANES), BF16),
                   jax.ShapeDtypeStruct((nseq, LANES, nch), BF16)],
        compiler_params=_cparams(("arbitrary",), 48),
        name="compress_prompt",
    )(kvc2, cp["wk"], cp["bk"], cp["w2k"], cp["w2kt"])


def _overlap_t(n_cmp_pad, n_slc_pad):
    j = np.arange(n_cmp_pad)[None, :]
    s = np.arange(n_slc_pad)[:, None]
    ov = (j * CMP_STRIDE <= s * SLC_BLOCK + SLC_BLOCK - 1) & (j * CMP_STRIDE + CMP_LEN - 1 >= s * SLC_BLOCK)
    return jnp.asarray(ov, dtype=BF16)


def _softmax_cols(s, valid):
    sm = jnp.where(valid, s, NEG_INF)
    mx = jnp.max(sm, axis=0, keepdims=True)
    e = jnp.where(valid, jnp.exp2(sm - mx), 0.0)
    l = jnp.sum(e, axis=0, keepdims=True)
    return e * (1.0 / jnp.maximum(l, 1e-30))


def _select_blocks(imp, blk, pos, nblk, axis=0):
    cur = pos // SLC_BLOCK
    forced = (blk == 0) | (blk == cur) | (blk == cur - 1)
    v = jnp.where(forced, imp + FORCE_BONUS, imp)
    v = jnp.where(blk * SLC_BLOCK <= pos, v, NEG_INF)
    v = jnp.where(blk < nblk, v, -3e38)
    blk_f = blk.astype(F32)
    neg = jnp.full(imp.shape, NEG_INF, F32)
    for _ in range(min(TOP_N, nblk)):
        mx = jnp.max(v, axis=axis, keepdims=True)
        first = jnp.min(jnp.where(v == mx, blk_f, float(imp.shape[axis])), axis=axis, keepdims=True)
        pick = blk_f == first
        neg = jnp.where(pick, 0.0, neg)
        v = jnp.where(pick, -3e38, v)
    return neg


CB = 2 * LANES


def _attn_prompt_body(q_ref, gn_ref, ck_ref, cvt_ref, ks_ref, vst_ref, kw_ref, vwt_ref, ovt_ref,
                      o_ref, kaug_ref, kwaug_ref, qaug_ref, acc_ref):
    i = pl.program_id(1)
    t = ks_ref.shape[1]
    nch = ck_ref.shape[1]
    nslc = t // SLC_BLOCK
    qb = Q_BLOCK
    ncol = N_Q_HEADS * qb
    ncb = ncol // CB
    q0 = i * qb
    one_row = 2 * LANES - HEAD_DIM

    @pl.when(i == 0)
    def _():
        kaug_ref[:, 0:LANES] = ks_ref[0]
        blk = lax.broadcasted_iota(jnp.int32, (t, LANES), 0) // SLC_BLOCK
        col = lax.broadcasted_iota(jnp.int32, (t, LANES), 1)
        kaug_ref[:, LANES:2 * LANES] = jnp.where(blk == col, 1.0, 0.0).astype(BF16)
        padcol = lax.broadcasted_iota(jnp.int32, (WINDOW, 2 * LANES), 1)
        kwaug_ref[0:WINDOW, :] = jnp.where(padcol == one_row, NEG_INF, 0.0).astype(BF16)
        kwaug_ref[WINDOW:WINDOW + t, 0:LANES] = kw_ref[0]
        kwaug_ref[WINDOW:WINDOW + t, LANES:2 * LANES] = jnp.zeros((t, LANES), BF16)

    zeros64 = jnp.zeros((HEAD_DIM, qb), BF16)
    for j in range(N_Q_HEADS):
        dst = j // GQA
        qaug_ref[HEAD_DIM * dst:HEAD_DIM * (dst + 1), qb * j:qb * (j + 1)] = q_ref[0, HEAD_DIM * j:HEAD_DIM * (j + 1), :]
        qaug_ref[HEAD_DIM * (1 - dst):HEAD_DIM * (2 - dst), qb * j:qb * (j + 1)] = zeros64
    tail_row = lax.broadcasted_iota(jnp.int32, (HEAD_DIM, ncol), 0)
    qaug_ref[one_row:2 * LANES, :] = jnp.where(tail_row == 0, 1.0, 0.0).astype(BF16)

    qaug_ref[LANES:one_row, :] = jnp.zeros((one_row - LANES, ncol), BF16)
    q_blocks = qb // TK_WIN
    npiece = (WINDOW + qb) // TK_WIN
    kws = [kwaug_ref[pl.ds(pl.multiple_of(q0 + w * TK_WIN, TK_WIN), TK_WIN), :] for w in range(npiece)]
    vwt = jnp.concatenate([vwt_ref[0, jnp.maximum(i * q_blocks + w - WINDOW // TK_WIN, 0)] for w in range(npiece)],
                          axis=1)
    wrow = lax.broadcasted_iota(jnp.int32, (TK_WIN, CB), 0)
    wcol = lax.broadcasted_iota(jnp.int32, (TK_WIN, CB), 1) & (qb - 1)
    wbias = []
    for w in range(npiece):
        lo, hi = w * TK_WIN - WINDOW, w * TK_WIN - WINDOW + TK_WIN - 1
        if hi <= 0 and qb - 1 - lo < WINDOW:
            wbias.append(None)
        else:
            dlt = wcol - wrow - lo
            wbias.append(jnp.where((dlt >= 0) & (dlt < WINDOW), 0.0, NEG_INF))
    kw_all = jnp.concatenate(kws, axis=0)
    sws = [jnp.dot(kw_all, qaug_ref[:, CB * cb:CB * (cb + 1)], preferred_element_type=F32) for cb in range(ncb)]
    es, rls = [], []
    for s in sws:
        s = jnp.concatenate([s[TK_WIN * w:TK_WIN * (w + 1)] if b is None else s[TK_WIN * w:TK_WIN * (w + 1)] + b
                             for w, b in enumerate(wbias)], axis=0)
        e = jnp.exp2(s - jnp.max(s, axis=0, keepdims=True))
        es.append(e.astype(BF16))
        rls.append(1.0 / jnp.sum(e, axis=0, keepdims=True))
    ow = jnp.concatenate([jnp.dot(vwt, e, preferred_element_type=F32) * rl for e, rl in zip(es, rls)], axis=1)

    pos_c = q0 + (lax.broadcasted_iota(jnp.int32, (nch, CB), 1) & (qb - 1))
    cvalid = lax.broadcasted_iota(jnp.int32, (nch, CB), 0) * CMP_STRIDE + (CMP_LEN - 1) <= pos_c
    scs = [jnp.dot(ck_ref[0], qaug_ref[0:LANES, CB * cb:CB * (cb + 1)], preferred_element_type=F32)
           for cb in range(ncb)]
    pcs = [_softmax_cols(sc, cvalid).astype(BF16) for sc in scs]
    oc = jnp.concatenate([jnp.dot(cvt_ref[0], pc, preferred_element_type=F32) for pc in pcs], axis=1)
    imp = jnp.concatenate([jnp.dot(ovt_ref[...], pc, preferred_element_type=F32) for pc in pcs], axis=1)
    blk = lax.broadcasted_iota(jnp.int32, (nslc, qb), 0)
    pos_q = q0 + lax.broadcasted_iota(jnp.int32, (nslc, qb), 1)
    for h in range(N_KV_HEADS):
        v = imp[0:nslc, qb * GQA * h:qb * GQA * h + qb]
        for g in range(1, GQA):
            v = v + imp[0:nslc, qb * (GQA * h + g):qb * (GQA * h + g + 1)]
        neg = _select_blocks(v, blk, pos_q, nslc).astype(BF16)
        for g in range(GQA):
            j = GQA * h + g
            qaug_ref[LANES:LANES + nslc, qb * j:qb * (j + 1)] = neg

    brow = lax.broadcasted_iota(jnp.int32, (qb, CB), 0)
    bcol = lax.broadcasted_iota(jnp.int32, (qb, CB), 1) & (qb - 1)
    tri_lo = jnp.where(brow <= bcol, 0.0, NEG_INF)

    acc_ref[...] = jnp.zeros_like(acc_ref)

    def sel_tile(k0, nk, vt, carry, bias):
        m, l = carry
        ka = kaug_ref[pl.ds(k0, nk), :]
        css = [slice(CB * cb, CB * (cb + 1)) for cb in range(ncb)]
        ss = [jnp.dot(ka, qaug_ref[:, cs], preferred_element_type=F32) for cs in css]
        ms, ls, ps, alphas = [], [], [], []
        for cs, s in zip(css, ss):
            if bias is not None:
                s = s + bias
            mn = jnp.maximum(m[:, cs], jnp.max(s, axis=0, keepdims=True))
            alpha = jnp.exp2(m[:, cs] - mn)
            p = jnp.exp2(s - mn)
            ms.append(mn)
            ls.append(alpha * l[:, cs] + jnp.sum(p, axis=0, keepdims=True))
            ps.append(p.astype(BF16))
            alphas.append(alpha)
        pvs = [jnp.dot(vt, p, preferred_element_type=F32) for p in ps]
        for cs, alpha, pv in zip(css, alphas, pvs):
            acc_ref[:, cs] = alpha * acc_ref[:, cs] + pv
        return jnp.concatenate(ms, axis=1), jnp.concatenate(ls, axis=1)

    def vt_blocks(ref, b0, n):
        return jnp.concatenate([ref[0, b0 + j] for j in range(n)], axis=1) if n > 1 else ref[0, b0]

    big_blocks = TK_SLC // TK_WIN

    def big_tile(kt, carry):
        return sel_tile(pl.multiple_of(kt * TK_SLC, TK_SLC), TK_SLC, vt_blocks(vst_ref, kt * big_blocks, big_blocks),
                        carry, None)

    def small_tile(kb, carry):
        return sel_tile(pl.multiple_of(kb * qb, qb), qb, vt_blocks(vst_ref, kb * q_blocks, q_blocks), carry, None)

    carry = (jnp.full((1, ncol), NEG_INF, F32), jnp.zeros((1, ncol), F32))
    nbig = q0 // TK_SLC
    carry = lax.fori_loop(0, nbig, big_tile, carry)
    carry = lax.fori_loop(nbig * (TK_SLC // qb), i, small_tile, carry)
    _, l = sel_tile(pl.multiple_of(q0, qb), qb, vt_blocks(vst_ref, i * q_blocks, q_blocks), carry, tri_lo)
    osel = acc_ref[...] * (1.0 / l)

    gt = gn_ref[0]
    for c in range(N_Q_HEADS // 2):
        rows = []
        for hh in range(2):
            j = 2 * c + hh
            rs = slice(HEAD_DIM * (j // GQA), HEAD_DIM * (j // GQA + 1))
            cs = slice(qb * j, qb * (j + 1))
            rows.append(gt[3 * j:3 * j + 1, :] * oc[rs, cs] + gt[3 * j + 1:3 * j + 2, :] * osel[rs, cs]
                        + gt[3 * j + 2:3 * j + 3, :] * ow[rs, cs])
        o_ref[:, LANES * c:LANES * (c + 1)] = jnp.concatenate(rows, axis=0).T.astype(o_ref.dtype)


def _attn_prompt(q, gn, ck, cvt, ksb, vst, kwb, vwt, nseq, t):
    nb = t // Q_BLOCK
    nch = t // CMP_STRIDE
    nslc = t // SLC_BLOCK
    ovt = _overlap_t(nch, max(nslc, SUBLANES))
    row = lambda n, i: (n * nb + i, 0)
    seq3 = lambda n, i: (n, 0, 0)
    seq4 = lambda n, i: (n, 0, 0, 0)
    col3 = lambda n, i: (n, 0, i)
    return pl.pallas_call(
        _attn_prompt_body,
        grid=(nseq, nb),
        in_specs=[pl.BlockSpec((1, Q_W, Q_BLOCK), col3), pl.BlockSpec((1, gn.shape[1], Q_BLOCK), col3),
                  pl.BlockSpec((1, nch, LANES), seq3), pl.BlockSpec((1, LANES, nch), seq3),
                  pl.BlockSpec((1, t, LANES), seq3), pl.BlockSpec((1, t // TK_WIN, LANES, TK_WIN), seq4),
                  pl.BlockSpec((1, t, LANES), seq3), pl.BlockSpec((1, t // TK_WIN, LANES, TK_WIN), seq4),
                  _full(ovt.shape)],
        out_specs=pl.BlockSpec((Q_BLOCK, Q_W), row),
        out_shape=jax.ShapeDtypeStruct((nseq * t, Q_W), BF16),
        scratch_shapes=[pltpu.VMEM((t, 2 * LANES), BF16),
                        pltpu.VMEM((WINDOW + t, 2 * LANES), BF16),
                        pltpu.VMEM((2 * LANES, N_Q_HEADS * Q_BLOCK), BF16),
                        pltpu.VMEM((LANES, N_Q_HEADS * Q_BLOCK), F32)],
        compiler_params=_cparams(("arbitrary", "arbitrary"), 56),
        name="attn_prompt",
    )(q, gn, ck, cvt, ksb.reshape(nseq, t, LANES), vst, kwb.reshape(nseq, t, LANES), vwt, ovt)


def _post_body(x_ref, abr_ref, on_ref, ga_ref, gb_ref, wno_ref, wo_ref, g2_ref, x1_ref, h2_ref):
    bbr = jnp.dot(on_ref[...].astype(BF16), wno_ref[...], preferred_element_type=F32)
    merged = ga_ref[...] * abr_ref[...] + gb_ref[...] * bbr
    x1 = x_ref[...] + jnp.dot(merged.astype(BF16), wo_ref[...], preferred_element_type=F32)
    x1_ref[...] = x1
    inv = lax.rsqrt(jnp.mean(x1 * x1, axis=-1, keepdims=True) + RMS_EPS)
    h2_ref[...] = (x1 * inv * g2_ref[...]).astype(BF16)


def _post(x2d, abr, abr_lay, onsa, ga, gb, wno, wo, g2, lay, out_lay):
    x_shape, x_spec = lay["a"](D_MODEL)
    abr_shape, abr_spec = lay[abr_lay](D_MODEL)
    on_shape, on_spec = lay["a"](Q_W)
    o_shape, o_spec = lay[out_lay](D_MODEL)
    return pl.pallas_call(
        _post_body,
        grid=lay["grid"],
        in_specs=[x_spec, abr_spec, on_spec, x_spec, x_spec, _full(wno.shape), _full(wo.shape), _full(g2.shape)],
        out_specs=[o_spec, o_spec],
        out_shape=[jax.ShapeDtypeStruct(o_shape, F32), jax.ShapeDtypeStruct(o_shape, BF16)],
        compiler_params=_cparams(("arbitrary",) * len(lay["grid"]), 48),
        name="post",
    )(x2d.reshape(x_shape), abr.reshape(abr_shape), onsa.reshape(on_shape), ga, gb, wno, wo, g2)


def _route(logits):
    lane = lax.broadcasted_iota(jnp.int32, logits.shape, 1).astype(F32)
    big = float(LANES)
    glog = jnp.where(lane < N_EXPERT_GROUPS, logits, -jnp.inf)
    gmax = jnp.max(glog, axis=1, keepdims=True)
    gsel = jnp.min(jnp.where(glog == gmax, lane, big), axis=1, keepdims=True)
    gw = 1.0 / jnp.sum(jnp.exp(glog - gmax), axis=1, keepdims=True)
    lo = N_EXPERT_GROUPS + EXPERTS_PER_GROUP * gsel
    el = jnp.where((lane >= lo) & (lane < lo + EXPERTS_PER_GROUP), logits, -jnp.inf)
    v1 = jnp.max(el, axis=1, keepdims=True)
    i1 = jnp.min(jnp.where(el == v1, lane, big), axis=1, keepdims=True)
    el2 = jnp.where(lane == i1, -jnp.inf, el)
    v2 = jnp.max(el2, axis=1, keepdims=True)
    i2 = jnp.min(jnp.where(el2 == v2, lane, big), axis=1, keepdims=True)
    e2 = jnp.exp(v2 - v1)
    w1 = gw / (1.0 + e2)
    return jnp.where(lane == i1, w1, 0.0) + jnp.where(lane == i2, w1 * e2, 0.0)


def _moe_body(x1_ref, h2_ref, p_ref, wr_ref, br_ref, wg_ref, wu_ref, wd_ref, wpg_ref, wp_ref, gf_ref,
              y_ref, acc_ref, comb_ref, *, tsplit):
    g = pl.program_id(1)
    h2 = h2_ref[...]

    @pl.when(g == 0)
    def _():
        logits = jnp.dot(h2, wr_ref[...], preferred_element_type=F32) + br_ref[...]
        comb_ref[...] = _route(logits)
        acc_ref[...] = jnp.zeros_like(acc_ref)

    comb = comb_ref[...]
    lane = lax.broadcasted_iota(jnp.int32, comb.shape, 1)
    acc = acc_ref[...]
    for k in range(EXPERTS_PER_GROUP):
        e_lane = N_EXPERT_GROUPS + EXPERTS_PER_GROUP * g + k
        ce = jnp.sum(jnp.where(lane == e_lane, comb, 0.0), axis=1, keepdims=True)
        a = jnp.dot(h2, wg_ref[k], preferred_element_type=F32)
        b = jnp.dot(h2, wu_ref[k], preferred_element_type=F32)
        act = (jax.nn.silu(a) * b * ce).astype(BF16)
        acc = acc + jnp.dot(act, wd_ref[k], preferred_element_type=F32)
    acc_ref[...] = acc

    @pl.when(g == N_EXPERT_GROUPS - 1)
    def _():
        x2 = x1_ref[...] + acc_ref[...]
        rows = x2.shape[0] // tsplit
        if tsplit == 1:
            p = p_ref[...]
        else:
            p = jnp.concatenate([p_ref[:, PLE_DIM * t:PLE_DIM * (t + 1)] for t in range(tsplit)], axis=0)
        gate = jax.nn.sigmoid(jnp.dot(x2.astype(BF16), wpg_ref[...], preferred_element_type=F32))
        x3 = x2 + gate * jnp.dot(p.astype(BF16), wp_ref[...], preferred_element_type=F32)
        inv = lax.rsqrt(jnp.mean(x3 * x3, axis=-1, keepdims=True) + RMS_EPS)
        y = x3 * inv * gf_ref[...]
        if tsplit == 1:
            y_ref[...] = y
        else:
            for t in range(tsplit):
                y_ref[:, D_MODEL * t:D_MODEL * (t + 1)] = y[rows * t:rows * (t + 1)]


def _moe(x1, h2, p, mp, tm, tsplit):
    rows = x1.shape[0]
    nrb = rows // tm
    rb = lambda r, g: (r, 0)
    grp = lambda r, g: (g, 0, 0)
    if tsplit == 1:
        p_spec = pl.BlockSpec((tm, PLE_DIM), rb)
        y_spec = pl.BlockSpec((tm, D_MODEL), rb)
        y_shape = (rows, D_MODEL)
    else:
        assert nrb == 1
        p_spec = _full(p.shape)
        y_shape = (rows // tsplit, tsplit * D_MODEL)
        y_spec = _full(y_shape)
    return pl.pallas_call(
        functools.partial(_moe_body, tsplit=tsplit),
        grid=(nrb, N_EXPERT_GROUPS),
        in_specs=[pl.BlockSpec((tm, D_MODEL), rb), pl.BlockSpec((tm, D_MODEL), rb), p_spec,
                  _full(mp["wr"].shape), _full(mp["br"].shape),
                  pl.BlockSpec((EXPERTS_PER_GROUP, D_MODEL, D_FF_EXPERT), grp),
                  pl.BlockSpec((EXPERTS_PER_GROUP, D_MODEL, D_FF_EXPERT), grp),
                  pl.BlockSpec((EXPERTS_PER_GROUP, D_FF_EXPERT, D_MODEL), grp),
                  _full(mp["wpg"].shape, True), _full(mp["wp"].shape, True), _full(mp["gf"].shape)],
        out_specs=y_spec,
        out_shape=jax.ShapeDtypeStruct(y_shape, F32),
        scratch_shapes=[pltpu.VMEM((tm, D_MODEL), F32), pltpu.VMEM((tm, LANES), F32)],
        compiler_params=_cparams(("arbitrary", "arbitrary"), 60),
        name="moe_ple",
    )(x1, h2, p, mp["wr"], mp["br"], mp["wg"], mp["wu"], mp["wd"], mp["wpg"], mp["wp"], mp["gf"])


def _s5_sample_body(u_ref, h0re_ref, h0im_ref, wre_ref, wim_ref, ar_ref, ai_ref, cre_ref, cim_ref, d_ref,
                    wglu_ref, bglu_ref, wso_ref, abr_ref, hre_out_ref, him_out_ref,
                    bure_ref, buim_ref, hre_ref, him_ref, *, nseq, nstep):
    u = u_ref[...]
    ub = u.astype(BF16)
    for j in range(4):
        lhs = ub[:, LANES * j:LANES * (j + 1)]
        bure_ref[:, 512 * j:512 * (j + 1)] = jnp.dot(lhs, wre_ref[j], preferred_element_type=F32)
        buim_ref[:, 512 * j:512 * (j + 1)] = jnp.dot(lhs, wim_ref[j], preferred_element_type=F32)
    for lc in range(4):
        sl = slice(512 * lc, 512 * (lc + 1))
        ar = jnp.broadcast_to(ar_ref[:, sl], (SUBLANES, 512))
        ai = jnp.broadcast_to(ai_ref[:, sl], (SUBLANES, 512))

        def body(rc, carry, sl=sl, ar=ar, ai=ai):
            r0 = pl.multiple_of(rc * SUBLANES, SUBLANES)
            hr = h0re_ref[pl.ds(r0, SUBLANES), sl]
            hi = h0im_ref[pl.ds(r0, SUBLANES), sl]
            for t in range(nstep):
                rr = pl.multiple_of(t * nseq + rc * SUBLANES, SUBLANES)
                hr, hi = (ar * hr - ai * hi + bure_ref[pl.ds(rr, SUBLANES), sl],
                          ar * hi + ai * hr + buim_ref[pl.ds(rr, SUBLANES), sl])
                hre_ref[pl.ds(rr, SUBLANES), sl] = hr
                him_ref[pl.ds(rr, SUBLANES), sl] = hi
            hre_out_ref[pl.ds(r0, SUBLANES), sl] = hr
            him_out_ref[pl.ds(r0, SUBLANES), sl] = hi
            return carry

        lax.fori_loop(0, nseq // SUBLANES, body, 0)
    parts = []
    for j in range(4):
        sl = slice(512 * j, 512 * (j + 1))
        parts.append(jnp.dot(hre_ref[:, sl].astype(BF16), cre_ref[j], preferred_element_type=F32)
                     - jnp.dot(him_ref[:, sl].astype(BF16), cim_ref[j], preferred_element_type=F32))
    y = jnp.concatenate(parts, axis=1) + d_ref[...] * u
    zg = jax.nn.gelu(y)
    gate = jnp.dot(zg.astype(BF16), wglu_ref[...], preferred_element_type=F32) + bglu_ref[...]
    glu = (zg * jax.nn.sigmoid(gate)).astype(BF16)
    abr_ref[...] = jnp.dot(glu, wso_ref[...], preferred_element_type=F32)


def _s5_sample(u_ts, h0re, h0im, sp, wglu, bglu, wso, nseq, nstep):
    rows = nseq * nstep
    ops = [u_ts, h0re, h0im, sp["wre"], sp["wim"], sp["ar"], sp["ai"], sp["cre"], sp["cim"], sp["d"], wglu, bglu, wso]
    return pl.pallas_call(
        functools.partial(_s5_sample_body, nseq=nseq, nstep=nstep),
        grid=(1,),
        in_specs=[_full(o.shape) for o in ops],
        out_specs=[_full((rows, D_MODEL)), _full((nseq, N_STATE)), _full((nseq, N_STATE))],
        out_shape=[jax.ShapeDtypeStruct((rows, D_MODEL), F32),
                   jax.ShapeDtypeStruct((nseq, N_STATE), F32), jax.ShapeDtypeStruct((nseq, N_STATE), F32)],
        scratch_shapes=[pltpu.VMEM((rows, N_STATE), F32) for _ in range(4)],
        compiler_params=_cparams(("arbitrary",), 56),
        name="s5_sample",
    )(*ops)


def _softmax_rows(s, valid):
    sm = jnp.where(valid, s, NEG_INF)
    mx = jnp.max(sm, axis=1, keepdims=True)
    e = jnp.where(valid, jnp.exp2(sm - mx), 0.0)
    l = jnp.sum(e, axis=1, keepdims=True)
    return e * (1.0 / jnp.maximum(l, 1e-30))


SAMPLE_SEQS_PER_STEP = 4
CMP_PITCH = 24


def _attn_sample_body(pt_ref, q_ref, gn_ref, nks_ref, nkw_ref, wint_ref, wk_ref, bk_ref, w2k_ref, ov_ref, e_ref, *rest,
                      npage, past_len, nsub, tq):
    o_ref, nwint_ref, xrow_ref = rest[2 * nsub * npage:]
    nrow = N_Q_HEADS * tq
    nwin = wint_ref.shape[2]
    nslc = -(-(past_len + tq) // SLC_BLOCK)
    nch = past_len // CMP_STRIDE
    per_page = PAGE_SIZE // CMP_STRIDE

    cmp = []
    for kv in range(2):
        for s in range(nsub):
            for p in range(npage):
                rows = rest[s * npage + p][0, LANES * kv:LANES * (kv + 1), :].T
                for c in range(per_page):
                    r0 = CMP_PITCH * (per_page * p + c)
                    xrow_ref[s, r0:r0 + CMP_STRIDE, :] = rows[CMP_STRIDE * c:CMP_STRIDE * (c + 1)]
        x = jnp.concatenate(
            [jnp.concatenate([xrow_ref[s, pl.ds(i, nch, stride=CMP_PITCH), :] for i in range(CMP_STRIDE)], axis=1)
             for s in range(nsub)], axis=0)
        pp = jnp.dot(x.astype(BF16), wk_ref[kv], preferred_element_type=F32)
        pre = pp[:, 0:LANES] + pltpu.roll(pp[:, LANES:2 * LANES], nsub * nch - 1, axis=0) + bk_ref[kv:kv + 1, :]
        cmp.append(jnp.dot(jax.nn.gelu(pre).astype(BF16), w2k_ref[kv], preferred_element_type=F32).astype(BF16))
    cks = [cmp[0][nch * s:nch * (s + 1)] for s in range(nsub)]
    cvs = [cmp[1][nch * s:nch * (s + 1)] for s in range(nsub)]
    slc_pages = [rest[(nsub + s) * npage:(nsub + s + 1) * npage] for s in range(nsub)]
    seqs = range(nsub)
    rcat = lambda parts: jnp.concatenate(parts, axis=0)

    lane_w = lax.broadcasted_iota(jnp.int32, (KV_W, LANES), 1)
    lane8 = lax.broadcasted_iota(jnp.int32, (tq, LANES), 1)
    nks_l, nkw_l, wint_l, qs_l = [], [], [], []
    for s in seqs:
        rows_s = slice(tq * s, tq * (s + 1))
        nks_l.append(jnp.concatenate([nks_ref[rows_s, :], jnp.zeros((LANES - tq, KV_W), F32)], axis=0))
        nkw = jnp.concatenate([nkw_ref[rows_s, :], jnp.zeros((LANES - tq, KV_W), F32)], axis=0)
        nkw_l.append(nkw)
        wint = wint_ref[s]
        wint_l.append(wint)
        shifted = pltpu.roll(wint, nwin - tq, axis=1)
        new_t = pltpu.roll(nkw.T, LANES - tq, axis=1)
        nwint_ref[s, :, 0:nwin - LANES] = shifted[:, 0:nwin - LANES]
        nwint_ref[s, :, nwin - LANES:nwin] = jnp.where(lane_w >= LANES - tq, new_t, shifted[:, nwin - LANES:nwin])
        q = q_ref[rows_s, :]
        qrows = []
        for j in range(N_Q_HEADS):
            chunk = q[:, LANES * (j // 2):LANES * (j // 2 + 1)]
            dst = j // GQA
            if (j % 2) != dst:
                chunk = pltpu.roll(chunk, HEAD_DIM, axis=1)
            keep = (lane8 < HEAD_DIM) if dst == 0 else (lane8 >= HEAD_DIM)
            qrows.append(jnp.where(keep, chunk, 0.0))
        qs_l.append(jnp.concatenate(qrows, axis=0).astype(BF16))

    rtot = nsub * nrow
    seq_rows = [slice(nrow * s, nrow * (s + 1)) for s in seqs]
    pos = past_len + (lax.broadcasted_iota(jnp.int32, (rtot, LANES), 0) & (tq - 1))
    lane = lax.broadcasted_iota(jnp.int32, (rtot, LANES), 1)

    sc = rcat([_dot_t(qs_l[s], cks[s]) for s in seqs])
    pc = _softmax_rows(sc, lane * CMP_STRIDE + (CMP_LEN - 1) <= pos).astype(BF16)
    oc = rcat([jnp.dot(pc[seq_rows[s]], cvs[s], preferred_element_type=F32) for s in seqs])
    imp = jnp.dot(pc, ov_ref[...], preferred_element_type=F32)
    vs = []
    for s in seqs:
        for h in range(N_KV_HEADS):
            r0 = nrow * s + tq * GQA * h
            v = imp[r0:r0 + tq]
            for g in range(1, GQA):
                v = v + imp[r0 + tq * g:r0 + tq * (g + 1)]
            vs.append(v)
    nsel = len(vs) * tq
    vt = rcat(vs + [jnp.zeros((LANES - nsel, LANES), F32)]).T
    nblk_pad = -(-nslc // SUBLANES) * SUBLANES
    blk_t = lax.broadcasted_iota(jnp.int32, (nblk_pad, LANES), 0)
    pos_t = past_len + (lax.broadcasted_iota(jnp.int32, (nblk_pad, LANES), 1) & (tq - 1))
    neg_t = _select_blocks(vt[0:nblk_pad], blk_t, pos_t, nslc, axis=0)
    neg = rcat([neg_t, jnp.zeros((LANES - nblk_pad, LANES), F32)]).T
    negsel = rcat([neg[tq * (N_KV_HEADS * s + j // GQA):tq * (N_KV_HEADS * s + j // GQA + 1)]
                   for s in seqs for j in range(N_Q_HEADS)])
    negsel_b = negsel.astype(BF16)

    new_blk = past_len // SLC_BLOCK
    ss_l = []
    for s in seqs:
        qaug = jnp.concatenate([qs_l[s], negsel_b[seq_rows[s]]], axis=1)
        parts = []
        for p in range(0, npage, 2):
            kt = jnp.concatenate([slc_pages[s][p][0][0:LANES], slc_pages[s][p + 1][0][0:LANES]], axis=1).astype(BF16)
            et = jnp.concatenate([e_ref[p], e_ref[p + 1]], axis=1)
            parts.append(jnp.dot(qaug, jnp.concatenate([kt, et], axis=0), preferred_element_type=F32))
        parts.append(_dot_t(qs_l[s], nks_l[s][:, 0:LANES].astype(BF16)) + negsel[seq_rows[s], new_blk:new_blk + 1])
        ss_l.append(jnp.concatenate(parts, axis=1))
    ss = rcat(ss_l)
    nkeys = ss.shape[1]
    kpos = lax.broadcasted_iota(jnp.int32, (rtot, nkeys), 1)
    pos_k = past_len + (lax.broadcasted_iota(jnp.int32, (rtot, nkeys), 0) & (tq - 1))
    ps = _softmax_rows(ss, kpos <= pos_k).astype(BF16)
    osel_l = []
    for s in seqs:
        psq = ps[seq_rows[s]]
        o = jnp.dot(psq[:, past_len:nkeys], nks_l[s][:, LANES:2 * LANES].astype(BF16), preferred_element_type=F32)
        for p in range(0, npage, 2):
            vtp = jnp.concatenate([slc_pages[s][p][0][LANES:2 * LANES], slc_pages[s][p + 1][0][LANES:2 * LANES]],
                                  axis=1).astype(BF16)
            o = o + _dot_t(psq[:, PAGE_SIZE * p:PAGE_SIZE * (p + 2)], vtp)
        osel_l.append(o)
    osel = rcat(osel_l)

    sw = rcat([jnp.concatenate([jnp.dot(qs_l[s], wint_l[s][0:LANES].astype(BF16), preferred_element_type=F32),
                                _dot_t(qs_l[s], nkw_l[s][:, 0:LANES].astype(BF16))], axis=1) for s in seqs])
    nw = sw.shape[1]
    widx = lax.broadcasted_iota(jnp.int32, (rtot, nw), 1)
    pos_w = past_len + (lax.broadcasted_iota(jnp.int32, (rtot, nw), 0) & (tq - 1))
    dlt = pos_w - (past_len - nwin + widx)
    pw = _softmax_rows(sw, (dlt >= 0) & (dlt < WINDOW) & (widx < nwin + tq)).astype(BF16)
    ow = rcat([_dot_t(pw[seq_rows[s], 0:nwin], wint_l[s][LANES:2 * LANES].astype(BF16))
               + jnp.dot(pw[seq_rows[s], nwin:nw], nkw_l[s][:, LANES:2 * LANES].astype(BF16),
                         preferred_element_type=F32) for s in seqs])

    for s in seqs:
        rows_s = slice(tq * s, tq * (s + 1))
        gn = gn_ref[rows_s, :]
        for c in range(N_Q_HEADS // 2):
            halves = []
            for hh in range(2):
                j = 2 * c + hh
                rs = slice(nrow * s + tq * j, nrow * s + tq * (j + 1))
                oj = (gn[:, 3 * j:3 * j + 1] * oc[rs] + gn[:, 3 * j + 1:3 * j + 2] * osel[rs]
                      + gn[:, 3 * j + 2:3 * j + 3] * ow[rs])
                if (j // GQA) != hh:
                    oj = pltpu.roll(oj, HEAD_DIM, axis=1)
                halves.append(oj)
            o_ref[rows_s, LANES * c:LANES * (c + 1)] = jnp.where(lane8 < HEAD_DIM, halves[0], halves[1])


def _attn_sample(q, gn, nks, nkw, cache_cmp, cache_slc, cache_win, page_table, cp, nseq, tq, past_len):
    assert tq <= CMP_STRIDE and past_len % PAGE_SIZE == 0
    npage = past_len // PAGE_SIZE
    assert npage % 2 == 0 and PAGE_SIZE == LANES
    n_pool = cache_cmp.shape[0]
    nwin = cache_win.shape[1]
    chunks = past_len // CMP_STRIDE
    ov = _overlap_t(chunks, LANES).T
    key = np.arange(past_len).reshape(npage, 1, PAGE_SIZE)
    e = jnp.asarray(np.arange(LANES).reshape(1, LANES, 1) == key // SLC_BLOCK, dtype=BF16)
    to_t = lambda c: jnp.transpose(c, (0, 2, 3, 4, 1)).reshape(c.shape[0], KV_W, c.shape[1])
    cmp_t, slc_t, win_t = to_t(cache_cmp), to_t(cache_slc), to_t(cache_win)
    nsub = SAMPLE_SEQS_PER_STEP
    assert nseq % nsub == 0
    row = lambda n, pt: (n, 0)
    seq3 = lambda n, pt: (n, 0, 0)
    page = lambda s, p: (lambda n, pt: (pt[n * nsub + s, p], 0, 0))
    consts = [cp["wk"], cp["bk"], cp["w2k"], ov, e]
    in_specs = [pl.BlockSpec((nsub * tq, Q_W), row), pl.BlockSpec((nsub * tq, LANES), row),
                pl.BlockSpec((nsub * tq, KV_W), row), pl.BlockSpec((nsub * tq, KV_W), row),
                pl.BlockSpec((nsub, KV_W, nwin), seq3)]
    in_specs += [pl.BlockSpec(c.shape, (lambda nd: lambda n, pt: (0,) * nd)(c.ndim)) for c in consts]
    pages = [pl.BlockSpec((1, KV_W, PAGE_SIZE), page(s, p)) for s in range(nsub) for p in range(npage)]
    in_specs += pages * 2
    grid_spec = pltpu.PrefetchScalarGridSpec(
        num_scalar_prefetch=1,
        grid=(nseq // nsub,),
        in_specs=in_specs,
        out_specs=[pl.BlockSpec((nsub * tq, Q_W), row), pl.BlockSpec((nsub, KV_W, nwin), seq3)],
        scratch_shapes=[pltpu.VMEM((nsub, chunks * CMP_PITCH, LANES), F32)],
    )
    return pl.pallas_call(
        functools.partial(_attn_sample_body, npage=npage, past_len=past_len, nsub=nsub, tq=tq),
        grid_spec=grid_spec,
        out_shape=[jax.ShapeDtypeStruct((nseq * tq, Q_W), F32), jax.ShapeDtypeStruct((nseq, KV_W, nwin), F32)],
        compiler_params=_cparams(("arbitrary",), 56),
        name="attn_sample",
    )(page_table, q, gn, nks, nkw, win_t, *consts, *([cmp_t] * (nsub * npage)), *([slc_t] * (nsub * npage)))


def _moe_params(w_rg, b_rg, w_re, b_re, w_gate, w_up, w_down, w_ple, w_ple_gate, gf):
    pad = LANES - N_EXPERT_GROUPS - N_EXPERTS
    return {"wr": jnp.pad(jnp.concatenate([w_rg, w_re], axis=1), ((0, 0), (0, pad))).astype(BF16),
            "br": jnp.pad(jnp.concatenate([b_rg, b_re]), (0, pad)).astype(F32).reshape(1, LANES),
            "wg": w_gate.astype(BF16), "wu": w_up.astype(BF16), "wd": w_down.astype(BF16),
            "wpg": w_ple_gate.astype(BF16), "wp": w_ple.astype(BF16), "gf": gf.astype(F32).reshape(1, D_MODEL)}


TM_PROMPT = 512
TM_MOE = 1024
TC_S5 = 128


def kernel(x_prompt, x_sample, p_prompt, p_sample, cache_cmp_kv, cache_slc_kv, cache_win_kv, state_ssm, page_table, norm1_g, w_in, ssm_lam_re, ssm_lam_im, ssm_log_dt, ssm_b_re, ssm_b_im, ssm_c_re, ssm_c_im, ssm_d, w_glu, b_glu, cmp_pe, cmp_w1, cmp_w2, w_ssm_out, w_nsa_out, w_o, norm2_g, w_route_group, b_route_group, w_route_expert, b_route_expert, w_exp_gate, w_exp_up, w_exp_down, w_ple, w_ple_gate, final_norm_g):
    assert w_in.shape[0] == 1, "one layer"
    l = 0
    nb, t = x_prompt.shape[:2]
    ns, ts = x_sample.shape[:2]
    past_len = page_table.shape[1] * PAGE_SIZE
    kvt = (2, N_KV_HEADS, HEAD_DIM)

    wi = _inproj_params(w_in[l])
    g1 = norm1_g[l].astype(F32).reshape(1, D_MODEL)
    g2 = norm2_g[l].astype(F32).reshape(1, D_MODEL)
    sp = _s5_params(ssm_lam_re[l], ssm_lam_im[l], ssm_log_dt[l], ssm_b_re[l], ssm_b_im[l], ssm_c_re[l], ssm_c_im[l],
                    ssm_d[l])
    cp = _cmp_params(cmp_pe[l], cmp_w1[l], cmp_w2[l])
    mp = _moe_params(w_route_group[l], b_route_group[l], w_route_expert[l], b_route_expert[l], w_exp_gate[l],
                     w_exp_up[l], w_exp_down[l], w_ple[l], w_ple_gate[l], final_norm_g)
    wglu = w_glu[l].astype(BF16)
    bglu = b_glu[l].astype(F32).reshape(1, SSM_WIDTH)
    wso = w_ssm_out[l].astype(BF16)
    wno = w_nsa_out[l].astype(BF16)
    wo = w_o[l].astype(BF16)

    lay = _prompt_layout(nb, t, TM_PROMPT)
    xp = x_prompt.reshape(nb * t, D_MODEL)
    r = _inproj_prompt(xp, lay, g1, wi)
    abr, hlast = _s5_prompt(r["u"], sp, wglu, bglu, wso, t, TC_S5)
    ck, cvt = _compress_prompt(r["kvc"], cp, nb, t)
    onsa = _attn_prompt(r["qt"], r["gnt"], ck, cvt, r["ksb"], r["vst"], r["kwb"], r["vwt"], nb, t)
    x1, h2 = _post(xp, abr, "a", onsa, r["ga"], r["gb"], wno, wo, g2, lay, "a")
    y_prompt = _moe(x1, h2, p_prompt[l].reshape(nb * t, PLE_DIM), mp, TM_MOE, 1).reshape(nb, t, D_MODEL)
    keep = min(WINDOW, t)

    def rows_last(a):
        return jnp.transpose(a.reshape((a.shape[0],) + kvt + (a.shape[2],)), (0, 4, 1, 2, 3))[None]

    new_cmp_p = rows_last(r["kvct"])
    new_slc_p = rows_last(r["kvst"])
    new_win_p = rows_last(r["kvwt"][:, :, t - keep:])
    new_ssm_p = jnp.stack([hlast[0:nb], hlast[nb:2 * nb]], axis=-1).reshape(1, nb, N_SSM_GROUPS, SSM_STATE, 2)

    lays = _sample_layout(ns, ts)
    xs = x_sample.reshape(ns * ts, D_MODEL)
    rs = _inproj_sample(xs, lays, g1, wi)
    h0 = state_ssm[l].astype(F32).reshape(ns, N_STATE, 2)
    abr_s, hre, him = _s5_sample(rs["u"], h0[..., 0], h0[..., 1], sp, wglu, bglu, wso, ns, ts)
    onsa_s, new_win = _attn_sample(rs["q"].reshape(ns * ts, Q_W), rs["gn"].reshape(ns * ts, LANES),
                                   rs["kvs"].reshape(ns * ts, KV_W), rs["kvw"].reshape(ns * ts, KV_W),
                                   cache_cmp_kv[l], cache_slc_kv[l], cache_win_kv[l], page_table, cp, ns, ts, past_len)
    x1s, h2s = _post(xs, abr_s, "b", onsa_s, rs["ga"], rs["gb"], wno, wo, g2, lays, "b")
    y_sample = _moe(x1s, h2s, p_sample[l].reshape(ns, ts * PLE_DIM), mp, ns * ts, ts).reshape(ns, ts, D_MODEL)
    steps_first = lambda a: jnp.transpose(a.reshape((ts,) + kvt + (ns,)), (4, 0, 1, 2, 3))[None]
    new_cmp_s = steps_first(rs["kvct"])
    new_slc_s = steps_first(rs["kvst"])
    new_win_s = rows_last(new_win)
    new_ssm_s = jnp.stack([hre, him], axis=-1).reshape(1, ns, N_SSM_GROUPS, SSM_STATE, 2)
    return (y_prompt, y_sample, new_cmp_p, new_slc_p, new_win_p, new_ssm_p,
            new_cmp_s, new_slc_s, new_win_s, new_ssm_s)
```

```python
import functools
import math

import jax
import jax.numpy as jnp
import numpy as np
from jax import lax
from jax.experimental import pallas as pl
from jax.experimental.pallas import tpu as pltpu

F32 = jnp.float32
BF16 = jnp.bfloat16

D_MODEL = 1024
SSM_WIDTH = 512
SSM_GROUP = 16
N_SSM_GROUPS = 32
SSM_STATE = 64
HEAD_DIM = 64
N_Q_HEADS = 8
N_KV_HEADS = 2
GQA = 4
CMP_LEN = 32
CMP_STRIDE = 16
SLC_BLOCK = 64
TOP_N = 8
WINDOW = 512
Q_BLOCK = 256
NEG_INF = -1e30
FORCE_BONUS = 1e4
Q_W = 512
KV_W = 256
NSA_GATE_W = 24
N_EXPERT_GROUPS = 4
EXPERTS_PER_GROUP = 4
N_EXPERTS = 16
D_FF_EXPERT = 256
PLE_DIM = 256
RMS_EPS = 1e-6
PAGE_SIZE = 128

LANES = 128
SUBLANES = 8
N_STATE = N_SSM_GROUPS * SSM_STATE
MIB = 2 ** 20


def _cparams(sem, vmem_mib):
    return pltpu.CompilerParams(dimension_semantics=sem, vmem_limit_bytes=vmem_mib * MIB)


def _full(shape, single_buffer=False):
    nd = len(shape)
    if single_buffer:
        return pl.BlockSpec(shape, lambda *_: (0,) * nd, pipeline_mode=pl.Buffered(1))
    return pl.BlockSpec(shape, lambda *_: (0,) * nd)


def _prompt_layout(nseq, t, tm):
    nb = t // tm
    return {
        "grid": (nb, nseq), "tm": tm, "nseq": nseq, "t": t,
        "a": lambda w: ((nseq * t, w), pl.BlockSpec((tm, w), lambda b, s: (s * nb + b, 0))),
    }


def _sample_layout(nseq, t):
    return {
        "grid": (1, t), "tm": nseq,
        "a": lambda w: ((nseq, t * w), pl.BlockSpec((nseq, w), lambda s, b: (0, b))),
        "b": lambda w: ((t * nseq, w), pl.BlockSpec((nseq, w), lambda s, b: (b, 0))),
    }


TK_SLC = 512
TK_WIN = 128


Q_SCALE = HEAD_DIM ** -0.5 * math.log2(math.e)
C_U, C_Q, C_KVC, C_KVS, C_KVW = 0, 512, 1024, 1280, 1536
N_MAIN = 1792


def _dot_t(a, b):
    return lax.dot_general(a, b, (((1,), (1,)), ((), ())), preferred_element_type=F32)


GN_ROWS = 32


def _inproj_prompt_body(x_ref, g_ref, wa_ref, wgn_ref, wgab_ref,
                        u_ref, kvc_ref, ksb_ref, kwb_ref, ga_ref, gb_ref,
                        qt_ref, kvct_ref, kvst_ref, kvwt_ref, gnt_ref, vst_ref, vwt_ref, *, nseq):
    s = pl.program_id(1)
    x = x_ref[...]
    inv = lax.rsqrt(jnp.mean(x * x, axis=-1, keepdims=True) + RMS_EPS)
    h = (x * inv * g_ref[...]).astype(BF16)
    tm = h.shape[0]

    def mm(w):
        return jnp.dot(h, w, preferred_element_type=F32)

    u = mm(wa_ref[:, C_U:C_U + SSM_WIDTH])
    for j in range(SSM_WIDTH // LANES):
        u_ref[j, pl.ds(s, tm, stride=nseq), :] = u[:, LANES * j:LANES * (j + 1)]
    ga_ref[...] = jax.nn.sigmoid(mm(wgab_ref[:, 0:D_MODEL])).astype(BF16)
    gb_ref[...] = jax.nn.sigmoid(mm(wgab_ref[:, D_MODEL:2 * D_MODEL])).astype(BF16)
    kvc = mm(wa_ref[:, C_KVC:C_KVC + KV_W])
    kvc_ref[0] = kvc[:, 0:LANES]
    kvc_ref[1] = kvc[:, LANES:2 * LANES]
    kvct_ref[0] = kvc.T
    kvs = mm(wa_ref[:, C_KVS:C_KVS + KV_W])
    ksb_ref[...] = kvs[:, 0:LANES].astype(BF16)
    kvst = kvs.T
    kvst_ref[0] = kvst
    kvw = mm(wa_ref[:, C_KVW:C_KVW + KV_W])
    kwb_ref[...] = kvw[:, 0:LANES].astype(BF16)
    kvwt = kvw.T
    kvwt_ref[0] = kvwt
    for c in range(tm // TK_WIN):
        vst_ref[0, c] = kvst[LANES:2 * LANES, c * TK_WIN:(c + 1) * TK_WIN].astype(BF16)
        vwt_ref[0, c] = kvwt[LANES:2 * LANES, c * TK_WIN:(c + 1) * TK_WIN].astype(BF16)
    qt_ref[0] = (mm(wa_ref[:, C_Q:C_Q + Q_W]) * Q_SCALE).T.astype(BF16)
    gnt_ref[0] = jax.nn.sigmoid(mm(wgn_ref[...])).T[0:GN_ROWS]


def _inproj_prompt(x2d, lay, g, w):
    tm, nseq, t = lay["tm"], lay["nseq"], lay["t"]
    nb = t // tm
    out_shapes, out_specs, names = [], [], []

    def add(name, shape_spec, dt):
        names.append(name)
        out_shapes.append(jax.ShapeDtypeStruct(shape_spec[0], dt))
        out_specs.append(shape_spec[1])

    def tr(rows):
        return (nseq, rows, t), pl.BlockSpec((1, rows, tm), lambda b, s: (s, 0, b))

    nu = SSM_WIDTH // LANES
    add("u", ((nu, t * nseq, LANES), pl.BlockSpec((nu, tm * nseq, LANES), lambda b, s: (0, b, 0))), F32)
    add("kvc", ((2, nseq * t, LANES), pl.BlockSpec((2, tm, LANES), lambda b, s: (0, s * nb + b, 0))), F32)
    add("ksb", lay["a"](LANES), BF16)
    add("kwb", lay["a"](LANES), BF16)
    add("ga", lay["a"](D_MODEL), BF16)
    add("gb", lay["a"](D_MODEL), BF16)
    add("qt", tr(Q_W), BF16)
    add("kvct", tr(KV_W), F32)
    add("kvst", tr(KV_W), F32)
    add("kvwt", tr(KV_W), F32)
    add("gnt", tr(GN_ROWS), F32)
    for name in ("vst", "vwt"):
        add(name, ((nseq, t // TK_WIN, LANES, TK_WIN),
                   pl.BlockSpec((1, tm // TK_WIN, LANES, TK_WIN), lambda b, s: (s, b, 0, 0))), BF16)
    x_shape, x_spec = lay["a"](D_MODEL)
    ops = [g, w["wa"], w["wgn"], w["wgab"]]
    outs = pl.pallas_call(
        functools.partial(_inproj_prompt_body, nseq=nseq),
        grid=lay["grid"],
        in_specs=[x_spec] + [_full(o.shape) for o in ops],
        out_specs=out_specs,
        out_shape=out_shapes,
        compiler_params=_cparams(("arbitrary",) * 2, 56),
        name="inproj_prompt",
    )(x2d.reshape(x_shape), *ops)
    return dict(zip(names, outs))


def _inproj_sample_body(x_ref, g_ref, wa_ref, wgn_ref, wgab_ref,
                        u_ref, q_ref, kvs_ref, kvw_ref, gn_ref, ga_ref, gb_ref, kvct_ref, kvst_ref, kvwt_ref):
    x = x_ref[...]
    inv = lax.rsqrt(jnp.mean(x * x, axis=-1, keepdims=True) + RMS_EPS)
    h = (x * inv * g_ref[...]).astype(BF16)

    def mm(w):
        return jnp.dot(h, w, preferred_element_type=F32)

    u_ref[...] = mm(wa_ref[:, C_U:C_U + SSM_WIDTH])
    q_ref[...] = mm(wa_ref[:, C_Q:C_Q + Q_W]) * Q_SCALE
    kvs = mm(wa_ref[:, C_KVS:C_KVS + KV_W])
    kvs_ref[...] = kvs
    kvw = mm(wa_ref[:, C_KVW:C_KVW + KV_W])
    kvw_ref[...] = kvw
    gn_ref[...] = jax.nn.sigmoid(mm(wgn_ref[...]))
    ga_ref[...] = jax.nn.sigmoid(mm(wgab_ref[:, 0:D_MODEL])).astype(BF16)
    gb_ref[...] = jax.nn.sigmoid(mm(wgab_ref[:, D_MODEL:2 * D_MODEL])).astype(BF16)
    kvct_ref[0] = mm(wa_ref[:, C_KVC:C_KVC + KV_W]).T
    kvst_ref[0] = kvs.T
    kvwt_ref[0] = kvw.T


def _inproj_sample(x2d, lay, g, w):
    nseq = lay["tm"]
    ts = lay["grid"][1]
    names = ["u", "q", "kvs", "kvw", "gn", "ga", "gb"]
    widths = [SSM_WIDTH, Q_W, KV_W, KV_W, LANES, D_MODEL, D_MODEL]
    out_shapes, out_specs = [], []
    for n, wd in zip(names, widths):
        shp, spec = lay["b" if n == "u" else "a"](wd)
        out_shapes.append(jax.ShapeDtypeStruct(shp, BF16 if n in ("ga", "gb") else F32))
        out_specs.append(spec)
    for n in ("kvct", "kvst", "kvwt"):
        names.append(n)
        out_shapes.append(jax.ShapeDtypeStruct((ts, KV_W, nseq), F32))
        out_specs.append(pl.BlockSpec((1, KV_W, nseq), lambda s, b: (b, 0, 0)))
    x_shape, x_spec = lay["a"](D_MODEL)
    ops = [g, w["wa"], w["wgn"], w["wgab"]]
    outs = pl.pallas_call(
        _inproj_sample_body,
        grid=lay["grid"],
        in_specs=[x_spec] + [_full(o.shape) for o in ops],
        out_specs=out_specs,
        out_shape=out_shapes,
        compiler_params=_cparams(("arbitrary",) * 2, 56),
        name="inproj_sample",
    )(x2d.reshape(x_shape), *ops)
    return dict(zip(names, outs))


def _inproj_params(w_in0):
    return {"wa": w_in0[:, :N_MAIN].astype(BF16),
            "wgn": jnp.pad(w_in0[:, N_MAIN:N_MAIN + NSA_GATE_W], ((0, 0), (0, LANES - NSA_GATE_W))).astype(BF16),
            "wgab": w_in0[:, N_MAIN + NSA_GATE_W:].astype(BF16)}


def _s5_prompt_body(u_ref, wb_ref, ar_ref, ai_ref, cw_ref, d_ref, wglu_ref, bglu_ref, wso_ref,
                    abr_ref, hlast_ref, lhs_ref, bu_ref, h8_ref, p_ref, hstate_ref):
    c = pl.program_id(0)
    nseq = 4
    r4 = u_ref.shape[1]
    tc = r4 // nseq
    half = tc // 2

    @pl.when(c == 0)
    def _():
        hstate_ref[...] = jnp.zeros_like(hstate_ref)

    u = jnp.concatenate([u_ref[j] for j in range(SSM_WIDTH // LANES)], axis=1)
    row2 = lax.broadcasted_iota(jnp.int32, (r4, SSM_WIDTH), 0)
    lo2 = (row2 % SUBLANES) < nseq
    up = pltpu.roll(u, r4 - nseq, axis=0)
    dn = pltpu.roll(u, nseq, axis=0)
    swapped = jnp.where(lo2, up, dn)
    zero = jnp.zeros_like(u)
    ev_re = jnp.where(lo2, u, zero).astype(BF16).reshape(half, SUBLANES, SSM_WIDTH)
    ev_im = jnp.where(lo2, zero, swapped).astype(BF16).reshape(half, SUBLANES, SSM_WIDTH)
    od_re = jnp.where(lo2, swapped, zero).astype(BF16).reshape(half, SUBLANES, SSM_WIDTH)
    od_im = jnp.where(lo2, zero, u).astype(BF16).reshape(half, SUBLANES, SSM_WIDTH)
    for j in range(4):
        sl = slice(LANES * j, LANES * (j + 1))
        lhs_ref[:, 0:8, 256 * j:256 * j + LANES] = ev_re[:, :, sl]
        lhs_ref[:, 0:8, 256 * j + LANES:256 * (j + 1)] = ev_im[:, :, sl]
        lhs_ref[:, 8:16, 256 * j:256 * j + LANES] = od_re[:, :, sl]
        lhs_ref[:, 8:16, 256 * j + LANES:256 * (j + 1)] = od_im[:, :, sl]
    for j in range(4):
        lhs = lhs_ref[:, :, 256 * j:256 * (j + 1)].reshape(tc * SUBLANES, 256)
        bu_ref[:, 512 * j:512 * (j + 1)] = jnp.dot(lhs, wb_ref[j], preferred_element_type=F32)

    for lc in range(4):
        sl = slice(512 * lc, 512 * (lc + 1))
        ar = ar_ref[:, sl]
        ai = ai_ref[:, sl]

        def step(t, h, sl=sl, ar=ar, ai=ai):
            r0 = pl.multiple_of(t * SUBLANES, SUBLANES)
            h = ar * h + ai * pltpu.roll(h, nseq, axis=0) + bu_ref[pl.ds(r0, SUBLANES), sl]
            h8_ref[pl.ds(r0, SUBLANES), sl] = h
            return h

        hstate_ref[:, sl] = lax.fori_loop(0, tc, step, hstate_ref[:, sl], unroll=8)
    hlast_ref[...] = hstate_ref[...]

    for j in range(4):
        pj = jnp.dot(h8_ref[:, 512 * j:512 * (j + 1)].astype(BF16), cw_ref[j], preferred_element_type=F32)
        p_ref[2 * j] = pj[:, 0:LANES]
        p_ref[2 * j + 1] = pj[:, LANES:2 * LANES]
    ys = []
    for s in range(nseq):
        parts = []
        for j in range(4):
            re = p_ref[2 * j, pl.ds(s, tc, stride=SUBLANES), :]
            im = p_ref[2 * j + 1, pl.ds(nseq + s, tc, stride=SUBLANES), :]
            us = u_ref[j, pl.ds(s, tc, stride=nseq), :]
            parts.append(re + im + d_ref[:, LANES * j:LANES * (j + 1)] * us)
        ys.append(jnp.concatenate(parts, axis=1))
    y = jnp.concatenate(ys, axis=0)
    zg = jax.nn.gelu(y)
    gate = jnp.dot(zg.astype(BF16), wglu_ref[...], preferred_element_type=F32) + bglu_ref[...]
    glu = (zg * jax.nn.sigmoid(gate)).astype(BF16)
    abr = jnp.dot(glu, wso_ref[...], preferred_element_type=F32)
    for s in range(nseq):
        abr_ref[s] = abr[s * tc:(s + 1) * tc].astype(BF16)


def _s5_prompt(u_ts, sp, wglu, bglu, wso, t_total, tc):
    nseq = 4
    grid = (t_total // tc,)
    abr, hlast = pl.pallas_call(
        _s5_prompt_body,
        grid=grid,
        in_specs=[pl.BlockSpec((SSM_WIDTH // LANES, tc * nseq, LANES), lambda c: (0, c, 0)),
                  _full(sp["wb8"].shape), _full(sp["ar8"].shape), _full(sp["ai8"].shape), _full(sp["cw8"].shape),
                  _full(sp["d"].shape), _full(wglu.shape), _full(bglu.shape), _full(wso.shape)],
        out_specs=[pl.BlockSpec((nseq, tc, D_MODEL), lambda c: (0, c, 0)),
                   pl.BlockSpec((SUBLANES, N_STATE), lambda c: (0, 0))],
        out_shape=[jax.ShapeDtypeStruct((nseq, t_total, D_MODEL), BF16),
                   jax.ShapeDtypeStruct((SUBLANES, N_STATE), F32)],
        scratch_shapes=[pltpu.VMEM((tc // 2, 2 * SUBLANES, 1024), BF16),
                        pltpu.VMEM((tc * SUBLANES, N_STATE), F32),
                        pltpu.VMEM((tc * SUBLANES, N_STATE), F32),
                        pltpu.VMEM((8, tc * SUBLANES, LANES), F32),
                        pltpu.VMEM((SUBLANES, N_STATE), F32)],
        compiler_params=_cparams(("arbitrary",), 56),
        name="s5_prompt",
    )(u_ts, sp["wb8"], sp["ar8"], sp["ai8"], sp["cw8"], sp["d"], wglu, bglu, wso)
    return abr, hlast


def _s5_params(lam_re, lam_im, log_dt, b_re, b_im, c_re, c_im, d_skip):
    lam = lax.complex(lam_re.astype(F32), lam_im.astype(F32))
    dt = jnp.exp(log_dt.astype(F32))[:, None]
    a_bar = jnp.exp(lam * dt)
    b = lax.complex(b_re.astype(F32), b_im.astype(F32))
    b_bar = ((a_bar - 1.0) / lam)[..., None] * b
    eye8 = jnp.eye(8, dtype=F32)

    def bd_b(m):
        return jnp.einsum("ab,jbpc->jacbp", eye8, m.reshape(4, 8, SSM_STATE, SSM_GROUP)).reshape(4, 128, 512)

    def bd_c(m):
        return jnp.einsum("ab,jbcp->japbc", eye8, m.reshape(4, 8, SSM_GROUP, SSM_STATE)).reshape(4, 512, 128)

    wre, wim = bd_b(b_bar.real), bd_b(b_bar.imag)
    cre, cim = bd_c(c_re.astype(F32)), bd_c(c_im.astype(F32))
    ar = a_bar.real.reshape(1, N_STATE)
    ai = a_bar.imag.reshape(1, N_STATE)
    sign = jnp.concatenate([-jnp.ones((4, 1), F32), jnp.ones((4, 1), F32)], axis=0)
    return {
        "wb8": jnp.concatenate([wre, wim], axis=1).astype(BF16),
        "cw8": jnp.concatenate([cre, -cim], axis=2).astype(BF16),
        "ar8": jnp.broadcast_to(ar, (SUBLANES, N_STATE)),
        "ai8": sign * ai,
        "wre": wre.astype(BF16), "wim": wim.astype(BF16),
        "cre": cre.astype(BF16), "cim": cim.astype(BF16),
        "ar": ar, "ai": ai,
        "d": d_skip.astype(F32).reshape(1, SSM_WIDTH),
    }


def _cmp_params(cmp_pe, cmp_w1, cmp_w2):
    eye2 = jnp.eye(2, dtype=F32)
    nhalf = CMP_LEN // CMP_STRIDE
    w1r = cmp_w1.astype(F32).reshape(2, nhalf, CMP_STRIDE, HEAD_DIM, HEAD_DIM)
    wk = jnp.einsum("kside,ph->kipdshe", w1r, eye2).reshape(2, CMP_STRIDE * LANES, nhalf * LANES)
    bk = jnp.einsum("kld,klde->ke", cmp_pe.astype(F32), cmp_w1.astype(F32), precision=lax.Precision.HIGHEST)
    w2k = jnp.einsum("kef,ph->kpehf", cmp_w2.astype(F32), eye2).reshape(2, LANES, LANES)
    return {"wk": wk.astype(BF16), "bk": jnp.tile(bk, (1, N_KV_HEADS)), "w2k": w2k.astype(BF16),
            "w2kt": jnp.swapaxes(w2k, 1, 2).astype(BF16)}


def _compress_hidden(tap, nch, kv, wk_ref, bk_ref):
    x = jnp.concatenate([tap(i).astype(BF16) for i in range(CMP_STRIDE)], axis=1)
    pp = jnp.dot(x, wk_ref[kv], preferred_element_type=F32)
    pre = pp[:, 0:LANES] + pltpu.roll(pp[:, LANES:2 * LANES], nch - 1, axis=0) + bk_ref[kv:kv + 1, :]
    return jax.nn.gelu(pre).astype(BF16)


def _compress_prompt_body(x_ref, wk_ref, bk_ref, w2k_ref, w2kt_ref, ck_ref, cvt_ref):
    nch = x_ref.shape[1] // CMP_STRIDE
    hid = [_compress_hidden(lambda i, kv=kv: x_ref[kv, pl.ds(i, nch, stride=CMP_STRIDE), :], nch, kv, wk_ref, bk_ref)
           for kv in range(2)]
    ck_ref[0] = jnp.dot(hid[0], w2k_ref[0], preferred_element_type=F32).astype(BF16)
    cvt_ref[0] = _dot_t(w2kt_ref[1], hid[1]).astype(BF16)


def _compress_prompt(kvc2, cp, nseq, t):
    nch = t // CMP_STRIDE
    return pl.pallas_call(
        _compress_prompt_body,
        grid=(nseq,),
        in_specs=[pl.BlockSpec((2, t, LANES), lambda n: (0, n, 0)),
                  _full(cp["wk"].shape), _full(cp["bk"].shape), _full(cp["w2k"].shape), _full(cp["w2kt"].shape)],
        out_specs=[pl.BlockSpec((1, nch, LANES), lambda n: (n, 0, 0)),
                   pl.BlockSpec((1, LANES, nch), lambda n: (n, 0, 0))],
        out_shape=[jax.ShapeDtypeStruct((nseq, nch, LANES), BF16),
                   jax.ShapeDtypeStruct((nseq, LANES, nch), BF16)],
        compiler_params=_cparams(("arbitrary",), 48),
        name="compress_prompt",
    )(kvc2, cp["wk"], cp["bk"], cp["w2k"], cp["w2kt"])


def _overlap_t(n_cmp_pad, n_slc_pad):
    j = np.arange(n_cmp_pad)[None, :]
    s = np.arange(n_slc_pad)[:, None]
    ov = (j * CMP_STRIDE <= s * SLC_BLOCK + SLC_BLOCK - 1) & (j * CMP_STRIDE + CMP_LEN - 1 >= s * SLC_BLOCK)
    return jnp.asarray(ov, dtype=BF16)


def _softmax_cols(s, valid):
    sm = jnp.where(valid, s, NEG_INF)
    mx = jnp.max(sm, axis=0, keepdims=True)
    e = jnp.where(valid, jnp.exp2(sm - mx), 0.0)
    l = jnp.sum(e, axis=0, keepdims=True)
    return e * (1.0 / jnp.maximum(l, 1e-30))


def _select_blocks(imp, blk, pos, nblk, axis=0):
    cur = pos // SLC_BLOCK
    forced = (blk == 0) | (blk == cur) | (blk == cur - 1)
    v = jnp.where(forced, imp + FORCE_BONUS, imp)
    v = jnp.where(blk * SLC_BLOCK <= pos, v, NEG_INF)
    v = jnp.where(blk < nblk, v, -3e38)
    blk_f = blk.astype(F32)
    neg = jnp.full(imp.shape, NEG_INF, F32)
    for _ in range(min(TOP_N, nblk)):
        mx = jnp.max(v, axis=axis, keepdims=True)
        first = jnp.min(jnp.where(v == mx, blk_f, float(imp.shape[axis])), axis=axis, keepdims=True)
        pick = blk_f == first
        neg = jnp.where(pick, 0.0, neg)
        v = jnp.where(pick, -3e38, v)
    return neg


CB = 2 * LANES


def _attn_prompt_body(q_ref, gn_ref, ck_ref, cvt_ref, ks_ref, vst_ref, kw_ref, vwt_ref, ovt_ref,
                      o_ref, kaug_ref, kwaug_ref, qaug_ref, acc_ref):
    i = pl.program_id(1)
    t = ks_ref.shape[1]
    nch = ck_ref.shape[1]
    nslc = t // SLC_BLOCK
    qb = Q_BLOCK
    ncol = N_Q_HEADS * qb
    ncb = ncol // CB
    q0 = i * qb
    one_row = 2 * LANES - HEAD_DIM

    @pl.when(i == 0)
    def _():
        kaug_ref[:, 0:LANES] = ks_ref[0]
        blk = lax.broadcasted_iota(jnp.int32, (t, LANES), 0) // SLC_BLOCK
        col = lax.broadcasted_iota(jnp.int32, (t, LANES), 1)
        kaug_ref[:, LANES:2 * LANES] = jnp.where(blk == col, 1.0, 0.0).astype(BF16)
        padcol = lax.broadcasted_iota(jnp.int32, (WINDOW, 2 * LANES), 1)
        kwaug_ref[0:WINDOW, :] = jnp.where(padcol == one_row, NEG_INF, 0.0).astype(BF16)
        kwaug_ref[WINDOW:WINDOW + t, 0:LANES] = kw_ref[0]
        kwaug_ref[WINDOW:WINDOW + t, LANES:2 * LANES] = jnp.zeros((t, LANES), BF16)

    zeros64 = jnp.zeros((HEAD_DIM, qb), BF16)
    for j in range(N_Q_HEADS):
        dst = j // GQA
        qaug_ref[HEAD_DIM * dst:HEAD_DIM * (dst + 1), qb * j:qb * (j + 1)] = q_ref[0, HEAD_DIM * j:HEAD_DIM * (j + 1), :]
        qaug_ref[HEAD_DIM * (1 - dst):HEAD_DIM * (2 - dst), qb * j:qb * (j + 1)] = zeros64
    tail_row = lax.broadcasted_iota(jnp.int32, (HEAD_DIM, ncol), 0)
    qaug_ref[one_row:2 * LANES, :] = jnp.where(tail_row == 0, 1.0, 0.0).astype(BF16)

    qaug_ref[LANES:one_row, :] = jnp.zeros((one_row - LANES, ncol), BF16)
    q_blocks = qb // TK_WIN
    npiece = (WINDOW + qb) // TK_WIN
    kws = [kwaug_ref[pl.ds(pl.multiple_of(q0 + w * TK_WIN, TK_WIN), TK_WIN), :] for w in range(npiece)]
    vwt = jnp.concatenate([vwt_ref[0, jnp.maximum(i * q_blocks + w - WINDOW // TK_WIN, 0)] for w in range(npiece)],
                          axis=1)
    wrow = lax.broadcasted_iota(jnp.int32, (TK_WIN, CB), 0)
    wcol = lax.broadcasted_iota(jnp.int32, (TK_WIN, CB), 1) & (qb - 1)
    wbias = []
    for w in range(npiece):
        lo, hi = w * TK_WIN - WINDOW, w * TK_WIN - WINDOW + TK_WIN - 1
        if hi <= 0 and qb - 1 - lo < WINDOW:
            wbias.append(None)
        else:
            dlt = wcol - wrow - lo
            wbias.append(jnp.where((dlt >= 0) & (dlt < WINDOW), 0.0, NEG_INF))
    kw_all = jnp.concatenate(kws, axis=0)
    sws = [jnp.dot(kw_all, qaug_ref[:, CB * cb:CB * (cb + 1)], preferred_element_type=F32) for cb in range(ncb)]
    es, rls = [], []
    for s in sws:
        s = jnp.concatenate([s[TK_WIN * w:TK_WIN * (w + 1)] if b is None else s[TK_WIN * w:TK_WIN * (w + 1)] + b
                             for w, b in enumerate(wbias)], axis=0)
        e = jnp.exp2(s - jnp.max(s, axis=0, keepdims=True))
        es.append(e.astype(BF16))
        rls.append(1.0 / jnp.sum(e, axis=0, keepdims=True))
    ow = jnp.concatenate([jnp.dot(vwt, e, preferred_element_type=F32) * rl for e, rl in zip(es, rls)], axis=1)

    pos_c = q0 + (lax.broadcasted_iota(jnp.int32, (nch, CB), 1) & (qb - 1))
    cvalid = lax.broadcasted_iota(jnp.int32, (nch, CB), 0) * CMP_STRIDE + (CMP_LEN - 1) <= pos_c
    scs = [jnp.dot(ck_ref[0], qaug_ref[0:LANES, CB * cb:CB * (cb + 1)], preferred_element_type=F32)
           for cb in range(ncb)]
    pcs = [_softmax_cols(sc, cvalid).astype(BF16) for sc in scs]
    oc = jnp.concatenate([jnp.dot(cvt_ref[0], pc, preferred_element_type=F32) for pc in pcs], axis=1)
    imp = jnp.concatenate([jnp.dot(ovt_ref[...], pc, preferred_element_type=F32) for pc in pcs], axis=1)
    blk = lax.broadcasted_iota(jnp.int32, (nslc, qb), 0)
    pos_q = q0 + lax.broadcasted_iota(jnp.int32, (nslc, qb), 1)
    for h in range(N_KV_HEADS):
        v = imp[0:nslc, qb * GQA * h:qb * GQA * h + qb]
        for g in range(1, GQA):
            v = v + imp[0:nslc, qb * (GQA * h + g):qb * (GQA * h + g + 1)]
        neg = _select_blocks(v, blk, pos_q, nslc).astype(BF16)
        for g in range(GQA):
            j = GQA * h + g
            qaug_ref[LANES:LANES + nslc, qb * j:qb * (j + 1)] = neg

    brow = lax.broadcasted_iota(jnp.int32, (qb, CB), 0)
    bcol = lax.broadcasted_iota(jnp.int32, (qb, CB), 1) & (qb - 1)
    tri_lo = jnp.where(brow <= bcol, 0.0, NEG_INF)

    acc_ref[...] = jnp.zeros_like(acc_ref)

    def sel_tile(k0, nk, vt, carry, bias):
        m, l = carry
        ka = kaug_ref[pl.ds(k0, nk), :]
        css = [slice(CB * cb, CB * (cb + 1)) for cb in range(ncb)]
        ss = [jnp.dot(ka, qaug_ref[:, cs], preferred_element_type=F32) for cs in css]
        ms, ls, ps, alphas = [], [], [], []
        for cs, s in zip(css, ss):
            if bias is not None:
                s = s + bias
            mn = jnp.maximum(m[:, cs], jnp.max(s, axis=0, keepdims=True))
            alpha = jnp.exp2(m[:, cs] - mn)
            p = jnp.exp2(s - mn)
            ms.append(mn)
            ls.append(alpha * l[:, cs] + jnp.sum(p, axis=0, keepdims=True))
            ps.append(p.astype(BF16))
            alphas.append(alpha)
        pvs = [jnp.dot(vt, p, preferred_element_type=F32) for p in ps]
        for cs, alpha, pv in zip(css, alphas, pvs):
            acc_ref[:, cs] = alpha * acc_ref[:, cs] + pv
        return jnp.concatenate(ms, axis=1), jnp.concatenate(ls, axis=1)

    def vt_blocks(ref, b0, n):
        return jnp.concatenate([ref[0, b0 + j] for j in range(n)], axis=1) if n > 1 else ref[0, b0]

    big_blocks = TK_SLC // TK_WIN

    def big_tile(kt, carry):
        return sel_tile(pl.multiple_of(kt * TK_SLC, TK_SLC), TK_SLC, vt_blocks(vst_ref, kt * big_blocks, big_blocks),
                        carry, None)

    def small_tile(kb, carry):
        return sel_tile(pl.multiple_of(kb * qb, qb), qb, vt_blocks(vst_ref, kb * q_blocks, q_blocks), carry, None)

    carry = (jnp.full((1, ncol), NEG_INF, F32), jnp.zeros((1, ncol), F32))
    nbig = q0 // TK_SLC
    carry = lax.fori_loop(0, nbig, big_tile, carry)
    carry = lax.fori_loop(nbig * (TK_SLC // qb), i, small_tile, carry)
    _, l = sel_tile(pl.multiple_of(q0, qb), qb, vt_blocks(vst_ref, i * q_blocks, q_blocks), carry, tri_lo)
    osel = acc_ref[...] * (1.0 / l)

    gt = gn_ref[0]
    for c in range(N_Q_HEADS // 2):
        rows = []
        for hh in range(2):
            j = 2 * c + hh
            rs = slice(HEAD_DIM * (j // GQA), HEAD_DIM * (j // GQA + 1))
            cs = slice(qb * j, qb * (j + 1))
            rows.append(gt[3 * j:3 * j + 1, :] * oc[rs, cs] + gt[3 * j + 1:3 * j + 2, :] * osel[rs, cs]
                        + gt[3 * j + 2:3 * j + 3, :] * ow[rs, cs])
        o_ref[:, LANES * c:LANES * (c + 1)] = jnp.concatenate(rows, axis=0).T.astype(o_ref.dtype)


def _attn_prompt(q, gn, ck, cvt, ksb, vst, kwb, vwt, nseq, t):
    nb = t // Q_BLOCK
    nch = t // CMP_STRIDE
    nslc = t // SLC_BLOCK
    ovt = _overlap_t(nch, max(nslc, SUBLANES))
    row = lambda n, i: (n * nb + i, 0)
    seq3 = lambda n, i: (n, 0, 0)
    seq4 = lambda n, i: (n, 0, 0, 0)
    col3 = lambda n, i: (n, 0, i)
    return pl.pallas_call(
        _attn_prompt_body,
        grid=(nseq, nb),
        in_specs=[pl.BlockSpec((1, Q_W, Q_BLOCK), col3), pl.BlockSpec((1, gn.shape[1], Q_BLOCK), col3),
                  pl.BlockSpec((1, nch, LANES), seq3), pl.BlockSpec((1, LANES, nch), seq3),
                  pl.BlockSpec((1, t, LANES), seq3), pl.BlockSpec((1, t // TK_WIN, LANES, TK_WIN), seq4),
                  pl.BlockSpec((1, t, LANES), seq3), pl.BlockSpec((1, t // TK_WIN, LANES, TK_WIN), seq4),
                  _full(ovt.shape)],
        out_specs=pl.BlockSpec((Q_BLOCK, Q_W), row),
        out_shape=jax.ShapeDtypeStruct((nseq * t, Q_W), BF16),
        scratch_shapes=[pltpu.VMEM((t, 2 * LANES), BF16),
                        pltpu.VMEM((WINDOW + t, 2 * LANES), BF16),
                        pltpu.VMEM((2 * LANES, N_Q_HEADS * Q_BLOCK), BF16),
                        pltpu.VMEM((LANES, N_Q_HEADS * Q_BLOCK), F32)],
        compiler_params=_cparams(("arbitrary", "arbitrary"), 56),
        name="attn_prompt",
    )(q, gn, ck, cvt, ksb.reshape(nseq, t, LANES), vst, kwb.reshape(nseq, t, LANES), vwt, ovt)


def _post_body(x_ref, abr_ref, on_ref, ga_ref, gb_ref, wno_ref, wo_ref, g2_ref, x1_ref, h2_ref):
    bbr = jnp.dot(on_ref[...].astype(BF16), wno_ref[...], preferred_element_type=F32)
    merged = ga_ref[...].astype(F32) * abr_ref[...].astype(F32) + gb_ref[...].astype(F32) * bbr
    x1 = x_ref[...] + jnp.dot(merged.astype(BF16), wo_ref[...], preferred_element_type=F32)
    x1_ref[...] = x1
    inv = lax.rsqrt(jnp.mean(x1 * x1, axis=-1, keepdims=True) + RMS_EPS)
    h2_ref[...] = (x1 * inv * g2_ref[...]).astype(BF16)


def _post(x2d, abr, abr_lay, onsa, ga, gb, wno, wo, g2, lay, out_lay):
    x_shape, x_spec = lay["a"](D_MODEL)
    abr_shape, abr_spec = lay[abr_lay](D_MODEL)
    on_shape, on_spec = lay["a"](Q_W)
    o_shape, o_spec = lay[out_lay](D_MODEL)
    return pl.pallas_call(
        _post_body,
        grid=lay["grid"],
        in_specs=[x_spec, abr_spec, on_spec, x_spec, x_spec, _full(wno.shape), _full(wo.shape), _full(g2.shape)],
        out_specs=[o_spec, o_spec],
        out_shape=[jax.ShapeDtypeStruct(o_shape, F32), jax.ShapeDtypeStruct(o_shape, BF16)],
        compiler_params=_cparams(("arbitrary",) * len(lay["grid"]), 48),
        name="post",
    )(x2d.reshape(x_shape), abr.reshape(abr_shape), onsa.reshape(on_shape), ga, gb, wno, wo, g2)


def _route(logits):
    lane = lax.broadcasted_iota(jnp.int32, logits.shape, 1).astype(F32)
    big = float(LANES)
    glog = jnp.where(lane < N_EXPERT_GROUPS, logits, -jnp.inf)
    gmax = jnp.max(glog, axis=1, keepdims=True)
    gsel = jnp.min(jnp.where(glog == gmax, lane, big), axis=1, keepdims=True)
    gw = 1.0 / jnp.sum(jnp.exp(glog - gmax), axis=1, keepdims=True)
    lo = N_EXPERT_GROUPS + EXPERTS_PER_GROUP * gsel
    el = jnp.where((lane >= lo) & (lane < lo + EXPERTS_PER_GROUP), logits, -jnp.inf)
    v1 = jnp.max(el, axis=1, keepdims=True)
    i1 = jnp.min(jnp.where(el == v1, lane, big), axis=1, keepdims=True)
    el2 = jnp.where(lane == i1, -jnp.inf, el)
    v2 = jnp.max(el2, axis=1, keepdims=True)
    i2 = jnp.min(jnp.where(el2 == v2, lane, big), axis=1, keepdims=True)
    e2 = jnp.exp(v2 - v1)
    w1 = gw / (1.0 + e2)
    return jnp.where(lane == i1, w1, 0.0) + jnp.where(lane == i2, w1 * e2, 0.0)


def _moe_body(x1_ref, h2_ref, p_ref, wr_ref, br_ref, wg_ref, wu_ref, wd_ref, wpg_ref, wp_ref, gf_ref,
              y_ref, acc_ref, comb_ref, *, tsplit):
    g = pl.program_id(1)
    h2 = h2_ref[...]

    @pl.when(g == 0)
    def _():
        logits = jnp.dot(h2, wr_ref[...], preferred_element_type=F32) + br_ref[...]
        comb_ref[...] = _route(logits)
        acc_ref[...] = jnp.zeros_like(acc_ref)

    comb = comb_ref[...]
    lane = lax.broadcasted_iota(jnp.int32, comb.shape, 1)
    acc = acc_ref[...]
    for k in range(EXPERTS_PER_GROUP):
        e_lane = N_EXPERT_GROUPS + EXPERTS_PER_GROUP * g + k
        ce = jnp.sum(jnp.where(lane == e_lane, comb, 0.0), axis=1, keepdims=True)
        a = jnp.dot(h2, wg_ref[k], preferred_element_type=F32)
        b = jnp.dot(h2, wu_ref[k], preferred_element_type=F32)
        act = (jax.nn.silu(a) * b * ce).astype(BF16)
        acc = acc + jnp.dot(act, wd_ref[k], preferred_element_type=F32)
    acc_ref[...] = acc

    @pl.when(g == N_EXPERT_GROUPS - 1)
    def _():
        x2 = x1_ref[...] + acc_ref[...]
        rows = x2.shape[0] // tsplit
        if tsplit == 1:
            p = p_ref[...]
        else:
            p = jnp.concatenate([p_ref[:, PLE_DIM * t:PLE_DIM * (t + 1)] for t in range(tsplit)], axis=0)
        gate = jax.nn.sigmoid(jnp.dot(x2.astype(BF16), wpg_ref[...], preferred_element_type=F32))
        x3 = x2 + gate * jnp.dot(p.astype(BF16), wp_ref[...], preferred_element_type=F32)
        inv = lax.rsqrt(jnp.mean(x3 * x3, axis=-1, keepdims=True) + RMS_EPS)
        y = x3 * inv * gf_ref[...]
        if tsplit == 1:
            y_ref[...] = y
        else:
            for t in range(tsplit):
                y_ref[:, D_MODEL * t:D_MODEL * (t + 1)] = y[rows * t:rows * (t + 1)]


def _moe(x1, h2, p, mp, tm, tsplit):
    rows = x1.shape[0]
    nrb = rows // tm
    rb = lambda r, g: (r, 0)
    grp = lambda r, g: (g, 0, 0)
    if tsplit == 1:
        p_spec = pl.BlockSpec((tm, PLE_DIM), rb)
        y_spec = pl.BlockSpec((tm, D_MODEL), rb)
        y_shape = (rows, D_MODEL)
    else:
        assert nrb == 1
        p_spec = _full(p.shape)
        y_shape = (rows // tsplit, tsplit * D_MODEL)
        y_spec = _full(y_shape)
    return pl.pallas_call(
        functools.partial(_moe_body, tsplit=tsplit),
        grid=(nrb, N_EXPERT_GROUPS),
        in_specs=[pl.BlockSpec((tm, D_MODEL), rb), pl.BlockSpec((tm, D_MODEL), rb), p_spec,
                  _full(mp["wr"].shape), _full(mp["br"].shape),
                  pl.BlockSpec((EXPERTS_PER_GROUP, D_MODEL, D_FF_EXPERT), grp),
                  pl.BlockSpec((EXPERTS_PER_GROUP, D_MODEL, D_FF_EXPERT), grp),
                  pl.BlockSpec((EXPERTS_PER_GROUP, D_FF_EXPERT, D_MODEL), grp),
                  _full(mp["wpg"].shape, True), _full(mp["wp"].shape, True), _full(mp["gf"].shape)],
        out_specs=y_spec,
        out_shape=jax.ShapeDtypeStruct(y_shape, F32),
        scratch_shapes=[pltpu.VMEM((tm, D_MODEL), F32), pltpu.VMEM((tm, LANES), F32)],
        compiler_params=_cparams(("arbitrary", "arbitrary"), 60),
        name="moe_ple",
    )(x1, h2, p, mp["wr"], mp["br"], mp["wg"], mp["wu"], mp["wd"], mp["wpg"], mp["wp"], mp["gf"])


def _s5_sample_body(u_ref, h0re_ref, h0im_ref, wre_ref, wim_ref, ar_ref, ai_ref, cre_ref, cim_ref, d_ref,
                    wglu_ref, bglu_ref, wso_ref, abr_ref, hre_out_ref, him_out_ref,
                    bure_ref, buim_ref, hre_ref, him_ref, *, nseq, nstep):
    u = u_ref[...]
    ub = u.astype(BF16)
    for j in range(4):
        lhs = ub[:, LANES * j:LANES * (j + 1)]
        bure_ref[:, 512 * j:512 * (j + 1)] = jnp.dot(lhs, wre_ref[j], preferred_element_type=F32)
        buim_ref[:, 512 * j:512 * (j + 1)] = jnp.dot(lhs, wim_ref[j], preferred_element_type=F32)
    for lc in range(4):
        sl = slice(512 * lc, 512 * (lc + 1))
        ar = jnp.broadcast_to(ar_ref[:, sl], (SUBLANES, 512))
        ai = jnp.broadcast_to(ai_ref[:, sl], (SUBLANES, 512))

        def body(rc, carry, sl=sl, ar=ar, ai=ai):
            r0 = pl.multiple_of(rc * SUBLANES, SUBLANES)
            hr = h0re_ref[pl.ds(r0, SUBLANES), sl]
            hi = h0im_ref[pl.ds(r0, SUBLANES), sl]
            for t in range(nstep):
                rr = pl.multiple_of(t * nseq + rc * SUBLANES, SUBLANES)
                hr, hi = (ar * hr - ai * hi + bure_ref[pl.ds(rr, SUBLANES), sl],
                          ar * hi + ai * hr + buim_ref[pl.ds(rr, SUBLANES), sl])
                hre_ref[pl.ds(rr, SUBLANES), sl] = hr
                him_ref[pl.ds(rr, SUBLANES), sl] = hi
            hre_out_ref[pl.ds(r0, SUBLANES), sl] = hr
            him_out_ref[pl.ds(r0, SUBLANES), sl] = hi
            return carry

        lax.fori_loop(0, nseq // SUBLANES, body, 0)
    parts = []
    for j in range(4):
        sl = slice(512 * j, 512 * (j + 1))
        parts.append(jnp.dot(hre_ref[:, sl].astype(BF16), cre_ref[j], preferred_element_type=F32)
                     - jnp.dot(him_ref[:, sl].astype(BF16), cim_ref[j], preferred_element_type=F32))
    y = jnp.concatenate(parts, axis=1) + d_ref[...] * u
    zg = jax.nn.gelu(y)
    gate = jnp.dot(zg.astype(BF16), wglu_ref[...], preferred_element_type=F32) + bglu_ref[...]
    glu = (zg * jax.nn.sigmoid(gate)).astype(BF16)
    abr_ref[...] = jnp.dot(glu, wso_ref[...], preferred_element_type=F32).astype(BF16)


def _s5_sample(u_ts, h0re, h0im, sp, wglu, bglu, wso, nseq, nstep):
    rows = nseq * nstep
    ops = [u_ts, h0re, h0im, sp["wre"], sp["wim"], sp["ar"], sp["ai"], sp["cre"], sp["cim"], sp["d"], wglu, bglu, wso]
    return pl.pallas_call(
        functools.partial(_s5_sample_body, nseq=nseq, nstep=nstep),
        grid=(1,),
        in_specs=[_full(o.shape) for o in ops],
        out_specs=[_full((rows, D_MODEL)), _full((nseq, N_STATE)), _full((nseq, N_STATE))],
        out_shape=[jax.ShapeDtypeStruct((rows, D_MODEL), BF16),
                   jax.ShapeDtypeStruct((nseq, N_STATE), F32), jax.ShapeDtypeStruct((nseq, N_STATE), F32)],
        scratch_shapes=[pltpu.VMEM((rows, N_STATE), F32) for _ in range(4)],
        compiler_params=_cparams(("arbitrary",), 56),
        name="s5_sample",
    )(*ops)


def _softmax_rows(s, valid):
    sm = jnp.where(valid, s, NEG_INF)
    mx = jnp.max(sm, axis=1, keepdims=True)
    e = jnp.where(valid, jnp.exp2(sm - mx), 0.0)
    l = jnp.sum(e, axis=1, keepdims=True)
    return e * (1.0 / jnp.maximum(l, 1e-30))


SAMPLE_SEQS_PER_STEP = 4
CMP_PITCH = 24


def _attn_sample_body(pt_ref, q_ref, gn_ref, nks_ref, nkw_ref, wint_ref, wk_ref, bk_ref, w2k_ref, ov_ref, e_ref, *rest,
                      npage, past_len, nsub, tq):
    o_ref, nwint_ref, xrow_ref = rest[2 * nsub * npage:]
    nrow = N_Q_HEADS * tq
    nwin = wint_ref.shape[2]
    nslc = -(-(past_len + tq) // SLC_BLOCK)
    nch = past_len // CMP_STRIDE
    per_page = PAGE_SIZE // CMP_STRIDE

    cmp = []
    for kv in range(2):
        for s in range(nsub):
            for p in range(npage):
                rows = rest[s * npage + p][0, LANES * kv:LANES * (kv + 1), :].T
                for c in range(per_page):
                    r0 = CMP_PITCH * (per_page * p + c)
                    xrow_ref[s, r0:r0 + CMP_STRIDE, :] = rows[CMP_STRIDE * c:CMP_STRIDE * (c + 1)]
        x = jnp.concatenate(
            [jnp.concatenate([xrow_ref[s, pl.ds(i, nch, stride=CMP_PITCH), :] for i in range(CMP_STRIDE)], axis=1)
             for s in range(nsub)], axis=0)
        pp = jnp.dot(x.astype(BF16), wk_ref[kv], preferred_element_type=F32)
        pre = pp[:, 0:LANES] + pltpu.roll(pp[:, LANES:2 * LANES], nsub * nch - 1, axis=0) + bk_ref[kv:kv + 1, :]
        cmp.append(jnp.dot(jax.nn.gelu(pre).astype(BF16), w2k_ref[kv], preferred_element_type=F32).astype(BF16))
    cks = [cmp[0][nch * s:nch * (s + 1)] for s in range(nsub)]
    cvs = [cmp[1][nch * s:nch * (s + 1)] for s in range(nsub)]
    slc_pages = [rest[(nsub + s) * npage:(nsub + s + 1) * npage] for s in range(nsub)]
    seqs = range(nsub)
    rcat = lambda parts: jnp.concatenate(parts, axis=0)

    lane_w = lax.broadcasted_iota(jnp.int32, (KV_W, LANES), 1)
    lane8 = lax.broadcasted_iota(jnp.int32, (tq, LANES), 1)
    nks_l, nkw_l, wint_l, qs_l = [], [], [], []
    for s in seqs:
        rows_s = slice(tq * s, tq * (s + 1))
        nks_l.append(jnp.concatenate([nks_ref[rows_s, :], jnp.zeros((LANES - tq, KV_W), F32)], axis=0))
        nkw = jnp.concatenate([nkw_ref[rows_s, :], jnp.zeros((LANES - tq, KV_W), F32)], axis=0)
        nkw_l.append(nkw)
        wint = wint_ref[s]
        wint_l.append(wint)
        shifted = pltpu.roll(wint, nwin - tq, axis=1)
        new_t = pltpu.roll(nkw.T, LANES - tq, axis=1)
        nwint_ref[s, :, 0:nwin - LANES] = shifted[:, 0:nwin - LANES]
        nwint_ref[s, :, nwin - LANES:nwin] = jnp.where(lane_w >= LANES - tq, new_t, shifted[:, nwin - LANES:nwin])
        q = q_ref[rows_s, :]
        qrows = []
        for j in range(N_Q_HEADS):
            chunk = q[:, LANES * (j // 2):LANES * (j // 2 + 1)]
            dst = j // GQA
            if (j % 2) != dst:
                chunk = pltpu.roll(chunk, HEAD_DIM, axis=1)
            keep = (lane8 < HEAD_DIM) if dst == 0 else (lane8 >= HEAD_DIM)
            qrows.append(jnp.where(keep, chunk, 0.0))
        qs_l.append(jnp.concatenate(qrows, axis=0).astype(BF16))

    rtot = nsub * nrow
    seq_rows = [slice(nrow * s, nrow * (s + 1)) for s in seqs]
    pos = past_len + (lax.broadcasted_iota(jnp.int32, (rtot, LANES), 0) & (tq - 1))
    lane = lax.broadcasted_iota(jnp.int32, (rtot, LANES), 1)

    sc = rcat([_dot_t(qs_l[s], cks[s]) for s in seqs])
    pc = _softmax_rows(sc, lane * CMP_STRIDE + (CMP_LEN - 1) <= pos).astype(BF16)
    oc = rcat([jnp.dot(pc[seq_rows[s]], cvs[s], preferred_element_type=F32) for s in seqs])
    imp = jnp.dot(pc, ov_ref[...], preferred_element_type=F32)
    vs = []
    for s in seqs:
        for h in range(N_KV_HEADS):
            r0 = nrow * s + tq * GQA * h
            v = imp[r0:r0 + tq]
            for g in range(1, GQA):
                v = v + imp[r0 + tq * g:r0 + tq * (g + 1)]
            vs.append(v)
    nsel = len(vs) * tq
    vt = rcat(vs + [jnp.zeros((LANES - nsel, LANES), F32)]).T
    nblk_pad = -(-nslc // SUBLANES) * SUBLANES
    blk_t = lax.broadcasted_iota(jnp.int32, (nblk_pad, LANES), 0)
    pos_t = past_len + (lax.broadcasted_iota(jnp.int32, (nblk_pad, LANES), 1) & (tq - 1))
    neg_t = _select_blocks(vt[0:nblk_pad], blk_t, pos_t, nslc, axis=0)
    neg = rcat([neg_t, jnp.zeros((LANES - nblk_pad, LANES), F32)]).T
    negsel = rcat([neg[tq * (N_KV_HEADS * s + j // GQA):tq * (N_KV_HEADS * s + j // GQA + 1)]
                   for s in seqs for j in range(N_Q_HEADS)])
    negsel_b = negsel.astype(BF16)

    new_blk = past_len // SLC_BLOCK
    ss_l = []
    for s in seqs:
        qaug = jnp.concatenate([qs_l[s], negsel_b[seq_rows[s]]], axis=1)
        parts = []
        for p in range(0, npage, 2):
            kt = jnp.concatenate([slc_pages[s][p][0][0:LANES], slc_pages[s][p + 1][0][0:LANES]], axis=1).astype(BF16)
            et = jnp.concatenate([e_ref[p], e_ref[p + 1]], axis=1)
            parts.append(jnp.dot(qaug, jnp.concatenate([kt, et], axis=0), preferred_element_type=F32))
        parts.append(_dot_t(qs_l[s], nks_l[s][:, 0:LANES].astype(BF16)) + negsel[seq_rows[s], new_blk:new_blk + 1])
        ss_l.append(jnp.concatenate(parts, axis=1))
    ss = rcat(ss_l)
    nkeys = ss.shape[1]
    kpos = lax.broadcasted_iota(jnp.int32, (rtot, nkeys), 1)
    pos_k = past_len + (lax.broadcasted_iota(jnp.int32, (rtot, nkeys), 0) & (tq - 1))
    ps = _softmax_rows(ss, kpos <= pos_k).astype(BF16)
    osel_l = []
    for s in seqs:
        psq = ps[seq_rows[s]]
        o = jnp.dot(psq[:, past_len:nkeys], nks_l[s][:, LANES:2 * LANES].astype(BF16), preferred_element_type=F32)
        for p in range(0, npage, 2):
            vtp = jnp.concatenate([slc_pages[s][p][0][LANES:2 * LANES], slc_pages[s][p + 1][0][LANES:2 * LANES]],
                                  axis=1).astype(BF16)
            o = o + _dot_t(psq[:, PAGE_SIZE * p:PAGE_SIZE * (p + 2)], vtp)
        osel_l.append(o)
    osel = rcat(osel_l)

    sw = rcat([jnp.concatenate([jnp.dot(qs_l[s], wint_l[s][0:LANES].astype(BF16), preferred_element_type=F32),
                                _dot_t(qs_l[s], nkw_l[s][:, 0:LANES].astype(BF16))], axis=1) for s in seqs])
    nw = sw.shape[1]
    widx = lax.broadcasted_iota(jnp.int32, (rtot, nw), 1)
    pos_w = past_len + (lax.broadcasted_iota(jnp.int32, (rtot, nw), 0) & (tq - 1))
    dlt = pos_w - (past_len - nwin + widx)
    pw = _softmax_rows(sw, (dlt >= 0) & (dlt < WINDOW) & (widx < nwin + tq)).astype(BF16)
    ow = rcat([_dot_t(pw[seq_rows[s], 0:nwin], wint_l[s][LANES:2 * LANES].astype(BF16))
               + jnp.dot(pw[seq_rows[s], nwin:nw], nkw_l[s][:, LANES:2 * LANES].astype(BF16),
                         preferred_element_type=F32) for s in seqs])

    for s in seqs:
        rows_s = slice(tq * s, tq * (s + 1))
        gn = gn_ref[rows_s, :]
        for c in range(N_Q_HEADS // 2):
            halves = []
            for hh in range(2):
                j = 2 * c + hh
                rs = slice(nrow * s + tq * j, nrow * s + tq * (j + 1))
                oj = (gn[:, 3 * j:3 * j + 1] * oc[rs] + gn[:, 3 * j + 1:3 * j + 2] * osel[rs]
                      + gn[:, 3 * j + 2:3 * j + 3] * ow[rs])
                if (j // GQA) != hh:
                    oj = pltpu.roll(oj, HEAD_DIM, axis=1)
                halves.append(oj)
            o_ref[rows_s, LANES * c:LANES * (c + 1)] = jnp.where(lane8 < HEAD_DIM, halves[0], halves[1])


def _attn_sample(q, gn, nks, nkw, cache_cmp, cache_slc, cache_win, page_table, cp, nseq, tq, past_len):
    assert tq <= CMP_STRIDE and past_len % PAGE_SIZE == 0
    npage = past_len // PAGE_SIZE
    assert npage % 2 == 0 and PAGE_SIZE == LANES
    n_pool = cache_cmp.shape[0]
    nwin = cache_win.shape[1]
    chunks = past_len // CMP_STRIDE
    ov = _overlap_t(chunks, LANES).T
    key = np.arange(past_len).reshape(npage, 1, PAGE_SIZE)
    e = jnp.asarray(np.arange(LANES).reshape(1, LANES, 1) == key // SLC_BLOCK, dtype=BF16)
    to_t = lambda c: jnp.transpose(c, (0, 2, 3, 4, 1)).reshape(c.shape[0], KV_W, c.shape[1])
    cmp_t, slc_t, win_t = to_t(cache_cmp), to_t(cache_slc), to_t(cache_win)
    nsub = SAMPLE_SEQS_PER_STEP
    assert nseq % nsub == 0
    row = lambda n, pt: (n, 0)
    seq3 = lambda n, pt: (n, 0, 0)
    page = lambda s, p: (lambda n, pt: (pt[n * nsub + s, p], 0, 0))
    consts = [cp["wk"], cp["bk"], cp["w2k"], ov, e]
    in_specs = [pl.BlockSpec((nsub * tq, Q_W), row), pl.BlockSpec((nsub * tq, LANES), row),
                pl.BlockSpec((nsub * tq, KV_W), row), pl.BlockSpec((nsub * tq, KV_W), row),
                pl.BlockSpec((nsub, KV_W, nwin), seq3)]
    in_specs += [pl.BlockSpec(c.shape, (lambda nd: lambda n, pt: (0,) * nd)(c.ndim)) for c in consts]
    pages = [pl.BlockSpec((1, KV_W, PAGE_SIZE), page(s, p)) for s in range(nsub) for p in range(npage)]
    in_specs += pages * 2
    grid_spec = pltpu.PrefetchScalarGridSpec(
        num_scalar_prefetch=1,
        grid=(nseq // nsub,),
        in_specs=in_specs,
        out_specs=[pl.BlockSpec((nsub * tq, Q_W), row), pl.BlockSpec((nsub, KV_W, nwin), seq3)],
        scratch_shapes=[pltpu.VMEM((nsub, chunks * CMP_PITCH, LANES), F32)],
    )
    return pl.pallas_call(
        functools.partial(_attn_sample_body, npage=npage, past_len=past_len, nsub=nsub, tq=tq),
        grid_spec=grid_spec,
        out_shape=[jax.ShapeDtypeStruct((nseq * tq, Q_W), F32), jax.ShapeDtypeStruct((nseq, KV_W, nwin), F32)],
        compiler_params=_cparams(("arbitrary",), 56),
        name="attn_sample",
    )(page_table, q, gn, nks, nkw, win_t, *consts, *([cmp_t] * (nsub * npage)), *([slc_t] * (nsub * npage)))


def _moe_params(w_rg, b_rg, w_re, b_re, w_gate, w_up, w_down, w_ple, w_ple_gate, gf):
    pad = LANES - N_EXPERT_GROUPS - N_EXPERTS
    return {"wr": jnp.pad(jnp.concatenate([w_rg, w_re], axis=1), ((0, 0), (0, pad))).astype(BF16),
            "br": jnp.pad(jnp.concatenate([b_rg, b_re]), (0, pad)).astype(F32).reshape(1, LANES),
            "wg": w_gate.astype(BF16), "wu": w_up.astype(BF16), "wd": w_down.astype(BF16),
            "wpg": w_ple_gate.astype(BF16), "wp": w_ple.astype(BF16), "gf": gf.astype(F32).reshape(1, D_MODEL)}


TM_PROMPT = 512
TM_MOE = 1024
TC_S5 = 128


def kernel(x_prompt, x_sample, p_prompt, p_sample, cache_cmp_kv, cache_slc_kv, cache_win_kv, state_ssm, page_table, norm1_g, w_in, ssm_lam_re, ssm_lam_im, ssm_log_dt, ssm_b_re, ssm_b_im, ssm_c_re, ssm_c_im, ssm_d, w_glu, b_glu, cmp_pe, cmp_w1, cmp_w2, w_ssm_out, w_nsa_out, w_o, norm2_g, w_route_group, b_route_group, w_route_expert, b_route_expert, w_exp_gate, w_exp_up, w_exp_down, w_ple, w_ple_gate, final_norm_g):
    assert w_in.shape[0] == 1, "one layer"
    l = 0
    nb, t = x_prompt.shape[:2]
    ns, ts = x_sample.shape[:2]
    past_len = page_table.shape[1] * PAGE_SIZE
    kvt = (2, N_KV_HEADS, HEAD_DIM)

    wi = _inproj_params(w_in[l])
    g1 = norm1_g[l].astype(F32).reshape(1, D_MODEL)
    g2 = norm2_g[l].astype(F32).reshape(1, D_MODEL)
    sp = _s5_params(ssm_lam_re[l], ssm_lam_im[l], ssm_log_dt[l], ssm_b_re[l], ssm_b_im[l], ssm_c_re[l], ssm_c_im[l],
                    ssm_d[l])
    cp = _cmp_params(cmp_pe[l], cmp_w1[l], cmp_w2[l])
    mp = _moe_params(w_route_group[l], b_route_group[l], w_route_expert[l], b_route_expert[l], w_exp_gate[l],
                     w_exp_up[l], w_exp_down[l], w_ple[l], w_ple_gate[l], final_norm_g)
    wglu = w_glu[l].astype(BF16)
    bglu = b_glu[l].astype(F32).reshape(1, SSM_WIDTH)
    wso = w_ssm_out[l].astype(BF16)
    wno = w_nsa_out[l].astype(BF16)
    wo = w_o[l].astype(BF16)

    lay = _prompt_layout(nb, t, TM_PROMPT)
    xp = x_prompt.reshape(nb * t, D_MODEL)
    r = _inproj_prompt(xp, lay, g1, wi)
    abr, hlast = _s5_prompt(r["u"], sp, wglu, bglu, wso, t, TC_S5)
    ck, cvt = _compress_prompt(r["kvc"], cp, nb, t)
    onsa = _attn_prompt(r["qt"], r["gnt"], ck, cvt, r["ksb"], r["vst"], r["kwb"], r["vwt"], nb, t)
    x1, h2 = _post(xp, abr, "a", onsa, r["ga"], r["gb"], wno, wo, g2, lay, "a")
    y_prompt = _moe(x1, h2, p_prompt[l].reshape(nb * t, PLE_DIM), mp, TM_MOE, 1).reshape(nb, t, D_MODEL)
    keep = min(WINDOW, t)

    def rows_last(a):
        return jnp.transpose(a.reshape((a.shape[0],) + kvt + (a.shape[2],)), (0, 4, 1, 2, 3))[None]

    new_cmp_p = rows_last(r["kvct"])
    new_slc_p = rows_last(r["kvst"])
    new_win_p = rows_last(r["kvwt"][:, :, t - keep:])
    new_ssm_p = jnp.stack([hlast[0:nb], hlast[nb:2 * nb]], axis=-1).reshape(1, nb, N_SSM_GROUPS, SSM_STATE, 2)

    lays = _sample_layout(ns, ts)
    xs = x_sample.reshape(ns * ts, D_MODEL)
    rs = _inproj_sample(xs, lays, g1, wi)
    h0 = state_ssm[l].astype(F32).reshape(ns, N_STATE, 2)
    abr_s, hre, him = _s5_sample(rs["u"], h0[..., 0], h0[..., 1], sp, wglu, bglu, wso, ns, ts)
    onsa_s, new_win = _attn_sample(rs["q"].reshape(ns * ts, Q_W), rs["gn"].reshape(ns * ts, LANES),
                                   rs["kvs"].reshape(ns * ts, KV_W), rs["kvw"].reshape(ns * ts, KV_W),
                                   cache_cmp_kv[l], cache_slc_kv[l], cache_win_kv[l], page_table, cp, ns, ts, past_len)
    x1s, h2s = _post(xs, abr_s, "b", onsa_s, rs["ga"], rs["gb"], wno, wo, g2, lays, "b")
    y_sample = _moe(x1s, h2s, p_sample[l].reshape(ns, ts * PLE_DIM), mp, ns * ts, ts).reshape(ns, ts, D_MODEL)
    steps_first = lambda a: jnp.transpose(a.reshape((ts,) + kvt + (ns,)), (4, 0, 1, 2, 3))[None]
    new_cmp_s = steps_first(rs["kvct"])
    new_slc_s = steps_first(rs["kvst"])
    new_win_s = rows_last(new_win)
    new_ssm_s = jnp.stack([hre, him], axis=-1).reshape(1, ns, N_SSM_GROUPS, SSM_STATE, 2)
    return (y_prompt, y_sample, new_cmp_p, new_slc_p, new_win_p, new_ssm_p,
            new_cmp_s, new_slc_s, new_win_s, new_ssm_s)
```

```python
import functools
import math

import jax
import jax.numpy as jnp
import numpy as np
from jax import lax
from jax.experimental import pallas as pl
from jax.experimental.pallas import tpu as pltpu

F32 = jnp.float32
BF16 = jnp.bfloat16

D_MODEL = 1024
SSM_WIDTH = 512
SSM_GROUP = 16
N_SSM_GROUPS = 32
SSM_STATE = 64
HEAD_DIM = 64
N_Q_HEADS = 8
N_KV_HEADS = 2
GQA = 4
CMP_LEN = 32
CMP_STRIDE = 16
SLC_BLOCK = 64
TOP_N = 8
WINDOW = 512
Q_BLOCK = 256
NEG_INF = -1e30
FORCE_BONUS = 1e4
Q_W = 512
KV_W = 256
NSA_GATE_W = 24
N_EXPERT_GROUPS = 4
EXPERTS_PER_GROUP = 4
N_EXPERTS = 16
D_FF_EXPERT = 256
PLE_DIM = 256
RMS_EPS = 1e-6
PAGE_SIZE = 128

LANES = 128
SUBLANES = 8
N_STATE = N_SSM_GROUPS * SSM_STATE
MIB = 2 ** 20


def _cparams(sem, vmem_mib):
    return pltpu.CompilerParams(dimension_semantics=sem, vmem_limit_bytes=vmem_mib * MIB)


def _full(shape, single_buffer=False):
    nd = len(shape)
    if single_buffer:
        return pl.BlockSpec(shape, lambda *_: (0,) * nd, pipeline_mode=pl.Buffered(1))
    return pl.BlockSpec(shape, lambda *_: (0,) * nd)


def _prompt_layout(nseq, t, tm):
    nb = t // tm
    return {
        "grid": (nb, nseq), "tm": tm, "nseq": nseq, "t": t,
        "a": lambda w: ((nseq * t, w), pl.BlockSpec((tm, w), lambda b, s: (s * nb + b, 0))),
    }


def _sample_layout(nseq, t):
    return {
        "grid": (1, t), "tm": nseq,
        "a": lambda w: ((nseq, t * w), pl.BlockSpec((nseq, w), lambda s, b: (0, b))),
        "b": lambda w: ((t * nseq, w), pl.BlockSpec((nseq, w), lambda s, b: (b, 0))),
    }


TK_SLC = 512
TK_WIN = 128


Q_SCALE = HEAD_DIM ** -0.5 * math.log2(math.e)
C_U, C_Q, C_KVC, C_KVS, C_KVW = 0, 512, 1024, 1280, 1536
N_MAIN = 1792


def _dot_t(a, b):
    return lax.dot_general(a, b, (((1,), (1,)), ((), ())), preferred_element_type=F32)


GN_ROWS = 32


def _inproj_prompt_body(x_ref, g_ref, wa_ref, wgn_ref, wgab_ref,
                        u_ref, kvc_ref, ksb_ref, kwb_ref, ga_ref, gb_ref,
                        qt_ref, kvct_ref, kvst_ref, kvwt_ref, gnt_ref, vst_ref, vwt_ref, *, nseq):
    s = pl.program_id(1)
    x = x_ref[...]
    inv = lax.rsqrt(jnp.mean(x * x, axis=-1, keepdims=True) + RMS_EPS)
    h = (x * inv * g_ref[...]).astype(BF16)
    tm = h.shape[0]

    def mm(w):
        return jnp.dot(h, w, preferred_element_type=F32)

    u = mm(wa_ref[:, C_U:C_U + SSM_WIDTH])
    for j in range(SSM_WIDTH // LANES):
        u_ref[j, pl.ds(s, tm, stride=nseq), :] = u[:, LANES * j:LANES * (j + 1)]
    ga_ref[...] = jax.nn.sigmoid(mm(wgab_ref[:, 0:D_MODEL])).astype(BF16)
    gb_ref[...] = jax.nn.sigmoid(mm(wgab_ref[:, D_MODEL:2 * D_MODEL])).astype(BF16)
    kvc = mm(wa_ref[:, C_KVC:C_KVC + KV_W])
    kvc_ref[0] = kvc[:, 0:LANES]
    kvc_ref[1] = kvc[:, LANES:2 * LANES]
    kvct_ref[0] = kvc.T
    kvs = mm(wa_ref[:, C_KVS:C_KVS + KV_W])
    ksb_ref[...] = kvs[:, 0:LANES].astype(BF16)
    kvst = kvs.T
    kvst_ref[0] = kvst
    kvw = mm(wa_ref[:, C_KVW:C_KVW + KV_W])
    kwb_ref[...] = kvw[:, 0:LANES].astype(BF16)
    kvwt = kvw.T
    kvwt_ref[0] = kvwt
    for c in range(tm // TK_WIN):
        vst_ref[0, c] = kvst[LANES:2 * LANES, c * TK_WIN:(c + 1) * TK_WIN].astype(BF16)
        vwt_ref[0, c] = kvwt[LANES:2 * LANES, c * TK_WIN:(c + 1) * TK_WIN].astype(BF16)
    qt_ref[0] = (mm(wa_ref[:, C_Q:C_Q + Q_W]) * Q_SCALE).T.astype(BF16)
    gnt_ref[0] = jax.nn.sigmoid(mm(wgn_ref[...])).T[0:GN_ROWS]


def _inproj_prompt(x2d, lay, g, w):
    tm, nseq, t = lay["tm"], lay["nseq"], lay["t"]
    nb = t // tm
    out_shapes, out_specs, names = [], [], []

    def add(name, shape_spec, dt):
        names.append(name)
        out_shapes.append(jax.ShapeDtypeStruct(shape_spec[0], dt))
        out_specs.append(shape_spec[1])

    def tr(rows):
        return (nseq, rows, t), pl.BlockSpec((1, rows, tm), lambda b, s: (s, 0, b))

    nu = SSM_WIDTH // LANES
    add("u", ((nu, t * nseq, LANES), pl.BlockSpec((nu, tm * nseq, LANES), lambda b, s: (0, b, 0))), F32)
    add("kvc", ((2, nseq * t, LANES), pl.BlockSpec((2, tm, LANES), lambda b, s: (0, s * nb + b, 0))), F32)
    add("ksb", lay["a"](LANES), BF16)
    add("kwb", lay["a"](LANES), BF16)
    add("ga", lay["a"](D_MODEL), BF16)
    add("gb", lay["a"](D_MODEL), BF16)
    add("qt", tr(Q_W), BF16)
    add("kvct", tr(KV_W), F32)
    add("kvst", tr(KV_W), F32)
    add("kvwt", tr(KV_W), F32)
    add("gnt", tr(GN_ROWS), F32)
    for name in ("vst", "vwt"):
        add(name, ((nseq, t // TK_WIN, LANES, TK_WIN),
                   pl.BlockSpec((1, tm // TK_WIN, LANES, TK_WIN), lambda b, s: (s, b, 0, 0))), BF16)
    x_shape, x_spec = lay["a"](D_MODEL)
    ops = [g, w["wa"], w["wgn"], w["wgab"]]
    outs = pl.pallas_call(
        functools.partial(_inproj_prompt_body, nseq=nseq),
        grid=lay["grid"],
        in_specs=[x_spec] + [_full(o.shape) for o in ops],
        out_specs=out_specs,
        out_shape=out_shapes,
        compiler_params=_cparams(("arbitrary",) * 2, 56),
        name="inproj_prompt",
    )(x2d.reshape(x_shape), *ops)
    return dict(zip(names, outs))


def _inproj_sample_body(x_ref, g_ref, wa_ref, wgn_ref, wgab_ref,
                        u_ref, q_ref, kvs_ref, kvw_ref, gn_ref, ga_ref, gb_ref, kvct_ref, kvst_ref, kvwt_ref):
    x = x_ref[...]
    inv = lax.rsqrt(jnp.mean(x * x, axis=-1, keepdims=True) + RMS_EPS)
    h = (x * inv * g_ref[...]).astype(BF16)

    def mm(w):
        return jnp.dot(h, w, preferred_element_type=F32)

    u_ref[...] = mm(wa_ref[:, C_U:C_U + SSM_WIDTH])
    q_ref[...] = mm(wa_ref[:, C_Q:C_Q + Q_W]) * Q_SCALE
    kvs = mm(wa_ref[:, C_KVS:C_KVS + KV_W])
    kvs_ref[...] = kvs
    kvw = mm(wa_ref[:, C_KVW:C_KVW + KV_W])
    kvw_ref[...] = kvw
    gn_ref[...] = jax.nn.sigmoid(mm(wgn_ref[...]))
    ga_ref[...] = jax.nn.sigmoid(mm(wgab_ref[:, 0:D_MODEL])).astype(BF16)
    gb_ref[...] = jax.nn.sigmoid(mm(wgab_ref[:, D_MODEL:2 * D_MODEL])).astype(BF16)
    kvct_ref[0] = mm(wa_ref[:, C_KVC:C_KVC + KV_W]).T
    kvst_ref[0] = kvs.T
    kvwt_ref[0] = kvw.T


def _inproj_sample(x2d, lay, g, w):
    nseq = lay["tm"]
    ts = lay["grid"][1]
    names = ["u", "q", "kvs", "kvw", "gn", "ga", "gb"]
    widths = [SSM_WIDTH, Q_W, KV_W, KV_W, LANES, D_MODEL, D_MODEL]
    out_shapes, out_specs = [], []
    for n, wd in zip(names, widths):
        shp, spec = lay["b" if n == "u" else "a"](wd)
        out_shapes.append(jax.ShapeDtypeStruct(shp, BF16 if n in ("ga", "gb") else F32))
        out_specs.append(spec)
    for n in ("kvct", "kvst", "kvwt"):
        names.append(n)
        out_shapes.append(jax.ShapeDtypeStruct((ts, KV_W, nseq), F32))
        out_specs.append(pl.BlockSpec((1, KV_W, nseq), lambda s, b: (b, 0, 0)))
    x_shape, x_spec = lay["a"](D_MODEL)
    ops = [g, w["wa"], w["wgn"], w["wgab"]]
    outs = pl.pallas_call(
        _inproj_sample_body,
        grid=lay["grid"],
        in_specs=[x_spec] + [_full(o.shape) for o in ops],
        out_specs=out_specs,
        out_shape=out_shapes,
        compiler_params=_cparams(("arbitrary",) * 2, 56),
        name="inproj_sample",
    )(x2d.reshape(x_shape), *ops)
    return dict(zip(names, outs))


def _inproj_params(w_in0):
    return {"wa": w_in0[:, :N_MAIN].astype(BF16),
            "wgn": jnp.pad(w_in0[:, N_MAIN:N_MAIN + NSA_GATE_W], ((0, 0), (0, LANES - NSA_GATE_W))).astype(BF16),
            "wgab": w_in0[:, N_MAIN + NSA_GATE_W:].astype(BF16)}


def _s5_prompt_body(u_ref, wb_ref, ar_ref, ai_ref, cw_ref, d_ref, wglu_ref, bglu_ref, wso_ref,
                    abr_ref, hlast_ref, lhs_ref, bu_ref, h8_ref, p_ref, hstate_ref):
    c = pl.program_id(0)
    nseq = 4
    r4 = u_ref.shape[1]
    tc = r4 // nseq
    half = tc // 2

    @pl.when(c == 0)
    def _():
        hstate_ref[...] = jnp.zeros_like(hstate_ref)

    u = jnp.concatenate([u_ref[j] for j in range(SSM_WIDTH // LANES)], axis=1)
    row2 = lax.broadcasted_iota(jnp.int32, (r4, SSM_WIDTH), 0)
    lo2 = (row2 % SUBLANES) < nseq
    up = pltpu.roll(u, r4 - nseq, axis=0)
    dn = pltpu.roll(u, nseq, axis=0)
    swapped = jnp.where(lo2, up, dn)
    zero = jnp.zeros_like(u)
    ev_re = jnp.where(lo2, u, zero).astype(BF16).reshape(half, SUBLANES, SSM_WIDTH)
    ev_im = jnp.where(lo2, zero, swapped).astype(BF16).reshape(half, SUBLANES, SSM_WIDTH)
    od_re = jnp.where(lo2, swapped, zero).astype(BF16).reshape(half, SUBLANES, SSM_WIDTH)
    od_im = jnp.where(lo2, zero, u).astype(BF16).reshape(half, SUBLANES, SSM_WIDTH)
    for j in range(4):
        sl = slice(LANES * j, LANES * (j + 1))
        lhs_ref[:, 0:8, 256 * j:256 * j + LANES] = ev_re[:, :, sl]
        lhs_ref[:, 0:8, 256 * j + LANES:256 * (j + 1)] = ev_im[:, :, sl]
        lhs_ref[:, 8:16, 256 * j:256 * j + LANES] = od_re[:, :, sl]
        lhs_ref[:, 8:16, 256 * j + LANES:256 * (j + 1)] = od_im[:, :, sl]
    for j in range(4):
        lhs = lhs_ref[:, :, 256 * j:256 * (j + 1)].reshape(tc * SUBLANES, 256)
        bu_ref[:, 512 * j:512 * (j + 1)] = jnp.dot(lhs, wb_ref[j], preferred_element_type=F32)

    for lc in range(4):
        sl = slice(512 * lc, 512 * (lc + 1))
        ar = ar_ref[:, sl]
        ai = ai_ref[:, sl]

        def step(t, h, sl=sl, ar=ar, ai=ai):
            r0 = pl.multiple_of(t * SUBLANES, SUBLANES)
            h = ar * h + ai * pltpu.roll(h, nseq, axis=0) + bu_ref[pl.ds(r0, SUBLANES), sl]
            h8_ref[pl.ds(r0, SUBLANES), sl] = h
            return h

        hstate_ref[:, sl] = lax.fori_loop(0, tc, step, hstate_ref[:, sl], unroll=8)
    hlast_ref[...] = hstate_ref[...]

    for j in range(4):
        pj = jnp.dot(h8_ref[:, 512 * j:512 * (j + 1)].astype(BF16), cw_ref[j], preferred_element_type=F32)
        p_ref[2 * j] = pj[:, 0:LANES]
        p_ref[2 * j + 1] = pj[:, LANES:2 * LANES]
    ys = []
    for s in range(nseq):
        parts = []
        for j in range(4):
            re = p_ref[2 * j, pl.ds(s, tc, stride=SUBLANES), :]
            im = p_ref[2 * j + 1, pl.ds(nseq + s, tc, stride=SUBLANES), :]
            us = u_ref[j, pl.ds(s, tc, stride=nseq), :]
            parts.append(re + im + d_ref[:, LANES * j:LANES * (j + 1)] * us)
        ys.append(jnp.concatenate(parts, axis=1))
    y = jnp.concatenate(ys, axis=0)
    zg = jax.nn.gelu(y)
    gate = jnp.dot(zg.astype(BF16), wglu_ref[...], preferred_element_type=F32) + bglu_ref[...]
    glu = (zg * jax.nn.sigmoid(gate)).astype(BF16)
    abr = jnp.dot(glu, wso_ref[...], preferred_element_type=F32)
    for s in range(nseq):
        abr_ref[s] = abr[s * tc:(s + 1) * tc].astype(BF16)


def _s5_prompt(u_ts, sp, wglu, bglu, wso, t_total, tc):
    nseq = 4
    grid = (t_total // tc,)
    abr, hlast = pl.pallas_call(
        _s5_prompt_body,
        grid=grid,
        in_specs=[pl.BlockSpec((SSM_WIDTH // LANES, tc * nseq, LANES), lambda c: (0, c, 0)),
                  _full(sp["wb8"].shape), _full(sp["ar8"].shape), _full(sp["ai8"].shape), _full(sp["cw8"].shape),
                  _full(sp["d"].shape), _full(wglu.shape), _full(bglu.shape), _full(wso.shape)],
        out_specs=[pl.BlockSpec((nseq, tc, D_MODEL), lambda c: (0, c, 0)),
                   pl.BlockSpec((SUBLANES, N_STATE), lambda c: (0, 0))],
        out_shape=[jax.ShapeDtypeStruct((nseq, t_total, D_MODEL), BF16),
                   jax.ShapeDtypeStruct((SUBLANES, N_STATE), F32)],
        scratch_shapes=[pltpu.VMEM((tc // 2, 2 * SUBLANES, 1024), BF16),
                        pltpu.VMEM((tc * SUBLANES, N_STATE), F32),
                        pltpu.VMEM((tc * SUBLANES, N_STATE), F32),
                        pltpu.VMEM((8, tc * SUBLANES, LANES), F32),
                        pltpu.VMEM((SUBLANES, N_STATE), F32)],
        compiler_params=_cparams(("arbitrary",), 56),
        name="s5_prompt",
    )(u_ts, sp["wb8"], sp["ar8"], sp["ai8"], sp["cw8"], sp["d"], wglu, bglu, wso)
    return abr, hlast


def _s5_params(lam_re, lam_im, log_dt, b_re, b_im, c_re, c_im, d_skip):
    lam = lax.complex(lam_re.astype(F32), lam_im.astype(F32))
    dt = jnp.exp(log_dt.astype(F32))[:, None]
    a_bar = jnp.exp(lam * dt)
    b = lax.complex(b_re.astype(F32), b_im.astype(F32))
    b_bar = ((a_bar - 1.0) / lam)[..., None] * b
    eye8 = jnp.eye(8, dtype=F32)

    def bd_b(m):
        return jnp.einsum("ab,jbpc->jacbp", eye8, m.reshape(4, 8, SSM_STATE, SSM_GROUP)).reshape(4, 128, 512)

    def bd_c(m):
        return jnp.einsum("ab,jbcp->japbc", eye8, m.reshape(4, 8, SSM_GROUP, SSM_STATE)).reshape(4, 512, 128)

    wre, wim = bd_b(b_bar.real), bd_b(b_bar.imag)
    cre, cim = bd_c(c_re.astype(F32)), bd_c(c_im.astype(F32))
    ar = a_bar.real.reshape(1, N_STATE)
    ai = a_bar.imag.reshape(1, N_STATE)
    sign = jnp.concatenate([-jnp.ones((4, 1), F32), jnp.ones((4, 1), F32)], axis=0)
    return {
        "wb8": jnp.concatenate([wre, wim], axis=1).astype(BF16),
        "cw8": jnp.concatenate([cre, -cim], axis=2).astype(BF16),
        "ar8": jnp.broadcast_to(ar, (SUBLANES, N_STATE)),
        "ai8": sign * ai,
        "wre": wre.astype(BF16), "wim": wim.astype(BF16),
        "cre": cre.astype(BF16), "cim": cim.astype(BF16),
        "ar": ar, "ai": ai,
        "d": d_skip.astype(F32).reshape(1, SSM_WIDTH),
    }


def _cmp_params(cmp_pe, cmp_w1, cmp_w2):
    eye2 = jnp.eye(2, dtype=F32)
    nhalf = CMP_LEN // CMP_STRIDE
    w1r = cmp_w1.astype(F32).reshape(2, nhalf, CMP_STRIDE, HEAD_DIM, HEAD_DIM)
    wk = jnp.einsum("kside,ph->kipdshe", w1r, eye2).reshape(2, CMP_STRIDE * LANES, nhalf * LANES)
    bk = jnp.einsum("kld,klde->ke", cmp_pe.astype(F32), cmp_w1.astype(F32), precision=lax.Precision.HIGHEST)
    w2k = jnp.einsum("kef,ph->kpehf", cmp_w2.astype(F32), eye2).reshape(2, LANES, LANES)
    return {"wk": wk.astype(BF16), "bk": jnp.tile(bk, (1, N_KV_HEADS)), "w2k": w2k.astype(BF16),
            "w2kt": jnp.swapaxes(w2k, 1, 2).astype(BF16)}


def _compress_hidden(tap, nch, kv, wk_ref, bk_ref):
    x = jnp.concatenate([tap(i).astype(BF16) for i in range(CMP_STRIDE)], axis=1)
    pp = jnp.dot(x, wk_ref[kv], preferred_element_type=F32)
    pre = pp[:, 0:LANES] + pltpu.roll(pp[:, LANES:2 * LANES], nch - 1, axis=0) + bk_ref[kv:kv + 1, :]
    return jax.nn.gelu(pre).astype(BF16)


def _compress_prompt_body(x_ref, wk_ref, bk_ref, w2k_ref, w2kt_ref, ck_ref, cvt_ref):
    nch = x_ref.shape[1] // CMP_STRIDE
    hid = [_compress_hidden(lambda i, kv=kv: x_ref[kv, pl.ds(i, nch, stride=CMP_STRIDE), :], nch, kv, wk_ref, bk_ref)
           for kv in range(2)]
    ck_ref[0] = jnp.dot(hid[0], w2k_ref[0], preferred_element_type=F32).astype(BF16)
    cvt_ref[0] = _dot_t(w2kt_ref[1], hid[1]).astype(BF16)


def _compress_prompt(kvc2, cp, nseq, t):
    nch = t // CMP_STRIDE
    return pl.pallas_call(
        _compress_prompt_body,
        grid=(nseq,),
        in_specs=[pl.BlockSpec((2, t, LANES), lambda n: (0, n, 0)),
                  _full(cp["wk"].shape), _full(cp["bk"].shape), _full(cp["w2k"].shape), _full(cp["w2kt"].shape)],
        out_specs=[pl.BlockSpec((1, nch, LANES), lambda n: (n, 0, 0)),
                   pl.BlockSpec((1, LANES, nch), lambda n: (n, 0, 0))],
        out_shape=[jax.ShapeDtypeStruct((nseq, nch, LANES), BF16),
                   jax.ShapeDtypeStruct((nseq, LANES, nch), BF16)],
        compiler_params=_cparams(("arbitrary",), 48),
        name="compress_prompt",
    )(kvc2, cp["wk"], cp["bk"], cp["w2k"], cp["w2kt"])


def _overlap_t(n_cmp_pad, n_slc_pad):
    j = np.arange(n_cmp_pad)[None, :]
    s = np.arange(n_slc_pad)[:, None]
    ov = (j * CMP_STRIDE <= s * SLC_BLOCK + SLC_BLOCK - 1) & (j * CMP_STRIDE + CMP_LEN - 1 >= s * SLC_BLOCK)
    return jnp.asarray(ov, dtype=BF16)


def _softmax_cols(s, valid):
    sm = jnp.where(valid, s, NEG_INF)
    mx = jnp.max(sm, axis=0, keepdims=True)
    e = jnp.where(valid, jnp.exp2(sm - mx), 0.0)
    l = jnp.sum(e, axis=0, keepdims=True)
    return e * (1.0 / jnp.maximum(l, 1e-30))


def _select_blocks(imp, blk, pos, nblk, axis=0):
    cur = pos // SLC_BLOCK
    forced = (blk == 0) | (blk == cur) | (blk == cur - 1)
    v = jnp.where(forced, imp + FORCE_BONUS, imp)
    v = jnp.where(blk * SLC_BLOCK <= pos, v, NEG_INF)
    v = jnp.where(blk < nblk, v, -3e38)
    blk_f = blk.astype(F32)
    neg = jnp.full(imp.shape, NEG_INF, F32)
    for _ in range(min(TOP_N, nblk)):
        mx = jnp.max(v, axis=axis, keepdims=True)
        first = jnp.min(jnp.where(v == mx, blk_f, float(imp.shape[axis])), axis=axis, keepdims=True)
        pick = blk_f == first
        neg = jnp.where(pick, 0.0, neg)
        v = jnp.where(pick, -3e38, v)
    return neg


CB = 2 * LANES


def _attn_prompt_body(q_ref, gn_ref, ck_ref, cvt_ref, ks_ref, vst_ref, kw_ref, vwt_ref, ovt_ref,
                      o_ref, kaug_ref, kwaug_ref, qaug_ref, acc_ref):
    i = pl.program_id(1)
    t = ks_ref.shape[1]
    nch = ck_ref.shape[1]
    nslc = t // SLC_BLOCK
    qb = Q_BLOCK
    ncol = N_Q_HEADS * qb
    ncb = ncol // CB
    q0 = i * qb
    one_row = 2 * LANES - HEAD_DIM
    hrows = [slice(HEAD_DIM * ((CB * cb // qb) // GQA), HEAD_DIM * ((CB * cb // qb) // GQA + 1)) for cb in range(ncb)]

    @pl.when(i == 0)
    def _():
        kaug_ref[:, 0:LANES] = ks_ref[0]
        blk = lax.broadcasted_iota(jnp.int32, (t, LANES), 0) // SLC_BLOCK
        col = lax.broadcasted_iota(jnp.int32, (t, LANES), 1)
        kaug_ref[:, LANES:2 * LANES] = jnp.where(blk == col, 1.0, 0.0).astype(BF16)
        padcol = lax.broadcasted_iota(jnp.int32, (WINDOW, 2 * LANES), 1)
        kwaug_ref[0:WINDOW, :] = jnp.where(padcol == one_row, NEG_INF, 0.0).astype(BF16)
        kwaug_ref[WINDOW:WINDOW + t, 0:LANES] = kw_ref[0]
        kwaug_ref[WINDOW:WINDOW + t, LANES:2 * LANES] = jnp.zeros((t, LANES), BF16)

    zeros64 = jnp.zeros((HEAD_DIM, qb), BF16)
    for j in range(N_Q_HEADS):
        dst = j // GQA
        qaug_ref[HEAD_DIM * dst:HEAD_DIM * (dst + 1), qb * j:qb * (j + 1)] = q_ref[0, HEAD_DIM * j:HEAD_DIM * (j + 1), :]
        qaug_ref[HEAD_DIM * (1 - dst):HEAD_DIM * (2 - dst), qb * j:qb * (j + 1)] = zeros64
    tail_row = lax.broadcasted_iota(jnp.int32, (HEAD_DIM, ncol), 0)
    qaug_ref[one_row:2 * LANES, :] = jnp.where(tail_row == 0, 1.0, 0.0).astype(BF16)

    qaug_ref[LANES:one_row, :] = jnp.zeros((one_row - LANES, ncol), BF16)
    q_blocks = qb // TK_WIN
    npiece = (WINDOW + qb) // TK_WIN
    kws = [kwaug_ref[pl.ds(pl.multiple_of(q0 + w * TK_WIN, TK_WIN), TK_WIN), :] for w in range(npiece)]
    vwt = jnp.concatenate([vwt_ref[0, jnp.maximum(i * q_blocks + w - WINDOW // TK_WIN, 0)] for w in range(npiece)],
                          axis=1)
    wrow = lax.broadcasted_iota(jnp.int32, (TK_WIN, CB), 0)
    wcol = lax.broadcasted_iota(jnp.int32, (TK_WIN, CB), 1) & (qb - 1)
    wbias = []
    for w in range(npiece):
        lo, hi = w * TK_WIN - WINDOW, w * TK_WIN - WINDOW + TK_WIN - 1
        if hi <= 0 and qb - 1 - lo < WINDOW:
            wbias.append(None)
        else:
            dlt = wcol - wrow - lo
            wbias.append(jnp.where((dlt >= 0) & (dlt < WINDOW), 0.0, NEG_INF))
    kw_all = jnp.concatenate(kws, axis=0)
    sws = [jnp.dot(kw_all, qaug_ref[:, CB * cb:CB * (cb + 1)], preferred_element_type=F32) for cb in range(ncb)]
    es, rls = [], []
    for s in sws:
        s = jnp.concatenate([s[TK_WIN * w:TK_WIN * (w + 1)] if b is None else s[TK_WIN * w:TK_WIN * (w + 1)] + b
                             for w, b in enumerate(wbias)], axis=0)
        e = jnp.exp2(s - jnp.max(s, axis=0, keepdims=True))
        es.append(e.astype(BF16))
        rls.append(1.0 / jnp.sum(e, axis=0, keepdims=True))
    ow = jnp.concatenate([jnp.dot(vwt[hrows[cb]], e, preferred_element_type=F32) * rl
                          for cb, (e, rl) in enumerate(zip(es, rls))], axis=1)

    pos_c = q0 + (lax.broadcasted_iota(jnp.int32, (nch, CB), 1) & (qb - 1))
    cvalid = lax.broadcasted_iota(jnp.int32, (nch, CB), 0) * CMP_STRIDE + (CMP_LEN - 1) <= pos_c
    scs = [jnp.dot(ck_ref[0], qaug_ref[0:LANES, CB * cb:CB * (cb + 1)], preferred_element_type=F32)
           for cb in range(ncb)]
    pcs = [_softmax_cols(sc, cvalid).astype(BF16) for sc in scs]
    oc = jnp.concatenate([jnp.dot(cvt_ref[0, hrows[cb], :], pc, preferred_element_type=F32)
                          for cb, pc in enumerate(pcs)], axis=1)
    imp = jnp.concatenate([jnp.dot(ovt_ref[...], pc, preferred_element_type=F32) for pc in pcs], axis=1)
    blk = lax.broadcasted_iota(jnp.int32, (nslc, qb), 0)
    pos_q = q0 + lax.broadcasted_iota(jnp.int32, (nslc, qb), 1)
    for h in range(N_KV_HEADS):
        v = imp[0:nslc, qb * GQA * h:qb * GQA * h + qb]
        for g in range(1, GQA):
            v = v + imp[0:nslc, qb * (GQA * h + g):qb * (GQA * h + g + 1)]
        neg = _select_blocks(v, blk, pos_q, nslc).astype(BF16)
        for g in range(GQA):
            j = GQA * h + g
            qaug_ref[LANES:LANES + nslc, qb * j:qb * (j + 1)] = neg

    brow = lax.broadcasted_iota(jnp.int32, (qb, CB), 0)
    bcol = lax.broadcasted_iota(jnp.int32, (qb, CB), 1) & (qb - 1)
    tri_lo = jnp.where(brow <= bcol, 0.0, NEG_INF)

    acc_ref[...] = jnp.zeros_like(acc_ref)

    def sel_tile(k0, nk, vt, carry, bias):
        m, l = carry
        ka = kaug_ref[pl.ds(k0, nk), :]
        css = [slice(CB * cb, CB * (cb + 1)) for cb in range(ncb)]
        ss = [jnp.dot(ka, qaug_ref[:, cs], preferred_element_type=F32) for cs in css]
        ms, ls, ps, alphas = [], [], [], []
        for cs, s in zip(css, ss):
            if bias is not None:
                s = s + bias
            mn = jnp.maximum(m[:, cs], jnp.max(s, axis=0, keepdims=True))
            alpha = jnp.exp2(m[:, cs] - mn)
            p = jnp.exp2(s - mn)
            ms.append(mn)
            ls.append(alpha * l[:, cs] + jnp.sum(p, axis=0, keepdims=True))
            ps.append(p.astype(BF16))
            alphas.append(alpha)
        pvs = [jnp.dot(vt[hrows[cb]], p, preferred_element_type=F32) for cb, p in enumerate(ps)]
        for cs, alpha, pv in zip(css, alphas, pvs):
            acc_ref[:, cs] = alpha * acc_ref[:, cs] + pv
        return jnp.concatenate(ms, axis=1), jnp.concatenate(ls, axis=1)

    def vt_blocks(ref, b0, n):
        return jnp.concatenate([ref[0, b0 + j] for j in range(n)], axis=1) if n > 1 else ref[0, b0]

    big_blocks = TK_SLC // TK_WIN

    def big_tile(kt, carry):
        return sel_tile(pl.multiple_of(kt * TK_SLC, TK_SLC), TK_SLC, vt_blocks(vst_ref, kt * big_blocks, big_blocks),
                        carry, None)

    def small_tile(kb, carry):
        return sel_tile(pl.multiple_of(kb * qb, qb), qb, vt_blocks(vst_ref, kb * q_blocks, q_blocks), carry, None)

    carry = (jnp.full((1, ncol), NEG_INF, F32), jnp.zeros((1, ncol), F32))
    nbig = q0 // TK_SLC
    carry = lax.fori_loop(0, nbig, big_tile, carry)
    carry = lax.fori_loop(nbig * (TK_SLC // qb), i, small_tile, carry)
    _, l = sel_tile(pl.multiple_of(q0, qb), qb, vt_blocks(vst_ref, i * q_blocks, q_blocks), carry, tri_lo)
    osel = acc_ref[...] * (1.0 / l)

    gt = gn_ref[0]
    for c in range(N_Q_HEADS // 2):
        rows = []
        for hh in range(2):
            j = 2 * c + hh
            cs = slice(qb * j, qb * (j + 1))
            rows.append(gt[3 * j:3 * j + 1, :] * oc[:, cs] + gt[3 * j + 1:3 * j + 2, :] * osel[:, cs]
                        + gt[3 * j + 2:3 * j + 3, :] * ow[:, cs])
        o_ref[:, LANES * c:LANES * (c + 1)] = jnp.concatenate(rows, axis=0).T.astype(o_ref.dtype)


def _attn_prompt(q, gn, ck, cvt, ksb, vst, kwb, vwt, nseq, t):
    nb = t // Q_BLOCK
    nch = t // CMP_STRIDE
    nslc = t // SLC_BLOCK
    ovt = _overlap_t(nch, max(nslc, SUBLANES))
    row = lambda n, i: (n * nb + i, 0)
    seq3 = lambda n, i: (n, 0, 0)
    seq4 = lambda n, i: (n, 0, 0, 0)
    col3 = lambda n, i: (n, 0, i)
    return pl.pallas_call(
        _attn_prompt_body,
        grid=(nseq, nb),
        in_specs=[pl.BlockSpec((1, Q_W, Q_BLOCK), col3), pl.BlockSpec((1, gn.shape[1], Q_BLOCK), col3),
                  pl.BlockSpec((1, nch, LANES), seq3), pl.BlockSpec((1, LANES, nch), seq3),
                  pl.BlockSpec((1, t, LANES), seq3), pl.BlockSpec((1, t // TK_WIN, LANES, TK_WIN), seq4),
                  pl.BlockSpec((1, t, LANES), seq3), pl.BlockSpec((1, t // TK_WIN, LANES, TK_WIN), seq4),
                  _full(ovt.shape)],
        out_specs=pl.BlockSpec((Q_BLOCK, Q_W), row),
        out_shape=jax.ShapeDtypeStruct((nseq * t, Q_W), BF16),
        scratch_shapes=[pltpu.VMEM((t, 2 * LANES), BF16),
                        pltpu.VMEM((WINDOW + t, 2 * LANES), BF16),
                        pltpu.VMEM((2 * LANES, N_Q_HEADS * Q_BLOCK), BF16),
                        pltpu.VMEM((HEAD_DIM, N_Q_HEADS * Q_BLOCK), F32)],
        compiler_params=_cparams(("arbitrary", "arbitrary"), 56),
        name="attn_prompt",
    )(q, gn, ck, cvt, ksb.reshape(nseq, t, LANES), vst, kwb.reshape(nseq, t, LANES), vwt, ovt)


def _post_body(x_ref, abr_ref, on_ref, ga_ref, gb_ref, wno_ref, wo_ref, g2_ref, x1_ref, h2_ref):
    bbr = jnp.dot(on_ref[...].astype(BF16), wno_ref[...], preferred_element_type=F32)
    merged = ga_ref[...].astype(F32) * abr_ref[...].astype(F32) + gb_ref[...].astype(F32) * bbr
    x1 = x_ref[...] + jnp.dot(merged.astype(BF16), wo_ref[...], preferred_element_type=F32)
    x1_ref[...] = x1
    inv = lax.rsqrt(jnp.mean(x1 * x1, axis=-1, keepdims=True) + RMS_EPS)
    h2_ref[...] = (x1 * inv * g2_ref[...]).astype(BF16)


def _post(x2d, abr, abr_lay, onsa, ga, gb, wno, wo, g2, lay, out_lay):
    x_shape, x_spec = lay["a"](D_MODEL)
    abr_shape, abr_spec = lay[abr_lay](D_MODEL)
    on_shape, on_spec = lay["a"](Q_W)
    o_shape, o_spec = lay[out_lay](D_MODEL)
    return pl.pallas_call(
        _post_body,
        grid=lay["grid"],
        in_specs=[x_spec, abr_spec, on_spec, x_spec, x_spec, _full(wno.shape), _full(wo.shape), _full(g2.shape)],
        out_specs=[o_spec, o_spec],
        out_shape=[jax.ShapeDtypeStruct(o_shape, F32), jax.ShapeDtypeStruct(o_shape, BF16)],
        compiler_params=_cparams(("arbitrary",) * len(lay["grid"]), 48),
        name="post",
    )(x2d.reshape(x_shape), abr.reshape(abr_shape), onsa.reshape(on_shape), ga, gb, wno, wo, g2)


def _route(logits):
    lane = lax.broadcasted_iota(jnp.int32, logits.shape, 1).astype(F32)
    big = float(LANES)
    glog = jnp.where(lane < N_EXPERT_GROUPS, logits, -jnp.inf)
    gmax = jnp.max(glog, axis=1, keepdims=True)
    gsel = jnp.min(jnp.where(glog == gmax, lane, big), axis=1, keepdims=True)
    gw = 1.0 / jnp.sum(jnp.exp(glog - gmax), axis=1, keepdims=True)
    lo = N_EXPERT_GROUPS + EXPERTS_PER_GROUP * gsel
    el = jnp.where((lane >= lo) & (lane < lo + EXPERTS_PER_GROUP), logits, -jnp.inf)
    v1 = jnp.max(el, axis=1, keepdims=True)
    i1 = jnp.min(jnp.where(el == v1, lane, big), axis=1, keepdims=True)
    el2 = jnp.where(lane == i1, -jnp.inf, el)
    v2 = jnp.max(el2, axis=1, keepdims=True)
    i2 = jnp.min(jnp.where(el2 == v2, lane, big), axis=1, keepdims=True)
    e2 = jnp.exp(v2 - v1)
    w1 = gw / (1.0 + e2)
    return jnp.where(lane == i1, w1, 0.0) + jnp.where(lane == i2, w1 * e2, 0.0)


def _moe_body(x1_ref, h2_ref, p_ref, wr_ref, br_ref, wg_ref, wu_ref, wd_ref, wpg_ref, wp_ref, gf_ref,
              y_ref, acc_ref, comb_ref, *, tsplit):
    g = pl.program_id(1)
    h2 = h2_ref[...]

    @pl.when(g == 0)
    def _():
        logits = jnp.dot(h2, wr_ref[...], preferred_element_type=F32) + br_ref[...]
        comb_ref[...] = _route(logits)
        acc_ref[...] = jnp.zeros_like(acc_ref)

    comb = comb_ref[...]
    lane = lax.broadcasted_iota(jnp.int32, comb.shape, 1)
    acc = acc_ref[...]
    for k in range(EXPERTS_PER_GROUP):
        e_lane = N_EXPERT_GROUPS + EXPERTS_PER_GROUP * g + k
        ce = jnp.sum(jnp.where(lane == e_lane, comb, 0.0), axis=1, keepdims=True)
        a = jnp.dot(h2, wg_ref[k], preferred_element_type=F32)
        b = jnp.dot(h2, wu_ref[k], preferred_element_type=F32)
        act = (jax.nn.silu(a) * b * ce).astype(BF16)
        acc = acc + jnp.dot(act, wd_ref[k], preferred_element_type=F32)
    acc_ref[...] = acc

    @pl.when(g == N_EXPERT_GROUPS - 1)
    def _():
        x2 = x1_ref[...] + acc_ref[...]
        rows = x2.shape[0] // tsplit
        if tsplit == 1:
            p = p_ref[...]
        else:
            p = jnp.concatenate([p_ref[:, PLE_DIM * t:PLE_DIM * (t + 1)] for t in range(tsplit)], axis=0)
        gate = jax.nn.sigmoid(jnp.dot(x2.astype(BF16), wpg_ref[...], preferred_element_type=F32))
        x3 = x2 + gate * jnp.dot(p.astype(BF16), wp_ref[...], preferred_element_type=F32)
        inv = lax.rsqrt(jnp.mean(x3 * x3, axis=-1, keepdims=True) + RMS_EPS)
        y = x3 * inv * gf_ref[...]
        if tsplit == 1:
            y_ref[...] = y
        else:
            for t in range(tsplit):
                y_ref[:, D_MODEL * t:D_MODEL * (t + 1)] = y[rows * t:rows * (t + 1)]


def _moe(x1, h2, p, mp, tm, tsplit):
    rows = x1.shape[0]
    nrb = rows // tm
    rb = lambda r, g: (r, 0)
    grp = lambda r, g: (g, 0, 0)
    if tsplit == 1:
        p_spec = pl.BlockSpec((tm, PLE_DIM), rb)
        y_spec = pl.BlockSpec((tm, D_MODEL), rb)
        y_shape = (rows, D_MODEL)
    else:
        assert nrb == 1
        p_spec = _full(p.shape)
        y_shape = (rows // tsplit, tsplit * D_MODEL)
        y_spec = _full(y_shape)
    return pl.pallas_call(
        functools.partial(_moe_body, tsplit=tsplit),
        grid=(nrb, N_EXPERT_GROUPS),
        in_specs=[pl.BlockSpec((tm, D_MODEL), rb), pl.BlockSpec((tm, D_MODEL), rb), p_spec,
                  _full(mp["wr"].shape), _full(mp["br"].shape),
                  pl.BlockSpec((EXPERTS_PER_GROUP, D_MODEL, D_FF_EXPERT), grp),
                  pl.BlockSpec((EXPERTS_PER_GROUP, D_MODEL, D_FF_EXPERT), grp),
                  pl.BlockSpec((EXPERTS_PER_GROUP, D_FF_EXPERT, D_MODEL), grp),
                  _full(mp["wpg"].shape, True), _full(mp["wp"].shape, True), _full(mp["gf"].shape)],
        out_specs=y_spec,
        out_shape=jax.ShapeDtypeStruct(y_shape, F32),
        scratch_shapes=[pltpu.VMEM((tm, D_MODEL), F32), pltpu.VMEM((tm, LANES), F32)],
        compiler_params=_cparams(("arbitrary", "arbitrary"), 60),
        name="moe_ple",
    )(x1, h2, p, mp["wr"], mp["br"], mp["wg"], mp["wu"], mp["wd"], mp["wpg"], mp["wp"], mp["gf"])


def _s5_sample_body(u_ref, h0re_ref, h0im_ref, wre_ref, wim_ref, ar_ref, ai_ref, cre_ref, cim_ref, d_ref,
                    wglu_ref, bglu_ref, wso_ref, abr_ref, hre_out_ref, him_out_ref,
                    bure_ref, buim_ref, hre_ref, him_ref, *, nseq, nstep):
    u = u_ref[...]
    ub = u.astype(BF16)
    for j in range(4):
        lhs = ub[:, LANES * j:LANES * (j + 1)]
        bure_ref[:, 512 * j:512 * (j + 1)] = jnp.dot(lhs, wre_ref[j], preferred_element_type=F32)
        buim_ref[:, 512 * j:512 * (j + 1)] = jnp.dot(lhs, wim_ref[j], preferred_element_type=F32)
    for lc in range(4):
        sl = slice(512 * lc, 512 * (lc + 1))
        ar = jnp.broadcast_to(ar_ref[:, sl], (SUBLANES, 512))
        ai = jnp.broadcast_to(ai_ref[:, sl], (SUBLANES, 512))

        def body(rc, carry, sl=sl, ar=ar, ai=ai):
            r0 = pl.multiple_of(rc * SUBLANES, SUBLANES)
            hr = h0re_ref[pl.ds(r0, SUBLANES), sl]
            hi = h0im_ref[pl.ds(r0, SUBLANES), sl]
            for t in range(nstep):
                rr = pl.multiple_of(t * nseq + rc * SUBLANES, SUBLANES)
                hr, hi = (ar * hr - ai * hi + bure_ref[pl.ds(rr, SUBLANES), sl],
                          ar * hi + ai * hr + buim_ref[pl.ds(rr, SUBLANES), sl])
                hre_ref[pl.ds(rr, SUBLANES), sl] = hr
                him_ref[pl.ds(rr, SUBLANES), sl] = hi
            hre_out_ref[pl.ds(r0, SUBLANES), sl] = hr
            him_out_ref[pl.ds(r0, SUBLANES), sl] = hi
            return carry

        lax.fori_loop(0, nseq // SUBLANES, body, 0)
    parts = []
    for j in range(4):
        sl = slice(512 * j, 512 * (j + 1))
        parts.append(jnp.dot(hre_ref[:, sl].astype(BF16), cre_ref[j], preferred_element_type=F32)
                     - jnp.dot(him_ref[:, sl].astype(BF16), cim_ref[j], preferred_element_type=F32))
    y = jnp.concatenate(parts, axis=1) + d_ref[...] * u
    zg = jax.nn.gelu(y)
    gate = jnp.dot(zg.astype(BF16), wglu_ref[...], preferred_element_type=F32) + bglu_ref[...]
    glu = (zg * jax.nn.sigmoid(gate)).astype(BF16)
    abr_ref[...] = jnp.dot(glu, wso_ref[...], preferred_element_type=F32).astype(BF16)


def _s5_sample(u_ts, h0re, h0im, sp, wglu, bglu, wso, nseq, nstep):
    rows = nseq * nstep
    ops = [u_ts, h0re, h0im, sp["wre"], sp["wim"], sp["ar"], sp["ai"], sp["cre"], sp["cim"], sp["d"], wglu, bglu, wso]
    return pl.pallas_call(
        functools.partial(_s5_sample_body, nseq=nseq, nstep=nstep),
        grid=(1,),
        in_specs=[_full(o.shape) for o in ops],
        out_specs=[_full((rows, D_MODEL)), _full((nseq, N_STATE)), _full((nseq, N_STATE))],
        out_shape=[jax.ShapeDtypeStruct((rows, D_MODEL), BF16),
                   jax.ShapeDtypeStruct((nseq, N_STATE), F32), jax.ShapeDtypeStruct((nseq, N_STATE), F32)],
        scratch_shapes=[pltpu.VMEM((rows, N_STATE), F32) for _ in range(4)],
        compiler_params=_cparams(("arbitrary",), 56),
        name="s5_sample",
    )(*ops)


def _softmax_rows(s, valid):
    sm = jnp.where(valid, s, NEG_INF)
    mx = jnp.max(sm, axis=1, keepdims=True)
    e = jnp.where(valid, jnp.exp2(sm - mx), 0.0)
    l = jnp.sum(e, axis=1, keepdims=True)
    return e * (1.0 / jnp.maximum(l, 1e-30))


SAMPLE_SEQS_PER_STEP = 4
CMP_PITCH = 24


def _attn_sample_body(pt_ref, q_ref, gn_ref, nks_ref, nkw_ref, wint_ref, wk_ref, bk_ref, w2k_ref, ov_ref, e_ref, *rest,
                      npage, past_len, nsub, tq):
    o_ref, nwint_ref, xrow_ref = rest[2 * nsub * npage:]
    nrow = N_Q_HEADS * tq
    nwin = wint_ref.shape[2]
    nslc = -(-(past_len + tq) // SLC_BLOCK)
    nch = past_len // CMP_STRIDE
    per_page = PAGE_SIZE // CMP_STRIDE

    cmp = []
    for kv in range(2):
        for s in range(nsub):
            for p in range(npage):
                rows = rest[s * npage + p][0, LANES * kv:LANES * (kv + 1), :].T
                for c in range(per_page):
                    r0 = CMP_PITCH * (per_page * p + c)
                    xrow_ref[s, r0:r0 + CMP_STRIDE, :] = rows[CMP_STRIDE * c:CMP_STRIDE * (c + 1)]
        x = jnp.concatenate(
            [jnp.concatenate([xrow_ref[s, pl.ds(i, nch, stride=CMP_PITCH), :] for i in range(CMP_STRIDE)], axis=1)
             for s in range(nsub)], axis=0)
        pp = jnp.dot(x.astype(BF16), wk_ref[kv], preferred_element_type=F32)
        pre = pp[:, 0:LANES] + pltpu.roll(pp[:, LANES:2 * LANES], nsub * nch - 1, axis=0) + bk_ref[kv:kv + 1, :]
        cmp.append(jnp.dot(jax.nn.gelu(pre).astype(BF16), w2k_ref[kv], preferred_element_type=F32).astype(BF16))
    cks = [cmp[0][nch * s:nch * (s + 1)] for s in range(nsub)]
    cvs = [cmp[1][nch * s:nch * (s + 1)] for s in range(nsub)]
    slc_pages = [rest[(nsub + s) * npage:(nsub + s + 1) * npage] for s in range(nsub)]
    seqs = range(nsub)
    rcat = lambda parts: jnp.concatenate(parts, axis=0)

    lane_w = lax.broadcasted_iota(jnp.int32, (KV_W, LANES), 1)
    lane8 = lax.broadcasted_iota(jnp.int32, (tq, LANES), 1)
    nks_l, nkw_l, wint_l, qs_l = [], [], [], []
    for s in seqs:
        rows_s = slice(tq * s, tq * (s + 1))
        nks_l.append(jnp.concatenate([nks_ref[rows_s, :], jnp.zeros((LANES - tq, KV_W), F32)], axis=0))
        nkw = jnp.concatenate([nkw_ref[rows_s, :], jnp.zeros((LANES - tq, KV_W), F32)], axis=0)
        nkw_l.append(nkw)
        wint = wint_ref[s]
        wint_l.append(wint)
        shifted = pltpu.roll(wint, nwin - tq, axis=1)
        new_t = pltpu.roll(nkw.T, LANES - tq, axis=1)
        nwint_ref[s, :, 0:nwin - LANES] = shifted[:, 0:nwin - LANES]
        nwint_ref[s, :, nwin - LANES:nwin] = jnp.where(lane_w >= LANES - tq, new_t, shifted[:, nwin - LANES:nwin])
        q = q_ref[rows_s, :]
        qrows = []
        for j in range(N_Q_HEADS):
            chunk = q[:, LANES * (j // 2):LANES * (j // 2 + 1)]
            dst = j // GQA
            if (j % 2) != dst:
                chunk = pltpu.roll(chunk, HEAD_DIM, axis=1)
            keep = (lane8 < HEAD_DIM) if dst == 0 else (lane8 >= HEAD_DIM)
            qrows.append(jnp.where(keep, chunk, 0.0))
        qs_l.append(jnp.concatenate(qrows, axis=0).astype(BF16))

    rtot = nsub * nrow
    seq_rows = [slice(nrow * s, nrow * (s + 1)) for s in seqs]
    pos = past_len + (lax.broadcasted_iota(jnp.int32, (rtot, LANES), 0) & (tq - 1))
    lane = lax.broadcasted_iota(jnp.int32, (rtot, LANES), 1)

    sc = rcat([_dot_t(qs_l[s], cks[s]) for s in seqs])
    pc = _softmax_rows(sc, lane * CMP_STRIDE + (CMP_LEN - 1) <= pos).astype(BF16)
    oc = rcat([jnp.dot(pc[seq_rows[s]], cvs[s], preferred_element_type=F32) for s in seqs])
    imp = jnp.dot(pc, ov_ref[...], preferred_element_type=F32)
    vs = []
    for s in seqs:
        for h in range(N_KV_HEADS):
            r0 = nrow * s + tq * GQA * h
            v = imp[r0:r0 + tq]
            for g in range(1, GQA):
                v = v + imp[r0 + tq * g:r0 + tq * (g + 1)]
            vs.append(v)
    nsel = len(vs) * tq
    vt = rcat(vs + [jnp.zeros((LANES - nsel, LANES), F32)]).T
    nblk_pad = -(-nslc // SUBLANES) * SUBLANES
    blk_t = lax.broadcasted_iota(jnp.int32, (nblk_pad, LANES), 0)
    pos_t = past_len + (lax.broadcasted_iota(jnp.int32, (nblk_pad, LANES), 1) & (tq - 1))
    neg_t = _select_blocks(vt[0:nblk_pad], blk_t, pos_t, nslc, axis=0)
    neg = rcat([neg_t, jnp.zeros((LANES - nblk_pad, LANES), F32)]).T
    negsel = rcat([neg[tq * (N_KV_HEADS * s + j // GQA):tq * (N_KV_HEADS * s + j // GQA + 1)]
                   for s in seqs for j in range(N_Q_HEADS)])
    negsel_b = negsel.astype(BF16)

    new_blk = past_len // SLC_BLOCK
    ss_l = []
    for s in seqs:
        qaug = jnp.concatenate([qs_l[s], negsel_b[seq_rows[s]]], axis=1)
        parts = []
        for p in range(0, npage, 2):
            kt = jnp.concatenate([slc_pages[s][p][0][0:LANES], slc_pages[s][p + 1][0][0:LANES]], axis=1).astype(BF16)
            et = jnp.concatenate([e_ref[p], e_ref[p + 1]], axis=1)
            parts.append(jnp.dot(qaug, jnp.concatenate([kt, et], axis=0), preferred_element_type=F32))
        parts.append(_dot_t(qs_l[s], nks_l[s][:, 0:LANES].astype(BF16)) + negsel[seq_rows[s], new_blk:new_blk + 1])
        ss_l.append(jnp.concatenate(parts, axis=1))
    ss = rcat(ss_l)
    nkeys = ss.shape[1]
    kpos = lax.broadcasted_iota(jnp.int32, (rtot, nkeys), 1)
    pos_k = past_len + (lax.broadcasted_iota(jnp.int32, (rtot, nkeys), 0) & (tq - 1))
    ps = _softmax_rows(ss, kpos <= pos_k).astype(BF16)
    osel_l = []
    for s in seqs:
        psq = ps[seq_rows[s]]
        o = jnp.dot(psq[:, past_len:nkeys], nks_l[s][:, LANES:2 * LANES].astype(BF16), preferred_element_type=F32)
        for p in range(0, npage, 2):
            vtp = jnp.concatenate([slc_pages[s][p][0][LANES:2 * LANES], slc_pages[s][p + 1][0][LANES:2 * LANES]],
                                  axis=1).astype(BF16)
            o = o + _dot_t(psq[:, PAGE_SIZE * p:PAGE_SIZE * (p + 2)], vtp)
        osel_l.append(o)
    osel = rcat(osel_l)

    sw = rcat([jnp.concatenate([jnp.dot(qs_l[s], wint_l[s][0:LANES].astype(BF16), preferred_element_type=F32),
                                _dot_t(qs_l[s], nkw_l[s][:, 0:LANES].astype(BF16))], axis=1) for s in seqs])
    nw = sw.shape[1]
    widx = lax.broadcasted_iota(jnp.int32, (rtot, nw), 1)
    pos_w = past_len + (lax.broadcasted_iota(jnp.int32, (rtot, nw), 0) & (tq - 1))
    dlt = pos_w - (past_len - nwin + widx)
    pw = _softmax_rows(sw, (dlt >= 0) & (dlt < WINDOW) & (widx < nwin + tq)).astype(BF16)
    ow = rcat([_dot_t(pw[seq_rows[s], 0:nwin], wint_l[s][LANES:2 * LANES].astype(BF16))
               + jnp.dot(pw[seq_rows[s], nwin:nw], nkw_l[s][:, LANES:2 * LANES].astype(BF16),
                         preferred_element_type=F32) for s in seqs])

    for s in seqs:
        rows_s = slice(tq * s, tq * (s + 1))
        gn = gn_ref[rows_s, :]
        for c in range(N_Q_HEADS // 2):
            halves = []
            for hh in range(2):
                j = 2 * c + hh
                rs = slice(nrow * s + tq * j, nrow * s + tq * (j + 1))
                oj = (gn[:, 3 * j:3 * j + 1] * oc[rs] + gn[:, 3 * j + 1:3 * j + 2] * osel[rs]
                      + gn[:, 3 * j + 2:3 * j + 3] * ow[rs])
                if (j // GQA) != hh:
                    oj = pltpu.roll(oj, HEAD_DIM, axis=1)
                halves.append(oj)
            o_ref[rows_s, LANES * c:LANES * (c + 1)] = jnp.where(lane8 < HEAD_DIM, halves[0], halves[1])


def _attn_sample(q, gn, nks, nkw, cache_cmp, cache_slc, cache_win, page_table, cp, nseq, tq, past_len):
    assert tq <= CMP_STRIDE and past_len % PAGE_SIZE == 0
    npage = past_len // PAGE_SIZE
    assert npage % 2 == 0 and PAGE_SIZE == LANES
    n_pool = cache_cmp.shape[0]
    nwin = cache_win.shape[1]
    chunks = past_len // CMP_STRIDE
    ov = _overlap_t(chunks, LANES).T
    key = np.arange(past_len).reshape(npage, 1, PAGE_SIZE)
    e = jnp.asarray(np.arange(LANES).reshape(1, LANES, 1) == key // SLC_BLOCK, dtype=BF16)
    to_t = lambda c: jnp.transpose(c, (0, 2, 3, 4, 1)).reshape(c.shape[0], KV_W, c.shape[1])
    cmp_t, slc_t, win_t = to_t(cache_cmp), to_t(cache_slc), to_t(cache_win)
    nsub = SAMPLE_SEQS_PER_STEP
    assert nseq % nsub == 0
    row = lambda n, pt: (n, 0)
    seq3 = lambda n, pt: (n, 0, 0)
    page = lambda s, p: (lambda n, pt: (pt[n * nsub + s, p], 0, 0))
    consts = [cp["wk"], cp["bk"], cp["w2k"], ov, e]
    in_specs = [pl.BlockSpec((nsub * tq, Q_W), row), pl.BlockSpec((nsub * tq, LANES), row),
                pl.BlockSpec((nsub * tq, KV_W), row), pl.BlockSpec((nsub * tq, KV_W), row),
                pl.BlockSpec((nsub, KV_W, nwin), seq3)]
    in_specs += [pl.BlockSpec(c.shape, (lambda nd: lambda n, pt: (0,) * nd)(c.ndim)) for c in consts]
    pages = [pl.BlockSpec((1, KV_W, PAGE_SIZE), page(s, p)) for s in range(nsub) for p in range(npage)]
    in_specs += pages * 2
    grid_spec = pltpu.PrefetchScalarGridSpec(
        num_scalar_prefetch=1,
        grid=(nseq // nsub,),
        in_specs=in_specs,
        out_specs=[pl.BlockSpec((nsub * tq, Q_W), row), pl.BlockSpec((nsub, KV_W, nwin), seq3)],
        scratch_shapes=[pltpu.VMEM((nsub, chunks * CMP_PITCH, LANES), F32)],
    )
    return pl.pallas_call(
        functools.partial(_attn_sample_body, npage=npage, past_len=past_len, nsub=nsub, tq=tq),
        grid_spec=grid_spec,
        out_shape=[jax.ShapeDtypeStruct((nseq * tq, Q_W), F32), jax.ShapeDtypeStruct((nseq, KV_W, nwin), F32)],
        compiler_params=_cparams(("arbitrary",), 56),
        name="attn_sample",
    )(page_table, q, gn, nks, nkw, win_t, *consts, *([cmp_t] * (nsub * npage)), *([slc_t] * (nsub * npage)))


def _moe_params(w_rg, b_rg, w_re, b_re, w_gate, w_up, w_down, w_ple, w_ple_gate, gf):
    pad = LANES - N_EXPERT_GROUPS - N_EXPERTS
    return {"wr": jnp.pad(jnp.concatenate([w_rg, w_re], axis=1), ((0, 0), (0, pad))).astype(BF16),
            "br": jnp.pad(jnp.concatenate([b_rg, b_re]), (0, pad)).astype(F32).reshape(1, LANES),
            "wg": w_gate.astype(BF16), "wu": w_up.astype(BF16), "wd": w_down.astype(BF16),
            "wpg": w_ple_gate.astype(BF16), "wp": w_ple.astype(BF16), "gf": gf.astype(F32).reshape(1, D_MODEL)}


TM_PROMPT = 512
TM_MOE = 1024
TC_S5 = 128


def kernel(x_prompt, x_sample, p_prompt, p_sample, cache_cmp_kv, cache_slc_kv, cache_win_kv, state_ssm, page_table, norm1_g, w_in, ssm_lam_re, ssm_lam_im, ssm_log_dt, ssm_b_re, ssm_b_im, ssm_c_re, ssm_c_im, ssm_d, w_glu, b_glu, cmp_pe, cmp_w1, cmp_w2, w_ssm_out, w_nsa_out, w_o, norm2_g, w_route_group, b_route_group, w_route_expert, b_route_expert, w_exp_gate, w_exp_up, w_exp_down, w_ple, w_ple_gate, final_norm_g):
    assert w_in.shape[0] == 1, "one layer"
    l = 0
    nb, t = x_prompt.shape[:2]
    ns, ts = x_sample.shape[:2]
    past_len = page_table.shape[1] * PAGE_SIZE
    kvt = (2, N_KV_HEADS, HEAD_DIM)

    wi = _inproj_params(w_in[l])
    g1 = norm1_g[l].astype(F32).reshape(1, D_MODEL)
    g2 = norm2_g[l].astype(F32).reshape(1, D_MODEL)
    sp = _s5_params(ssm_lam_re[l], ssm_lam_im[l], ssm_log_dt[l], ssm_b_re[l], ssm_b_im[l], ssm_c_re[l], ssm_c_im[l],
                    ssm_d[l])
    cp = _cmp_params(cmp_pe[l], cmp_w1[l], cmp_w2[l])
    mp = _moe_params(w_route_group[l], b_route_group[l], w_route_expert[l], b_route_expert[l], w_exp_gate[l],
                     w_exp_up[l], w_exp_down[l], w_ple[l], w_ple_gate[l], final_norm_g)
    wglu = w_glu[l].astype(BF16)
    bglu = b_glu[l].astype(F32).reshape(1, SSM_WIDTH)
    wso = w_ssm_out[l].astype(BF16)
    wno = w_nsa_out[l].astype(BF16)
    wo = w_o[l].astype(BF16)

    lay = _prompt_layout(nb, t, TM_PROMPT)
    xp = x_prompt.reshape(nb * t, D_MODEL)
    r = _inproj_prompt(xp, lay, g1, wi)
    abr, hlast = _s5_prompt(r["u"], sp, wglu, bglu, wso, t, TC_S5)
    ck, cvt = _compress_prompt(r["kvc"], cp, nb, t)
    onsa = _attn_prompt(r["qt"], r["gnt"], ck, cvt, r["ksb"], r["vst"], r["kwb"], r["vwt"], nb, t)
    x1, h2 = _post(xp, abr, "a", onsa, r["ga"], r["gb"], wno, wo, g2, lay, "a")
    y_prompt = _moe(x1, h2, p_prompt[l].reshape(nb * t, PLE_DIM), mp, TM_MOE, 1).reshape(nb, t, D_MODEL)
    keep = min(WINDOW, t)

    def rows_last(a):
        return jnp.transpose(a.reshape((a.shape[0],) + kvt + (a.shape[2],)), (0, 4, 1, 2, 3))[None]

    new_cmp_p = rows_last(r["kvct"])
    new_slc_p = rows_last(r["kvst"])
    new_win_p = rows_last(r["kvwt"][:, :, t - keep:])
    new_ssm_p = jnp.stack([hlast[0:nb], hlast[nb:2 * nb]], axis=-1).reshape(1, nb, N_SSM_GROUPS, SSM_STATE, 2)

    lays = _sample_layout(ns, ts)
    xs = x_sample.reshape(ns * ts, D_MODEL)
    rs = _inproj_sample(xs, lays, g1, wi)
    h0 = state_ssm[l].astype(F32).reshape(ns, N_STATE, 2)
    abr_s, hre, him = _s5_sample(rs["u"], h0[..., 0], h0[..., 1], sp, wglu, bglu, wso, ns, ts)
    onsa_s, new_win = _attn_sample(rs["q"].reshape(ns * ts, Q_W), rs["gn"].reshape(ns * ts, LANES),
                                   rs["kvs"].reshape(ns * ts, KV_W), rs["kvw"].reshape(ns * ts, KV_W),
                                   cache_cmp_kv[l], cache_slc_kv[l], cache_win_kv[l], page_table, cp, ns, ts, past_len)
    x1s, h2s = _post(xs, abr_s, "b", onsa_s, rs["ga"], rs["gb"], wno, wo, g2, lays, "b")
    y_sample = _moe(x1s, h2s, p_sample[l].reshape(ns, ts * PLE_DIM), mp, ns * ts, ts).reshape(ns, ts, D_MODEL)
    steps_first = lambda a: jnp.transpose(a.reshape((ts,) + kvt + (ns,)), (4, 0, 1, 2, 3))[None]
    new_cmp_s = steps_first(rs["kvct"])
    new_slc_s = steps_first(rs["kvst"])
    new_win_s = rows_last(new_win)
    new_ssm_s = jnp.stack([hre, him], axis=-1).reshape(1, ns, N_SSM_GROUPS, SSM_STATE, 2)
    return (y_prompt, y_sample, new_cmp_p, new_slc_p, new_win_p, new_ssm_p,
            new_cmp_s, new_slc_s, new_win_s, new_ssm_s)
```

```python
import functools
import math

import jax
import jax.numpy as jnp
import numpy as np
from jax import lax
from jax.experimental import pallas as pl
from jax.experimental.pallas import tpu as pltpu

F32 = jnp.float32
BF16 = jnp.bfloat16

D_MODEL = 1024
SSM_WIDTH = 512
SSM_GROUP = 16
N_SSM_GROUPS = 32
SSM_STATE = 64
HEAD_DIM = 64
N_Q_HEADS = 8
N_KV_HEADS = 2
GQA = 4
CMP_LEN = 32
CMP_STRIDE = 16
SLC_BLOCK = 64
TOP_N = 8
WINDOW = 512
Q_BLOCK = 256
NEG_INF = -1e30
FORCE_BONUS = 1e4
Q_W = 512
KV_W = 256
NSA_GATE_W = 24
N_EXPERT_GROUPS = 4
EXPERTS_PER_GROUP = 4
N_EXPERTS = 16
D_FF_EXPERT = 256
PLE_DIM = 256
RMS_EPS = 1e-6
PAGE_SIZE = 128

LANES = 128
SUBLANES = 8
N_STATE = N_SSM_GROUPS * SSM_STATE
MIB = 2 ** 20


def _cparams(sem, vmem_mib):
    return pltpu.CompilerParams(dimension_semantics=sem, vmem_limit_bytes=vmem_mib * MIB)


def _full(shape, single_buffer=False):
    nd = len(shape)
    if single_buffer:
        return pl.BlockSpec(shape, lambda *_: (0,) * nd, pipeline_mode=pl.Buffered(1))
    return pl.BlockSpec(shape, lambda *_: (0,) * nd)


def _prompt_layout(nseq, t, tm):
    nb = t // tm
    return {
        "grid": (nb, nseq), "tm": tm, "nseq": nseq, "t": t,
        "a": lambda w: ((nseq * t, w), pl.BlockSpec((tm, w), lambda b, s: (s * nb + b, 0))),
    }


def _sample_layout(nseq, t):
    return {
        "grid": (1, t), "tm": nseq,
        "a": lambda w: ((nseq, t * w), pl.BlockSpec((nseq, w), lambda s, b: (0, b))),
        "b": lambda w: ((t * nseq, w), pl.BlockSpec((nseq, w), lambda s, b: (b, 0))),
    }


TK_SLC = 512
TK_WIN = 128


Q_SCALE = HEAD_DIM ** -0.5 * math.log2(math.e)
C_U, C_Q, C_KVC, C_KVS, C_KVW = 0, 512, 1024, 1280, 1536
N_MAIN = 1792


def _dot_t(a, b):
    return lax.dot_general(a, b, (((1,), (1,)), ((), ())), preferred_element_type=F32)


GN_ROWS = 32


def _inproj_prompt_body(x_ref, g_ref, wa_ref, wgn_ref, wgab_ref,
                        u_ref, kvc_ref, ksb_ref, kwb_ref, ga_ref, gb_ref,
                        qt_ref, kvct_ref, kvst_ref, kvwt_ref, gnt_ref, vst_ref, vwt_ref, *, nseq):
    s = pl.program_id(1)
    x = x_ref[...]
    inv = lax.rsqrt(jnp.mean(x * x, axis=-1, keepdims=True) + RMS_EPS)
    h = (x * inv * g_ref[...]).astype(BF16)
    tm = h.shape[0]

    def mm(w):
        return jnp.dot(h, w, preferred_element_type=F32)

    u = mm(wa_ref[:, C_U:C_U + SSM_WIDTH])
    for j in range(SSM_WIDTH // LANES):
        u_ref[j, pl.ds(s, tm, stride=nseq), :] = u[:, LANES * j:LANES * (j + 1)]
    ga_ref[...] = jax.nn.sigmoid(mm(wgab_ref[:, 0:D_MODEL])).astype(BF16)
    gb_ref[...] = jax.nn.sigmoid(mm(wgab_ref[:, D_MODEL:2 * D_MODEL])).astype(BF16)
    kvc = mm(wa_ref[:, C_KVC:C_KVC + KV_W])
    kvc_ref[0] = kvc[:, 0:LANES]
    kvc_ref[1] = kvc[:, LANES:2 * LANES]
    kvct_ref[0] = kvc.T
    kvs = mm(wa_ref[:, C_KVS:C_KVS + KV_W])
    ksb_ref[...] = kvs[:, 0:LANES].astype(BF16)
    kvst = kvs.T
    kvst_ref[0] = kvst
    kvw = mm(wa_ref[:, C_KVW:C_KVW + KV_W])
    kwb_ref[...] = kvw[:, 0:LANES].astype(BF16)
    kvwt = kvw.T
    kvwt_ref[0] = kvwt
    for c in range(tm // TK_WIN):
        vst_ref[0, c] = kvst[LANES:2 * LANES, c * TK_WIN:(c + 1) * TK_WIN].astype(BF16)
        vwt_ref[0, c] = kvwt[LANES:2 * LANES, c * TK_WIN:(c + 1) * TK_WIN].astype(BF16)
    qt_ref[0] = (mm(wa_ref[:, C_Q:C_Q + Q_W]) * Q_SCALE).T.astype(BF16)
    gnt_ref[0] = jax.nn.sigmoid(mm(wgn_ref[...])).T[0:GN_ROWS]


def _inproj_prompt(x2d, lay, g, w):
    tm, nseq, t = lay["tm"], lay["nseq"], lay["t"]
    nb = t // tm
    out_shapes, out_specs, names = [], [], []

    def add(name, shape_spec, dt):
        names.append(name)
        out_shapes.append(jax.ShapeDtypeStruct(shape_spec[0], dt))
        out_specs.append(shape_spec[1])

    def tr(rows):
        return (nseq, rows, t), pl.BlockSpec((1, rows, tm), lambda b, s: (s, 0, b))

    nu = SSM_WIDTH // LANES
    add("u", ((nu, t * nseq, LANES), pl.BlockSpec((nu, tm * nseq, LANES), lambda b, s: (0, b, 0))), F32)
    add("kvc", ((2, nseq * t, LANES), pl.BlockSpec((2, tm, LANES), lambda b, s: (0, s * nb + b, 0))), F32)
    add("ksb", lay["a"](LANES), BF16)
    add("kwb", lay["a"](LANES), BF16)
    add("ga", lay["a"](D_MODEL), BF16)
    add("gb", lay["a"](D_MODEL), BF16)
    add("qt", tr(Q_W), BF16)
    add("kvct", tr(KV_W), F32)
    add("kvst", tr(KV_W), F32)
    add("kvwt", tr(KV_W), F32)
    add("gnt", tr(GN_ROWS), F32)
    for name in ("vst", "vwt"):
        add(name, ((nseq, t // TK_WIN, LANES, TK_WIN),
                   pl.BlockSpec((1, tm // TK_WIN, LANES, TK_WIN), lambda b, s: (s, b, 0, 0))), BF16)
    x_shape, x_spec = lay["a"](D_MODEL)
    ops = [g, w["wa"], w["wgn"], w["wgab"]]
    outs = pl.pallas_call(
        functools.partial(_inproj_prompt_body, nseq=nseq),
        grid=lay["grid"],
        in_specs=[x_spec] + [_full(o.shape) for o in ops],
        out_specs=out_specs,
        out_shape=out_shapes,
        compiler_params=_cparams(("arbitrary",) * 2, 56),
        name="inproj_prompt",
    )(x2d.reshape(x_shape), *ops)
    return dict(zip(names, outs))


def _inproj_sample_body(x_ref, g_ref, wa_ref, wgn_ref, wgab_ref,
                        u_ref, q_ref, kvs_ref, kvw_ref, gn_ref, ga_ref, gb_ref, kvct_ref, kvst_ref, kvwt_ref):
    x = x_ref[...]
    inv = lax.rsqrt(jnp.mean(x * x, axis=-1, keepdims=True) + RMS_EPS)
    h = (x * inv * g_ref[...]).astype(BF16)

    def mm(w):
        return jnp.dot(h, w, preferred_element_type=F32)

    u_ref[...] = mm(wa_ref[:, C_U:C_U + SSM_WIDTH])
    q_ref[...] = mm(wa_ref[:, C_Q:C_Q + Q_W]) * Q_SCALE
    kvs = mm(wa_ref[:, C_KVS:C_KVS + KV_W])
    kvs_ref[...] = kvs
    kvw = mm(wa_ref[:, C_KVW:C_KVW + KV_W])
    kvw_ref[...] = kvw
    gn_ref[...] = jax.nn.sigmoid(mm(wgn_ref[...]))
    ga_ref[...] = jax.nn.sigmoid(mm(wgab_ref[:, 0:D_MODEL])).astype(BF16)
    gb_ref[...] = jax.nn.sigmoid(mm(wgab_ref[:, D_MODEL:2 * D_MODEL])).astype(BF16)
    kvct_ref[0] = mm(wa_ref[:, C_KVC:C_KVC + KV_W]).T
    kvst_ref[0] = kvs.T
    kvwt_ref[0] = kvw.T


def _inproj_sample(x2d, lay, g, w):
    nseq = lay["tm"]
    ts = lay["grid"][1]
    names = ["u", "q", "kvs", "kvw", "gn", "ga", "gb"]
    widths = [SSM_WIDTH, Q_W, KV_W, KV_W, LANES, D_MODEL, D_MODEL]
    out_shapes, out_specs = [], []
    for n, wd in zip(names, widths):
        shp, spec = lay["b" if n == "u" else "a"](wd)
        out_shapes.append(jax.ShapeDtypeStruct(shp, BF16 if n in ("ga", "gb") else F32))
        out_specs.append(spec)
    for n in ("kvct", "kvst", "kvwt"):
        names.append(n)
        out_shapes.append(jax.ShapeDtypeStruct((ts, KV_W, nseq), F32))
        out_specs.append(pl.BlockSpec((1, KV_W, nseq), lambda s, b: (b, 0, 0)))
    x_shape, x_spec = lay["a"](D_MODEL)
    ops = [g, w["wa"], w["wgn"], w["wgab"]]
    outs = pl.pallas_call(
        _inproj_sample_body,
        grid=lay["grid"],
        in_specs=[x_spec] + [_full(o.shape) for o in ops],
        out_specs=out_specs,
        out_shape=out_shapes,
        compiler_params=_cparams(("arbitrary",) * 2, 56),
        name="inproj_sample",
    )(x2d.reshape(x_shape), *ops)
    return dict(zip(names, outs))


def _inproj_params(w_in0):
    return {"wa": w_in0[:, :N_MAIN].astype(BF16),
            "wgn": jnp.pad(w_in0[:, N_MAIN:N_MAIN + NSA_GATE_W], ((0, 0), (0, LANES - NSA_GATE_W))).astype(BF16),
            "wgab": w_in0[:, N_MAIN + NSA_GATE_W:].astype(BF16)}


def _s5_prompt_body(u_ref, wb_ref, ar_ref, ai_ref, cw_ref, d_ref, wglu_ref, bglu_ref, wso_ref,
                    abr_ref, hlast_ref, lhs_ref, bu_ref, h8_ref, p_ref, hstate_ref):
    c = pl.program_id(0)
    nseq = 4
    r4 = u_ref.shape[1]
    tc = r4 // nseq
    half = tc // 2

    @pl.when(c == 0)
    def _():
        hstate_ref[...] = jnp.zeros_like(hstate_ref)

    u = jnp.concatenate([u_ref[j] for j in range(SSM_WIDTH // LANES)], axis=1)
    row2 = lax.broadcasted_iota(jnp.int32, (r4, SSM_WIDTH), 0)
    lo2 = (row2 % SUBLANES) < nseq
    up = pltpu.roll(u, r4 - nseq, axis=0)
    dn = pltpu.roll(u, nseq, axis=0)
    swapped = jnp.where(lo2, up, dn)
    zero = jnp.zeros_like(u)
    ev_re = jnp.where(lo2, u, zero).astype(BF16).reshape(half, SUBLANES, SSM_WIDTH)
    ev_im = jnp.where(lo2, zero, swapped).astype(BF16).reshape(half, SUBLANES, SSM_WIDTH)
    od_re = jnp.where(lo2, swapped, zero).astype(BF16).reshape(half, SUBLANES, SSM_WIDTH)
    od_im = jnp.where(lo2, zero, u).astype(BF16).reshape(half, SUBLANES, SSM_WIDTH)
    for j in range(4):
        sl = slice(LANES * j, LANES * (j + 1))
        lhs_ref[:, 0:8, 256 * j:256 * j + LANES] = ev_re[:, :, sl]
        lhs_ref[:, 0:8, 256 * j + LANES:256 * (j + 1)] = ev_im[:, :, sl]
        lhs_ref[:, 8:16, 256 * j:256 * j + LANES] = od_re[:, :, sl]
        lhs_ref[:, 8:16, 256 * j + LANES:256 * (j + 1)] = od_im[:, :, sl]
    for j in range(4):
        lhs = lhs_ref[:, :, 256 * j:256 * (j + 1)].reshape(tc * SUBLANES, 256)
        bu_ref[:, 512 * j:512 * (j + 1)] = jnp.dot(lhs, wb_ref[j], preferred_element_type=F32)

    for lc in range(4):
        sl = slice(512 * lc, 512 * (lc + 1))
        ar = ar_ref[:, sl]
        ai = ai_ref[:, sl]

        def step(t, h, sl=sl, ar=ar, ai=ai):
            r0 = pl.multiple_of(t * SUBLANES, SUBLANES)
            h = ar * h + ai * pltpu.roll(h, nseq, axis=0) + bu_ref[pl.ds(r0, SUBLANES), sl]
            h8_ref[pl.ds(r0, SUBLANES), sl] = h
            return h

        hstate_ref[:, sl] = lax.fori_loop(0, tc, step, hstate_ref[:, sl], unroll=8)
    hlast_ref[...] = hstate_ref[...]

    for j in range(4):
        pj = jnp.dot(h8_ref[:, 512 * j:512 * (j + 1)].astype(BF16), cw_ref[j], preferred_element_type=F32)
        p_ref[2 * j] = pj[:, 0:LANES]
        p_ref[2 * j + 1] = pj[:, LANES:2 * LANES]
    ys = []
    for s in range(nseq):
        parts = []
        for j in range(4):
            re = p_ref[2 * j, pl.ds(s, tc, stride=SUBLANES), :]
            im = p_ref[2 * j + 1, pl.ds(nseq + s, tc, stride=SUBLANES), :]
            us = u_ref[j, pl.ds(s, tc, stride=nseq), :]
            parts.append(re + im + d_ref[:, LANES * j:LANES * (j + 1)] * us)
        ys.append(jnp.concatenate(parts, axis=1))
    y = jnp.concatenate(ys, axis=0)
    zg = jax.nn.gelu(y)
    gate = jnp.dot(zg.astype(BF16), wglu_ref[...], preferred_element_type=F32) + bglu_ref[...]
    glu = (zg * jax.nn.sigmoid(gate)).astype(BF16)
    abr = jnp.dot(glu, wso_ref[...], preferred_element_type=F32)
    for s in range(nseq):
        abr_ref[s] = abr[s * tc:(s + 1) * tc].astype(BF16)


def _s5_prompt(u_ts, sp, wglu, bglu, wso, t_total, tc):
    nseq = 4
    grid = (t_total // tc,)
    abr, hlast = pl.pallas_call(
        _s5_prompt_body,
        grid=grid,
        in_specs=[pl.BlockSpec((SSM_WIDTH // LANES, tc * nseq, LANES), lambda c: (0, c, 0)),
                  _full(sp["wb8"].shape), _full(sp["ar8"].shape), _full(sp["ai8"].shape), _full(sp["cw8"].shape),
                  _full(sp["d"].shape), _full(wglu.shape), _full(bglu.shape), _full(wso.shape)],
        out_specs=[pl.BlockSpec((nseq, tc, D_MODEL), lambda c: (0, c, 0)),
                   pl.BlockSpec((SUBLANES, N_STATE), lambda c: (0, 0))],
        out_shape=[jax.ShapeDtypeStruct((nseq, t_total, D_MODEL), BF16),
                   jax.ShapeDtypeStruct((SUBLANES, N_STATE), F32)],
        scratch_shapes=[pltpu.VMEM((tc // 2, 2 * SUBLANES, 1024), BF16),
                        pltpu.VMEM((tc * SUBLANES, N_STATE), F32),
                        pltpu.VMEM((tc * SUBLANES, N_STATE), F32),
                        pltpu.VMEM((8, tc * SUBLANES, LANES), F32),
                        pltpu.VMEM((SUBLANES, N_STATE), F32)],
        compiler_params=_cparams(("arbitrary",), 56),
        name="s5_prompt",
    )(u_ts, sp["wb8"], sp["ar8"], sp["ai8"], sp["cw8"], sp["d"], wglu, bglu, wso)
    return abr, hlast


def _s5_params(lam_re, lam_im, log_dt, b_re, b_im, c_re, c_im, d_skip):
    lam = lax.complex(lam_re.astype(F32), lam_im.astype(F32))
    dt = jnp.exp(log_dt.astype(F32))[:, None]
    a_bar = jnp.exp(lam * dt)
    b = lax.complex(b_re.astype(F32), b_im.astype(F32))
    b_bar = ((a_bar - 1.0) / lam)[..., None] * b
    eye8 = jnp.eye(8, dtype=F32)

    def bd_b(m):
        return jnp.einsum("ab,jbpc->jacbp", eye8, m.reshape(4, 8, SSM_STATE, SSM_GROUP)).reshape(4, 128, 512)

    def bd_c(m):
        return jnp.einsum("ab,jbcp->japbc", eye8, m.reshape(4, 8, SSM_GROUP, SSM_STATE)).reshape(4, 512, 128)

    wre, wim = bd_b(b_bar.real), bd_b(b_bar.imag)
    cre, cim = bd_c(c_re.astype(F32)), bd_c(c_im.astype(F32))
    ar = a_bar.real.reshape(1, N_STATE)
    ai = a_bar.imag.reshape(1, N_STATE)
    sign = jnp.concatenate([-jnp.ones((4, 1), F32), jnp.ones((4, 1), F32)], axis=0)
    return {
        "wb8": jnp.concatenate([wre, wim], axis=1).astype(BF16),
        "cw8": jnp.concatenate([cre, -cim], axis=2).astype(BF16),
        "ar8": jnp.broadcast_to(ar, (SUBLANES, N_STATE)),
        "ai8": sign * ai,
        "wre": wre.astype(BF16), "wim": wim.astype(BF16),
        "cre": cre.astype(BF16), "cim": cim.astype(BF16),
        "ar": ar, "ai": ai,
        "d": d_skip.astype(F32).reshape(1, SSM_WIDTH),
    }


def _cmp_params(cmp_pe, cmp_w1, cmp_w2):
    eye2 = jnp.eye(2, dtype=F32)
    nhalf = CMP_LEN // CMP_STRIDE
    w1r = cmp_w1.astype(F32).reshape(2, nhalf, CMP_STRIDE, HEAD_DIM, HEAD_DIM)
    wk = jnp.einsum("kside,ph->kipdshe", w1r, eye2).reshape(2, CMP_STRIDE * LANES, nhalf * LANES)
    bk = jnp.einsum("kld,klde->ke", cmp_pe.astype(F32), cmp_w1.astype(F32), precision=lax.Precision.HIGHEST)
    w2k = jnp.einsum("kef,ph->kpehf", cmp_w2.astype(F32), eye2).reshape(2, LANES, LANES)
    return {"wk": wk.astype(BF16), "bk": jnp.tile(bk, (1, N_KV_HEADS)), "w2k": w2k.astype(BF16),
            "w2kt": jnp.swapaxes(w2k, 1, 2).astype(BF16)}


def _compress_hidden(tap, nch, kv, wk_ref, bk_ref):
    x = jnp.concatenate([tap(i).astype(BF16) for i in range(CMP_STRIDE)], axis=1)
    pp = jnp.dot(x, wk_ref[kv], preferred_element_type=F32)
    pre = pp[:, 0:LANES] + pltpu.roll(pp[:, LANES:2 * LANES], nch - 1, axis=0) + bk_ref[kv:kv + 1, :]
    return jax.nn.gelu(pre).astype(BF16)


def _compress_prompt_body(x_ref, wk_ref, bk_ref, w2k_ref, w2kt_ref, ck_ref, cvt_ref):
    nch = x_ref.shape[1] // CMP_STRIDE
    hid = [_compress_hidden(lambda i, kv=kv: x_ref[kv, pl.ds(i, nch, stride=CMP_STRIDE), :], nch, kv, wk_ref, bk_ref)
           for kv in range(2)]
    ck_ref[0] = jnp.dot(hid[0], w2k_ref[0], preferred_element_type=F32).astype(BF16)
    cvt_ref[0] = _dot_t(w2kt_ref[1], hid[1]).astype(BF16)


def _compress_prompt(kvc2, cp, nseq, t):
    nch = t // CMP_STRIDE
    return pl.pallas_call(
        _compress_prompt_body,
        grid=(nseq,),
        in_specs=[pl.BlockSpec((2, t, LANES), lambda n: (0, n, 0)),
                  _full(cp["wk"].shape), _full(cp["bk"].shape), _full(cp["w2k"].shape), _full(cp["w2kt"].shape)],
        out_specs=[pl.BlockSpec((1, nch, LANES), lambda n: (n, 0, 0)),
                   pl.BlockSpec((1, LANES, nch), lambda n: (n, 0, 0))],
        out_shape=[jax.ShapeDtypeStruct((nseq, nch, LANES), BF16),
                   jax.ShapeDtypeStruct((nseq, LANES, nch), BF16)],
        compiler_params=_cparams(("arbitrary",), 48),
        name="compress_prompt",
    )(kvc2, cp["wk"], cp["bk"], cp["w2k"], cp["w2kt"])


def _overlap_t(n_cmp_pad, n_slc_pad):
    j = np.arange(n_cmp_pad)[None, :]
    s = np.arange(n_slc_pad)[:, None]
    ov = (j * CMP_STRIDE <= s * SLC_BLOCK + SLC_BLOCK - 1) & (j * CMP_STRIDE + CMP_LEN - 1 >= s * SLC_BLOCK)
    return jnp.asarray(ov, dtype=BF16)


def _softmax_cols(s, valid):
    sm = jnp.where(valid, s, NEG_INF)
    mx = jnp.max(sm, axis=0, keepdims=True)
    e = jnp.where(valid, jnp.exp2(sm - mx), 0.0)
    l = jnp.sum(e, axis=0, keepdims=True)
    return e * (1.0 / jnp.maximum(l, 1e-30))


def _select_blocks(imp, blk, pos, nblk, axis=0):
    cur = pos // SLC_BLOCK
    forced = (blk == 0) | (blk == cur) | (blk == cur - 1)
    v = jnp.where(forced, imp + FORCE_BONUS, imp)
    v = jnp.where(blk * SLC_BLOCK <= pos, v, NEG_INF)
    v = jnp.where(blk < nblk, v, -3e38)
    blk_f = blk.astype(F32)
    neg = jnp.full(imp.shape, NEG_INF, F32)
    for _ in range(min(TOP_N, nblk)):
        mx = jnp.max(v, axis=axis, keepdims=True)
        first = jnp.min(jnp.where(v == mx, blk_f, float(imp.shape[axis])), axis=axis, keepdims=True)
        pick = blk_f == first
        neg = jnp.where(pick, 0.0, neg)
        v = jnp.where(pick, -3e38, v)
    return neg


CB = 2 * LANES


def _attn_prompt_body(q_ref, gn_ref, ck_ref, cvt_ref, ks_ref, vst_ref, kw_ref, vwt_ref, ovt_ref,
                      o_ref, kaug_ref, kwaug_ref, qaug_ref, acc_ref):
    i = pl.program_id(1)
    t = ks_ref.shape[1]
    nch = ck_ref.shape[1]
    nslc = t // SLC_BLOCK
    qb = Q_BLOCK
    ncol = N_Q_HEADS * qb
    ncb = ncol // CB
    q0 = i * qb
    one_row = 2 * LANES - HEAD_DIM
    hrows = [slice(HEAD_DIM * ((CB * cb // qb) // GQA), HEAD_DIM * ((CB * cb // qb) // GQA + 1)) for cb in range(ncb)]

    @pl.when(i == 0)
    def _():
        kaug_ref[:, 0:LANES] = ks_ref[0]
        blk = lax.broadcasted_iota(jnp.int32, (t, LANES), 0) // SLC_BLOCK
        col = lax.broadcasted_iota(jnp.int32, (t, LANES), 1)
        kaug_ref[:, LANES:2 * LANES] = jnp.where(blk == col, 1.0, 0.0).astype(BF16)
        padcol = lax.broadcasted_iota(jnp.int32, (WINDOW, 2 * LANES), 1)
        kwaug_ref[0:WINDOW, :] = jnp.where(padcol == one_row, NEG_INF, 0.0).astype(BF16)
        kwaug_ref[WINDOW:WINDOW + t, 0:LANES] = kw_ref[0]
        kwaug_ref[WINDOW:WINDOW + t, LANES:2 * LANES] = jnp.zeros((t, LANES), BF16)

    zeros64 = jnp.zeros((HEAD_DIM, qb), BF16)
    for j in range(N_Q_HEADS):
        dst = j // GQA
        qaug_ref[HEAD_DIM * dst:HEAD_DIM * (dst + 1), qb * j:qb * (j + 1)] = q_ref[0, HEAD_DIM * j:HEAD_DIM * (j + 1), :]
        qaug_ref[HEAD_DIM * (1 - dst):HEAD_DIM * (2 - dst), qb * j:qb * (j + 1)] = zeros64
    tail_row = lax.broadcasted_iota(jnp.int32, (HEAD_DIM, ncol), 0)
    qaug_ref[one_row:2 * LANES, :] = jnp.where(tail_row == 0, 1.0, 0.0).astype(BF16)

    qaug_ref[LANES:one_row, :] = jnp.zeros((one_row - LANES, ncol), BF16)
    q_blocks = qb // TK_WIN
    npiece = (WINDOW + qb) // TK_WIN
    kws = [kwaug_ref[pl.ds(pl.multiple_of(q0 + w * TK_WIN, TK_WIN), TK_WIN), :] for w in range(npiece)]
    vwt = jnp.concatenate([vwt_ref[0, jnp.maximum(i * q_blocks + w - WINDOW // TK_WIN, 0)] for w in range(npiece)],
                          axis=1)
    wrow = lax.broadcasted_iota(jnp.int32, (TK_WIN, CB), 0)
    wcol = lax.broadcasted_iota(jnp.int32, (TK_WIN, CB), 1) & (qb - 1)
    wbias = []
    for w in range(npiece):
        lo, hi = w * TK_WIN - WINDOW, w * TK_WIN - WINDOW + TK_WIN - 1
        if hi <= 0 and qb - 1 - lo < WINDOW:
            wbias.append(None)
        else:
            dlt = wcol - wrow - lo
            wbias.append(jnp.where((dlt >= 0) & (dlt < WINDOW), 0.0, NEG_INF))
    kw_all = jnp.concatenate(kws, axis=0)
    sws = [jnp.dot(kw_all, qaug_ref[:, CB * cb:CB * (cb + 1)], preferred_element_type=F32) for cb in range(ncb)]
    es, rls = [], []
    for s in sws:
        s = jnp.concatenate([s[TK_WIN * w:TK_WIN * (w + 1)] if b is None else s[TK_WIN * w:TK_WIN * (w + 1)] + b
                             for w, b in enumerate(wbias)], axis=0)
        e = jnp.exp2(s - jnp.max(s, axis=0, keepdims=True))
        es.append(e.astype(BF16))
        rls.append(1.0 / jnp.sum(e, axis=0, keepdims=True))
    ow = jnp.concatenate([jnp.dot(vwt[hrows[cb]], e, preferred_element_type=F32) * rl
                          for cb, (e, rl) in enumerate(zip(es, rls))], axis=1)

    pos_c = q0 + (lax.broadcasted_iota(jnp.int32, (nch, CB), 1) & (qb - 1))
    cvalid = lax.broadcasted_iota(jnp.int32, (nch, CB), 0) * CMP_STRIDE + (CMP_LEN - 1) <= pos_c
    scs = [jnp.dot(ck_ref[0], qaug_ref[0:LANES, CB * cb:CB * (cb + 1)], preferred_element_type=F32)
           for cb in range(ncb)]
    pcs = [_softmax_cols(sc, cvalid).astype(BF16) for sc in scs]
    oc = jnp.concatenate([jnp.dot(cvt_ref[0, hrows[cb], :], pc, preferred_element_type=F32)
                          for cb, pc in enumerate(pcs)], axis=1)
    imp = jnp.concatenate([jnp.dot(ovt_ref[...], pc, preferred_element_type=F32) for pc in pcs], axis=1)
    blk = lax.broadcasted_iota(jnp.int32, (nslc, qb), 0)
    pos_q = q0 + lax.broadcasted_iota(jnp.int32, (nslc, qb), 1)
    for h in range(N_KV_HEADS):
        v = imp[0:nslc, qb * GQA * h:qb * GQA * h + qb]
        for g in range(1, GQA):
            v = v + imp[0:nslc, qb * (GQA * h + g):qb * (GQA * h + g + 1)]
        neg = _select_blocks(v, blk, pos_q, nslc).astype(BF16)
        for g in range(GQA):
            j = GQA * h + g
            qaug_ref[LANES:LANES + nslc, qb * j:qb * (j + 1)] = neg

    brow = lax.broadcasted_iota(jnp.int32, (qb, CB), 0)
    bcol = lax.broadcasted_iota(jnp.int32, (qb, CB), 1) & (qb - 1)
    tri_lo = jnp.where(brow <= bcol, 0.0, NEG_INF)

    acc_ref[...] = jnp.zeros_like(acc_ref)

    def sel_tile(k0, nk, vt, carry, bias):
        m, l = carry
        ka = kaug_ref[pl.ds(k0, nk), :]
        css = [slice(CB * cb, CB * (cb + 1)) for cb in range(ncb)]
        ss = [jnp.dot(ka, qaug_ref[:, cs], preferred_element_type=F32) for cs in css]
        ms, ls, ps, alphas = [], [], [], []
        for cs, s in zip(css, ss):
            if bias is not None:
                s = s + bias
            mn = jnp.maximum(m[:, cs], jnp.max(s, axis=0, keepdims=True))
            alpha = jnp.exp2(m[:, cs] - mn)
            p = jnp.exp2(s - mn)
            ms.append(mn)
            ls.append(alpha * l[:, cs] + jnp.sum(p, axis=0, keepdims=True))
            ps.append(p.astype(BF16))
            alphas.append(alpha)
        pvs = [jnp.dot(vt[hrows[cb]], p, preferred_element_type=F32) for cb, p in enumerate(ps)]
        for cs, alpha, pv in zip(css, alphas, pvs):
            acc_ref[:, cs] = alpha * acc_ref[:, cs] + pv
        return jnp.concatenate(ms, axis=1), jnp.concatenate(ls, axis=1)

    def vt_blocks(ref, b0, n):
        return jnp.concatenate([ref[0, b0 + j] for j in range(n)], axis=1) if n > 1 else ref[0, b0]

    big_blocks = TK_SLC // TK_WIN

    def big_tile(kt, carry):
        return sel_tile(pl.multiple_of(kt * TK_SLC, TK_SLC), TK_SLC, vt_blocks(vst_ref, kt * big_blocks, big_blocks),
                        carry, None)

    def small_tile(kb, carry):
        return sel_tile(pl.multiple_of(kb * qb, qb), qb, vt_blocks(vst_ref, kb * q_blocks, q_blocks), carry, None)

    carry = (jnp.full((1, ncol), NEG_INF, F32), jnp.zeros((1, ncol), F32))
    nbig = q0 // TK_SLC
    carry = lax.fori_loop(0, nbig, big_tile, carry)
    carry = lax.fori_loop(nbig * (TK_SLC // qb), i, small_tile, carry)
    _, l = sel_tile(pl.multiple_of(q0, qb), qb, vt_blocks(vst_ref, i * q_blocks, q_blocks), carry, tri_lo)
    osel = acc_ref[...] * (1.0 / l)

    gt = gn_ref[0]
    for c in range(N_Q_HEADS // 2):
        rows = []
        for hh in range(2):
            j = 2 * c + hh
            cs = slice(qb * j, qb * (j + 1))
            rows.append(gt[3 * j:3 * j + 1, :] * oc[:, cs] + gt[3 * j + 1:3 * j + 2, :] * osel[:, cs]
                        + gt[3 * j + 2:3 * j + 3, :] * ow[:, cs])
        o_ref[:, LANES * c:LANES * (c + 1)] = jnp.concatenate(rows, axis=0).T.astype(o_ref.dtype)


def _attn_prompt(q, gn, ck, cvt, ksb, vst, kwb, vwt, nseq, t):
    nb = t // Q_BLOCK
    nch = t // CMP_STRIDE
    nslc = t // SLC_BLOCK
    ovt = _overlap_t(nch, max(nslc, SUBLANES))
    row = lambda n, i: (n * nb + i, 0)
    seq3 = lambda n, i: (n, 0, 0)
    seq4 = lambda n, i: (n, 0, 0, 0)
    col3 = lambda n, i: (n, 0, i)
    return pl.pallas_call(
        _attn_prompt_body,
        grid=(nseq, nb),
        in_specs=[pl.BlockSpec((1, Q_W, Q_BLOCK), col3), pl.BlockSpec((1, gn.shape[1], Q_BLOCK), col3),
                  pl.BlockSpec((1, nch, LANES), seq3), pl.BlockSpec((1, LANES, nch), seq3),
                  pl.BlockSpec((1, t, LANES), seq3), pl.BlockSpec((1, t // TK_WIN, LANES, TK_WIN), seq4),
                  pl.BlockSpec((1, t, LANES), seq3), pl.BlockSpec((1, t // TK_WIN, LANES, TK_WIN), seq4),
                  _full(ovt.shape)],
        out_specs=pl.BlockSpec((Q_BLOCK, Q_W), row),
        out_shape=jax.ShapeDtypeStruct((nseq * t, Q_W), BF16),
        scratch_shapes=[pltpu.VMEM((t, 2 * LANES), BF16),
                        pltpu.VMEM((WINDOW + t, 2 * LANES), BF16),
                        pltpu.VMEM((2 * LANES, N_Q_HEADS * Q_BLOCK), BF16),
                        pltpu.VMEM((HEAD_DIM, N_Q_HEADS * Q_BLOCK), F32)],
        compiler_params=_cparams(("arbitrary", "arbitrary"), 56),
        name="attn_prompt",
    )(q, gn, ck, cvt, ksb.reshape(nseq, t, LANES), vst, kwb.reshape(nseq, t, LANES), vwt, ovt)


def _post_body(x_ref, abr_ref, on_ref, ga_ref, gb_ref, wno_ref, wo_ref, g2_ref, x1_ref, h2_ref):
    bbr = jnp.dot(on_ref[...].astype(BF16), wno_ref[...], preferred_element_type=F32)
    merged = ga_ref[...].astype(F32) * abr_ref[...].astype(F32) + gb_ref[...].astype(F32) * bbr
    x1 = x_ref[...] + jnp.dot(merged.astype(BF16), wo_ref[...], preferred_element_type=F32)
    x1_ref[...] = x1
    inv = lax.rsqrt(jnp.mean(x1 * x1, axis=-1, keepdims=True) + RMS_EPS)
    h2_ref[...] = (x1 * inv * g2_ref[...]).astype(BF16)


def _post(x2d, abr, abr_lay, onsa, ga, gb, wno, wo, g2, lay, out_lay):
    x_shape, x_spec = lay["a"](D_MODEL)
    abr_shape, abr_spec = lay[abr_lay](D_MODEL)
    on_shape, on_spec = lay["a"](Q_W)
    o_shape, o_spec = lay[out_lay](D_MODEL)
    return pl.pallas_call(
        _post_body,
        grid=lay["grid"],
        in_specs=[x_spec, abr_spec, on_spec, x_spec, x_spec, _full(wno.shape), _full(wo.shape), _full(g2.shape)],
        out_specs=[o_spec, o_spec],
        out_shape=[jax.ShapeDtypeStruct(o_shape, F32), jax.ShapeDtypeStruct(o_shape, BF16)],
        compiler_params=_cparams(("arbitrary",) * len(lay["grid"]), 48),
        name="post",
    )(x2d.reshape(x_shape), abr.reshape(abr_shape), onsa.reshape(on_shape), ga, gb, wno, wo, g2)


def _route(logits):
    lane = lax.broadcasted_iota(jnp.int32, logits.shape, 1).astype(F32)
    big = float(LANES)
    glog = jnp.where(lane < N_EXPERT_GROUPS, logits, -jnp.inf)
    gmax = jnp.max(glog, axis=1, keepdims=True)
    gsel = jnp.min(jnp.where(glog == gmax, lane, big), axis=1, keepdims=True)
    gw = 1.0 / jnp.sum(jnp.exp(glog - gmax), axis=1, keepdims=True)
    lo = N_EXPERT_GROUPS + EXPERTS_PER_GROUP * gsel
    el = jnp.where((lane >= lo) & (lane < lo + EXPERTS_PER_GROUP), logits, -jnp.inf)
    v1 = jnp.max(el, axis=1, keepdims=True)
    i1 = jnp.min(jnp.where(el == v1, lane, big), axis=1, keepdims=True)
    el2 = jnp.where(lane == i1, -jnp.inf, el)
    v2 = jnp.max(el2, axis=1, keepdims=True)
    i2 = jnp.min(jnp.where(el2 == v2, lane, big), axis=1, keepdims=True)
    e2 = jnp.exp(v2 - v1)
    w1 = gw / (1.0 + e2)
    return jnp.where(lane == i1, w1, 0.0) + jnp.where(lane == i2, w1 * e2, 0.0)


def _moe_body(x1_ref, h2_ref, p_ref, wr_ref, br_ref, wg_ref, wu_ref, wd_ref, wpg_ref, wp_ref, gf_ref,
              y_ref, acc_ref, comb_ref, *, tsplit):
    g = pl.program_id(1)
    h2 = h2_ref[...]

    @pl.when(g == 0)
    def _():
        logits = jnp.dot(h2, wr_ref[...], preferred_element_type=F32) + br_ref[...]
        comb_ref[...] = _route(logits)
        acc_ref[...] = jnp.zeros_like(acc_ref)

    comb = comb_ref[...]
    lane = lax.broadcasted_iota(jnp.int32, comb.shape, 1)
    acc = acc_ref[...]
    for k in range(EXPERTS_PER_GROUP):
        e_lane = N_EXPERT_GROUPS + EXPERTS_PER_GROUP * g + k
        ce = jnp.sum(jnp.where(lane == e_lane, comb, 0.0), axis=1, keepdims=True)
        a = jnp.dot(h2, wg_ref[k], preferred_element_type=F32)
        b = jnp.dot(h2, wu_ref[k], preferred_element_type=F32)
        act = (jax.nn.silu(a) * b * ce).astype(BF16)
        acc = acc + jnp.dot(act, wd_ref[k], preferred_element_type=F32)
    acc_ref[...] = acc

    @pl.when(g == N_EXPERT_GROUPS - 1)
    def _():
        x2 = x1_ref[...] + acc_ref[...]
        rows = x2.shape[0] // tsplit
        if tsplit == 1:
            p = p_ref[...]
        else:
            p = jnp.concatenate([p_ref[:, PLE_DIM * t:PLE_DIM * (t + 1)] for t in range(tsplit)], axis=0)
        gate = jax.nn.sigmoid(jnp.dot(x2.astype(BF16), wpg_ref[...], preferred_element_type=F32))
        x3 = x2 + gate * jnp.dot(p.astype(BF16), wp_ref[...], preferred_element_type=F32)
        inv = lax.rsqrt(jnp.mean(x3 * x3, axis=-1, keepdims=True) + RMS_EPS)
        y = x3 * inv * gf_ref[...]
        if tsplit == 1:
            y_ref[...] = y
        else:
            for t in range(tsplit):
                y_ref[:, D_MODEL * t:D_MODEL * (t + 1)] = y[rows * t:rows * (t + 1)]


def _moe(x1, h2, p, mp, tm, tsplit):
    rows = x1.shape[0]
    nrb = rows // tm
    rb = lambda r, g: (r, 0)
    grp = lambda r, g: (g, 0, 0)
    if tsplit == 1:
        p_spec = pl.BlockSpec((tm, PLE_DIM), rb)
        y_spec = pl.BlockSpec((tm, D_MODEL), rb)
        y_shape = (rows, D_MODEL)
    else:
        assert nrb == 1
        p_spec = _full(p.shape)
        y_shape = (rows // tsplit, tsplit * D_MODEL)
        y_spec = _full(y_shape)
    return pl.pallas_call(
        functools.partial(_moe_body, tsplit=tsplit),
        grid=(nrb, N_EXPERT_GROUPS),
        in_specs=[pl.BlockSpec((tm, D_MODEL), rb), pl.BlockSpec((tm, D_MODEL), rb), p_spec,
                  _full(mp["wr"].shape), _full(mp["br"].shape),
                  pl.BlockSpec((EXPERTS_PER_GROUP, D_MODEL, D_FF_EXPERT), grp),
                  pl.BlockSpec((EXPERTS_PER_GROUP, D_MODEL, D_FF_EXPERT), grp),
                  pl.BlockSpec((EXPERTS_PER_GROUP, D_FF_EXPERT, D_MODEL), grp),
                  _full(mp["wpg"].shape, True), _full(mp["wp"].shape, True), _full(mp["gf"].shape)],
        out_specs=y_spec,
        out_shape=jax.ShapeDtypeStruct(y_shape, F32),
        scratch_shapes=[pltpu.VMEM((tm, D_MODEL), F32), pltpu.VMEM((tm, LANES), F32)],
        compiler_params=_cparams(("arbitrary", "arbitrary"), 60),
        name="moe_ple",
    )(x1, h2, p, mp["wr"], mp["br"], mp["wg"], mp["wu"], mp["wd"], mp["wpg"], mp["wp"], mp["gf"])


def _s5_sample_body(u_ref, h0re_ref, h0im_ref, wre_ref, wim_ref, ar_ref, ai_ref, cre_ref, cim_ref, d_ref,
                    wglu_ref, bglu_ref, wso_ref, abr_ref, hre_out_ref, him_out_ref,
                    bure_ref, buim_ref, hre_ref, him_ref, *, nseq, nstep):
    u = u_ref[...]
    ub = u.astype(BF16)
    for j in range(4):
        lhs = ub[:, LANES * j:LANES * (j + 1)]
        bure_ref[:, 512 * j:512 * (j + 1)] = jnp.dot(lhs, wre_ref[j], preferred_element_type=F32)
        buim_ref[:, 512 * j:512 * (j + 1)] = jnp.dot(lhs, wim_ref[j], preferred_element_type=F32)
    for lc in range(4):
        sl = slice(512 * lc, 512 * (lc + 1))
        ar = jnp.broadcast_to(ar_ref[:, sl], (SUBLANES, 512))
        ai = jnp.broadcast_to(ai_ref[:, sl], (SUBLANES, 512))

        def body(rc, carry, sl=sl, ar=ar, ai=ai):
            r0 = pl.multiple_of(rc * SUBLANES, SUBLANES)
            hr = h0re_ref[pl.ds(r0, SUBLANES), sl]
            hi = h0im_ref[pl.ds(r0, SUBLANES), sl]
            for t in range(nstep):
                rr = pl.multiple_of(t * nseq + rc * SUBLANES, SUBLANES)
                hr, hi = (ar * hr - ai * hi + bure_ref[pl.ds(rr, SUBLANES), sl],
                          ar * hi + ai * hr + buim_ref[pl.ds(rr, SUBLANES), sl])
                hre_ref[pl.ds(rr, SUBLANES), sl] = hr
                him_ref[pl.ds(rr, SUBLANES), sl] = hi
            hre_out_ref[pl.ds(r0, SUBLANES), sl] = hr
            him_out_ref[pl.ds(r0, SUBLANES), sl] = hi
            return carry

        lax.fori_loop(0, nseq // SUBLANES, body, 0)
    parts = []
    for j in range(4):
        sl = slice(512 * j, 512 * (j + 1))
        parts.append(jnp.dot(hre_ref[:, sl].astype(BF16), cre_ref[j], preferred_element_type=F32)
                     - jnp.dot(him_ref[:, sl].astype(BF16), cim_ref[j], preferred_element_type=F32))
    y = jnp.concatenate(parts, axis=1) + d_ref[...] * u
    zg = jax.nn.gelu(y)
    gate = jnp.dot(zg.astype(BF16), wglu_ref[...], preferred_element_type=F32) + bglu_ref[...]
    glu = (zg * jax.nn.sigmoid(gate)).astype(BF16)
    abr_ref[...] = jnp.dot(glu, wso_ref[...], preferred_element_type=F32).astype(BF16)


def _s5_sample(u_ts, h0re, h0im, sp, wglu, bglu, wso, nseq, nstep):
    rows = nseq * nstep
    ops = [u_ts, h0re, h0im, sp["wre"], sp["wim"], sp["ar"], sp["ai"], sp["cre"], sp["cim"], sp["d"], wglu, bglu, wso]
    return pl.pallas_call(
        functools.partial(_s5_sample_body, nseq=nseq, nstep=nstep),
        grid=(1,),
        in_specs=[_full(o.shape) for o in ops],
        out_specs=[_full((rows, D_MODEL)), _full((nseq, N_STATE)), _full((nseq, N_STATE))],
        out_shape=[jax.ShapeDtypeStruct((rows, D_MODEL), BF16),
                   jax.ShapeDtypeStruct((nseq, N_STATE), F32), jax.ShapeDtypeStruct((nseq, N_STATE), F32)],
        scratch_shapes=[pltpu.VMEM((rows, N_STATE), F32) for _ in range(4)],
        compiler_params=_cparams(("arbitrary",), 56),
        name="s5_sample",
    )(*ops)


def _softmax_rows(s, valid):
    sm = jnp.where(valid, s, NEG_INF)
    mx = jnp.max(sm, axis=1, keepdims=True)
    e = jnp.where(valid, jnp.exp2(sm - mx), 0.0)
    l = jnp.sum(e, axis=1, keepdims=True)
    return e * (1.0 / jnp.maximum(l, 1e-30))


SAMPLE_SEQS_PER_STEP = 4
CMP_PITCH = 24


def _attn_sample_body(pt_ref, q_ref, gn_ref, nks_ref, nkw_ref, wint_ref, wk_ref, bk_ref, w2k_ref, ov_ref, e_ref,
                      cmp_hbm, slc_hbm, o_ref, nwint_ref, xrow_ref, pages_ref, sem_ref, *, npage, past_len, nsub, tq):
    n = pl.program_id(0)
    nsteps = pl.num_programs(0)
    slot = lax.rem(n, 2)
    nrow = N_Q_HEADS * tq
    nwin = wint_ref.shape[2]
    nslc = -(-(past_len + tq) // SLC_BLOCK)
    nch = past_len // CMP_STRIDE
    per_page = PAGE_SIZE // CMP_STRIDE

    def page_copy(step, into, c, s, p):
        src = (cmp_hbm, slc_hbm)[c]
        return pltpu.make_async_copy(src.at[pt_ref[step * nsub + s, p]],
                                     pages_ref.at[into, (c * nsub + s) * npage + p], sem_ref.at[into])

    def all_pages(step, into, op):
        for c in range(2):
            for s in range(nsub):
                for p in range(npage):
                    op(page_copy(step, into, c, s, p))

    @pl.when(n == 0)
    def _():
        all_pages(0, 0, lambda cp: cp.start())

    all_pages(n, slot, lambda cp: cp.wait())
    nxt = jnp.minimum(n + 1, nsteps - 1)
    all_pages(nxt, 1 - slot, lambda cp: cp.start())

    def page(c, s, p):
        return pages_ref.at[slot, (c * nsub + s) * npage + p]

    cmp = []
    for kv in range(2):
        for s in range(nsub):
            for p in range(npage):
                rows = page(0, s, p)[LANES * kv:LANES * (kv + 1), :].T
                for c in range(per_page):
                    r0 = CMP_PITCH * (per_page * p + c)
                    xrow_ref[s, r0:r0 + CMP_STRIDE, :] = rows[CMP_STRIDE * c:CMP_STRIDE * (c + 1)]
        x = jnp.concatenate(
            [jnp.concatenate([xrow_ref[s, pl.ds(i, nch, stride=CMP_PITCH), :] for i in range(CMP_STRIDE)], axis=1)
             for s in range(nsub)], axis=0)
        pp = jnp.dot(x.astype(BF16), wk_ref[kv], preferred_element_type=F32)
        pre = pp[:, 0:LANES] + pltpu.roll(pp[:, LANES:2 * LANES], nsub * nch - 1, axis=0) + bk_ref[kv:kv + 1, :]
        cmp.append(jnp.dot(jax.nn.gelu(pre).astype(BF16), w2k_ref[kv], preferred_element_type=F32).astype(BF16))
    cks = [cmp[0][nch * s:nch * (s + 1)] for s in range(nsub)]
    cvs = [cmp[1][nch * s:nch * (s + 1)] for s in range(nsub)]
    seqs = range(nsub)
    rcat = lambda parts: jnp.concatenate(parts, axis=0)

    lane_w = lax.broadcasted_iota(jnp.int32, (KV_W, LANES), 1)
    lane8 = lax.broadcasted_iota(jnp.int32, (tq, LANES), 1)
    nks_l, nkw_l, wint_l, qs_l = [], [], [], []
    for s in seqs:
        rows_s = slice(tq * s, tq * (s + 1))
        nks_l.append(jnp.concatenate([nks_ref[rows_s, :], jnp.zeros((LANES - tq, KV_W), F32)], axis=0))
        nkw = jnp.concatenate([nkw_ref[rows_s, :], jnp.zeros((LANES - tq, KV_W), F32)], axis=0)
        nkw_l.append(nkw)
        wint = wint_ref[s]
        wint_l.append(wint)
        shifted = pltpu.roll(wint, nwin - tq, axis=1)
        new_t = pltpu.roll(nkw.T, LANES - tq, axis=1)
        nwint_ref[s, :, 0:nwin - LANES] = shifted[:, 0:nwin - LANES]
        nwint_ref[s, :, nwin - LANES:nwin] = jnp.where(lane_w >= LANES - tq, new_t, shifted[:, nwin - LANES:nwin])
        q = q_ref[rows_s, :]
        qrows = []
        for j in range(N_Q_HEADS):
            chunk = q[:, LANES * (j // 2):LANES * (j // 2 + 1)]
            dst = j // GQA
            if (j % 2) != dst:
                chunk = pltpu.roll(chunk, HEAD_DIM, axis=1)
            keep = (lane8 < HEAD_DIM) if dst == 0 else (lane8 >= HEAD_DIM)
            qrows.append(jnp.where(keep, chunk, 0.0))
        qs_l.append(jnp.concatenate(qrows, axis=0).astype(BF16))

    rtot = nsub * nrow
    seq_rows = [slice(nrow * s, nrow * (s + 1)) for s in seqs]
    pos = past_len + (lax.broadcasted_iota(jnp.int32, (rtot, LANES), 0) & (tq - 1))
    lane = lax.broadcasted_iota(jnp.int32, (rtot, LANES), 1)

    sc = rcat([_dot_t(qs_l[s], cks[s]) for s in seqs])
    pc = _softmax_rows(sc, lane * CMP_STRIDE + (CMP_LEN - 1) <= pos).astype(BF16)
    oc = rcat([jnp.dot(pc[seq_rows[s]], cvs[s], preferred_element_type=F32) for s in seqs])
    imp = jnp.dot(pc, ov_ref[...], preferred_element_type=F32)
    vs = []
    for s in seqs:
        for h in range(N_KV_HEADS):
            r0 = nrow * s + tq * GQA * h
            v = imp[r0:r0 + tq]
            for g in range(1, GQA):
                v = v + imp[r0 + tq * g:r0 + tq * (g + 1)]
            vs.append(v)
    nsel = len(vs) * tq
    vt = rcat(vs + [jnp.zeros((LANES - nsel, LANES), F32)]).T
    nblk_pad = -(-nslc // SUBLANES) * SUBLANES
    blk_t = lax.broadcasted_iota(jnp.int32, (nblk_pad, LANES), 0)
    pos_t = past_len + (lax.broadcasted_iota(jnp.int32, (nblk_pad, LANES), 1) & (tq - 1))
    neg_t = _select_blocks(vt[0:nblk_pad], blk_t, pos_t, nslc, axis=0)
    neg = rcat([neg_t, jnp.zeros((LANES - nblk_pad, LANES), F32)]).T
    negsel = rcat([neg[tq * (N_KV_HEADS * s + j // GQA):tq * (N_KV_HEADS * s + j // GQA + 1)]
                   for s in seqs for j in range(N_Q_HEADS)])
    negsel_b = negsel.astype(BF16)

    new_blk = past_len // SLC_BLOCK
    ss_l = []
    for s in seqs:
        qaug = jnp.concatenate([qs_l[s], negsel_b[seq_rows[s]]], axis=1)
        parts = []
        for p in range(0, npage, 2):
            kt = jnp.concatenate([page(1, s, p)[0:LANES, :], page(1, s, p + 1)[0:LANES, :]], axis=1).astype(BF16)
            et = jnp.concatenate([e_ref[p], e_ref[p + 1]], axis=1)
            parts.append(jnp.dot(qaug, jnp.concatenate([kt, et], axis=0), preferred_element_type=F32))
        parts.append(_dot_t(qs_l[s], nks_l[s][:, 0:LANES].astype(BF16)) + negsel[seq_rows[s], new_blk:new_blk + 1])
        ss_l.append(jnp.concatenate(parts, axis=1))
    ss = rcat(ss_l)
    nkeys = ss.shape[1]
    kpos = lax.broadcasted_iota(jnp.int32, (rtot, nkeys), 1)
    pos_k = past_len + (lax.broadcasted_iota(jnp.int32, (rtot, nkeys), 0) & (tq - 1))
    ps = _softmax_rows(ss, kpos <= pos_k).astype(BF16)
    osel_l = []
    for s in seqs:
        psq = ps[seq_rows[s]]
        o = jnp.dot(psq[:, past_len:nkeys], nks_l[s][:, LANES:2 * LANES].astype(BF16), preferred_element_type=F32)
        for p in range(0, npage, 2):
            vtp = jnp.concatenate([page(1, s, p)[LANES:2 * LANES, :], page(1, s, p + 1)[LANES:2 * LANES, :]],
                                  axis=1).astype(BF16)
            o = o + _dot_t(psq[:, PAGE_SIZE * p:PAGE_SIZE * (p + 2)], vtp)
        osel_l.append(o)
    osel = rcat(osel_l)

    sw = rcat([jnp.concatenate([jnp.dot(qs_l[s], wint_l[s][0:LANES].astype(BF16), preferred_element_type=F32),
                                _dot_t(qs_l[s], nkw_l[s][:, 0:LANES].astype(BF16))], axis=1) for s in seqs])
    nw = sw.shape[1]
    widx = lax.broadcasted_iota(jnp.int32, (rtot, nw), 1)
    pos_w = past_len + (lax.broadcasted_iota(jnp.int32, (rtot, nw), 0) & (tq - 1))
    dlt = pos_w - (past_len - nwin + widx)
    pw = _softmax_rows(sw, (dlt >= 0) & (dlt < WINDOW) & (widx < nwin + tq)).astype(BF16)
    ow = rcat([_dot_t(pw[seq_rows[s], 0:nwin], wint_l[s][LANES:2 * LANES].astype(BF16))
               + jnp.dot(pw[seq_rows[s], nwin:nw], nkw_l[s][:, LANES:2 * LANES].astype(BF16),
                         preferred_element_type=F32) for s in seqs])

    for s in seqs:
        rows_s = slice(tq * s, tq * (s + 1))
        gn = gn_ref[rows_s, :]
        for c in range(N_Q_HEADS // 2):
            halves = []
            for hh in range(2):
                j = 2 * c + hh
                rs = slice(nrow * s + tq * j, nrow * s + tq * (j + 1))
                oj = (gn[:, 3 * j:3 * j + 1] * oc[rs] + gn[:, 3 * j + 1:3 * j + 2] * osel[rs]
                      + gn[:, 3 * j + 2:3 * j + 3] * ow[rs])
                if (j // GQA) != hh:
                    oj = pltpu.roll(oj, HEAD_DIM, axis=1)
                halves.append(oj)
            o_ref[rows_s, LANES * c:LANES * (c + 1)] = jnp.where(lane8 < HEAD_DIM, halves[0], halves[1])

    @pl.when(n == nsteps - 1)
    def _():
        all_pages(nxt, 1 - slot, lambda cp: cp.wait())


def _attn_sample(q, gn, nks, nkw, cache_cmp, cache_slc, cache_win, page_table, cp, nseq, tq, past_len):
    assert tq <= CMP_STRIDE and past_len % PAGE_SIZE == 0
    npage = past_len // PAGE_SIZE
    assert npage % 2 == 0 and PAGE_SIZE == LANES
    n_pool = cache_cmp.shape[0]
    nwin = cache_win.shape[1]
    chunks = past_len // CMP_STRIDE
    ov = _overlap_t(chunks, LANES).T
    key = np.arange(past_len).reshape(npage, 1, PAGE_SIZE)
    e = jnp.asarray(np.arange(LANES).reshape(1, LANES, 1) == key // SLC_BLOCK, dtype=BF16)
    to_t = lambda c: jnp.transpose(c, (0, 2, 3, 4, 1)).reshape(c.shape[0], KV_W, c.shape[1])
    cmp_t, slc_t, win_t = to_t(cache_cmp), to_t(cache_slc), to_t(cache_win)
    nsub = SAMPLE_SEQS_PER_STEP
    assert nseq % nsub == 0
    row = lambda n, pt: (n, 0)
    seq3 = lambda n, pt: (n, 0, 0)
    consts = [cp["wk"], cp["bk"], cp["w2k"], ov, e]
    in_specs = [pl.BlockSpec((nsub * tq, Q_W), row), pl.BlockSpec((nsub * tq, LANES), row),
                pl.BlockSpec((nsub * tq, KV_W), row), pl.BlockSpec((nsub * tq, KV_W), row),
                pl.BlockSpec((nsub, KV_W, nwin), seq3)]
    in_specs += [pl.BlockSpec(c.shape, (lambda nd: lambda n, pt: (0,) * nd)(c.ndim)) for c in consts]
    in_specs += [pl.BlockSpec(memory_space=pl.ANY)] * 2
    grid_spec = pltpu.PrefetchScalarGridSpec(
        num_scalar_prefetch=1,
        grid=(nseq // nsub,),
        in_specs=in_specs,
        out_specs=[pl.BlockSpec((nsub * tq, Q_W), row), pl.BlockSpec((nsub, KV_W, nwin), seq3)],
        scratch_shapes=[pltpu.VMEM((nsub, chunks * CMP_PITCH, LANES), F32),
                        pltpu.VMEM((2, 2 * nsub * npage, KV_W, PAGE_SIZE), F32),
                        pltpu.SemaphoreType.DMA((2,))],
    )
    return pl.pallas_call(
        functools.partial(_attn_sample_body, npage=npage, past_len=past_len, nsub=nsub, tq=tq),
        grid_spec=grid_spec,
        out_shape=[jax.ShapeDtypeStruct((nseq * tq, Q_W), F32), jax.ShapeDtypeStruct((nseq, KV_W, nwin), F32)],
        compiler_params=_cparams(("arbitrary",), 56),
        name="attn_sample",
    )(page_table, q, gn, nks, nkw, win_t, *consts, cmp_t, slc_t)


def _moe_params(w_rg, b_rg, w_re, b_re, w_gate, w_up, w_down, w_ple, w_ple_gate, gf):
    pad = LANES - N_EXPERT_GROUPS - N_EXPERTS
    return {"wr": jnp.pad(jnp.concatenate([w_rg, w_re], axis=1), ((0, 0), (0, pad))).astype(BF16),
            "br": jnp.pad(jnp.concatenate([b_rg, b_re]), (0, pad)).astype(F32).reshape(1, LANES),
            "wg": w_gate.astype(BF16), "wu": w_up.astype(BF16), "wd": w_down.astype(BF16),
            "wpg": w_ple_gate.astype(BF16), "wp": w_ple.astype(BF16), "gf": gf.astype(F32).reshape(1, D_MODEL)}


TM_PROMPT = 512
TM_MOE = 1024
TC_S5 = 128


def kernel(x_prompt, x_sample, p_prompt, p_sample, cache_cmp_kv, cache_slc_kv, cache_win_kv, state_ssm, page_table, norm1_g, w_in, ssm_lam_re, ssm_lam_im, ssm_log_dt, ssm_b_re, ssm_b_im, ssm_c_re, ssm_c_im, ssm_d, w_glu, b_glu, cmp_pe, cmp_w1, cmp_w2, w_ssm_out, w_nsa_out, w_o, norm2_g, w_route_group, b_route_group, w_route_expert, b_route_expert, w_exp_gate, w_exp_up, w_exp_down, w_ple, w_ple_gate, final_norm_g):
    assert w_in.shape[0] == 1, "one layer"
    l = 0
    nb, t = x_prompt.shape[:2]
    ns, ts = x_sample.shape[:2]
    past_len = page_table.shape[1] * PAGE_SIZE
    kvt = (2, N_KV_HEADS, HEAD_DIM)

    wi = _inproj_params(w_in[l])
    g1 = norm1_g[l].astype(F32).reshape(1, D_MODEL)
    g2 = norm2_g[l].astype(F32).reshape(1, D_MODEL)
    sp = _s5_params(ssm_lam_re[l], ssm_lam_im[l], ssm_log_dt[l], ssm_b_re[l], ssm_b_im[l], ssm_c_re[l], ssm_c_im[l],
                    ssm_d[l])
    cp = _cmp_params(cmp_pe[l], cmp_w1[l], cmp_w2[l])
    mp = _moe_params(w_route_group[l], b_route_group[l], w_route_expert[l], b_route_expert[l], w_exp_gate[l],
                     w_exp_up[l], w_exp_down[l], w_ple[l], w_ple_gate[l], final_norm_g)
    wglu = w_glu[l].astype(BF16)
    bglu = b_glu[l].astype(F32).reshape(1, SSM_WIDTH)
    wso = w_ssm_out[l].astype(BF16)
    wno = w_nsa_out[l].astype(BF16)
    wo = w_o[l].astype(BF16)

    lay = _prompt_layout(nb, t, TM_PROMPT)
    xp = x_prompt.reshape(nb * t, D_MODEL)
    r = _inproj_prompt(xp, lay, g1, wi)
    abr, hlast = _s5_prompt(r["u"], sp, wglu, bglu, wso, t, TC_S5)
    ck, cvt = _compress_prompt(r["kvc"], cp, nb, t)
    onsa = _attn_prompt(r["qt"], r["gnt"], ck, cvt, r["ksb"], r["vst"], r["kwb"], r["vwt"], nb, t)
    x1, h2 = _post(xp, abr, "a", onsa, r["ga"], r["gb"], wno, wo, g2, lay, "a")
    y_prompt = _moe(x1, h2, p_prompt[l].reshape(nb * t, PLE_DIM), mp, TM_MOE, 1).reshape(nb, t, D_MODEL)
    keep = min(WINDOW, t)

    def rows_last(a):
        return jnp.transpose(a.reshape((a.shape[0],) + kvt + (a.shape[2],)), (0, 4, 1, 2, 3))[None]

    new_cmp_p = rows_last(r["kvct"])
    new_slc_p = rows_last(r["kvst"])
    new_win_p = rows_last(r["kvwt"][:, :, t - keep:])
    new_ssm_p = jnp.stack([hlast[0:nb], hlast[nb:2 * nb]], axis=-1).reshape(1, nb, N_SSM_GROUPS, SSM_STATE, 2)

    lays = _sample_layout(ns, ts)
    xs = x_sample.reshape(ns * ts, D_MODEL)
    rs = _inproj_sample(xs, lays, g1, wi)
    h0 = state_ssm[l].astype(F32).reshape(ns, N_STATE, 2)
    abr_s, hre, him = _s5_sample(rs["u"], h0[..., 0], h0[..., 1], sp, wglu, bglu, wso, ns, ts)
    onsa_s, new_win = _attn_sample(rs["q"].reshape(ns * ts, Q_W), rs["gn"].reshape(ns * ts, LANES),
                                   rs["kvs"].reshape(ns * ts, KV_W), rs["kvw"].reshape(ns * ts, KV_W),
                                   cache_cmp_kv[l], cache_slc_kv[l], cache_win_kv[l], page_table, cp, ns, ts, past_len)
    x1s, h2s = _post(xs, abr_s, "b", onsa_s, rs["ga"], rs["gb"], wno, wo, g2, lays, "b")
    y_sample = _moe(x1s, h2s, p_sample[l].reshape(ns, ts * PLE_DIM), mp, ns * ts, ts).reshape(ns, ts, D_MODEL)
    steps_first = lambda a: jnp.transpose(a.reshape((ts,) + kvt + (ns,)), (4, 0, 1, 2, 3))[None]
    new_cmp_s = steps_first(rs["kvct"])
    new_slc_s = steps_first(rs["kvst"])
    new_win_s = rows_last(new_win)
    new_ssm_s = jnp.stack([hre, him], axis=-1).reshape(1, ns, N_SSM_GROUPS, SSM_STATE, 2)
    return (y_prompt, y_sample, new_cmp_p, new_slc_p, new_win_p, new_ssm_p,
            new_cmp_s, new_slc_s, new_win_s, new_ssm_s)
```

```python
import functools
import math

import jax
import jax.numpy as jnp
import numpy as np
from jax import lax
from jax.experimental import pallas as pl
from jax.experimental.pallas import tpu as pltpu

F32 = jnp.float32
BF16 = jnp.bfloat16

D_MODEL = 1024
SSM_WIDTH = 512
SSM_GROUP = 16
N_SSM_GROUPS = 32
SSM_STATE = 64
HEAD_DIM = 64
N_Q_HEADS = 8
N_KV_HEADS = 2
GQA = 4
CMP_LEN = 32
CMP_STRIDE = 16
SLC_BLOCK = 64
TOP_N = 8
WINDOW = 512
Q_BLOCK = 256
NEG_INF = -1e30
FORCE_BONUS = 1e4
Q_W = 512
KV_W = 256
NSA_GATE_W = 24
N_EXPERT_GROUPS = 4
EXPERTS_PER_GROUP = 4
N_EXPERTS = 16
D_FF_EXPERT = 256
PLE_DIM = 256
RMS_EPS = 1e-6
PAGE_SIZE = 128

LANES = 128
SUBLANES = 8
N_STATE = N_SSM_GROUPS * SSM_STATE
MIB = 2 ** 20
V7X_VMEM_MIB = 64
VMEM_SMALL_MIB = 48
VMEM_LARGE_MIB = 56
VMEM_MOE_MIB = V7X_VMEM_MIB - 4


def _cparams(sem, vmem_mib):
    return pltpu.CompilerParams(dimension_semantics=sem, vmem_limit_bytes=vmem_mib * MIB)


def _full(shape):
    nd = len(shape)
    return pl.BlockSpec(shape, lambda *_: (0,) * nd)


def _prompt_layout(nseq, t, tm):
    nb = t // tm
    return {
        "grid": (nb, nseq), "tm": tm, "nseq": nseq, "t": t,
        "a": lambda w: ((nseq * t, w), pl.BlockSpec((tm, w), lambda b, s: (s * nb + b, 0))),
    }


def _sample_layout(nseq, t):
    return {
        "grid": (1, t), "tm": nseq,
        "a": lambda w: ((nseq, t * w), pl.BlockSpec((nseq, w), lambda s, b: (0, b))),
        "b": lambda w: ((t * nseq, w), pl.BlockSpec((nseq, w), lambda s, b: (b, 0))),
    }


TK_SLC = 512
TK_WIN = 128


Q_SCALE = HEAD_DIM ** -0.5 * math.log2(math.e)
C_U, C_Q, C_KVC, C_KVS, C_KVW = 0, 512, 1024, 1280, 1536
N_MAIN = 1792


def _dot_t(a, b):
    return lax.dot_general(a, b, (((1,), (1,)), ((), ())), preferred_element_type=F32)


GN_ROWS = 32


def _inproj_prompt_body(x_ref, g_ref, wa_ref, wgn_ref, wgab_ref,
                        u_ref, kvc_ref, ksb_ref, kwb_ref, ga_ref, gb_ref,
                        qt_ref, kvct_ref, kvst_ref, kvwt_ref, gnt_ref, vst_ref, vwt_ref, *, nseq):
    s = pl.program_id(1)
    x = x_ref[...]
    inv = lax.rsqrt(jnp.mean(x * x, axis=-1, keepdims=True) + RMS_EPS)
    h = (x * inv * g_ref[...]).astype(BF16)
    tm = h.shape[0]

    def mm(w):
        return jnp.dot(h, w, preferred_element_type=F32)

    u = mm(wa_ref[:, C_U:C_U + SSM_WIDTH])
    for j in range(SSM_WIDTH // LANES):
        u_ref[j, pl.ds(s, tm, stride=nseq), :] = u[:, LANES * j:LANES * (j + 1)]
    ga_ref[...] = jax.nn.sigmoid(mm(wgab_ref[:, 0:D_MODEL])).astype(BF16)
    gb_ref[...] = jax.nn.sigmoid(mm(wgab_ref[:, D_MODEL:2 * D_MODEL])).astype(BF16)
    kvc = mm(wa_ref[:, C_KVC:C_KVC + KV_W])
    kvc_ref[0] = kvc[:, 0:LANES]
    kvc_ref[1] = kvc[:, LANES:2 * LANES]
    kvct_ref[0] = kvc.T
    kvs = mm(wa_ref[:, C_KVS:C_KVS + KV_W])
    ksb_ref[...] = kvs[:, 0:LANES].astype(BF16)
    kvst = kvs.T
    kvst_ref[0] = kvst
    kvw = mm(wa_ref[:, C_KVW:C_KVW + KV_W])
    kwb_ref[...] = kvw[:, 0:LANES].astype(BF16)
    kvwt = kvw.T
    kvwt_ref[0] = kvwt
    for c in range(tm // TK_WIN):
        vst_ref[0, c] = kvst[LANES:2 * LANES, c * TK_WIN:(c + 1) * TK_WIN].astype(BF16)
        vwt_ref[0, c] = kvwt[LANES:2 * LANES, c * TK_WIN:(c + 1) * TK_WIN].astype(BF16)
    qt_ref[0] = (mm(wa_ref[:, C_Q:C_Q + Q_W]) * Q_SCALE).T.astype(BF16)
    gnt_ref[0] = jax.nn.sigmoid(mm(wgn_ref[...])).T[0:GN_ROWS]


def _inproj_prompt(x2d, lay, g, w):
    tm, nseq, t = lay["tm"], lay["nseq"], lay["t"]
    nb = t // tm
    out_shapes, out_specs, names = [], [], []

    def add(name, shape_spec, dt):
        names.append(name)
        out_shapes.append(jax.ShapeDtypeStruct(shape_spec[0], dt))
        out_specs.append(shape_spec[1])

    def tr(rows):
        return (nseq, rows, t), pl.BlockSpec((1, rows, tm), lambda b, s: (s, 0, b))

    nu = SSM_WIDTH // LANES
    add("u", ((nu, t * nseq, LANES), pl.BlockSpec((nu, tm * nseq, LANES), lambda b, s: (0, b, 0))), F32)
    add("kvc", ((2, nseq * t, LANES), pl.BlockSpec((2, tm, LANES), lambda b, s: (0, s * nb + b, 0))), F32)
    add("ksb", lay["a"](LANES), BF16)
    add("kwb", lay["a"](LANES), BF16)
    add("ga", lay["a"](D_MODEL), BF16)
    add("gb", lay["a"](D_MODEL), BF16)
    add("qt", tr(Q_W), BF16)
    add("kvct", tr(KV_W), F32)
    add("kvst", tr(KV_W), F32)
    add("kvwt", tr(KV_W), F32)
    add("gnt", tr(GN_ROWS), F32)
    for name in ("vst", "vwt"):
        add(name, ((nseq, t // TK_WIN, LANES, TK_WIN),
                   pl.BlockSpec((1, tm // TK_WIN, LANES, TK_WIN), lambda b, s: (s, b, 0, 0))), BF16)
    x_shape, x_spec = lay["a"](D_MODEL)
    ops = [g, w["wa"], w["wgn"], w["wgab"]]
    outs = pl.pallas_call(
        functools.partial(_inproj_prompt_body, nseq=nseq),
        grid=lay["grid"],
        in_specs=[x_spec] + [_full(o.shape) for o in ops],
        out_specs=out_specs,
        out_shape=out_shapes,
        compiler_params=_cparams(("arbitrary",) * 2, VMEM_LARGE_MIB),
        name="inproj_prompt",
    )(x2d.reshape(x_shape), *ops)
    return dict(zip(names, outs))


def _inproj_sample_body(x_ref, g_ref, wa_ref, wgn_ref, wgab_ref,
                        u_ref, q_ref, kvs_ref, kvw_ref, gn_ref, ga_ref, gb_ref, kvct_ref, kvst_ref, kvwt_ref):
    x = x_ref[...]
    inv = lax.rsqrt(jnp.mean(x * x, axis=-1, keepdims=True) + RMS_EPS)
    h = (x * inv * g_ref[...]).astype(BF16)

    def mm(w):
        return jnp.dot(h, w, preferred_element_type=F32)

    u_ref[...] = mm(wa_ref[:, C_U:C_U + SSM_WIDTH])
    q_ref[...] = mm(wa_ref[:, C_Q:C_Q + Q_W]) * Q_SCALE
    kvs = mm(wa_ref[:, C_KVS:C_KVS + KV_W])
    kvs_ref[...] = kvs
    kvw = mm(wa_ref[:, C_KVW:C_KVW + KV_W])
    kvw_ref[...] = kvw
    gn_ref[...] = jax.nn.sigmoid(mm(wgn_ref[...]))
    ga_ref[...] = jax.nn.sigmoid(mm(wgab_ref[:, 0:D_MODEL])).astype(BF16)
    gb_ref[...] = jax.nn.sigmoid(mm(wgab_ref[:, D_MODEL:2 * D_MODEL])).astype(BF16)
    kvct_ref[0] = mm(wa_ref[:, C_KVC:C_KVC + KV_W]).T
    kvst_ref[0] = kvs.T
    kvwt_ref[0] = kvw.T


def _inproj_sample(x2d, lay, g, w):
    nseq = lay["tm"]
    ts = lay["grid"][1]
    names = ["u", "q", "kvs", "kvw", "gn", "ga", "gb"]
    widths = [SSM_WIDTH, Q_W, KV_W, KV_W, LANES, D_MODEL, D_MODEL]
    out_shapes, out_specs = [], []
    for n, wd in zip(names, widths):
        shp, spec = lay["b" if n == "u" else "a"](wd)
        out_shapes.append(jax.ShapeDtypeStruct(shp, BF16 if n in ("ga", "gb") else F32))
        out_specs.append(spec)
    for n in ("kvct", "kvst", "kvwt"):
        names.append(n)
        out_shapes.append(jax.ShapeDtypeStruct((ts, KV_W, nseq), F32))
        out_specs.append(pl.BlockSpec((1, KV_W, nseq), lambda s, b: (b, 0, 0)))
    x_shape, x_spec = lay["a"](D_MODEL)
    ops = [g, w["wa"], w["wgn"], w["wgab"]]
    outs = pl.pallas_call(
        _inproj_sample_body,
        grid=lay["grid"],
        in_specs=[x_spec] + [_full(o.shape) for o in ops],
        out_specs=out_specs,
        out_shape=out_shapes,
        compiler_params=_cparams(("arbitrary",) * 2, VMEM_LARGE_MIB),
        name="inproj_sample",
    )(x2d.reshape(x_shape), *ops)
    return dict(zip(names, outs))


def _inproj_params(w_in0):
    return {"wa": w_in0[:, :N_MAIN].astype(BF16),
            "wgn": jnp.pad(w_in0[:, N_MAIN:N_MAIN + NSA_GATE_W], ((0, 0), (0, LANES - NSA_GATE_W))).astype(BF16),
            "wgab": w_in0[:, N_MAIN + NSA_GATE_W:].astype(BF16)}


def _s5_prompt_body(u_ref, wb_ref, ar_ref, ai_ref, cw_ref, d_ref, wglu_ref, bglu_ref, wso_ref,
                    abr_ref, hlast_ref, lhs_ref, bu_ref, h8_ref, p_ref, hstate_ref):
    c = pl.program_id(0)
    nseq = 4
    r4 = u_ref.shape[1]
    tc = r4 // nseq
    half = tc // 2

    @pl.when(c == 0)
    def _():
        hstate_ref[...] = jnp.zeros_like(hstate_ref)

    u = jnp.concatenate([u_ref[j] for j in range(SSM_WIDTH // LANES)], axis=1)
    row2 = lax.broadcasted_iota(jnp.int32, (r4, SSM_WIDTH), 0)
    lo2 = (row2 % SUBLANES) < nseq
    up = pltpu.roll(u, r4 - nseq, axis=0)
    dn = pltpu.roll(u, nseq, axis=0)
    swapped = jnp.where(lo2, up, dn)
    zero = jnp.zeros_like(u)
    ev_re = jnp.where(lo2, u, zero).astype(BF16).reshape(half, SUBLANES, SSM_WIDTH)
    ev_im = jnp.where(lo2, zero, swapped).astype(BF16).reshape(half, SUBLANES, SSM_WIDTH)
    od_re = jnp.where(lo2, swapped, zero).astype(BF16).reshape(half, SUBLANES, SSM_WIDTH)
    od_im = jnp.where(lo2, zero, u).astype(BF16).reshape(half, SUBLANES, SSM_WIDTH)
    for j in range(4):
        sl = slice(LANES * j, LANES * (j + 1))
        lhs_ref[:, 0:8, 256 * j:256 * j + LANES] = ev_re[:, :, sl]
        lhs_ref[:, 0:8, 256 * j + LANES:256 * (j + 1)] = ev_im[:, :, sl]
        lhs_ref[:, 8:16, 256 * j:256 * j + LANES] = od_re[:, :, sl]
        lhs_ref[:, 8:16, 256 * j + LANES:256 * (j + 1)] = od_im[:, :, sl]
    for j in range(4):
        lhs = lhs_ref[:, :, 256 * j:256 * (j + 1)].reshape(tc * SUBLANES, 256)
        bu_ref[:, 512 * j:512 * (j + 1)] = jnp.dot(lhs, wb_ref[j], preferred_element_type=F32)

    for lc in range(4):
        sl = slice(512 * lc, 512 * (lc + 1))
        ar = ar_ref[:, sl]
        ai = ai_ref[:, sl]

        def step(t, h, sl=sl, ar=ar, ai=ai):
            r0 = pl.multiple_of(t * SUBLANES, SUBLANES)
            h = ar * h + ai * pltpu.roll(h, nseq, axis=0) + bu_ref[pl.ds(r0, SUBLANES), sl]
            h8_ref[pl.ds(r0, SUBLANES), sl] = h
            return h

        hstate_ref[:, sl] = lax.fori_loop(0, tc, step, hstate_ref[:, sl], unroll=8)
    hlast_ref[...] = hstate_ref[...]

    for j in range(4):
        pj = jnp.dot(h8_ref[:, 512 * j:512 * (j + 1)].astype(BF16), cw_ref[j], preferred_element_type=F32)
        p_ref[2 * j] = pj[:, 0:LANES]
        p_ref[2 * j + 1] = pj[:, LANES:2 * LANES]
    ys = []
    for s in range(nseq):
        parts = []
        for j in range(4):
            re = p_ref[2 * j, pl.ds(s, tc, stride=SUBLANES), :]
            im = p_ref[2 * j + 1, pl.ds(nseq + s, tc, stride=SUBLANES), :]
            us = u_ref[j, pl.ds(s, tc, stride=nseq), :]
            parts.append(re + im + d_ref[:, LANES * j:LANES * (j + 1)] * us)
        ys.append(jnp.concatenate(parts, axis=1))
    y = jnp.concatenate(ys, axis=0)
    zg = jax.nn.gelu(y)
    gate = jnp.dot(zg.astype(BF16), wglu_ref[...], preferred_element_type=F32) + bglu_ref[...]
    glu = (zg * jax.nn.sigmoid(gate)).astype(BF16)
    abr = jnp.dot(glu, wso_ref[...], preferred_element_type=F32)
    for s in range(nseq):
        abr_ref[s] = abr[s * tc:(s + 1) * tc].astype(BF16)


def _s5_prompt(u_ts, sp, wglu, bglu, wso, t_total, tc):
    nseq = 4
    grid = (t_total // tc,)
    abr, hlast = pl.pallas_call(
        _s5_prompt_body,
        grid=grid,
        in_specs=[pl.BlockSpec((SSM_WIDTH // LANES, tc * nseq, LANES), lambda c: (0, c, 0)),
                  _full(sp["wb8"].shape), _full(sp["ar8"].shape), _full(sp["ai8"].shape), _full(sp["cw8"].shape),
                  _full(sp["d"].shape), _full(wglu.shape), _full(bglu.shape), _full(wso.shape)],
        out_specs=[pl.BlockSpec((nseq, tc, D_MODEL), lambda c: (0, c, 0)),
                   pl.BlockSpec((SUBLANES, N_STATE), lambda c: (0, 0))],
        out_shape=[jax.ShapeDtypeStruct((nseq, t_total, D_MODEL), BF16),
                   jax.ShapeDtypeStruct((SUBLANES, N_STATE), F32)],
        scratch_shapes=[pltpu.VMEM((tc // 2, 2 * SUBLANES, 1024), BF16),
                        pltpu.VMEM((tc * SUBLANES, N_STATE), F32),
                        pltpu.VMEM((tc * SUBLANES, N_STATE), F32),
                        pltpu.VMEM((8, tc * SUBLANES, LANES), F32),
                        pltpu.VMEM((SUBLANES, N_STATE), F32)],
        compiler_params=_cparams(("arbitrary",), VMEM_LARGE_MIB),
        name="s5_prompt",
    )(u_ts, sp["wb8"], sp["ar8"], sp["ai8"], sp["cw8"], sp["d"], wglu, bglu, wso)
    return abr, hlast


def _s5_params(lam_re, lam_im, log_dt, b_re, b_im, c_re, c_im, d_skip):
    lam = lax.complex(lam_re.astype(F32), lam_im.astype(F32))
    dt = jnp.exp(log_dt.astype(F32))[:, None]
    a_bar = jnp.exp(lam * dt)
    b = lax.complex(b_re.astype(F32), b_im.astype(F32))
    b_bar = ((a_bar - 1.0) / lam)[..., None] * b
    eye8 = jnp.eye(8, dtype=F32)

    def bd_b(m):
        return jnp.einsum("ab,jbpc->jacbp", eye8, m.reshape(4, 8, SSM_STATE, SSM_GROUP)).reshape(4, 128, 512)

    def bd_c(m):
        return jnp.einsum("ab,jbcp->japbc", eye8, m.reshape(4, 8, SSM_GROUP, SSM_STATE)).reshape(4, 512, 128)

    wre, wim = bd_b(b_bar.real), bd_b(b_bar.imag)
    cre, cim = bd_c(c_re.astype(F32)), bd_c(c_im.astype(F32))
    ar = a_bar.real.reshape(1, N_STATE)
    ai = a_bar.imag.reshape(1, N_STATE)
    sign = jnp.concatenate([-jnp.ones((4, 1), F32), jnp.ones((4, 1), F32)], axis=0)
    return {
        "wb8": jnp.concatenate([wre, wim], axis=1).astype(BF16),
        "cw8": jnp.concatenate([cre, -cim], axis=2).astype(BF16),
        "ar8": jnp.broadcast_to(ar, (SUBLANES, N_STATE)),
        "ai8": sign * ai,
        "wre": wre.astype(BF16), "wim": wim.astype(BF16),
        "cre": cre.astype(BF16), "cim": cim.astype(BF16),
        "ar": ar, "ai": ai,
        "d": d_skip.astype(F32).reshape(1, SSM_WIDTH),
    }


def _cmp_params(cmp_pe, cmp_w1, cmp_w2):
    eye2 = jnp.eye(2, dtype=F32)
    nhalf = CMP_LEN // CMP_STRIDE
    w1r = cmp_w1.astype(F32).reshape(2, nhalf, CMP_STRIDE, HEAD_DIM, HEAD_DIM)
    wk = jnp.einsum("kside,ph->kipdshe", w1r, eye2).reshape(2, CMP_STRIDE * LANES, nhalf * LANES)
    bk = jnp.einsum("kld,klde->ke", cmp_pe.astype(F32), cmp_w1.astype(F32), precision=lax.Precision.HIGHEST)
    w2k = jnp.einsum("kef,ph->kpehf", cmp_w2.astype(F32), eye2).reshape(2, LANES, LANES)
    return {"wk": wk.astype(BF16), "bk": jnp.tile(bk, (1, N_KV_HEADS)), "w2k": w2k.astype(BF16),
            "w2kt": jnp.swapaxes(w2k, 1, 2).astype(BF16)}


def _compress_hidden(tap, nch, kv, wk_ref, bk_ref):
    x = jnp.concatenate([tap(i).astype(BF16) for i in range(CMP_STRIDE)], axis=1)
    pp = jnp.dot(x, wk_ref[kv], preferred_element_type=F32)
    pre = pp[:, 0:LANES] + pltpu.roll(pp[:, LANES:2 * LANES], nch - 1, axis=0) + bk_ref[kv:kv + 1, :]
    return jax.nn.gelu(pre).astype(BF16)


def _compress_prompt_body(x_ref, wk_ref, bk_ref, w2k_ref, w2kt_ref, ck_ref, cvt_ref):
    nch = x_ref.shape[1] // CMP_STRIDE
    hid = [_compress_hidden(lambda i, kv=kv: x_ref[kv, pl.ds(i, nch, stride=CMP_STRIDE), :], nch, kv, wk_ref, bk_ref)
           for kv in range(2)]
    ck_ref[0] = jnp.dot(hid[0], w2k_ref[0], preferred_element_type=F32).astype(BF16)
    cvt_ref[0] = _dot_t(w2kt_ref[1], hid[1]).astype(BF16)


def _compress_prompt(kvc2, cp, nseq, t):
    nch = t // CMP_STRIDE
    return pl.pallas_call(
        _compress_prompt_body,
        grid=(nseq,),
        in_specs=[pl.BlockSpec((2, t, LANES), lambda n: (0, n, 0)),
                  _full(cp["wk"].shape), _full(cp["bk"].shape), _full(cp["w2k"].shape), _full(cp["w2kt"].shape)],
        out_specs=[pl.BlockSpec((1, nch, LANES), lambda n: (n, 0, 0)),
                   pl.BlockSpec((1, LANES, nch), lambda n: (n, 0, 0))],
        out_shape=[jax.ShapeDtypeStruct((nseq, nch, LANES), BF16),
                   jax.ShapeDtypeStruct((nseq, LANES, nch), BF16)],
        compiler_params=_cparams(("arbitrary",), VMEM_SMALL_MIB),
        name="compress_prompt",
    )(kvc2, cp["wk"], cp["bk"], cp["w2k"], cp["w2kt"])


def _overlap_t(n_cmp_pad, n_slc_pad):
    j = np.arange(n_cmp_pad)[None, :]
    s = np.arange(n_slc_pad)[:, None]
    ov = (j * CMP_STRIDE <= s * SLC_BLOCK + SLC_BLOCK - 1) & (j * CMP_STRIDE + CMP_LEN - 1 >= s * SLC_BLOCK)
    return jnp.asarray(ov, dtype=BF16)


def _softmax_cols(s, valid):
    sm = jnp.where(valid, s, NEG_INF)
    mx = jnp.max(sm, axis=0, keepdims=True)
    e = jnp.where(valid, jnp.exp2(sm - mx), 0.0)
    l = jnp.sum(e, axis=0, keepdims=True)
    return e * (1.0 / jnp.maximum(l, 1e-30))


def _select_blocks(imp, blk, pos, nblk, axis=0):
    cur = pos // SLC_BLOCK
    forced = (blk == 0) | (blk == cur) | (blk == cur - 1)
    v = jnp.where(forced, imp + FORCE_BONUS, imp)
    v = jnp.where(blk * SLC_BLOCK <= pos, v, NEG_INF)
    v = jnp.where(blk < nblk, v, -3e38)
    blk_f = blk.astype(F32)
    neg = jnp.full(imp.shape, NEG_INF, F32)
    for _ in range(min(TOP_N, nblk)):
        mx = jnp.max(v, axis=axis, keepdims=True)
        first = jnp.min(jnp.where(v == mx, blk_f, float(imp.shape[axis])), axis=axis, keepdims=True)
        pick = blk_f == first
        neg = jnp.where(pick, 0.0, neg)
        v = jnp.where(pick, -3e38, v)
    return neg


CB = 2 * LANES


def _attn_prompt_body(q_ref, gn_ref, ck_ref, cvt_ref, ks_ref, vst_ref, kw_ref, vwt_ref, ovt_ref,
                      o_ref, kaug_ref, kwaug_ref, qaug_ref, acc_ref):
    i = pl.program_id(1)
    t = ks_ref.shape[1]
    nch = ck_ref.shape[1]
    nslc = t // SLC_BLOCK
    qb = Q_BLOCK
    ncol = N_Q_HEADS * qb
    ncb = ncol // CB
    q0 = i * qb
    one_row = 2 * LANES - HEAD_DIM
    hrows = [slice(HEAD_DIM * ((CB * cb // qb) // GQA), HEAD_DIM * ((CB * cb // qb) // GQA + 1)) for cb in range(ncb)]

    @pl.when(i == 0)
    def _():
        kaug_ref[:, 0:LANES] = ks_ref[0]
        blk = lax.broadcasted_iota(jnp.int32, (t, LANES), 0) // SLC_BLOCK
        col = lax.broadcasted_iota(jnp.int32, (t, LANES), 1)
        kaug_ref[:, LANES:2 * LANES] = jnp.where(blk == col, 1.0, 0.0).astype(BF16)
        padcol = lax.broadcasted_iota(jnp.int32, (WINDOW, 2 * LANES), 1)
        kwaug_ref[0:WINDOW, :] = jnp.where(padcol == one_row, NEG_INF, 0.0).astype(BF16)
        kwaug_ref[WINDOW:WINDOW + t, 0:LANES] = kw_ref[0]
        kwaug_ref[WINDOW:WINDOW + t, LANES:2 * LANES] = jnp.zeros((t, LANES), BF16)

    zeros64 = jnp.zeros((HEAD_DIM, qb), BF16)
    for j in range(N_Q_HEADS):
        dst = j // GQA
        qaug_ref[HEAD_DIM * dst:HEAD_DIM * (dst + 1), qb * j:qb * (j + 1)] = q_ref[0, HEAD_DIM * j:HEAD_DIM * (j + 1), :]
        qaug_ref[HEAD_DIM * (1 - dst):HEAD_DIM * (2 - dst), qb * j:qb * (j + 1)] = zeros64
    tail_row = lax.broadcasted_iota(jnp.int32, (HEAD_DIM, ncol), 0)
    qaug_ref[one_row:2 * LANES, :] = jnp.where(tail_row == 0, 1.0, 0.0).astype(BF16)

    qaug_ref[LANES:one_row, :] = jnp.zeros((one_row - LANES, ncol), BF16)
    q_blocks = qb // TK_WIN
    npiece = (WINDOW + qb) // TK_WIN
    kws = [kwaug_ref[pl.ds(pl.multiple_of(q0 + w * TK_WIN, TK_WIN), TK_WIN), :] for w in range(npiece)]
    vwt = jnp.concatenate([vwt_ref[0, jnp.maximum(i * q_blocks + w - WINDOW // TK_WIN, 0)] for w in range(npiece)],
                          axis=1)
    wrow = lax.broadcasted_iota(jnp.int32, (TK_WIN, CB), 0)
    wcol = lax.broadcasted_iota(jnp.int32, (TK_WIN, CB), 1) & (qb - 1)
    wbias = []
    for w in range(npiece):
        lo, hi = w * TK_WIN - WINDOW, w * TK_WIN - WINDOW + TK_WIN - 1
        if hi <= 0 and qb - 1 - lo < WINDOW:
            wbias.append(None)
        else:
            dlt = wcol - wrow - lo
            wbias.append(jnp.where((dlt >= 0) & (dlt < WINDOW), 0.0, NEG_INF))
    kw_all = jnp.concatenate(kws, axis=0)
    sws = [jnp.dot(kw_all, qaug_ref[:, CB * cb:CB * (cb + 1)], preferred_element_type=F32) for cb in range(ncb)]
    es, rls = [], []
    for s in sws:
        s = jnp.concatenate([s[TK_WIN * w:TK_WIN * (w + 1)] if b is None else s[TK_WIN * w:TK_WIN * (w + 1)] + b
                             for w, b in enumerate(wbias)], axis=0)
        e = jnp.exp2(s - jnp.max(s, axis=0, keepdims=True))
        es.append(e.astype(BF16))
        rls.append(1.0 / jnp.sum(e, axis=0, keepdims=True))
    ow = jnp.concatenate([jnp.dot(vwt[hrows[cb]], e, preferred_element_type=F32) * rl
                          for cb, (e, rl) in enumerate(zip(es, rls))], axis=1)

    pos_c = q0 + (lax.broadcasted_iota(jnp.int32, (nch, CB), 1) & (qb - 1))
    cvalid = lax.broadcasted_iota(jnp.int32, (nch, CB), 0) * CMP_STRIDE + (CMP_LEN - 1) <= pos_c
    scs = [jnp.dot(ck_ref[0], qaug_ref[0:LANES, CB * cb:CB * (cb + 1)], preferred_element_type=F32)
           for cb in range(ncb)]
    pcs = [_softmax_cols(sc, cvalid).astype(BF16) for sc in scs]
    oc = jnp.concatenate([jnp.dot(cvt_ref[0, hrows[cb], :], pc, preferred_element_type=F32)
                          for cb, pc in enumerate(pcs)], axis=1)
    imp = jnp.concatenate([jnp.dot(ovt_ref[...], pc, preferred_element_type=F32) for pc in pcs], axis=1)
    blk = lax.broadcasted_iota(jnp.int32, (nslc, qb), 0)
    pos_q = q0 + lax.broadcasted_iota(jnp.int32, (nslc, qb), 1)
    for h in range(N_KV_HEADS):
        v = imp[0:nslc, qb * GQA * h:qb * GQA * h + qb]
        for g in range(1, GQA):
            v = v + imp[0:nslc, qb * (GQA * h + g):qb * (GQA * h + g + 1)]
        neg = _select_blocks(v, blk, pos_q, nslc).astype(BF16)
        for g in range(GQA):
            j = GQA * h + g
            qaug_ref[LANES:LANES + nslc, qb * j:qb * (j + 1)] = neg

    brow = lax.broadcasted_iota(jnp.int32, (qb, CB), 0)
    bcol = lax.broadcasted_iota(jnp.int32, (qb, CB), 1) & (qb - 1)
    tri_lo = jnp.where(brow <= bcol, 0.0, NEG_INF)

    acc_ref[...] = jnp.zeros_like(acc_ref)

    def sel_tile(k0, nk, vt, carry, bias):
        m, l = carry
        ka = kaug_ref[pl.ds(k0, nk), :]
        css = [slice(CB * cb, CB * (cb + 1)) for cb in range(ncb)]
        ss = [jnp.dot(ka, qaug_ref[:, cs], preferred_element_type=F32) for cs in css]
        ms, ls, ps, alphas = [], [], [], []
        for cs, s in zip(css, ss):
            if bias is not None:
                s = s + bias
            mn = jnp.maximum(m[:, cs], jnp.max(s, axis=0, keepdims=True))
            alpha = jnp.exp2(m[:, cs] - mn)
            p = jnp.exp2(s - mn)
            ms.append(mn)
            ls.append(alpha * l[:, cs] + jnp.sum(p, axis=0, keepdims=True))
            ps.append(p.astype(BF16))
            alphas.append(alpha)
        pvs = [jnp.dot(vt[hrows[cb]], p, preferred_element_type=F32) for cb, p in enumerate(ps)]
        for cs, alpha, pv in zip(css, alphas, pvs):
            acc_ref[:, cs] = alpha * acc_ref[:, cs] + pv
        return jnp.concatenate(ms, axis=1), jnp.concatenate(ls, axis=1)

    def vt_blocks(ref, b0, n):
        return jnp.concatenate([ref[0, b0 + j] for j in range(n)], axis=1) if n > 1 else ref[0, b0]

    big_blocks = TK_SLC // TK_WIN

    def big_tile(kt, carry):
        return sel_tile(pl.multiple_of(kt * TK_SLC, TK_SLC), TK_SLC, vt_blocks(vst_ref, kt * big_blocks, big_blocks),
                        carry, None)

    def small_tile(kb, carry):
        return sel_tile(pl.multiple_of(kb * qb, qb), qb, vt_blocks(vst_ref, kb * q_blocks, q_blocks), carry, None)

    carry = (jnp.full((1, ncol), NEG_INF, F32), jnp.zeros((1, ncol), F32))
    nbig = q0 // TK_SLC
    carry = lax.fori_loop(0, nbig, big_tile, carry)
    carry = lax.fori_loop(nbig * (TK_SLC // qb), i, small_tile, carry)
    _, l = sel_tile(pl.multiple_of(q0, qb), qb, vt_blocks(vst_ref, i * q_blocks, q_blocks), carry, tri_lo)
    osel = acc_ref[...] * (1.0 / l)

    gt = gn_ref[0]
    for c in range(N_Q_HEADS // 2):
        rows = []
        for hh in range(2):
            j = 2 * c + hh
            cs = slice(qb * j, qb * (j + 1))
            rows.append(gt[3 * j:3 * j + 1, :] * oc[:, cs] + gt[3 * j + 1:3 * j + 2, :] * osel[:, cs]
                        + gt[3 * j + 2:3 * j + 3, :] * ow[:, cs])
        o_ref[:, LANES * c:LANES * (c + 1)] = jnp.concatenate(rows, axis=0).T.astype(o_ref.dtype)


def _attn_prompt(q, gn, ck, cvt, ksb, vst, kwb, vwt, nseq, t):
    nb = t // Q_BLOCK
    nch = t // CMP_STRIDE
    nslc = t // SLC_BLOCK
    ovt = _overlap_t(nch, max(nslc, SUBLANES))
    row = lambda n, i: (n * nb + i, 0)
    seq3 = lambda n, i: (n, 0, 0)
    seq4 = lambda n, i: (n, 0, 0, 0)
    col3 = lambda n, i: (n, 0, i)
    return pl.pallas_call(
        _attn_prompt_body,
        grid=(nseq, nb),
        in_specs=[pl.BlockSpec((1, Q_W, Q_BLOCK), col3), pl.BlockSpec((1, gn.shape[1], Q_BLOCK), col3),
                  pl.BlockSpec((1, nch, LANES), seq3), pl.BlockSpec((1, LANES, nch), seq3),
                  pl.BlockSpec((1, t, LANES), seq3), pl.BlockSpec((1, t // TK_WIN, LANES, TK_WIN), seq4),
                  pl.BlockSpec((1, t, LANES), seq3), pl.BlockSpec((1, t // TK_WIN, LANES, TK_WIN), seq4),
                  _full(ovt.shape)],
        out_specs=pl.BlockSpec((Q_BLOCK, Q_W), row),
        out_shape=jax.ShapeDtypeStruct((nseq * t, Q_W), BF16),
        scratch_shapes=[pltpu.VMEM((t, 2 * LANES), BF16),
                        pltpu.VMEM((WINDOW + t, 2 * LANES), BF16),
                        pltpu.VMEM((2 * LANES, N_Q_HEADS * Q_BLOCK), BF16),
                        pltpu.VMEM((HEAD_DIM, N_Q_HEADS * Q_BLOCK), F32)],
        compiler_params=_cparams(("arbitrary", "arbitrary"), VMEM_LARGE_MIB),
        name="attn_prompt",
    )(q, gn, ck, cvt, ksb.reshape(nseq, t, LANES), vst, kwb.reshape(nseq, t, LANES), vwt, ovt)


def _post_body(x_ref, abr_ref, on_ref, ga_ref, gb_ref, wno_ref, wo_ref, g2_ref, x1_ref, h2_ref):
    bbr = jnp.dot(on_ref[...].astype(BF16), wno_ref[...], preferred_element_type=F32)
    merged = ga_ref[...].astype(F32) * abr_ref[...].astype(F32) + gb_ref[...].astype(F32) * bbr
    x1 = x_ref[...] + jnp.dot(merged.astype(BF16), wo_ref[...], preferred_element_type=F32)
    x1_ref[...] = x1
    inv = lax.rsqrt(jnp.mean(x1 * x1, axis=-1, keepdims=True) + RMS_EPS)
    h2_ref[...] = (x1 * inv * g2_ref[...]).astype(BF16)


def _post(x2d, abr, abr_lay, onsa, ga, gb, wno, wo, g2, lay, out_lay):
    x_shape, x_spec = lay["a"](D_MODEL)
    abr_shape, abr_spec = lay[abr_lay](D_MODEL)
    on_shape, on_spec = lay["a"](Q_W)
    o_shape, o_spec = lay[out_lay](D_MODEL)
    return pl.pallas_call(
        _post_body,
        grid=lay["grid"],
        in_specs=[x_spec, abr_spec, on_spec, x_spec, x_spec, _full(wno.shape), _full(wo.shape), _full(g2.shape)],
        out_specs=[o_spec, o_spec],
        out_shape=[jax.ShapeDtypeStruct(o_shape, F32), jax.ShapeDtypeStruct(o_shape, BF16)],
        compiler_params=_cparams(("arbitrary",) * len(lay["grid"]), VMEM_SMALL_MIB),
        name="post",
    )(x2d.reshape(x_shape), abr.reshape(abr_shape), onsa.reshape(on_shape), ga, gb, wno, wo, g2)


def _route(logits):
    lane = lax.broadcasted_iota(jnp.int32, logits.shape, 1).astype(F32)
    big = float(LANES)
    glog = jnp.where(lane < N_EXPERT_GROUPS, logits, -jnp.inf)
    gmax = jnp.max(glog, axis=1, keepdims=True)
    gsel = jnp.min(jnp.where(glog == gmax, lane, big), axis=1, keepdims=True)
    gw = 1.0 / jnp.sum(jnp.exp(glog - gmax), axis=1, keepdims=True)
    lo = N_EXPERT_GROUPS + EXPERTS_PER_GROUP * gsel
    el = jnp.where((lane >= lo) & (lane < lo + EXPERTS_PER_GROUP), logits, -jnp.inf)
    v1 = jnp.max(el, axis=1, keepdims=True)
    i1 = jnp.min(jnp.where(el == v1, lane, big), axis=1, keepdims=True)
    el2 = jnp.where(lane == i1, -jnp.inf, el)
    v2 = jnp.max(el2, axis=1, keepdims=True)
    i2 = jnp.min(jnp.where(el2 == v2, lane, big), axis=1, keepdims=True)
    e2 = jnp.exp(v2 - v1)
    w1 = gw / (1.0 + e2)
    return jnp.where(lane == i1, w1, 0.0) + jnp.where(lane == i2, w1 * e2, 0.0)


def _moe_body(x1_ref, h2_ref, p_ref, wr_ref, br_ref, wg_ref, wu_ref, wd_ref, wpg_ref, wp_ref, gf_ref,
              y_ref, acc_ref, comb_ref, *, tsplit):
    g = pl.program_id(1)
    h2 = h2_ref[...]

    @pl.when(g == 0)
    def _():
        logits = jnp.dot(h2, wr_ref[...], preferred_element_type=F32) + br_ref[...]
        comb_ref[...] = _route(logits)
        acc_ref[...] = jnp.zeros_like(acc_ref)

    comb = comb_ref[...]
    lane = lax.broadcasted_iota(jnp.int32, comb.shape, 1)
    acc = acc_ref[...]
    for k in range(EXPERTS_PER_GROUP):
        e_lane = N_EXPERT_GROUPS + EXPERTS_PER_GROUP * g + k
        ce = jnp.sum(jnp.where(lane == e_lane, comb, 0.0), axis=1, keepdims=True)
        a = jnp.dot(h2, wg_ref[k], preferred_element_type=F32)
        b = jnp.dot(h2, wu_ref[k], preferred_element_type=F32)
        act = (jax.nn.silu(a) * b * ce).astype(BF16)
        acc = acc + jnp.dot(act, wd_ref[k], preferred_element_type=F32)
    acc_ref[...] = acc

    @pl.when(g == N_EXPERT_GROUPS - 1)
    def _():
        x2 = x1_ref[...] + acc_ref[...]
        rows = x2.shape[0] // tsplit
        if tsplit == 1:
            p = p_ref[...]
        else:
            p = jnp.concatenate([p_ref[:, PLE_DIM * t:PLE_DIM * (t + 1)] for t in range(tsplit)], axis=0)
        gate = jax.nn.sigmoid(jnp.dot(x2.astype(BF16), wpg_ref[...], preferred_element_type=F32))
        x3 = x2 + gate * jnp.dot(p.astype(BF16), wp_ref[...], preferred_element_type=F32)
        inv = lax.rsqrt(jnp.mean(x3 * x3, axis=-1, keepdims=True) + RMS_EPS)
        y = x3 * inv * gf_ref[...]
        if tsplit == 1:
            y_ref[...] = y
        else:
            for t in range(tsplit):
                y_ref[:, D_MODEL * t:D_MODEL * (t + 1)] = y[rows * t:rows * (t + 1)]


def _moe(x1, h2, p, mp, tm, tsplit):
    rows = x1.shape[0]
    nrb = rows // tm
    rb = lambda r, g: (r, 0)
    grp = lambda r, g: (g, 0, 0)
    if tsplit == 1:
        p_spec = pl.BlockSpec((tm, PLE_DIM), rb)
        y_spec = pl.BlockSpec((tm, D_MODEL), rb)
        y_shape = (rows, D_MODEL)
    else:
        assert nrb == 1
        p_spec = _full(p.shape)
        y_shape = (rows // tsplit, tsplit * D_MODEL)
        y_spec = _full(y_shape)
    return pl.pallas_call(
        functools.partial(_moe_body, tsplit=tsplit),
        grid=(nrb, N_EXPERT_GROUPS),
        in_specs=[pl.BlockSpec((tm, D_MODEL), rb), pl.BlockSpec((tm, D_MODEL), rb), p_spec,
                  _full(mp["wr"].shape), _full(mp["br"].shape),
                  pl.BlockSpec((EXPERTS_PER_GROUP, D_MODEL, D_FF_EXPERT), grp),
                  pl.BlockSpec((EXPERTS_PER_GROUP, D_MODEL, D_FF_EXPERT), grp),
                  pl.BlockSpec((EXPERTS_PER_GROUP, D_FF_EXPERT, D_MODEL), grp),
                  _full(mp["wpg"].shape), _full(mp["wp"].shape), _full(mp["gf"].shape)],
        out_specs=y_spec,
        out_shape=jax.ShapeDtypeStruct(y_shape, F32),
        scratch_shapes=[pltpu.VMEM((tm, D_MODEL), F32), pltpu.VMEM((tm, LANES), F32)],
        compiler_params=_cparams(("arbitrary", "arbitrary"), VMEM_MOE_MIB),
        name="moe_ple",
    )(x1, h2, p, mp["wr"], mp["br"], mp["wg"], mp["wu"], mp["wd"], mp["wpg"], mp["wp"], mp["gf"])


def _s5_sample_body(u_ref, h0re_ref, h0im_ref, wre_ref, wim_ref, ar_ref, ai_ref, cre_ref, cim_ref, d_ref,
                    wglu_ref, bglu_ref, wso_ref, abr_ref, hre_out_ref, him_out_ref,
                    bure_ref, buim_ref, hre_ref, him_ref, *, nseq, nstep):
    u = u_ref[...]
    ub = u.astype(BF16)
    for j in range(4):
        lhs = ub[:, LANES * j:LANES * (j + 1)]
        bure_ref[:, 512 * j:512 * (j + 1)] = jnp.dot(lhs, wre_ref[j], preferred_element_type=F32)
        buim_ref[:, 512 * j:512 * (j + 1)] = jnp.dot(lhs, wim_ref[j], preferred_element_type=F32)
    for lc in range(4):
        sl = slice(512 * lc, 512 * (lc + 1))
        ar = jnp.broadcast_to(ar_ref[:, sl], (SUBLANES, 512))
        ai = jnp.broadcast_to(ai_ref[:, sl], (SUBLANES, 512))

        def body(rc, carry, sl=sl, ar=ar, ai=ai):
            r0 = pl.multiple_of(rc * SUBLANES, SUBLANES)
            hr = h0re_ref[pl.ds(r0, SUBLANES), sl]
            hi = h0im_ref[pl.ds(r0, SUBLANES), sl]
            for t in range(nstep):
                rr = pl.multiple_of(t * nseq + rc * SUBLANES, SUBLANES)
                hr, hi = (ar * hr - ai * hi + bure_ref[pl.ds(rr, SUBLANES), sl],
                          ar * hi + ai * hr + buim_ref[pl.ds(rr, SUBLANES), sl])
                hre_ref[pl.ds(rr, SUBLANES), sl] = hr
                him_ref[pl.ds(rr, SUBLANES), sl] = hi
            hre_out_ref[pl.ds(r0, SUBLANES), sl] = hr
            him_out_ref[pl.ds(r0, SUBLANES), sl] = hi
            return carry

        lax.fori_loop(0, nseq // SUBLANES, body, 0)
    parts = []
    for j in range(4):
        sl = slice(512 * j, 512 * (j + 1))
        parts.append(jnp.dot(hre_ref[:, sl].astype(BF16), cre_ref[j], preferred_element_type=F32)
                     - jnp.dot(him_ref[:, sl].astype(BF16), cim_ref[j], preferred_element_type=F32))
    y = jnp.concatenate(parts, axis=1) + d_ref[...] * u
    zg = jax.nn.gelu(y)
    gate = jnp.dot(zg.astype(BF16), wglu_ref[...], preferred_element_type=F32) + bglu_ref[...]
    glu = (zg * jax.nn.sigmoid(gate)).astype(BF16)
    abr_ref[...] = jnp.dot(glu, wso_ref[...], preferred_element_type=F32).astype(BF16)


def _s5_sample(u_ts, h0re, h0im, sp, wglu, bglu, wso, nseq, nstep):
    rows = nseq * nstep
    ops = [u_ts, h0re, h0im, sp["wre"], sp["wim"], sp["ar"], sp["ai"], sp["cre"], sp["cim"], sp["d"], wglu, bglu, wso]
    return pl.pallas_call(
        functools.partial(_s5_sample_body, nseq=nseq, nstep=nstep),
        grid=(1,),
        in_specs=[_full(o.shape) for o in ops],
        out_specs=[_full((rows, D_MODEL)), _full((nseq, N_STATE)), _full((nseq, N_STATE))],
        out_shape=[jax.ShapeDtypeStruct((rows, D_MODEL), BF16),
                   jax.ShapeDtypeStruct((nseq, N_STATE), F32), jax.ShapeDtypeStruct((nseq, N_STATE), F32)],
        scratch_shapes=[pltpu.VMEM((rows, N_STATE), F32) for _ in range(4)],
        compiler_params=_cparams(("arbitrary",), VMEM_LARGE_MIB),
        name="s5_sample",
    )(*ops)


def _softmax_rows(s, valid):
    sm = jnp.where(valid, s, NEG_INF)
    mx = jnp.max(sm, axis=1, keepdims=True)
    e = jnp.where(valid, jnp.exp2(sm - mx), 0.0)
    l = jnp.sum(e, axis=1, keepdims=True)
    return e * (1.0 / jnp.maximum(l, 1e-30))


SAMPLE_SEQS_PER_STEP = 4
CMP_PITCH = 24


def _attn_sample_body(pt_ref, q_ref, gn_ref, nks_ref, nkw_ref, wint_ref, wk_ref, bk_ref, w2k_ref, ov_ref, e_ref,
                      cmp_hbm, slc_hbm, o_ref, nwint_ref, xrow_ref, pages_ref, sem_ref, *, npage, past_len, nsub, tq):
    n = pl.program_id(0)
    nsteps = pl.num_programs(0)
    slot = lax.rem(n, 2)
    nrow = N_Q_HEADS * tq
    nwin = wint_ref.shape[2]
    nslc = -(-(past_len + tq) // SLC_BLOCK)
    nch = past_len // CMP_STRIDE
    per_page = PAGE_SIZE // CMP_STRIDE

    def page_copy(step, into, c, s, p):
        src = (cmp_hbm, slc_hbm)[c]
        return pltpu.make_async_copy(src.at[pt_ref[step * nsub + s, p]],
                                     pages_ref.at[into, (c * nsub + s) * npage + p], sem_ref.at[into])

    def all_pages(step, into, op):
        for c in range(2):
            for s in range(nsub):
                for p in range(npage):
                    op(page_copy(step, into, c, s, p))

    @pl.when(n == 0)
    def _():
        all_pages(0, 0, lambda cp: cp.start())

    all_pages(n, slot, lambda cp: cp.wait())
    nxt = jnp.minimum(n + 1, nsteps - 1)
    all_pages(nxt, 1 - slot, lambda cp: cp.start())

    def page(c, s, p):
        return pages_ref.at[slot, (c * nsub + s) * npage + p]

    cmp = []
    for kv in range(2):
        for s in range(nsub):
            for p in range(npage):
                rows = page(0, s, p)[LANES * kv:LANES * (kv + 1), :].T
                for c in range(per_page):
                    r0 = CMP_PITCH * (per_page * p + c)
                    xrow_ref[s, r0:r0 + CMP_STRIDE, :] = rows[CMP_STRIDE * c:CMP_STRIDE * (c + 1)]
        x = jnp.concatenate(
            [jnp.concatenate([xrow_ref[s, pl.ds(i, nch, stride=CMP_PITCH), :] for i in range(CMP_STRIDE)], axis=1)
             for s in range(nsub)], axis=0)
        pp = jnp.dot(x.astype(BF16), wk_ref[kv], preferred_element_type=F32)
        pre = pp[:, 0:LANES] + pltpu.roll(pp[:, LANES:2 * LANES], nsub * nch - 1, axis=0) + bk_ref[kv:kv + 1, :]
        cmp.append(jnp.dot(jax.nn.gelu(pre).astype(BF16), w2k_ref[kv], preferred_element_type=F32).astype(BF16))
    cks = [cmp[0][nch * s:nch * (s + 1)] for s in range(nsub)]
    cvs = [cmp[1][nch * s:nch * (s + 1)] for s in range(nsub)]
    seqs = range(nsub)
    rcat = lambda parts: jnp.concatenate(parts, axis=0)

    lane_w = lax.broadcasted_iota(jnp.int32, (KV_W, LANES), 1)
    lane8 = lax.broadcasted_iota(jnp.int32, (tq, LANES), 1)
    nks_l, nkw_l, wint_l, qs_l = [], [], [], []
    for s in seqs:
        rows_s = slice(tq * s, tq * (s + 1))
        nks_l.append(jnp.concatenate([nks_ref[rows_s, :], jnp.zeros((LANES - tq, KV_W), F32)], axis=0))
        nkw = jnp.concatenate([nkw_ref[rows_s, :], jnp.zeros((LANES - tq, KV_W), F32)], axis=0)
        nkw_l.append(nkw)
        wint = wint_ref[s]
        wint_l.append(wint)
        shifted = pltpu.roll(wint, nwin - tq, axis=1)
        new_t = pltpu.roll(nkw.T, LANES - tq, axis=1)
        nwint_ref[s, :, 0:nwin - LANES] = shifted[:, 0:nwin - LANES]
        nwint_ref[s, :, nwin - LANES:nwin] = jnp.where(lane_w >= LANES - tq, new_t, shifted[:, nwin - LANES:nwin])
        q = q_ref[rows_s, :]
        qrows = []
        for j in range(N_Q_HEADS):
            chunk = q[:, LANES * (j // 2):LANES * (j // 2 + 1)]
            dst = j // GQA
            if (j % 2) != dst:
                chunk = pltpu.roll(chunk, HEAD_DIM, axis=1)
            keep = (lane8 < HEAD_DIM) if dst == 0 else (lane8 >= HEAD_DIM)
            qrows.append(jnp.where(keep, chunk, 0.0))
        qs_l.append(jnp.concatenate(qrows, axis=0).astype(BF16))

    rtot = nsub * nrow
    seq_rows = [slice(nrow * s, nrow * (s + 1)) for s in seqs]
    pos = past_len + (lax.broadcasted_iota(jnp.int32, (rtot, LANES), 0) & (tq - 1))
    lane = lax.broadcasted_iota(jnp.int32, (rtot, LANES), 1)

    sc = rcat([_dot_t(qs_l[s], cks[s]) for s in seqs])
    pc = _softmax_rows(sc, lane * CMP_STRIDE + (CMP_LEN - 1) <= pos).astype(BF16)
    oc = rcat([jnp.dot(pc[seq_rows[s]], cvs[s], preferred_element_type=F32) for s in seqs])
    imp = jnp.dot(pc, ov_ref[...], preferred_element_type=F32)
    vs = []
    for s in seqs:
        for h in range(N_KV_HEADS):
            r0 = nrow * s + tq * GQA * h
            v = imp[r0:r0 + tq]
            for g in range(1, GQA):
                v = v + imp[r0 + tq * g:r0 + tq * (g + 1)]
            vs.append(v)
    nsel = len(vs) * tq
    vt = rcat(vs + [jnp.zeros((LANES - nsel, LANES), F32)]).T
    nblk_pad = -(-nslc // SUBLANES) * SUBLANES
    blk_t = lax.broadcasted_iota(jnp.int32, (nblk_pad, LANES), 0)
    pos_t = past_len + (lax.broadcasted_iota(jnp.int32, (nblk_pad, LANES), 1) & (tq - 1))
    neg_t = _select_blocks(vt[0:nblk_pad], blk_t, pos_t, nslc, axis=0)
    neg = rcat([neg_t, jnp.zeros((LANES - nblk_pad, LANES), F32)]).T
    negsel = rcat([neg[tq * (N_KV_HEADS * s + j // GQA):tq * (N_KV_HEADS * s + j // GQA + 1)]
                   for s in seqs for j in range(N_Q_HEADS)])
    negsel_b = negsel.astype(BF16)

    new_blk = past_len // SLC_BLOCK
    ss_l = []
    for s in seqs:
        qaug = jnp.concatenate([qs_l[s], negsel_b[seq_rows[s]]], axis=1)
        parts = []
        for p in range(0, npage, 2):
            kt = jnp.concatenate([page(1, s, p)[0:LANES, :], page(1, s, p + 1)[0:LANES, :]], axis=1).astype(BF16)
            et = jnp.concatenate([e_ref[p], e_ref[p + 1]], axis=1)
            parts.append(jnp.dot(qaug, jnp.concatenate([kt, et], axis=0), preferred_element_type=F32))
        parts.append(_dot_t(qs_l[s], nks_l[s][:, 0:LANES].astype(BF16)) + negsel[seq_rows[s], new_blk:new_blk + 1])
        ss_l.append(jnp.concatenate(parts, axis=1))
    ss = rcat(ss_l)
    nkeys = ss.shape[1]
    kpos = lax.broadcasted_iota(jnp.int32, (rtot, nkeys), 1)
    pos_k = past_len + (lax.broadcasted_iota(jnp.int32, (rtot, nkeys), 0) & (tq - 1))
    ps = _softmax_rows(ss, kpos <= pos_k).astype(BF16)
    osel_l = []
    for s in seqs:
        psq = ps[seq_rows[s]]
        o = jnp.dot(psq[:, past_len:nkeys], nks_l[s][:, LANES:2 * LANES].astype(BF16), preferred_element_type=F32)
        for p in range(0, npage, 2):
            vtp = jnp.concatenate([page(1, s, p)[LANES:2 * LANES, :], page(1, s, p + 1)[LANES:2 * LANES, :]],
                                  axis=1).astype(BF16)
            o = o + _dot_t(psq[:, PAGE_SIZE * p:PAGE_SIZE * (p + 2)], vtp)
        osel_l.append(o)
    osel = rcat(osel_l)

    sw = rcat([jnp.concatenate([jnp.dot(qs_l[s], wint_l[s][0:LANES].astype(BF16), preferred_element_type=F32),
                                _dot_t(qs_l[s], nkw_l[s][:, 0:LANES].astype(BF16))], axis=1) for s in seqs])
    nw = sw.shape[1]
    widx = lax.broadcasted_iota(jnp.int32, (rtot, nw), 1)
    pos_w = past_len + (lax.broadcasted_iota(jnp.int32, (rtot, nw), 0) & (tq - 1))
    dlt = pos_w - (past_len - nwin + widx)
    pw = _softmax_rows(sw, (dlt >= 0) & (dlt < WINDOW) & (widx < nwin + tq)).astype(BF16)
    ow = rcat([_dot_t(pw[seq_rows[s], 0:nwin], wint_l[s][LANES:2 * LANES].astype(BF16))
               + jnp.dot(pw[seq_rows[s], nwin:nw], nkw_l[s][:, LANES:2 * LANES].astype(BF16),
                         preferred_element_type=F32) for s in seqs])

    for s in seqs:
        rows_s = slice(tq * s, tq * (s + 1))
        gn = gn_ref[rows_s, :]
        for c in range(N_Q_HEADS // 2):
            halves = []
            for hh in range(2):
                j = 2 * c + hh
                rs = slice(nrow * s + tq * j, nrow * s + tq * (j + 1))
                oj = (gn[:, 3 * j:3 * j + 1] * oc[rs] + gn[:, 3 * j + 1:3 * j + 2] * osel[rs]
                      + gn[:, 3 * j + 2:3 * j + 3] * ow[rs])
                if (j // GQA) != hh:
                    oj = pltpu.roll(oj, HEAD_DIM, axis=1)
                halves.append(oj)
            o_ref[rows_s, LANES * c:LANES * (c + 1)] = jnp.where(lane8 < HEAD_DIM, halves[0], halves[1])

    @pl.when(n == nsteps - 1)
    def _():
        all_pages(nxt, 1 - slot, lambda cp: cp.wait())


def _attn_sample(q, gn, nks, nkw, cache_cmp, cache_slc, cache_win, page_table, cp, nseq, tq, past_len):
    assert tq <= CMP_STRIDE and past_len % PAGE_SIZE == 0
    npage = past_len // PAGE_SIZE
    assert npage % 2 == 0 and PAGE_SIZE == LANES
    n_pool = cache_cmp.shape[0]
    nwin = cache_win.shape[1]
    chunks = past_len // CMP_STRIDE
    ov = _overlap_t(chunks, LANES).T
    key = np.arange(past_len).reshape(npage, 1, PAGE_SIZE)
    e = jnp.asarray(np.arange(LANES).reshape(1, LANES, 1) == key // SLC_BLOCK, dtype=BF16)
    to_t = lambda c: jnp.transpose(c, (0, 2, 3, 4, 1)).reshape(c.shape[0], KV_W, c.shape[1])
    cmp_t, slc_t, win_t = to_t(cache_cmp), to_t(cache_slc), to_t(cache_win)
    nsub = SAMPLE_SEQS_PER_STEP
    assert nseq % nsub == 0
    row = lambda n, pt: (n, 0)
    seq3 = lambda n, pt: (n, 0, 0)
    consts = [cp["wk"], cp["bk"], cp["w2k"], ov, e]
    in_specs = [pl.BlockSpec((nsub * tq, Q_W), row), pl.BlockSpec((nsub * tq, LANES), row),
                pl.BlockSpec((nsub * tq, KV_W), row), pl.BlockSpec((nsub * tq, KV_W), row),
                pl.BlockSpec((nsub, KV_W, nwin), seq3)]
    in_specs += [pl.BlockSpec(c.shape, (lambda nd: lambda n, pt: (0,) * nd)(c.ndim)) for c in consts]
    in_specs += [pl.BlockSpec(memory_space=pl.ANY)] * 2
    grid_spec = pltpu.PrefetchScalarGridSpec(
        num_scalar_prefetch=1,
        grid=(nseq // nsub,),
        in_specs=in_specs,
        out_specs=[pl.BlockSpec((nsub * tq, Q_W), row), pl.BlockSpec((nsub, KV_W, nwin), seq3)],
        scratch_shapes=[pltpu.VMEM((nsub, chunks * CMP_PITCH, LANES), F32),
                        pltpu.VMEM((2, 2 * nsub * npage, KV_W, PAGE_SIZE), F32),
                        pltpu.SemaphoreType.DMA((2,))],
    )
    return pl.pallas_call(
        functools.partial(_attn_sample_body, npage=npage, past_len=past_len, nsub=nsub, tq=tq),
        grid_spec=grid_spec,
        out_shape=[jax.ShapeDtypeStruct((nseq * tq, Q_W), F32), jax.ShapeDtypeStruct((nseq, KV_W, nwin), F32)],
        compiler_params=_cparams(("arbitrary",), VMEM_LARGE_MIB),
        name="attn_sample",
    )(page_table, q, gn, nks, nkw, win_t, *consts, cmp_t, slc_t)


def _moe_params(w_rg, b_rg, w_re, b_re, w_gate, w_up, w_down, w_ple, w_ple_gate, gf):
    pad = LANES - N_EXPERT_GROUPS - N_EXPERTS
    return {"wr": jnp.pad(jnp.concatenate([w_rg, w_re], axis=1), ((0, 0), (0, pad))).astype(BF16),
            "br": jnp.pad(jnp.concatenate([b_rg, b_re]), (0, pad)).astype(F32).reshape(1, LANES),
            "wg": w_gate.astype(BF16), "wu": w_up.astype(BF16), "wd": w_down.astype(BF16),
            "wpg": w_ple_gate.astype(BF16), "wp": w_ple.astype(BF16), "gf": gf.astype(F32).reshape(1, D_MODEL)}


TM_PROMPT = 512
TM_MOE = 1024
TC_S5 = 128


def kernel(x_prompt, x_sample, p_prompt, p_sample, cache_cmp_kv, cache_slc_kv, cache_win_kv, state_ssm, page_table, norm1_g, w_in, ssm_lam_re, ssm_lam_im, ssm_log_dt, ssm_b_re, ssm_b_im, ssm_c_re, ssm_c_im, ssm_d, w_glu, b_glu, cmp_pe, cmp_w1, cmp_w2, w_ssm_out, w_nsa_out, w_o, norm2_g, w_route_group, b_route_group, w_route_expert, b_route_expert, w_exp_gate, w_exp_up, w_exp_down, w_ple, w_ple_gate, final_norm_g):
    assert w_in.shape[0] == 1, "one layer"
    l = 0
    nb, t = x_prompt.shape[:2]
    ns, ts = x_sample.shape[:2]
    past_len = page_table.shape[1] * PAGE_SIZE
    kvt = (2, N_KV_HEADS, HEAD_DIM)

    wi = _inproj_params(w_in[l])
    g1 = norm1_g[l].astype(F32).reshape(1, D_MODEL)
    g2 = norm2_g[l].astype(F32).reshape(1, D_MODEL)
    sp = _s5_params(ssm_lam_re[l], ssm_lam_im[l], ssm_log_dt[l], ssm_b_re[l], ssm_b_im[l], ssm_c_re[l], ssm_c_im[l],
                    ssm_d[l])
    cp = _cmp_params(cmp_pe[l], cmp_w1[l], cmp_w2[l])
    mp = _moe_params(w_route_group[l], b_route_group[l], w_route_expert[l], b_route_expert[l], w_exp_gate[l],
                     w_exp_up[l], w_exp_down[l], w_ple[l], w_ple_gate[l], final_norm_g)
    wglu = w_glu[l].astype(BF16)
    bglu = b_glu[l].astype(F32).reshape(1, SSM_WIDTH)
    wso = w_ssm_out[l].astype(BF16)
    wno = w_nsa_out[l].astype(BF16)
    wo = w_o[l].astype(BF16)

    lay = _prompt_layout(nb, t, TM_PROMPT)
    xp = x_prompt.reshape(nb * t, D_MODEL)
    r = _inproj_prompt(xp, lay, g1, wi)
    abr, hlast = _s5_prompt(r["u"], sp, wglu, bglu, wso, t, TC_S5)
    ck, cvt = _compress_prompt(r["kvc"], cp, nb, t)
    onsa = _attn_prompt(r["qt"], r["gnt"], ck, cvt, r["ksb"], r["vst"], r["kwb"], r["vwt"], nb, t)
    x1, h2 = _post(xp, abr, "a", onsa, r["ga"], r["gb"], wno, wo, g2, lay, "a")
    y_prompt = _moe(x1, h2, p_prompt[l].reshape(nb * t, PLE_DIM), mp, TM_MOE, 1).reshape(nb, t, D_MODEL)
    keep = min(WINDOW, t)

    def rows_last(a):
        return jnp.transpose(a.reshape((a.shape[0],) + kvt + (a.shape[2],)), (0, 4, 1, 2, 3))[None]

    new_cmp_p = rows_last(r["kvct"])
    new_slc_p = rows_last(r["kvst"])
    new_win_p = rows_last(r["kvwt"][:, :, t - keep:])
    new_ssm_p = jnp.stack([hlast[0:nb], hlast[nb:2 * nb]], axis=-1).reshape(1, nb, N_SSM_GROUPS, SSM_STATE, 2)

    lays = _sample_layout(ns, ts)
    xs = x_sample.reshape(ns * ts, D_MODEL)
    rs = _inproj_sample(xs, lays, g1, wi)
    h0 = state_ssm[l].astype(F32).reshape(ns, N_STATE, 2)
    abr_s, hre, him = _s5_sample(rs["u"], h0[..., 0], h0[..., 1], sp, wglu, bglu, wso, ns, ts)
    onsa_s, new_win = _attn_sample(rs["q"].reshape(ns * ts, Q_W), rs["gn"].reshape(ns * ts, LANES),
                                   rs["kvs"].reshape(ns * ts, KV_W), rs["kvw"].reshape(ns * ts, KV_W),
                                   cache_cmp_kv[l], cache_slc_kv[l], cache_win_kv[l], page_table, cp, ns, ts, past_len)
    x1s, h2s = _post(xs, abr_s, "b", onsa_s, rs["ga"], rs["gb"], wno, wo, g2, lays, "b")
    y_sample = _moe(x1s, h2s, p_sample[l].reshape(ns, ts * PLE_DIM), mp, ns * ts, ts).reshape(ns, ts, D_MODEL)
    steps_first = lambda a: jnp.transpose(a.reshape((ts,) + kvt + (ns,)), (4, 0, 1, 2, 3))[None]
    new_cmp_s = steps_first(rs["kvct"])
    new_slc_s = steps_first(rs["kvst"])
    new_win_s = rows_last(new_win)
    new_ssm_s = jnp.stack([hre, him], axis=-1).reshape(1, ns, N_SSM_GROUPS, SSM_STATE, 2)
    return (y_prompt, y_sample, new_cmp_p, new_slc_p, new_win_p, new_ssm_p,
            new_cmp_s, new_slc_s, new_win_s, new_ssm_s)
```

```python
import functools
import math

import jax
import jax.numpy as jnp
import numpy as np
from jax import lax
from jax.experimental import pallas as pl
from jax.experimental.pallas import tpu as pltpu

F32 = jnp.float32
BF16 = jnp.bfloat16

D_MODEL = 1024
SSM_WIDTH = 512
SSM_GROUP = 16
N_SSM_GROUPS = 32
SSM_STATE = 64
HEAD_DIM = 64
N_Q_HEADS = 8
N_KV_HEADS = 2
GQA = 4
CMP_LEN = 32
CMP_STRIDE = 16
SLC_BLOCK = 64
TOP_N = 8
WINDOW = 512
Q_BLOCK = 256
NEG_INF = -1e30
FORCE_BONUS = 1e4
Q_W = 512
KV_W = 256
NSA_GATE_W = 24
N_EXPERT_GROUPS = 4
EXPERTS_PER_GROUP = 4
N_EXPERTS = 16
D_FF_EXPERT = 256
PLE_DIM = 256
RMS_EPS = 1e-6
PAGE_SIZE = 128

LANES = 128
SUBLANES = 8
N_STATE = N_SSM_GROUPS * SSM_STATE
MIB = 2 ** 20
V7X_VMEM_MIB = 64
VMEM_SMALL_MIB = 48
VMEM_LARGE_MIB = 56
VMEM_MOE_MIB = V7X_VMEM_MIB - 4


def _cparams(sem, vmem_mib):
    return pltpu.CompilerParams(dimension_semantics=sem, vmem_limit_bytes=vmem_mib * MIB)


def _full(shape):
    nd = len(shape)
    return pl.BlockSpec(shape, lambda *_: (0,) * nd)


def _prompt_layout(nseq, t, tm):
    nb = t // tm
    return {
        "grid": (nb, nseq), "tm": tm, "nseq": nseq, "t": t,
        "a": lambda w: ((nseq * t, w), pl.BlockSpec((tm, w), lambda b, s: (s * nb + b, 0))),
    }


def _sample_layout(nseq, t):
    return {
        "grid": (1, t), "tm": nseq,
        "a": lambda w: ((nseq, t * w), pl.BlockSpec((nseq, w), lambda s, b: (0, b))),
        "b": lambda w: ((t * nseq, w), pl.BlockSpec((nseq, w), lambda s, b: (b, 0))),
    }


TK_SLC = 512
TK_WIN = 128


Q_SCALE = HEAD_DIM ** -0.5 * math.log2(math.e)
C_U, C_Q, C_KVC, C_KVS, C_KVW = 0, 512, 1024, 1280, 1536
N_MAIN = 1792


def _dot_t(a, b):
    return lax.dot_general(a, b, (((1,), (1,)), ((), ())), preferred_element_type=F32)


GN_ROWS = 32


def _inproj_prompt_body(x_ref, g_ref, wa_ref, wgn_ref, wgab_ref,
                        u_ref, kvc_ref, ksb_ref, kwb_ref, ga_ref, gb_ref,
                        qt_ref, kvct_ref, kvst_ref, kvwt_ref, gnt_ref, vst_ref, vwt_ref, *, nseq):
    s = pl.program_id(1)
    x = x_ref[...]
    inv = lax.rsqrt(jnp.mean(x * x, axis=-1, keepdims=True) + RMS_EPS)
    h = (x * inv * g_ref[...]).astype(BF16)
    tm = h.shape[0]

    def mm(w):
        return jnp.dot(h, w, preferred_element_type=F32)

    u = mm(wa_ref[:, C_U:C_U + SSM_WIDTH])
    for j in range(SSM_WIDTH // LANES):
        u_ref[j, pl.ds(s, tm, stride=nseq), :] = u[:, LANES * j:LANES * (j + 1)]
    ga_ref[...] = jax.nn.sigmoid(mm(wgab_ref[:, 0:D_MODEL])).astype(BF16)
    gb_ref[...] = jax.nn.sigmoid(mm(wgab_ref[:, D_MODEL:2 * D_MODEL])).astype(BF16)
    kvc = mm(wa_ref[:, C_KVC:C_KVC + KV_W])
    kvc_ref[0] = kvc[:, 0:LANES]
    kvc_ref[1] = kvc[:, LANES:2 * LANES]
    kvct_ref[0] = kvc.T
    kvs = mm(wa_ref[:, C_KVS:C_KVS + KV_W])
    ksb_ref[...] = kvs[:, 0:LANES].astype(BF16)
    kvst = kvs.T
    kvst_ref[0] = kvst
    kvw = mm(wa_ref[:, C_KVW:C_KVW + KV_W])
    kwb_ref[...] = kvw[:, 0:LANES].astype(BF16)
    kvwt = kvw.T
    kvwt_ref[0] = kvwt
    for c in range(tm // TK_WIN):
        vst_ref[0, c] = kvst[LANES:2 * LANES, c * TK_WIN:(c + 1) * TK_WIN].astype(BF16)
        vwt_ref[0, c] = kvwt[LANES:2 * LANES, c * TK_WIN:(c + 1) * TK_WIN].astype(BF16)
    qt_ref[0] = (mm(wa_ref[:, C_Q:C_Q + Q_W]) * Q_SCALE).T.astype(BF16)
    gnt_ref[0] = jax.nn.sigmoid(mm(wgn_ref[...])).T[0:GN_ROWS]


def _inproj_prompt(x2d, lay, g, w):
    tm, nseq, t = lay["tm"], lay["nseq"], lay["t"]
    nb = t // tm
    out_shapes, out_specs, names = [], [], []

    def add(name, shape_spec, dt):
        names.append(name)
        out_shapes.append(jax.ShapeDtypeStruct(shape_spec[0], dt))
        out_specs.append(shape_spec[1])

    def tr(rows):
        return (nseq, rows, t), pl.BlockSpec((1, rows, tm), lambda b, s: (s, 0, b))

    nu = SSM_WIDTH // LANES
    add("u", ((nu, t * nseq, LANES), pl.BlockSpec((nu, tm * nseq, LANES), lambda b, s: (0, b, 0))), F32)
    add("kvc", ((2, nseq * t, LANES), pl.BlockSpec((2, tm, LANES), lambda b, s: (0, s * nb + b, 0))), F32)
    add("ksb", lay["a"](LANES), BF16)
    add("kwb", lay["a"](LANES), BF16)
    add("ga", lay["a"](D_MODEL), BF16)
    add("gb", lay["a"](D_MODEL), BF16)
    add("qt", tr(Q_W), BF16)
    add("kvct", tr(KV_W), F32)
    add("kvst", tr(KV_W), F32)
    add("kvwt", tr(KV_W), F32)
    add("gnt", tr(GN_ROWS), F32)
    for name in ("vst", "vwt"):
        add(name, ((nseq, t // TK_WIN, LANES, TK_WIN),
                   pl.BlockSpec((1, tm // TK_WIN, LANES, TK_WIN), lambda b, s: (s, b, 0, 0))), BF16)
    x_shape, x_spec = lay["a"](D_MODEL)
    ops = [g, w["wa"], w["wgn"], w["wgab"]]
    outs = pl.pallas_call(
        functools.partial(_inproj_prompt_body, nseq=nseq),
        grid=lay["grid"],
        in_specs=[x_spec] + [_full(o.shape) for o in ops],
        out_specs=out_specs,
        out_shape=out_shapes,
        compiler_params=_cparams(("arbitrary",) * 2, VMEM_LARGE_MIB),
        name="inproj_prompt",
    )(x2d.reshape(x_shape), *ops)
    return dict(zip(names, outs))


def _inproj_sample_body(x_ref, g_ref, wa_ref, wgn_ref, wgab_ref,
                        u_ref, q_ref, kvs_ref, kvw_ref, gn_ref, ga_ref, gb_ref, kvct_ref, kvst_ref, kvwt_ref):
    x = x_ref[...]
    inv = lax.rsqrt(jnp.mean(x * x, axis=-1, keepdims=True) + RMS_EPS)
    h = (x * inv * g_ref[...]).astype(BF16)

    def mm(w):
        return jnp.dot(h, w, preferred_element_type=F32)

    u_ref[...] = mm(wa_ref[:, C_U:C_U + SSM_WIDTH])
    q_ref[...] = mm(wa_ref[:, C_Q:C_Q + Q_W]) * Q_SCALE
    kvs = mm(wa_ref[:, C_KVS:C_KVS + KV_W])
    kvs_ref[...] = kvs
    kvw = mm(wa_ref[:, C_KVW:C_KVW + KV_W])
    kvw_ref[...] = kvw
    gn_ref[...] = jax.nn.sigmoid(mm(wgn_ref[...]))
    ga_ref[...] = jax.nn.sigmoid(mm(wgab_ref[:, 0:D_MODEL])).astype(BF16)
    gb_ref[...] = jax.nn.sigmoid(mm(wgab_ref[:, D_MODEL:2 * D_MODEL])).astype(BF16)
    kvct_ref[0] = mm(wa_ref[:, C_KVC:C_KVC + KV_W]).T
    kvst_ref[0] = kvs.T
    kvwt_ref[0] = kvw.T


def _inproj_sample(x2d, lay, g, w):
    nseq = lay["tm"]
    ts = lay["grid"][1]
    names = ["u", "q", "kvs", "kvw", "gn", "ga", "gb"]
    widths = [SSM_WIDTH, Q_W, KV_W, KV_W, LANES, D_MODEL, D_MODEL]
    out_shapes, out_specs = [], []
    for n, wd in zip(names, widths):
        shp, spec = lay["b" if n == "u" else "a"](wd)
        out_shapes.append(jax.ShapeDtypeStruct(shp, BF16 if n in ("ga", "gb") else F32))
        out_specs.append(spec)
    for n in ("kvct", "kvst", "kvwt"):
        names.append(n)
        out_shapes.append(jax.ShapeDtypeStruct((ts, KV_W, nseq), F32))
        out_specs.append(pl.BlockSpec((1, KV_W, nseq), lambda s, b: (b, 0, 0)))
    x_shape, x_spec = lay["a"](D_MODEL)
    ops = [g, w["wa"], w["wgn"], w["wgab"]]
    outs = pl.pallas_call(
        _inproj_sample_body,
        grid=lay["grid"],
        in_specs=[x_spec] + [_full(o.shape) for o in ops],
        out_specs=out_specs,
        out_shape=out_shapes,
        compiler_params=_cparams(("arbitrary",) * 2, VMEM_LARGE_MIB),
        name="inproj_sample",
    )(x2d.reshape(x_shape), *ops)
    return dict(zip(names, outs))


def _inproj_params(w_in0):
    return {"wa": w_in0[:, :N_MAIN].astype(BF16),
            "wgn": jnp.pad(w_in0[:, N_MAIN:N_MAIN + NSA_GATE_W], ((0, 0), (0, LANES - NSA_GATE_W))).astype(BF16),
            "wgab": w_in0[:, N_MAIN + NSA_GATE_W:].astype(BF16)}


def _s5_prompt_body(u_ref, wb_ref, ar_ref, ai_ref, cw_ref, d_ref, wglu_ref, bglu_ref, wso_ref,
                    abr_ref, hlast_ref, lhs_ref, bu_ref, h8_ref, p_ref, hstate_ref):
    c = pl.program_id(0)
    nseq = 4
    r4 = u_ref.shape[1]
    tc = r4 // nseq
    half = tc // 2

    @pl.when(c == 0)
    def _():
        hstate_ref[...] = jnp.zeros_like(hstate_ref)

    u = jnp.concatenate([u_ref[j] for j in range(SSM_WIDTH // LANES)], axis=1)
    row2 = lax.broadcasted_iota(jnp.int32, (r4, SSM_WIDTH), 0)
    lo2 = (row2 % SUBLANES) < nseq
    up = pltpu.roll(u, r4 - nseq, axis=0)
    dn = pltpu.roll(u, nseq, axis=0)
    swapped = jnp.where(lo2, up, dn)
    zero = jnp.zeros_like(u)
    ev_re = jnp.where(lo2, u, zero).astype(BF16).reshape(half, SUBLANES, SSM_WIDTH)
    ev_im = jnp.where(lo2, zero, swapped).astype(BF16).reshape(half, SUBLANES, SSM_WIDTH)
    od_re = jnp.where(lo2, swapped, zero).astype(BF16).reshape(half, SUBLANES, SSM_WIDTH)
    od_im = jnp.where(lo2, zero, u).astype(BF16).reshape(half, SUBLANES, SSM_WIDTH)
    for j in range(4):
        sl = slice(LANES * j, LANES * (j + 1))
        lhs_ref[:, 0:8, 256 * j:256 * j + LANES] = ev_re[:, :, sl]
        lhs_ref[:, 0:8, 256 * j + LANES:256 * (j + 1)] = ev_im[:, :, sl]
        lhs_ref[:, 8:16, 256 * j:256 * j + LANES] = od_re[:, :, sl]
        lhs_ref[:, 8:16, 256 * j + LANES:256 * (j + 1)] = od_im[:, :, sl]
    for j in range(4):
        lhs = lhs_ref[:, :, 256 * j:256 * (j + 1)].reshape(tc * SUBLANES, 256)
        bu_ref[:, 512 * j:512 * (j + 1)] = jnp.dot(lhs, wb_ref[j], preferred_element_type=F32)

    for lc in range(4):
        sl = slice(512 * lc, 512 * (lc + 1))
        ar = ar_ref[:, sl]
        ai = ai_ref[:, sl]

        def step(t, h, sl=sl, ar=ar, ai=ai):
            r0 = pl.multiple_of(t * SUBLANES, SUBLANES)
            h = ar * h + ai * pltpu.roll(h, nseq, axis=0) + bu_ref[pl.ds(r0, SUBLANES), sl]
            h8_ref[pl.ds(r0, SUBLANES), sl] = h
            return h

        hstate_ref[:, sl] = lax.fori_loop(0, tc, step, hstate_ref[:, sl], unroll=8)
    hlast_ref[...] = hstate_ref[...]

    for j in range(4):
        pj = jnp.dot(h8_ref[:, 512 * j:512 * (j + 1)].astype(BF16), cw_ref[j], preferred_element_type=F32)
        p_ref[2 * j] = pj[:, 0:LANES]
        p_ref[2 * j + 1] = pj[:, LANES:2 * LANES]
    ys = []
    for s in range(nseq):
        parts = []
        for j in range(4):
            re = p_ref[2 * j, pl.ds(s, tc, stride=SUBLANES), :]
            im = p_ref[2 * j + 1, pl.ds(nseq + s, tc, stride=SUBLANES), :]
            us = u_ref[j, pl.ds(s, tc, stride=nseq), :]
            parts.append(re + im + d_ref[:, LANES * j:LANES * (j + 1)] * us)
        ys.append(jnp.concatenate(parts, axis=1))
    y = jnp.concatenate(ys, axis=0)
    zg = jax.nn.gelu(y)
    gate = jnp.dot(zg.astype(BF16), wglu_ref[...], preferred_element_type=F32) + bglu_ref[...]
    glu = (zg * jax.nn.sigmoid(gate)).astype(BF16)
    abr = jnp.dot(glu, wso_ref[...], preferred_element_type=F32)
    for s in range(nseq):
        abr_ref[s] = abr[s * tc:(s + 1) * tc].astype(BF16)


def _s5_prompt(u_ts, sp, wglu, bglu, wso, t_total, tc):
    nseq = 4
    grid = (t_total // tc,)
    abr, hlast = pl.pallas_call(
        _s5_prompt_body,
        grid=grid,
        in_specs=[pl.BlockSpec((SSM_WIDTH // LANES, tc * nseq, LANES), lambda c: (0, c, 0)),
                  _full(sp["wb8"].shape), _full(sp["ar8"].shape), _full(sp["ai8"].shape), _full(sp["cw8"].shape),
                  _full(sp["d"].shape), _full(wglu.shape), _full(bglu.shape), _full(wso.shape)],
        out_specs=[pl.BlockSpec((nseq, tc, D_MODEL), lambda c: (0, c, 0)),
                   pl.BlockSpec((SUBLANES, N_STATE), lambda c: (0, 0))],
        out_shape=[jax.ShapeDtypeStruct((nseq, t_total, D_MODEL), BF16),
                   jax.ShapeDtypeStruct((SUBLANES, N_STATE), F32)],
        scratch_shapes=[pltpu.VMEM((tc // 2, 2 * SUBLANES, 1024), BF16),
                        pltpu.VMEM((tc * SUBLANES, N_STATE), F32),
                        pltpu.VMEM((tc * SUBLANES, N_STATE), F32),
                        pltpu.VMEM((8, tc * SUBLANES, LANES), F32),
                        pltpu.VMEM((SUBLANES, N_STATE), F32)],
        compiler_params=_cparams(("arbitrary",), VMEM_LARGE_MIB),
        name="s5_prompt",
    )(u_ts, sp["wb8"], sp["ar8"], sp["ai8"], sp["cw8"], sp["d"], wglu, bglu, wso)
    return abr, hlast


def _s5_params(lam_re, lam_im, log_dt, b_re, b_im, c_re, c_im, d_skip):
    lam = lax.complex(lam_re.astype(F32), lam_im.astype(F32))
    dt = jnp.exp(log_dt.astype(F32))[:, None]
    a_bar = jnp.exp(lam * dt)
    b = lax.complex(b_re.astype(F32), b_im.astype(F32))
    b_bar = ((a_bar - 1.0) / lam)[..., None] * b
    eye8 = jnp.eye(8, dtype=F32)

    def bd_b(m):
        return jnp.einsum("ab,jbpc->jacbp", eye8, m.reshape(4, 8, SSM_STATE, SSM_GROUP)).reshape(4, 128, 512)

    def bd_c(m):
        return jnp.einsum("ab,jbcp->japbc", eye8, m.reshape(4, 8, SSM_GROUP, SSM_STATE)).reshape(4, 512, 128)

    wre, wim = bd_b(b_bar.real), bd_b(b_bar.imag)
    cre, cim = bd_c(c_re.astype(F32)), bd_c(c_im.astype(F32))
    ar = a_bar.real.reshape(1, N_STATE)
    ai = a_bar.imag.reshape(1, N_STATE)
    sign = jnp.concatenate([-jnp.ones((4, 1), F32), jnp.ones((4, 1), F32)], axis=0)
    return {
        "wb8": jnp.concatenate([wre, wim], axis=1).astype(BF16),
        "cw8": jnp.concatenate([cre, -cim], axis=2).astype(BF16),
        "ar8": jnp.broadcast_to(ar, (SUBLANES, N_STATE)),
        "ai8": sign * ai,
        "wre": wre.astype(BF16), "wim": wim.astype(BF16),
        "cre": cre.astype(BF16), "cim": cim.astype(BF16),
        "ar": ar, "ai": ai,
        "d": d_skip.astype(F32).reshape(1, SSM_WIDTH),
    }


def _cmp_params(cmp_pe, cmp_w1, cmp_w2):
    eye2 = jnp.eye(2, dtype=F32)
    nhalf = CMP_LEN // CMP_STRIDE
    w1r = cmp_w1.astype(F32).reshape(2, nhalf, CMP_STRIDE, HEAD_DIM, HEAD_DIM)
    wk = jnp.einsum("kside,ph->kipdshe", w1r, eye2).reshape(2, CMP_STRIDE * LANES, nhalf * LANES)
    bk = jnp.einsum("kld,klde->ke", cmp_pe.astype(F32), cmp_w1.astype(F32), precision=lax.Precision.HIGHEST)
    w2k = jnp.einsum("kef,ph->kpehf", cmp_w2.astype(F32), eye2).reshape(2, LANES, LANES)
    return {"wk": wk.astype(BF16), "bk": jnp.tile(bk, (1, N_KV_HEADS)), "w2k": w2k.astype(BF16),
            "w2kt": jnp.swapaxes(w2k, 1, 2).astype(BF16)}


def _compress_hidden(tap, nch, kv, wk_ref, bk_ref):
    x = jnp.concatenate([tap(i).astype(BF16) for i in range(CMP_STRIDE)], axis=1)
    pp = jnp.dot(x, wk_ref[kv], preferred_element_type=F32)
    pre = pp[:, 0:LANES] + pltpu.roll(pp[:, LANES:2 * LANES], nch - 1, axis=0) + bk_ref[kv:kv + 1, :]
    return jax.nn.gelu(pre).astype(BF16)


def _compress_prompt_body(x_ref, wk_ref, bk_ref, w2k_ref, w2kt_ref, ck_ref, cvt_ref):
    nch = x_ref.shape[1] // CMP_STRIDE
    hid = [_compress_hidden(lambda i, kv=kv: x_ref[kv, pl.ds(i, nch, stride=CMP_STRIDE), :], nch, kv, wk_ref, bk_ref)
           for kv in range(2)]
    ck_ref[0] = jnp.dot(hid[0], w2k_ref[0], preferred_element_type=F32).astype(BF16)
    cvt_ref[0] = _dot_t(w2kt_ref[1], hid[1]).astype(BF16)


def _compress_prompt(kvc2, cp, nseq, t):
    nch = t // CMP_STRIDE
    return pl.pallas_call(
        _compress_prompt_body,
        grid=(nseq,),
        in_specs=[pl.BlockSpec((2, t, LANES), lambda n: (0, n, 0)),
                  _full(cp["wk"].shape), _full(cp["bk"].shape), _full(cp["w2k"].shape), _full(cp["w2kt"].shape)],
        out_specs=[pl.BlockSpec((1, nch, LANES), lambda n: (n, 0, 0)),
                   pl.BlockSpec((1, LANES, nch), lambda n: (n, 0, 0))],
        out_shape=[jax.ShapeDtypeStruct((nseq, nch, LANES), BF16),
                   jax.ShapeDtypeStruct((nseq, LANES, nch), BF16)],
        compiler_params=_cparams(("arbitrary",), VMEM_SMALL_MIB),
        name="compress_prompt",
    )(kvc2, cp["wk"], cp["bk"], cp["w2k"], cp["w2kt"])


def _overlap_t(n_cmp_pad, n_slc_pad):
    j = np.arange(n_cmp_pad)[None, :]
    s = np.arange(n_slc_pad)[:, None]
    ov = (j * CMP_STRIDE <= s * SLC_BLOCK + SLC_BLOCK - 1) & (j * CMP_STRIDE + CMP_LEN - 1 >= s * SLC_BLOCK)
    return jnp.asarray(ov, dtype=BF16)


def _softmax_cols(s, valid):
    sm = jnp.where(valid, s, NEG_INF)
    mx = jnp.max(sm, axis=0, keepdims=True)
    e = jnp.where(valid, jnp.exp2(sm - mx), 0.0)
    l = jnp.sum(e, axis=0, keepdims=True)
    return e * (1.0 / jnp.maximum(l, 1e-30))


def _select_blocks(imp, blk, pos, nblk, axis=0):
    cur = pos // SLC_BLOCK
    forced = (blk == 0) | (blk == cur) | (blk == cur - 1)
    v = jnp.where(forced, imp + FORCE_BONUS, imp)
    v = jnp.where(blk * SLC_BLOCK <= pos, v, NEG_INF)
    v = jnp.where(blk < nblk, v, -3e38)
    blk_f = blk.astype(F32)
    neg = jnp.full(imp.shape, NEG_INF, F32)
    for _ in range(min(TOP_N, nblk)):
        mx = jnp.max(v, axis=axis, keepdims=True)
        first = jnp.min(jnp.where(v == mx, blk_f, float(imp.shape[axis])), axis=axis, keepdims=True)
        pick = blk_f == first
        neg = jnp.where(pick, 0.0, neg)
        v = jnp.where(pick, -3e38, v)
    return neg


CB = 2 * LANES


def _attn_prompt_body(q_ref, gn_ref, ck_ref, cvt_ref, ks_ref, vst_ref, kw_ref, vwt_ref, ovt_ref,
                      o_ref, kaug_ref, kwaug_ref, qaug_ref, acc_ref):
    i = pl.program_id(1)
    t = ks_ref.shape[1]
    nch = ck_ref.shape[1]
    nslc = t // SLC_BLOCK
    qb = Q_BLOCK
    ncol = N_Q_HEADS * qb
    ncb = ncol // CB
    q0 = i * qb
    one_row = 2 * LANES - HEAD_DIM
    hrows = [slice(HEAD_DIM * ((CB * cb // qb) // GQA), HEAD_DIM * ((CB * cb // qb) // GQA + 1)) for cb in range(ncb)]

    @pl.when(i == 0)
    def _():
        kaug_ref[:, 0:LANES] = ks_ref[0]
        blk = lax.broadcasted_iota(jnp.int32, (t, LANES), 0) // SLC_BLOCK
        col = lax.broadcasted_iota(jnp.int32, (t, LANES), 1)
        kaug_ref[:, LANES:2 * LANES] = jnp.where(blk == col, 1.0, 0.0).astype(BF16)
        padcol = lax.broadcasted_iota(jnp.int32, (WINDOW, 2 * LANES), 1)
        kwaug_ref[0:WINDOW, :] = jnp.where(padcol == one_row, NEG_INF, 0.0).astype(BF16)
        kwaug_ref[WINDOW:WINDOW + t, 0:LANES] = kw_ref[0]
        kwaug_ref[WINDOW:WINDOW + t, LANES:2 * LANES] = jnp.zeros((t, LANES), BF16)

    zeros64 = jnp.zeros((HEAD_DIM, qb), BF16)
    for j in range(N_Q_HEADS):
        dst = j // GQA
        qaug_ref[HEAD_DIM * dst:HEAD_DIM * (dst + 1), qb * j:qb * (j + 1)] = q_ref[0, HEAD_DIM * j:HEAD_DIM * (j + 1), :]
        qaug_ref[HEAD_DIM * (1 - dst):HEAD_DIM * (2 - dst), qb * j:qb * (j + 1)] = zeros64
    tail_row = lax.broadcasted_iota(jnp.int32, (HEAD_DIM, ncol), 0)
    qaug_ref[one_row:2 * LANES, :] = jnp.where(tail_row == 0, 1.0, 0.0).astype(BF16)

    qaug_ref[LANES:one_row, :] = jnp.zeros((one_row - LANES, ncol), BF16)
    q_blocks = qb // TK_WIN
    npiece = (WINDOW + qb) // TK_WIN
    kws = [kwaug_ref[pl.ds(pl.multiple_of(q0 + w * TK_WIN, TK_WIN), TK_WIN), :] for w in range(npiece)]
    vwt = jnp.concatenate([vwt_ref[0, jnp.maximum(i * q_blocks + w - WINDOW // TK_WIN, 0)] for w in range(npiece)],
                          axis=1)
    wrow = lax.broadcasted_iota(jnp.int32, (TK_WIN, CB), 0)
    wcol = lax.broadcasted_iota(jnp.int32, (TK_WIN, CB), 1) & (qb - 1)
    wbias = []
    for w in range(npiece):
        lo, hi = w * TK_WIN - WINDOW, w * TK_WIN - WINDOW + TK_WIN - 1
        if hi <= 0 and qb - 1 - lo < WINDOW:
            wbias.append(None)
        else:
            dlt = wcol - wrow - lo
            wbias.append(jnp.where((dlt >= 0) & (dlt < WINDOW), 0.0, NEG_INF))
    kw_all = jnp.concatenate(kws, axis=0)
    sws = [jnp.dot(kw_all, qaug_ref[:, CB * cb:CB * (cb + 1)], preferred_element_type=F32) for cb in range(ncb)]
    es, rls = [], []
    for s in sws:
        s = jnp.concatenate([s[TK_WIN * w:TK_WIN * (w + 1)] if b is None else s[TK_WIN * w:TK_WIN * (w + 1)] + b
                             for w, b in enumerate(wbias)], axis=0)
        e = jnp.exp2(s - jnp.max(s, axis=0, keepdims=True))
        es.append(e.astype(BF16))
        rls.append(1.0 / jnp.sum(e, axis=0, keepdims=True))
    ow = jnp.concatenate([jnp.dot(vwt[hrows[cb]], e, preferred_element_type=F32) * rl
                          for cb, (e, rl) in enumerate(zip(es, rls))], axis=1)

    pos_c = q0 + (lax.broadcasted_iota(jnp.int32, (nch, CB), 1) & (qb - 1))
    cvalid = lax.broadcasted_iota(jnp.int32, (nch, CB), 0) * CMP_STRIDE + (CMP_LEN - 1) <= pos_c
    scs = [jnp.dot(ck_ref[0], qaug_ref[0:LANES, CB * cb:CB * (cb + 1)], preferred_element_type=F32)
           for cb in range(ncb)]
    pcs = [_softmax_cols(sc, cvalid).astype(BF16) for sc in scs]
    oc = jnp.concatenate([jnp.dot(cvt_ref[0, hrows[cb], :], pc, preferred_element_type=F32)
                          for cb, pc in enumerate(pcs)], axis=1)
    imp = jnp.concatenate([jnp.dot(ovt_ref[...], pc, preferred_element_type=F32) for pc in pcs], axis=1)
    blk = lax.broadcasted_iota(jnp.int32, (nslc, qb), 0)
    pos_q = q0 + lax.broadcasted_iota(jnp.int32, (nslc, qb), 1)
    for h in range(N_KV_HEADS):
        v = imp[0:nslc, qb * GQA * h:qb * GQA * h + qb]
        for g in range(1, GQA):
            v = v + imp[0:nslc, qb * (GQA * h + g):qb * (GQA * h + g + 1)]
        neg = _select_blocks(v, blk, pos_q, nslc).astype(BF16)
        for g in range(GQA):
            j = GQA * h + g
            qaug_ref[LANES:LANES + nslc, qb * j:qb * (j + 1)] = neg

    brow = lax.broadcasted_iota(jnp.int32, (qb, CB), 0)
    bcol = lax.broadcasted_iota(jnp.int32, (qb, CB), 1) & (qb - 1)
    tri_lo = jnp.where(brow <= bcol, 0.0, NEG_INF)

    acc_ref[...] = jnp.zeros_like(acc_ref)

    def sel_tile(k0, nk, vt, carry, bias):
        m, l = carry
        ka = kaug_ref[pl.ds(k0, nk), :]
        css = [slice(CB * cb, CB * (cb + 1)) for cb in range(ncb)]
        ss = [jnp.dot(ka, qaug_ref[:, cs], preferred_element_type=F32) for cs in css]
        ms, ls, ps, alphas = [], [], [], []
        for cs, s in zip(css, ss):
            if bias is not None:
                s = s + bias
            mn = jnp.maximum(m[:, cs], jnp.max(s, axis=0, keepdims=True))
            alpha = jnp.exp2(m[:, cs] - mn)
            p = jnp.exp2(s - mn)
            ms.append(mn)
            ls.append(alpha * l[:, cs] + jnp.sum(p, axis=0, keepdims=True))
            ps.append(p.astype(BF16))
            alphas.append(alpha)
        pvs = [jnp.dot(vt[hrows[cb]], p, preferred_element_type=F32) for cb, p in enumerate(ps)]
        for cs, alpha, pv in zip(css, alphas, pvs):
            acc_ref[:, cs] = alpha * acc_ref[:, cs] + pv
        return jnp.concatenate(ms, axis=1), jnp.concatenate(ls, axis=1)

    def vt_blocks(ref, b0, n):
        return jnp.concatenate([ref[0, b0 + j] for j in range(n)], axis=1) if n > 1 else ref[0, b0]

    big_blocks = TK_SLC // TK_WIN

    def big_tile(kt, carry):
        return sel_tile(pl.multiple_of(kt * TK_SLC, TK_SLC), TK_SLC, vt_blocks(vst_ref, kt * big_blocks, big_blocks),
                        carry, None)

    def small_tile(kb, carry):
        return sel_tile(pl.multiple_of(kb * qb, qb), qb, vt_blocks(vst_ref, kb * q_blocks, q_blocks), carry, None)

    carry = (jnp.full((1, ncol), NEG_INF, F32), jnp.zeros((1, ncol), F32))
    nbig = q0 // TK_SLC
    carry = lax.fori_loop(0, nbig // 2, lambda kp, c: big_tile(2 * kp + 1, big_tile(2 * kp, c)), carry)
    carry = lax.fori_loop(nbig - nbig % 2, nbig, big_tile, carry)
    carry = lax.fori_loop(nbig * (TK_SLC // qb), i, small_tile, carry)
    _, l = sel_tile(pl.multiple_of(q0, qb), qb, vt_blocks(vst_ref, i * q_blocks, q_blocks), carry, tri_lo)
    osel = acc_ref[...] * (1.0 / l)

    gt = gn_ref[0]
    for c in range(N_Q_HEADS // 2):
        rows = []
        for hh in range(2):
            j = 2 * c + hh
            cs = slice(qb * j, qb * (j + 1))
            rows.append(gt[3 * j:3 * j + 1, :] * oc[:, cs] + gt[3 * j + 1:3 * j + 2, :] * osel[:, cs]
                        + gt[3 * j + 2:3 * j + 3, :] * ow[:, cs])
        o_ref[:, LANES * c:LANES * (c + 1)] = jnp.concatenate(rows, axis=0).T.astype(o_ref.dtype)


def _attn_prompt(q, gn, ck, cvt, ksb, vst, kwb, vwt, nseq, t):
    nb = t // Q_BLOCK
    nch = t // CMP_STRIDE
    nslc = t // SLC_BLOCK
    ovt = _overlap_t(nch, max(nslc, SUBLANES))
    row = lambda n, i: (n * nb + i, 0)
    seq3 = lambda n, i: (n, 0, 0)
    seq4 = lambda n, i: (n, 0, 0, 0)
    col3 = lambda n, i: (n, 0, i)
    return pl.pallas_call(
        _attn_prompt_body,
        grid=(nseq, nb),
        in_specs=[pl.BlockSpec((1, Q_W, Q_BLOCK), col3), pl.BlockSpec((1, gn.shape[1], Q_BLOCK), col3),
                  pl.BlockSpec((1, nch, LANES), seq3), pl.BlockSpec((1, LANES, nch), seq3),
                  pl.BlockSpec((1, t, LANES), seq3), pl.BlockSpec((1, t // TK_WIN, LANES, TK_WIN), seq4),
                  pl.BlockSpec((1, t, LANES), seq3), pl.BlockSpec((1, t // TK_WIN, LANES, TK_WIN), seq4),
                  _full(ovt.shape)],
        out_specs=pl.BlockSpec((Q_BLOCK, Q_W), row),
        out_shape=jax.ShapeDtypeStruct((nseq * t, Q_W), BF16),
        scratch_shapes=[pltpu.VMEM((t, 2 * LANES), BF16),
                        pltpu.VMEM((WINDOW + t, 2 * LANES), BF16),
                        pltpu.VMEM((2 * LANES, N_Q_HEADS * Q_BLOCK), BF16),
                        pltpu.VMEM((HEAD_DIM, N_Q_HEADS * Q_BLOCK), F32)],
        compiler_params=_cparams(("arbitrary", "arbitrary"), VMEM_LARGE_MIB),
        name="attn_prompt",
    )(q, gn, ck, cvt, ksb.reshape(nseq, t, LANES), vst, kwb.reshape(nseq, t, LANES), vwt, ovt)


def _post_body(x_ref, abr_ref, on_ref, ga_ref, gb_ref, wno_ref, wo_ref, g2_ref, x1_ref, h2_ref):
    bbr = jnp.dot(on_ref[...].astype(BF16), wno_ref[...], preferred_element_type=F32)
    merged = ga_ref[...].astype(F32) * abr_ref[...].astype(F32) + gb_ref[...].astype(F32) * bbr
    x1 = x_ref[...] + jnp.dot(merged.astype(BF16), wo_ref[...], preferred_element_type=F32)
    x1_ref[...] = x1
    inv = lax.rsqrt(jnp.mean(x1 * x1, axis=-1, keepdims=True) + RMS_EPS)
    h2_ref[...] = (x1 * inv * g2_ref[...]).astype(BF16)


def _post(x2d, abr, abr_lay, onsa, ga, gb, wno, wo, g2, lay, out_lay):
    x_shape, x_spec = lay["a"](D_MODEL)
    abr_shape, abr_spec = lay[abr_lay](D_MODEL)
    on_shape, on_spec = lay["a"](Q_W)
    o_shape, o_spec = lay[out_lay](D_MODEL)
    return pl.pallas_call(
        _post_body,
        grid=lay["grid"],
        in_specs=[x_spec, abr_spec, on_spec, x_spec, x_spec, _full(wno.shape), _full(wo.shape), _full(g2.shape)],
        out_specs=[o_spec, o_spec],
        out_shape=[jax.ShapeDtypeStruct(o_shape, F32), jax.ShapeDtypeStruct(o_shape, BF16)],
        compiler_params=_cparams(("arbitrary",) * len(lay["grid"]), VMEM_SMALL_MIB),
        name="post",
    )(x2d.reshape(x_shape), abr.reshape(abr_shape), onsa.reshape(on_shape), ga, gb, wno, wo, g2)


def _route(logits):
    lane = lax.broadcasted_iota(jnp.int32, logits.shape, 1).astype(F32)
    big = float(LANES)
    glog = jnp.where(lane < N_EXPERT_GROUPS, logits, -jnp.inf)
    gmax = jnp.max(glog, axis=1, keepdims=True)
    gsel = jnp.min(jnp.where(glog == gmax, lane, big), axis=1, keepdims=True)
    gw = 1.0 / jnp.sum(jnp.exp(glog - gmax), axis=1, keepdims=True)
    lo = N_EXPERT_GROUPS + EXPERTS_PER_GROUP * gsel
    el = jnp.where((lane >= lo) & (lane < lo + EXPERTS_PER_GROUP), logits, -jnp.inf)
    v1 = jnp.max(el, axis=1, keepdims=True)
    i1 = jnp.min(jnp.where(el == v1, lane, big), axis=1, keepdims=True)
    el2 = jnp.where(lane == i1, -jnp.inf, el)
    v2 = jnp.max(el2, axis=1, keepdims=True)
    i2 = jnp.min(jnp.where(el2 == v2, lane, big), axis=1, keepdims=True)
    e2 = jnp.exp(v2 - v1)
    w1 = gw / (1.0 + e2)
    return jnp.where(lane == i1, w1, 0.0) + jnp.where(lane == i2, w1 * e2, 0.0)


def _moe_body(x1_ref, h2_ref, p_ref, wr_ref, br_ref, wg_ref, wu_ref, wd_ref, wpg_ref, wp_ref, gf_ref,
              y_ref, acc_ref, comb_ref, *, tsplit):
    g = pl.program_id(1)
    h2 = h2_ref[...]

    @pl.when(g == 0)
    def _():
        logits = jnp.dot(h2, wr_ref[...], preferred_element_type=F32) + br_ref[...]
        comb_ref[...] = _route(logits)
        acc_ref[...] = jnp.zeros_like(acc_ref)

    comb = comb_ref[...]
    lane = lax.broadcasted_iota(jnp.int32, comb.shape, 1)
    acc = acc_ref[...]
    for k in range(EXPERTS_PER_GROUP):
        e_lane = N_EXPERT_GROUPS + EXPERTS_PER_GROUP * g + k
        ce = jnp.sum(jnp.where(lane == e_lane, comb, 0.0), axis=1, keepdims=True)
        a = jnp.dot(h2, wg_ref[k], preferred_element_type=F32)
        b = jnp.dot(h2, wu_ref[k], preferred_element_type=F32)
        act = (jax.nn.silu(a) * b * ce).astype(BF16)
        acc = acc + jnp.dot(act, wd_ref[k], preferred_element_type=F32)
    acc_ref[...] = acc

    @pl.when(g == N_EXPERT_GROUPS - 1)
    def _():
        x2 = x1_ref[...] + acc_ref[...]
        rows = x2.shape[0] // tsplit
        if tsplit == 1:
            p = p_ref[...]
        else:
            p = jnp.concatenate([p_ref[:, PLE_DIM * t:PLE_DIM * (t + 1)] for t in range(tsplit)], axis=0)
        gate = jax.nn.sigmoid(jnp.dot(x2.astype(BF16), wpg_ref[...], preferred_element_type=F32))
        x3 = x2 + gate * jnp.dot(p.astype(BF16), wp_ref[...], preferred_element_type=F32)
        inv = lax.rsqrt(jnp.mean(x3 * x3, axis=-1, keepdims=True) + RMS_EPS)
        y = x3 * inv * gf_ref[...]
        if tsplit == 1:
            y_ref[...] = y
        else:
            for t in range(tsplit):
                y_ref[:, D_MODEL * t:D_MODEL * (t + 1)] = y[rows * t:rows * (t + 1)]


def _moe(x1, h2, p, mp, tm, tsplit):
    rows = x1.shape[0]
    nrb = rows // tm
    rb = lambda r, g: (r, 0)
    grp = lambda r, g: (g, 0, 0)
    if tsplit == 1:
        p_spec = pl.BlockSpec((tm, PLE_DIM), rb)
        y_spec = pl.BlockSpec((tm, D_MODEL), rb)
        y_shape = (rows, D_MODEL)
    else:
        assert nrb == 1
        p_spec = _full(p.shape)
        y_shape = (rows // tsplit, tsplit * D_MODEL)
        y_spec = _full(y_shape)
    return pl.pallas_call(
        functools.partial(_moe_body, tsplit=tsplit),
        grid=(nrb, N_EXPERT_GROUPS),
        in_specs=[pl.BlockSpec((tm, D_MODEL), rb), pl.BlockSpec((tm, D_MODEL), rb), p_spec,
                  _full(mp["wr"].shape), _full(mp["br"].shape),
                  pl.BlockSpec((EXPERTS_PER_GROUP, D_MODEL, D_FF_EXPERT), grp),
                  pl.BlockSpec((EXPERTS_PER_GROUP, D_MODEL, D_FF_EXPERT), grp),
                  pl.BlockSpec((EXPERTS_PER_GROUP, D_FF_EXPERT, D_MODEL), grp),
                  _full(mp["wpg"].shape), _full(mp["wp"].shape), _full(mp["gf"].shape)],
        out_specs=y_spec,
        out_shape=jax.ShapeDtypeStruct(y_shape, F32),
        scratch_shapes=[pltpu.VMEM((tm, D_MODEL), F32), pltpu.VMEM((tm, LANES), F32)],
        compiler_params=_cparams(("arbitrary", "arbitrary"), VMEM_MOE_MIB),
        name="moe_ple",
    )(x1, h2, p, mp["wr"], mp["br"], mp["wg"], mp["wu"], mp["wd"], mp["wpg"], mp["wp"], mp["gf"])


def _s5_sample_body(u_ref, h0re_ref, h0im_ref, wre_ref, wim_ref, ar_ref, ai_ref, cre_ref, cim_ref, d_ref,
                    wglu_ref, bglu_ref, wso_ref, abr_ref, hre_out_ref, him_out_ref,
                    bure_ref, buim_ref, hre_ref, him_ref, *, nseq, nstep):
    u = u_ref[...]
    ub = u.astype(BF16)
    for j in range(4):
        lhs = ub[:, LANES * j:LANES * (j + 1)]
        bure_ref[:, 512 * j:512 * (j + 1)] = jnp.dot(lhs, wre_ref[j], preferred_element_type=F32)
        buim_ref[:, 512 * j:512 * (j + 1)] = jnp.dot(lhs, wim_ref[j], preferred_element_type=F32)
    for lc in range(4):
        sl = slice(512 * lc, 512 * (lc + 1))
        ar = jnp.broadcast_to(ar_ref[:, sl], (SUBLANES, 512))
        ai = jnp.broadcast_to(ai_ref[:, sl], (SUBLANES, 512))

        def body(rc, carry, sl=sl, ar=ar, ai=ai):
            r0 = pl.multiple_of(rc * SUBLANES, SUBLANES)
            hr = h0re_ref[pl.ds(r0, SUBLANES), sl]
            hi = h0im_ref[pl.ds(r0, SUBLANES), sl]
            for t in range(nstep):
                rr = pl.multiple_of(t * nseq + rc * SUBLANES, SUBLANES)
                hr, hi = (ar * hr - ai * hi + bure_ref[pl.ds(rr, SUBLANES), sl],
                          ar * hi + ai * hr + buim_ref[pl.ds(rr, SUBLANES), sl])
                hre_ref[pl.ds(rr, SUBLANES), sl] = hr
                him_ref[pl.ds(rr, SUBLANES), sl] = hi
            hre_out_ref[pl.ds(r0, SUBLANES), sl] = hr
            him_out_ref[pl.ds(r0, SUBLANES), sl] = hi
            return carry

        lax.fori_loop(0, nseq // SUBLANES, body, 0)
    parts = []
    for j in range(4):
        sl = slice(512 * j, 512 * (j + 1))
        parts.append(jnp.dot(hre_ref[:, sl].astype(BF16), cre_ref[j], preferred_element_type=F32)
                     - jnp.dot(him_ref[:, sl].astype(BF16), cim_ref[j], preferred_element_type=F32))
    y = jnp.concatenate(parts, axis=1) + d_ref[...] * u
    zg = jax.nn.gelu(y)
    gate = jnp.dot(zg.astype(BF16), wglu_ref[...], preferred_element_type=F32) + bglu_ref[...]
    glu = (zg * jax.nn.sigmoid(gate)).astype(BF16)
    abr_ref[...] = jnp.dot(glu, wso_ref[...], preferred_element_type=F32).astype(BF16)


def _s5_sample(u_ts, h0re, h0im, sp, wglu, bglu, wso, nseq, nstep):
    rows = nseq * nstep
    ops = [u_ts, h0re, h0im, sp["wre"], sp["wim"], sp["ar"], sp["ai"], sp["cre"], sp["cim"], sp["d"], wglu, bglu, wso]
    return pl.pallas_call(
        functools.partial(_s5_sample_body, nseq=nseq, nstep=nstep),
        grid=(1,),
        in_specs=[_full(o.shape) for o in ops],
        out_specs=[_full((rows, D_MODEL)), _full((nseq, N_STATE)), _full((nseq, N_STATE))],
        out_shape=[jax.ShapeDtypeStruct((rows, D_MODEL), BF16),
                   jax.ShapeDtypeStruct((nseq, N_STATE), F32), jax.ShapeDtypeStruct((nseq, N_STATE), F32)],
        scratch_shapes=[pltpu.VMEM((rows, N_STATE), F32) for _ in range(4)],
        compiler_params=_cparams(("arbitrary",), VMEM_LARGE_MIB),
        name="s5_sample",
    )(*ops)


def _softmax_rows(s, valid):
    sm = jnp.where(valid, s, NEG_INF)
    mx = jnp.max(sm, axis=1, keepdims=True)
    e = jnp.where(valid, jnp.exp2(sm - mx), 0.0)
    l = jnp.sum(e, axis=1, keepdims=True)
    return e * (1.0 / jnp.maximum(l, 1e-30))


SAMPLE_SEQS_PER_STEP = 4
CMP_PITCH = 24


def _attn_sample_body(pt_ref, q_ref, gn_ref, nks_ref, nkw_ref, wint_ref, wk_ref, bk_ref, w2k_ref, ov_ref, e_ref,
                      cmp_hbm, slc_hbm, o_ref, nwint_ref, xrow_ref, pages_ref, sem_ref, *, npage, past_len, nsub, tq):
    n = pl.program_id(0)
    nsteps = pl.num_programs(0)
    slot = lax.rem(n, 2)
    nrow = N_Q_HEADS * tq
    nwin = wint_ref.shape[2]
    nslc = -(-(past_len + tq) // SLC_BLOCK)
    nch = past_len // CMP_STRIDE
    per_page = PAGE_SIZE // CMP_STRIDE

    def page_copy(step, into, c, s, p):
        src = (cmp_hbm, slc_hbm)[c]
        return pltpu.make_async_copy(src.at[pt_ref[step * nsub + s, p]],
                                     pages_ref.at[into, (c * nsub + s) * npage + p], sem_ref.at[into])

    def all_pages(step, into, op):
        for c in range(2):
            for s in range(nsub):
                for p in range(npage):
                    op(page_copy(step, into, c, s, p))

    @pl.when(n == 0)
    def _():
        all_pages(0, 0, lambda cp: cp.start())

    all_pages(n, slot, lambda cp: cp.wait())
    nxt = jnp.minimum(n + 1, nsteps - 1)
    all_pages(nxt, 1 - slot, lambda cp: cp.start())

    def page(c, s, p):
        return pages_ref.at[slot, (c * nsub + s) * npage + p]

    cmp = []
    for kv in range(2):
        for s in range(nsub):
            for p in range(npage):
                rows = page(0, s, p)[LANES * kv:LANES * (kv + 1), :].T
                for c in range(per_page):
                    r0 = CMP_PITCH * (per_page * p + c)
                    xrow_ref[s, r0:r0 + CMP_STRIDE, :] = rows[CMP_STRIDE * c:CMP_STRIDE * (c + 1)]
        x = jnp.concatenate(
            [jnp.concatenate([xrow_ref[s, pl.ds(i, nch, stride=CMP_PITCH), :] for i in range(CMP_STRIDE)], axis=1)
             for s in range(nsub)], axis=0)
        pp = jnp.dot(x.astype(BF16), wk_ref[kv], preferred_element_type=F32)
        pre = pp[:, 0:LANES] + pltpu.roll(pp[:, LANES:2 * LANES], nsub * nch - 1, axis=0) + bk_ref[kv:kv + 1, :]
        cmp.append(jnp.dot(jax.nn.gelu(pre).astype(BF16), w2k_ref[kv], preferred_element_type=F32).astype(BF16))
    cks = [cmp[0][nch * s:nch * (s + 1)] for s in range(nsub)]
    cvs = [cmp[1][nch * s:nch * (s + 1)] for s in range(nsub)]
    seqs = range(nsub)
    rcat = lambda parts: jnp.concatenate(parts, axis=0)

    lane_w = lax.broadcasted_iota(jnp.int32, (KV_W, LANES), 1)
    lane8 = lax.broadcasted_iota(jnp.int32, (tq, LANES), 1)
    nks_l, nkw_l, wint_l, qs_l = [], [], [], []
    for s in seqs:
        rows_s = slice(tq * s, tq * (s + 1))
        nks_l.append(jnp.concatenate([nks_ref[rows_s, :], jnp.zeros((LANES - tq, KV_W), F32)], axis=0))
        nkw = jnp.concatenate([nkw_ref[rows_s, :], jnp.zeros((LANES - tq, KV_W), F32)], axis=0)
        nkw_l.append(nkw)
        wint = wint_ref[s]
        wint_l.append(wint)
        shifted = pltpu.roll(wint, nwin - tq, axis=1)
        new_t = pltpu.roll(nkw.T, LANES - tq, axis=1)
        nwint_ref[s, :, 0:nwin - LANES] = shifted[:, 0:nwin - LANES]
        nwint_ref[s, :, nwin - LANES:nwin] = jnp.where(lane_w >= LANES - tq, new_t, shifted[:, nwin - LANES:nwin])
        q = q_ref[rows_s, :]
        qrows = []
        for j in range(N_Q_HEADS):
            chunk = q[:, LANES * (j // 2):LANES * (j // 2 + 1)]
            dst = j // GQA
            if (j % 2) != dst:
                chunk = pltpu.roll(chunk, HEAD_DIM, axis=1)
            keep = (lane8 < HEAD_DIM) if dst == 0 else (lane8 >= HEAD_DIM)
            qrows.append(jnp.where(keep, chunk, 0.0))
        qs_l.append(jnp.concatenate(qrows, axis=0).astype(BF16))

    rtot = nsub * nrow
    seq_rows = [slice(nrow * s, nrow * (s + 1)) for s in seqs]
    pos = past_len + (lax.broadcasted_iota(jnp.int32, (rtot, LANES), 0) & (tq - 1))
    lane = lax.broadcasted_iota(jnp.int32, (rtot, LANES), 1)

    sc = rcat([_dot_t(qs_l[s], cks[s]) for s in seqs])
    pc = _softmax_rows(sc, lane * CMP_STRIDE + (CMP_LEN - 1) <= pos).astype(BF16)
    oc = rcat([jnp.dot(pc[seq_rows[s]], cvs[s], preferred_element_type=F32) for s in seqs])
    imp = jnp.dot(pc, ov_ref[...], preferred_element_type=F32)
    vs = []
    for s in seqs:
        for h in range(N_KV_HEADS):
            r0 = nrow * s + tq * GQA * h
            v = imp[r0:r0 + tq]
            for g in range(1, GQA):
                v = v + imp[r0 + tq * g:r0 + tq * (g + 1)]
            vs.append(v)
    nsel = len(vs) * tq
    vt = rcat(vs + [jnp.zeros((LANES - nsel, LANES), F32)]).T
    nblk_pad = -(-nslc // SUBLANES) * SUBLANES
    blk_t = lax.broadcasted_iota(jnp.int32, (nblk_pad, LANES), 0)
    pos_t = past_len + (lax.broadcasted_iota(jnp.int32, (nblk_pad, LANES), 1) & (tq - 1))
    neg_t = _select_blocks(vt[0:nblk_pad], blk_t, pos_t, nslc, axis=0)
    neg = rcat([neg_t, jnp.zeros((LANES - nblk_pad, LANES), F32)]).T
    negsel = rcat([neg[tq * (N_KV_HEADS * s + j // GQA):tq * (N_KV_HEADS * s + j // GQA + 1)]
                   for s in seqs for j in range(N_Q_HEADS)])
    negsel_b = negsel.astype(BF16)

    new_blk = past_len // SLC_BLOCK
    ss_l = []
    for s in seqs:
        qaug = jnp.concatenate([qs_l[s], negsel_b[seq_rows[s]]], axis=1)
        parts = []
        for p in range(0, npage, 2):
            kt = jnp.concatenate([page(1, s, p)[0:LANES, :], page(1, s, p + 1)[0:LANES, :]], axis=1).astype(BF16)
            et = jnp.concatenate([e_ref[p], e_ref[p + 1]], axis=1)
            parts.append(jnp.dot(qaug, jnp.concatenate([kt, et], axis=0), preferred_element_type=F32))
        parts.append(_dot_t(qs_l[s], nks_l[s][:, 0:LANES].astype(BF16)) + negsel[seq_rows[s], new_blk:new_blk + 1])
        ss_l.append(jnp.concatenate(parts, axis=1))
    ss = rcat(ss_l)
    nkeys = ss.shape[1]
    kpos = lax.broadcasted_iota(jnp.int32, (rtot, nkeys), 1)
    pos_k = past_len + (lax.broadcasted_iota(jnp.int32, (rtot, nkeys), 0) & (tq - 1))
    ps = _softmax_rows(ss, kpos <= pos_k).astype(BF16)
    osel_l = []
    for s in seqs:
        psq = ps[seq_rows[s]]
        o = jnp.dot(psq[:, past_len:nkeys], nks_l[s][:, LANES:2 * LANES].astype(BF16), preferred_element_type=F32)
        for p in range(0, npage, 2):
            vtp = jnp.concatenate([page(1, s, p)[LANES:2 * LANES, :], page(1, s, p + 1)[LANES:2 * LANES, :]],
                                  axis=1).astype(BF16)
            o = o + _dot_t(psq[:, PAGE_SIZE * p:PAGE_SIZE * (p + 2)], vtp)
        osel_l.append(o)
    osel = rcat(osel_l)

    sw = rcat([jnp.concatenate([jnp.dot(qs_l[s], wint_l[s][0:LANES].astype(BF16), preferred_element_type=F32),
                                _dot_t(qs_l[s], nkw_l[s][:, 0:LANES].astype(BF16))], axis=1) for s in seqs])
    nw = sw.shape[1]
    widx = lax.broadcasted_iota(jnp.int32, (rtot, nw), 1)
    pos_w = past_len + (lax.broadcasted_iota(jnp.int32, (rtot, nw), 0) & (tq - 1))
    dlt = pos_w - (past_len - nwin + widx)
    pw = _softmax_rows(sw, (dlt >= 0) & (dlt < WINDOW) & (widx < nwin + tq)).astype(BF16)
    ow = rcat([_dot_t(pw[seq_rows[s], 0:nwin], wint_l[s][LANES:2 * LANES].astype(BF16))
               + jnp.dot(pw[seq_rows[s], nwin:nw], nkw_l[s][:, LANES:2 * LANES].astype(BF16),
                         preferred_element_type=F32) for s in seqs])

    for s in seqs:
        rows_s = slice(tq * s, tq * (s + 1))
        gn = gn_ref[rows_s, :]
        for c in range(N_Q_HEADS // 2):
            halves = []
            for hh in range(2):
                j = 2 * c + hh
                rs = slice(nrow * s + tq * j, nrow * s + tq * (j + 1))
                oj = (gn[:, 3 * j:3 * j + 1] * oc[rs] + gn[:, 3 * j + 1:3 * j + 2] * osel[rs]
                      + gn[:, 3 * j + 2:3 * j + 3] * ow[rs])
                if (j // GQA) != hh:
                    oj = pltpu.roll(oj, HEAD_DIM, axis=1)
                halves.append(oj)
            o_ref[rows_s, LANES * c:LANES * (c + 1)] = jnp.where(lane8 < HEAD_DIM, halves[0], halves[1])

    @pl.when(n == nsteps - 1)
    def _():
        all_pages(nxt, 1 - slot, lambda cp: cp.wait())


def _attn_sample(q, gn, nks, nkw, cache_cmp, cache_slc, cache_win, page_table, cp, nseq, tq, past_len):
    assert tq <= CMP_STRIDE and past_len % PAGE_SIZE == 0
    npage = past_len // PAGE_SIZE
    assert npage % 2 == 0 and PAGE_SIZE == LANES
    n_pool = cache_cmp.shape[0]
    nwin = cache_win.shape[1]
    chunks = past_len // CMP_STRIDE
    ov = _overlap_t(chunks, LANES).T
    key = np.arange(past_len).reshape(npage, 1, PAGE_SIZE)
    e = jnp.asarray(np.arange(LANES).reshape(1, LANES, 1) == key // SLC_BLOCK, dtype=BF16)
    to_t = lambda c: jnp.transpose(c, (0, 2, 3, 4, 1)).reshape(c.shape[0], KV_W, c.shape[1])
    cmp_t, slc_t, win_t = to_t(cache_cmp), to_t(cache_slc), to_t(cache_win)
    nsub = SAMPLE_SEQS_PER_STEP
    assert nseq % nsub == 0
    row = lambda n, pt: (n, 0)
    seq3 = lambda n, pt: (n, 0, 0)
    consts = [cp["wk"], cp["bk"], cp["w2k"], ov, e]
    in_specs = [pl.BlockSpec((nsub * tq, Q_W), row), pl.BlockSpec((nsub * tq, LANES), row),
                pl.BlockSpec((nsub * tq, KV_W), row), pl.BlockSpec((nsub * tq, KV_W), row),
                pl.BlockSpec((nsub, KV_W, nwin), seq3)]
    in_specs += [pl.BlockSpec(c.shape, (lambda nd: lambda n, pt: (0,) * nd)(c.ndim)) for c in consts]
    in_specs += [pl.BlockSpec(memory_space=pl.ANY)] * 2
    grid_spec = pltpu.PrefetchScalarGridSpec(
        num_scalar_prefetch=1,
        grid=(nseq // nsub,),
        in_specs=in_specs,
        out_specs=[pl.BlockSpec((nsub * tq, Q_W), row), pl.BlockSpec((nsub, KV_W, nwin), seq3)],
        scratch_shapes=[pltpu.VMEM((nsub, chunks * CMP_PITCH, LANES), F32),
                        pltpu.VMEM((2, 2 * nsub * npage, KV_W, PAGE_SIZE), F32),
                        pltpu.SemaphoreType.DMA((2,))],
    )
    return pl.pallas_call(
        functools.partial(_attn_sample_body, npage=npage, past_len=past_len, nsub=nsub, tq=tq),
        grid_spec=grid_spec,
        out_shape=[jax.ShapeDtypeStruct((nseq * tq, Q_W), F32), jax.ShapeDtypeStruct((nseq, KV_W, nwin), F32)],
        compiler_params=_cparams(("arbitrary",), VMEM_LARGE_MIB),
        name="attn_sample",
    )(page_table, q, gn, nks, nkw, win_t, *consts, cmp_t, slc_t)


def _moe_params(w_rg, b_rg, w_re, b_re, w_gate, w_up, w_down, w_ple, w_ple_gate, gf):
    pad = LANES - N_EXPERT_GROUPS - N_EXPERTS
    return {"wr": jnp.pad(jnp.concatenate([w_rg, w_re], axis=1), ((0, 0), (0, pad))).astype(BF16),
            "br": jnp.pad(jnp.concatenate([b_rg, b_re]), (0, pad)).astype(F32).reshape(1, LANES),
            "wg": w_gate.astype(BF16), "wu": w_up.astype(BF16), "wd": w_down.astype(BF16),
            "wpg": w_ple_gate.astype(BF16), "wp": w_ple.astype(BF16), "gf": gf.astype(F32).reshape(1, D_MODEL)}


TM_PROMPT = 512
TM_MOE = 1024
TC_S5 = 128


def kernel(x_prompt, x_sample, p_prompt, p_sample, cache_cmp_kv, cache_slc_kv, cache_win_kv, state_ssm, page_table, norm1_g, w_in, ssm_lam_re, ssm_lam_im, ssm_log_dt, ssm_b_re, ssm_b_im, ssm_c_re, ssm_c_im, ssm_d, w_glu, b_glu, cmp_pe, cmp_w1, cmp_w2, w_ssm_out, w_nsa_out, w_o, norm2_g, w_route_group, b_route_group, w_route_expert, b_route_expert, w_exp_gate, w_exp_up, w_exp_down, w_ple, w_ple_gate, final_norm_g):
    assert w_in.shape[0] == 1, "one layer"
    l = 0
    nb, t = x_prompt.shape[:2]
    ns, ts = x_sample.shape[:2]
    past_len = page_table.shape[1] * PAGE_SIZE
    kvt = (2, N_KV_HEADS, HEAD_DIM)

    wi = _inproj_params(w_in[l])
    g1 = norm1_g[l].astype(F32).reshape(1, D_MODEL)
    g2 = norm2_g[l].astype(F32).reshape(1, D_MODEL)
    sp = _s5_params(ssm_lam_re[l], ssm_lam_im[l], ssm_log_dt[l], ssm_b_re[l], ssm_b_im[l], ssm_c_re[l], ssm_c_im[l],
                    ssm_d[l])
    cp = _cmp_params(cmp_pe[l], cmp_w1[l], cmp_w2[l])
    mp = _moe_params(w_route_group[l], b_route_group[l], w_route_expert[l], b_route_expert[l], w_exp_gate[l],
                     w_exp_up[l], w_exp_down[l], w_ple[l], w_ple_gate[l], final_norm_g)
    wglu = w_glu[l].astype(BF16)
    bglu = b_glu[l].astype(F32).reshape(1, SSM_WIDTH)
    wso = w_ssm_out[l].astype(BF16)
    wno = w_nsa_out[l].astype(BF16)
    wo = w_o[l].astype(BF16)

    lay = _prompt_layout(nb, t, TM_PROMPT)
    xp = x_prompt.reshape(nb * t, D_MODEL)
    r = _inproj_prompt(xp, lay, g1, wi)
    abr, hlast = _s5_prompt(r["u"], sp, wglu, bglu, wso, t, TC_S5)
    ck, cvt = _compress_prompt(r["kvc"], cp, nb, t)
    onsa = _attn_prompt(r["qt"], r["gnt"], ck, cvt, r["ksb"], r["vst"], r["kwb"], r["vwt"], nb, t)
    x1, h2 = _post(xp, abr, "a", onsa, r["ga"], r["gb"], wno, wo, g2, lay, "a")
    y_prompt = _moe(x1, h2, p_prompt[l].reshape(nb * t, PLE_DIM), mp, TM_MOE, 1).reshape(nb, t, D_MODEL)
    keep = min(WINDOW, t)

    def rows_last(a):
        return jnp.transpose(a.reshape((a.shape[0],) + kvt + (a.shape[2],)), (0, 4, 1, 2, 3))[None]

    new_cmp_p = rows_last(r["kvct"])
    new_slc_p = rows_last(r["kvst"])
    new_win_p = rows_last(r["kvwt"][:, :, t - keep:])
    new_ssm_p = jnp.stack([hlast[0:nb], hlast[nb:2 * nb]], axis=-1).reshape(1, nb, N_SSM_GROUPS, SSM_STATE, 2)

    lays = _sample_layout(ns, ts)
    xs = x_sample.reshape(ns * ts, D_MODEL)
    rs = _inproj_sample(xs, lays, g1, wi)
    h0 = state_ssm[l].astype(F32).reshape(ns, N_STATE, 2)
    abr_s, hre, him = _s5_sample(rs["u"], h0[..., 0], h0[..., 1], sp, wglu, bglu, wso, ns, ts)
    onsa_s, new_win = _attn_sample(rs["q"].reshape(ns * ts, Q_W), rs["gn"].reshape(ns * ts, LANES),
                                   rs["kvs"].reshape(ns * ts, KV_W), rs["kvw"].reshape(ns * ts, KV_W),
                                   cache_cmp_kv[l], cache_slc_kv[l], cache_win_kv[l], page_table, cp, ns, ts, past_len)
    x1s, h2s = _post(xs, abr_s, "b", onsa_s, rs["ga"], rs["gb"], wno, wo, g2, lays, "b")
    y_sample = _moe(x1s, h2s, p_sample[l].reshape(ns, ts * PLE_DIM), mp, ns * ts, ts).reshape(ns, ts, D_MODEL)
    steps_first = lambda a: jnp.transpose(a.reshape((ts,) + kvt + (ns,)), (4, 0, 1, 2, 3))[None]
    new_cmp_s = steps_first(rs["kvct"])
    new_slc_s = steps_first(rs["kvst"])
    new_win_s = rows_last(new_win)
    new_ssm_s = jnp.stack([hre, him], axis=-1).reshape(1, ns, N_SSM_GROUPS, SSM_STATE, 2)
    return (y_prompt, y_sample, new_cmp_p, new_slc_p, new_win_p, new_ssm_p,
            new_cmp_s, new_slc_s, new_win_s, new_ssm_s)
```

```python
import functools
import math

import jax
import jax.numpy as jnp
import numpy as np
from jax import lax
from jax.experimental import pallas as pl
from jax.experimental.pallas import tpu as pltpu

F32 = jnp.float32
BF16 = jnp.bfloat16

D_MODEL = 1024
SSM_WIDTH = 512
SSM_GROUP = 16
N_SSM_GROUPS = 32
SSM_STATE = 64
HEAD_DIM = 64
N_Q_HEADS = 8
N_KV_HEADS = 2
GQA = 4
CMP_LEN = 32
CMP_STRIDE = 16
SLC_BLOCK = 64
TOP_N = 8
WINDOW = 512
Q_BLOCK = 256
NEG_INF = -1e30
FORCE_BONUS = 1e4
Q_W = 512
KV_W = 256
NSA_GATE_W = 24
N_EXPERT_GROUPS = 4
EXPERTS_PER_GROUP = 4
N_EXPERTS = 16
D_FF_EXPERT = 256
PLE_DIM = 256
RMS_EPS = 1e-6
PAGE_SIZE = 128

LANES = 128
SUBLANES = 8
N_STATE = N_SSM_GROUPS * SSM_STATE
MIB = 2 ** 20
V7X_VMEM_MIB = 64
VMEM_SMALL_MIB = 48
VMEM_LARGE_MIB = 56
VMEM_MOE_MIB = V7X_VMEM_MIB - 4


def _cparams(sem, vmem_mib):
    return pltpu.CompilerParams(dimension_semantics=sem, vmem_limit_bytes=vmem_mib * MIB)


def _full(shape):
    nd = len(shape)
    return pl.BlockSpec(shape, lambda *_: (0,) * nd)


def _prompt_layout(nseq, t, tm):
    nb = t // tm
    return {
        "grid": (nb, nseq), "tm": tm, "nseq": nseq, "t": t,
        "a": lambda w: ((nseq * t, w), pl.BlockSpec((tm, w), lambda b, s: (s * nb + b, 0))),
    }


def _sample_layout(nseq, t):
    return {
        "grid": (1, t), "tm": nseq,
        "a": lambda w: ((nseq, t * w), pl.BlockSpec((nseq, w), lambda s, b: (0, b))),
        "b": lambda w: ((t * nseq, w), pl.BlockSpec((nseq, w), lambda s, b: (b, 0))),
    }


TK_SLC = 512
TK_WIN = 128


Q_SCALE = HEAD_DIM ** -0.5 * math.log2(math.e)
C_U, C_Q, C_KVC, C_KVS, C_KVW = 0, 512, 1024, 1280, 1536
N_MAIN = 1792


def _dot_t(a, b):
    return lax.dot_general(a, b, (((1,), (1,)), ((), ())), preferred_element_type=F32)


GN_ROWS = 32


def _inproj_prompt_body(x_ref, g_ref, wa_ref, wgn_ref, wgab_ref,
                        u_ref, kvc_ref, ksb_ref, kwb_ref, ga_ref, gb_ref,
                        qt_ref, kvct_ref, kvst_ref, kvwt_ref, gnt_ref, vst_ref, vwt_ref, *, nseq):
    s = pl.program_id(1)
    x = x_ref[...]
    inv = lax.rsqrt(jnp.mean(x * x, axis=-1, keepdims=True) + RMS_EPS)
    h = (x * inv * g_ref[...]).astype(BF16)
    tm = h.shape[0]

    def mm(w):
        return jnp.dot(h, w, preferred_element_type=F32)

    u = mm(wa_ref[:, C_U:C_U + SSM_WIDTH])
    for j in range(SSM_WIDTH // LANES):
        u_ref[j, pl.ds(s, tm, stride=nseq), :] = u[:, LANES * j:LANES * (j + 1)]
    ga_ref[...] = jax.nn.sigmoid(mm(wgab_ref[:, 0:D_MODEL])).astype(BF16)
    gb_ref[...] = jax.nn.sigmoid(mm(wgab_ref[:, D_MODEL:2 * D_MODEL])).astype(BF16)
    kvc = mm(wa_ref[:, C_KVC:C_KVC + KV_W])
    kvc_ref[0] = kvc[:, 0:LANES]
    kvc_ref[1] = kvc[:, LANES:2 * LANES]
    kvct_ref[0] = kvc.T
    kvs = mm(wa_ref[:, C_KVS:C_KVS + KV_W])
    ksb_ref[...] = kvs[:, 0:LANES].astype(BF16)
    kvst = kvs.T
    kvst_ref[0] = kvst
    kvw = mm(wa_ref[:, C_KVW:C_KVW + KV_W])
    kwb_ref[...] = kvw[:, 0:LANES].astype(BF16)
    kvwt = kvw.T
    kvwt_ref[0] = kvwt
    for c in range(tm // TK_WIN):
        vst_ref[0, c] = kvst[LANES:2 * LANES, c * TK_WIN:(c + 1) * TK_WIN].astype(BF16)
        vwt_ref[0, c] = kvwt[LANES:2 * LANES, c * TK_WIN:(c + 1) * TK_WIN].astype(BF16)
    qt_ref[0] = (mm(wa_ref[:, C_Q:C_Q + Q_W]) * Q_SCALE).T.astype(BF16)
    gnt_ref[0] = jax.nn.sigmoid(mm(wgn_ref[...])).T[0:GN_ROWS]


def _inproj_prompt(x2d, lay, g, w):
    tm, nseq, t = lay["tm"], lay["nseq"], lay["t"]
    nb = t // tm
    out_shapes, out_specs, names = [], [], []

    def add(name, shape_spec, dt):
        names.append(name)
        out_shapes.append(jax.ShapeDtypeStruct(shape_spec[0], dt))
        out_specs.append(shape_spec[1])

    def tr(rows):
        return (nseq, rows, t), pl.BlockSpec((1, rows, tm), lambda b, s: (s, 0, b))

    nu = SSM_WIDTH // LANES
    add("u", ((nu, t * nseq, LANES), pl.BlockSpec((nu, tm * nseq, LANES), lambda b, s: (0, b, 0))), F32)
    add("kvc", ((2, nseq * t, LANES), pl.BlockSpec((2, tm, LANES), lambda b, s: (0, s * nb + b, 0))), F32)
    add("ksb", lay["a"](LANES), BF16)
    add("kwb", lay["a"](LANES), BF16)
    add("ga", lay["a"](D_MODEL), BF16)
    add("gb", lay["a"](D_MODEL), BF16)
    add("qt", tr(Q_W), BF16)
    add("kvct", tr(KV_W), F32)
    add("kvst", tr(KV_W), F32)
    add("kvwt", tr(KV_W), F32)
    add("gnt", tr(GN_ROWS), F32)
    for name in ("vst", "vwt"):
        add(name, ((nseq, t // TK_WIN, LANES, TK_WIN),
                   pl.BlockSpec((1, tm // TK_WIN, LANES, TK_WIN), lambda b, s: (s, b, 0, 0))), BF16)
    x_shape, x_spec = lay["a"](D_MODEL)
    ops = [g, w["wa"], w["wgn"], w["wgab"]]
    outs = pl.pallas_call(
        functools.partial(_inproj_prompt_body, nseq=nseq),
        grid=lay["grid"],
        in_specs=[x_spec] + [_full(o.shape) for o in ops],
        out_specs=out_specs,
        out_shape=out_shapes,
        compiler_params=_cparams(("arbitrary",) * 2, VMEM_LARGE_MIB),
        name="inproj_prompt",
    )(x2d.reshape(x_shape), *ops)
    return dict(zip(names, outs))


def _inproj_sample_body(x_ref, g_ref, wa_ref, wgn_ref, wgab_ref,
                        u_ref, q_ref, kvs_ref, kvw_ref, gn_ref, ga_ref, gb_ref, kvct_ref, kvst_ref, kvwt_ref):
    x = x_ref[...]
    inv = lax.rsqrt(jnp.mean(x * x, axis=-1, keepdims=True) + RMS_EPS)
    h = (x * inv * g_ref[...]).astype(BF16)

    def mm(w):
        return jnp.dot(h, w, preferred_element_type=F32)

    u_ref[...] = mm(wa_ref[:, C_U:C_U + SSM_WIDTH])
    q_ref[...] = mm(wa_ref[:, C_Q:C_Q + Q_W]) * Q_SCALE
    kvs = mm(wa_ref[:, C_KVS:C_KVS + KV_W])
    kvs_ref[...] = kvs
    kvw = mm(wa_ref[:, C_KVW:C_KVW + KV_W])
    kvw_ref[...] = kvw
    gn_ref[...] = jax.nn.sigmoid(mm(wgn_ref[...]))
    ga_ref[...] = jax.nn.sigmoid(mm(wgab_ref[:, 0:D_MODEL])).astype(BF16)
    gb_ref[...] = jax.nn.sigmoid(mm(wgab_ref[:, D_MODEL:2 * D_MODEL])).astype(BF16)
    kvct_ref[0] = mm(wa_ref[:, C_KVC:C_KVC + KV_W]).T
    kvst_ref[0] = kvs.T
    kvwt_ref[0] = kvw.T


def _inproj_sample(x2d, lay, g, w):
    nseq = lay["tm"]
    ts = lay["grid"][1]
    names = ["u", "q", "kvs", "kvw", "gn", "ga", "gb"]
    widths = [SSM_WIDTH, Q_W, KV_W, KV_W, LANES, D_MODEL, D_MODEL]
    out_shapes, out_specs = [], []
    for n, wd in zip(names, widths):
        shp, spec = lay["b" if n == "u" else "a"](wd)
        out_shapes.append(jax.ShapeDtypeStruct(shp, BF16 if n in ("ga", "gb") else F32))
        out_specs.append(spec)
    for n in ("kvct", "kvst", "kvwt"):
        names.append(n)
        out_shapes.append(jax.ShapeDtypeStruct((ts, KV_W, nseq), F32))
        out_specs.append(pl.BlockSpec((1, KV_W, nseq), lambda s, b: (b, 0, 0)))
    x_shape, x_spec = lay["a"](D_MODEL)
    ops = [g, w["wa"], w["wgn"], w["wgab"]]
    outs = pl.pallas_call(
        _inproj_sample_body,
        grid=lay["grid"],
        in_specs=[x_spec] + [_full(o.shape) for o in ops],
        out_specs=out_specs,
        out_shape=out_shapes,
        compiler_params=_cparams(("arbitrary",) * 2, VMEM_LARGE_MIB),
        name="inproj_sample",
    )(x2d.reshape(x_shape), *ops)
    return dict(zip(names, outs))


def _inproj_params(w_in0):
    return {"wa": w_in0[:, :N_MAIN].astype(BF16),
            "wgn": jnp.pad(w_in0[:, N_MAIN:N_MAIN + NSA_GATE_W], ((0, 0), (0, LANES - NSA_GATE_W))).astype(BF16),
            "wgab": w_in0[:, N_MAIN + NSA_GATE_W:].astype(BF16)}


S5_LANES = 1024


def _s5_prompt_body(u_ref, wb_ref, ar_ref, ai_ref, cw_ref, d_ref, wglu_ref, bglu_ref, wso_ref,
                    abr_ref, hlast_ref, lhs_ref, bu_ref, h8_ref, p_ref, hstate_ref):
    c = pl.program_id(0)
    nseq = 4
    r4 = u_ref.shape[1]
    tc = r4 // nseq
    half = tc // 2

    @pl.when(c == 0)
    def _():
        hstate_ref[...] = jnp.zeros_like(hstate_ref)

    u = jnp.concatenate([u_ref[j] for j in range(SSM_WIDTH // LANES)], axis=1)
    row2 = lax.broadcasted_iota(jnp.int32, (r4, SSM_WIDTH), 0)
    lo2 = (row2 % SUBLANES) < nseq
    up = pltpu.roll(u, r4 - nseq, axis=0)
    dn = pltpu.roll(u, nseq, axis=0)
    swapped = jnp.where(lo2, up, dn)
    zero = jnp.zeros_like(u)
    ev_re = jnp.where(lo2, u, zero).astype(BF16).reshape(half, SUBLANES, SSM_WIDTH)
    ev_im = jnp.where(lo2, zero, swapped).astype(BF16).reshape(half, SUBLANES, SSM_WIDTH)
    od_re = jnp.where(lo2, swapped, zero).astype(BF16).reshape(half, SUBLANES, SSM_WIDTH)
    od_im = jnp.where(lo2, zero, u).astype(BF16).reshape(half, SUBLANES, SSM_WIDTH)
    for j in range(4):
        sl = slice(LANES * j, LANES * (j + 1))
        lhs_ref[:, 0:8, 256 * j:256 * j + LANES] = ev_re[:, :, sl]
        lhs_ref[:, 0:8, 256 * j + LANES:256 * (j + 1)] = ev_im[:, :, sl]
        lhs_ref[:, 8:16, 256 * j:256 * j + LANES] = od_re[:, :, sl]
        lhs_ref[:, 8:16, 256 * j + LANES:256 * (j + 1)] = od_im[:, :, sl]
    for j in range(4):
        lhs = lhs_ref[:, :, 256 * j:256 * (j + 1)].reshape(tc * SUBLANES, 256)
        bu_ref[:, 512 * j:512 * (j + 1)] = jnp.dot(lhs, wb_ref[j], preferred_element_type=F32)

    for lc in range(N_STATE // S5_LANES):
        sl = slice(S5_LANES * lc, S5_LANES * (lc + 1))
        ar = ar_ref[:, sl]
        ai = ai_ref[:, sl]

        def step(t, h, sl=sl, ar=ar, ai=ai):
            r0 = pl.multiple_of(t * SUBLANES, SUBLANES)
            h = ar * h + ai * pltpu.roll(h, nseq, axis=0) + bu_ref[pl.ds(r0, SUBLANES), sl]
            h8_ref[pl.ds(r0, SUBLANES), sl] = h
            return h

        hstate_ref[:, sl] = lax.fori_loop(0, tc, step, hstate_ref[:, sl], unroll=8)
    hlast_ref[...] = hstate_ref[...]

    for j in range(4):
        pj = jnp.dot(h8_ref[:, 512 * j:512 * (j + 1)].astype(BF16), cw_ref[j], preferred_element_type=F32)
        p_ref[2 * j] = pj[:, 0:LANES]
        p_ref[2 * j + 1] = pj[:, LANES:2 * LANES]
    ys = []
    for s in range(nseq):
        parts = []
        for j in range(4):
            re = p_ref[2 * j, pl.ds(s, tc, stride=SUBLANES), :]
            im = p_ref[2 * j + 1, pl.ds(nseq + s, tc, stride=SUBLANES), :]
            us = u_ref[j, pl.ds(s, tc, stride=nseq), :]
            parts.append(re + im + d_ref[:, LANES * j:LANES * (j + 1)] * us)
        ys.append(jnp.concatenate(parts, axis=1))
    y = jnp.concatenate(ys, axis=0)
    zg = jax.nn.gelu(y)
    gate = jnp.dot(zg.astype(BF16), wglu_ref[...], preferred_element_type=F32) + bglu_ref[...]
    glu = (zg * jax.nn.sigmoid(gate)).astype(BF16)
    abr = jnp.dot(glu, wso_ref[...], preferred_element_type=F32)
    for s in range(nseq):
        abr_ref[s] = abr[s * tc:(s + 1) * tc].astype(BF16)


def _s5_prompt(u_ts, sp, wglu, bglu, wso, t_total, tc):
    nseq = 4
    grid = (t_total // tc,)
    abr, hlast = pl.pallas_call(
        _s5_prompt_body,
        grid=grid,
        in_specs=[pl.BlockSpec((SSM_WIDTH // LANES, tc * nseq, LANES), lambda c: (0, c, 0)),
                  _full(sp["wb8"].shape), _full(sp["ar8"].shape), _full(sp["ai8"].shape), _full(sp["cw8"].shape),
                  _full(sp["d"].shape), _full(wglu.shape), _full(bglu.shape), _full(wso.shape)],
        out_specs=[pl.BlockSpec((nseq, tc, D_MODEL), lambda c: (0, c, 0)),
                   pl.BlockSpec((SUBLANES, N_STATE), lambda c: (0, 0))],
        out_shape=[jax.ShapeDtypeStruct((nseq, t_total, D_MODEL), BF16),
                   jax.ShapeDtypeStruct((SUBLANES, N_STATE), F32)],
        scratch_shapes=[pltpu.VMEM((tc // 2, 2 * SUBLANES, 1024), BF16),
                        pltpu.VMEM((tc * SUBLANES, N_STATE), F32),
                        pltpu.VMEM((tc * SUBLANES, N_STATE), F32),
                        pltpu.VMEM((8, tc * SUBLANES, LANES), F32),
                        pltpu.VMEM((SUBLANES, N_STATE), F32)],
        compiler_params=_cparams(("arbitrary",), VMEM_LARGE_MIB),
        name="s5_prompt",
    )(u_ts, sp["wb8"], sp["ar8"], sp["ai8"], sp["cw8"], sp["d"], wglu, bglu, wso)
    return abr, hlast


def _s5_params(lam_re, lam_im, log_dt, b_re, b_im, c_re, c_im, d_skip):
    lam = lax.complex(lam_re.astype(F32), lam_im.astype(F32))
    dt = jnp.exp(log_dt.astype(F32))[:, None]
    a_bar = jnp.exp(lam * dt)
    b = lax.complex(b_re.astype(F32), b_im.astype(F32))
    b_bar = ((a_bar - 1.0) / lam)[..., None] * b
    eye8 = jnp.eye(8, dtype=F32)

    def bd_b(m):
        return jnp.einsum("ab,jbpc->jacbp", eye8, m.reshape(4, 8, SSM_STATE, SSM_GROUP)).reshape(4, 128, 512)

    def bd_c(m):
        return jnp.einsum("ab,jbcp->japbc", eye8, m.reshape(4, 8, SSM_GROUP, SSM_STATE)).reshape(4, 512, 128)

    wre, wim = bd_b(b_bar.real), bd_b(b_bar.imag)
    cre, cim = bd_c(c_re.astype(F32)), bd_c(c_im.astype(F32))
    ar = a_bar.real.reshape(1, N_STATE)
    ai = a_bar.imag.reshape(1, N_STATE)
    sign = jnp.concatenate([-jnp.ones((4, 1), F32), jnp.ones((4, 1), F32)], axis=0)
    return {
        "wb8": jnp.concatenate([wre, wim], axis=1).astype(BF16),
        "cw8": jnp.concatenate([cre, -cim], axis=2).astype(BF16),
        "ar8": jnp.broadcast_to(ar, (SUBLANES, N_STATE)),
        "ai8": sign * ai,
        "wre": wre.astype(BF16), "wim": wim.astype(BF16),
        "cre": cre.astype(BF16), "cim": cim.astype(BF16),
        "ar": ar, "ai": ai,
        "d": d_skip.astype(F32).reshape(1, SSM_WIDTH),
    }


def _cmp_params(cmp_pe, cmp_w1, cmp_w2):
    eye2 = jnp.eye(2, dtype=F32)
    nhalf = CMP_LEN // CMP_STRIDE
    w1r = cmp_w1.astype(F32).reshape(2, nhalf, CMP_STRIDE, HEAD_DIM, HEAD_DIM)
    wk = jnp.einsum("kside,ph->kipdshe", w1r, eye2).reshape(2, CMP_STRIDE * LANES, nhalf * LANES)
    bk = jnp.einsum("kld,klde->ke", cmp_pe.astype(F32), cmp_w1.astype(F32), precision=lax.Precision.HIGHEST)
    w2k = jnp.einsum("kef,ph->kpehf", cmp_w2.astype(F32), eye2).reshape(2, LANES, LANES)
    return {"wk": wk.astype(BF16), "bk": jnp.tile(bk, (1, N_KV_HEADS)), "w2k": w2k.astype(BF16),
            "w2kt": jnp.swapaxes(w2k, 1, 2).astype(BF16)}


def _compress_hidden(tap, nch, kv, wk_ref, bk_ref):
    x = jnp.concatenate([tap(i).astype(BF16) for i in range(CMP_STRIDE)], axis=1)
    pp = jnp.dot(x, wk_ref[kv], preferred_element_type=F32)
    pre = pp[:, 0:LANES] + pltpu.roll(pp[:, LANES:2 * LANES], nch - 1, axis=0) + bk_ref[kv:kv + 1, :]
    return jax.nn.gelu(pre).astype(BF16)


def _compress_prompt_body(x_ref, wk_ref, bk_ref, w2k_ref, w2kt_ref, ck_ref, cvt_ref):
    nch = x_ref.shape[1] // CMP_STRIDE
    hid = [_compress_hidden(lambda i, kv=kv: x_ref[kv, pl.ds(i, nch, stride=CMP_STRIDE), :], nch, kv, wk_ref, bk_ref)
           for kv in range(2)]
    ck_ref[0] = jnp.dot(hid[0], w2k_ref[0], preferred_element_type=F32).astype(BF16)
    cvt_ref[0] = _dot_t(w2kt_ref[1], hid[1]).astype(BF16)


def _compress_prompt(kvc2, cp, nseq, t):
    nch = t // CMP_STRIDE
    return pl.pallas_call(
        _compress_prompt_body,
        grid=(nseq,),
        in_specs=[pl.BlockSpec((2, t, LANES), lambda n: (0, n, 0)),
                  _full(cp["wk"].shape), _full(cp["bk"].shape), _full(cp["w2k"].shape), _full(cp["w2kt"].shape)],
        out_specs=[pl.BlockSpec((1, nch, LANES), lambda n: (n, 0, 0)),
                   pl.BlockSpec((1, LANES, nch), lambda n: (n, 0, 0))],
        out_shape=[jax.ShapeDtypeStruct((nseq, nch, LANES), BF16),
                   jax.ShapeDtypeStruct((nseq, LANES, nch), BF16)],
        compiler_params=_cparams(("arbitrary",), VMEM_SMALL_MIB),
        name="compress_prompt",
    )(kvc2, cp["wk"], cp["bk"], cp["w2k"], cp["w2kt"])


def _overlap_t(n_cmp_pad, n_slc_pad):
    j = np.arange(n_cmp_pad)[None, :]
    s = np.arange(n_slc_pad)[:, None]
    ov = (j * CMP_STRIDE <= s * SLC_BLOCK + SLC_BLOCK - 1) & (j * CMP_STRIDE + CMP_LEN - 1 >= s * SLC_BLOCK)
    return jnp.asarray(ov, dtype=BF16)


def _softmax_cols(s, valid):
    sm = jnp.where(valid, s, NEG_INF)
    mx = jnp.max(sm, axis=0, keepdims=True)
    e = jnp.where(valid, jnp.exp2(sm - mx), 0.0)
    l = jnp.sum(e, axis=0, keepdims=True)
    return e * (1.0 / jnp.maximum(l, 1e-30))


def _select_blocks(imp, blk, pos, nblk, axis=0):
    cur = pos // SLC_BLOCK
    forced = (blk == 0) | (blk == cur) | (blk == cur - 1)
    v = jnp.where(forced, imp + FORCE_BONUS, imp)
    v = jnp.where(blk * SLC_BLOCK <= pos, v, NEG_INF)
    v = jnp.where(blk < nblk, v, -3e38)
    blk_f = blk.astype(F32)
    neg = jnp.full(imp.shape, NEG_INF, F32)
    for _ in range(min(TOP_N, nblk)):
        mx = jnp.max(v, axis=axis, keepdims=True)
        first = jnp.min(jnp.where(v == mx, blk_f, float(imp.shape[axis])), axis=axis, keepdims=True)
        pick = blk_f == first
        neg = jnp.where(pick, 0.0, neg)
        v = jnp.where(pick, -3e38, v)
    return neg


CB = 2 * LANES


def _attn_prompt_body(q_ref, gn_ref, ck_ref, cvt_ref, ks_ref, vst_ref, kw_ref, vwt_ref, ovt_ref,
                      o_ref, kaug_ref, kwaug_ref, qaug_ref, acc_ref):
    i = pl.program_id(1)
    t = ks_ref.shape[1]
    nch = ck_ref.shape[1]
    nslc = t // SLC_BLOCK
    qb = Q_BLOCK
    ncol = N_Q_HEADS * qb
    ncb = ncol // CB
    q0 = i * qb
    one_row = 2 * LANES - HEAD_DIM
    hrows = [slice(HEAD_DIM * ((CB * cb // qb) // GQA), HEAD_DIM * ((CB * cb // qb) // GQA + 1)) for cb in range(ncb)]

    @pl.when(i == 0)
    def _():
        kaug_ref[:, 0:LANES] = ks_ref[0]
        blk = lax.broadcasted_iota(jnp.int32, (t, LANES), 0) // SLC_BLOCK
        col = lax.broadcasted_iota(jnp.int32, (t, LANES), 1)
        kaug_ref[:, LANES:2 * LANES] = jnp.where(blk == col, 1.0, 0.0).astype(BF16)
        padcol = lax.broadcasted_iota(jnp.int32, (WINDOW, 2 * LANES), 1)
        kwaug_ref[0:WINDOW, :] = jnp.where(padcol == one_row, NEG_INF, 0.0).astype(BF16)
        kwaug_ref[WINDOW:WINDOW + t, 0:LANES] = kw_ref[0]
        kwaug_ref[WINDOW:WINDOW + t, LANES:2 * LANES] = jnp.zeros((t, LANES), BF16)

    zeros64 = jnp.zeros((HEAD_DIM, qb), BF16)
    for j in range(N_Q_HEADS):
        dst = j // GQA
        qaug_ref[HEAD_DIM * dst:HEAD_DIM * (dst + 1), qb * j:qb * (j + 1)] = q_ref[0, HEAD_DIM * j:HEAD_DIM * (j + 1), :]
        qaug_ref[HEAD_DIM * (1 - dst):HEAD_DIM * (2 - dst), qb * j:qb * (j + 1)] = zeros64
    tail_row = lax.broadcasted_iota(jnp.int32, (HEAD_DIM, ncol), 0)
    qaug_ref[one_row:2 * LANES, :] = jnp.where(tail_row == 0, 1.0, 0.0).astype(BF16)

    qaug_ref[LANES:one_row, :] = jnp.zeros((one_row - LANES, ncol), BF16)
    q_blocks = qb // TK_WIN
    npiece = (WINDOW + qb) // TK_WIN
    kws = [kwaug_ref[pl.ds(pl.multiple_of(q0 + w * TK_WIN, TK_WIN), TK_WIN), :] for w in range(npiece)]
    vwt = jnp.concatenate([vwt_ref[0, jnp.maximum(i * q_blocks + w - WINDOW // TK_WIN, 0)] for w in range(npiece)],
                          axis=1)
    wrow = lax.broadcasted_iota(jnp.int32, (TK_WIN, CB), 0)
    wcol = lax.broadcasted_iota(jnp.int32, (TK_WIN, CB), 1) & (qb - 1)
    wbias = []
    for w in range(npiece):
        lo, hi = w * TK_WIN - WINDOW, w * TK_WIN - WINDOW + TK_WIN - 1
        if hi <= 0 and qb - 1 - lo < WINDOW:
            wbias.append(None)
        else:
            dlt = wcol - wrow - lo
            wbias.append(jnp.where((dlt >= 0) & (dlt < WINDOW), 0.0, NEG_INF))
    kw_all = jnp.concatenate(kws, axis=0)
    sws = [jnp.dot(kw_all, qaug_ref[:, CB * cb:CB * (cb + 1)], preferred_element_type=F32) for cb in range(ncb)]
    es, rls = [], []
    for s in sws:
        s = jnp.concatenate([s[TK_WIN * w:TK_WIN * (w + 1)] if b is None else s[TK_WIN * w:TK_WIN * (w + 1)] + b
                             for w, b in enumerate(wbias)], axis=0)
        e = jnp.exp2(s - jnp.max(s, axis=0, keepdims=True))
        es.append(e.astype(BF16))
        rls.append(1.0 / jnp.sum(e, axis=0, keepdims=True))
    ow = jnp.concatenate([jnp.dot(vwt[hrows[cb]], e, preferred_element_type=F32) * rl
                          for cb, (e, rl) in enumerate(zip(es, rls))], axis=1)

    pos_c = q0 + (lax.broadcasted_iota(jnp.int32, (nch, CB), 1) & (qb - 1))
    cvalid = lax.broadcasted_iota(jnp.int32, (nch, CB), 0) * CMP_STRIDE + (CMP_LEN - 1) <= pos_c
    scs = [jnp.dot(ck_ref[0], qaug_ref[0:LANES, CB * cb:CB * (cb + 1)], preferred_element_type=F32)
           for cb in range(ncb)]
    pcs = [_softmax_cols(sc, cvalid).astype(BF16) for sc in scs]
    oc = jnp.concatenate([jnp.dot(cvt_ref[0, hrows[cb], :], pc, preferred_element_type=F32)
                          for cb, pc in enumerate(pcs)], axis=1)
    imp = jnp.concatenate([jnp.dot(ovt_ref[...], pc, preferred_element_type=F32) for pc in pcs], axis=1)
    blk = lax.broadcasted_iota(jnp.int32, (nslc, qb), 0)
    pos_q = q0 + lax.broadcasted_iota(jnp.int32, (nslc, qb), 1)
    for h in range(N_KV_HEADS):
        v = imp[0:nslc, qb * GQA * h:qb * GQA * h + qb]
        for g in range(1, GQA):
            v = v + imp[0:nslc, qb * (GQA * h + g):qb * (GQA * h + g + 1)]
        neg = _select_blocks(v, blk, pos_q, nslc).astype(BF16)
        for g in range(GQA):
            j = GQA * h + g
            qaug_ref[LANES:LANES + nslc, qb * j:qb * (j + 1)] = neg

    brow = lax.broadcasted_iota(jnp.int32, (qb, CB), 0)
    bcol = lax.broadcasted_iota(jnp.int32, (qb, CB), 1) & (qb - 1)
    tri_lo = jnp.where(brow <= bcol, 0.0, NEG_INF)

    acc_ref[...] = jnp.zeros_like(acc_ref)

    def sel_tile(k0, nk, vt, carry, bias):
        m, l = carry
        ka = kaug_ref[pl.ds(k0, nk), :]
        css = [slice(CB * cb, CB * (cb + 1)) for cb in range(ncb)]
        ss = [jnp.dot(ka, qaug_ref[:, cs], preferred_element_type=F32) for cs in css]
        ms, ls, ps, alphas = [], [], [], []
        for cs, s in zip(css, ss):
            if bias is not None:
                s = s + bias
            mn = jnp.maximum(m[:, cs], jnp.max(s, axis=0, keepdims=True))
            alpha = jnp.exp2(m[:, cs] - mn)
            p = jnp.exp2(s - mn)
            ms.append(mn)
            ls.append(alpha * l[:, cs] + jnp.sum(p, axis=0, keepdims=True))
            ps.append(p.astype(BF16))
            alphas.append(alpha)
        pvs = [jnp.dot(vt[hrows[cb]], p, preferred_element_type=F32) for cb, p in enumerate(ps)]
        for cs, alpha, pv in zip(css, alphas, pvs):
            acc_ref[:, cs] = alpha * acc_ref[:, cs] + pv
        return jnp.concatenate(ms, axis=1), jnp.concatenate(ls, axis=1)

    def vt_blocks(ref, b0, n):
        return jnp.concatenate([ref[0, b0 + j] for j in range(n)], axis=1) if n > 1 else ref[0, b0]

    big_blocks = TK_SLC // TK_WIN

    def big_tile(kt, carry):
        return sel_tile(pl.multiple_of(kt * TK_SLC, TK_SLC), TK_SLC, vt_blocks(vst_ref, kt * big_blocks, big_blocks),
                        carry, None)

    def small_tile(kb, carry):
        return sel_tile(pl.multiple_of(kb * qb, qb), qb, vt_blocks(vst_ref, kb * q_blocks, q_blocks), carry, None)

    carry = (jnp.full((1, ncol), NEG_INF, F32), jnp.zeros((1, ncol), F32))
    nbig = q0 // TK_SLC
    carry = lax.fori_loop(0, nbig // 2, lambda kp, c: big_tile(2 * kp + 1, big_tile(2 * kp, c)), carry)
    carry = lax.fori_loop(nbig - nbig % 2, nbig, big_tile, carry)
    carry = lax.fori_loop(nbig * (TK_SLC // qb), i, small_tile, carry)
    _, l = sel_tile(pl.multiple_of(q0, qb), qb, vt_blocks(vst_ref, i * q_blocks, q_blocks), carry, tri_lo)
    osel = acc_ref[...] * (1.0 / l)

    gt = gn_ref[0]
    for c in range(N_Q_HEADS // 2):
        rows = []
        for hh in range(2):
            j = 2 * c + hh
            cs = slice(qb * j, qb * (j + 1))
            rows.append(gt[3 * j:3 * j + 1, :] * oc[:, cs] + gt[3 * j + 1:3 * j + 2, :] * osel[:, cs]
                        + gt[3 * j + 2:3 * j + 3, :] * ow[:, cs])
        o_ref[:, LANES * c:LANES * (c + 1)] = jnp.concatenate(rows, axis=0).T.astype(o_ref.dtype)


def _attn_prompt(q, gn, ck, cvt, ksb, vst, kwb, vwt, nseq, t):
    nb = t // Q_BLOCK
    nch = t // CMP_STRIDE
    nslc = t // SLC_BLOCK
    ovt = _overlap_t(nch, max(nslc, SUBLANES))
    row = lambda n, i: (n * nb + i, 0)
    seq3 = lambda n, i: (n, 0, 0)
    seq4 = lambda n, i: (n, 0, 0, 0)
    col3 = lambda n, i: (n, 0, i)
    return pl.pallas_call(
        _attn_prompt_body,
        grid=(nseq, nb),
        in_specs=[pl.BlockSpec((1, Q_W, Q_BLOCK), col3), pl.BlockSpec((1, gn.shape[1], Q_BLOCK), col3),
                  pl.BlockSpec((1, nch, LANES), seq3), pl.BlockSpec((1, LANES, nch), seq3),
                  pl.BlockSpec((1, t, LANES), seq3), pl.BlockSpec((1, t // TK_WIN, LANES, TK_WIN), seq4),
                  pl.BlockSpec((1, t, LANES), seq3), pl.BlockSpec((1, t // TK_WIN, LANES, TK_WIN), seq4),
                  _full(ovt.shape)],
        out_specs=pl.BlockSpec((Q_BLOCK, Q_W), row),
        out_shape=jax.ShapeDtypeStruct((nseq * t, Q_W), BF16),
        scratch_shapes=[pltpu.VMEM((t, 2 * LANES), BF16),
                        pltpu.VMEM((WINDOW + t, 2 * LANES), BF16),
                        pltpu.VMEM((2 * LANES, N_Q_HEADS * Q_BLOCK), BF16),
                        pltpu.VMEM((HEAD_DIM, N_Q_HEADS * Q_BLOCK), F32)],
        compiler_params=_cparams(("arbitrary", "arbitrary"), VMEM_LARGE_MIB),
        name="attn_prompt",
    )(q, gn, ck, cvt, ksb.reshape(nseq, t, LANES), vst, kwb.reshape(nseq, t, LANES), vwt, ovt)


def _post_body(x_ref, abr_ref, on_ref, ga_ref, gb_ref, wno_ref, wo_ref, g2_ref, x1_ref, h2_ref):
    bbr = jnp.dot(on_ref[...].astype(BF16), wno_ref[...], preferred_element_type=F32)
    merged = ga_ref[...].astype(F32) * abr_ref[...].astype(F32) + gb_ref[...].astype(F32) * bbr
    x1 = x_ref[...] + jnp.dot(merged.astype(BF16), wo_ref[...], preferred_element_type=F32)
    x1_ref[...] = x1
    inv = lax.rsqrt(jnp.mean(x1 * x1, axis=-1, keepdims=True) + RMS_EPS)
    h2_ref[...] = (x1 * inv * g2_ref[...]).astype(BF16)


def _post(x2d, abr, abr_lay, onsa, ga, gb, wno, wo, g2, lay, out_lay):
    x_shape, x_spec = lay["a"](D_MODEL)
    abr_shape, abr_spec = lay[abr_lay](D_MODEL)
    on_shape, on_spec = lay["a"](Q_W)
    o_shape, o_spec = lay[out_lay](D_MODEL)
    return pl.pallas_call(
        _post_body,
        grid=lay["grid"],
        in_specs=[x_spec, abr_spec, on_spec, x_spec, x_spec, _full(wno.shape), _full(wo.shape), _full(g2.shape)],
        out_specs=[o_spec, o_spec],
        out_shape=[jax.ShapeDtypeStruct(o_shape, F32), jax.ShapeDtypeStruct(o_shape, BF16)],
        compiler_params=_cparams(("arbitrary",) * len(lay["grid"]), VMEM_SMALL_MIB),
        name="post",
    )(x2d.reshape(x_shape), abr.reshape(abr_shape), onsa.reshape(on_shape), ga, gb, wno, wo, g2)


def _route(logits):
    lane = lax.broadcasted_iota(jnp.int32, logits.shape, 1).astype(F32)
    big = float(LANES)
    glog = jnp.where(lane < N_EXPERT_GROUPS, logits, -jnp.inf)
    gmax = jnp.max(glog, axis=1, keepdims=True)
    gsel = jnp.min(jnp.where(glog == gmax, lane, big), axis=1, keepdims=True)
    gw = 1.0 / jnp.sum(jnp.exp(glog - gmax), axis=1, keepdims=True)
    lo = N_EXPERT_GROUPS + EXPERTS_PER_GROUP * gsel
    el = jnp.where((lane >= lo) & (lane < lo + EXPERTS_PER_GROUP), logits, -jnp.inf)
    v1 = jnp.max(el, axis=1, keepdims=True)
    i1 = jnp.min(jnp.where(el == v1, lane, big), axis=1, keepdims=True)
    el2 = jnp.where(lane == i1, -jnp.inf, el)
    v2 = jnp.max(el2, axis=1, keepdims=True)
    i2 = jnp.min(jnp.where(el2 == v2, lane, big), axis=1, keepdims=True)
    e2 = jnp.exp(v2 - v1)
    w1 = gw / (1.0 + e2)
    return jnp.where(lane == i1, w1, 0.0) + jnp.where(lane == i2, w1 * e2, 0.0)


def _moe_body(x1_ref, h2_ref, p_ref, wr_ref, br_ref, wg_ref, wu_ref, wd_ref, wpg_ref, wp_ref, gf_ref,
              y_ref, acc_ref, comb_ref, *, tsplit):
    g = pl.program_id(1)
    h2 = h2_ref[...]

    @pl.when(g == 0)
    def _():
        logits = jnp.dot(h2, wr_ref[...], preferred_element_type=F32) + br_ref[...]
        comb_ref[...] = _route(logits)
        acc_ref[...] = jnp.zeros_like(acc_ref)

    comb = comb_ref[...]
    lane = lax.broadcasted_iota(jnp.int32, comb.shape, 1)
    acc = acc_ref[...]
    for k in range(EXPERTS_PER_GROUP):
        e_lane = N_EXPERT_GROUPS + EXPERTS_PER_GROUP * g + k
        ce = jnp.sum(jnp.where(lane == e_lane, comb, 0.0), axis=1, keepdims=True)
        a = jnp.dot(h2, wg_ref[k], preferred_element_type=F32)
        b = jnp.dot(h2, wu_ref[k], preferred_element_type=F32)
        act = (jax.nn.silu(a) * b * ce).astype(BF16)
        acc = acc + jnp.dot(act, wd_ref[k], preferred_element_type=F32)
    acc_ref[...] = acc

    @pl.when(g == N_EXPERT_GROUPS - 1)
    def _():
        x2 = x1_ref[...] + acc_ref[...]
        rows = x2.shape[0] // tsplit
        if tsplit == 1:
            p = p_ref[...]
        else:
            p = jnp.concatenate([p_ref[:, PLE_DIM * t:PLE_DIM * (t + 1)] for t in range(tsplit)], axis=0)
        gate = jax.nn.sigmoid(jnp.dot(x2.astype(BF16), wpg_ref[...], preferred_element_type=F32))
        x3 = x2 + gate * jnp.dot(p.astype(BF16), wp_ref[...], preferred_element_type=F32)
        inv = lax.rsqrt(jnp.mean(x3 * x3, axis=-1, keepdims=True) + RMS_EPS)
        y = x3 * inv * gf_ref[...]
        if tsplit == 1:
            y_ref[...] = y
        else:
            for t in range(tsplit):
                y_ref[:, D_MODEL * t:D_MODEL * (t + 1)] = y[rows * t:rows * (t + 1)]


def _moe(x1, h2, p, mp, tm, tsplit):
    rows = x1.shape[0]
    nrb = rows // tm
    rb = lambda r, g: (r, 0)
    grp = lambda r, g: (g, 0, 0)
    if tsplit == 1:
        p_spec = pl.BlockSpec((tm, PLE_DIM), rb)
        y_spec = pl.BlockSpec((tm, D_MODEL), rb)
        y_shape = (rows, D_MODEL)
    else:
        assert nrb == 1
        p_spec = _full(p.shape)
        y_shape = (rows // tsplit, tsplit * D_MODEL)
        y_spec = _full(y_shape)
    return pl.pallas_call(
        functools.partial(_moe_body, tsplit=tsplit),
        grid=(nrb, N_EXPERT_GROUPS),
        in_specs=[pl.BlockSpec((tm, D_MODEL), rb), pl.BlockSpec((tm, D_MODEL), rb), p_spec,
                  _full(mp["wr"].shape), _full(mp["br"].shape),
                  pl.BlockSpec((EXPERTS_PER_GROUP, D_MODEL, D_FF_EXPERT), grp),
                  pl.BlockSpec((EXPERTS_PER_GROUP, D_MODEL, D_FF_EXPERT), grp),
                  pl.BlockSpec((EXPERTS_PER_GROUP, D_FF_EXPERT, D_MODEL), grp),
                  _full(mp["wpg"].shape), _full(mp["wp"].shape), _full(mp["gf"].shape)],
        out_specs=y_spec,
        out_shape=jax.ShapeDtypeStruct(y_shape, F32),
        scratch_shapes=[pltpu.VMEM((tm, D_MODEL), F32), pltpu.VMEM((tm, LANES), F32)],
        compiler_params=_cparams(("arbitrary", "arbitrary"), VMEM_MOE_MIB),
        name="moe_ple",
    )(x1, h2, p, mp["wr"], mp["br"], mp["wg"], mp["wu"], mp["wd"], mp["wpg"], mp["wp"], mp["gf"])


def _s5_sample_body(u_ref, h0re_ref, h0im_ref, wre_ref, wim_ref, ar_ref, ai_ref, cre_ref, cim_ref, d_ref,
                    wglu_ref, bglu_ref, wso_ref, abr_ref, hre_out_ref, him_out_ref,
                    bure_ref, buim_ref, hre_ref, him_ref, *, nseq, nstep):
    u = u_ref[...]
    ub = u.astype(BF16)
    for j in range(4):
        lhs = ub[:, LANES * j:LANES * (j + 1)]
        bure_ref[:, 512 * j:512 * (j + 1)] = jnp.dot(lhs, wre_ref[j], preferred_element_type=F32)
        buim_ref[:, 512 * j:512 * (j + 1)] = jnp.dot(lhs, wim_ref[j], preferred_element_type=F32)
    for lc in range(4):
        sl = slice(512 * lc, 512 * (lc + 1))
        ar = jnp.broadcast_to(ar_ref[:, sl], (SUBLANES, 512))
        ai = jnp.broadcast_to(ai_ref[:, sl], (SUBLANES, 512))

        def body(rc, carry, sl=sl, ar=ar, ai=ai):
            r0 = pl.multiple_of(rc * SUBLANES, SUBLANES)
            hr = h0re_ref[pl.ds(r0, SUBLANES), sl]
            hi = h0im_ref[pl.ds(r0, SUBLANES), sl]
            for t in range(nstep):
                rr = pl.multiple_of(t * nseq + rc * SUBLANES, SUBLANES)
                hr, hi = (ar * hr - ai * hi + bure_ref[pl.ds(rr, SUBLANES), sl],
                          ar * hi + ai * hr + buim_ref[pl.ds(rr, SUBLANES), sl])
                hre_ref[pl.ds(rr, SUBLANES), sl] = hr
                him_ref[pl.ds(rr, SUBLANES), sl] = hi
            hre_out_ref[pl.ds(r0, SUBLANES), sl] = hr
            him_out_ref[pl.ds(r0, SUBLANES), sl] = hi
            return carry

        lax.fori_loop(0, nseq // SUBLANES, body, 0)
    parts = []
    for j in range(4):
        sl = slice(512 * j, 512 * (j + 1))
        parts.append(jnp.dot(hre_ref[:, sl].astype(BF16), cre_ref[j], preferred_element_type=F32)
                     - jnp.dot(him_ref[:, sl].astype(BF16), cim_ref[j], preferred_element_type=F32))
    y = jnp.concatenate(parts, axis=1) + d_ref[...] * u
    zg = jax.nn.gelu(y)
    gate = jnp.dot(zg.astype(BF16), wglu_ref[...], preferred_element_type=F32) + bglu_ref[...]
    glu = (zg * jax.nn.sigmoid(gate)).astype(BF16)
    abr_ref[...] = jnp.dot(glu, wso_ref[...], preferred_element_type=F32).astype(BF16)


def _s5_sample(u_ts, h0re, h0im, sp, wglu, bglu, wso, nseq, nstep):
    rows = nseq * nstep
    ops = [u_ts, h0re, h0im, sp["wre"], sp["wim"], sp["ar"], sp["ai"], sp["cre"], sp["cim"], sp["d"], wglu, bglu, wso]
    return pl.pallas_call(
        functools.partial(_s5_sample_body, nseq=nseq, nstep=nstep),
        grid=(1,),
        in_specs=[_full(o.shape) for o in ops],
        out_specs=[_full((rows, D_MODEL)), _full((nseq, N_STATE)), _full((nseq, N_STATE))],
        out_shape=[jax.ShapeDtypeStruct((rows, D_MODEL), BF16),
                   jax.ShapeDtypeStruct((nseq, N_STATE), F32), jax.ShapeDtypeStruct((nseq, N_STATE), F32)],
        scratch_shapes=[pltpu.VMEM((rows, N_STATE), F32) for _ in range(4)],
        compiler_params=_cparams(("arbitrary",), VMEM_LARGE_MIB),
        name="s5_sample",
    )(*ops)


def _softmax_rows(s, valid):
    sm = jnp.where(valid, s, NEG_INF)
    mx = jnp.max(sm, axis=1, keepdims=True)
    e = jnp.where(valid, jnp.exp2(sm - mx), 0.0)
    l = jnp.sum(e, axis=1, keepdims=True)
    return e * (1.0 / jnp.maximum(l, 1e-30))


SAMPLE_SEQS_PER_STEP = 4
CMP_PITCH = 24


def _attn_sample_body(pt_ref, q_ref, gn_ref, nks_ref, nkw_ref, wint_ref, wk_ref, bk_ref, w2k_ref, ov_ref, e_ref,
                      cmp_hbm, slc_hbm, o_ref, nwint_ref, xrow_ref, pages_ref, sem_ref, *, npage, past_len, nsub, tq):
    n = pl.program_id(0)
    nsteps = pl.num_programs(0)
    slot = lax.rem(n, 2)
    nrow = N_Q_HEADS * tq
    nwin = wint_ref.shape[2]
    nslc = -(-(past_len + tq) // SLC_BLOCK)
    nch = past_len // CMP_STRIDE
    per_page = PAGE_SIZE // CMP_STRIDE

    def page_copy(step, into, c, s, p):
        src = (cmp_hbm, slc_hbm)[c]
        return pltpu.make_async_copy(src.at[pt_ref[step * nsub + s, p]],
                                     pages_ref.at[into, (c * nsub + s) * npage + p], sem_ref.at[into])

    def all_pages(step, into, op):
        for c in range(2):
            for s in range(nsub):
                for p in range(npage):
                    op(page_copy(step, into, c, s, p))

    @pl.when(n == 0)
    def _():
        all_pages(0, 0, lambda cp: cp.start())

    all_pages(n, slot, lambda cp: cp.wait())
    nxt = jnp.minimum(n + 1, nsteps - 1)
    all_pages(nxt, 1 - slot, lambda cp: cp.start())

    def page(c, s, p):
        return pages_ref.at[slot, (c * nsub + s) * npage + p]

    cmp = []
    for kv in range(2):
        for s in range(nsub):
            for p in range(npage):
                rows = page(0, s, p)[LANES * kv:LANES * (kv + 1), :].T
                for c in range(per_page):
                    r0 = CMP_PITCH * (per_page * p + c)
                    xrow_ref[s, r0:r0 + CMP_STRIDE, :] = rows[CMP_STRIDE * c:CMP_STRIDE * (c + 1)]
        x = jnp.concatenate(
            [jnp.concatenate([xrow_ref[s, pl.ds(i, nch, stride=CMP_PITCH), :] for i in range(CMP_STRIDE)], axis=1)
             for s in range(nsub)], axis=0)
        pp = jnp.dot(x.astype(BF16), wk_ref[kv], preferred_element_type=F32)
        pre = pp[:, 0:LANES] + pltpu.roll(pp[:, LANES:2 * LANES], nsub * nch - 1, axis=0) + bk_ref[kv:kv + 1, :]
        cmp.append(jnp.dot(jax.nn.gelu(pre).astype(BF16), w2k_ref[kv], preferred_element_type=F32).astype(BF16))
    cks = [cmp[0][nch * s:nch * (s + 1)] for s in range(nsub)]
    cvs = [cmp[1][nch * s:nch * (s + 1)] for s in range(nsub)]
    seqs = range(nsub)
    rcat = lambda parts: jnp.concatenate(parts, axis=0)

    lane_w = lax.broadcasted_iota(jnp.int32, (KV_W, LANES), 1)
    lane8 = lax.broadcasted_iota(jnp.int32, (tq, LANES), 1)
    nks_l, nkw_l, wint_l, qs_l = [], [], [], []
    for s in seqs:
        rows_s = slice(tq * s, tq * (s + 1))
        nks_l.append(jnp.concatenate([nks_ref[rows_s, :], jnp.zeros((LANES - tq, KV_W), F32)], axis=0))
        nkw = jnp.concatenate([nkw_ref[rows_s, :], jnp.zeros((LANES - tq, KV_W), F32)], axis=0)
        nkw_l.append(nkw)
        wint = wint_ref[s]
        wint_l.append(wint)
        shifted = pltpu.roll(wint, nwin - tq, axis=1)
        new_t = pltpu.roll(nkw.T, LANES - tq, axis=1)
        nwint_ref[s, :, 0:nwin - LANES] = shifted[:, 0:nwin - LANES]
        nwint_ref[s, :, nwin - LANES:nwin] = jnp.where(lane_w >= LANES - tq, new_t, shifted[:, nwin - LANES:nwin])
        q = q_ref[rows_s, :]
        qrows = []
        for j in range(N_Q_HEADS):
            chunk = q[:, LANES * (j // 2):LANES * (j // 2 + 1)]
            dst = j // GQA
            if (j % 2) != dst:
                chunk = pltpu.roll(chunk, HEAD_DIM, axis=1)
            keep = (lane8 < HEAD_DIM) if dst == 0 else (lane8 >= HEAD_DIM)
            qrows.append(jnp.where(keep, chunk, 0.0))
        qs_l.append(jnp.concatenate(qrows, axis=0).astype(BF16))

    rtot = nsub * nrow
    seq_rows = [slice(nrow * s, nrow * (s + 1)) for s in seqs]
    pos = past_len + (lax.broadcasted_iota(jnp.int32, (rtot, LANES), 0) & (tq - 1))
    lane = lax.broadcasted_iota(jnp.int32, (rtot, LANES), 1)

    sc = rcat([_dot_t(qs_l[s], cks[s]) for s in seqs])
    pc = _softmax_rows(sc, lane * CMP_STRIDE + (CMP_LEN - 1) <= pos).astype(BF16)
    oc = rcat([jnp.dot(pc[seq_rows[s]], cvs[s], preferred_element_type=F32) for s in seqs])
    imp = jnp.dot(pc, ov_ref[...], preferred_element_type=F32)
    vs = []
    for s in seqs:
        for h in range(N_KV_HEADS):
            r0 = nrow * s + tq * GQA * h
            v = imp[r0:r0 + tq]
            for g in range(1, GQA):
                v = v + imp[r0 + tq * g:r0 + tq * (g + 1)]
            vs.append(v)
    nsel = len(vs) * tq
    vt = rcat(vs + [jnp.zeros((LANES - nsel, LANES), F32)]).T
    nblk_pad = -(-nslc // SUBLANES) * SUBLANES
    blk_t = lax.broadcasted_iota(jnp.int32, (nblk_pad, LANES), 0)
    pos_t = past_len + (lax.broadcasted_iota(jnp.int32, (nblk_pad, LANES), 1) & (tq - 1))
    neg_t = _select_blocks(vt[0:nblk_pad], blk_t, pos_t, nslc, axis=0)
    neg = rcat([neg_t, jnp.zeros((LANES - nblk_pad, LANES), F32)]).T
    negsel = rcat([neg[tq * (N_KV_HEADS * s + j // GQA):tq * (N_KV_HEADS * s + j // GQA + 1)]
                   for s in seqs for j in range(N_Q_HEADS)])
    negsel_b = negsel.astype(BF16)

    new_blk = past_len // SLC_BLOCK
    ss_l = []
    for s in seqs:
        qaug = jnp.concatenate([qs_l[s], negsel_b[seq_rows[s]]], axis=1)
        parts = []
        for p in range(0, npage, 2):
            kt = jnp.concatenate([page(1, s, p)[0:LANES, :], page(1, s, p + 1)[0:LANES, :]], axis=1).astype(BF16)
            et = jnp.concatenate([e_ref[p], e_ref[p + 1]], axis=1)
            parts.append(jnp.dot(qaug, jnp.concatenate([kt, et], axis=0), preferred_element_type=F32))
        parts.append(_dot_t(qs_l[s], nks_l[s][:, 0:LANES].astype(BF16)) + negsel[seq_rows[s], new_blk:new_blk + 1])
        ss_l.append(jnp.concatenate(parts, axis=1))
    ss = rcat(ss_l)
    nkeys = ss.shape[1]
    kpos = lax.broadcasted_iota(jnp.int32, (rtot, nkeys), 1)
    pos_k = past_len + (lax.broadcasted_iota(jnp.int32, (rtot, nkeys), 0) & (tq - 1))
    ps = _softmax_rows(ss, kpos <= pos_k).astype(BF16)
    osel_l = []
    for s in seqs:
        psq = ps[seq_rows[s]]
        o = jnp.dot(psq[:, past_len:nkeys], nks_l[s][:, LANES:2 * LANES].astype(BF16), preferred_element_type=F32)
        for p in range(0, npage, 2):
            vtp = jnp.concatenate([page(1, s, p)[LANES:2 * LANES, :], page(1, s, p + 1)[LANES:2 * LANES, :]],
                                  axis=1).astype(BF16)
            o = o + _dot_t(psq[:, PAGE_SIZE * p:PAGE_SIZE * (p + 2)], vtp)
        osel_l.append(o)
    osel = rcat(osel_l)

    sw = rcat([jnp.concatenate([jnp.dot(qs_l[s], wint_l[s][0:LANES].astype(BF16), preferred_element_type=F32),
                                _dot_t(qs_l[s], nkw_l[s][:, 0:LANES].astype(BF16))], axis=1) for s in seqs])
    nw = sw.shape[1]
    widx = lax.broadcasted_iota(jnp.int32, (rtot, nw), 1)
    pos_w = past_len + (lax.broadcasted_iota(jnp.int32, (rtot, nw), 0) & (tq - 1))
    dlt = pos_w - (past_len - nwin + widx)
    pw = _softmax_rows(sw, (dlt >= 0) & (dlt < WINDOW) & (widx < nwin + tq)).astype(BF16)
    ow = rcat([_dot_t(pw[seq_rows[s], 0:nwin], wint_l[s][LANES:2 * LANES].astype(BF16))
               + jnp.dot(pw[seq_rows[s], nwin:nw], nkw_l[s][:, LANES:2 * LANES].astype(BF16),
                         preferred_element_type=F32) for s in seqs])

    for s in seqs:
        rows_s = slice(tq * s, tq * (s + 1))
        gn = gn_ref[rows_s, :]
        for c in range(N_Q_HEADS // 2):
            halves = []
            for hh in range(2):
                j = 2 * c + hh
                rs = slice(nrow * s + tq * j, nrow * s + tq * (j + 1))
                oj = (gn[:, 3 * j:3 * j + 1] * oc[rs] + gn[:, 3 * j + 1:3 * j + 2] * osel[rs]
                      + gn[:, 3 * j + 2:3 * j + 3] * ow[rs])
                if (j // GQA) != hh:
                    oj = pltpu.roll(oj, HEAD_DIM, axis=1)
                halves.append(oj)
            o_ref[rows_s, LANES * c:LANES * (c + 1)] = jnp.where(lane8 < HEAD_DIM, halves[0], halves[1])

    @pl.when(n == nsteps - 1)
    def _():
        all_pages(nxt, 1 - slot, lambda cp: cp.wait())


def _attn_sample(q, gn, nks, nkw, cache_cmp, cache_slc, cache_win, page_table, cp, nseq, tq, past_len):
    assert tq <= CMP_STRIDE and past_len % PAGE_SIZE == 0
    npage = past_len // PAGE_SIZE
    assert npage % 2 == 0 and PAGE_SIZE == LANES
    n_pool = cache_cmp.shape[0]
    nwin = cache_win.shape[1]
    chunks = past_len // CMP_STRIDE
    ov = _overlap_t(chunks, LANES).T
    key = np.arange(past_len).reshape(npage, 1, PAGE_SIZE)
    e = jnp.asarray(np.arange(LANES).reshape(1, LANES, 1) == key // SLC_BLOCK, dtype=BF16)
    to_t = lambda c: jnp.transpose(c, (0, 2, 3, 4, 1)).reshape(c.shape[0], KV_W, c.shape[1])
    cmp_t, slc_t, win_t = to_t(cache_cmp), to_t(cache_slc), to_t(cache_win)
    nsub = SAMPLE_SEQS_PER_STEP
    assert nseq % nsub == 0
    row = lambda n, pt: (n, 0)
    seq3 = lambda n, pt: (n, 0, 0)
    consts = [cp["wk"], cp["bk"], cp["w2k"], ov, e]
    in_specs = [pl.BlockSpec((nsub * tq, Q_W), row), pl.BlockSpec((nsub * tq, LANES), row),
                pl.BlockSpec((nsub * tq, KV_W), row), pl.BlockSpec((nsub * tq, KV_W), row),
                pl.BlockSpec((nsub, KV_W, nwin), seq3)]
    in_specs += [pl.BlockSpec(c.shape, (lambda nd: lambda n, pt: (0,) * nd)(c.ndim)) for c in consts]
    in_specs += [pl.BlockSpec(memory_space=pl.ANY)] * 2
    grid_spec = pltpu.PrefetchScalarGridSpec(
        num_scalar_prefetch=1,
        grid=(nseq // nsub,),
        in_specs=in_specs,
        out_specs=[pl.BlockSpec((nsub * tq, Q_W), row), pl.BlockSpec((nsub, KV_W, nwin), seq3)],
        scratch_shapes=[pltpu.VMEM((nsub, chunks * CMP_PITCH, LANES), F32),
                        pltpu.VMEM((2, 2 * nsub * npage, KV_W, PAGE_SIZE), F32),
                        pltpu.SemaphoreType.DMA((2,))],
    )
    return pl.pallas_call(
        functools.partial(_attn_sample_body, npage=npage, past_len=past_len, nsub=nsub, tq=tq),
        grid_spec=grid_spec,
        out_shape=[jax.ShapeDtypeStruct((nseq * tq, Q_W), F32), jax.ShapeDtypeStruct((nseq, KV_W, nwin), F32)],
        compiler_params=_cparams(("arbitrary",), VMEM_LARGE_MIB),
        name="attn_sample",
    )(page_table, q, gn, nks, nkw, win_t, *consts, cmp_t, slc_t)


def _moe_params(w_rg, b_rg, w_re, b_re, w_gate, w_up, w_down, w_ple, w_ple_gate, gf):
    pad = LANES - N_EXPERT_GROUPS - N_EXPERTS
    return {"wr": jnp.pad(jnp.concatenate([w_rg, w_re], axis=1), ((0, 0), (0, pad))).astype(BF16),
            "br": jnp.pad(jnp.concatenate([b_rg, b_re]), (0, pad)).astype(F32).reshape(1, LANES),
            "wg": w_gate.astype(BF16), "wu": w_up.astype(BF16), "wd": w_down.astype(BF16),
            "wpg": w_ple_gate.astype(BF16), "wp": w_ple.astype(BF16), "gf": gf.astype(F32).reshape(1, D_MODEL)}


TM_PROMPT = 512
TM_MOE = 1024
TC_S5 = 128


def kernel(x_prompt, x_sample, p_prompt, p_sample, cache_cmp_kv, cache_slc_kv, cache_win_kv, state_ssm, page_table, norm1_g, w_in, ssm_lam_re, ssm_lam_im, ssm_log_dt, ssm_b_re, ssm_b_im, ssm_c_re, ssm_c_im, ssm_d, w_glu, b_glu, cmp_pe, cmp_w1, cmp_w2, w_ssm_out, w_nsa_out, w_o, norm2_g, w_route_group, b_route_group, w_route_expert, b_route_expert, w_exp_gate, w_exp_up, w_exp_down, w_ple, w_ple_gate, final_norm_g):
    assert w_in.shape[0] == 1, "one layer"
    l = 0
    nb, t = x_prompt.shape[:2]
    ns, ts = x_sample.shape[:2]
    past_len = page_table.shape[1] * PAGE_SIZE
    kvt = (2, N_KV_HEADS, HEAD_DIM)

    wi = _inproj_params(w_in[l])
    g1 = norm1_g[l].astype(F32).reshape(1, D_MODEL)
    g2 = norm2_g[l].astype(F32).reshape(1, D_MODEL)
    sp = _s5_params(ssm_lam_re[l], ssm_lam_im[l], ssm_log_dt[l], ssm_b_re[l], ssm_b_im[l], ssm_c_re[l], ssm_c_im[l],
                    ssm_d[l])
    cp = _cmp_params(cmp_pe[l], cmp_w1[l], cmp_w2[l])
    mp = _moe_params(w_route_group[l], b_route_group[l], w_route_expert[l], b_route_expert[l], w_exp_gate[l],
                     w_exp_up[l], w_exp_down[l], w_ple[l], w_ple_gate[l], final_norm_g)
    wglu = w_glu[l].astype(BF16)
    bglu = b_glu[l].astype(F32).reshape(1, SSM_WIDTH)
    wso = w_ssm_out[l].astype(BF16)
    wno = w_nsa_out[l].astype(BF16)
    wo = w_o[l].astype(BF16)

    lay = _prompt_layout(nb, t, TM_PROMPT)
    xp = x_prompt.reshape(nb * t, D_MODEL)
    r = _inproj_prompt(xp, lay, g1, wi)
    abr, hlast = _s5_prompt(r["u"], sp, wglu, bglu, wso, t, TC_S5)
    ck, cvt = _compress_prompt(r["kvc"], cp, nb, t)
    onsa = _attn_prompt(r["qt"], r["gnt"], ck, cvt, r["ksb"], r["vst"], r["kwb"], r["vwt"], nb, t)
    x1, h2 = _post(xp, abr, "a", onsa, r["ga"], r["gb"], wno, wo, g2, lay, "a")
    y_prompt = _moe(x1, h2, p_prompt[l].reshape(nb * t, PLE_DIM), mp, TM_MOE, 1).reshape(nb, t, D_MODEL)
    keep = min(WINDOW, t)

    def rows_last(a):
        return jnp.transpose(a.reshape((a.shape[0],) + kvt + (a.shape[2],)), (0, 4, 1, 2, 3))[None]

    new_cmp_p = rows_last(r["kvct"])
    new_slc_p = rows_last(r["kvst"])
    new_win_p = rows_last(r["kvwt"][:, :, t - keep:])
    new_ssm_p = jnp.stack([hlast[0:nb], hlast[nb:2 * nb]], axis=-1).reshape(1, nb, N_SSM_GROUPS, SSM_STATE, 2)

    lays = _sample_layout(ns, ts)
    xs = x_sample.reshape(ns * ts, D_MODEL)
    rs = _inproj_sample(xs, lays, g1, wi)
    h0 = state_ssm[l].astype(F32).reshape(ns, N_STATE, 2)
    abr_s, hre, him = _s5_sample(rs["u"], h0[..., 0], h0[..., 1], sp, wglu, bglu, wso, ns, ts)
    onsa_s, new_win = _attn_sample(rs["q"].reshape(ns * ts, Q_W), rs["gn"].reshape(ns * ts, LANES),
                                   rs["kvs"].reshape(ns * ts, KV_W), rs["kvw"].reshape(ns * ts, KV_W),
                                   cache_cmp_kv[l], cache_slc_kv[l], cache_win_kv[l], page_table, cp, ns, ts, past_len)
    x1s, h2s = _post(xs, abr_s, "b", onsa_s, rs["ga"], rs["gb"], wno, wo, g2, lays, "b")
    y_sample = _moe(x1s, h2s, p_sample[l].reshape(ns, ts * PLE_DIM), mp, ns * ts, ts).reshape(ns, ts, D_MODEL)
    steps_first = lambda a: jnp.transpose(a.reshape((ts,) + kvt + (ns,)), (4, 0, 1, 2, 3))[None]
    new_cmp_s = steps_first(rs["kvct"])
    new_slc_s = steps_first(rs["kvst"])
    new_win_s = rows_last(new_win)
    new_ssm_s = jnp.stack([hre, him], axis=-1).reshape(1, ns, N_SSM_GROUPS, SSM_STATE, 2)
    return (y_prompt, y_sample, new_cmp_p, new_slc_p, new_win_p, new_ssm_p,
            new_cmp_s, new_slc_s, new_win_s, new_ssm_s)
```

```python
import functools
import math

import jax
import jax.numpy as jnp
import numpy as np
from jax import lax
from jax.experimental import pallas as pl
from jax.experimental.pallas import tpu as pltpu

F32 = jnp.float32
BF16 = jnp.bfloat16

D_MODEL = 1024
SSM_WIDTH = 512
SSM_GROUP = 16
N_SSM_GROUPS = 32
SSM_STATE = 64
HEAD_DIM = 64
N_Q_HEADS = 8
N_KV_HEADS = 2
GQA = 4
CMP_LEN = 32
CMP_STRIDE = 16
SLC_BLOCK = 64
TOP_N = 8
WINDOW = 512
Q_BLOCK = 256
NEG_INF = -1e30
FORCE_BONUS = 1e4
Q_W = 512
KV_W = 256
NSA_GATE_W = 24
N_EXPERT_GROUPS = 4
EXPERTS_PER_GROUP = 4
N_EXPERTS = 16
D_FF_EXPERT = 256
PLE_DIM = 256
RMS_EPS = 1e-6
PAGE_SIZE = 128

LANES = 128
SUBLANES = 8
N_STATE = N_SSM_GROUPS * SSM_STATE
MIB = 2 ** 20
V7X_VMEM_MIB = 64
VMEM_SMALL_MIB = 48
VMEM_LARGE_MIB = 56
VMEM_MOE_MIB = V7X_VMEM_MIB - 4


def _cparams(sem, vmem_mib):
    return pltpu.CompilerParams(dimension_semantics=sem, vmem_limit_bytes=vmem_mib * MIB)


def _full(shape):
    nd = len(shape)
    return pl.BlockSpec(shape, lambda *_: (0,) * nd)


def _prompt_layout(nseq, t, tm):
    nb = t // tm
    return {
        "grid": (nb, nseq), "tm": tm, "nseq": nseq, "t": t,
        "a": lambda w: ((nseq * t, w), pl.BlockSpec((tm, w), lambda b, s: (s * nb + b, 0))),
    }


def _sample_layout(nseq, t):
    return {
        "grid": (1, t), "tm": nseq,
        "a": lambda w: ((nseq, t * w), pl.BlockSpec((nseq, w), lambda s, b: (0, b))),
        "b": lambda w: ((t * nseq, w), pl.BlockSpec((nseq, w), lambda s, b: (b, 0))),
    }


TK_SLC = 512
TK_WIN = 128


Q_SCALE = HEAD_DIM ** -0.5 * math.log2(math.e)
C_U, C_Q, C_KVC, C_KVS, C_KVW = 0, 512, 1024, 1280, 1536
N_MAIN = 1792


def _dot_t(a, b):
    return lax.dot_general(a, b, (((1,), (1,)), ((), ())), preferred_element_type=F32)


GN_ROWS = 32


def _inproj_prompt_body(x_ref, g_ref, wa_ref, wgn_ref, wgab_ref,
                        u_ref, kvc_ref, ksb_ref, kwb_ref, ga_ref, gb_ref,
                        qt_ref, kvct_ref, kvst_ref, kvwt_ref, gnt_ref, vst_ref, vwt_ref, *, nseq):
    s = pl.program_id(1)
    x = x_ref[...]
    inv = lax.rsqrt(jnp.mean(x * x, axis=-1, keepdims=True) + RMS_EPS)
    h = (x * inv * g_ref[...]).astype(BF16)
    tm = h.shape[0]

    def mm(w):
        return jnp.dot(h, w, preferred_element_type=F32)

    u = mm(wa_ref[:, C_U:C_U + SSM_WIDTH])
    for j in range(SSM_WIDTH // LANES):
        u_ref[j, pl.ds(s, tm, stride=nseq), :] = u[:, LANES * j:LANES * (j + 1)]
    ga_ref[...] = jax.nn.sigmoid(mm(wgab_ref[:, 0:D_MODEL])).astype(BF16)
    gb_ref[...] = jax.nn.sigmoid(mm(wgab_ref[:, D_MODEL:2 * D_MODEL])).astype(BF16)
    kvc = mm(wa_ref[:, C_KVC:C_KVC + KV_W])
    kvc_ref[0] = kvc[:, 0:LANES]
    kvc_ref[1] = kvc[:, LANES:2 * LANES]
    kvct_ref[0] = kvc.T
    kvs = mm(wa_ref[:, C_KVS:C_KVS + KV_W])
    ksb_ref[...] = kvs[:, 0:LANES].astype(BF16)
    kvst = kvs.T
    kvst_ref[0] = kvst
    kvw = mm(wa_ref[:, C_KVW:C_KVW + KV_W])
    kwb_ref[...] = kvw[:, 0:LANES].astype(BF16)
    kvwt = kvw.T
    kvwt_ref[0] = kvwt
    for c in range(tm // TK_WIN):
        vst_ref[0, c] = kvst[LANES:2 * LANES, c * TK_WIN:(c + 1) * TK_WIN].astype(BF16)
        vwt_ref[0, c] = kvwt[LANES:2 * LANES, c * TK_WIN:(c + 1) * TK_WIN].astype(BF16)
    qt_ref[0] = (mm(wa_ref[:, C_Q:C_Q + Q_W]) * Q_SCALE).T.astype(BF16)
    gnt_ref[0] = jax.nn.sigmoid(mm(wgn_ref[...])).T[0:GN_ROWS]


def _inproj_prompt(x2d, lay, g, w):
    tm, nseq, t = lay["tm"], lay["nseq"], lay["t"]
    nb = t // tm
    out_shapes, out_specs, names = [], [], []

    def add(name, shape_spec, dt):
        names.append(name)
        out_shapes.append(jax.ShapeDtypeStruct(shape_spec[0], dt))
        out_specs.append(shape_spec[1])

    def tr(rows):
        return (nseq, rows, t), pl.BlockSpec((1, rows, tm), lambda b, s: (s, 0, b))

    nu = SSM_WIDTH // LANES
    add("u", ((nu, t * nseq, LANES), pl.BlockSpec((nu, tm * nseq, LANES), lambda b, s: (0, b, 0))), F32)
    add("kvc", ((2, nseq * t, LANES), pl.BlockSpec((2, tm, LANES), lambda b, s: (0, s * nb + b, 0))), F32)
    add("ksb", lay["a"](LANES), BF16)
    add("kwb", lay["a"](LANES), BF16)
    add("ga", lay["a"](D_MODEL), BF16)
    add("gb", lay["a"](D_MODEL), BF16)
    add("qt", tr(Q_W), BF16)
    add("kvct", tr(KV_W), F32)
    add("kvst", tr(KV_W), F32)
    add("kvwt", tr(KV_W), F32)
    add("gnt", tr(GN_ROWS), F32)
    for name in ("vst", "vwt"):
        add(name, ((nseq, t // TK_WIN, LANES, TK_WIN),
                   pl.BlockSpec((1, tm // TK_WIN, LANES, TK_WIN), lambda b, s: (s, b, 0, 0))), BF16)
    x_shape, x_spec = lay["a"](D_MODEL)
    ops = [g, w["wa"], w["wgn"], w["wgab"]]
    outs = pl.pallas_call(
        functools.partial(_inproj_prompt_body, nseq=nseq),
        grid=lay["grid"],
        in_specs=[x_spec] + [_full(o.shape) for o in ops],
        out_specs=out_specs,
        out_shape=out_shapes,
        compiler_params=_cparams(("arbitrary",) * 2, VMEM_LARGE_MIB),
        name="inproj_prompt",
    )(x2d.reshape(x_shape), *ops)
    return dict(zip(names, outs))


def _inproj_sample_body(x_ref, g_ref, wa_ref, wgn_ref, wgab_ref,
                        u_ref, q_ref, kvs_ref, kvw_ref, gn_ref, ga_ref, gb_ref, kvct_ref, kvst_ref, kvwt_ref):
    x = x_ref[...]
    inv = lax.rsqrt(jnp.mean(x * x, axis=-1, keepdims=True) + RMS_EPS)
    h = (x * inv * g_ref[...]).astype(BF16)

    def mm(w):
        return jnp.dot(h, w, preferred_element_type=F32)

    u_ref[...] = mm(wa_ref[:, C_U:C_U + SSM_WIDTH])
    q_ref[...] = mm(wa_ref[:, C_Q:C_Q + Q_W]) * Q_SCALE
    kvs = mm(wa_ref[:, C_KVS:C_KVS + KV_W])
    kvs_ref[...] = kvs
    kvw = mm(wa_ref[:, C_KVW:C_KVW + KV_W])
    kvw_ref[...] = kvw
    gn_ref[...] = jax.nn.sigmoid(mm(wgn_ref[...]))
    ga_ref[...] = jax.nn.sigmoid(mm(wgab_ref[:, 0:D_MODEL])).astype(BF16)
    gb_ref[...] = jax.nn.sigmoid(mm(wgab_ref[:, D_MODEL:2 * D_MODEL])).astype(BF16)
    kvct_ref[0] = mm(wa_ref[:, C_KVC:C_KVC + KV_W]).T
    kvst_ref[0] = kvs.T
    kvwt_ref[0] = kvw.T


def _inproj_sample(x2d, lay, g, w):
    nseq = lay["tm"]
    ts = lay["grid"][1]
    names = ["u", "q", "kvs", "kvw", "gn", "ga", "gb"]
    widths = [SSM_WIDTH, Q_W, KV_W, KV_W, LANES, D_MODEL, D_MODEL]
    out_shapes, out_specs = [], []
    for n, wd in zip(names, widths):
        shp, spec = lay["b" if n == "u" else "a"](wd)
        out_shapes.append(jax.ShapeDtypeStruct(shp, BF16 if n in ("ga", "gb") else F32))
        out_specs.append(spec)
    for n in ("kvct", "kvst", "kvwt"):
        names.append(n)
        out_shapes.append(jax.ShapeDtypeStruct((ts, KV_W, nseq), F32))
        out_specs.append(pl.BlockSpec((1, KV_W, nseq), lambda s, b: (b, 0, 0)))
    x_shape, x_spec = lay["a"](D_MODEL)
    ops = [g, w["wa"], w["wgn"], w["wgab"]]
    outs = pl.pallas_call(
        _inproj_sample_body,
        grid=lay["grid"],
        in_specs=[x_spec] + [_full(o.shape) for o in ops],
        out_specs=out_specs,
        out_shape=out_shapes,
        compiler_params=_cparams(("arbitrary",) * 2, VMEM_LARGE_MIB),
        name="inproj_sample",
    )(x2d.reshape(x_shape), *ops)
    return dict(zip(names, outs))


def _inproj_params(w_in0):
    return {"wa": w_in0[:, :N_MAIN].astype(BF16),
            "wgn": jnp.pad(w_in0[:, N_MAIN:N_MAIN + NSA_GATE_W], ((0, 0), (0, LANES - NSA_GATE_W))).astype(BF16),
            "wgab": w_in0[:, N_MAIN + NSA_GATE_W:].astype(BF16)}


S5_LANES = 1024


def _s5_prompt_body(u_ref, wb_ref, ar_ref, ai_ref, cw_ref, d_ref, wglu_ref, bglu_ref, wso_ref,
                    abr_ref, hlast_ref, lhs_ref, bu_ref, h8_ref, p_ref, hstate_ref):
    c = pl.program_id(0)
    nseq = 4
    r4 = u_ref.shape[1]
    tc = r4 // nseq
    half = tc // 2

    @pl.when(c == 0)
    def _():
        hstate_ref[...] = jnp.zeros_like(hstate_ref)

    u = jnp.concatenate([u_ref[j] for j in range(SSM_WIDTH // LANES)], axis=1)
    row2 = lax.broadcasted_iota(jnp.int32, (r4, SSM_WIDTH), 0)
    lo2 = (row2 % SUBLANES) < nseq
    up = pltpu.roll(u, r4 - nseq, axis=0)
    dn = pltpu.roll(u, nseq, axis=0)
    swapped = jnp.where(lo2, up, dn)
    zero = jnp.zeros_like(u)
    ev_re = jnp.where(lo2, u, zero).astype(BF16).reshape(half, SUBLANES, SSM_WIDTH)
    ev_im = jnp.where(lo2, zero, swapped).astype(BF16).reshape(half, SUBLANES, SSM_WIDTH)
    od_re = jnp.where(lo2, swapped, zero).astype(BF16).reshape(half, SUBLANES, SSM_WIDTH)
    od_im = jnp.where(lo2, zero, u).astype(BF16).reshape(half, SUBLANES, SSM_WIDTH)
    for j in range(4):
        sl = slice(LANES * j, LANES * (j + 1))
        lhs_ref[:, 0:8, 256 * j:256 * j + LANES] = ev_re[:, :, sl]
        lhs_ref[:, 0:8, 256 * j + LANES:256 * (j + 1)] = ev_im[:, :, sl]
        lhs_ref[:, 8:16, 256 * j:256 * j + LANES] = od_re[:, :, sl]
        lhs_ref[:, 8:16, 256 * j + LANES:256 * (j + 1)] = od_im[:, :, sl]
    for j in range(4):
        lhs = lhs_ref[:, :, 256 * j:256 * (j + 1)].reshape(tc * SUBLANES, 256)
        bu_ref[:, 512 * j:512 * (j + 1)] = jnp.dot(lhs, wb_ref[j], preferred_element_type=F32)

    for lc in range(N_STATE // S5_LANES):
        sl = slice(S5_LANES * lc, S5_LANES * (lc + 1))
        ar = ar_ref[:, sl]
        ai = ai_ref[:, sl]

        def step(t, h, sl=sl, ar=ar, ai=ai):
            r0 = pl.multiple_of(t * SUBLANES, SUBLANES)
            h = ar * h + ai * pltpu.roll(h, nseq, axis=0) + bu_ref[pl.ds(r0, SUBLANES), sl]
            h8_ref[pl.ds(r0, SUBLANES), sl] = h
            return h

        hstate_ref[:, sl] = lax.fori_loop(0, tc, step, hstate_ref[:, sl], unroll=8)
    hlast_ref[...] = hstate_ref[...]

    for j in range(4):
        pj = jnp.dot(h8_ref[:, 512 * j:512 * (j + 1)].astype(BF16), cw_ref[j], preferred_element_type=F32)
        p_ref[2 * j] = pj[:, 0:LANES]
        p_ref[2 * j + 1] = pj[:, LANES:2 * LANES]
    ys = []
    for s in range(nseq):
        parts = []
        for j in range(4):
            re = p_ref[2 * j, pl.ds(s, tc, stride=SUBLANES), :]
            im = p_ref[2 * j + 1, pl.ds(nseq + s, tc, stride=SUBLANES), :]
            us = u_ref[j, pl.ds(s, tc, stride=nseq), :]
            parts.append(re + im + d_ref[:, LANES * j:LANES * (j + 1)] * us)
        ys.append(jnp.concatenate(parts, axis=1))
    y = jnp.concatenate(ys, axis=0)
    zg = jax.nn.gelu(y)
    gate = jnp.dot(zg.astype(BF16), wglu_ref[...], preferred_element_type=F32) + bglu_ref[...]
    glu = (zg * jax.nn.sigmoid(gate)).astype(BF16)
    abr = jnp.dot(glu, wso_ref[...], preferred_element_type=F32)
    for s in range(nseq):
        abr_ref[s] = abr[s * tc:(s + 1) * tc].astype(BF16)


def _s5_prompt(u_ts, sp, wglu, bglu, wso, t_total, tc):
    nseq = 4
    grid = (t_total // tc,)
    abr, hlast = pl.pallas_call(
        _s5_prompt_body,
        grid=grid,
        in_specs=[pl.BlockSpec((SSM_WIDTH // LANES, tc * nseq, LANES), lambda c: (0, c, 0)),
                  _full(sp["wb8"].shape), _full(sp["ar8"].shape), _full(sp["ai8"].shape), _full(sp["cw8"].shape),
                  _full(sp["d"].shape), _full(wglu.shape), _full(bglu.shape), _full(wso.shape)],
        out_specs=[pl.BlockSpec((nseq, tc, D_MODEL), lambda c: (0, c, 0)),
                   pl.BlockSpec((SUBLANES, N_STATE), lambda c: (0, 0))],
        out_shape=[jax.ShapeDtypeStruct((nseq, t_total, D_MODEL), BF16),
                   jax.ShapeDtypeStruct((SUBLANES, N_STATE), F32)],
        scratch_shapes=[pltpu.VMEM((tc // 2, 2 * SUBLANES, 1024), BF16),
                        pltpu.VMEM((tc * SUBLANES, N_STATE), F32),
                        pltpu.VMEM((tc * SUBLANES, N_STATE), F32),
                        pltpu.VMEM((8, tc * SUBLANES, LANES), F32),
                        pltpu.VMEM((SUBLANES, N_STATE), F32)],
        compiler_params=_cparams(("arbitrary",), VMEM_LARGE_MIB),
        name="s5_prompt",
    )(u_ts, sp["wb8"], sp["ar8"], sp["ai8"], sp["cw8"], sp["d"], wglu, bglu, wso)
    return abr, hlast


def _s5_params(lam_re, lam_im, log_dt, b_re, b_im, c_re, c_im, d_skip):
    lam = lax.complex(lam_re.astype(F32), lam_im.astype(F32))
    dt = jnp.exp(log_dt.astype(F32))[:, None]
    a_bar = jnp.exp(lam * dt)
    b = lax.complex(b_re.astype(F32), b_im.astype(F32))
    b_bar = ((a_bar - 1.0) / lam)[..., None] * b
    eye8 = jnp.eye(8, dtype=F32)

    def bd_b(m):
        return jnp.einsum("ab,jbpc->jacbp", eye8, m.reshape(4, 8, SSM_STATE, SSM_GROUP)).reshape(4, 128, 512)

    def bd_c(m):
        return jnp.einsum("ab,jbcp->japbc", eye8, m.reshape(4, 8, SSM_GROUP, SSM_STATE)).reshape(4, 512, 128)

    wre, wim = bd_b(b_bar.real), bd_b(b_bar.imag)
    cre, cim = bd_c(c_re.astype(F32)), bd_c(c_im.astype(F32))
    ar = a_bar.real.reshape(1, N_STATE)
    ai = a_bar.imag.reshape(1, N_STATE)
    sign = jnp.concatenate([-jnp.ones((4, 1), F32), jnp.ones((4, 1), F32)], axis=0)
    return {
        "wb8": jnp.concatenate([wre, wim], axis=1).astype(BF16),
        "cw8": jnp.concatenate([cre, -cim], axis=2).astype(BF16),
        "ar8": jnp.broadcast_to(ar, (SUBLANES, N_STATE)),
        "ai8": sign * ai,
        "wre": wre.astype(BF16), "wim": wim.astype(BF16),
        "cre": cre.astype(BF16), "cim": cim.astype(BF16),
        "ar": ar, "ai": ai,
        "d": d_skip.astype(F32).reshape(1, SSM_WIDTH),
    }


def _cmp_params(cmp_pe, cmp_w1, cmp_w2):
    eye2 = jnp.eye(2, dtype=F32)
    nhalf = CMP_LEN // CMP_STRIDE
    w1r = cmp_w1.astype(F32).reshape(2, nhalf, CMP_STRIDE, HEAD_DIM, HEAD_DIM)
    wk = jnp.einsum("kside,ph->kipdshe", w1r, eye2).reshape(2, CMP_STRIDE * LANES, nhalf * LANES)
    bk = jnp.einsum("kld,klde->ke", cmp_pe.astype(F32), cmp_w1.astype(F32), precision=lax.Precision.HIGHEST)
    w2k = jnp.einsum("kef,ph->kpehf", cmp_w2.astype(F32), eye2).reshape(2, LANES, LANES)
    return {"wk": wk.astype(BF16), "bk": jnp.tile(bk, (1, N_KV_HEADS)), "w2k": w2k.astype(BF16),
            "w2kt": jnp.swapaxes(w2k, 1, 2).astype(BF16)}


def _compress_hidden(tap, nch, kv, wk_ref, bk_ref):
    x = jnp.concatenate([tap(i).astype(BF16) for i in range(CMP_STRIDE)], axis=1)
    pp = jnp.dot(x, wk_ref[kv], preferred_element_type=F32)
    pre = pp[:, 0:LANES] + pltpu.roll(pp[:, LANES:2 * LANES], nch - 1, axis=0) + bk_ref[kv:kv + 1, :]
    return jax.nn.gelu(pre).astype(BF16)


def _compress_prompt_body(x_ref, wk_ref, bk_ref, w2k_ref, w2kt_ref, ck_ref, cvt_ref):
    nch = x_ref.shape[1] // CMP_STRIDE
    hid = [_compress_hidden(lambda i, kv=kv: x_ref[kv, pl.ds(i, nch, stride=CMP_STRIDE), :], nch, kv, wk_ref, bk_ref)
           for kv in range(2)]
    ck_ref[0] = jnp.dot(hid[0], w2k_ref[0], preferred_element_type=F32).astype(BF16)
    cvt_ref[0] = _dot_t(w2kt_ref[1], hid[1]).astype(BF16)


def _compress_prompt(kvc2, cp, nseq, t):
    nch = t // CMP_STRIDE
    return pl.pallas_call(
        _compress_prompt_body,
        grid=(nseq,),
        in_specs=[pl.BlockSpec((2, t, LANES), lambda n: (0, n, 0)),
                  _full(cp["wk"].shape), _full(cp["bk"].shape), _full(cp["w2k"].shape), _full(cp["w2kt"].shape)],
        out_specs=[pl.BlockSpec((1, nch, LANES), lambda n: (n, 0, 0)),
                   pl.BlockSpec((1, LANES, nch), lambda n: (n, 0, 0))],
        out_shape=[jax.ShapeDtypeStruct((nseq, nch, LANES), BF16),
                   jax.ShapeDtypeStruct((nseq, LANES, nch), BF16)],
        compiler_params=_cparams(("arbitrary",), VMEM_SMALL_MIB),
        name="compress_prompt",
    )(kvc2, cp["wk"], cp["bk"], cp["w2k"], cp["w2kt"])


def _overlap_t(n_cmp_pad, n_slc_pad):
    j = np.arange(n_cmp_pad)[None, :]
    s = np.arange(n_slc_pad)[:, None]
    ov = (j * CMP_STRIDE <= s * SLC_BLOCK + SLC_BLOCK - 1) & (j * CMP_STRIDE + CMP_LEN - 1 >= s * SLC_BLOCK)
    return jnp.asarray(ov, dtype=BF16)


def _softmax_cols(s, valid):
    sm = jnp.where(valid, s, NEG_INF)
    mx = jnp.max(sm, axis=0, keepdims=True)
    e = jnp.where(valid, jnp.exp2(sm - mx), 0.0)
    l = jnp.sum(e, axis=0, keepdims=True)
    return e * (1.0 / jnp.maximum(l, 1e-30))


def _select_blocks(imp, blk, pos, nblk, axis=0):
    cur = pos // SLC_BLOCK
    forced = (blk == 0) | (blk == cur) | (blk == cur - 1)
    v = jnp.where(forced, imp + FORCE_BONUS, imp)
    v = jnp.where(blk * SLC_BLOCK <= pos, v, NEG_INF)
    v = jnp.where(blk < nblk, v, -3e38)
    blk_f = blk.astype(F32)
    neg = jnp.full(imp.shape, NEG_INF, F32)
    for _ in range(min(TOP_N, nblk)):
        mx = jnp.max(v, axis=axis, keepdims=True)
        first = jnp.min(jnp.where(v == mx, blk_f, float(imp.shape[axis])), axis=axis, keepdims=True)
        pick = blk_f == first
        neg = jnp.where(pick, 0.0, neg)
        v = jnp.where(pick, -3e38, v)
    return neg


CB = 2 * LANES


def _attn_prompt_body(q_ref, gn_ref, ck_ref, cvt_ref, ks_ref, vst_ref, kw_ref, vwt_ref, ovt_ref,
                      o_ref, kaug_ref, kwaug_ref, qaug_ref, acc_ref):
    i = pl.program_id(1)
    t = ks_ref.shape[1]
    nch = ck_ref.shape[1]
    nslc = t // SLC_BLOCK
    qb = Q_BLOCK
    ncol = N_Q_HEADS * qb
    ncb = ncol // CB
    q0 = i * qb
    one_row = 2 * LANES - HEAD_DIM
    hrows = [slice(HEAD_DIM * ((CB * cb // qb) // GQA), HEAD_DIM * ((CB * cb // qb) // GQA + 1)) for cb in range(ncb)]

    @pl.when(i == 0)
    def _():
        kaug_ref[:, 0:LANES] = ks_ref[0]
        blk = lax.broadcasted_iota(jnp.int32, (t, LANES), 0) // SLC_BLOCK
        col = lax.broadcasted_iota(jnp.int32, (t, LANES), 1)
        kaug_ref[:, LANES:2 * LANES] = jnp.where(blk == col, 1.0, 0.0).astype(BF16)
        padcol = lax.broadcasted_iota(jnp.int32, (WINDOW, 2 * LANES), 1)
        kwaug_ref[0:WINDOW, :] = jnp.where(padcol == one_row, NEG_INF, 0.0).astype(BF16)
        kwaug_ref[WINDOW:WINDOW + t, 0:LANES] = kw_ref[0]
        kwaug_ref[WINDOW:WINDOW + t, LANES:2 * LANES] = jnp.zeros((t, LANES), BF16)

    zeros64 = jnp.zeros((HEAD_DIM, qb), BF16)
    for j in range(N_Q_HEADS):
        dst = j // GQA
        qaug_ref[HEAD_DIM * dst:HEAD_DIM * (dst + 1), qb * j:qb * (j + 1)] = q_ref[0, HEAD_DIM * j:HEAD_DIM * (j + 1), :]
        qaug_ref[HEAD_DIM * (1 - dst):HEAD_DIM * (2 - dst), qb * j:qb * (j + 1)] = zeros64
    tail_row = lax.broadcasted_iota(jnp.int32, (HEAD_DIM, ncol), 0)
    qaug_ref[one_row:2 * LANES, :] = jnp.where(tail_row == 0, 1.0, 0.0).astype(BF16)

    pos_c = q0 + (lax.broadcasted_iota(jnp.int32, (nch, CB), 1) & (qb - 1))
    cvalid = lax.broadcasted_iota(jnp.int32, (nch, CB), 0) * CMP_STRIDE + (CMP_LEN - 1) <= pos_c
    scs = [jnp.dot(ck_ref[0], qaug_ref[0:LANES, CB * cb:CB * (cb + 1)], preferred_element_type=F32)
           for cb in range(ncb)]
    pcs = [_softmax_cols(sc, cvalid).astype(BF16) for sc in scs]
    oc = jnp.concatenate([jnp.dot(cvt_ref[0, hrows[cb], :], pc, preferred_element_type=F32)
                          for cb, pc in enumerate(pcs)], axis=1)
    imp = jnp.concatenate([jnp.dot(ovt_ref[...], pc, preferred_element_type=F32) for pc in pcs], axis=1)

    qaug_ref[LANES:one_row, :] = jnp.zeros((one_row - LANES, ncol), BF16)
    q_blocks = qb // TK_WIN
    npiece = (WINDOW + qb) // TK_WIN
    kws = [kwaug_ref[pl.ds(pl.multiple_of(q0 + w * TK_WIN, TK_WIN), TK_WIN), :] for w in range(npiece)]
    vwt = jnp.concatenate([vwt_ref[0, jnp.maximum(i * q_blocks + w - WINDOW // TK_WIN, 0)] for w in range(npiece)],
                          axis=1)
    wrow = lax.broadcasted_iota(jnp.int32, (TK_WIN, CB), 0)
    wcol = lax.broadcasted_iota(jnp.int32, (TK_WIN, CB), 1) & (qb - 1)
    wbias = []
    for w in range(npiece):
        lo, hi = w * TK_WIN - WINDOW, w * TK_WIN - WINDOW + TK_WIN - 1
        if hi <= 0 and qb - 1 - lo < WINDOW:
            wbias.append(None)
        else:
            dlt = wcol - wrow - lo
            wbias.append(jnp.where((dlt >= 0) & (dlt < WINDOW), 0.0, NEG_INF))
    kw_all = jnp.concatenate(kws, axis=0)
    sws = [jnp.dot(kw_all, qaug_ref[:, CB * cb:CB * (cb + 1)], preferred_element_type=F32) for cb in range(ncb)]
    es, rls = [], []
    for s in sws:
        s = jnp.concatenate([s[TK_WIN * w:TK_WIN * (w + 1)] if b is None else s[TK_WIN * w:TK_WIN * (w + 1)] + b
                             for w, b in enumerate(wbias)], axis=0)
        e = jnp.exp2(s - jnp.max(s, axis=0, keepdims=True))
        es.append(e.astype(BF16))
        rls.append(1.0 / jnp.sum(e, axis=0, keepdims=True))
    ow = jnp.concatenate([jnp.dot(vwt[hrows[cb]], e, preferred_element_type=F32) * rl
                          for cb, (e, rl) in enumerate(zip(es, rls))], axis=1)

    blk = lax.broadcasted_iota(jnp.int32, (nslc, N_KV_HEADS * qb), 0)
    pos_q = q0 + (lax.broadcasted_iota(jnp.int32, (nslc, N_KV_HEADS * qb), 1) & (qb - 1))
    vs = []
    for h in range(N_KV_HEADS):
        v = imp[0:nslc, qb * GQA * h:qb * GQA * h + qb]
        for g in range(1, GQA):
            v = v + imp[0:nslc, qb * (GQA * h + g):qb * (GQA * h + g + 1)]
        vs.append(v)
    neg = _select_blocks(jnp.concatenate(vs, axis=1), blk, pos_q, nslc).astype(BF16)
    for j in range(N_Q_HEADS):
        h = j // GQA
        qaug_ref[LANES:LANES + nslc, qb * j:qb * (j + 1)] = neg[:, qb * h:qb * (h + 1)]

    brow = lax.broadcasted_iota(jnp.int32, (qb, CB), 0)
    bcol = lax.broadcasted_iota(jnp.int32, (qb, CB), 1) & (qb - 1)
    tri_lo = jnp.where(brow <= bcol, 0.0, NEG_INF)

    acc_ref[...] = jnp.zeros_like(acc_ref)

    def sel_tile(k0, nk, vt, carry, bias):
        m, l = carry
        ka = kaug_ref[pl.ds(k0, nk), :]
        css = [slice(CB * cb, CB * (cb + 1)) for cb in range(ncb)]
        ss = [jnp.dot(ka, qaug_ref[:, cs], preferred_element_type=F32) for cs in css]
        ms, ls, ps, alphas = [], [], [], []
        for cs, s in zip(css, ss):
            if bias is not None:
                s = s + bias
            mn = jnp.maximum(m[:, cs], jnp.max(s, axis=0, keepdims=True))
            alpha = jnp.exp2(m[:, cs] - mn)
            p = jnp.exp2(s - mn)
            ms.append(mn)
            ls.append(alpha * l[:, cs] + jnp.sum(p, axis=0, keepdims=True))
            ps.append(p.astype(BF16))
            alphas.append(alpha)
        pvs = [jnp.dot(vt[hrows[cb]], p, preferred_element_type=F32) for cb, p in enumerate(ps)]
        for cs, alpha, pv in zip(css, alphas, pvs):
            acc_ref[:, cs] = alpha * acc_ref[:, cs] + pv
        return jnp.concatenate(ms, axis=1), jnp.concatenate(ls, axis=1)

    def vt_blocks(ref, b0, n):
        return jnp.concatenate([ref[0, b0 + j] for j in range(n)], axis=1) if n > 1 else ref[0, b0]

    big_blocks = TK_SLC // TK_WIN

    def big_tile(kt, carry):
        return sel_tile(pl.multiple_of(kt * TK_SLC, TK_SLC), TK_SLC, vt_blocks(vst_ref, kt * big_blocks, big_blocks),
                        carry, None)

    def small_tile(kb, carry):
        return sel_tile(pl.multiple_of(kb * qb, qb), qb, vt_blocks(vst_ref, kb * q_blocks, q_blocks), carry, None)

    carry = (jnp.full((1, ncol), NEG_INF, F32), jnp.zeros((1, ncol), F32))
    nbig = q0 // TK_SLC
    carry = lax.fori_loop(0, nbig // 2, lambda kp, c: big_tile(2 * kp + 1, big_tile(2 * kp, c)), carry)
    carry = lax.fori_loop(nbig - nbig % 2, nbig, big_tile, carry)
    carry = lax.fori_loop(nbig * (TK_SLC // qb), i, small_tile, carry)
    _, l = sel_tile(pl.multiple_of(q0, qb), qb, vt_blocks(vst_ref, i * q_blocks, q_blocks), carry, tri_lo)
    osel = acc_ref[...] * (1.0 / l)

    gt = gn_ref[0]
    for c in range(N_Q_HEADS // 2):
        rows = []
        for hh in range(2):
            j = 2 * c + hh
            cs = slice(qb * j, qb * (j + 1))
            rows.append(gt[3 * j:3 * j + 1, :] * oc[:, cs] + gt[3 * j + 1:3 * j + 2, :] * osel[:, cs]
                        + gt[3 * j + 2:3 * j + 3, :] * ow[:, cs])
        o_ref[:, LANES * c:LANES * (c + 1)] = jnp.concatenate(rows, axis=0).T.astype(o_ref.dtype)


def _attn_prompt(q, gn, ck, cvt, ksb, vst, kwb, vwt, nseq, t):
    nb = t // Q_BLOCK
    nch = t // CMP_STRIDE
    nslc = t // SLC_BLOCK
    ovt = _overlap_t(nch, max(nslc, SUBLANES))
    row = lambda n, i: (n * nb + i, 0)
    seq3 = lambda n, i: (n, 0, 0)
    seq4 = lambda n, i: (n, 0, 0, 0)
    col3 = lambda n, i: (n, 0, i)
    return pl.pallas_call(
        _attn_prompt_body,
        grid=(nseq, nb),
        in_specs=[pl.BlockSpec((1, Q_W, Q_BLOCK), col3), pl.BlockSpec((1, gn.shape[1], Q_BLOCK), col3),
                  pl.BlockSpec((1, nch, LANES), seq3), pl.BlockSpec((1, LANES, nch), seq3),
                  pl.BlockSpec((1, t, LANES), seq3), pl.BlockSpec((1, t // TK_WIN, LANES, TK_WIN), seq4),
                  pl.BlockSpec((1, t, LANES), seq3), pl.BlockSpec((1, t // TK_WIN, LANES, TK_WIN), seq4),
                  _full(ovt.shape)],
        out_specs=pl.BlockSpec((Q_BLOCK, Q_W), row),
        out_shape=jax.ShapeDtypeStruct((nseq * t, Q_W), BF16),
        scratch_shapes=[pltpu.VMEM((t, 2 * LANES), BF16),
                        pltpu.VMEM((WINDOW + t, 2 * LANES), BF16),
                        pltpu.VMEM((2 * LANES, N_Q_HEADS * Q_BLOCK), BF16),
                        pltpu.VMEM((HEAD_DIM, N_Q_HEADS * Q_BLOCK), F32)],
        compiler_params=_cparams(("arbitrary", "arbitrary"), VMEM_LARGE_MIB),
        name="attn_prompt",
    )(q, gn, ck, cvt, ksb.reshape(nseq, t, LANES), vst, kwb.reshape(nseq, t, LANES), vwt, ovt)


def _post_body(x_ref, abr_ref, on_ref, ga_ref, gb_ref, wno_ref, wo_ref, g2_ref, x1_ref, h2_ref):
    bbr = jnp.dot(on_ref[...].astype(BF16), wno_ref[...], preferred_element_type=F32)
    merged = ga_ref[...].astype(F32) * abr_ref[...].astype(F32) + gb_ref[...].astype(F32) * bbr
    x1 = x_ref[...] + jnp.dot(merged.astype(BF16), wo_ref[...], preferred_element_type=F32)
    x1_ref[...] = x1
    inv = lax.rsqrt(jnp.mean(x1 * x1, axis=-1, keepdims=True) + RMS_EPS)
    h2_ref[...] = (x1 * inv * g2_ref[...]).astype(BF16)


def _post(x2d, abr, abr_lay, onsa, ga, gb, wno, wo, g2, lay, out_lay):
    x_shape, x_spec = lay["a"](D_MODEL)
    abr_shape, abr_spec = lay[abr_lay](D_MODEL)
    on_shape, on_spec = lay["a"](Q_W)
    o_shape, o_spec = lay[out_lay](D_MODEL)
    return pl.pallas_call(
        _post_body,
        grid=lay["grid"],
        in_specs=[x_spec, abr_spec, on_spec, x_spec, x_spec, _full(wno.shape), _full(wo.shape), _full(g2.shape)],
        out_specs=[o_spec, o_spec],
        out_shape=[jax.ShapeDtypeStruct(o_shape, F32), jax.ShapeDtypeStruct(o_shape, BF16)],
        compiler_params=_cparams(("arbitrary",) * len(lay["grid"]), VMEM_SMALL_MIB),
        name="post",
    )(x2d.reshape(x_shape), abr.reshape(abr_shape), onsa.reshape(on_shape), ga, gb, wno, wo, g2)


def _route(logits):
    lane = lax.broadcasted_iota(jnp.int32, logits.shape, 1).astype(F32)
    big = float(LANES)
    glog = jnp.where(lane < N_EXPERT_GROUPS, logits, -jnp.inf)
    gmax = jnp.max(glog, axis=1, keepdims=True)
    gsel = jnp.min(jnp.where(glog == gmax, lane, big), axis=1, keepdims=True)
    gw = 1.0 / jnp.sum(jnp.exp(glog - gmax), axis=1, keepdims=True)
    lo = N_EXPERT_GROUPS + EXPERTS_PER_GROUP * gsel
    el = jnp.where((lane >= lo) & (lane < lo + EXPERTS_PER_GROUP), logits, -jnp.inf)
    v1 = jnp.max(el, axis=1, keepdims=True)
    i1 = jnp.min(jnp.where(el == v1, lane, big), axis=1, keepdims=True)
    el2 = jnp.where(lane == i1, -jnp.inf, el)
    v2 = jnp.max(el2, axis=1, keepdims=True)
    i2 = jnp.min(jnp.where(el2 == v2, lane, big), axis=1, keepdims=True)
    e2 = jnp.exp(v2 - v1)
    w1 = gw / (1.0 + e2)
    return jnp.where(lane == i1, w1, 0.0) + jnp.where(lane == i2, w1 * e2, 0.0)


def _moe_body(x1_ref, h2_ref, p_ref, wr_ref, br_ref, wg_ref, wu_ref, wd_ref, wpg_ref, wp_ref, gf_ref,
              y_ref, acc_ref, comb_ref, *, tsplit):
    g = pl.program_id(1)
    h2 = h2_ref[...]

    @pl.when(g == 0)
    def _():
        logits = jnp.dot(h2, wr_ref[...], preferred_element_type=F32) + br_ref[...]
        comb_ref[...] = _route(logits)
        acc_ref[...] = jnp.zeros_like(acc_ref)

    comb = comb_ref[...]
    lane = lax.broadcasted_iota(jnp.int32, comb.shape, 1)
    acc = acc_ref[...]
    for k in range(EXPERTS_PER_GROUP):
        e_lane = N_EXPERT_GROUPS + EXPERTS_PER_GROUP * g + k
        ce = jnp.sum(jnp.where(lane == e_lane, comb, 0.0), axis=1, keepdims=True)
        a = jnp.dot(h2, wg_ref[k], preferred_element_type=F32)
        b = jnp.dot(h2, wu_ref[k], preferred_element_type=F32)
        act = (jax.nn.silu(a) * b * ce).astype(BF16)
        acc = acc + jnp.dot(act, wd_ref[k], preferred_element_type=F32)
    acc_ref[...] = acc

    @pl.when(g == N_EXPERT_GROUPS - 1)
    def _():
        x2 = x1_ref[...] + acc_ref[...]
        rows = x2.shape[0] // tsplit
        if tsplit == 1:
            p = p_ref[...]
        else:
            p = jnp.concatenate([p_ref[:, PLE_DIM * t:PLE_DIM * (t + 1)] for t in range(tsplit)], axis=0)
        gate = jax.nn.sigmoid(jnp.dot(x2.astype(BF16), wpg_ref[...], preferred_element_type=F32))
        x3 = x2 + gate * jnp.dot(p.astype(BF16), wp_ref[...], preferred_element_type=F32)
        inv = lax.rsqrt(jnp.mean(x3 * x3, axis=-1, keepdims=True) + RMS_EPS)
        y = x3 * inv * gf_ref[...]
        if tsplit == 1:
            y_ref[...] = y
        else:
            for t in range(tsplit):
                y_ref[:, D_MODEL * t:D_MODEL * (t + 1)] = y[rows * t:rows * (t + 1)]


def _moe(x1, h2, p, mp, tm, tsplit):
    rows = x1.shape[0]
    nrb = rows // tm
    rb = lambda r, g: (r, 0)
    grp = lambda r, g: (g, 0, 0)
    if tsplit == 1:
        p_spec = pl.BlockSpec((tm, PLE_DIM), rb)
        y_spec = pl.BlockSpec((tm, D_MODEL), rb)
        y_shape = (rows, D_MODEL)
    else:
        assert nrb == 1
        p_spec = _full(p.shape)
        y_shape = (rows // tsplit, tsplit * D_MODEL)
        y_spec = _full(y_shape)
    return pl.pallas_call(
        functools.partial(_moe_body, tsplit=tsplit),
        grid=(nrb, N_EXPERT_GROUPS),
        in_specs=[pl.BlockSpec((tm, D_MODEL), rb), pl.BlockSpec((tm, D_MODEL), rb), p_spec,
                  _full(mp["wr"].shape), _full(mp["br"].shape),
                  pl.BlockSpec((EXPERTS_PER_GROUP, D_MODEL, D_FF_EXPERT), grp),
                  pl.BlockSpec((EXPERTS_PER_GROUP, D_MODEL, D_FF_EXPERT), grp),
                  pl.BlockSpec((EXPERTS_PER_GROUP, D_FF_EXPERT, D_MODEL), grp),
                  _full(mp["wpg"].shape), _full(mp["wp"].shape), _full(mp["gf"].shape)],
        out_specs=y_spec,
        out_shape=jax.ShapeDtypeStruct(y_shape, F32),
        scratch_shapes=[pltpu.VMEM((tm, D_MODEL), F32), pltpu.VMEM((tm, LANES), F32)],
        compiler_params=_cparams(("arbitrary", "arbitrary"), VMEM_MOE_MIB),
        name="moe_ple",
    )(x1, h2, p, mp["wr"], mp["br"], mp["wg"], mp["wu"], mp["wd"], mp["wpg"], mp["wp"], mp["gf"])


def _s5_sample_body(u_ref, h0re_ref, h0im_ref, wre_ref, wim_ref, ar_ref, ai_ref, cre_ref, cim_ref, d_ref,
                    wglu_ref, bglu_ref, wso_ref, abr_ref, hre_out_ref, him_out_ref,
                    bure_ref, buim_ref, hre_ref, him_ref, *, nseq, nstep):
    u = u_ref[...]
    ub = u.astype(BF16)
    for j in range(4):
        lhs = ub[:, LANES * j:LANES * (j + 1)]
        bure_ref[:, 512 * j:512 * (j + 1)] = jnp.dot(lhs, wre_ref[j], preferred_element_type=F32)
        buim_ref[:, 512 * j:512 * (j + 1)] = jnp.dot(lhs, wim_ref[j], preferred_element_type=F32)
    for lc in range(4):
        sl = slice(512 * lc, 512 * (lc + 1))
        ar = jnp.broadcast_to(ar_ref[:, sl], (SUBLANES, 512))
        ai = jnp.broadcast_to(ai_ref[:, sl], (SUBLANES, 512))

        def body(rc, carry, sl=sl, ar=ar, ai=ai):
            r0 = pl.multiple_of(rc * SUBLANES, SUBLANES)
            hr = h0re_ref[pl.ds(r0, SUBLANES), sl]
            hi = h0im_ref[pl.ds(r0, SUBLANES), sl]
            for t in range(nstep):
                rr = pl.multiple_of(t * nseq + rc * SUBLANES, SUBLANES)
                hr, hi = (ar * hr - ai * hi + bure_ref[pl.ds(rr, SUBLANES), sl],
                          ar * hi + ai * hr + buim_ref[pl.ds(rr, SUBLANES), sl])
                hre_ref[pl.ds(rr, SUBLANES), sl] = hr
                him_ref[pl.ds(rr, SUBLANES), sl] = hi
            hre_out_ref[pl.ds(r0, SUBLANES), sl] = hr
            him_out_ref[pl.ds(r0, SUBLANES), sl] = hi
            return carry

        lax.fori_loop(0, nseq // SUBLANES, body, 0)
    parts = []
    for j in range(4):
        sl = slice(512 * j, 512 * (j + 1))
        parts.append(jnp.dot(hre_ref[:, sl].astype(BF16), cre_ref[j], preferred_element_type=F32)
                     - jnp.dot(him_ref[:, sl].astype(BF16), cim_ref[j], preferred_element_type=F32))
    y = jnp.concatenate(parts, axis=1) + d_ref[...] * u
    zg = jax.nn.gelu(y)
    gate = jnp.dot(zg.astype(BF16), wglu_ref[...], preferred_element_type=F32) + bglu_ref[...]
    glu = (zg * jax.nn.sigmoid(gate)).astype(BF16)
    abr_ref[...] = jnp.dot(glu, wso_ref[...], preferred_element_type=F32).astype(BF16)


def _s5_sample(u_ts, h0re, h0im, sp, wglu, bglu, wso, nseq, nstep):
    rows = nseq * nstep
    ops = [u_ts, h0re, h0im, sp["wre"], sp["wim"], sp["ar"], sp["ai"], sp["cre"], sp["cim"], sp["d"], wglu, bglu, wso]
    return pl.pallas_call(
        functools.partial(_s5_sample_body, nseq=nseq, nstep=nstep),
        grid=(1,),
        in_specs=[_full(o.shape) for o in ops],
        out_specs=[_full((rows, D_MODEL)), _full((nseq, N_STATE)), _full((nseq, N_STATE))],
        out_shape=[jax.ShapeDtypeStruct((rows, D_MODEL), BF16),
                   jax.ShapeDtypeStruct((nseq, N_STATE), F32), jax.ShapeDtypeStruct((nseq, N_STATE), F32)],
        scratch_shapes=[pltpu.VMEM((rows, N_STATE), F32) for _ in range(4)],
        compiler_params=_cparams(("arbitrary",), VMEM_LARGE_MIB),
        name="s5_sample",
    )(*ops)


def _softmax_rows(s, valid):
    sm = jnp.where(valid, s, NEG_INF)
    mx = jnp.max(sm, axis=1, keepdims=True)
    e = jnp.where(valid, jnp.exp2(sm - mx), 0.0)
    l = jnp.sum(e, axis=1, keepdims=True)
    return e * (1.0 / jnp.maximum(l, 1e-30))


SAMPLE_SEQS_PER_STEP = 4
CMP_PITCH = 24


def _attn_sample_body(pt_ref, q_ref, gn_ref, nks_ref, nkw_ref, wint_ref, wk_ref, bk_ref, w2k_ref, ov_ref, e_ref,
                      cmp_hbm, slc_hbm, o_ref, nwint_ref, xrow_ref, pages_ref, sem_ref, *, npage, past_len, nsub, tq):
    n = pl.program_id(0)
    nsteps = pl.num_programs(0)
    slot = lax.rem(n, 2)
    nrow = N_Q_HEADS * tq
    nwin = wint_ref.shape[2]
    nslc = -(-(past_len + tq) // SLC_BLOCK)
    nch = past_len // CMP_STRIDE
    per_page = PAGE_SIZE // CMP_STRIDE

    def page_copy(step, into, c, s, p):
        src = (cmp_hbm, slc_hbm)[c]
        return pltpu.make_async_copy(src.at[pt_ref[step * nsub + s, p]],
                                     pages_ref.at[into, (c * nsub + s) * npage + p], sem_ref.at[into])

    def all_pages(step, into, op):
        for c in range(2):
            for s in range(nsub):
                for p in range(npage):
                    op(page_copy(step, into, c, s, p))

    @pl.when(n == 0)
    def _():
        all_pages(0, 0, lambda cp: cp.start())

    all_pages(n, slot, lambda cp: cp.wait())
    nxt = jnp.minimum(n + 1, nsteps - 1)
    all_pages(nxt, 1 - slot, lambda cp: cp.start())

    def page(c, s, p):
        return pages_ref.at[slot, (c * nsub + s) * npage + p]

    cmp = []
    for kv in range(2):
        for s in range(nsub):
            for p in range(npage):
                rows = page(0, s, p)[LANES * kv:LANES * (kv + 1), :].T
                for c in range(per_page):
                    r0 = CMP_PITCH * (per_page * p + c)
                    xrow_ref[s, r0:r0 + CMP_STRIDE, :] = rows[CMP_STRIDE * c:CMP_STRIDE * (c + 1)]
        x = jnp.concatenate(
            [jnp.concatenate([xrow_ref[s, pl.ds(i, nch, stride=CMP_PITCH), :] for i in range(CMP_STRIDE)], axis=1)
             for s in range(nsub)], axis=0)
        pp = jnp.dot(x.astype(BF16), wk_ref[kv], preferred_element_type=F32)
        pre = pp[:, 0:LANES] + pltpu.roll(pp[:, LANES:2 * LANES], nsub * nch - 1, axis=0) + bk_ref[kv:kv + 1, :]
        cmp.append(jnp.dot(jax.nn.gelu(pre).astype(BF16), w2k_ref[kv], preferred_element_type=F32).astype(BF16))
    cks = [cmp[0][nch * s:nch * (s + 1)] for s in range(nsub)]
    cvs = [cmp[1][nch * s:nch * (s + 1)] for s in range(nsub)]
    seqs = range(nsub)
    rcat = lambda parts: jnp.concatenate(parts, axis=0)

    lane_w = lax.broadcasted_iota(jnp.int32, (KV_W, LANES), 1)
    lane8 = lax.broadcasted_iota(jnp.int32, (tq, LANES), 1)
    nks_l, nkw_l, wint_l, qs_l = [], [], [], []
    for s in seqs:
        rows_s = slice(tq * s, tq * (s + 1))
        nks_l.append(jnp.concatenate([nks_ref[rows_s, :], jnp.zeros((LANES - tq, KV_W), F32)], axis=0))
        nkw = jnp.concatenate([nkw_ref[rows_s, :], jnp.zeros((LANES - tq, KV_W), F32)], axis=0)
        nkw_l.append(nkw)
        wint = wint_ref[s]
        wint_l.append(wint)
        shifted = pltpu.roll(wint, nwin - tq, axis=1)
        new_t = pltpu.roll(nkw.T, LANES - tq, axis=1)
        nwint_ref[s, :, 0:nwin - LANES] = shifted[:, 0:nwin - LANES]
        nwint_ref[s, :, nwin - LANES:nwin] = jnp.where(lane_w >= LANES - tq, new_t, shifted[:, nwin - LANES:nwin])
        q = q_ref[rows_s, :]
        qrows = []
        for j in range(N_Q_HEADS):
            chunk = q[:, LANES * (j // 2):LANES * (j // 2 + 1)]
            dst = j // GQA
            if (j % 2) != dst:
                chunk = pltpu.roll(chunk, HEAD_DIM, axis=1)
            keep = (lane8 < HEAD_DIM) if dst == 0 else (lane8 >= HEAD_DIM)
            qrows.append(jnp.where(keep, chunk, 0.0))
        qs_l.append(jnp.concatenate(qrows, axis=0).astype(BF16))

    rtot = nsub * nrow
    seq_rows = [slice(nrow * s, nrow * (s + 1)) for s in seqs]
    pos = past_len + (lax.broadcasted_iota(jnp.int32, (rtot, LANES), 0) & (tq - 1))
    lane = lax.broadcasted_iota(jnp.int32, (rtot, LANES), 1)

    sc = rcat([_dot_t(qs_l[s], cks[s]) for s in seqs])
    pc = _softmax_rows(sc, lane * CMP_STRIDE + (CMP_LEN - 1) <= pos).astype(BF16)
    oc = rcat([jnp.dot(pc[seq_rows[s]], cvs[s], preferred_element_type=F32) for s in seqs])
    imp = jnp.dot(pc, ov_ref[...], preferred_element_type=F32)
    vs = []
    for s in seqs:
        for h in range(N_KV_HEADS):
            r0 = nrow * s + tq * GQA * h
            v = imp[r0:r0 + tq]
            for g in range(1, GQA):
                v = v + imp[r0 + tq * g:r0 + tq * (g + 1)]
            vs.append(v)
    nsel = len(vs) * tq
    vt = rcat(vs + [jnp.zeros((LANES - nsel, LANES), F32)]).T
    nblk_pad = -(-nslc // SUBLANES) * SUBLANES
    blk_t = lax.broadcasted_iota(jnp.int32, (nblk_pad, LANES), 0)
    pos_t = past_len + (lax.broadcasted_iota(jnp.int32, (nblk_pad, LANES), 1) & (tq - 1))
    neg_t = _select_blocks(vt[0:nblk_pad], blk_t, pos_t, nslc, axis=0)
    neg = rcat([neg_t, jnp.zeros((LANES - nblk_pad, LANES), F32)]).T
    negsel = rcat([neg[tq * (N_KV_HEADS * s + j // GQA):tq * (N_KV_HEADS * s + j // GQA + 1)]
                   for s in seqs for j in range(N_Q_HEADS)])
    negsel_b = negsel.astype(BF16)

    new_blk = past_len // SLC_BLOCK
    ss_l = []
    for s in seqs:
        qaug = jnp.concatenate([qs_l[s], negsel_b[seq_rows[s]]], axis=1)
        parts = []
        for p in range(0, npage, 2):
            kt = jnp.concatenate([page(1, s, p)[0:LANES, :], page(1, s, p + 1)[0:LANES, :]], axis=1).astype(BF16)
            et = jnp.concatenate([e_ref[p], e_ref[p + 1]], axis=1)
            parts.append(jnp.dot(qaug, jnp.concatenate([kt, et], axis=0), preferred_element_type=F32))
        parts.append(_dot_t(qs_l[s], nks_l[s][:, 0:LANES].astype(BF16)) + negsel[seq_rows[s], new_blk:new_blk + 1])
        ss_l.append(jnp.concatenate(parts, axis=1))
    ss = rcat(ss_l)
    nkeys = ss.shape[1]
    kpos = lax.broadcasted_iota(jnp.int32, (rtot, nkeys), 1)
    pos_k = past_len + (lax.broadcasted_iota(jnp.int32, (rtot, nkeys), 0) & (tq - 1))
    ps = _softmax_rows(ss, kpos <= pos_k).astype(BF16)
    osel_l = []
    for s in seqs:
        psq = ps[seq_rows[s]]
        o = jnp.dot(psq[:, past_len:nkeys], nks_l[s][:, LANES:2 * LANES].astype(BF16), preferred_element_type=F32)
        for p in range(0, npage, 2):
            vtp = jnp.concatenate([page(1, s, p)[LANES:2 * LANES, :], page(1, s, p + 1)[LANES:2 * LANES, :]],
                                  axis=1).astype(BF16)
            o = o + _dot_t(psq[:, PAGE_SIZE * p:PAGE_SIZE * (p + 2)], vtp)
        osel_l.append(o)
    osel = rcat(osel_l)

    sw = rcat([jnp.concatenate([jnp.dot(qs_l[s], wint_l[s][0:LANES].astype(BF16), preferred_element_type=F32),
                                _dot_t(qs_l[s], nkw_l[s][:, 0:LANES].astype(BF16))], axis=1) for s in seqs])
    nw = sw.shape[1]
    widx = lax.broadcasted_iota(jnp.int32, (rtot, nw), 1)
    pos_w = past_len + (lax.broadcasted_iota(jnp.int32, (rtot, nw), 0) & (tq - 1))
    dlt = pos_w - (past_len - nwin + widx)
    pw = _softmax_rows(sw, (dlt >= 0) & (dlt < WINDOW) & (widx < nwin + tq)).astype(BF16)
    ow = rcat([_dot_t(pw[seq_rows[s], 0:nwin], wint_l[s][LANES:2 * LANES].astype(BF16))
               + jnp.dot(pw[seq_rows[s], nwin:nw], nkw_l[s][:, LANES:2 * LANES].astype(BF16),
                         preferred_element_type=F32) for s in seqs])

    for s in seqs:
        rows_s = slice(tq * s, tq * (s + 1))
        gn = gn_ref[rows_s, :]
        for c in range(N_Q_HEADS // 2):
            halves = []
            for hh in range(2):
                j = 2 * c + hh
                rs = slice(nrow * s + tq * j, nrow * s + tq * (j + 1))
                oj = (gn[:, 3 * j:3 * j + 1] * oc[rs] + gn[:, 3 * j + 1:3 * j + 2] * osel[rs]
                      + gn[:, 3 * j + 2:3 * j + 3] * ow[rs])
                if (j // GQA) != hh:
                    oj = pltpu.roll(oj, HEAD_DIM, axis=1)
                halves.append(oj)
            o_ref[rows_s, LANES * c:LANES * (c + 1)] = jnp.where(lane8 < HEAD_DIM, halves[0], halves[1])

    @pl.when(n == nsteps - 1)
    def _():
        all_pages(nxt, 1 - slot, lambda cp: cp.wait())


def _attn_sample(q, gn, nks, nkw, cache_cmp, cache_slc, cache_win, page_table, cp, nseq, tq, past_len):
    assert tq <= CMP_STRIDE and past_len % PAGE_SIZE == 0
    npage = past_len // PAGE_SIZE
    assert npage % 2 == 0 and PAGE_SIZE == LANES
    n_pool = cache_cmp.shape[0]
    nwin = cache_win.shape[1]
    chunks = past_len // CMP_STRIDE
    ov = _overlap_t(chunks, LANES).T
    key = np.arange(past_len).reshape(npage, 1, PAGE_SIZE)
    e = jnp.asarray(np.arange(LANES).reshape(1, LANES, 1) == key // SLC_BLOCK, dtype=BF16)
    to_t = lambda c: jnp.transpose(c, (0, 2, 3, 4, 1)).reshape(c.shape[0], KV_W, c.shape[1])
    cmp_t, slc_t, win_t = to_t(cache_cmp), to_t(cache_slc), to_t(cache_win)
    nsub = SAMPLE_SEQS_PER_STEP
    assert nseq % nsub == 0
    row = lambda n, pt: (n, 0)
    seq3 = lambda n, pt: (n, 0, 0)
    consts = [cp["wk"], cp["bk"], cp["w2k"], ov, e]
    in_specs = [pl.BlockSpec((nsub * tq, Q_W), row), pl.BlockSpec((nsub * tq, LANES), row),
                pl.BlockSpec((nsub * tq, KV_W), row), pl.BlockSpec((nsub * tq, KV_W), row),
                pl.BlockSpec((nsub, KV_W, nwin), seq3)]
    in_specs += [pl.BlockSpec(c.shape, (lambda nd: lambda n, pt: (0,) * nd)(c.ndim)) for c in consts]
    in_specs += [pl.BlockSpec(memory_space=pl.ANY)] * 2
    grid_spec = pltpu.PrefetchScalarGridSpec(
        num_scalar_prefetch=1,
        grid=(nseq // nsub,),
        in_specs=in_specs,
        out_specs=[pl.BlockSpec((nsub * tq, Q_W), row), pl.BlockSpec((nsub, KV_W, nwin), seq3)],
        scratch_shapes=[pltpu.VMEM((nsub, chunks * CMP_PITCH, LANES), F32),
                        pltpu.VMEM((2, 2 * nsub * npage, KV_W, PAGE_SIZE), F32),
                        pltpu.SemaphoreType.DMA((2,))],
    )
    return pl.pallas_call(
        functools.partial(_attn_sample_body, npage=npage, past_len=past_len, nsub=nsub, tq=tq),
        grid_spec=grid_spec,
        out_shape=[jax.ShapeDtypeStruct((nseq * tq, Q_W), F32), jax.ShapeDtypeStruct((nseq, KV_W, nwin), F32)],
        compiler_params=_cparams(("arbitrary",), VMEM_LARGE_MIB),
        name="attn_sample",
    )(page_table, q, gn, nks, nkw, win_t, *consts, cmp_t, slc_t)


def _moe_params(w_rg, b_rg, w_re, b_re, w_gate, w_up, w_down, w_ple, w_ple_gate, gf):
    pad = LANES - N_EXPERT_GROUPS - N_EXPERTS
    return {"wr": jnp.pad(jnp.concatenate([w_rg, w_re], axis=1), ((0, 0), (0, pad))).astype(BF16),
            "br": jnp.pad(jnp.concatenate([b_rg, b_re]), (0, pad)).astype(F32).reshape(1, LANES),
            "wg": w_gate.astype(BF16), "wu": w_up.astype(BF16), "wd": w_down.astype(BF16),
            "wpg": w_ple_gate.astype(BF16), "wp": w_ple.astype(BF16), "gf": gf.astype(F32).reshape(1, D_MODEL)}


TM_PROMPT = 512
TM_MOE = 1024
TC_S5 = 128


def kernel(x_prompt, x_sample, p_prompt, p_sample, cache_cmp_kv, cache_slc_kv, cache_win_kv, state_ssm, page_table, norm1_g, w_in, ssm_lam_re, ssm_lam_im, ssm_log_dt, ssm_b_re, ssm_b_im, ssm_c_re, ssm_c_im, ssm_d, w_glu, b_glu, cmp_pe, cmp_w1, cmp_w2, w_ssm_out, w_nsa_out, w_o, norm2_g, w_route_group, b_route_group, w_route_expert, b_route_expert, w_exp_gate, w_exp_up, w_exp_down, w_ple, w_ple_gate, final_norm_g):
    assert w_in.shape[0] == 1, "one layer"
    l = 0
    nb, t = x_prompt.shape[:2]
    ns, ts = x_sample.shape[:2]
    past_len = page_table.shape[1] * PAGE_SIZE
    kvt = (2, N_KV_HEADS, HEAD_DIM)

    wi = _inproj_params(w_in[l])
    g1 = norm1_g[l].astype(F32).reshape(1, D_MODEL)
    g2 = norm2_g[l].astype(F32).reshape(1, D_MODEL)
    sp = _s5_params(ssm_lam_re[l], ssm_lam_im[l], ssm_log_dt[l], ssm_b_re[l], ssm_b_im[l], ssm_c_re[l], ssm_c_im[l],
                    ssm_d[l])
    cp = _cmp_params(cmp_pe[l], cmp_w1[l], cmp_w2[l])
    mp = _moe_params(w_route_group[l], b_route_group[l], w_route_expert[l], b_route_expert[l], w_exp_gate[l],
                     w_exp_up[l], w_exp_down[l], w_ple[l], w_ple_gate[l], final_norm_g)
    wglu = w_glu[l].astype(BF16)
    bglu = b_glu[l].astype(F32).reshape(1, SSM_WIDTH)
    wso = w_ssm_out[l].astype(BF16)
    wno = w_nsa_out[l].astype(BF16)
    wo = w_o[l].astype(BF16)

    lay = _prompt_layout(nb, t, TM_PROMPT)
    xp = x_prompt.reshape(nb * t, D_MODEL)
    r = _inproj_prompt(xp, lay, g1, wi)
    abr, hlast = _s5_prompt(r["u"], sp, wglu, bglu, wso, t, TC_S5)
    ck, cvt = _compress_prompt(r["kvc"], cp, nb, t)
    onsa = _attn_prompt(r["qt"], r["gnt"], ck, cvt, r["ksb"], r["vst"], r["kwb"], r["vwt"], nb, t)
    x1, h2 = _post(xp, abr, "a", onsa, r["ga"], r["gb"], wno, wo, g2, lay, "a")
    y_prompt = _moe(x1, h2, p_prompt[l].reshape(nb * t, PLE_DIM), mp, TM_MOE, 1).reshape(nb, t, D_MODEL)
    keep = min(WINDOW, t)

    def rows_last(a):
        return jnp.transpose(a.reshape((a.shape[0],) + kvt + (a.shape[2],)), (0, 4, 1, 2, 3))[None]

    new_cmp_p = rows_last(r["kvct"])
    new_slc_p = rows_last(r["kvst"])
    new_win_p = rows_last(r["kvwt"][:, :, t - keep:])
    new_ssm_p = jnp.stack([hlast[0:nb], hlast[nb:2 * nb]], axis=-1).reshape(1, nb, N_SSM_GROUPS, SSM_STATE, 2)

    lays = _sample_layout(ns, ts)
    xs = x_sample.reshape(ns * ts, D_MODEL)
    rs = _inproj_sample(xs, lays, g1, wi)
    h0 = state_ssm[l].astype(F32).reshape(ns, N_STATE, 2)
    abr_s, hre, him = _s5_sample(rs["u"], h0[..., 0], h0[..., 1], sp, wglu, bglu, wso, ns, ts)
    onsa_s, new_win = _attn_sample(rs["q"].reshape(ns * ts, Q_W), rs["gn"].reshape(ns * ts, LANES),
                                   rs["kvs"].reshape(ns * ts, KV_W), rs["kvw"].reshape(ns * ts, KV_W),
                                   cache_cmp_kv[l], cache_slc_kv[l], cache_win_kv[l], page_table, cp, ns, ts, past_len)
    x1s, h2s = _post(xs, abr_s, "b", onsa_s, rs["ga"], rs["gb"], wno, wo, g2, lays, "b")
    y_sample = _moe(x1s, h2s, p_sample[l].reshape(ns, ts * PLE_DIM), mp, ns * ts, ts).reshape(ns, ts, D_MODEL)
    steps_first = lambda a: jnp.transpose(a.reshape((ts,) + kvt + (ns,)), (4, 0, 1, 2, 3))[None]
    new_cmp_s = steps_first(rs["kvct"])
    new_slc_s = steps_first(rs["kvst"])
    new_win_s = rows_last(new_win)
    new_ssm_s = jnp.stack([hre, him], axis=-1).reshape(1, ns, N_SSM_GROUPS, SSM_STATE, 2)
    return (y_prompt, y_sample, new_cmp_p, new_slc_p, new_win_p, new_ssm_p,
            new_cmp_s, new_slc_s, new_win_s, new_ssm_s)
```

```python
import functools
import math

import jax
import jax.numpy as jnp
import numpy as np
from jax import lax
from jax.experimental import pallas as pl
from jax.experimental.pallas import tpu as pltpu

F32 = jnp.float32
BF16 = jnp.bfloat16

D_MODEL = 1024
SSM_WIDTH = 512
SSM_GROUP = 16
N_SSM_GROUPS = 32
SSM_STATE = 64
HEAD_DIM = 64
N_Q_HEADS = 8
N_KV_HEADS = 2
GQA = 4
CMP_LEN = 32
CMP_STRIDE = 16
SLC_BLOCK = 64
TOP_N = 8
WINDOW = 512
Q_BLOCK = 256
NEG_INF = -1e30
FORCE_BONUS = 1e4
Q_W = 512
KV_W = 256
NSA_GATE_W = 24
N_EXPERT_GROUPS = 4
EXPERTS_PER_GROUP = 4
N_EXPERTS = 16
D_FF_EXPERT = 256
PLE_DIM = 256
RMS_EPS = 1e-6
PAGE_SIZE = 128

LANES = 128
SUBLANES = 8
N_STATE = N_SSM_GROUPS * SSM_STATE
MIB = 2 ** 20
V7X_VMEM_MIB = 64
VMEM_SMALL_MIB = 48
VMEM_LARGE_MIB = 56
VMEM_MOE_MIB = V7X_VMEM_MIB - 4


def _cparams(sem, vmem_mib):
    return pltpu.CompilerParams(dimension_semantics=sem, vmem_limit_bytes=vmem_mib * MIB)


def _full(shape):
    nd = len(shape)
    return pl.BlockSpec(shape, lambda *_: (0,) * nd)


def _prompt_layout(nseq, t, tm):
    nb = t // tm
    return {
        "grid": (nb, nseq), "tm": tm, "nseq": nseq, "t": t,
        "a": lambda w: ((nseq * t, w), pl.BlockSpec((tm, w), lambda b, s: (s * nb + b, 0))),
    }


def _sample_layout(nseq, t):
    return {
        "grid": (1, t), "tm": nseq,
        "a": lambda w: ((nseq, t * w), pl.BlockSpec((nseq, w), lambda s, b: (0, b))),
        "b": lambda w: ((t * nseq, w), pl.BlockSpec((nseq, w), lambda s, b: (b, 0))),
    }


TK_SLC = 512
TK_WIN = 128


Q_SCALE = HEAD_DIM ** -0.5 * math.log2(math.e)
C_U, C_Q, C_KVC, C_KVS, C_KVW = 0, 512, 1024, 1280, 1536
N_MAIN = 1792


def _dot_t(a, b):
    return lax.dot_general(a, b, (((1,), (1,)), ((), ())), preferred_element_type=F32)


GN_ROWS = 32


def _inproj_prompt_body(x_ref, g_ref, wa_ref, wgn_ref,
                        u_ref, kvc_ref, ksb_ref, kwb_ref,
                        qt_ref, kvct_ref, kvst_ref, kvwt_ref, gnt_ref, vst_ref, vwt_ref, *, nseq):
    s = pl.program_id(1)
    x = x_ref[...]
    inv = lax.rsqrt(jnp.mean(x * x, axis=-1, keepdims=True) + RMS_EPS)
    h = (x * inv * g_ref[...]).astype(BF16)
    tm = h.shape[0]

    def mm(w):
        return jnp.dot(h, w, preferred_element_type=F32)

    u = mm(wa_ref[:, C_U:C_U + SSM_WIDTH])
    for j in range(SSM_WIDTH // LANES):
        u_ref[j, pl.ds(s, tm, stride=nseq), :] = u[:, LANES * j:LANES * (j + 1)]
    kvc = mm(wa_ref[:, C_KVC:C_KVC + KV_W])
    kvc_ref[0] = kvc[:, 0:LANES]
    kvc_ref[1] = kvc[:, LANES:2 * LANES]
    kvct_ref[0] = kvc.T
    kvs = mm(wa_ref[:, C_KVS:C_KVS + KV_W])
    ksb_ref[...] = kvs[:, 0:LANES].astype(BF16)
    kvst = kvs.T
    kvst_ref[0] = kvst
    kvw = mm(wa_ref[:, C_KVW:C_KVW + KV_W])
    kwb_ref[...] = kvw[:, 0:LANES].astype(BF16)
    kvwt = kvw.T
    kvwt_ref[0] = kvwt
    for c in range(tm // TK_WIN):
        vst_ref[0, c] = kvst[LANES:2 * LANES, c * TK_WIN:(c + 1) * TK_WIN].astype(BF16)
        vwt_ref[0, c] = kvwt[LANES:2 * LANES, c * TK_WIN:(c + 1) * TK_WIN].astype(BF16)
    qt_ref[0] = (mm(wa_ref[:, C_Q:C_Q + Q_W]) * Q_SCALE).T.astype(BF16)
    gnt_ref[0] = jax.nn.sigmoid(mm(wgn_ref[...])).T[0:GN_ROWS]


def _inproj_prompt(x2d, lay, g, w):
    tm, nseq, t = lay["tm"], lay["nseq"], lay["t"]
    nb = t // tm
    out_shapes, out_specs, names = [], [], []

    def add(name, shape_spec, dt):
        names.append(name)
        out_shapes.append(jax.ShapeDtypeStruct(shape_spec[0], dt))
        out_specs.append(shape_spec[1])

    def tr(rows):
        return (nseq, rows, t), pl.BlockSpec((1, rows, tm), lambda b, s: (s, 0, b))

    nu = SSM_WIDTH // LANES
    add("u", ((nu, t * nseq, LANES), pl.BlockSpec((nu, tm * nseq, LANES), lambda b, s: (0, b, 0))), F32)
    add("kvc", ((2, nseq * t, LANES), pl.BlockSpec((2, tm, LANES), lambda b, s: (0, s * nb + b, 0))), F32)
    add("ksb", lay["a"](LANES), BF16)
    add("kwb", lay["a"](LANES), BF16)
    add("qt", tr(Q_W), BF16)
    add("kvct", tr(KV_W), F32)
    add("kvst", tr(KV_W), F32)
    add("kvwt", tr(KV_W), F32)
    add("gnt", tr(GN_ROWS), F32)
    for name in ("vst", "vwt"):
        add(name, ((nseq, t // TK_WIN, LANES, TK_WIN),
                   pl.BlockSpec((1, tm // TK_WIN, LANES, TK_WIN), lambda b, s: (s, b, 0, 0))), BF16)
    x_shape, x_spec = lay["a"](D_MODEL)
    ops = [g, w["wa"], w["wgn"]]
    outs = pl.pallas_call(
        functools.partial(_inproj_prompt_body, nseq=nseq),
        grid=lay["grid"],
        in_specs=[x_spec] + [_full(o.shape) for o in ops],
        out_specs=out_specs,
        out_shape=out_shapes,
        compiler_params=_cparams(("arbitrary",) * 2, VMEM_LARGE_MIB),
        name="inproj_prompt",
    )(x2d.reshape(x_shape), *ops)
    return dict(zip(names, outs))


def _inproj_sample_body(x_ref, g_ref, wa_ref, wgn_ref,
                        u_ref, q_ref, kvs_ref, kvw_ref, gn_ref, kvct_ref, kvst_ref, kvwt_ref):
    x = x_ref[...]
    inv = lax.rsqrt(jnp.mean(x * x, axis=-1, keepdims=True) + RMS_EPS)
    h = (x * inv * g_ref[...]).astype(BF16)

    def mm(w):
        return jnp.dot(h, w, preferred_element_type=F32)

    u_ref[...] = mm(wa_ref[:, C_U:C_U + SSM_WIDTH])
    q_ref[...] = mm(wa_ref[:, C_Q:C_Q + Q_W]) * Q_SCALE
    kvs = mm(wa_ref[:, C_KVS:C_KVS + KV_W])
    kvs_ref[...] = kvs
    kvw = mm(wa_ref[:, C_KVW:C_KVW + KV_W])
    kvw_ref[...] = kvw
    gn_ref[...] = jax.nn.sigmoid(mm(wgn_ref[...]))
    kvct_ref[0] = mm(wa_ref[:, C_KVC:C_KVC + KV_W]).T
    kvst_ref[0] = kvs.T
    kvwt_ref[0] = kvw.T


def _inproj_sample(x2d, lay, g, w):
    nseq = lay["tm"]
    ts = lay["grid"][1]
    names = ["u", "q", "kvs", "kvw", "gn"]
    widths = [SSM_WIDTH, Q_W, KV_W, KV_W, LANES]
    out_shapes, out_specs = [], []
    for n, wd in zip(names, widths):
        shp, spec = lay["b" if n == "u" else "a"](wd)
        out_shapes.append(jax.ShapeDtypeStruct(shp, F32))
        out_specs.append(spec)
    for n in ("kvct", "kvst", "kvwt"):
        names.append(n)
        out_shapes.append(jax.ShapeDtypeStruct((ts, KV_W, nseq), F32))
        out_specs.append(pl.BlockSpec((1, KV_W, nseq), lambda s, b: (b, 0, 0)))
    x_shape, x_spec = lay["a"](D_MODEL)
    ops = [g, w["wa"], w["wgn"]]
    outs = pl.pallas_call(
        _inproj_sample_body,
        grid=lay["grid"],
        in_specs=[x_spec] + [_full(o.shape) for o in ops],
        out_specs=out_specs,
        out_shape=out_shapes,
        compiler_params=_cparams(("arbitrary",) * 2, VMEM_LARGE_MIB),
        name="inproj_sample",
    )(x2d.reshape(x_shape), *ops)
    return dict(zip(names, outs))


def _inproj_params(w_in0):
    return {"wa": w_in0[:, :N_MAIN].astype(BF16),
            "wgn": jnp.pad(w_in0[:, N_MAIN:N_MAIN + NSA_GATE_W], ((0, 0), (0, LANES - NSA_GATE_W))).astype(BF16),
            "wgab": w_in0[:, N_MAIN + NSA_GATE_W:].astype(BF16)}


S5_LANES = 1024


def _s5_prompt_body(u_ref, wb_ref, ar_ref, ai_ref, cw_ref, d_ref, wglu_ref, bglu_ref, wso_ref,
                    abr_ref, hlast_ref, lhs_ref, bu_ref, h8_ref, p_ref, hstate_ref):
    c = pl.program_id(0)
    nseq = 4
    r4 = u_ref.shape[1]
    tc = r4 // nseq
    half = tc // 2

    @pl.when(c == 0)
    def _():
        hstate_ref[...] = jnp.zeros_like(hstate_ref)

    u = jnp.concatenate([u_ref[j] for j in range(SSM_WIDTH // LANES)], axis=1)
    row2 = lax.broadcasted_iota(jnp.int32, (r4, SSM_WIDTH), 0)
    lo2 = (row2 % SUBLANES) < nseq
    up = pltpu.roll(u, r4 - nseq, axis=0)
    dn = pltpu.roll(u, nseq, axis=0)
    swapped = jnp.where(lo2, up, dn)
    zero = jnp.zeros_like(u)
    ev_re = jnp.where(lo2, u, zero).astype(BF16).reshape(half, SUBLANES, SSM_WIDTH)
    ev_im = jnp.where(lo2, zero, swapped).astype(BF16).reshape(half, SUBLANES, SSM_WIDTH)
    od_re = jnp.where(lo2, swapped, zero).astype(BF16).reshape(half, SUBLANES, SSM_WIDTH)
    od_im = jnp.where(lo2, zero, u).astype(BF16).reshape(half, SUBLANES, SSM_WIDTH)
    for j in range(4):
        sl = slice(LANES * j, LANES * (j + 1))
        lhs_ref[:, 0:8, 256 * j:256 * j + LANES] = ev_re[:, :, sl]
        lhs_ref[:, 0:8, 256 * j + LANES:256 * (j + 1)] = ev_im[:, :, sl]
        lhs_ref[:, 8:16, 256 * j:256 * j + LANES] = od_re[:, :, sl]
        lhs_ref[:, 8:16, 256 * j + LANES:256 * (j + 1)] = od_im[:, :, sl]
    for j in range(4):
        lhs = lhs_ref[:, :, 256 * j:256 * (j + 1)].reshape(tc * SUBLANES, 256)
        bu_ref[:, 512 * j:512 * (j + 1)] = jnp.dot(lhs, wb_ref[j], preferred_element_type=F32)

    for lc in range(N_STATE // S5_LANES):
        sl = slice(S5_LANES * lc, S5_LANES * (lc + 1))
        ar = ar_ref[:, sl]
        ai = ai_ref[:, sl]

        def step(t, h, sl=sl, ar=ar, ai=ai):
            r0 = pl.multiple_of(t * SUBLANES, SUBLANES)
            h = ar * h + ai * pltpu.roll(h, nseq, axis=0) + bu_ref[pl.ds(r0, SUBLANES), sl]
            h8_ref[pl.ds(r0, SUBLANES), sl] = h
            return h

        hstate_ref[:, sl] = lax.fori_loop(0, tc, step, hstate_ref[:, sl], unroll=8)
    hlast_ref[...] = hstate_ref[...]

    for j in range(4):
        pj = jnp.dot(h8_ref[:, 512 * j:512 * (j + 1)].astype(BF16), cw_ref[j], preferred_element_type=F32)
        p_ref[2 * j] = pj[:, 0:LANES]
        p_ref[2 * j + 1] = pj[:, LANES:2 * LANES]
    ys = []
    for s in range(nseq):
        parts = []
        for j in range(4):
            re = p_ref[2 * j, pl.ds(s, tc, stride=SUBLANES), :]
            im = p_ref[2 * j + 1, pl.ds(nseq + s, tc, stride=SUBLANES), :]
            us = u_ref[j, pl.ds(s, tc, stride=nseq), :]
            parts.append(re + im + d_ref[:, LANES * j:LANES * (j + 1)] * us)
        ys.append(jnp.concatenate(parts, axis=1))
    y = jnp.concatenate(ys, axis=0)
    zg = jax.nn.gelu(y)
    gate = jnp.dot(zg.astype(BF16), wglu_ref[...], preferred_element_type=F32) + bglu_ref[...]
    glu = (zg * jax.nn.sigmoid(gate)).astype(BF16)
    abr = jnp.dot(glu, wso_ref[...], preferred_element_type=F32)
    for s in range(nseq):
        abr_ref[s] = abr[s * tc:(s + 1) * tc].astype(BF16)


def _s5_prompt(u_ts, sp, wglu, bglu, wso, t_total, tc):
    nseq = 4
    grid = (t_total // tc,)
    abr, hlast = pl.pallas_call(
        _s5_prompt_body,
        grid=grid,
        in_specs=[pl.BlockSpec((SSM_WIDTH // LANES, tc * nseq, LANES), lambda c: (0, c, 0)),
                  _full(sp["wb8"].shape), _full(sp["ar8"].shape), _full(sp["ai8"].shape), _full(sp["cw8"].shape),
                  _full(sp["d"].shape), _full(wglu.shape), _full(bglu.shape), _full(wso.shape)],
        out_specs=[pl.BlockSpec((nseq, tc, D_MODEL), lambda c: (0, c, 0)),
                   pl.BlockSpec((SUBLANES, N_STATE), lambda c: (0, 0))],
        out_shape=[jax.ShapeDtypeStruct((nseq, t_total, D_MODEL), BF16),
                   jax.ShapeDtypeStruct((SUBLANES, N_STATE), F32)],
        scratch_shapes=[pltpu.VMEM((tc // 2, 2 * SUBLANES, 1024), BF16),
                        pltpu.VMEM((tc * SUBLANES, N_STATE), F32),
                        pltpu.VMEM((tc * SUBLANES, N_STATE), F32),
                        pltpu.VMEM((8, tc * SUBLANES, LANES), F32),
                        pltpu.VMEM((SUBLANES, N_STATE), F32)],
        compiler_params=_cparams(("arbitrary",), VMEM_LARGE_MIB),
        name="s5_prompt",
    )(u_ts, sp["wb8"], sp["ar8"], sp["ai8"], sp["cw8"], sp["d"], wglu, bglu, wso)
    return abr, hlast


def _s5_params(lam_re, lam_im, log_dt, b_re, b_im, c_re, c_im, d_skip):
    lam = lax.complex(lam_re.astype(F32), lam_im.astype(F32))
    dt = jnp.exp(log_dt.astype(F32))[:, None]
    a_bar = jnp.exp(lam * dt)
    b = lax.complex(b_re.astype(F32), b_im.astype(F32))
    b_bar = ((a_bar - 1.0) / lam)[..., None] * b
    eye8 = jnp.eye(8, dtype=F32)

    def bd_b(m):
        return jnp.einsum("ab,jbpc->jacbp", eye8, m.reshape(4, 8, SSM_STATE, SSM_GROUP)).reshape(4, 128, 512)

    def bd_c(m):
        return jnp.einsum("ab,jbcp->japbc", eye8, m.reshape(4, 8, SSM_GROUP, SSM_STATE)).reshape(4, 512, 128)

    wre, wim = bd_b(b_bar.real), bd_b(b_bar.imag)
    cre, cim = bd_c(c_re.astype(F32)), bd_c(c_im.astype(F32))
    ar = a_bar.real.reshape(1, N_STATE)
    ai = a_bar.imag.reshape(1, N_STATE)
    sign = jnp.concatenate([-jnp.ones((4, 1), F32), jnp.ones((4, 1), F32)], axis=0)
    return {
        "wb8": jnp.concatenate([wre, wim], axis=1).astype(BF16),
        "cw8": jnp.concatenate([cre, -cim], axis=2).astype(BF16),
        "ar8": jnp.broadcast_to(ar, (SUBLANES, N_STATE)),
        "ai8": sign * ai,
        "wre": wre.astype(BF16), "wim": wim.astype(BF16),
        "cre": cre.astype(BF16), "cim": cim.astype(BF16),
        "ar": ar, "ai": ai,
        "d": d_skip.astype(F32).reshape(1, SSM_WIDTH),
    }


def _cmp_params(cmp_pe, cmp_w1, cmp_w2):
    eye2 = jnp.eye(2, dtype=F32)
    nhalf = CMP_LEN // CMP_STRIDE
    w1r = cmp_w1.astype(F32).reshape(2, nhalf, CMP_STRIDE, HEAD_DIM, HEAD_DIM)
    wk = jnp.einsum("kside,ph->kipdshe", w1r, eye2).reshape(2, CMP_STRIDE * LANES, nhalf * LANES)
    bk = jnp.einsum("kld,klde->ke", cmp_pe.astype(F32), cmp_w1.astype(F32), precision=lax.Precision.HIGHEST)
    w2k = jnp.einsum("kef,ph->kpehf", cmp_w2.astype(F32), eye2).reshape(2, LANES, LANES)
    return {"wk": wk.astype(BF16), "bk": jnp.tile(bk, (1, N_KV_HEADS)), "w2k": w2k.astype(BF16),
            "w2kt": jnp.swapaxes(w2k, 1, 2).astype(BF16)}


def _compress_hidden(tap, nch, kv, wk_ref, bk_ref):
    x = jnp.concatenate([tap(i).astype(BF16) for i in range(CMP_STRIDE)], axis=1)
    pp = jnp.dot(x, wk_ref[kv], preferred_element_type=F32)
    pre = pp[:, 0:LANES] + pltpu.roll(pp[:, LANES:2 * LANES], nch - 1, axis=0) + bk_ref[kv:kv + 1, :]
    return jax.nn.gelu(pre).astype(BF16)


def _compress_prompt_body(x_ref, wk_ref, bk_ref, w2k_ref, w2kt_ref, ck_ref, cvt_ref):
    nch = x_ref.shape[1] // CMP_STRIDE
    hid = [_compress_hidden(lambda i, kv=kv: x_ref[kv, pl.ds(i, nch, stride=CMP_STRIDE), :], nch, kv, wk_ref, bk_ref)
           for kv in range(2)]
    ck_ref[0] = jnp.dot(hid[0], w2k_ref[0], preferred_element_type=F32).astype(BF16)
    cvt_ref[0] = _dot_t(w2kt_ref[1], hid[1]).astype(BF16)


def _compress_prompt(kvc2, cp, nseq, t):
    nch = t // CMP_STRIDE
    return pl.pallas_call(
        _compress_prompt_body,
        grid=(nseq,),
        in_specs=[pl.BlockSpec((2, t, LANES), lambda n: (0, n, 0)),
                  _full(cp["wk"].shape), _full(cp["bk"].shape), _full(cp["w2k"].shape), _full(cp["w2kt"].shape)],
        out_specs=[pl.BlockSpec((1, nch, LANES), lambda n: (n, 0, 0)),
                   pl.BlockSpec((1, LANES, nch), lambda n: (n, 0, 0))],
        out_shape=[jax.ShapeDtypeStruct((nseq, nch, LANES), BF16),
                   jax.ShapeDtypeStruct((nseq, LANES, nch), BF16)],
        compiler_params=_cparams(("arbitrary",), VMEM_SMALL_MIB),
        name="compress_prompt",
    )(kvc2, cp["wk"], cp["bk"], cp["w2k"], cp["w2kt"])


def _overlap_t(n_cmp_pad, n_slc_pad):
    j = np.arange(n_cmp_pad)[None, :]
    s = np.arange(n_slc_pad)[:, None]
    ov = (j * CMP_STRIDE <= s * SLC_BLOCK + SLC_BLOCK - 1) & (j * CMP_STRIDE + CMP_LEN - 1 >= s * SLC_BLOCK)
    return jnp.asarray(ov, dtype=BF16)


def _softmax_cols(s, valid):
    sm = jnp.where(valid, s, NEG_INF)
    mx = jnp.max(sm, axis=0, keepdims=True)
    e = jnp.where(valid, jnp.exp2(sm - mx), 0.0)
    l = jnp.sum(e, axis=0, keepdims=True)
    return e * (1.0 / jnp.maximum(l, 1e-30))


def _select_blocks(imp, blk, pos, nblk, axis=0):
    cur = pos // SLC_BLOCK
    forced = (blk == 0) | (blk == cur) | (blk == cur - 1)
    v = jnp.where(forced, imp + FORCE_BONUS, imp)
    v = jnp.where(blk * SLC_BLOCK <= pos, v, NEG_INF)
    v = jnp.where(blk < nblk, v, -3e38)
    blk_f = blk.astype(F32)
    neg = jnp.full(imp.shape, NEG_INF, F32)
    for _ in range(min(TOP_N, nblk)):
        mx = jnp.max(v, axis=axis, keepdims=True)
        first = jnp.min(jnp.where(v == mx, blk_f, float(imp.shape[axis])), axis=axis, keepdims=True)
        pick = blk_f == first
        neg = jnp.where(pick, 0.0, neg)
        v = jnp.where(pick, -3e38, v)
    return neg


CB = 2 * LANES


def _attn_prompt_body(q_ref, gn_ref, ck_ref, cvt_ref, ks_ref, vst_ref, kw_ref, vwt_ref, ovt_ref,
                      o_ref, kaug_ref, kwaug_ref, qaug_ref, acc_ref):
    i = pl.program_id(1)
    t = ks_ref.shape[1]
    nch = ck_ref.shape[1]
    nslc = t // SLC_BLOCK
    qb = Q_BLOCK
    ncol = N_Q_HEADS * qb
    ncb = ncol // CB
    q0 = i * qb
    one_row = 2 * LANES - HEAD_DIM
    hrows = [slice(HEAD_DIM * ((CB * cb // qb) // GQA), HEAD_DIM * ((CB * cb // qb) // GQA + 1)) for cb in range(ncb)]

    @pl.when(i == 0)
    def _():
        kaug_ref[:, 0:LANES] = ks_ref[0]
        blk = lax.broadcasted_iota(jnp.int32, (t, LANES), 0) // SLC_BLOCK
        col = lax.broadcasted_iota(jnp.int32, (t, LANES), 1)
        kaug_ref[:, LANES:2 * LANES] = jnp.where(blk == col, 1.0, 0.0).astype(BF16)
        padcol = lax.broadcasted_iota(jnp.int32, (WINDOW, 2 * LANES), 1)
        kwaug_ref[0:WINDOW, :] = jnp.where(padcol == one_row, NEG_INF, 0.0).astype(BF16)
        kwaug_ref[WINDOW:WINDOW + t, 0:LANES] = kw_ref[0]
        kwaug_ref[WINDOW:WINDOW + t, LANES:2 * LANES] = jnp.zeros((t, LANES), BF16)

    zeros64 = jnp.zeros((HEAD_DIM, qb), BF16)
    for j in range(N_Q_HEADS):
        dst = j // GQA
        qaug_ref[HEAD_DIM * dst:HEAD_DIM * (dst + 1), qb * j:qb * (j + 1)] = q_ref[0, HEAD_DIM * j:HEAD_DIM * (j + 1), :]
        qaug_ref[HEAD_DIM * (1 - dst):HEAD_DIM * (2 - dst), qb * j:qb * (j + 1)] = zeros64
    tail_row = lax.broadcasted_iota(jnp.int32, (HEAD_DIM, ncol), 0)
    qaug_ref[one_row:2 * LANES, :] = jnp.where(tail_row == 0, 1.0, 0.0).astype(BF16)

    pos_c = q0 + (lax.broadcasted_iota(jnp.int32, (nch, CB), 1) & (qb - 1))
    cvalid = lax.broadcasted_iota(jnp.int32, (nch, CB), 0) * CMP_STRIDE + (CMP_LEN - 1) <= pos_c
    scs = [jnp.dot(ck_ref[0], qaug_ref[0:LANES, CB * cb:CB * (cb + 1)], preferred_element_type=F32)
           for cb in range(ncb)]
    pcs = [_softmax_cols(sc, cvalid).astype(BF16) for sc in scs]
    oc = jnp.concatenate([jnp.dot(cvt_ref[0, hrows[cb], :], pc, preferred_element_type=F32)
                          for cb, pc in enumerate(pcs)], axis=1)
    imp = jnp.concatenate([jnp.dot(ovt_ref[...], pc, preferred_element_type=F32) for pc in pcs], axis=1)

    qaug_ref[LANES:one_row, :] = jnp.zeros((one_row - LANES, ncol), BF16)
    q_blocks = qb // TK_WIN
    npiece = (WINDOW + qb) // TK_WIN
    kws = [kwaug_ref[pl.ds(pl.multiple_of(q0 + w * TK_WIN, TK_WIN), TK_WIN), :] for w in range(npiece)]
    vwt = jnp.concatenate([vwt_ref[0, jnp.maximum(i * q_blocks + w - WINDOW // TK_WIN, 0)] for w in range(npiece)],
                          axis=1)
    wrow = lax.broadcasted_iota(jnp.int32, (TK_WIN, CB), 0)
    wcol = lax.broadcasted_iota(jnp.int32, (TK_WIN, CB), 1) & (qb - 1)
    wbias = []
    for w in range(npiece):
        lo, hi = w * TK_WIN - WINDOW, w * TK_WIN - WINDOW + TK_WIN - 1
        if hi <= 0 and qb - 1 - lo < WINDOW:
            wbias.append(None)
        else:
            dlt = wcol - wrow - lo
            wbias.append(jnp.where((dlt >= 0) & (dlt < WINDOW), 0.0, NEG_INF))
    kw_all = jnp.concatenate(kws, axis=0)
    sws = [jnp.dot(kw_all, qaug_ref[:, CB * cb:CB * (cb + 1)], preferred_element_type=F32) for cb in range(ncb)]
    es, rls = [], []
    for s in sws:
        s = jnp.concatenate([s[TK_WIN * w:TK_WIN * (w + 1)] if b is None else s[TK_WIN * w:TK_WIN * (w + 1)] + b
                             for w, b in enumerate(wbias)], axis=0)
        e = jnp.exp2(s - jnp.max(s, axis=0, keepdims=True))
        es.append(e.astype(BF16))
        rls.append(1.0 / jnp.sum(e, axis=0, keepdims=True))
    ow = jnp.concatenate([jnp.dot(vwt[hrows[cb]], e, preferred_element_type=F32) * rl
                          for cb, (e, rl) in enumerate(zip(es, rls))], axis=1)

    blk = lax.broadcasted_iota(jnp.int32, (nslc, N_KV_HEADS * qb), 0)
    pos_q = q0 + (lax.broadcasted_iota(jnp.int32, (nslc, N_KV_HEADS * qb), 1) & (qb - 1))
    vs = []
    for h in range(N_KV_HEADS):
        v = imp[0:nslc, qb * GQA * h:qb * GQA * h + qb]
        for g in range(1, GQA):
            v = v + imp[0:nslc, qb * (GQA * h + g):qb * (GQA * h + g + 1)]
        vs.append(v)
    neg = _select_blocks(jnp.concatenate(vs, axis=1), blk, pos_q, nslc).astype(BF16)
    for j in range(N_Q_HEADS):
        h = j // GQA
        qaug_ref[LANES:LANES + nslc, qb * j:qb * (j + 1)] = neg[:, qb * h:qb * (h + 1)]

    brow = lax.broadcasted_iota(jnp.int32, (qb, CB), 0)
    bcol = lax.broadcasted_iota(jnp.int32, (qb, CB), 1) & (qb - 1)
    tri_lo = jnp.where(brow <= bcol, 0.0, NEG_INF)

    acc_ref[...] = jnp.zeros_like(acc_ref)

    def sel_tile(k0, nk, vt, carry, bias):
        m, l = carry
        ka = kaug_ref[pl.ds(k0, nk), :]
        css = [slice(CB * cb, CB * (cb + 1)) for cb in range(ncb)]
        ss = [jnp.dot(ka, qaug_ref[:, cs], preferred_element_type=F32) for cs in css]
        ms, ls, ps, alphas = [], [], [], []
        for cs, s in zip(css, ss):
            if bias is not None:
                s = s + bias
            mn = jnp.maximum(m[:, cs], jnp.max(s, axis=0, keepdims=True))
            alpha = jnp.exp2(m[:, cs] - mn)
            p = jnp.exp2(s - mn)
            ms.append(mn)
            ls.append(alpha * l[:, cs] + jnp.sum(p, axis=0, keepdims=True))
            ps.append(p.astype(BF16))
            alphas.append(alpha)
        pvs = [jnp.dot(vt[hrows[cb]], p, preferred_element_type=F32) for cb, p in enumerate(ps)]
        for cs, alpha, pv in zip(css, alphas, pvs):
            acc_ref[:, cs] = alpha * acc_ref[:, cs] + pv
        return jnp.concatenate(ms, axis=1), jnp.concatenate(ls, axis=1)

    def vt_blocks(ref, b0, n):
        return jnp.concatenate([ref[0, b0 + j] for j in range(n)], axis=1) if n > 1 else ref[0, b0]

    big_blocks = TK_SLC // TK_WIN

    def big_tile(kt, carry):
        return sel_tile(pl.multiple_of(kt * TK_SLC, TK_SLC), TK_SLC, vt_blocks(vst_ref, kt * big_blocks, big_blocks),
                        carry, None)

    def small_tile(kb, carry):
        return sel_tile(pl.multiple_of(kb * qb, qb), qb, vt_blocks(vst_ref, kb * q_blocks, q_blocks), carry, None)

    carry = (jnp.full((1, ncol), NEG_INF, F32), jnp.zeros((1, ncol), F32))
    nbig = q0 // TK_SLC
    carry = lax.fori_loop(0, nbig // 2, lambda kp, c: big_tile(2 * kp + 1, big_tile(2 * kp, c)), carry)
    carry = lax.fori_loop(nbig - nbig % 2, nbig, big_tile, carry)
    carry = lax.fori_loop(nbig * (TK_SLC // qb), i, small_tile, carry)
    _, l = sel_tile(pl.multiple_of(q0, qb), qb, vt_blocks(vst_ref, i * q_blocks, q_blocks), carry, tri_lo)
    osel = acc_ref[...] * (1.0 / l)

    gt = gn_ref[0]
    for c in range(N_Q_HEADS // 2):
        rows = []
        for hh in range(2):
            j = 2 * c + hh
            cs = slice(qb * j, qb * (j + 1))
            rows.append(gt[3 * j:3 * j + 1, :] * oc[:, cs] + gt[3 * j + 1:3 * j + 2, :] * osel[:, cs]
                        + gt[3 * j + 2:3 * j + 3, :] * ow[:, cs])
        o_ref[:, LANES * c:LANES * (c + 1)] = jnp.concatenate(rows, axis=0).T.astype(o_ref.dtype)


def _attn_prompt(q, gn, ck, cvt, ksb, vst, kwb, vwt, nseq, t):
    nb = t // Q_BLOCK
    nch = t // CMP_STRIDE
    nslc = t // SLC_BLOCK
    ovt = _overlap_t(nch, max(nslc, SUBLANES))
    row = lambda n, i: (n * nb + i, 0)
    seq3 = lambda n, i: (n, 0, 0)
    seq4 = lambda n, i: (n, 0, 0, 0)
    col3 = lambda n, i: (n, 0, i)
    return pl.pallas_call(
        _attn_prompt_body,
        grid=(nseq, nb),
        in_specs=[pl.BlockSpec((1, Q_W, Q_BLOCK), col3), pl.BlockSpec((1, gn.shape[1], Q_BLOCK), col3),
                  pl.BlockSpec((1, nch, LANES), seq3), pl.BlockSpec((1, LANES, nch), seq3),
                  pl.BlockSpec((1, t, LANES), seq3), pl.BlockSpec((1, t // TK_WIN, LANES, TK_WIN), seq4),
                  pl.BlockSpec((1, t, LANES), seq3), pl.BlockSpec((1, t // TK_WIN, LANES, TK_WIN), seq4),
                  _full(ovt.shape)],
        out_specs=pl.BlockSpec((Q_BLOCK, Q_W), row),
        out_shape=jax.ShapeDtypeStruct((nseq * t, Q_W), BF16),
        scratch_shapes=[pltpu.VMEM((t, 2 * LANES), BF16),
                        pltpu.VMEM((WINDOW + t, 2 * LANES), BF16),
                        pltpu.VMEM((2 * LANES, N_Q_HEADS * Q_BLOCK), BF16),
                        pltpu.VMEM((HEAD_DIM, N_Q_HEADS * Q_BLOCK), F32)],
        compiler_params=_cparams(("arbitrary", "arbitrary"), VMEM_LARGE_MIB),
        name="attn_prompt",
    )(q, gn, ck, cvt, ksb.reshape(nseq, t, LANES), vst, kwb.reshape(nseq, t, LANES), vwt, ovt)


def _post_body(x_ref, abr_ref, on_ref, g1_ref, wgab_ref, wno_ref, wo_ref, g2_ref, x1_ref, h2_ref):
    x = x_ref[...]
    inv1 = lax.rsqrt(jnp.mean(x * x, axis=-1, keepdims=True) + RMS_EPS)
    h = (x * inv1 * g1_ref[...]).astype(BF16)
    ga = jax.nn.sigmoid(jnp.dot(h, wgab_ref[:, 0:D_MODEL], preferred_element_type=F32))
    gb = jax.nn.sigmoid(jnp.dot(h, wgab_ref[:, D_MODEL:2 * D_MODEL], preferred_element_type=F32))
    bbr = jnp.dot(on_ref[...].astype(BF16), wno_ref[...], preferred_element_type=F32)
    merged = ga * abr_ref[...].astype(F32) + gb * bbr
    x1 = x + jnp.dot(merged.astype(BF16), wo_ref[...], preferred_element_type=F32)
    x1_ref[...] = x1
    inv = lax.rsqrt(jnp.mean(x1 * x1, axis=-1, keepdims=True) + RMS_EPS)
    h2_ref[...] = (x1 * inv * g2_ref[...]).astype(BF16)


def _post(x2d, abr, abr_lay, onsa, g1, wgab, wno, wo, g2, lay, out_lay):
    x_shape, x_spec = lay["a"](D_MODEL)
    abr_shape, abr_spec = lay[abr_lay](D_MODEL)
    on_shape, on_spec = lay["a"](Q_W)
    o_shape, o_spec = lay[out_lay](D_MODEL)
    return pl.pallas_call(
        _post_body,
        grid=lay["grid"],
        in_specs=[x_spec, abr_spec, on_spec, _full(g1.shape), _full(wgab.shape), _full(wno.shape), _full(wo.shape),
                  _full(g2.shape)],
        out_specs=[o_spec, o_spec],
        out_shape=[jax.ShapeDtypeStruct(o_shape, F32), jax.ShapeDtypeStruct(o_shape, BF16)],
        compiler_params=_cparams(("arbitrary",) * len(lay["grid"]), VMEM_SMALL_MIB),
        name="post",
    )(x2d.reshape(x_shape), abr.reshape(abr_shape), onsa.reshape(on_shape), g1, wgab, wno, wo, g2)


def _route(logits):
    lane = lax.broadcasted_iota(jnp.int32, logits.shape, 1).astype(F32)
    big = float(LANES)
    glog = jnp.where(lane < N_EXPERT_GROUPS, logits, -jnp.inf)
    gmax = jnp.max(glog, axis=1, keepdims=True)
    gsel = jnp.min(jnp.where(glog == gmax, lane, big), axis=1, keepdims=True)
    gw = 1.0 / jnp.sum(jnp.exp(glog - gmax), axis=1, keepdims=True)
    lo = N_EXPERT_GROUPS + EXPERTS_PER_GROUP * gsel
    el = jnp.where((lane >= lo) & (lane < lo + EXPERTS_PER_GROUP), logits, -jnp.inf)
    v1 = jnp.max(el, axis=1, keepdims=True)
    i1 = jnp.min(jnp.where(el == v1, lane, big), axis=1, keepdims=True)
    el2 = jnp.where(lane == i1, -jnp.inf, el)
    v2 = jnp.max(el2, axis=1, keepdims=True)
    i2 = jnp.min(jnp.where(el2 == v2, lane, big), axis=1, keepdims=True)
    e2 = jnp.exp(v2 - v1)
    w1 = gw / (1.0 + e2)
    return jnp.where(lane == i1, w1, 0.0) + jnp.where(lane == i2, w1 * e2, 0.0)


def _moe_body(x1_ref, h2_ref, p_ref, wr_ref, br_ref, wg_ref, wu_ref, wd_ref, wpg_ref, wp_ref, gf_ref,
              y_ref, acc_ref, comb_ref, *, tsplit):
    g = pl.program_id(1)
    h2 = h2_ref[...]

    @pl.when(g == 0)
    def _():
        logits = jnp.dot(h2, wr_ref[...], preferred_element_type=F32) + br_ref[...]
        comb_ref[...] = _route(logits)
        acc_ref[...] = jnp.zeros_like(acc_ref)

    comb = comb_ref[...]
    lane = lax.broadcasted_iota(jnp.int32, comb.shape, 1)
    acc = acc_ref[...]
    for k in range(EXPERTS_PER_GROUP):
        e_lane = N_EXPERT_GROUPS + EXPERTS_PER_GROUP * g + k
        ce = jnp.sum(jnp.where(lane == e_lane, comb, 0.0), axis=1, keepdims=True)
        a = jnp.dot(h2, wg_ref[k], preferred_element_type=F32)
        b = jnp.dot(h2, wu_ref[k], preferred_element_type=F32)
        act = (jax.nn.silu(a) * b * ce).astype(BF16)
        acc = acc + jnp.dot(act, wd_ref[k], preferred_element_type=F32)
    acc_ref[...] = acc

    @pl.when(g == N_EXPERT_GROUPS - 1)
    def _():
        x2 = x1_ref[...] + acc_ref[...]
        rows = x2.shape[0] // tsplit
        if tsplit == 1:
            p = p_ref[...]
        else:
            p = jnp.concatenate([p_ref[:, PLE_DIM * t:PLE_DIM * (t + 1)] for t in range(tsplit)], axis=0)
        gate = jax.nn.sigmoid(jnp.dot(x2.astype(BF16), wpg_ref[...], preferred_element_type=F32))
        x3 = x2 + gate * jnp.dot(p.astype(BF16), wp_ref[...], preferred_element_type=F32)
        inv = lax.rsqrt(jnp.mean(x3 * x3, axis=-1, keepdims=True) + RMS_EPS)
        y = x3 * inv * gf_ref[...]
        if tsplit == 1:
            y_ref[...] = y
        else:
            for t in range(tsplit):
                y_ref[:, D_MODEL * t:D_MODEL * (t + 1)] = y[rows * t:rows * (t + 1)]


def _moe(x1, h2, p, mp, tm, tsplit):
    rows = x1.shape[0]
    nrb = rows // tm
    rb = lambda r, g: (r, 0)
    grp = lambda r, g: (g, 0, 0)
    if tsplit == 1:
        p_spec = pl.BlockSpec((tm, PLE_DIM), rb)
        y_spec = pl.BlockSpec((tm, D_MODEL), rb)
        y_shape = (rows, D_MODEL)
    else:
        assert nrb == 1
        p_spec = _full(p.shape)
        y_shape = (rows // tsplit, tsplit * D_MODEL)
        y_spec = _full(y_shape)
    return pl.pallas_call(
        functools.partial(_moe_body, tsplit=tsplit),
        grid=(nrb, N_EXPERT_GROUPS),
        in_specs=[pl.BlockSpec((tm, D_MODEL), rb), pl.BlockSpec((tm, D_MODEL), rb), p_spec,
                  _full(mp["wr"].shape), _full(mp["br"].shape),
                  pl.BlockSpec((EXPERTS_PER_GROUP, D_MODEL, D_FF_EXPERT), grp),
                  pl.BlockSpec((EXPERTS_PER_GROUP, D_MODEL, D_FF_EXPERT), grp),
                  pl.BlockSpec((EXPERTS_PER_GROUP, D_FF_EXPERT, D_MODEL), grp),
                  _full(mp["wpg"].shape), _full(mp["wp"].shape), _full(mp["gf"].shape)],
        out_specs=y_spec,
        out_shape=jax.ShapeDtypeStruct(y_shape, F32),
        scratch_shapes=[pltpu.VMEM((tm, D_MODEL), F32), pltpu.VMEM((tm, LANES), F32)],
        compiler_params=_cparams(("arbitrary", "arbitrary"), VMEM_MOE_MIB),
        name="moe_ple",
    )(x1, h2, p, mp["wr"], mp["br"], mp["wg"], mp["wu"], mp["wd"], mp["wpg"], mp["wp"], mp["gf"])


def _s5_sample_body(u_ref, h0re_ref, h0im_ref, wre_ref, wim_ref, ar_ref, ai_ref, cre_ref, cim_ref, d_ref,
                    wglu_ref, bglu_ref, wso_ref, abr_ref, hre_out_ref, him_out_ref,
                    bure_ref, buim_ref, hre_ref, him_ref, *, nseq, nstep):
    u = u_ref[...]
    ub = u.astype(BF16)
    for j in range(4):
        lhs = ub[:, LANES * j:LANES * (j + 1)]
        bure_ref[:, 512 * j:512 * (j + 1)] = jnp.dot(lhs, wre_ref[j], preferred_element_type=F32)
        buim_ref[:, 512 * j:512 * (j + 1)] = jnp.dot(lhs, wim_ref[j], preferred_element_type=F32)
    for lc in range(4):
        sl = slice(512 * lc, 512 * (lc + 1))
        ar = jnp.broadcast_to(ar_ref[:, sl], (SUBLANES, 512))
        ai = jnp.broadcast_to(ai_ref[:, sl], (SUBLANES, 512))

        def body(rc, carry, sl=sl, ar=ar, ai=ai):
            r0 = pl.multiple_of(rc * SUBLANES, SUBLANES)
            hr = h0re_ref[pl.ds(r0, SUBLANES), sl]
            hi = h0im_ref[pl.ds(r0, SUBLANES), sl]
            for t in range(nstep):
                rr = pl.multiple_of(t * nseq + rc * SUBLANES, SUBLANES)
                hr, hi = (ar * hr - ai * hi + bure_ref[pl.ds(rr, SUBLANES), sl],
                          ar * hi + ai * hr + buim_ref[pl.ds(rr, SUBLANES), sl])
                hre_ref[pl.ds(rr, SUBLANES), sl] = hr
                him_ref[pl.ds(rr, SUBLANES), sl] = hi
            hre_out_ref[pl.ds(r0, SUBLANES), sl] = hr
            him_out_ref[pl.ds(r0, SUBLANES), sl] = hi
            return carry

        lax.fori_loop(0, nseq // SUBLANES, body, 0)
    parts = []
    for j in range(4):
        sl = slice(512 * j, 512 * (j + 1))
        parts.append(jnp.dot(hre_ref[:, sl].astype(BF16), cre_ref[j], preferred_element_type=F32)
                     - jnp.dot(him_ref[:, sl].astype(BF16), cim_ref[j], preferred_element_type=F32))
    y = jnp.concatenate(parts, axis=1) + d_ref[...] * u
    zg = jax.nn.gelu(y)
    gate = jnp.dot(zg.astype(BF16), wglu_ref[...], preferred_element_type=F32) + bglu_ref[...]
    glu = (zg * jax.nn.sigmoid(gate)).astype(BF16)
    abr_ref[...] = jnp.dot(glu, wso_ref[...], preferred_element_type=F32).astype(BF16)


def _s5_sample(u_ts, h0re, h0im, sp, wglu, bglu, wso, nseq, nstep):
    rows = nseq * nstep
    ops = [u_ts, h0re, h0im, sp["wre"], sp["wim"], sp["ar"], sp["ai"], sp["cre"], sp["cim"], sp["d"], wglu, bglu, wso]
    return pl.pallas_call(
        functools.partial(_s5_sample_body, nseq=nseq, nstep=nstep),
        grid=(1,),
        in_specs=[_full(o.shape) for o in ops],
        out_specs=[_full((rows, D_MODEL)), _full((nseq, N_STATE)), _full((nseq, N_STATE))],
        out_shape=[jax.ShapeDtypeStruct((rows, D_MODEL), BF16),
                   jax.ShapeDtypeStruct((nseq, N_STATE), F32), jax.ShapeDtypeStruct((nseq, N_STATE), F32)],
        scratch_shapes=[pltpu.VMEM((rows, N_STATE), F32) for _ in range(4)],
        compiler_params=_cparams(("arbitrary",), VMEM_LARGE_MIB),
        name="s5_sample",
    )(*ops)


def _softmax_rows(s, valid):
    sm = jnp.where(valid, s, NEG_INF)
    mx = jnp.max(sm, axis=1, keepdims=True)
    e = jnp.where(valid, jnp.exp2(sm - mx), 0.0)
    l = jnp.sum(e, axis=1, keepdims=True)
    return e * (1.0 / jnp.maximum(l, 1e-30))


SAMPLE_SEQS_PER_STEP = 4
CMP_PITCH = 24


def _attn_sample_body(pt_ref, q_ref, gn_ref, nks_ref, nkw_ref, wint_ref, wk_ref, bk_ref, w2k_ref, ov_ref, e_ref,
                      cmp_hbm, slc_hbm, o_ref, nwint_ref, xrow_ref, pages_ref, sem_ref, *, npage, past_len, nsub, tq):
    n = pl.program_id(0)
    nsteps = pl.num_programs(0)
    slot = lax.rem(n, 2)
    nrow = N_Q_HEADS * tq
    nwin = wint_ref.shape[2]
    nslc = -(-(past_len + tq) // SLC_BLOCK)
    nch = past_len // CMP_STRIDE
    per_page = PAGE_SIZE // CMP_STRIDE

    def page_copy(step, into, c, s, p):
        src = (cmp_hbm, slc_hbm)[c]
        return pltpu.make_async_copy(src.at[pt_ref[step * nsub + s, p]],
                                     pages_ref.at[into, (c * nsub + s) * npage + p], sem_ref.at[into])

    def all_pages(step, into, op):
        for c in range(2):
            for s in range(nsub):
                for p in range(npage):
                    op(page_copy(step, into, c, s, p))

    @pl.when(n == 0)
    def _():
        all_pages(0, 0, lambda cp: cp.start())

    all_pages(n, slot, lambda cp: cp.wait())
    nxt = jnp.minimum(n + 1, nsteps - 1)
    all_pages(nxt, 1 - slot, lambda cp: cp.start())

    def page(c, s, p):
        return pages_ref.at[slot, (c * nsub + s) * npage + p]

    cmp = []
    for kv in range(2):
        for s in range(nsub):
            for p in range(npage):
                rows = page(0, s, p)[LANES * kv:LANES * (kv + 1), :].T
                for c in range(per_page):
                    r0 = CMP_PITCH * (per_page * p + c)
                    xrow_ref[s, r0:r0 + CMP_STRIDE, :] = rows[CMP_STRIDE * c:CMP_STRIDE * (c + 1)]
        x = jnp.concatenate(
            [jnp.concatenate([xrow_ref[s, pl.ds(i, nch, stride=CMP_PITCH), :] for i in range(CMP_STRIDE)], axis=1)
             for s in range(nsub)], axis=0)
        pp = jnp.dot(x.astype(BF16), wk_ref[kv], preferred_element_type=F32)
        pre = pp[:, 0:LANES] + pltpu.roll(pp[:, LANES:2 * LANES], nsub * nch - 1, axis=0) + bk_ref[kv:kv + 1, :]
        cmp.append(jnp.dot(jax.nn.gelu(pre).astype(BF16), w2k_ref[kv], preferred_element_type=F32).astype(BF16))
    cks = [cmp[0][nch * s:nch * (s + 1)] for s in range(nsub)]
    cvs = [cmp[1][nch * s:nch * (s + 1)] for s in range(nsub)]
    seqs = range(nsub)
    rcat = lambda parts: jnp.concatenate(parts, axis=0)

    lane_w = lax.broadcasted_iota(jnp.int32, (KV_W, LANES), 1)
    lane8 = lax.broadcasted_iota(jnp.int32, (tq, LANES), 1)
    nks_l, nkw_l, wint_l, qs_l = [], [], [], []
    for s in seqs:
        rows_s = slice(tq * s, tq * (s + 1))
        nks_l.append(jnp.concatenate([nks_ref[rows_s, :], jnp.zeros((LANES - tq, KV_W), F32)], axis=0))
        nkw = jnp.concatenate([nkw_ref[rows_s, :], jnp.zeros((LANES - tq, KV_W), F32)], axis=0)
        nkw_l.append(nkw)
        wint = wint_ref[s]
        wint_l.append(wint)
        shifted = pltpu.roll(wint, nwin - tq, axis=1)
        new_t = pltpu.roll(nkw.T, LANES - tq, axis=1)
        nwint_ref[s, :, 0:nwin - LANES] = shifted[:, 0:nwin - LANES]
        nwint_ref[s, :, nwin - LANES:nwin] = jnp.where(lane_w >= LANES - tq, new_t, shifted[:, nwin - LANES:nwin])
        q = q_ref[rows_s, :]
        qrows = []
        for j in range(N_Q_HEADS):
            chunk = q[:, LANES * (j // 2):LANES * (j // 2 + 1)]
            dst = j // GQA
            if (j % 2) != dst:
                chunk = pltpu.roll(chunk, HEAD_DIM, axis=1)
            keep = (lane8 < HEAD_DIM) if dst == 0 else (lane8 >= HEAD_DIM)
            qrows.append(jnp.where(keep, chunk, 0.0))
        qs_l.append(jnp.concatenate(qrows, axis=0).astype(BF16))

    rtot = nsub * nrow
    seq_rows = [slice(nrow * s, nrow * (s + 1)) for s in seqs]
    pos = past_len + (lax.broadcasted_iota(jnp.int32, (rtot, LANES), 0) & (tq - 1))
    lane = lax.broadcasted_iota(jnp.int32, (rtot, LANES), 1)

    sc = rcat([_dot_t(qs_l[s], cks[s]) for s in seqs])
    pc = _softmax_rows(sc, lane * CMP_STRIDE + (CMP_LEN - 1) <= pos).astype(BF16)
    oc = rcat([jnp.dot(pc[seq_rows[s]], cvs[s], preferred_element_type=F32) for s in seqs])
    imp = jnp.dot(pc, ov_ref[...], preferred_element_type=F32)
    vs = []
    for s in seqs:
        for h in range(N_KV_HEADS):
            r0 = nrow * s + tq * GQA * h
            v = imp[r0:r0 + tq]
            for g in range(1, GQA):
                v = v + imp[r0 + tq * g:r0 + tq * (g + 1)]
            vs.append(v)
    nsel = len(vs) * tq
    vt = rcat(vs + [jnp.zeros((LANES - nsel, LANES), F32)]).T
    nblk_pad = -(-nslc // SUBLANES) * SUBLANES
    blk_t = lax.broadcasted_iota(jnp.int32, (nblk_pad, LANES), 0)
    pos_t = past_len + (lax.broadcasted_iota(jnp.int32, (nblk_pad, LANES), 1) & (tq - 1))
    neg_t = _select_blocks(vt[0:nblk_pad], blk_t, pos_t, nslc, axis=0)
    neg = rcat([neg_t, jnp.zeros((LANES - nblk_pad, LANES), F32)]).T
    negsel = rcat([neg[tq * (N_KV_HEADS * s + j // GQA):tq * (N_KV_HEADS * s + j // GQA + 1)]
                   for s in seqs for j in range(N_Q_HEADS)])
    negsel_b = negsel.astype(BF16)

    new_blk = past_len // SLC_BLOCK
    ss_l = []
    for s in seqs:
        qaug = jnp.concatenate([qs_l[s], negsel_b[seq_rows[s]]], axis=1)
        parts = []
        for p in range(0, npage, 2):
            kt = jnp.concatenate([page(1, s, p)[0:LANES, :], page(1, s, p + 1)[0:LANES, :]], axis=1).astype(BF16)
            et = jnp.concatenate([e_ref[p], e_ref[p + 1]], axis=1)
            parts.append(jnp.dot(qaug, jnp.concatenate([kt, et], axis=0), preferred_element_type=F32))
        parts.append(_dot_t(qs_l[s], nks_l[s][:, 0:LANES].astype(BF16)) + negsel[seq_rows[s], new_blk:new_blk + 1])
        ss_l.append(jnp.concatenate(parts, axis=1))
    ss = rcat(ss_l)
    nkeys = ss.shape[1]
    kpos = lax.broadcasted_iota(jnp.int32, (rtot, nkeys), 1)
    pos_k = past_len + (lax.broadcasted_iota(jnp.int32, (rtot, nkeys), 0) & (tq - 1))
    ps = _softmax_rows(ss, kpos <= pos_k).astype(BF16)
    osel_l = []
    for s in seqs:
        psq = ps[seq_rows[s]]
        o = jnp.dot(psq[:, past_len:nkeys], nks_l[s][:, LANES:2 * LANES].astype(BF16), preferred_element_type=F32)
        for p in range(0, npage, 2):
            vtp = jnp.concatenate([page(1, s, p)[LANES:2 * LANES, :], page(1, s, p + 1)[LANES:2 * LANES, :]],
                                  axis=1).astype(BF16)
            o = o + _dot_t(psq[:, PAGE_SIZE * p:PAGE_SIZE * (p + 2)], vtp)
        osel_l.append(o)
    osel = rcat(osel_l)

    sw = rcat([jnp.concatenate([jnp.dot(qs_l[s], wint_l[s][0:LANES].astype(BF16), preferred_element_type=F32),
                                _dot_t(qs_l[s], nkw_l[s][:, 0:LANES].astype(BF16))], axis=1) for s in seqs])
    nw = sw.shape[1]
    widx = lax.broadcasted_iota(jnp.int32, (rtot, nw), 1)
    pos_w = past_len + (lax.broadcasted_iota(jnp.int32, (rtot, nw), 0) & (tq - 1))
    dlt = pos_w - (past_len - nwin + widx)
    pw = _softmax_rows(sw, (dlt >= 0) & (dlt < WINDOW) & (widx < nwin + tq)).astype(BF16)
    ow = rcat([_dot_t(pw[seq_rows[s], 0:nwin], wint_l[s][LANES:2 * LANES].astype(BF16))
               + jnp.dot(pw[seq_rows[s], nwin:nw], nkw_l[s][:, LANES:2 * LANES].astype(BF16),
                         preferred_element_type=F32) for s in seqs])

    for s in seqs:
        rows_s = slice(tq * s, tq * (s + 1))
        gn = gn_ref[rows_s, :]
        for c in range(N_Q_HEADS // 2):
            halves = []
            for hh in range(2):
                j = 2 * c + hh
                rs = slice(nrow * s + tq * j, nrow * s + tq * (j + 1))
                oj = (gn[:, 3 * j:3 * j + 1] * oc[rs] + gn[:, 3 * j + 1:3 * j + 2] * osel[rs]
                      + gn[:, 3 * j + 2:3 * j + 3] * ow[rs])
                if (j // GQA) != hh:
                    oj = pltpu.roll(oj, HEAD_DIM, axis=1)
                halves.append(oj)
            o_ref[rows_s, LANES * c:LANES * (c + 1)] = jnp.where(lane8 < HEAD_DIM, halves[0], halves[1])

    @pl.when(n == nsteps - 1)
    def _():
        all_pages(nxt, 1 - slot, lambda cp: cp.wait())


def _attn_sample(q, gn, nks, nkw, cache_cmp, cache_slc, cache_win, page_table, cp, nseq, tq, past_len):
    assert tq <= CMP_STRIDE and past_len % PAGE_SIZE == 0
    npage = past_len // PAGE_SIZE
    assert npage % 2 == 0 and PAGE_SIZE == LANES
    n_pool = cache_cmp.shape[0]
    nwin = cache_win.shape[1]
    chunks = past_len // CMP_STRIDE
    ov = _overlap_t(chunks, LANES).T
    key = np.arange(past_len).reshape(npage, 1, PAGE_SIZE)
    e = jnp.asarray(np.arange(LANES).reshape(1, LANES, 1) == key // SLC_BLOCK, dtype=BF16)
    to_t = lambda c: jnp.transpose(c, (0, 2, 3, 4, 1)).reshape(c.shape[0], KV_W, c.shape[1])
    cmp_t, slc_t, win_t = to_t(cache_cmp), to_t(cache_slc), to_t(cache_win)
    nsub = SAMPLE_SEQS_PER_STEP
    assert nseq % nsub == 0
    row = lambda n, pt: (n, 0)
    seq3 = lambda n, pt: (n, 0, 0)
    consts = [cp["wk"], cp["bk"], cp["w2k"], ov, e]
    in_specs = [pl.BlockSpec((nsub * tq, Q_W), row), pl.BlockSpec((nsub * tq, LANES), row),
                pl.BlockSpec((nsub * tq, KV_W), row), pl.BlockSpec((nsub * tq, KV_W), row),
                pl.BlockSpec((nsub, KV_W, nwin), seq3)]
    in_specs += [pl.BlockSpec(c.shape, (lambda nd: lambda n, pt: (0,) * nd)(c.ndim)) for c in consts]
    in_specs += [pl.BlockSpec(memory_space=pl.ANY)] * 2
    grid_spec = pltpu.PrefetchScalarGridSpec(
        num_scalar_prefetch=1,
        grid=(nseq // nsub,),
        in_specs=in_specs,
        out_specs=[pl.BlockSpec((nsub * tq, Q_W), row), pl.BlockSpec((nsub, KV_W, nwin), seq3)],
        scratch_shapes=[pltpu.VMEM((nsub, chunks * CMP_PITCH, LANES), F32),
                        pltpu.VMEM((2, 2 * nsub * npage, KV_W, PAGE_SIZE), F32),
                        pltpu.SemaphoreType.DMA((2,))],
    )
    return pl.pallas_call(
        functools.partial(_attn_sample_body, npage=npage, past_len=past_len, nsub=nsub, tq=tq),
        grid_spec=grid_spec,
        out_shape=[jax.ShapeDtypeStruct((nseq * tq, Q_W), F32), jax.ShapeDtypeStruct((nseq, KV_W, nwin), F32)],
        compiler_params=_cparams(("arbitrary",), VMEM_LARGE_MIB),
        name="attn_sample",
    )(page_table, q, gn, nks, nkw, win_t, *consts, cmp_t, slc_t)


def _moe_params(w_rg, b_rg, w_re, b_re, w_gate, w_up, w_down, w_ple, w_ple_gate, gf):
    pad = LANES - N_EXPERT_GROUPS - N_EXPERTS
    return {"wr": jnp.pad(jnp.concatenate([w_rg, w_re], axis=1), ((0, 0), (0, pad))).astype(BF16),
            "br": jnp.pad(jnp.concatenate([b_rg, b_re]), (0, pad)).astype(F32).reshape(1, LANES),
            "wg": w_gate.astype(BF16), "wu": w_up.astype(BF16), "wd": w_down.astype(BF16),
            "wpg": w_ple_gate.astype(BF16), "wp": w_ple.astype(BF16), "gf": gf.astype(F32).reshape(1, D_MODEL)}


TM_PROMPT = 512
TM_MOE = 1024
TC_S5 = 128


def kernel(x_prompt, x_sample, p_prompt, p_sample, cache_cmp_kv, cache_slc_kv, cache_win_kv, state_ssm, page_table, norm1_g, w_in, ssm_lam_re, ssm_lam_im, ssm_log_dt, ssm_b_re, ssm_b_im, ssm_c_re, ssm_c_im, ssm_d, w_glu, b_glu, cmp_pe, cmp_w1, cmp_w2, w_ssm_out, w_nsa_out, w_o, norm2_g, w_route_group, b_route_group, w_route_expert, b_route_expert, w_exp_gate, w_exp_up, w_exp_down, w_ple, w_ple_gate, final_norm_g):
    assert w_in.shape[0] == 1, "one layer"
    l = 0
    nb, t = x_prompt.shape[:2]
    ns, ts = x_sample.shape[:2]
    past_len = page_table.shape[1] * PAGE_SIZE
    kvt = (2, N_KV_HEADS, HEAD_DIM)

    wi = _inproj_params(w_in[l])
    g1 = norm1_g[l].astype(F32).reshape(1, D_MODEL)
    g2 = norm2_g[l].astype(F32).reshape(1, D_MODEL)
    sp = _s5_params(ssm_lam_re[l], ssm_lam_im[l], ssm_log_dt[l], ssm_b_re[l], ssm_b_im[l], ssm_c_re[l], ssm_c_im[l],
                    ssm_d[l])
    cp = _cmp_params(cmp_pe[l], cmp_w1[l], cmp_w2[l])
    mp = _moe_params(w_route_group[l], b_route_group[l], w_route_expert[l], b_route_expert[l], w_exp_gate[l],
                     w_exp_up[l], w_exp_down[l], w_ple[l], w_ple_gate[l], final_norm_g)
    wglu = w_glu[l].astype(BF16)
    bglu = b_glu[l].astype(F32).reshape(1, SSM_WIDTH)
    wso = w_ssm_out[l].astype(BF16)
    wno = w_nsa_out[l].astype(BF16)
    wo = w_o[l].astype(BF16)

    lay = _prompt_layout(nb, t, TM_PROMPT)
    xp = x_prompt.reshape(nb * t, D_MODEL)
    r = _inproj_prompt(xp, lay, g1, wi)
    abr, hlast = _s5_prompt(r["u"], sp, wglu, bglu, wso, t, TC_S5)
    ck, cvt = _compress_prompt(r["kvc"], cp, nb, t)
    onsa = _attn_prompt(r["qt"], r["gnt"], ck, cvt, r["ksb"], r["vst"], r["kwb"], r["vwt"], nb, t)
    x1, h2 = _post(xp, abr, "a", onsa, g1, wi["wgab"], wno, wo, g2, lay, "a")
    y_prompt = _moe(x1, h2, p_prompt[l].reshape(nb * t, PLE_DIM), mp, TM_MOE, 1).reshape(nb, t, D_MODEL)
    keep = min(WINDOW, t)

    def rows_last(a):
        return jnp.transpose(a.reshape((a.shape[0],) + kvt + (a.shape[2],)), (0, 4, 1, 2, 3))[None]

    new_cmp_p = rows_last(r["kvct"])
    new_slc_p = rows_last(r["kvst"])
    new_win_p = rows_last(r["kvwt"][:, :, t - keep:])
    new_ssm_p = jnp.stack([hlast[0:nb], hlast[nb:2 * nb]], axis=-1).reshape(1, nb, N_SSM_GROUPS, SSM_STATE, 2)

    lays = _sample_layout(ns, ts)
    xs = x_sample.reshape(ns * ts, D_MODEL)
    rs = _inproj_sample(xs, lays, g1, wi)
    h0 = state_ssm[l].astype(F32).reshape(ns, N_STATE, 2)
    abr_s, hre, him = _s5_sample(rs["u"], h0[..., 0], h0[..., 1], sp, wglu, bglu, wso, ns, ts)
    onsa_s, new_win = _attn_sample(rs["q"].reshape(ns * ts, Q_W), rs["gn"].reshape(ns * ts, LANES),
                                   rs["kvs"].reshape(ns * ts, KV_W), rs["kvw"].reshape(ns * ts, KV_W),
                                   cache_cmp_kv[l], cache_slc_kv[l], cache_win_kv[l], page_table, cp, ns, ts, past_len)
    x1s, h2s = _post(xs, abr_s, "b", onsa_s, g1, wi["wgab"], wno, wo, g2, lays, "b")
    y_sample = _moe(x1s, h2s, p_sample[l].reshape(ns, ts * PLE_DIM), mp, ns * ts, ts).reshape(ns, ts, D_MODEL)
    steps_first = lambda a: jnp.transpose(a.reshape((ts,) + kvt + (ns,)), (4, 0, 1, 2, 3))[None]
    new_cmp_s = steps_first(rs["kvct"])
    new_slc_s = steps_first(rs["kvst"])
    new_win_s = rows_last(new_win)
    new_ssm_s = jnp.stack([hre, him], axis=-1).reshape(1, ns, N_SSM_GROUPS, SSM_STATE, 2)
    return (y_prompt, y_sample, new_cmp_p, new_slc_p, new_win_p, new_ssm_p,
            new_cmp_s, new_slc_s, new_win_s, new_ssm_s)
```

```python
import functools
import math

import jax
import jax.numpy as jnp
import numpy as np
from jax import lax
from jax.experimental import pallas as pl
from jax.experimental.pallas import tpu as pltpu

F32 = jnp.float32
BF16 = jnp.bfloat16

D_MODEL = 1024
SSM_WIDTH = 512
SSM_GROUP = 16
N_SSM_GROUPS = 32
SSM_STATE = 64
HEAD_DIM = 64
N_Q_HEADS = 8
N_KV_HEADS = 2
GQA = 4
CMP_LEN = 32
CMP_STRIDE = 16
SLC_BLOCK = 64
TOP_N = 8
WINDOW = 512
Q_BLOCK = 256
NEG_INF = -1e30
FORCE_BONUS = 1e4
Q_W = 512
KV_W = 256
NSA_GATE_W = 24
N_EXPERT_GROUPS = 4
EXPERTS_PER_GROUP = 4
N_EXPERTS = 16
D_FF_EXPERT = 256
PLE_DIM = 256
RMS_EPS = 1e-6
PAGE_SIZE = 128

LANES = 128
SUBLANES = 8
N_STATE = N_SSM_GROUPS * SSM_STATE
MIB = 2 ** 20
V7X_VMEM_MIB = 64
VMEM_SMALL_MIB = 48
VMEM_LARGE_MIB = 56
VMEM_MOE_MIB = V7X_VMEM_MIB - 4


def _cparams(sem, vmem_mib):
    return pltpu.CompilerParams(dimension_semantics=sem, vmem_limit_bytes=vmem_mib * MIB)


def _full(shape):
    nd = len(shape)
    return pl.BlockSpec(shape, lambda *_: (0,) * nd)


def _prompt_layout(nseq, t, tm):
    nb = t // tm
    return {
        "grid": (nb, nseq), "tm": tm, "nseq": nseq, "t": t,
        "a": lambda w: ((nseq * t, w), pl.BlockSpec((tm, w), lambda b, s: (s * nb + b, 0))),
    }


def _sample_layout(nseq, t):
    return {
        "grid": (1, t), "tm": nseq,
        "a": lambda w: ((nseq, t * w), pl.BlockSpec((nseq, w), lambda s, b: (0, b))),
        "b": lambda w: ((t * nseq, w), pl.BlockSpec((nseq, w), lambda s, b: (b, 0))),
    }


TK_SLC = 512
TK_WIN = 128


Q_SCALE = HEAD_DIM ** -0.5 * math.log2(math.e)
C_U, C_Q, C_KVC, C_KVS, C_KVW = 0, 512, 1024, 1280, 1536
N_MAIN = 1792


def _dot_t(a, b):
    return lax.dot_general(a, b, (((1,), (1,)), ((), ())), preferred_element_type=F32)


GN_ROWS = 32


def _inproj_prompt_body(x_ref, g_ref, wa_ref, wgn_ref, wgb_ref,
                        u_ref, kvc_ref, ksb_ref, kwb_ref, gb_ref,
                        qt_ref, kvct_ref, kvst_ref, kvwt_ref, gnt_ref, vst_ref, vwt_ref, *, nseq):
    s = pl.program_id(1)
    x = x_ref[...]
    inv = lax.rsqrt(jnp.mean(x * x, axis=-1, keepdims=True) + RMS_EPS)
    h = (x * inv * g_ref[...]).astype(BF16)
    tm = h.shape[0]

    def mm(w):
        return jnp.dot(h, w, preferred_element_type=F32)

    u = mm(wa_ref[:, C_U:C_U + SSM_WIDTH])
    for j in range(SSM_WIDTH // LANES):
        u_ref[j, pl.ds(s, tm, stride=nseq), :] = u[:, LANES * j:LANES * (j + 1)]
    gb_ref[...] = jax.nn.sigmoid(mm(wgb_ref[...])).astype(BF16)
    kvc = mm(wa_ref[:, C_KVC:C_KVC + KV_W])
    kvc_ref[0] = kvc[:, 0:LANES]
    kvc_ref[1] = kvc[:, LANES:2 * LANES]
    kvct_ref[0] = kvc.T
    kvs = mm(wa_ref[:, C_KVS:C_KVS + KV_W])
    ksb_ref[...] = kvs[:, 0:LANES].astype(BF16)
    kvst = kvs.T
    kvst_ref[0] = kvst
    kvw = mm(wa_ref[:, C_KVW:C_KVW + KV_W])
    kwb_ref[...] = kvw[:, 0:LANES].astype(BF16)
    kvwt = kvw.T
    kvwt_ref[0] = kvwt
    for c in range(tm // TK_WIN):
        vst_ref[0, c] = kvst[LANES:2 * LANES, c * TK_WIN:(c + 1) * TK_WIN].astype(BF16)
        vwt_ref[0, c] = kvwt[LANES:2 * LANES, c * TK_WIN:(c + 1) * TK_WIN].astype(BF16)
    qt_ref[0] = (mm(wa_ref[:, C_Q:C_Q + Q_W]) * Q_SCALE).T.astype(BF16)
    gnt_ref[0] = jax.nn.sigmoid(mm(wgn_ref[...])).T[0:GN_ROWS]


def _inproj_prompt(x2d, lay, g, w):
    tm, nseq, t = lay["tm"], lay["nseq"], lay["t"]
    nb = t // tm
    out_shapes, out_specs, names = [], [], []

    def add(name, shape_spec, dt):
        names.append(name)
        out_shapes.append(jax.ShapeDtypeStruct(shape_spec[0], dt))
        out_specs.append(shape_spec[1])

    def tr(rows):
        return (nseq, rows, t), pl.BlockSpec((1, rows, tm), lambda b, s: (s, 0, b))

    nu = SSM_WIDTH // LANES
    add("u", ((nu, t * nseq, LANES), pl.BlockSpec((nu, tm * nseq, LANES), lambda b, s: (0, b, 0))), F32)
    add("kvc", ((2, nseq * t, LANES), pl.BlockSpec((2, tm, LANES), lambda b, s: (0, s * nb + b, 0))), F32)
    add("ksb", lay["a"](LANES), BF16)
    add("kwb", lay["a"](LANES), BF16)
    add("gb", lay["a"](D_MODEL), BF16)
    add("qt", tr(Q_W), BF16)
    add("kvct", tr(KV_W), F32)
    add("kvst", tr(KV_W), F32)
    add("kvwt", tr(KV_W), F32)
    add("gnt", tr(GN_ROWS), F32)
    for name in ("vst", "vwt"):
        add(name, ((nseq, t // TK_WIN, LANES, TK_WIN),
                   pl.BlockSpec((1, tm // TK_WIN, LANES, TK_WIN), lambda b, s: (s, b, 0, 0))), BF16)
    x_shape, x_spec = lay["a"](D_MODEL)
    ops = [g, w["wa"], w["wgn"], w["wgb"]]
    outs = pl.pallas_call(
        functools.partial(_inproj_prompt_body, nseq=nseq),
        grid=lay["grid"],
        in_specs=[x_spec] + [_full(o.shape) for o in ops],
        out_specs=out_specs,
        out_shape=out_shapes,
        compiler_params=_cparams(("arbitrary",) * 2, VMEM_LARGE_MIB),
        name="inproj_prompt",
    )(x2d.reshape(x_shape), *ops)
    return dict(zip(names, outs))


def _inproj_sample_body(x_ref, g_ref, wa_ref, wgn_ref, wgb_ref,
                        u_ref, q_ref, kvs_ref, kvw_ref, gn_ref, gb_ref, kvct_ref, kvst_ref, kvwt_ref):
    x = x_ref[...]
    inv = lax.rsqrt(jnp.mean(x * x, axis=-1, keepdims=True) + RMS_EPS)
    h = (x * inv * g_ref[...]).astype(BF16)

    def mm(w):
        return jnp.dot(h, w, preferred_element_type=F32)

    u_ref[...] = mm(wa_ref[:, C_U:C_U + SSM_WIDTH])
    q_ref[...] = mm(wa_ref[:, C_Q:C_Q + Q_W]) * Q_SCALE
    kvs = mm(wa_ref[:, C_KVS:C_KVS + KV_W])
    kvs_ref[...] = kvs
    kvw = mm(wa_ref[:, C_KVW:C_KVW + KV_W])
    kvw_ref[...] = kvw
    gn_ref[...] = jax.nn.sigmoid(mm(wgn_ref[...]))
    gb_ref[...] = jax.nn.sigmoid(mm(wgb_ref[...])).astype(BF16)
    kvct_ref[0] = mm(wa_ref[:, C_KVC:C_KVC + KV_W]).T
    kvst_ref[0] = kvs.T
    kvwt_ref[0] = kvw.T


def _inproj_sample(x2d, lay, g, w):
    nseq = lay["tm"]
    ts = lay["grid"][1]
    names = ["u", "q", "kvs", "kvw", "gn", "gb"]
    widths = [SSM_WIDTH, Q_W, KV_W, KV_W, LANES, D_MODEL]
    out_shapes, out_specs = [], []
    for n, wd in zip(names, widths):
        shp, spec = lay["b" if n == "u" else "a"](wd)
        out_shapes.append(jax.ShapeDtypeStruct(shp, BF16 if n == "gb" else F32))
        out_specs.append(spec)
    for n in ("kvct", "kvst", "kvwt"):
        names.append(n)
        out_shapes.append(jax.ShapeDtypeStruct((ts, KV_W, nseq), F32))
        out_specs.append(pl.BlockSpec((1, KV_W, nseq), lambda s, b: (b, 0, 0)))
    x_shape, x_spec = lay["a"](D_MODEL)
    ops = [g, w["wa"], w["wgn"], w["wgb"]]
    outs = pl.pallas_call(
        _inproj_sample_body,
        grid=lay["grid"],
        in_specs=[x_spec] + [_full(o.shape) for o in ops],
        out_specs=out_specs,
        out_shape=out_shapes,
        compiler_params=_cparams(("arbitrary",) * 2, VMEM_LARGE_MIB),
        name="inproj_sample",
    )(x2d.reshape(x_shape), *ops)
    return dict(zip(names, outs))


def _inproj_params(w_in0):
    return {"wa": w_in0[:, :N_MAIN].astype(BF16),
            "wgn": jnp.pad(w_in0[:, N_MAIN:N_MAIN + NSA_GATE_W], ((0, 0), (0, LANES - NSA_GATE_W))).astype(BF16),
            "wga": w_in0[:, N_MAIN + NSA_GATE_W:N_MAIN + NSA_GATE_W + D_MODEL].astype(BF16),
            "wgb": w_in0[:, N_MAIN + NSA_GATE_W + D_MODEL:].astype(BF16)}


S5_LANES = 1024


def _s5_prompt_body(u_ref, wb_ref, ar_ref, ai_ref, cw_ref, d_ref, wglu_ref, bglu_ref, wso_ref,
                    abr_ref, hlast_ref, lhs_ref, bu_ref, h8_ref, p_ref, hstate_ref):
    c = pl.program_id(0)
    nseq = 4
    r4 = u_ref.shape[1]
    tc = r4 // nseq
    half = tc // 2

    @pl.when(c == 0)
    def _():
        hstate_ref[...] = jnp.zeros_like(hstate_ref)

    u = jnp.concatenate([u_ref[j] for j in range(SSM_WIDTH // LANES)], axis=1)
    row2 = lax.broadcasted_iota(jnp.int32, (r4, SSM_WIDTH), 0)
    lo2 = (row2 % SUBLANES) < nseq
    up = pltpu.roll(u, r4 - nseq, axis=0)
    dn = pltpu.roll(u, nseq, axis=0)
    swapped = jnp.where(lo2, up, dn)
    zero = jnp.zeros_like(u)
    ev_re = jnp.where(lo2, u, zero).astype(BF16).reshape(half, SUBLANES, SSM_WIDTH)
    ev_im = jnp.where(lo2, zero, swapped).astype(BF16).reshape(half, SUBLANES, SSM_WIDTH)
    od_re = jnp.where(lo2, swapped, zero).astype(BF16).reshape(half, SUBLANES, SSM_WIDTH)
    od_im = jnp.where(lo2, zero, u).astype(BF16).reshape(half, SUBLANES, SSM_WIDTH)
    for j in range(4):
        sl = slice(LANES * j, LANES * (j + 1))
        lhs_ref[:, 0:8, 256 * j:256 * j + LANES] = ev_re[:, :, sl]
        lhs_ref[:, 0:8, 256 * j + LANES:256 * (j + 1)] = ev_im[:, :, sl]
        lhs_ref[:, 8:16, 256 * j:256 * j + LANES] = od_re[:, :, sl]
        lhs_ref[:, 8:16, 256 * j + LANES:256 * (j + 1)] = od_im[:, :, sl]
    for j in range(4):
        lhs = lhs_ref[:, :, 256 * j:256 * (j + 1)].reshape(tc * SUBLANES, 256)
        bu_ref[:, 512 * j:512 * (j + 1)] = jnp.dot(lhs, wb_ref[j], preferred_element_type=F32)

    for lc in range(N_STATE // S5_LANES):
        sl = slice(S5_LANES * lc, S5_LANES * (lc + 1))
        ar = ar_ref[:, sl]
        ai = ai_ref[:, sl]

        def step(t, h, sl=sl, ar=ar, ai=ai):
            r0 = pl.multiple_of(t * SUBLANES, SUBLANES)
            h = ar * h + ai * pltpu.roll(h, nseq, axis=0) + bu_ref[pl.ds(r0, SUBLANES), sl]
            h8_ref[pl.ds(r0, SUBLANES), sl] = h
            return h

        hstate_ref[:, sl] = lax.fori_loop(0, tc, step, hstate_ref[:, sl], unroll=8)
    hlast_ref[...] = hstate_ref[...]

    for j in range(4):
        pj = jnp.dot(h8_ref[:, 512 * j:512 * (j + 1)].astype(BF16), cw_ref[j], preferred_element_type=F32)
        p_ref[2 * j] = pj[:, 0:LANES]
        p_ref[2 * j + 1] = pj[:, LANES:2 * LANES]
    ys = []
    for s in range(nseq):
        parts = []
        for j in range(4):
            re = p_ref[2 * j, pl.ds(s, tc, stride=SUBLANES), :]
            im = p_ref[2 * j + 1, pl.ds(nseq + s, tc, stride=SUBLANES), :]
            us = u_ref[j, pl.ds(s, tc, stride=nseq), :]
            parts.append(re + im + d_ref[:, LANES * j:LANES * (j + 1)] * us)
        ys.append(jnp.concatenate(parts, axis=1))
    y = jnp.concatenate(ys, axis=0)
    zg = jax.nn.gelu(y)
    gate = jnp.dot(zg.astype(BF16), wglu_ref[...], preferred_element_type=F32) + bglu_ref[...]
    glu = (zg * jax.nn.sigmoid(gate)).astype(BF16)
    abr = jnp.dot(glu, wso_ref[...], preferred_element_type=F32)
    for s in range(nseq):
        abr_ref[s] = abr[s * tc:(s + 1) * tc].astype(BF16)


def _s5_prompt(u_ts, sp, wglu, bglu, wso, t_total, tc):
    nseq = 4
    grid = (t_total // tc,)
    abr, hlast = pl.pallas_call(
        _s5_prompt_body,
        grid=grid,
        in_specs=[pl.BlockSpec((SSM_WIDTH // LANES, tc * nseq, LANES), lambda c: (0, c, 0)),
                  _full(sp["wb8"].shape), _full(sp["ar8"].shape), _full(sp["ai8"].shape), _full(sp["cw8"].shape),
                  _full(sp["d"].shape), _full(wglu.shape), _full(bglu.shape), _full(wso.shape)],
        out_specs=[pl.BlockSpec((nseq, tc, D_MODEL), lambda c: (0, c, 0)),
                   pl.BlockSpec((SUBLANES, N_STATE), lambda c: (0, 0))],
        out_shape=[jax.ShapeDtypeStruct((nseq, t_total, D_MODEL), BF16),
                   jax.ShapeDtypeStruct((SUBLANES, N_STATE), F32)],
        scratch_shapes=[pltpu.VMEM((tc // 2, 2 * SUBLANES, 1024), BF16),
                        pltpu.VMEM((tc * SUBLANES, N_STATE), F32),
                        pltpu.VMEM((tc * SUBLANES, N_STATE), F32),
                        pltpu.VMEM((8, tc * SUBLANES, LANES), F32),
                        pltpu.VMEM((SUBLANES, N_STATE), F32)],
        compiler_params=_cparams(("arbitrary",), VMEM_LARGE_MIB),
        name="s5_prompt",
    )(u_ts, sp["wb8"], sp["ar8"], sp["ai8"], sp["cw8"], sp["d"], wglu, bglu, wso)
    return abr, hlast


def _s5_params(lam_re, lam_im, log_dt, b_re, b_im, c_re, c_im, d_skip):
    lam = lax.complex(lam_re.astype(F32), lam_im.astype(F32))
    dt = jnp.exp(log_dt.astype(F32))[:, None]
    a_bar = jnp.exp(lam * dt)
    b = lax.complex(b_re.astype(F32), b_im.astype(F32))
    b_bar = ((a_bar - 1.0) / lam)[..., None] * b
    eye8 = jnp.eye(8, dtype=F32)

    def bd_b(m):
        return jnp.einsum("ab,jbpc->jacbp", eye8, m.reshape(4, 8, SSM_STATE, SSM_GROUP)).reshape(4, 128, 512)

    def bd_c(m):
        return jnp.einsum("ab,jbcp->japbc", eye8, m.reshape(4, 8, SSM_GROUP, SSM_STATE)).reshape(4, 512, 128)

    wre, wim = bd_b(b_bar.real), bd_b(b_bar.imag)
    cre, cim = bd_c(c_re.astype(F32)), bd_c(c_im.astype(F32))
    ar = a_bar.real.reshape(1, N_STATE)
    ai = a_bar.imag.reshape(1, N_STATE)
    sign = jnp.concatenate([-jnp.ones((4, 1), F32), jnp.ones((4, 1), F32)], axis=0)
    return {
        "wb8": jnp.concatenate([wre, wim], axis=1).astype(BF16),
        "cw8": jnp.concatenate([cre, -cim], axis=2).astype(BF16),
        "ar8": jnp.broadcast_to(ar, (SUBLANES, N_STATE)),
        "ai8": sign * ai,
        "wre": wre.astype(BF16), "wim": wim.astype(BF16),
        "cre": cre.astype(BF16), "cim": cim.astype(BF16),
        "ar": ar, "ai": ai,
        "d": d_skip.astype(F32).reshape(1, SSM_WIDTH),
    }


def _cmp_params(cmp_pe, cmp_w1, cmp_w2):
    eye2 = jnp.eye(2, dtype=F32)
    nhalf = CMP_LEN // CMP_STRIDE
    w1r = cmp_w1.astype(F32).reshape(2, nhalf, CMP_STRIDE, HEAD_DIM, HEAD_DIM)
    wk = jnp.einsum("kside,ph->kipdshe", w1r, eye2).reshape(2, CMP_STRIDE * LANES, nhalf * LANES)
    bk = jnp.einsum("kld,klde->ke", cmp_pe.astype(F32), cmp_w1.astype(F32), precision=lax.Precision.HIGHEST)
    w2k = jnp.einsum("kef,ph->kpehf", cmp_w2.astype(F32), eye2).reshape(2, LANES, LANES)
    return {"wk": wk.astype(BF16), "bk": jnp.tile(bk, (1, N_KV_HEADS)), "w2k": w2k.astype(BF16),
            "w2kt": jnp.swapaxes(w2k, 1, 2).astype(BF16)}


def _compress_hidden(tap, nch, kv, wk_ref, bk_ref):
    x = jnp.concatenate([tap(i).astype(BF16) for i in range(CMP_STRIDE)], axis=1)
    pp = jnp.dot(x, wk_ref[kv], preferred_element_type=F32)
    pre = pp[:, 0:LANES] + pltpu.roll(pp[:, LANES:2 * LANES], nch - 1, axis=0) + bk_ref[kv:kv + 1, :]
    return jax.nn.gelu(pre).astype(BF16)


def _compress_prompt_body(x_ref, wk_ref, bk_ref, w2k_ref, w2kt_ref, ck_ref, cvt_ref):
    nch = x_ref.shape[1] // CMP_STRIDE
    hid = [_compress_hidden(lambda i, kv=kv: x_ref[kv, pl.ds(i, nch, stride=CMP_STRIDE), :], nch, kv, wk_ref, bk_ref)
           for kv in range(2)]
    ck_ref[0] = jnp.dot(hid[0], w2k_ref[0], preferred_element_type=F32).astype(BF16)
    cvt_ref[0] = _dot_t(w2kt_ref[1], hid[1]).astype(BF16)


def _compress_prompt(kvc2, cp, nseq, t):
    nch = t // CMP_STRIDE
    return pl.pallas_call(
        _compress_prompt_body,
        grid=(nseq,),
        in_specs=[pl.BlockSpec((2, t, LANES), lambda n: (0, n, 0)),
                  _full(cp["wk"].shape), _full(cp["bk"].shape), _full(cp["w2k"].shape), _full(cp["w2kt"].shape)],
        out_specs=[pl.BlockSpec((1, nch, LANES), lambda n: (n, 0, 0)),
                   pl.BlockSpec((1, LANES, nch), lambda n: (n, 0, 0))],
        out_shape=[jax.ShapeDtypeStruct((nseq, nch, LANES), BF16),
                   jax.ShapeDtypeStruct((nseq, LANES, nch), BF16)],
        compiler_params=_cparams(("arbitrary",), VMEM_SMALL_MIB),
        name="compress_prompt",
    )(kvc2, cp["wk"], cp["bk"], cp["w2k"], cp["w2kt"])


def _overlap_t(n_cmp_pad, n_slc_pad):
    j = np.arange(n_cmp_pad)[None, :]
    s = np.arange(n_slc_pad)[:, None]
    ov = (j * CMP_STRIDE <= s * SLC_BLOCK + SLC_BLOCK - 1) & (j * CMP_STRIDE + CMP_LEN - 1 >= s * SLC_BLOCK)
    return jnp.asarray(ov, dtype=BF16)


def _softmax_cols(s, valid):
    sm = jnp.where(valid, s, NEG_INF)
    mx = jnp.max(sm, axis=0, keepdims=True)
    e = jnp.where(valid, jnp.exp2(sm - mx), 0.0)
    l = jnp.sum(e, axis=0, keepdims=True)
    return e * (1.0 / jnp.maximum(l, 1e-30))


def _select_blocks(imp, blk, pos, nblk, axis=0):
    cur = pos // SLC_BLOCK
    forced = (blk == 0) | (blk == cur) | (blk == cur - 1)
    v = jnp.where(forced, imp + FORCE_BONUS, imp)
    v = jnp.where(blk * SLC_BLOCK <= pos, v, NEG_INF)
    v = jnp.where(blk < nblk, v, -3e38)
    blk_f = blk.astype(F32)
    neg = jnp.full(imp.shape, NEG_INF, F32)
    for _ in range(min(TOP_N, nblk)):
        mx = jnp.max(v, axis=axis, keepdims=True)
        first = jnp.min(jnp.where(v == mx, blk_f, float(imp.shape[axis])), axis=axis, keepdims=True)
        pick = blk_f == first
        neg = jnp.where(pick, 0.0, neg)
        v = jnp.where(pick, -3e38, v)
    return neg


CB = 2 * LANES


def _attn_prompt_body(q_ref, gn_ref, ck_ref, cvt_ref, ks_ref, vst_ref, kw_ref, vwt_ref, ovt_ref,
                      o_ref, kaug_ref, kwaug_ref, qaug_ref, acc_ref):
    i = pl.program_id(1)
    t = ks_ref.shape[1]
    nch = ck_ref.shape[1]
    nslc = t // SLC_BLOCK
    qb = Q_BLOCK
    ncol = N_Q_HEADS * qb
    ncb = ncol // CB
    q0 = i * qb
    one_row = 2 * LANES - HEAD_DIM
    hrows = [slice(HEAD_DIM * ((CB * cb // qb) // GQA), HEAD_DIM * ((CB * cb // qb) // GQA + 1)) for cb in range(ncb)]

    @pl.when(i == 0)
    def _():
        kaug_ref[:, 0:LANES] = ks_ref[0]
        blk = lax.broadcasted_iota(jnp.int32, (t, LANES), 0) // SLC_BLOCK
        col = lax.broadcasted_iota(jnp.int32, (t, LANES), 1)
        kaug_ref[:, LANES:2 * LANES] = jnp.where(blk == col, 1.0, 0.0).astype(BF16)
        padcol = lax.broadcasted_iota(jnp.int32, (WINDOW, 2 * LANES), 1)
        kwaug_ref[0:WINDOW, :] = jnp.where(padcol == one_row, NEG_INF, 0.0).astype(BF16)
        kwaug_ref[WINDOW:WINDOW + t, 0:LANES] = kw_ref[0]
        kwaug_ref[WINDOW:WINDOW + t, LANES:2 * LANES] = jnp.zeros((t, LANES), BF16)

    zeros64 = jnp.zeros((HEAD_DIM, qb), BF16)
    for j in range(N_Q_HEADS):
        dst = j // GQA
        qaug_ref[HEAD_DIM * dst:HEAD_DIM * (dst + 1), qb * j:qb * (j + 1)] = q_ref[0, HEAD_DIM * j:HEAD_DIM * (j + 1), :]
        qaug_ref[HEAD_DIM * (1 - dst):HEAD_DIM * (2 - dst), qb * j:qb * (j + 1)] = zeros64
    tail_row = lax.broadcasted_iota(jnp.int32, (HEAD_DIM, ncol), 0)
    qaug_ref[one_row:2 * LANES, :] = jnp.where(tail_row == 0, 1.0, 0.0).astype(BF16)

    pos_c = q0 + (lax.broadcasted_iota(jnp.int32, (nch, CB), 1) & (qb - 1))
    cvalid = lax.broadcasted_iota(jnp.int32, (nch, CB), 0) * CMP_STRIDE + (CMP_LEN - 1) <= pos_c
    scs = [jnp.dot(ck_ref[0], qaug_ref[0:LANES, CB * cb:CB * (cb + 1)], preferred_element_type=F32)
           for cb in range(ncb)]
    pcs = [_softmax_cols(sc, cvalid).astype(BF16) for sc in scs]
    oc = jnp.concatenate([jnp.dot(cvt_ref[0, hrows[cb], :], pc, preferred_element_type=F32)
                          for cb, pc in enumerate(pcs)], axis=1)
    imp = jnp.concatenate([jnp.dot(ovt_ref[...], pc, preferred_element_type=F32) for pc in pcs], axis=1)

    qaug_ref[LANES:one_row, :] = jnp.zeros((one_row - LANES, ncol), BF16)
    q_blocks = qb // TK_WIN
    npiece = (WINDOW + qb) // TK_WIN
    kws = [kwaug_ref[pl.ds(pl.multiple_of(q0 + w * TK_WIN, TK_WIN), TK_WIN), :] for w in range(npiece)]
    vwt = jnp.concatenate([vwt_ref[0, jnp.maximum(i * q_blocks + w - WINDOW // TK_WIN, 0)] for w in range(npiece)],
                          axis=1)
    wrow = lax.broadcasted_iota(jnp.int32, (TK_WIN, CB), 0)
    wcol = lax.broadcasted_iota(jnp.int32, (TK_WIN, CB), 1) & (qb - 1)
    wbias = []
    for w in range(npiece):
        lo, hi = w * TK_WIN - WINDOW, w * TK_WIN - WINDOW + TK_WIN - 1
        if hi <= 0 and qb - 1 - lo < WINDOW:
            wbias.append(None)
        else:
            dlt = wcol - wrow - lo
            wbias.append(jnp.where((dlt >= 0) & (dlt < WINDOW), 0.0, NEG_INF))
    kw_all = jnp.concatenate(kws, axis=0)
    sws = [jnp.dot(kw_all, qaug_ref[:, CB * cb:CB * (cb + 1)], preferred_element_type=F32) for cb in range(ncb)]
    es, rls = [], []
    for s in sws:
        s = jnp.concatenate([s[TK_WIN * w:TK_WIN * (w + 1)] if b is None else s[TK_WIN * w:TK_WIN * (w + 1)] + b
                             for w, b in enumerate(wbias)], axis=0)
        e = jnp.exp2(s - jnp.max(s, axis=0, keepdims=True))
        es.append(e.astype(BF16))
        rls.append(1.0 / jnp.sum(e, axis=0, keepdims=True))
    ow = jnp.concatenate([jnp.dot(vwt[hrows[cb]], e, preferred_element_type=F32) * rl
                          for cb, (e, rl) in enumerate(zip(es, rls))], axis=1)

    blk = lax.broadcasted_iota(jnp.int32, (nslc, N_KV_HEADS * qb), 0)
    pos_q = q0 + (lax.broadcasted_iota(jnp.int32, (nslc, N_KV_HEADS * qb), 1) & (qb - 1))
    vs = []
    for h in range(N_KV_HEADS):
        v = imp[0:nslc, qb * GQA * h:qb * GQA * h + qb]
        for g in range(1, GQA):
            v = v + imp[0:nslc, qb * (GQA * h + g):qb * (GQA * h + g + 1)]
        vs.append(v)
    neg = _select_blocks(jnp.concatenate(vs, axis=1), blk, pos_q, nslc).astype(BF16)
    for j in range(N_Q_HEADS):
        h = j // GQA
        qaug_ref[LANES:LANES + nslc, qb * j:qb * (j + 1)] = neg[:, qb * h:qb * (h + 1)]

    brow = lax.broadcasted_iota(jnp.int32, (qb, CB), 0)
    bcol = lax.broadcasted_iota(jnp.int32, (qb, CB), 1) & (qb - 1)
    tri_lo = jnp.where(brow <= bcol, 0.0, NEG_INF)

    acc_ref[...] = jnp.zeros_like(acc_ref)

    def sel_tile(k0, nk, vt, carry, bias):
        m, l = carry
        ka = kaug_ref[pl.ds(k0, nk), :]
        css = [slice(CB * cb, CB * (cb + 1)) for cb in range(ncb)]
        ss = [jnp.dot(ka, qaug_ref[:, cs], preferred_element_type=F32) for cs in css]
        ms, ls, ps, alphas = [], [], [], []
        for cs, s in zip(css, ss):
            if bias is not None:
                s = s + bias
            mn = jnp.maximum(m[:, cs], jnp.max(s, axis=0, keepdims=True))
            alpha = jnp.exp2(m[:, cs] - mn)
            p = jnp.exp2(s - mn)
            ms.append(mn)
            ls.append(alpha * l[:, cs] + jnp.sum(p, axis=0, keepdims=True))
            ps.append(p.astype(BF16))
            alphas.append(alpha)
        pvs = [jnp.dot(vt[hrows[cb]], p, preferred_element_type=F32) for cb, p in enumerate(ps)]
        for cs, alpha, pv in zip(css, alphas, pvs):
            acc_ref[:, cs] = alpha * acc_ref[:, cs] + pv
        return jnp.concatenate(ms, axis=1), jnp.concatenate(ls, axis=1)

    def vt_blocks(ref, b0, n):
        return jnp.concatenate([ref[0, b0 + j] for j in range(n)], axis=1) if n > 1 else ref[0, b0]

    big_blocks = TK_SLC // TK_WIN

    def big_tile(kt, carry):
        return sel_tile(pl.multiple_of(kt * TK_SLC, TK_SLC), TK_SLC, vt_blocks(vst_ref, kt * big_blocks, big_blocks),
                        carry, None)

    def small_tile(kb, carry):
        return sel_tile(pl.multiple_of(kb * qb, qb), qb, vt_blocks(vst_ref, kb * q_blocks, q_blocks), carry, None)

    carry = (jnp.full((1, ncol), NEG_INF, F32), jnp.zeros((1, ncol), F32))
    nbig = q0 // TK_SLC
    carry = lax.fori_loop(0, nbig // 2, lambda kp, c: big_tile(2 * kp + 1, big_tile(2 * kp, c)), carry)
    carry = lax.fori_loop(nbig - nbig % 2, nbig, big_tile, carry)
    carry = lax.fori_loop(nbig * (TK_SLC // qb), i, small_tile, carry)
    _, l = sel_tile(pl.multiple_of(q0, qb), qb, vt_blocks(vst_ref, i * q_blocks, q_blocks), carry, tri_lo)
    osel = acc_ref[...] * (1.0 / l)

    gt = gn_ref[0]
    for c in range(N_Q_HEADS // 2):
        rows = []
        for hh in range(2):
            j = 2 * c + hh
            cs = slice(qb * j, qb * (j + 1))
            rows.append(gt[3 * j:3 * j + 1, :] * oc[:, cs] + gt[3 * j + 1:3 * j + 2, :] * osel[:, cs]
                        + gt[3 * j + 2:3 * j + 3, :] * ow[:, cs])
        o_ref[:, LANES * c:LANES * (c + 1)] = jnp.concatenate(rows, axis=0).T.astype(o_ref.dtype)


def _attn_prompt(q, gn, ck, cvt, ksb, vst, kwb, vwt, nseq, t):
    nb = t // Q_BLOCK
    nch = t // CMP_STRIDE
    nslc = t // SLC_BLOCK
    ovt = _overlap_t(nch, max(nslc, SUBLANES))
    row = lambda n, i: (n * nb + i, 0)
    seq3 = lambda n, i: (n, 0, 0)
    seq4 = lambda n, i: (n, 0, 0, 0)
    col3 = lambda n, i: (n, 0, i)
    return pl.pallas_call(
        _attn_prompt_body,
        grid=(nseq, nb),
        in_specs=[pl.BlockSpec((1, Q_W, Q_BLOCK), col3), pl.BlockSpec((1, gn.shape[1], Q_BLOCK), col3),
                  pl.BlockSpec((1, nch, LANES), seq3), pl.BlockSpec((1, LANES, nch), seq3),
                  pl.BlockSpec((1, t, LANES), seq3), pl.BlockSpec((1, t // TK_WIN, LANES, TK_WIN), seq4),
                  pl.BlockSpec((1, t, LANES), seq3), pl.BlockSpec((1, t // TK_WIN, LANES, TK_WIN), seq4),
                  _full(ovt.shape)],
        out_specs=pl.BlockSpec((Q_BLOCK, Q_W), row),
        out_shape=jax.ShapeDtypeStruct((nseq * t, Q_W), BF16),
        scratch_shapes=[pltpu.VMEM((t, 2 * LANES), BF16),
                        pltpu.VMEM((WINDOW + t, 2 * LANES), BF16),
                        pltpu.VMEM((2 * LANES, N_Q_HEADS * Q_BLOCK), BF16),
                        pltpu.VMEM((HEAD_DIM, N_Q_HEADS * Q_BLOCK), F32)],
        compiler_params=_cparams(("arbitrary", "arbitrary"), VMEM_LARGE_MIB),
        name="attn_prompt",
    )(q, gn, ck, cvt, ksb.reshape(nseq, t, LANES), vst, kwb.reshape(nseq, t, LANES), vwt, ovt)


def _post_body(x_ref, abr_ref, on_ref, gb_ref, g1_ref, wga_ref, wno_ref, wo_ref, g2_ref, x1_ref, h2_ref):
    x = x_ref[...]
    inv1 = lax.rsqrt(jnp.mean(x * x, axis=-1, keepdims=True) + RMS_EPS)
    h = (x * inv1 * g1_ref[...]).astype(BF16)
    ga = jax.nn.sigmoid(jnp.dot(h, wga_ref[...], preferred_element_type=F32))
    bbr = jnp.dot(on_ref[...].astype(BF16), wno_ref[...], preferred_element_type=F32)
    merged = ga * abr_ref[...].astype(F32) + gb_ref[...].astype(F32) * bbr
    x1 = x + jnp.dot(merged.astype(BF16), wo_ref[...], preferred_element_type=F32)
    x1_ref[...] = x1
    inv = lax.rsqrt(jnp.mean(x1 * x1, axis=-1, keepdims=True) + RMS_EPS)
    h2_ref[...] = (x1 * inv * g2_ref[...]).astype(BF16)


def _post(x2d, abr, abr_lay, onsa, gb, g1, wga, wno, wo, g2, lay, out_lay):
    x_shape, x_spec = lay["a"](D_MODEL)
    abr_shape, abr_spec = lay[abr_lay](D_MODEL)
    on_shape, on_spec = lay["a"](Q_W)
    o_shape, o_spec = lay[out_lay](D_MODEL)
    return pl.pallas_call(
        _post_body,
        grid=lay["grid"],
        in_specs=[x_spec, abr_spec, on_spec, x_spec, _full(g1.shape), _full(wga.shape), _full(wno.shape),
                  _full(wo.shape), _full(g2.shape)],
        out_specs=[o_spec, o_spec],
        out_shape=[jax.ShapeDtypeStruct(o_shape, F32), jax.ShapeDtypeStruct(o_shape, BF16)],
        compiler_params=_cparams(("arbitrary",) * len(lay["grid"]), VMEM_SMALL_MIB),
        name="post",
    )(x2d.reshape(x_shape), abr.reshape(abr_shape), onsa.reshape(on_shape), gb, g1, wga, wno, wo, g2)


def _route(logits):
    lane = lax.broadcasted_iota(jnp.int32, logits.shape, 1).astype(F32)
    big = float(LANES)
    glog = jnp.where(lane < N_EXPERT_GROUPS, logits, -jnp.inf)
    gmax = jnp.max(glog, axis=1, keepdims=True)
    gsel = jnp.min(jnp.where(glog == gmax, lane, big), axis=1, keepdims=True)
    gw = 1.0 / jnp.sum(jnp.exp(glog - gmax), axis=1, keepdims=True)
    lo = N_EXPERT_GROUPS + EXPERTS_PER_GROUP * gsel
    el = jnp.where((lane >= lo) & (lane < lo + EXPERTS_PER_GROUP), logits, -jnp.inf)
    v1 = jnp.max(el, axis=1, keepdims=True)
    i1 = jnp.min(jnp.where(el == v1, lane, big), axis=1, keepdims=True)
    el2 = jnp.where(lane == i1, -jnp.inf, el)
    v2 = jnp.max(el2, axis=1, keepdims=True)
    i2 = jnp.min(jnp.where(el2 == v2, lane, big), axis=1, keepdims=True)
    e2 = jnp.exp(v2 - v1)
    w1 = gw / (1.0 + e2)
    return jnp.where(lane == i1, w1, 0.0) + jnp.where(lane == i2, w1 * e2, 0.0)


def _moe_body(x1_ref, h2_ref, p_ref, wr_ref, br_ref, wg_ref, wu_ref, wd_ref, wpg_ref, wp_ref, gf_ref,
              y_ref, acc_ref, comb_ref, *, tsplit):
    g = pl.program_id(1)
    h2 = h2_ref[...]

    @pl.when(g == 0)
    def _():
        logits = jnp.dot(h2, wr_ref[...], preferred_element_type=F32) + br_ref[...]
        comb_ref[...] = _route(logits)
        acc_ref[...] = jnp.zeros_like(acc_ref)

    comb = comb_ref[...]
    lane = lax.broadcasted_iota(jnp.int32, comb.shape, 1)
    acc = acc_ref[...]
    for k in range(EXPERTS_PER_GROUP):
        e_lane = N_EXPERT_GROUPS + EXPERTS_PER_GROUP * g + k
        ce = jnp.sum(jnp.where(lane == e_lane, comb, 0.0), axis=1, keepdims=True)
        a = jnp.dot(h2, wg_ref[k], preferred_element_type=F32)
        b = jnp.dot(h2, wu_ref[k], preferred_element_type=F32)
        act = (jax.nn.silu(a) * b * ce).astype(BF16)
        acc = acc + jnp.dot(act, wd_ref[k], preferred_element_type=F32)
    acc_ref[...] = acc

    @pl.when(g == N_EXPERT_GROUPS - 1)
    def _():
        x2 = x1_ref[...] + acc_ref[...]
        rows = x2.shape[0] // tsplit
        if tsplit == 1:
            p = p_ref[...]
        else:
            p = jnp.concatenate([p_ref[:, PLE_DIM * t:PLE_DIM * (t + 1)] for t in range(tsplit)], axis=0)
        gate = jax.nn.sigmoid(jnp.dot(x2.astype(BF16), wpg_ref[...], preferred_element_type=F32))
        x3 = x2 + gate * jnp.dot(p.astype(BF16), wp_ref[...], preferred_element_type=F32)
        inv = lax.rsqrt(jnp.mean(x3 * x3, axis=-1, keepdims=True) + RMS_EPS)
        y = x3 * inv * gf_ref[...]
        if tsplit == 1:
            y_ref[...] = y
        else:
            for t in range(tsplit):
                y_ref[:, D_MODEL * t:D_MODEL * (t + 1)] = y[rows * t:rows * (t + 1)]


def _moe(x1, h2, p, mp, tm, tsplit):
    rows = x1.shape[0]
    nrb = rows // tm
    rb = lambda r, g: (r, 0)
    grp = lambda r, g: (g, 0, 0)
    if tsplit == 1:
        p_spec = pl.BlockSpec((tm, PLE_DIM), rb)
        y_spec = pl.BlockSpec((tm, D_MODEL), rb)
        y_shape = (rows, D_MODEL)
    else:
        assert nrb == 1
        p_spec = _full(p.shape)
        y_shape = (rows // tsplit, tsplit * D_MODEL)
        y_spec = _full(y_shape)
    return pl.pallas_call(
        functools.partial(_moe_body, tsplit=tsplit),
        grid=(nrb, N_EXPERT_GROUPS),
        in_specs=[pl.BlockSpec((tm, D_MODEL), rb), pl.BlockSpec((tm, D_MODEL), rb), p_spec,
                  _full(mp["wr"].shape), _full(mp["br"].shape),
                  pl.BlockSpec((EXPERTS_PER_GROUP, D_MODEL, D_FF_EXPERT), grp),
                  pl.BlockSpec((EXPERTS_PER_GROUP, D_MODEL, D_FF_EXPERT), grp),
                  pl.BlockSpec((EXPERTS_PER_GROUP, D_FF_EXPERT, D_MODEL), grp),
                  _full(mp["wpg"].shape), _full(mp["wp"].shape), _full(mp["gf"].shape)],
        out_specs=y_spec,
        out_shape=jax.ShapeDtypeStruct(y_shape, F32),
        scratch_shapes=[pltpu.VMEM((tm, D_MODEL), F32), pltpu.VMEM((tm, LANES), F32)],
        compiler_params=_cparams(("arbitrary", "arbitrary"), VMEM_MOE_MIB),
        name="moe_ple",
    )(x1, h2, p, mp["wr"], mp["br"], mp["wg"], mp["wu"], mp["wd"], mp["wpg"], mp["wp"], mp["gf"])


def _s5_sample_body(u_ref, h0re_ref, h0im_ref, wre_ref, wim_ref, ar_ref, ai_ref, cre_ref, cim_ref, d_ref,
                    wglu_ref, bglu_ref, wso_ref, abr_ref, hre_out_ref, him_out_ref,
                    bure_ref, buim_ref, hre_ref, him_ref, *, nseq, nstep):
    u = u_ref[...]
    ub = u.astype(BF16)
    for j in range(4):
        lhs = ub[:, LANES * j:LANES * (j + 1)]
        bure_ref[:, 512 * j:512 * (j + 1)] = jnp.dot(lhs, wre_ref[j], preferred_element_type=F32)
        buim_ref[:, 512 * j:512 * (j + 1)] = jnp.dot(lhs, wim_ref[j], preferred_element_type=F32)
    for lc in range(4):
        sl = slice(512 * lc, 512 * (lc + 1))
        ar = jnp.broadcast_to(ar_ref[:, sl], (SUBLANES, 512))
        ai = jnp.broadcast_to(ai_ref[:, sl], (SUBLANES, 512))

        def body(rc, carry, sl=sl, ar=ar, ai=ai):
            r0 = pl.multiple_of(rc * SUBLANES, SUBLANES)
            hr = h0re_ref[pl.ds(r0, SUBLANES), sl]
            hi = h0im_ref[pl.ds(r0, SUBLANES), sl]
            for t in range(nstep):
                rr = pl.multiple_of(t * nseq + rc * SUBLANES, SUBLANES)
                hr, hi = (ar * hr - ai * hi + bure_ref[pl.ds(rr, SUBLANES), sl],
                          ar * hi + ai * hr + buim_ref[pl.ds(rr, SUBLANES), sl])
                hre_ref[pl.ds(rr, SUBLANES), sl] = hr
                him_ref[pl.ds(rr, SUBLANES), sl] = hi
            hre_out_ref[pl.ds(r0, SUBLANES), sl] = hr
            him_out_ref[pl.ds(r0, SUBLANES), sl] = hi
            return carry

        lax.fori_loop(0, nseq // SUBLANES, body, 0)
    parts = []
    for j in range(4):
        sl = slice(512 * j, 512 * (j + 1))
        parts.append(jnp.dot(hre_ref[:, sl].astype(BF16), cre_ref[j], preferred_element_type=F32)
                     - jnp.dot(him_ref[:, sl].astype(BF16), cim_ref[j], preferred_element_type=F32))
    y = jnp.concatenate(parts, axis=1) + d_ref[...] * u
    zg = jax.nn.gelu(y)
    gate = jnp.dot(zg.astype(BF16), wglu_ref[...], preferred_element_type=F32) + bglu_ref[...]
    glu = (zg * jax.nn.sigmoid(gate)).astype(BF16)
    abr_ref[...] = jnp.dot(glu, wso_ref[...], preferred_element_type=F32).astype(BF16)


def _s5_sample(u_ts, h0re, h0im, sp, wglu, bglu, wso, nseq, nstep):
    rows = nseq * nstep
    ops = [u_ts, h0re, h0im, sp["wre"], sp["wim"], sp["ar"], sp["ai"], sp["cre"], sp["cim"], sp["d"], wglu, bglu, wso]
    return pl.pallas_call(
        functools.partial(_s5_sample_body, nseq=nseq, nstep=nstep),
        grid=(1,),
        in_specs=[_full(o.shape) for o in ops],
        out_specs=[_full((rows, D_MODEL)), _full((nseq, N_STATE)), _full((nseq, N_STATE))],
        out_shape=[jax.ShapeDtypeStruct((rows, D_MODEL), BF16),
                   jax.ShapeDtypeStruct((nseq, N_STATE), F32), jax.ShapeDtypeStruct((nseq, N_STATE), F32)],
        scratch_shapes=[pltpu.VMEM((rows, N_STATE), F32) for _ in range(4)],
        compiler_params=_cparams(("arbitrary",), VMEM_LARGE_MIB),
        name="s5_sample",
    )(*ops)


def _softmax_rows(s, valid):
    sm = jnp.where(valid, s, NEG_INF)
    mx = jnp.max(sm, axis=1, keepdims=True)
    e = jnp.where(valid, jnp.exp2(sm - mx), 0.0)
    l = jnp.sum(e, axis=1, keepdims=True)
    return e * (1.0 / jnp.maximum(l, 1e-30))


SAMPLE_SEQS_PER_STEP = 4
CMP_PITCH = 24


def _attn_sample_body(pt_ref, q_ref, gn_ref, nks_ref, nkw_ref, wint_ref, wk_ref, bk_ref, w2k_ref, ov_ref, e_ref,
                      cmp_hbm, slc_hbm, o_ref, nwint_ref, xrow_ref, pages_ref, sem_ref, *, npage, past_len, nsub, tq):
    n = pl.program_id(0)
    nsteps = pl.num_programs(0)
    slot = lax.rem(n, 2)
    nrow = N_Q_HEADS * tq
    nwin = wint_ref.shape[2]
    nslc = -(-(past_len + tq) // SLC_BLOCK)
    nch = past_len // CMP_STRIDE
    per_page = PAGE_SIZE // CMP_STRIDE

    def page_copy(step, into, c, s, p):
        src = (cmp_hbm, slc_hbm)[c]
        return pltpu.make_async_copy(src.at[pt_ref[step * nsub + s, p]],
                                     pages_ref.at[into, (c * nsub + s) * npage + p], sem_ref.at[into])

    def all_pages(step, into, op):
        for c in range(2):
            for s in range(nsub):
                for p in range(npage):
                    op(page_copy(step, into, c, s, p))

    @pl.when(n == 0)
    def _():
        all_pages(0, 0, lambda cp: cp.start())

    all_pages(n, slot, lambda cp: cp.wait())
    nxt = jnp.minimum(n + 1, nsteps - 1)
    all_pages(nxt, 1 - slot, lambda cp: cp.start())

    def page(c, s, p):
        return pages_ref.at[slot, (c * nsub + s) * npage + p]

    cmp = []
    for kv in range(2):
        for s in range(nsub):
            for p in range(npage):
                rows = page(0, s, p)[LANES * kv:LANES * (kv + 1), :].T
                for c in range(per_page):
                    r0 = CMP_PITCH * (per_page * p + c)
                    xrow_ref[s, r0:r0 + CMP_STRIDE, :] = rows[CMP_STRIDE * c:CMP_STRIDE * (c + 1)]
        x = jnp.concatenate(
            [jnp.concatenate([xrow_ref[s, pl.ds(i, nch, stride=CMP_PITCH), :] for i in range(CMP_STRIDE)], axis=1)
             for s in range(nsub)], axis=0)
        pp = jnp.dot(x.astype(BF16), wk_ref[kv], preferred_element_type=F32)
        pre = pp[:, 0:LANES] + pltpu.roll(pp[:, LANES:2 * LANES], nsub * nch - 1, axis=0) + bk_ref[kv:kv + 1, :]
        cmp.append(jnp.dot(jax.nn.gelu(pre).astype(BF16), w2k_ref[kv], preferred_element_type=F32).astype(BF16))
    cks = [cmp[0][nch * s:nch * (s + 1)] for s in range(nsub)]
    cvs = [cmp[1][nch * s:nch * (s + 1)] for s in range(nsub)]
    seqs = range(nsub)
    rcat = lambda parts: jnp.concatenate(parts, axis=0)

    lane_w = lax.broadcasted_iota(jnp.int32, (KV_W, LANES), 1)
    lane8 = lax.broadcasted_iota(jnp.int32, (tq, LANES), 1)
    nks_l, nkw_l, wint_l, qs_l = [], [], [], []
    for s in seqs:
        rows_s = slice(tq * s, tq * (s + 1))
        nks_l.append(jnp.concatenate([nks_ref[rows_s, :], jnp.zeros((LANES - tq, KV_W), F32)], axis=0))
        nkw = jnp.concatenate([nkw_ref[rows_s, :], jnp.zeros((LANES - tq, KV_W), F32)], axis=0)
        nkw_l.append(nkw)
        wint = wint_ref[s]
        wint_l.append(wint)
        shifted = pltpu.roll(wint, nwin - tq, axis=1)
        new_t = pltpu.roll(nkw.T, LANES - tq, axis=1)
        nwint_ref[s, :, 0:nwin - LANES] = shifted[:, 0:nwin - LANES]
        nwint_ref[s, :, nwin - LANES:nwin] = jnp.where(lane_w >= LANES - tq, new_t, shifted[:, nwin - LANES:nwin])
        q = q_ref[rows_s, :]
        qrows = []
        for j in range(N_Q_HEADS):
            chunk = q[:, LANES * (j // 2):LANES * (j // 2 + 1)]
            dst = j // GQA
            if (j % 2) != dst:
                chunk = pltpu.roll(chunk, HEAD_DIM, axis=1)
            keep = (lane8 < HEAD_DIM) if dst == 0 else (lane8 >= HEAD_DIM)
            qrows.append(jnp.where(keep, chunk, 0.0))
        qs_l.append(jnp.concatenate(qrows, axis=0).astype(BF16))

    rtot = nsub * nrow
    seq_rows = [slice(nrow * s, nrow * (s + 1)) for s in seqs]
    pos = past_len + (lax.broadcasted_iota(jnp.int32, (rtot, LANES), 0) & (tq - 1))
    lane = lax.broadcasted_iota(jnp.int32, (rtot, LANES), 1)

    sc = rcat([_dot_t(qs_l[s], cks[s]) for s in seqs])
    pc = _softmax_rows(sc, lane * CMP_STRIDE + (CMP_LEN - 1) <= pos).astype(BF16)
    oc = rcat([jnp.dot(pc[seq_rows[s]], cvs[s], preferred_element_type=F32) for s in seqs])
    imp = jnp.dot(pc, ov_ref[...], preferred_element_type=F32)
    vs = []
    for s in seqs:
        for h in range(N_KV_HEADS):
            r0 = nrow * s + tq * GQA * h
            v = imp[r0:r0 + tq]
            for g in range(1, GQA):
                v = v + imp[r0 + tq * g:r0 + tq * (g + 1)]
            vs.append(v)
    nsel = len(vs) * tq
    vt = rcat(vs + [jnp.zeros((LANES - nsel, LANES), F32)]).T
    nblk_pad = -(-nslc // SUBLANES) * SUBLANES
    blk_t = lax.broadcasted_iota(jnp.int32, (nblk_pad, LANES), 0)
    pos_t = past_len + (lax.broadcasted_iota(jnp.int32, (nblk_pad, LANES), 1) & (tq - 1))
    neg_t = _select_blocks(vt[0:nblk_pad], blk_t, pos_t, nslc, axis=0)
    neg = rcat([neg_t, jnp.zeros((LANES - nblk_pad, LANES), F32)]).T
    negsel = rcat([neg[tq * (N_KV_HEADS * s + j // GQA):tq * (N_KV_HEADS * s + j // GQA + 1)]
                   for s in seqs for j in range(N_Q_HEADS)])
    negsel_b = negsel.astype(BF16)

    new_blk = past_len // SLC_BLOCK
    ss_l = []
    for s in seqs:
        qaug = jnp.concatenate([qs_l[s], negsel_b[seq_rows[s]]], axis=1)
        parts = []
        for p in range(0, npage, 2):
            kt = jnp.concatenate([page(1, s, p)[0:LANES, :], page(1, s, p + 1)[0:LANES, :]], axis=1).astype(BF16)
            et = jnp.concatenate([e_ref[p], e_ref[p + 1]], axis=1)
            parts.append(jnp.dot(qaug, jnp.concatenate([kt, et], axis=0), preferred_element_type=F32))
        parts.append(_dot_t(qs_l[s], nks_l[s][:, 0:LANES].astype(BF16)) + negsel[seq_rows[s], new_blk:new_blk + 1])
        ss_l.append(jnp.concatenate(parts, axis=1))
    ss = rcat(ss_l)
    nkeys = ss.shape[1]
    kpos = lax.broadcasted_iota(jnp.int32, (rtot, nkeys), 1)
    pos_k = past_len + (lax.broadcasted_iota(jnp.int32, (rtot, nkeys), 0) & (tq - 1))
    ps = _softmax_rows(ss, kpos <= pos_k).astype(BF16)
    osel_l = []
    for s in seqs:
        psq = ps[seq_rows[s]]
        o = jnp.dot(psq[:, past_len:nkeys], nks_l[s][:, LANES:2 * LANES].astype(BF16), preferred_element_type=F32)
        for p in range(0, npage, 2):
            vtp = jnp.concatenate([page(1, s, p)[LANES:2 * LANES, :], page(1, s, p + 1)[LANES:2 * LANES, :]],
                                  axis=1).astype(BF16)
            o = o + _dot_t(psq[:, PAGE_SIZE * p:PAGE_SIZE * (p + 2)], vtp)
        osel_l.append(o)
    osel = rcat(osel_l)

    sw = rcat([jnp.concatenate([jnp.dot(qs_l[s], wint_l[s][0:LANES].astype(BF16), preferred_element_type=F32),
                                _dot_t(qs_l[s], nkw_l[s][:, 0:LANES].astype(BF16))], axis=1) for s in seqs])
    nw = sw.shape[1]
    widx = lax.broadcasted_iota(jnp.int32, (rtot, nw), 1)
    pos_w = past_len + (lax.broadcasted_iota(jnp.int32, (rtot, nw), 0) & (tq - 1))
    dlt = pos_w - (past_len - nwin + widx)
    pw = _softmax_rows(sw, (dlt >= 0) & (dlt < WINDOW) & (widx < nwin + tq)).astype(BF16)
    ow = rcat([_dot_t(pw[seq_rows[s], 0:nwin], wint_l[s][LANES:2 * LANES].astype(BF16))
               + jnp.dot(pw[seq_rows[s], nwin:nw], nkw_l[s][:, LANES:2 * LANES].astype(BF16),
                         preferred_element_type=F32) for s in seqs])

    for s in seqs:
        rows_s = slice(tq * s, tq * (s + 1))
        gn = gn_ref[rows_s, :]
        for c in range(N_Q_HEADS // 2):
            halves = []
            for hh in range(2):
                j = 2 * c + hh
                rs = slice(nrow * s + tq * j, nrow * s + tq * (j + 1))
                oj = (gn[:, 3 * j:3 * j + 1] * oc[rs] + gn[:, 3 * j + 1:3 * j + 2] * osel[rs]
                      + gn[:, 3 * j + 2:3 * j + 3] * ow[rs])
                if (j // GQA) != hh:
                    oj = pltpu.roll(oj, HEAD_DIM, axis=1)
                halves.append(oj)
            o_ref[rows_s, LANES * c:LANES * (c + 1)] = jnp.where(lane8 < HEAD_DIM, halves[0], halves[1])

    @pl.when(n == nsteps - 1)
    def _():
        all_pages(nxt, 1 - slot, lambda cp: cp.wait())


def _attn_sample(q, gn, nks, nkw, cache_cmp, cache_slc, cache_win, page_table, cp, nseq, tq, past_len):
    assert tq <= CMP_STRIDE and past_len % PAGE_SIZE == 0
    npage = past_len // PAGE_SIZE
    assert npage % 2 == 0 and PAGE_SIZE == LANES
    n_pool = cache_cmp.shape[0]
    nwin = cache_win.shape[1]
    chunks = past_len // CMP_STRIDE
    ov = _overlap_t(chunks, LANES).T
    key = np.arange(past_len).reshape(npage, 1, PAGE_SIZE)
    e = jnp.asarray(np.arange(LANES).reshape(1, LANES, 1) == key // SLC_BLOCK, dtype=BF16)
    to_t = lambda c: jnp.transpose(c, (0, 2, 3, 4, 1)).reshape(c.shape[0], KV_W, c.shape[1])
    cmp_t, slc_t, win_t = to_t(cache_cmp), to_t(cache_slc), to_t(cache_win)
    nsub = SAMPLE_SEQS_PER_STEP
    assert nseq % nsub == 0
    row = lambda n, pt: (n, 0)
    seq3 = lambda n, pt: (n, 0, 0)
    consts = [cp["wk"], cp["bk"], cp["w2k"], ov, e]
    in_specs = [pl.BlockSpec((nsub * tq, Q_W), row), pl.BlockSpec((nsub * tq, LANES), row),
                pl.BlockSpec((nsub * tq, KV_W), row), pl.BlockSpec((nsub * tq, KV_W), row),
                pl.BlockSpec((nsub, KV_W, nwin), seq3)]
    in_specs += [pl.BlockSpec(c.shape, (lambda nd: lambda n, pt: (0,) * nd)(c.ndim)) for c in consts]
    in_specs += [pl.BlockSpec(memory_space=pl.ANY)] * 2
    grid_spec = pltpu.PrefetchScalarGridSpec(
        num_scalar_prefetch=1,
        grid=(nseq // nsub,),
        in_specs=in_specs,
        out_specs=[pl.BlockSpec((nsub * tq, Q_W), row), pl.BlockSpec((nsub, KV_W, nwin), seq3)],
        scratch_shapes=[pltpu.VMEM((nsub, chunks * CMP_PITCH, LANES), F32),
                        pltpu.VMEM((2, 2 * nsub * npage, KV_W, PAGE_SIZE), F32),
                        pltpu.SemaphoreType.DMA((2,))],
    )
    return pl.pallas_call(
        functools.partial(_attn_sample_body, npage=npage, past_len=past_len, nsub=nsub, tq=tq),
        grid_spec=grid_spec,
        out_shape=[jax.ShapeDtypeStruct((nseq * tq, Q_W), F32), jax.ShapeDtypeStruct((nseq, KV_W, nwin), F32)],
        compiler_params=_cparams(("arbitrary",), VMEM_LARGE_MIB),
        name="attn_sample",
    )(page_table, q, gn, nks, nkw, win_t, *consts, cmp_t, slc_t)


def _moe_params(w_rg, b_rg, w_re, b_re, w_gate, w_up, w_down, w_ple, w_ple_gate, gf):
    pad = LANES - N_EXPERT_GROUPS - N_EXPERTS
    return {"wr": jnp.pad(jnp.concatenate([w_rg, w_re], axis=1), ((0, 0), (0, pad))).astype(BF16),
            "br": jnp.pad(jnp.concatenate([b_rg, b_re]), (0, pad)).astype(F32).reshape(1, LANES),
            "wg": w_gate.astype(BF16), "wu": w_up.astype(BF16), "wd": w_down.astype(BF16),
            "wpg": w_ple_gate.astype(BF16), "wp": w_ple.astype(BF16), "gf": gf.astype(F32).reshape(1, D_MODEL)}


TM_PROMPT = 512
TM_MOE = 1024
TC_S5 = 128


def kernel(x_prompt, x_sample, p_prompt, p_sample, cache_cmp_kv, cache_slc_kv, cache_win_kv, state_ssm, page_table, norm1_g, w_in, ssm_lam_re, ssm_lam_im, ssm_log_dt, ssm_b_re, ssm_b_im, ssm_c_re, ssm_c_im, ssm_d, w_glu, b_glu, cmp_pe, cmp_w1, cmp_w2, w_ssm_out, w_nsa_out, w_o, norm2_g, w_route_group, b_route_group, w_route_expert, b_route_expert, w_exp_gate, w_exp_up, w_exp_down, w_ple, w_ple_gate, final_norm_g):
    assert w_in.shape[0] == 1, "one layer"
    l = 0
    nb, t = x_prompt.shape[:2]
    ns, ts = x_sample.shape[:2]
    past_len = page_table.shape[1] * PAGE_SIZE
    kvt = (2, N_KV_HEADS, HEAD_DIM)

    wi = _inproj_params(w_in[l])
    g1 = norm1_g[l].astype(F32).reshape(1, D_MODEL)
    g2 = norm2_g[l].astype(F32).reshape(1, D_MODEL)
    sp = _s5_params(ssm_lam_re[l], ssm_lam_im[l], ssm_log_dt[l], ssm_b_re[l], ssm_b_im[l], ssm_c_re[l], ssm_c_im[l],
                    ssm_d[l])
    cp = _cmp_params(cmp_pe[l], cmp_w1[l], cmp_w2[l])
    mp = _moe_params(w_route_group[l], b_route_group[l], w_route_expert[l], b_route_expert[l], w_exp_gate[l],
                     w_exp_up[l], w_exp_down[l], w_ple[l], w_ple_gate[l], final_norm_g)
    wglu = w_glu[l].astype(BF16)
    bglu = b_glu[l].astype(F32).reshape(1, SSM_WIDTH)
    wso = w_ssm_out[l].astype(BF16)
    wno = w_nsa_out[l].astype(BF16)
    wo = w_o[l].astype(BF16)

    lay = _prompt_layout(nb, t, TM_PROMPT)
    xp = x_prompt.reshape(nb * t, D_MODEL)
    r = _inproj_prompt(xp, lay, g1, wi)
    abr, hlast = _s5_prompt(r["u"], sp, wglu, bglu, wso, t, TC_S5)
    ck, cvt = _compress_prompt(r["kvc"], cp, nb, t)
    onsa = _attn_prompt(r["qt"], r["gnt"], ck, cvt, r["ksb"], r["vst"], r["kwb"], r["vwt"], nb, t)
    x1, h2 = _post(xp, abr, "a", onsa, r["gb"], g1, wi["wga"], wno, wo, g2, lay, "a")
    y_prompt = _moe(x1, h2, p_prompt[l].reshape(nb * t, PLE_DIM), mp, TM_MOE, 1).reshape(nb, t, D_MODEL)
    keep = min(WINDOW, t)

    def rows_last(a):
        return jnp.transpose(a.reshape((a.shape[0],) + kvt + (a.shape[2],)), (0, 4, 1, 2, 3))[None]

    new_cmp_p = rows_last(r["kvct"])
    new_slc_p = rows_last(r["kvst"])
    new_win_p = rows_last(r["kvwt"][:, :, t - keep:])
    new_ssm_p = jnp.stack([hlast[0:nb], hlast[nb:2 * nb]], axis=-1).reshape(1, nb, N_SSM_GROUPS, SSM_STATE, 2)

    lays = _sample_layout(ns, ts)
    xs = x_sample.reshape(ns * ts, D_MODEL)
    rs = _inproj_sample(xs, lays, g1, wi)
    h0 = state_ssm[l].astype(F32).reshape(ns, N_STATE, 2)
    abr_s, hre, him = _s5_sample(rs["u"], h0[..., 0], h0[..., 1], sp, wglu, bglu, wso, ns, ts)
    onsa_s, new_win = _attn_sample(rs["q"].reshape(ns * ts, Q_W), rs["gn"].reshape(ns * ts, LANES),
                                   rs["kvs"].reshape(ns * ts, KV_W), rs["kvw"].reshape(ns * ts, KV_W),
                                   cache_cmp_kv[l], cache_slc_kv[l], cache_win_kv[l], page_table, cp, ns, ts, past_len)
    x1s, h2s = _post(xs, abr_s, "b", onsa_s, rs["gb"], g1, wi["wga"], wno, wo, g2, lays, "b")
    y_sample = _moe(x1s, h2s, p_sample[l].reshape(ns, ts * PLE_DIM), mp, ns * ts, ts).reshape(ns, ts, D_MODEL)
    steps_first = lambda a: jnp.transpose(a.reshape((ts,) + kvt + (ns,)), (4, 0, 1, 2, 3))[None]
    new_cmp_s = steps_first(rs["kvct"])
    new_slc_s = steps_first(rs["kvst"])
    new_win_s = rows_last(new_win)
    new_ssm_s = jnp.stack([hre, him], axis=-1).reshape(1, ns, N_SSM_GROUPS, SSM_STATE, 2)
    return (y_prompt, y_sample, new_cmp_p, new_slc_p, new_win_p, new_ssm_p,
            new_cmp_s, new_slc_s, new_win_s, new_ssm_s)
```

```python
import functools
import math

import jax
import jax.numpy as jnp
import numpy as np
from jax import lax
from jax.experimental import pallas as pl
from jax.experimental.pallas import tpu as pltpu

F32 = jnp.float32
BF16 = jnp.bfloat16

D_MODEL = 1024
SSM_WIDTH = 512
SSM_GROUP = 16
N_SSM_GROUPS = 32
SSM_STATE = 64
HEAD_DIM = 64
N_Q_HEADS = 8
N_KV_HEADS = 2
GQA = 4
CMP_LEN = 32
CMP_STRIDE = 16
SLC_BLOCK = 64
TOP_N = 8
WINDOW = 512
Q_BLOCK = 256
NEG_INF = -1e30
FORCE_BONUS = 1e4
Q_W = 512
KV_W = 256
NSA_GATE_W = 24
N_EXPERT_GROUPS = 4
EXPERTS_PER_GROUP = 4
N_EXPERTS = 16
D_FF_EXPERT = 256
PLE_DIM = 256
RMS_EPS = 1e-6
PAGE_SIZE = 128

LANES = 128
SUBLANES = 8
N_STATE = N_SSM_GROUPS * SSM_STATE
MIB = 2 ** 20
V7X_VMEM_MIB = 64
VMEM_SMALL_MIB = 48
VMEM_LARGE_MIB = 56
VMEM_MOE_MIB = V7X_VMEM_MIB - 4


def _cparams(sem, vmem_mib):
    return pltpu.CompilerParams(dimension_semantics=sem, vmem_limit_bytes=vmem_mib * MIB)


def _full(shape):
    nd = len(shape)
    return pl.BlockSpec(shape, lambda *_: (0,) * nd)


def _prompt_layout(nseq, t, tm):
    nb = t // tm
    return {
        "grid": (nb, nseq), "tm": tm, "nseq": nseq, "t": t,
        "a": lambda w: ((nseq * t, w), pl.BlockSpec((tm, w), lambda b, s: (s * nb + b, 0))),
    }


def _sample_layout(nseq, t):
    return {
        "grid": (1, t), "tm": nseq,
        "a": lambda w: ((nseq, t * w), pl.BlockSpec((nseq, w), lambda s, b: (0, b))),
        "b": lambda w: ((t * nseq, w), pl.BlockSpec((nseq, w), lambda s, b: (b, 0))),
    }


TK_SLC = 512
TK_WIN = 128


Q_SCALE = HEAD_DIM ** -0.5 * math.log2(math.e)
C_U, C_Q, C_KVC, C_KVS, C_KVW = 0, 512, 1024, 1280, 1536
N_MAIN = 1792


def _dot_t(a, b):
    return lax.dot_general(a, b, (((1,), (1,)), ((), ())), preferred_element_type=F32)


GN_ROWS = 32


def _inproj_prompt_body(x_ref, g_ref, wa_ref, wgn_ref,
                        u_ref, kvc_ref, ksb_ref, kwb_ref,
                        qt_ref, kvct_ref, kvst_ref, kvwt_ref, gnt_ref, vst_ref, vwt_ref, *, nseq):
    s = pl.program_id(1)
    x = x_ref[...]
    inv = lax.rsqrt(jnp.mean(x * x, axis=-1, keepdims=True) + RMS_EPS)
    h = (x * inv * g_ref[...]).astype(BF16)
    tm = h.shape[0]

    def mm(w):
        return jnp.dot(h, w, preferred_element_type=F32)

    u = mm(wa_ref[:, C_U:C_U + SSM_WIDTH])
    for j in range(SSM_WIDTH // LANES):
        u_ref[j, pl.ds(s, tm, stride=nseq), :] = u[:, LANES * j:LANES * (j + 1)]
    kvc = mm(wa_ref[:, C_KVC:C_KVC + KV_W])
    kvc_ref[0] = kvc[:, 0:LANES]
    kvc_ref[1] = kvc[:, LANES:2 * LANES]
    kvct_ref[0] = kvc.T
    kvs = mm(wa_ref[:, C_KVS:C_KVS + KV_W])
    ksb_ref[...] = kvs[:, 0:LANES].astype(BF16)
    kvst = kvs.T
    kvst_ref[0] = kvst
    kvw = mm(wa_ref[:, C_KVW:C_KVW + KV_W])
    kwb_ref[...] = kvw[:, 0:LANES].astype(BF16)
    kvwt = kvw.T
    kvwt_ref[0] = kvwt
    for c in range(tm // TK_WIN):
        vst_ref[0, c] = kvst[LANES:2 * LANES, c * TK_WIN:(c + 1) * TK_WIN].astype(BF16)
        vwt_ref[0, c] = kvwt[LANES:2 * LANES, c * TK_WIN:(c + 1) * TK_WIN].astype(BF16)
    qt_ref[0] = (mm(wa_ref[:, C_Q:C_Q + Q_W]) * Q_SCALE).T.astype(BF16)
    gnt_ref[0] = jax.nn.sigmoid(mm(wgn_ref[...])).T[0:GN_ROWS]


def _inproj_prompt(x2d, lay, g, w):
    tm, nseq, t = lay["tm"], lay["nseq"], lay["t"]
    nb = t // tm
    out_shapes, out_specs, names = [], [], []

    def add(name, shape_spec, dt):
        names.append(name)
        out_shapes.append(jax.ShapeDtypeStruct(shape_spec[0], dt))
        out_specs.append(shape_spec[1])

    def tr(rows):
        return (nseq, rows, t), pl.BlockSpec((1, rows, tm), lambda b, s: (s, 0, b))

    nu = SSM_WIDTH // LANES
    add("u", ((nu, t * nseq, LANES), pl.BlockSpec((nu, tm * nseq, LANES), lambda b, s: (0, b, 0))), F32)
    add("kvc", ((2, nseq * t, LANES), pl.BlockSpec((2, tm, LANES), lambda b, s: (0, s * nb + b, 0))), F32)
    add("ksb", lay["a"](LANES), BF16)
    add("kwb", lay["a"](LANES), BF16)
    add("qt", tr(Q_W), BF16)
    add("kvct", tr(KV_W), F32)
    add("kvst", tr(KV_W), F32)
    add("kvwt", tr(KV_W), F32)
    add("gnt", tr(GN_ROWS), F32)
    for name in ("vst", "vwt"):
        add(name, ((nseq, t // TK_WIN, LANES, TK_WIN),
                   pl.BlockSpec((1, tm // TK_WIN, LANES, TK_WIN), lambda b, s: (s, b, 0, 0))), BF16)
    x_shape, x_spec = lay["a"](D_MODEL)
    ops = [g, w["wa"], w["wgn"]]
    outs = pl.pallas_call(
        functools.partial(_inproj_prompt_body, nseq=nseq),
        grid=lay["grid"],
        in_specs=[x_spec] + [_full(o.shape) for o in ops],
        out_specs=out_specs,
        out_shape=out_shapes,
        compiler_params=_cparams(("arbitrary",) * 2, VMEM_LARGE_MIB),
        name="inproj_prompt",
    )(x2d.reshape(x_shape), *ops)
    return dict(zip(names, outs))


def _inproj_sample_body(x_ref, g_ref, wa_ref, wgn_ref,
                        u_ref, q_ref, kvs_ref, kvw_ref, gn_ref, kvct_ref, kvst_ref, kvwt_ref):
    x = x_ref[...]
    inv = lax.rsqrt(jnp.mean(x * x, axis=-1, keepdims=True) + RMS_EPS)
    h = (x * inv * g_ref[...]).astype(BF16)

    def mm(w):
        return jnp.dot(h, w, preferred_element_type=F32)

    u_ref[...] = mm(wa_ref[:, C_U:C_U + SSM_WIDTH])
    q_ref[...] = mm(wa_ref[:, C_Q:C_Q + Q_W]) * Q_SCALE
    kvs = mm(wa_ref[:, C_KVS:C_KVS + KV_W])
    kvs_ref[...] = kvs
    kvw = mm(wa_ref[:, C_KVW:C_KVW + KV_W])
    kvw_ref[...] = kvw
    gn_ref[...] = jax.nn.sigmoid(mm(wgn_ref[...]))
    kvct_ref[0] = mm(wa_ref[:, C_KVC:C_KVC + KV_W]).T
    kvst_ref[0] = kvs.T
    kvwt_ref[0] = kvw.T


def _inproj_sample(x2d, lay, g, w):
    nseq = lay["tm"]
    ts = lay["grid"][1]
    names = ["u", "q", "kvs", "kvw", "gn"]
    widths = [SSM_WIDTH, Q_W, KV_W, KV_W, LANES]
    out_shapes, out_specs = [], []
    for n, wd in zip(names, widths):
        shp, spec = lay["b" if n == "u" else "a"](wd)
        out_shapes.append(jax.ShapeDtypeStruct(shp, F32))
        out_specs.append(spec)
    for n in ("kvct", "kvst", "kvwt"):
        names.append(n)
        out_shapes.append(jax.ShapeDtypeStruct((ts, KV_W, nseq), F32))
        out_specs.append(pl.BlockSpec((1, KV_W, nseq), lambda s, b: (b, 0, 0)))
    x_shape, x_spec = lay["a"](D_MODEL)
    ops = [g, w["wa"], w["wgn"]]
    outs = pl.pallas_call(
        _inproj_sample_body,
        grid=lay["grid"],
        in_specs=[x_spec] + [_full(o.shape) for o in ops],
        out_specs=out_specs,
        out_shape=out_shapes,
        compiler_params=_cparams(("arbitrary",) * 2, VMEM_LARGE_MIB),
        name="inproj_sample",
    )(x2d.reshape(x_shape), *ops)
    return dict(zip(names, outs))


def _inproj_params(w_in0):
    return {"wa": w_in0[:, :N_MAIN].astype(BF16),
            "wgn": jnp.pad(w_in0[:, N_MAIN:N_MAIN + NSA_GATE_W], ((0, 0), (0, LANES - NSA_GATE_W))).astype(BF16),
            "wgab": w_in0[:, N_MAIN + NSA_GATE_W:].astype(BF16)}


S5_LANES = 1024


def _s5_prompt_body(u_ref, wb_ref, ar_ref, ai_ref, cw_ref, d_ref, wglu_ref, bglu_ref, wso_ref,
                    abr_ref, hlast_ref, lhs_ref, bu_ref, h8_ref, p_ref, hstate_ref):
    c = pl.program_id(0)
    nseq = 4
    r4 = u_ref.shape[1]
    tc = r4 // nseq
    half = tc // 2

    @pl.when(c == 0)
    def _():
        hstate_ref[...] = jnp.zeros_like(hstate_ref)

    u = jnp.concatenate([u_ref[j] for j in range(SSM_WIDTH // LANES)], axis=1)
    row2 = lax.broadcasted_iota(jnp.int32, (r4, SSM_WIDTH), 0)
    lo2 = (row2 % SUBLANES) < nseq
    up = pltpu.roll(u, r4 - nseq, axis=0)
    dn = pltpu.roll(u, nseq, axis=0)
    swapped = jnp.where(lo2, up, dn)
    zero = jnp.zeros_like(u)
    ev_re = jnp.where(lo2, u, zero).astype(BF16).reshape(half, SUBLANES, SSM_WIDTH)
    ev_im = jnp.where(lo2, zero, swapped).astype(BF16).reshape(half, SUBLANES, SSM_WIDTH)
    od_re = jnp.where(lo2, swapped, zero).astype(BF16).reshape(half, SUBLANES, SSM_WIDTH)
    od_im = jnp.where(lo2, zero, u).astype(BF16).reshape(half, SUBLANES, SSM_WIDTH)
    for j in range(4):
        sl = slice(LANES * j, LANES * (j + 1))
        lhs_ref[:, 0:8, 256 * j:256 * j + LANES] = ev_re[:, :, sl]
        lhs_ref[:, 0:8, 256 * j + LANES:256 * (j + 1)] = ev_im[:, :, sl]
        lhs_ref[:, 8:16, 256 * j:256 * j + LANES] = od_re[:, :, sl]
        lhs_ref[:, 8:16, 256 * j + LANES:256 * (j + 1)] = od_im[:, :, sl]
    for j in range(4):
        lhs = lhs_ref[:, :, 256 * j:256 * (j + 1)].reshape(tc * SUBLANES, 256)
        bu_ref[:, 512 * j:512 * (j + 1)] = jnp.dot(lhs, wb_ref[j], preferred_element_type=F32)

    for lc in range(N_STATE // S5_LANES):
        sl = slice(S5_LANES * lc, S5_LANES * (lc + 1))
        ar = ar_ref[:, sl]
        ai = ai_ref[:, sl]

        def step(t, h, sl=sl, ar=ar, ai=ai):
            r0 = pl.multiple_of(t * SUBLANES, SUBLANES)
            h = ar * h + ai * pltpu.roll(h, nseq, axis=0) + bu_ref[pl.ds(r0, SUBLANES), sl]
            h8_ref[pl.ds(r0, SUBLANES), sl] = h
            return h

        hstate_ref[:, sl] = lax.fori_loop(0, tc, step, hstate_ref[:, sl], unroll=8)
    hlast_ref[...] = hstate_ref[...]

    for j in range(4):
        pj = jnp.dot(h8_ref[:, 512 * j:512 * (j + 1)].astype(BF16), cw_ref[j], preferred_element_type=F32)
        p_ref[2 * j] = pj[:, 0:LANES]
        p_ref[2 * j + 1] = pj[:, LANES:2 * LANES]
    ys = []
    for s in range(nseq):
        parts = []
        for j in range(4):
            re = p_ref[2 * j, pl.ds(s, tc, stride=SUBLANES), :]
            im = p_ref[2 * j + 1, pl.ds(nseq + s, tc, stride=SUBLANES), :]
            us = u_ref[j, pl.ds(s, tc, stride=nseq), :]
            parts.append(re + im + d_ref[:, LANES * j:LANES * (j + 1)] * us)
        ys.append(jnp.concatenate(parts, axis=1))
    y = jnp.concatenate(ys, axis=0)
    zg = jax.nn.gelu(y)
    gate = jnp.dot(zg.astype(BF16), wglu_ref[...], preferred_element_type=F32) + bglu_ref[...]
    glu = (zg * jax.nn.sigmoid(gate)).astype(BF16)
    abr = jnp.dot(glu, wso_ref[...], preferred_element_type=F32)
    for s in range(nseq):
        abr_ref[s] = abr[s * tc:(s + 1) * tc].astype(BF16)


def _s5_prompt(u_ts, sp, wglu, bglu, wso, t_total, tc):
    nseq = 4
    grid = (t_total // tc,)
    abr, hlast = pl.pallas_call(
        _s5_prompt_body,
        grid=grid,
        in_specs=[pl.BlockSpec((SSM_WIDTH // LANES, tc * nseq, LANES), lambda c: (0, c, 0)),
                  _full(sp["wb8"].shape), _full(sp["ar8"].shape), _full(sp["ai8"].shape), _full(sp["cw8"].shape),
                  _full(sp["d"].shape), _full(wglu.shape), _full(bglu.shape), _full(wso.shape)],
        out_specs=[pl.BlockSpec((nseq, tc, D_MODEL), lambda c: (0, c, 0)),
                   pl.BlockSpec((SUBLANES, N_STATE), lambda c: (0, 0))],
        out_shape=[jax.ShapeDtypeStruct((nseq, t_total, D_MODEL), BF16),
                   jax.ShapeDtypeStruct((SUBLANES, N_STATE), F32)],
        scratch_shapes=[pltpu.VMEM((tc // 2, 2 * SUBLANES, 1024), BF16),
                        pltpu.VMEM((tc * SUBLANES, N_STATE), F32),
                        pltpu.VMEM((tc * SUBLANES, N_STATE), F32),
                        pltpu.VMEM((8, tc * SUBLANES, LANES), F32),
                        pltpu.VMEM((SUBLANES, N_STATE), F32)],
        compiler_params=_cparams(("arbitrary",), VMEM_LARGE_MIB),
        name="s5_prompt",
    )(u_ts, sp["wb8"], sp["ar8"], sp["ai8"], sp["cw8"], sp["d"], wglu, bglu, wso)
    return abr, hlast


def _s5_params(lam_re, lam_im, log_dt, b_re, b_im, c_re, c_im, d_skip):
    lam = lax.complex(lam_re.astype(F32), lam_im.astype(F32))
    dt = jnp.exp(log_dt.astype(F32))[:, None]
    a_bar = jnp.exp(lam * dt)
    b = lax.complex(b_re.astype(F32), b_im.astype(F32))
    b_bar = ((a_bar - 1.0) / lam)[..., None] * b
    eye8 = jnp.eye(8, dtype=F32)

    def bd_b(m):
        return jnp.einsum("ab,jbpc->jacbp", eye8, m.reshape(4, 8, SSM_STATE, SSM_GROUP)).reshape(4, 128, 512)

    def bd_c(m):
        return jnp.einsum("ab,jbcp->japbc", eye8, m.reshape(4, 8, SSM_GROUP, SSM_STATE)).reshape(4, 512, 128)

    wre, wim = bd_b(b_bar.real), bd_b(b_bar.imag)
    cre, cim = bd_c(c_re.astype(F32)), bd_c(c_im.astype(F32))
    ar = a_bar.real.reshape(1, N_STATE)
    ai = a_bar.imag.reshape(1, N_STATE)
    sign = jnp.concatenate([-jnp.ones((4, 1), F32), jnp.ones((4, 1), F32)], axis=0)
    return {
        "wb8": jnp.concatenate([wre, wim], axis=1).astype(BF16),
        "cw8": jnp.concatenate([cre, -cim], axis=2).astype(BF16),
        "ar8": jnp.broadcast_to(ar, (SUBLANES, N_STATE)),
        "ai8": sign * ai,
        "wre": wre.astype(BF16), "wim": wim.astype(BF16),
        "cre": cre.astype(BF16), "cim": cim.astype(BF16),
        "ar": ar, "ai": ai,
        "d": d_skip.astype(F32).reshape(1, SSM_WIDTH),
    }


def _cmp_params(cmp_pe, cmp_w1, cmp_w2):
    eye2 = jnp.eye(2, dtype=F32)
    nhalf = CMP_LEN // CMP_STRIDE
    w1r = cmp_w1.astype(F32).reshape(2, nhalf, CMP_STRIDE, HEAD_DIM, HEAD_DIM)
    wk = jnp.einsum("kside,ph->kipdshe", w1r, eye2).reshape(2, CMP_STRIDE * LANES, nhalf * LANES)
    bk = jnp.einsum("kld,klde->ke", cmp_pe.astype(F32), cmp_w1.astype(F32), precision=lax.Precision.HIGHEST)
    w2k = jnp.einsum("kef,ph->kpehf", cmp_w2.astype(F32), eye2).reshape(2, LANES, LANES)
    return {"wk": wk.astype(BF16), "bk": jnp.tile(bk, (1, N_KV_HEADS)), "w2k": w2k.astype(BF16),
            "w2kt": jnp.swapaxes(w2k, 1, 2).astype(BF16)}


def _compress_hidden(tap, nch, kv, wk_ref, bk_ref):
    x = jnp.concatenate([tap(i).astype(BF16) for i in range(CMP_STRIDE)], axis=1)
    pp = jnp.dot(x, wk_ref[kv], preferred_element_type=F32)
    pre = pp[:, 0:LANES] + pltpu.roll(pp[:, LANES:2 * LANES], nch - 1, axis=0) + bk_ref[kv:kv + 1, :]
    return jax.nn.gelu(pre).astype(BF16)


def _compress_prompt_body(x_ref, wk_ref, bk_ref, w2k_ref, w2kt_ref, ck_ref, cvt_ref):
    nch = x_ref.shape[1] // CMP_STRIDE
    hid = [_compress_hidden(lambda i, kv=kv: x_ref[kv, pl.ds(i, nch, stride=CMP_STRIDE), :], nch, kv, wk_ref, bk_ref)
           for kv in range(2)]
    ck_ref[0] = jnp.dot(hid[0], w2k_ref[0], preferred_element_type=F32).astype(BF16)
    cvt_ref[0] = _dot_t(w2kt_ref[1], hid[1]).astype(BF16)


def _compress_prompt(kvc2, cp, nseq, t):
    nch = t // CMP_STRIDE
    return pl.pallas_call(
        _compress_prompt_body,
        grid=(nseq,),
        in_specs=[pl.BlockSpec((2, t, LANES), lambda n: (0, n, 0)),
                  _full(cp["wk"].shape), _full(cp["bk"].shape), _full(cp["w2k"].shape), _full(cp["w2kt"].shape)],
        out_specs=[pl.BlockSpec((1, nch, LANES), lambda n: (n, 0, 0)),
                   pl.BlockSpec((1, LANES, nch), lambda n: (n, 0, 0))],
        out_shape=[jax.ShapeDtypeStruct((nseq, nch, LANES), BF16),
                   jax.ShapeDtypeStruct((nseq, LANES, nch), BF16)],
        compiler_params=_cparams(("arbitrary",), VMEM_SMALL_MIB),
        name="compress_prompt",
    )(kvc2, cp["wk"], cp["bk"], cp["w2k"], cp["w2kt"])


def _overlap_t(n_cmp_pad, n_slc_pad):
    j = np.arange(n_cmp_pad)[None, :]
    s = np.arange(n_slc_pad)[:, None]
    ov = (j * CMP_STRIDE <= s * SLC_BLOCK + SLC_BLOCK - 1) & (j * CMP_STRIDE + CMP_LEN - 1 >= s * SLC_BLOCK)
    return jnp.asarray(ov, dtype=BF16)


def _softmax_cols(s, valid):
    sm = jnp.where(valid, s, NEG_INF)
    mx = jnp.max(sm, axis=0, keepdims=True)
    e = jnp.where(valid, jnp.exp2(sm - mx), 0.0)
    l = jnp.sum(e, axis=0, keepdims=True)
    return e * (1.0 / jnp.maximum(l, 1e-30))


def _select_blocks(imp, blk, pos, nblk, axis=0):
    cur = pos // SLC_BLOCK
    forced = (blk == 0) | (blk == cur) | (blk == cur - 1)
    v = jnp.where(forced, imp + FORCE_BONUS, imp)
    v = jnp.where(blk * SLC_BLOCK <= pos, v, NEG_INF)
    v = jnp.where(blk < nblk, v, -3e38)
    blk_f = blk.astype(F32)
    neg = jnp.full(imp.shape, NEG_INF, F32)
    for _ in range(min(TOP_N, nblk)):
        mx = jnp.max(v, axis=axis, keepdims=True)
        first = jnp.min(jnp.where(v == mx, blk_f, float(imp.shape[axis])), axis=axis, keepdims=True)
        pick = blk_f == first
        neg = jnp.where(pick, 0.0, neg)
        v = jnp.where(pick, -3e38, v)
    return neg


CB = 2 * LANES


def _attn_prompt_body(q_ref, gn_ref, ck_ref, cvt_ref, ks_ref, vst_ref, kw_ref, vwt_ref, ovt_ref,
                      o_ref, kaug_ref, kwaug_ref, qaug_ref, acc_ref):
    i = pl.program_id(1)
    t = ks_ref.shape[1]
    nch = ck_ref.shape[1]
    nslc = t // SLC_BLOCK
    qb = Q_BLOCK
    ncol = N_Q_HEADS * qb
    ncb = ncol // CB
    q0 = i * qb
    one_row = 2 * LANES - HEAD_DIM
    hrows = [slice(HEAD_DIM * ((CB * cb // qb) // GQA), HEAD_DIM * ((CB * cb // qb) // GQA + 1)) for cb in range(ncb)]

    @pl.when(i == 0)
    def _():
        kaug_ref[:, 0:LANES] = ks_ref[0]
        blk = lax.broadcasted_iota(jnp.int32, (t, LANES), 0) // SLC_BLOCK
        col = lax.broadcasted_iota(jnp.int32, (t, LANES), 1)
        kaug_ref[:, LANES:2 * LANES] = jnp.where(blk == col, 1.0, 0.0).astype(BF16)
        padcol = lax.broadcasted_iota(jnp.int32, (WINDOW, 2 * LANES), 1)
        kwaug_ref[0:WINDOW, :] = jnp.where(padcol == one_row, NEG_INF, 0.0).astype(BF16)
        kwaug_ref[WINDOW:WINDOW + t, 0:LANES] = kw_ref[0]
        kwaug_ref[WINDOW:WINDOW + t, LANES:2 * LANES] = jnp.zeros((t, LANES), BF16)

    zeros64 = jnp.zeros((HEAD_DIM, qb), BF16)
    for j in range(N_Q_HEADS):
        dst = j // GQA
        qaug_ref[HEAD_DIM * dst:HEAD_DIM * (dst + 1), qb * j:qb * (j + 1)] = q_ref[0, HEAD_DIM * j:HEAD_DIM * (j + 1), :]
        qaug_ref[HEAD_DIM * (1 - dst):HEAD_DIM * (2 - dst), qb * j:qb * (j + 1)] = zeros64
    tail_row = lax.broadcasted_iota(jnp.int32, (HEAD_DIM, ncol), 0)
    qaug_ref[one_row:2 * LANES, :] = jnp.where(tail_row == 0, 1.0, 0.0).astype(BF16)

    pos_c = q0 + (lax.broadcasted_iota(jnp.int32, (nch, CB), 1) & (qb - 1))
    cvalid = lax.broadcasted_iota(jnp.int32, (nch, CB), 0) * CMP_STRIDE + (CMP_LEN - 1) <= pos_c
    scs = [jnp.dot(ck_ref[0], qaug_ref[0:LANES, CB * cb:CB * (cb + 1)], preferred_element_type=F32)
           for cb in range(ncb)]
    pcs = [_softmax_cols(sc, cvalid).astype(BF16) for sc in scs]
    oc = jnp.concatenate([jnp.dot(cvt_ref[0, hrows[cb], :], pc, preferred_element_type=F32)
                          for cb, pc in enumerate(pcs)], axis=1)
    imp = jnp.concatenate([jnp.dot(ovt_ref[...], pc, preferred_element_type=F32) for pc in pcs], axis=1)

    qaug_ref[LANES:one_row, :] = jnp.zeros((one_row - LANES, ncol), BF16)
    q_blocks = qb // TK_WIN
    npiece = (WINDOW + qb) // TK_WIN
    kws = [kwaug_ref[pl.ds(pl.multiple_of(q0 + w * TK_WIN, TK_WIN), TK_WIN), :] for w in range(npiece)]
    vwt = jnp.concatenate([vwt_ref[0, jnp.maximum(i * q_blocks + w - WINDOW // TK_WIN, 0)] for w in range(npiece)],
                          axis=1)
    wrow = lax.broadcasted_iota(jnp.int32, (TK_WIN, CB), 0)
    wcol = lax.broadcasted_iota(jnp.int32, (TK_WIN, CB), 1) & (qb - 1)
    wbias = []
    for w in range(npiece):
        lo, hi = w * TK_WIN - WINDOW, w * TK_WIN - WINDOW + TK_WIN - 1
        if hi <= 0 and qb - 1 - lo < WINDOW:
            wbias.append(None)
        else:
            dlt = wcol - wrow - lo
            wbias.append(jnp.where((dlt >= 0) & (dlt < WINDOW), 0.0, NEG_INF))
    kw_all = jnp.concatenate(kws, axis=0)
    sws = [jnp.dot(kw_all, qaug_ref[:, CB * cb:CB * (cb + 1)], preferred_element_type=F32) for cb in range(ncb)]
    es, rls = [], []
    for s in sws:
        s = jnp.concatenate([s[TK_WIN * w:TK_WIN * (w + 1)] if b is None else s[TK_WIN * w:TK_WIN * (w + 1)] + b
                             for w, b in enumerate(wbias)], axis=0)
        e = jnp.exp2(s - jnp.max(s, axis=0, keepdims=True))
        es.append(e.astype(BF16))
        rls.append(1.0 / jnp.sum(e, axis=0, keepdims=True))
    ow = jnp.concatenate([jnp.dot(vwt[hrows[cb]], e, preferred_element_type=F32) * rl
                          for cb, (e, rl) in enumerate(zip(es, rls))], axis=1)

    blk = lax.broadcasted_iota(jnp.int32, (nslc, N_KV_HEADS * qb), 0)
    pos_q = q0 + (lax.broadcasted_iota(jnp.int32, (nslc, N_KV_HEADS * qb), 1) & (qb - 1))
    vs = []
    for h in range(N_KV_HEADS):
        v = imp[0:nslc, qb * GQA * h:qb * GQA * h + qb]
        for g in range(1, GQA):
            v = v + imp[0:nslc, qb * (GQA * h + g):qb * (GQA * h + g + 1)]
        vs.append(v)
    neg = _select_blocks(jnp.concatenate(vs, axis=1), blk, pos_q, nslc).astype(BF16)
    for j in range(N_Q_HEADS):
        h = j // GQA
        qaug_ref[LANES:LANES + nslc, qb * j:qb * (j + 1)] = neg[:, qb * h:qb * (h + 1)]

    brow = lax.broadcasted_iota(jnp.int32, (qb, CB), 0)
    bcol = lax.broadcasted_iota(jnp.int32, (qb, CB), 1) & (qb - 1)
    tri_lo = jnp.where(brow <= bcol, 0.0, NEG_INF)

    acc_ref[...] = jnp.zeros_like(acc_ref)

    def sel_tile(k0, nk, vt, carry, bias):
        m, l = carry
        ka = kaug_ref[pl.ds(k0, nk), :]
        css = [slice(CB * cb, CB * (cb + 1)) for cb in range(ncb)]
        ss = [jnp.dot(ka, qaug_ref[:, cs], preferred_element_type=F32) for cs in css]
        ms, ls, ps, alphas = [], [], [], []
        for cs, s in zip(css, ss):
            if bias is not None:
                s = s + bias
            mn = jnp.maximum(m[:, cs], jnp.max(s, axis=0, keepdims=True))
            alpha = jnp.exp2(m[:, cs] - mn)
            p = jnp.exp2(s - mn)
            ms.append(mn)
            ls.append(alpha * l[:, cs] + jnp.sum(p, axis=0, keepdims=True))
            ps.append(p.astype(BF16))
            alphas.append(alpha)
        pvs = [jnp.dot(vt[hrows[cb]], p, preferred_element_type=F32) for cb, p in enumerate(ps)]
        for cs, alpha, pv in zip(css, alphas, pvs):
            acc_ref[:, cs] = alpha * acc_ref[:, cs] + pv
        return jnp.concatenate(ms, axis=1), jnp.concatenate(ls, axis=1)

    def vt_blocks(ref, b0, n):
        return jnp.concatenate([ref[0, b0 + j] for j in range(n)], axis=1) if n > 1 else ref[0, b0]

    big_blocks = TK_SLC // TK_WIN

    def big_tile(kt, carry):
        return sel_tile(pl.multiple_of(kt * TK_SLC, TK_SLC), TK_SLC, vt_blocks(vst_ref, kt * big_blocks, big_blocks),
                        carry, None)

    def small_tile(kb, carry):
        return sel_tile(pl.multiple_of(kb * qb, qb), qb, vt_blocks(vst_ref, kb * q_blocks, q_blocks), carry, None)

    carry = (jnp.full((1, ncol), NEG_INF, F32), jnp.zeros((1, ncol), F32))
    nbig = q0 // TK_SLC
    carry = lax.fori_loop(0, nbig // 2, lambda kp, c: big_tile(2 * kp + 1, big_tile(2 * kp, c)), carry)
    carry = lax.fori_loop(nbig - nbig % 2, nbig, big_tile, carry)
    carry = lax.fori_loop(nbig * (TK_SLC // qb), i, small_tile, carry)
    _, l = sel_tile(pl.multiple_of(q0, qb), qb, vt_blocks(vst_ref, i * q_blocks, q_blocks), carry, tri_lo)
    osel = acc_ref[...] * (1.0 / l)

    gt = gn_ref[0]
    for c in range(N_Q_HEADS // 2):
        rows = []
        for hh in range(2):
            j = 2 * c + hh
            cs = slice(qb * j, qb * (j + 1))
            rows.append(gt[3 * j:3 * j + 1, :] * oc[:, cs] + gt[3 * j + 1:3 * j + 2, :] * osel[:, cs]
                        + gt[3 * j + 2:3 * j + 3, :] * ow[:, cs])
        o_ref[:, LANES * c:LANES * (c + 1)] = jnp.concatenate(rows, axis=0).T.astype(o_ref.dtype)


def _attn_prompt(q, gn, ck, cvt, ksb, vst, kwb, vwt, nseq, t):
    nb = t // Q_BLOCK
    nch = t // CMP_STRIDE
    nslc = t // SLC_BLOCK
    ovt = _overlap_t(nch, max(nslc, SUBLANES))
    row = lambda n, i: (n * nb + i, 0)
    seq3 = lambda n, i: (n, 0, 0)
    seq4 = lambda n, i: (n, 0, 0, 0)
    col3 = lambda n, i: (n, 0, i)
    return pl.pallas_call(
        _attn_prompt_body,
        grid=(nseq, nb),
        in_specs=[pl.BlockSpec((1, Q_W, Q_BLOCK), col3), pl.BlockSpec((1, gn.shape[1], Q_BLOCK), col3),
                  pl.BlockSpec((1, nch, LANES), seq3), pl.BlockSpec((1, LANES, nch), seq3),
                  pl.BlockSpec((1, t, LANES), seq3), pl.BlockSpec((1, t // TK_WIN, LANES, TK_WIN), seq4),
                  pl.BlockSpec((1, t, LANES), seq3), pl.BlockSpec((1, t // TK_WIN, LANES, TK_WIN), seq4),
                  _full(ovt.shape)],
        out_specs=pl.BlockSpec((Q_BLOCK, Q_W), row),
        out_shape=jax.ShapeDtypeStruct((nseq * t, Q_W), BF16),
        scratch_shapes=[pltpu.VMEM((t, 2 * LANES), BF16),
                        pltpu.VMEM((WINDOW + t, 2 * LANES), BF16),
                        pltpu.VMEM((2 * LANES, N_Q_HEADS * Q_BLOCK), BF16),
                        pltpu.VMEM((HEAD_DIM, N_Q_HEADS * Q_BLOCK), F32)],
        compiler_params=_cparams(("arbitrary", "arbitrary"), VMEM_LARGE_MIB),
        name="attn_prompt",
    )(q, gn, ck, cvt, ksb.reshape(nseq, t, LANES), vst, kwb.reshape(nseq, t, LANES), vwt, ovt)


def _post_body(x_ref, abr_ref, on_ref, g1_ref, wgab_ref, wno_ref, wo_ref, g2_ref, x1_ref, h2_ref):
    x = x_ref[...]
    inv1 = lax.rsqrt(jnp.mean(x * x, axis=-1, keepdims=True) + RMS_EPS)
    h = (x * inv1 * g1_ref[...]).astype(BF16)
    ga = jax.nn.sigmoid(jnp.dot(h, wgab_ref[:, 0:D_MODEL], preferred_element_type=F32))
    gb = jax.nn.sigmoid(jnp.dot(h, wgab_ref[:, D_MODEL:2 * D_MODEL], preferred_element_type=F32))
    bbr = jnp.dot(on_ref[...].astype(BF16), wno_ref[...], preferred_element_type=F32)
    merged = ga * abr_ref[...].astype(F32) + gb * bbr
    x1 = x + jnp.dot(merged.astype(BF16), wo_ref[...], preferred_element_type=F32)
    x1_ref[...] = x1
    inv = lax.rsqrt(jnp.mean(x1 * x1, axis=-1, keepdims=True) + RMS_EPS)
    h2_ref[...] = (x1 * inv * g2_ref[...]).astype(BF16)


def _post(x2d, abr, abr_lay, onsa, g1, wgab, wno, wo, g2, lay, out_lay):
    x_shape, x_spec = lay["a"](D_MODEL)
    abr_shape, abr_spec = lay[abr_lay](D_MODEL)
    on_shape, on_spec = lay["a"](Q_W)
    o_shape, o_spec = lay[out_lay](D_MODEL)
    return pl.pallas_call(
        _post_body,
        grid=lay["grid"],
        in_specs=[x_spec, abr_spec, on_spec, _full(g1.shape), _full(wgab.shape), _full(wno.shape), _full(wo.shape),
                  _full(g2.shape)],
        out_specs=[o_spec, o_spec],
        out_shape=[jax.ShapeDtypeStruct(o_shape, F32), jax.ShapeDtypeStruct(o_shape, BF16)],
        compiler_params=_cparams(("arbitrary",) * len(lay["grid"]), VMEM_SMALL_MIB),
        name="post",
    )(x2d.reshape(x_shape), abr.reshape(abr_shape), onsa.reshape(on_shape), g1, wgab, wno, wo, g2)


def _route(logits):
    lane = lax.broadcasted_iota(jnp.int32, logits.shape, 1).astype(F32)
    big = float(LANES)
    glog = jnp.where(lane < N_EXPERT_GROUPS, logits, -jnp.inf)
    gmax = jnp.max(glog, axis=1, keepdims=True)
    gsel = jnp.min(jnp.where(glog == gmax, lane, big), axis=1, keepdims=True)
    gw = 1.0 / jnp.sum(jnp.exp(glog - gmax), axis=1, keepdims=True)
    lo = N_EXPERT_GROUPS + EXPERTS_PER_GROUP * gsel
    el = jnp.where((lane >= lo) & (lane < lo + EXPERTS_PER_GROUP), logits, -jnp.inf)
    v1 = jnp.max(el, axis=1, keepdims=True)
    i1 = jnp.min(jnp.where(el == v1, lane, big), axis=1, keepdims=True)
    el2 = jnp.where(lane == i1, -jnp.inf, el)
    v2 = jnp.max(el2, axis=1, keepdims=True)
    i2 = jnp.min(jnp.where(el2 == v2, lane, big), axis=1, keepdims=True)
    e2 = jnp.exp(v2 - v1)
    w1 = gw / (1.0 + e2)
    return jnp.where(lane == i1, w1, 0.0) + jnp.where(lane == i2, w1 * e2, 0.0)


def _moe_body(x1_ref, h2_ref, p_ref, wr_ref, br_ref, wg_ref, wu_ref, wd_ref, wpg_ref, wp_ref, gf_ref,
              y_ref, acc_ref, comb_ref, *, tsplit):
    g = pl.program_id(1)
    h2 = h2_ref[...]

    @pl.when(g == 0)
    def _():
        logits = jnp.dot(h2, wr_ref[...], preferred_element_type=F32) + br_ref[...]
        comb_ref[...] = _route(logits)
        acc_ref[...] = jnp.zeros_like(acc_ref)

    comb = comb_ref[...]
    lane = lax.broadcasted_iota(jnp.int32, comb.shape, 1)
    acc = acc_ref[...]
    for k in range(EXPERTS_PER_GROUP):
        e_lane = N_EXPERT_GROUPS + EXPERTS_PER_GROUP * g + k
        ce = jnp.sum(jnp.where(lane == e_lane, comb, 0.0), axis=1, keepdims=True)
        a = jnp.dot(h2, wg_ref[k], preferred_element_type=F32)
        b = jnp.dot(h2, wu_ref[k], preferred_element_type=F32)
        act = (jax.nn.silu(a) * b * ce).astype(BF16)
        acc = acc + jnp.dot(act, wd_ref[k], preferred_element_type=F32)
    acc_ref[...] = acc

    @pl.when(g == N_EXPERT_GROUPS - 1)
    def _():
        x2 = x1_ref[...] + acc_ref[...]
        rows = x2.shape[0] // tsplit
        if tsplit == 1:
            p = p_ref[...]
        else:
            p = jnp.concatenate([p_ref[:, PLE_DIM * t:PLE_DIM * (t + 1)] for t in range(tsplit)], axis=0)
        gate = jax.nn.sigmoid(jnp.dot(x2.astype(BF16), wpg_ref[...], preferred_element_type=F32))
        x3 = x2 + gate * jnp.dot(p.astype(BF16), wp_ref[...], preferred_element_type=F32)
        inv = lax.rsqrt(jnp.mean(x3 * x3, axis=-1, keepdims=True) + RMS_EPS)
        y = x3 * inv * gf_ref[...]
        if tsplit == 1:
            y_ref[...] = y
        else:
            for t in range(tsplit):
                y_ref[:, D_MODEL * t:D_MODEL * (t + 1)] = y[rows * t:rows * (t + 1)]


def _moe(x1, h2, p, mp, tm, tsplit):
    rows = x1.shape[0]
    nrb = rows // tm
    rb = lambda r, g: (r, 0)
    grp = lambda r, g: (g, 0, 0)
    if tsplit == 1:
        p_spec = pl.BlockSpec((tm, PLE_DIM), rb)
        y_spec = pl.BlockSpec((tm, D_MODEL), rb)
        y_shape = (rows, D_MODEL)
    else:
        assert nrb == 1
        p_spec = _full(p.shape)
        y_shape = (rows // tsplit, tsplit * D_MODEL)
        y_spec = _full(y_shape)
    return pl.pallas_call(
        functools.partial(_moe_body, tsplit=tsplit),
        grid=(nrb, N_EXPERT_GROUPS),
        in_specs=[pl.BlockSpec((tm, D_MODEL), rb), pl.BlockSpec((tm, D_MODEL), rb), p_spec,
                  _full(mp["wr"].shape), _full(mp["br"].shape),
                  pl.BlockSpec((EXPERTS_PER_GROUP, D_MODEL, D_FF_EXPERT), grp),
                  pl.BlockSpec((EXPERTS_PER_GROUP, D_MODEL, D_FF_EXPERT), grp),
                  pl.BlockSpec((EXPERTS_PER_GROUP, D_FF_EXPERT, D_MODEL), grp),
                  _full(mp["wpg"].shape), _full(mp["wp"].shape), _full(mp["gf"].shape)],
        out_specs=y_spec,
        out_shape=jax.ShapeDtypeStruct(y_shape, F32),
        scratch_shapes=[pltpu.VMEM((tm, D_MODEL), F32), pltpu.VMEM((tm, LANES), F32)],
        compiler_params=_cparams(("arbitrary", "arbitrary"), VMEM_MOE_MIB),
        name="moe_ple",
    )(x1, h2, p, mp["wr"], mp["br"], mp["wg"], mp["wu"], mp["wd"], mp["wpg"], mp["wp"], mp["gf"])


def _s5_sample_body(u_ref, h0re_ref, h0im_ref, wre_ref, wim_ref, ar_ref, ai_ref, cre_ref, cim_ref, d_ref,
                    wglu_ref, bglu_ref, wso_ref, abr_ref, hre_out_ref, him_out_ref,
                    bure_ref, buim_ref, hre_ref, him_ref, *, nseq, nstep):
    u = u_ref[...]
    ub = u.astype(BF16)
    for j in range(4):
        lhs = ub[:, LANES * j:LANES * (j + 1)]
        bure_ref[:, 512 * j:512 * (j + 1)] = jnp.dot(lhs, wre_ref[j], preferred_element_type=F32)
        buim_ref[:, 512 * j:512 * (j + 1)] = jnp.dot(lhs, wim_ref[j], preferred_element_type=F32)
    for lc in range(4):
        sl = slice(512 * lc, 512 * (lc + 1))
        ar = jnp.broadcast_to(ar_ref[:, sl], (SUBLANES, 512))
        ai = jnp.broadcast_to(ai_ref[:, sl], (SUBLANES, 512))

        def body(rc, carry, sl=sl, ar=ar, ai=ai):
            r0 = pl.multiple_of(rc * SUBLANES, SUBLANES)
            hr = h0re_ref[pl.ds(r0, SUBLANES), sl]
            hi = h0im_ref[pl.ds(r0, SUBLANES), sl]
            for t in range(nstep):
                rr = pl.multiple_of(t * nseq + rc * SUBLANES, SUBLANES)
                hr, hi = (ar * hr - ai * hi + bure_ref[pl.ds(rr, SUBLANES), sl],
                          ar * hi + ai * hr + buim_ref[pl.ds(rr, SUBLANES), sl])
                hre_ref[pl.ds(rr, SUBLANES), sl] = hr
                him_ref[pl.ds(rr, SUBLANES), sl] = hi
            hre_out_ref[pl.ds(r0, SUBLANES), sl] = hr
            him_out_ref[pl.ds(r0, SUBLANES), sl] = hi
            return carry

        lax.fori_loop(0, nseq // SUBLANES, body, 0)
    parts = []
    for j in range(4):
        sl = slice(512 * j, 512 * (j + 1))
        parts.append(jnp.dot(hre_ref[:, sl].astype(BF16), cre_ref[j], preferred_element_type=F32)
                     - jnp.dot(him_ref[:, sl].astype(BF16), cim_ref[j], preferred_element_type=F32))
    y = jnp.concatenate(parts, axis=1) + d_ref[...] * u
    zg = jax.nn.gelu(y)
    gate = jnp.dot(zg.astype(BF16), wglu_ref[...], preferred_element_type=F32) + bglu_ref[...]
    glu = (zg * jax.nn.sigmoid(gate)).astype(BF16)
    abr_ref[...] = jnp.dot(glu, wso_ref[...], preferred_element_type=F32).astype(BF16)


def _s5_sample(u_ts, h0re, h0im, sp, wglu, bglu, wso, nseq, nstep):
    rows = nseq * nstep
    ops = [u_ts, h0re, h0im, sp["wre"], sp["wim"], sp["ar"], sp["ai"], sp["cre"], sp["cim"], sp["d"], wglu, bglu, wso]
    return pl.pallas_call(
        functools.partial(_s5_sample_body, nseq=nseq, nstep=nstep),
        grid=(1,),
        in_specs=[_full(o.shape) for o in ops],
        out_specs=[_full((rows, D_MODEL)), _full((nseq, N_STATE)), _full((nseq, N_STATE))],
        out_shape=[jax.ShapeDtypeStruct((rows, D_MODEL), BF16),
                   jax.ShapeDtypeStruct((nseq, N_STATE), F32), jax.ShapeDtypeStruct((nseq, N_STATE), F32)],
        scratch_shapes=[pltpu.VMEM((rows, N_STATE), F32) for _ in range(4)],
        compiler_params=_cparams(("arbitrary",), VMEM_LARGE_MIB),
        name="s5_sample",
    )(*ops)


def _softmax_rows(s, valid):
    sm = jnp.where(valid, s, NEG_INF)
    mx = jnp.max(sm, axis=1, keepdims=True)
    e = jnp.where(valid, jnp.exp2(sm - mx), 0.0)
    l = jnp.sum(e, axis=1, keepdims=True)
    return e * (1.0 / jnp.maximum(l, 1e-30))


SAMPLE_SEQS_PER_STEP = 4
CMP_PITCH = 24


def _attn_sample_body(pt_ref, q_ref, gn_ref, nks_ref, nkw_ref, wint_ref, wk_ref, bk_ref, w2k_ref, ov_ref, e_ref,
                      cmp_hbm, slc_hbm, o_ref, nwint_ref, xrow_ref, pages_ref, sem_ref, *, npage, past_len, nsub, tq):
    n = pl.program_id(0)
    nsteps = pl.num_programs(0)
    slot = lax.rem(n, 2)
    nrow = N_Q_HEADS * tq
    nwin = wint_ref.shape[2]
    nslc = -(-(past_len + tq) // SLC_BLOCK)
    nch = past_len // CMP_STRIDE
    per_page = PAGE_SIZE // CMP_STRIDE

    def page_copy(step, into, c, s, p):
        src = (cmp_hbm, slc_hbm)[c]
        return pltpu.make_async_copy(src.at[pt_ref[step * nsub + s, p]],
                                     pages_ref.at[into, (c * nsub + s) * npage + p], sem_ref.at[into])

    def all_pages(step, into, op):
        for c in range(2):
            for s in range(nsub):
                for p in range(npage):
                    op(page_copy(step, into, c, s, p))

    @pl.when(n == 0)
    def _():
        all_pages(0, 0, lambda cp: cp.start())

    all_pages(n, slot, lambda cp: cp.wait())
    nxt = jnp.minimum(n + 1, nsteps - 1)
    all_pages(nxt, 1 - slot, lambda cp: cp.start())

    def page(c, s, p):
        return pages_ref.at[slot, (c * nsub + s) * npage + p]

    cmp = []
    for kv in range(2):
        for s in range(nsub):
            for p in range(npage):
                rows = page(0, s, p)[LANES * kv:LANES * (kv + 1), :].T
                for c in range(per_page):
                    r0 = CMP_PITCH * (per_page * p + c)
                    xrow_ref[s, r0:r0 + CMP_STRIDE, :] = rows[CMP_STRIDE * c:CMP_STRIDE * (c + 1)]
        x = jnp.concatenate(
            [jnp.concatenate([xrow_ref[s, pl.ds(i, nch, stride=CMP_PITCH), :] for i in range(CMP_STRIDE)], axis=1)
             for s in range(nsub)], axis=0)
        pp = jnp.dot(x.astype(BF16), wk_ref[kv], preferred_element_type=F32)
        pre = pp[:, 0:LANES] + pltpu.roll(pp[:, LANES:2 * LANES], nsub * nch - 1, axis=0) + bk_ref[kv:kv + 1, :]
        cmp.append(jnp.dot(jax.nn.gelu(pre).astype(BF16), w2k_ref[kv], preferred_element_type=F32).astype(BF16))
    cks = [cmp[0][nch * s:nch * (s + 1)] for s in range(nsub)]
    cvs = [cmp[1][nch * s:nch * (s + 1)] for s in range(nsub)]
    seqs = range(nsub)
    rcat = lambda parts: jnp.concatenate(parts, axis=0)

    lane_w = lax.broadcasted_iota(jnp.int32, (KV_W, LANES), 1)
    lane8 = lax.broadcasted_iota(jnp.int32, (tq, LANES), 1)
    nks_l, nkw_l, wint_l, qs_l = [], [], [], []
    for s in seqs:
        rows_s = slice(tq * s, tq * (s + 1))
        nks_l.append(jnp.concatenate([nks_ref[rows_s, :], jnp.zeros((LANES - tq, KV_W), F32)], axis=0))
        nkw = jnp.concatenate([nkw_ref[rows_s, :], jnp.zeros((LANES - tq, KV_W), F32)], axis=0)
        nkw_l.append(nkw)
        wint = wint_ref[s]
        wint_l.append(wint)
        shifted = pltpu.roll(wint, nwin - tq, axis=1)
        new_t = pltpu.roll(nkw.T, LANES - tq, axis=1)
        nwint_ref[s, :, 0:nwin - LANES] = shifted[:, 0:nwin - LANES]
        nwint_ref[s, :, nwin - LANES:nwin] = jnp.where(lane_w >= LANES - tq, new_t, shifted[:, nwin - LANES:nwin])
        q = q_ref[rows_s, :]
        qrows = []
        for j in range(N_Q_HEADS):
            chunk = q[:, LANES * (j // 2):LANES * (j // 2 + 1)]
            dst = j // GQA
            if (j % 2) != dst:
                chunk = pltpu.roll(chunk, HEAD_DIM, axis=1)
            keep = (lane8 < HEAD_DIM) if dst == 0 else (lane8 >= HEAD_DIM)
            qrows.append(jnp.where(keep, chunk, 0.0))
        qs_l.append(jnp.concatenate(qrows, axis=0).astype(BF16))

    rtot = nsub * nrow
    seq_rows = [slice(nrow * s, nrow * (s + 1)) for s in seqs]
    pos = past_len + (lax.broadcasted_iota(jnp.int32, (rtot, LANES), 0) & (tq - 1))
    lane = lax.broadcasted_iota(jnp.int32, (rtot, LANES), 1)

    sc = rcat([_dot_t(qs_l[s], cks[s]) for s in seqs])
    pc = _softmax_rows(sc, lane * CMP_STRIDE + (CMP_LEN - 1) <= pos).astype(BF16)
    oc = rcat([jnp.dot(pc[seq_rows[s]], cvs[s], preferred_element_type=F32) for s in seqs])
    imp = jnp.dot(pc, ov_ref[...], preferred_element_type=F32)
    vs = []
    for s in seqs:
        for h in range(N_KV_HEADS):
            r0 = nrow * s + tq * GQA * h
            v = imp[r0:r0 + tq]
            for g in range(1, GQA):
                v = v + imp[r0 + tq * g:r0 + tq * (g + 1)]
            vs.append(v)
    nsel = len(vs) * tq
    vt = rcat(vs + [jnp.zeros((LANES - nsel, LANES), F32)]).T
    nblk_pad = -(-nslc // SUBLANES) * SUBLANES
    blk_t = lax.broadcasted_iota(jnp.int32, (nblk_pad, LANES), 0)
    pos_t = past_len + (lax.broadcasted_iota(jnp.int32, (nblk_pad, LANES), 1) & (tq - 1))
    neg_t = _select_blocks(vt[0:nblk_pad], blk_t, pos_t, nslc, axis=0)
    neg = rcat([neg_t, jnp.zeros((LANES - nblk_pad, LANES), F32)]).T
    negsel = rcat([neg[tq * (N_KV_HEADS * s + j // GQA):tq * (N_KV_HEADS * s + j // GQA + 1)]
                   for s in seqs for j in range(N_Q_HEADS)])
    negsel_b = negsel.astype(BF16)

    new_blk = past_len // SLC_BLOCK
    ss_l = []
    for s in seqs:
        qaug = jnp.concatenate([qs_l[s], negsel_b[seq_rows[s]]], axis=1)
        parts = []
        for p in range(0, npage, 2):
            kt = jnp.concatenate([page(1, s, p)[0:LANES, :], page(1, s, p + 1)[0:LANES, :]], axis=1).astype(BF16)
            et = jnp.concatenate([e_ref[p], e_ref[p + 1]], axis=1)
            parts.append(jnp.dot(qaug, jnp.concatenate([kt, et], axis=0), preferred_element_type=F32))
        parts.append(_dot_t(qs_l[s], nks_l[s][:, 0:LANES].astype(BF16)) + negsel[seq_rows[s], new_blk:new_blk + 1])
        ss_l.append(jnp.concatenate(parts, axis=1))
    ss = rcat(ss_l)
    nkeys = ss.shape[1]
    kpos = lax.broadcasted_iota(jnp.int32, (rtot, nkeys), 1)
    pos_k = past_len + (lax.broadcasted_iota(jnp.int32, (rtot, nkeys), 0) & (tq - 1))
    ps = _softmax_rows(ss, kpos <= pos_k).astype(BF16)
    osel_l = []
    for s in seqs:
        psq = ps[seq_rows[s]]
        o = jnp.dot(psq[:, past_len:nkeys], nks_l[s][:, LANES:2 * LANES].astype(BF16), preferred_element_type=F32)
        for p in range(0, npage, 2):
            vtp = jnp.concatenate([page(1, s, p)[LANES:2 * LANES, :], page(1, s, p + 1)[LANES:2 * LANES, :]],
                                  axis=1).astype(BF16)
            o = o + _dot_t(psq[:, PAGE_SIZE * p:PAGE_SIZE * (p + 2)], vtp)
        osel_l.append(o)
    osel = rcat(osel_l)

    sw = rcat([jnp.concatenate([jnp.dot(qs_l[s], wint_l[s][0:LANES].astype(BF16), preferred_element_type=F32),
                                _dot_t(qs_l[s], nkw_l[s][:, 0:LANES].astype(BF16))], axis=1) for s in seqs])
    nw = sw.shape[1]
    widx = lax.broadcasted_iota(jnp.int32, (rtot, nw), 1)
    pos_w = past_len + (lax.broadcasted_iota(jnp.int32, (rtot, nw), 0) & (tq - 1))
    dlt = pos_w - (past_len - nwin + widx)
    pw = _softmax_rows(sw, (dlt >= 0) & (dlt < WINDOW) & (widx < nwin + tq)).astype(BF16)
    ow = rcat([_dot_t(pw[seq_rows[s], 0:nwin], wint_l[s][LANES:2 * LANES].astype(BF16))
               + jnp.dot(pw[seq_rows[s], nwin:nw], nkw_l[s][:, LANES:2 * LANES].astype(BF16),
                         preferred_element_type=F32) for s in seqs])

    for s in seqs:
        rows_s = slice(tq * s, tq * (s + 1))
        gn = gn_ref[rows_s, :]
        for c in range(N_Q_HEADS // 2):
            halves = []
            for hh in range(2):
                j = 2 * c + hh
                rs = slice(nrow * s + tq * j, nrow * s + tq * (j + 1))
                oj = (gn[:, 3 * j:3 * j + 1] * oc[rs] + gn[:, 3 * j + 1:3 * j + 2] * osel[rs]
                      + gn[:, 3 * j + 2:3 * j + 3] * ow[rs])
                if (j // GQA) != hh:
                    oj = pltpu.roll(oj, HEAD_DIM, axis=1)
                halves.append(oj)
            o_ref[rows_s, LANES * c:LANES * (c + 1)] = jnp.where(lane8 < HEAD_DIM, halves[0], halves[1])

    @pl.when(n == nsteps - 1)
    def _():
        all_pages(nxt, 1 - slot, lambda cp: cp.wait())


def _attn_sample(q, gn, nks, nkw, cache_cmp, cache_slc, cache_win, page_table, cp, nseq, tq, past_len):
    assert tq <= CMP_STRIDE and past_len % PAGE_SIZE == 0
    npage = past_len // PAGE_SIZE
    assert npage % 2 == 0 and PAGE_SIZE == LANES
    n_pool = cache_cmp.shape[0]
    nwin = cache_win.shape[1]
    chunks = past_len // CMP_STRIDE
    ov = _overlap_t(chunks, LANES).T
    key = np.arange(past_len).reshape(npage, 1, PAGE_SIZE)
    e = jnp.asarray(np.arange(LANES).reshape(1, LANES, 1) == key // SLC_BLOCK, dtype=BF16)
    to_t = lambda c: jnp.transpose(c, (0, 2, 3, 4, 1)).reshape(c.shape[0], KV_W, c.shape[1])
    cmp_t, slc_t, win_t = to_t(cache_cmp), to_t(cache_slc), to_t(cache_win)
    nsub = SAMPLE_SEQS_PER_STEP
    assert nseq % nsub == 0
    row = lambda n, pt: (n, 0)
    seq3 = lambda n, pt: (n, 0, 0)
    consts = [cp["wk"], cp["bk"], cp["w2k"], ov, e]
    in_specs = [pl.BlockSpec((nsub * tq, Q_W), row), pl.BlockSpec((nsub * tq, LANES), row),
                pl.BlockSpec((nsub * tq, KV_W), row), pl.BlockSpec((nsub * tq, KV_W), row),
                pl.BlockSpec((nsub, KV_W, nwin), seq3)]
    in_specs += [pl.BlockSpec(c.shape, (lambda nd: lambda n, pt: (0,) * nd)(c.ndim)) for c in consts]
    in_specs += [pl.BlockSpec(memory_space=pl.ANY)] * 2
    grid_spec = pltpu.PrefetchScalarGridSpec(
        num_scalar_prefetch=1,
        grid=(nseq // nsub,),
        in_specs=in_specs,
        out_specs=[pl.BlockSpec((nsub * tq, Q_W), row), pl.BlockSpec((nsub, KV_W, nwin), seq3)],
        scratch_shapes=[pltpu.VMEM((nsub, chunks * CMP_PITCH, LANES), F32),
                        pltpu.VMEM((2, 2 * nsub * npage, KV_W, PAGE_SIZE), F32),
                        pltpu.SemaphoreType.DMA((2,))],
    )
    return pl.pallas_call(
        functools.partial(_attn_sample_body, npage=npage, past_len=past_len, nsub=nsub, tq=tq),
        grid_spec=grid_spec,
        out_shape=[jax.ShapeDtypeStruct((nseq * tq, Q_W), F32), jax.ShapeDtypeStruct((nseq, KV_W, nwin), F32)],
        compiler_params=_cparams(("arbitrary",), VMEM_LARGE_MIB),
        name="attn_sample",
    )(page_table, q, gn, nks, nkw, win_t, *consts, cmp_t, slc_t)


def _moe_params(w_rg, b_rg, w_re, b_re, w_gate, w_up, w_down, w_ple, w_ple_gate, gf):
    pad = LANES - N_EXPERT_GROUPS - N_EXPERTS
    return {"wr": jnp.pad(jnp.concatenate([w_rg, w_re], axis=1), ((0, 0), (0, pad))).astype(BF16),
            "br": jnp.pad(jnp.concatenate([b_rg, b_re]), (0, pad)).astype(F32).reshape(1, LANES),
            "wg": w_gate.astype(BF16), "wu": w_up.astype(BF16), "wd": w_down.astype(BF16),
            "wpg": w_ple_gate.astype(BF16), "wp": w_ple.astype(BF16), "gf": gf.astype(F32).reshape(1, D_MODEL)}


TM_PROMPT = 512
TM_MOE = 1024
TC_S5 = 128


def kernel(x_prompt, x_sample, p_prompt, p_sample, cache_cmp_kv, cache_slc_kv, cache_win_kv, state_ssm, page_table, norm1_g, w_in, ssm_lam_re, ssm_lam_im, ssm_log_dt, ssm_b_re, ssm_b_im, ssm_c_re, ssm_c_im, ssm_d, w_glu, b_glu, cmp_pe, cmp_w1, cmp_w2, w_ssm_out, w_nsa_out, w_o, norm2_g, w_route_group, b_route_group, w_route_expert, b_route_expert, w_exp_gate, w_exp_up, w_exp_down, w_ple, w_ple_gate, final_norm_g):
    assert w_in.shape[0] == 1, "one layer"
    l = 0
    nb, t = x_prompt.shape[:2]
    ns, ts = x_sample.shape[:2]
    past_len = page_table.shape[1] * PAGE_SIZE
    kvt = (2, N_KV_HEADS, HEAD_DIM)

    wi = _inproj_params(w_in[l])
    g1 = norm1_g[l].astype(F32).reshape(1, D_MODEL)
    g2 = norm2_g[l].astype(F32).reshape(1, D_MODEL)
    sp = _s5_params(ssm_lam_re[l], ssm_lam_im[l], ssm_log_dt[l], ssm_b_re[l], ssm_b_im[l], ssm_c_re[l], ssm_c_im[l],
                    ssm_d[l])
    cp = _cmp_params(cmp_pe[l], cmp_w1[l], cmp_w2[l])
    mp = _moe_params(w_route_group[l], b_route_group[l], w_route_expert[l], b_route_expert[l], w_exp_gate[l],
                     w_exp_up[l], w_exp_down[l], w_ple[l], w_ple_gate[l], final_norm_g)
    wglu = w_glu[l].astype(BF16)
    bglu = b_glu[l].astype(F32).reshape(1, SSM_WIDTH)
    wso = w_ssm_out[l].astype(BF16)
    wno = w_nsa_out[l].astype(BF16)
    wo = w_o[l].astype(BF16)

    lay = _prompt_layout(nb, t, TM_PROMPT)
    xp = x_prompt.reshape(nb * t, D_MODEL)
    r = _inproj_prompt(xp, lay, g1, wi)
    abr, hlast = _s5_prompt(r["u"], sp, wglu, bglu, wso, t, TC_S5)
    ck, cvt = _compress_prompt(r["kvc"], cp, nb, t)
    onsa = _attn_prompt(r["qt"], r["gnt"], ck, cvt, r["ksb"], r["vst"], r["kwb"], r["vwt"], nb, t)
    x1, h2 = _post(xp, abr, "a", onsa, g1, wi["wgab"], wno, wo, g2, lay, "a")
    y_prompt = _moe(x1, h2, p_prompt[l].reshape(nb * t, PLE_DIM), mp, TM_MOE, 1).reshape(nb, t, D_MODEL)
    keep = min(WINDOW, t)

    def rows_last(a):
        return jnp.transpose(a.reshape((a.shape[0],) + kvt + (a.shape[2],)), (0, 4, 1, 2, 3))[None]

    new_cmp_p = rows_last(r["kvct"])
    new_slc_p = rows_last(r["kvst"])
    new_win_p = rows_last(r["kvwt"][:, :, t - keep:])
    new_ssm_p = jnp.stack([hlast[0:nb], hlast[nb:2 * nb]], axis=-1).reshape(1, nb, N_SSM_GROUPS, SSM_STATE, 2)

    lays = _sample_layout(ns, ts)
    xs = x_sample.reshape(ns * ts, D_MODEL)
    rs = _inproj_sample(xs, lays, g1, wi)
    h0 = state_ssm[l].astype(F32).reshape(ns, N_STATE, 2)
    abr_s, hre, him = _s5_sample(rs["u"], h0[..., 0], h0[..., 1], sp, wglu, bglu, wso, ns, ts)
    onsa_s, new_win = _attn_sample(rs["q"].reshape(ns * ts, Q_W), rs["gn"].reshape(ns * ts, LANES),
                                   rs["kvs"].reshape(ns * ts, KV_W), rs["kvw"].reshape(ns * ts, KV_W),
                                   cache_cmp_kv[l], cache_slc_kv[l], cache_win_kv[l], page_table, cp, ns, ts, past_len)
    x1s, h2s = _post(xs, abr_s, "b", onsa_s, g1, wi["wgab"], wno, wo, g2, lays, "b")
    y_sample = _moe(x1s, h2s, p_sample[l].reshape(ns, ts * PLE_DIM), mp, ns * ts, ts).reshape(ns, ts, D_MODEL)
    steps_first = lambda a: jnp.transpose(a.reshape((ts,) + kvt + (ns,)), (4, 0, 1, 2, 3))[None]
    new_cmp_s = steps_first(rs["kvct"])
    new_slc_s = steps_first(rs["kvst"])
    new_win_s = rows_last(new_win)
    new_ssm_s = jnp.stack([hre, him], axis=-1).reshape(1, ns, N_SSM_GROUPS, SSM_STATE, 2)
    return (y_prompt, y_sample, new_cmp_p, new_slc_p, new_win_p, new_ssm_p,
            new_cmp_s, new_slc_s, new_win_s, new_ssm_s)
```

```python
import functools
import math

import jax
import jax.numpy as jnp
import numpy as np
from jax import lax
from jax.experimental import pallas as pl
from jax.experimental.pallas import tpu as pltpu

F32 = jnp.float32
BF16 = jnp.bfloat16

D_MODEL = 1024
SSM_WIDTH = 512
SSM_GROUP = 16
N_SSM_GROUPS = 32
SSM_STATE = 64
HEAD_DIM = 64
N_Q_HEADS = 8
N_KV_HEADS = 2
GQA = 4
CMP_LEN = 32
CMP_STRIDE = 16
SLC_BLOCK = 64
TOP_N = 8
WINDOW = 512
Q_BLOCK = 256
NEG_INF = -1e30
FORCE_BONUS = 1e4
Q_W = 512
KV_W = 256
NSA_GATE_W = 24
N_EXPERT_GROUPS = 4
EXPERTS_PER_GROUP = 4
N_EXPERTS = 16
D_FF_EXPERT = 256
PLE_DIM = 256
RMS_EPS = 1e-6
PAGE_SIZE = 128

LANES = 128
SUBLANES = 8
N_STATE = N_SSM_GROUPS * SSM_STATE
MIB = 2 ** 20
V7X_VMEM_MIB = 64
VMEM_SMALL_MIB = 48
VMEM_LARGE_MIB = 56
VMEM_MOE_MIB = V7X_VMEM_MIB - 4


def _cparams(sem, vmem_mib):
    return pltpu.CompilerParams(dimension_semantics=sem, vmem_limit_bytes=vmem_mib * MIB)


def _full(shape):
    nd = len(shape)
    return pl.BlockSpec(shape, lambda *_: (0,) * nd)


def _prompt_layout(nseq, t, tm):
    nb = t // tm
    return {
        "grid": (nb, nseq), "tm": tm, "nseq": nseq, "t": t,
        "a": lambda w: ((nseq * t, w), pl.BlockSpec((tm, w), lambda b, s: (s * nb + b, 0))),
    }


def _sample_layout(nseq, t):
    return {
        "grid": (1, t), "tm": nseq,
        "a": lambda w: ((nseq, t * w), pl.BlockSpec((nseq, w), lambda s, b: (0, b))),
        "b": lambda w: ((t * nseq, w), pl.BlockSpec((nseq, w), lambda s, b: (b, 0))),
    }


TK_SLC = 512
TK_WIN = 128


Q_SCALE = HEAD_DIM ** -0.5 * math.log2(math.e)
C_U, C_Q, C_KVC, C_KVS, C_KVW = 0, 512, 1024, 1280, 1536
N_MAIN = 1792


def _dot_t(a, b):
    return lax.dot_general(a, b, (((1,), (1,)), ((), ())), preferred_element_type=F32)


GN_ROWS = 32


def _inproj_prompt_body(x_ref, g_ref, wa_ref, wgn_ref,
                        u_ref, kvc_ref, ksb_ref, kwb_ref,
                        qt_ref, kvct_ref, kvst_ref, kvwt_ref, gnt_ref, vst_ref, vwt_ref, *, nseq):
    s = pl.program_id(1)
    x = x_ref[...]
    inv = lax.rsqrt(jnp.mean(x * x, axis=-1, keepdims=True) + RMS_EPS)
    h = (x * inv * g_ref[...]).astype(BF16)
    tm = h.shape[0]

    def mm(w):
        return jnp.dot(h, w, preferred_element_type=F32)

    u = mm(wa_ref[:, C_U:C_U + SSM_WIDTH])
    for j in range(SSM_WIDTH // LANES):
        u_ref[j, pl.ds(s, tm, stride=nseq), :] = u[:, LANES * j:LANES * (j + 1)]
    kvc = mm(wa_ref[:, C_KVC:C_KVC + KV_W])
    kvc_ref[0] = kvc[:, 0:LANES]
    kvc_ref[1] = kvc[:, LANES:2 * LANES]
    kvct_ref[0] = kvc.T
    kvs = mm(wa_ref[:, C_KVS:C_KVS + KV_W])
    ksb_ref[...] = kvs[:, 0:LANES].astype(BF16)
    kvst = kvs.T
    kvst_ref[0] = kvst
    kvw = mm(wa_ref[:, C_KVW:C_KVW + KV_W])
    kwb_ref[...] = kvw[:, 0:LANES].astype(BF16)
    kvwt = kvw.T
    kvwt_ref[0] = kvwt
    for c in range(tm // TK_WIN):
        vst_ref[0, c] = kvst[LANES:2 * LANES, c * TK_WIN:(c + 1) * TK_WIN].astype(BF16)
        vwt_ref[0, c] = kvwt[LANES:2 * LANES, c * TK_WIN:(c + 1) * TK_WIN].astype(BF16)
    qt_ref[0] = (mm(wa_ref[:, C_Q:C_Q + Q_W]) * Q_SCALE).T.astype(BF16)
    gnt_ref[0] = jax.nn.sigmoid(mm(wgn_ref[...])).T[0:GN_ROWS]


def _inproj_prompt(x2d, lay, g, w):
    tm, nseq, t = lay["tm"], lay["nseq"], lay["t"]
    nb = t // tm
    out_shapes, out_specs, names = [], [], []

    def add(name, shape_spec, dt):
        names.append(name)
        out_shapes.append(jax.ShapeDtypeStruct(shape_spec[0], dt))
        out_specs.append(shape_spec[1])

    def tr(rows):
        return (nseq, rows, t), pl.BlockSpec((1, rows, tm), lambda b, s: (s, 0, b))

    nu = SSM_WIDTH // LANES
    add("u", ((nu, t * nseq, LANES), pl.BlockSpec((nu, tm * nseq, LANES), lambda b, s: (0, b, 0))), F32)
    add("kvc", ((2, nseq * t, LANES), pl.BlockSpec((2, tm, LANES), lambda b, s: (0, s * nb + b, 0))), F32)
    add("ksb", lay["a"](LANES), BF16)
    add("kwb", lay["a"](LANES), BF16)
    add("qt", tr(Q_W), BF16)
    add("kvct", tr(KV_W), F32)
    add("kvst", tr(KV_W), F32)
    add("kvwt", tr(KV_W), F32)
    add("gnt", tr(GN_ROWS), F32)
    for name in ("vst", "vwt"):
        add(name, ((nseq, t // TK_WIN, LANES, TK_WIN),
                   pl.BlockSpec((1, tm // TK_WIN, LANES, TK_WIN), lambda b, s: (s, b, 0, 0))), BF16)
    x_shape, x_spec = lay["a"](D_MODEL)
    ops = [g, w["wa"], w["wgn"]]
    outs = pl.pallas_call(
        functools.partial(_inproj_prompt_body, nseq=nseq),
        grid=lay["grid"],
        in_specs=[x_spec] + [_full(o.shape) for o in ops],
        out_specs=out_specs,
        out_shape=out_shapes,
        compiler_params=_cparams(("arbitrary",) * 2, VMEM_LARGE_MIB),
        name="inproj_prompt",
    )(x2d.reshape(x_shape), *ops)
    return dict(zip(names, outs))


def _inproj_sample_body(x_ref, g_ref, wa_ref, wgn_ref,
                        u_ref, q_ref, kvs_ref, kvw_ref, gn_ref, kvct_ref, kvst_ref, kvwt_ref):
    x = x_ref[...]
    inv = lax.rsqrt(jnp.mean(x * x, axis=-1, keepdims=True) + RMS_EPS)
    h = (x * inv * g_ref[...]).astype(BF16)

    def mm(w):
        return jnp.dot(h, w, preferred_element_type=F32)

    u_ref[...] = mm(wa_ref[:, C_U:C_U + SSM_WIDTH])
    q_ref[...] = mm(wa_ref[:, C_Q:C_Q + Q_W]) * Q_SCALE
    kvs = mm(wa_ref[:, C_KVS:C_KVS + KV_W])
    kvs_ref[...] = kvs
    kvw = mm(wa_ref[:, C_KVW:C_KVW + KV_W])
    kvw_ref[...] = kvw
    gn_ref[...] = jax.nn.sigmoid(mm(wgn_ref[...]))
    kvct_ref[0] = mm(wa_ref[:, C_KVC:C_KVC + KV_W]).T
    kvst_ref[0] = kvs.T
    kvwt_ref[0] = kvw.T


def _inproj_sample(x2d, lay, g, w):
    nseq = lay["tm"]
    ts = lay["grid"][1]
    names = ["u", "q", "kvs", "kvw", "gn"]
    widths = [SSM_WIDTH, Q_W, KV_W, KV_W, LANES]
    out_shapes, out_specs = [], []
    for n, wd in zip(names, widths):
        shp, spec = lay["b" if n == "u" else "a"](wd)
        out_shapes.append(jax.ShapeDtypeStruct(shp, F32))
        out_specs.append(spec)
    for n in ("kvct", "kvst", "kvwt"):
        names.append(n)
        out_shapes.append(jax.ShapeDtypeStruct((ts, KV_W, nseq), F32))
        out_specs.append(pl.BlockSpec((1, KV_W, nseq), lambda s, b: (b, 0, 0)))
    x_shape, x_spec = lay["a"](D_MODEL)
    ops = [g, w["wa"], w["wgn"]]
    outs = pl.pallas_call(
        _inproj_sample_body,
        grid=lay["grid"],
        in_specs=[x_spec] + [_full(o.shape) for o in ops],
        out_specs=out_specs,
        out_shape=out_shapes,
        compiler_params=_cparams(("arbitrary",) * 2, VMEM_LARGE_MIB),
        name="inproj_sample",
    )(x2d.reshape(x_shape), *ops)
    return dict(zip(names, outs))


def _inproj_params(w_in0):
    return {"wa": w_in0[:, :N_MAIN].astype(BF16),
            "wgn": jnp.pad(w_in0[:, N_MAIN:N_MAIN + NSA_GATE_W], ((0, 0), (0, LANES - NSA_GATE_W))).astype(BF16),
            "wgab": w_in0[:, N_MAIN + NSA_GATE_W:].astype(BF16)}


S5_LANES = 1024


def _s5_prompt_body(u_ref, wb_ref, ar_ref, ai_ref, cw_ref, d_ref, wglu_ref, bglu_ref, wso_ref,
                    abr_ref, hlast_ref, lhs_ref, bu_ref, h8_ref, p_ref, hstate_ref):
    c = pl.program_id(0)
    nseq = 4
    r4 = u_ref.shape[1]
    tc = r4 // nseq
    half = tc // 2

    @pl.when(c == 0)
    def _():
        hstate_ref[...] = jnp.zeros_like(hstate_ref)

    u = jnp.concatenate([u_ref[j] for j in range(SSM_WIDTH // LANES)], axis=1)
    row2 = lax.broadcasted_iota(jnp.int32, (r4, SSM_WIDTH), 0)
    lo2 = (row2 % SUBLANES) < nseq
    up = pltpu.roll(u, r4 - nseq, axis=0)
    dn = pltpu.roll(u, nseq, axis=0)
    swapped = jnp.where(lo2, up, dn)
    zero = jnp.zeros_like(u)
    ev_re = jnp.where(lo2, u, zero).astype(BF16).reshape(half, SUBLANES, SSM_WIDTH)
    ev_im = jnp.where(lo2, zero, swapped).astype(BF16).reshape(half, SUBLANES, SSM_WIDTH)
    od_re = jnp.where(lo2, swapped, zero).astype(BF16).reshape(half, SUBLANES, SSM_WIDTH)
    od_im = jnp.where(lo2, zero, u).astype(BF16).reshape(half, SUBLANES, SSM_WIDTH)
    for j in range(4):
        sl = slice(LANES * j, LANES * (j + 1))
        lhs_ref[:, 0:8, 256 * j:256 * j + LANES] = ev_re[:, :, sl]
        lhs_ref[:, 0:8, 256 * j + LANES:256 * (j + 1)] = ev_im[:, :, sl]
        lhs_ref[:, 8:16, 256 * j:256 * j + LANES] = od_re[:, :, sl]
        lhs_ref[:, 8:16, 256 * j + LANES:256 * (j + 1)] = od_im[:, :, sl]
    for j in range(4):
        lhs = lhs_ref[:, :, 256 * j:256 * (j + 1)].reshape(tc * SUBLANES, 256)
        bu_ref[:, 512 * j:512 * (j + 1)] = jnp.dot(lhs, wb_ref[j], preferred_element_type=F32)

    for lc in range(N_STATE // S5_LANES):
        sl = slice(S5_LANES * lc, S5_LANES * (lc + 1))
        ar = ar_ref[:, sl]
        ai = ai_ref[:, sl]

        def step(t, h, sl=sl, ar=ar, ai=ai):
            r0 = pl.multiple_of(t * SUBLANES, SUBLANES)
            h = ar * h + ai * pltpu.roll(h, nseq, axis=0) + bu_ref[pl.ds(r0, SUBLANES), sl]
            h8_ref[pl.ds(r0, SUBLANES), sl] = h
            return h

        hstate_ref[:, sl] = lax.fori_loop(0, tc, step, hstate_ref[:, sl], unroll=8)
    hlast_ref[...] = hstate_ref[...]

    for j in range(4):
        pj = jnp.dot(h8_ref[:, 512 * j:512 * (j + 1)].astype(BF16), cw_ref[j], preferred_element_type=F32)
        p_ref[2 * j] = pj[:, 0:LANES]
        p_ref[2 * j + 1] = pj[:, LANES:2 * LANES]
    ys = []
    for s in range(nseq):
        parts = []
        for j in range(4):
            re = p_ref[2 * j, pl.ds(s, tc, stride=SUBLANES), :]
            im = p_ref[2 * j + 1, pl.ds(nseq + s, tc, stride=SUBLANES), :]
            us = u_ref[j, pl.ds(s, tc, stride=nseq), :]
            parts.append(re + im + d_ref[:, LANES * j:LANES * (j + 1)] * us)
        ys.append(jnp.concatenate(parts, axis=1))
    y = jnp.concatenate(ys, axis=0)
    zg = jax.nn.gelu(y)
    gate = jnp.dot(zg.astype(BF16), wglu_ref[...], preferred_element_type=F32) + bglu_ref[...]
    glu = (zg * jax.nn.sigmoid(gate)).astype(BF16)
    abr = jnp.dot(glu, wso_ref[...], preferred_element_type=F32)
    for s in range(nseq):
        abr_ref[s] = abr[s * tc:(s + 1) * tc].astype(BF16)


def _s5_prompt(u_ts, sp, wglu, bglu, wso, t_total, tc):
    nseq = 4
    grid = (t_total // tc,)
    abr, hlast = pl.pallas_call(
        _s5_prompt_body,
        grid=grid,
        in_specs=[pl.BlockSpec((SSM_WIDTH // LANES, tc * nseq, LANES), lambda c: (0, c, 0)),
                  _full(sp["wb8"].shape), _full(sp["ar8"].shape), _full(sp["ai8"].shape), _full(sp["cw8"].shape),
                  _full(sp["d"].shape), _full(wglu.shape), _full(bglu.shape), _full(wso.shape)],
        out_specs=[pl.BlockSpec((nseq, tc, D_MODEL), lambda c: (0, c, 0)),
                   pl.BlockSpec((SUBLANES, N_STATE), lambda c: (0, 0))],
        out_shape=[jax.ShapeDtypeStruct((nseq, t_total, D_MODEL), BF16),
                   jax.ShapeDtypeStruct((SUBLANES, N_STATE), F32)],
        scratch_shapes=[pltpu.VMEM((tc // 2, 2 * SUBLANES, 1024), BF16),
                        pltpu.VMEM((tc * SUBLANES, N_STATE), F32),
                        pltpu.VMEM((tc * SUBLANES, N_STATE), F32),
                        pltpu.VMEM((8, tc * SUBLANES, LANES), F32),
                        pltpu.VMEM((SUBLANES, N_STATE), F32)],
        compiler_params=_cparams(("arbitrary",), VMEM_LARGE_MIB),
        name="s5_prompt",
    )(u_ts, sp["wb8"], sp["ar8"], sp["ai8"], sp["cw8"], sp["d"], wglu, bglu, wso)
    return abr, hlast


def _s5_params(lam_re, lam_im, log_dt, b_re, b_im, c_re, c_im, d_skip):
    lam = lax.complex(lam_re.astype(F32), lam_im.astype(F32))
    dt = jnp.exp(log_dt.astype(F32))[:, None]
    a_bar = jnp.exp(lam * dt)
    b = lax.complex(b_re.astype(F32), b_im.astype(F32))
    b_bar = ((a_bar - 1.0) / lam)[..., None] * b
    eye8 = jnp.eye(8, dtype=F32)

    def bd_b(m):
        return jnp.einsum("ab,jbpc->jacbp", eye8, m.reshape(4, 8, SSM_STATE, SSM_GROUP)).reshape(4, 128, 512)

    def bd_c(m):
        return jnp.einsum("ab,jbcp->japbc", eye8, m.reshape(4, 8, SSM_GROUP, SSM_STATE)).reshape(4, 512, 128)

    wre, wim = bd_b(b_bar.real), bd_b(b_bar.imag)
    cre, cim = bd_c(c_re.astype(F32)), bd_c(c_im.astype(F32))
    ar = a_bar.real.reshape(1, N_STATE)
    ai = a_bar.imag.reshape(1, N_STATE)
    sign = jnp.concatenate([-jnp.ones((4, 1), F32), jnp.ones((4, 1), F32)], axis=0)
    return {
        "wb8": jnp.concatenate([wre, wim], axis=1).astype(BF16),
        "cw8": jnp.concatenate([cre, -cim], axis=2).astype(BF16),
        "ar8": jnp.broadcast_to(ar, (SUBLANES, N_STATE)),
        "ai8": sign * ai,
        "wre": wre.astype(BF16), "wim": wim.astype(BF16),
        "cre": cre.astype(BF16), "cim": cim.astype(BF16),
        "ar": ar, "ai": ai,
        "d": d_skip.astype(F32).reshape(1, SSM_WIDTH),
    }


def _cmp_params(cmp_pe, cmp_w1, cmp_w2):
    eye2 = jnp.eye(2, dtype=F32)
    nhalf = CMP_LEN // CMP_STRIDE
    w1r = cmp_w1.astype(F32).reshape(2, nhalf, CMP_STRIDE, HEAD_DIM, HEAD_DIM)
    wk = jnp.einsum("kside,ph->kipdshe", w1r, eye2).reshape(2, CMP_STRIDE * LANES, nhalf * LANES)
    bk = jnp.einsum("kld,klde->ke", cmp_pe.astype(F32), cmp_w1.astype(F32), precision=lax.Precision.HIGHEST)
    w2k = jnp.einsum("kef,ph->kpehf", cmp_w2.astype(F32), eye2).reshape(2, LANES, LANES)
    return {"wk": wk.astype(BF16), "bk": jnp.tile(bk, (1, N_KV_HEADS)), "w2k": w2k.astype(BF16),
            "w2kt": jnp.swapaxes(w2k, 1, 2).astype(BF16)}


def _compress_hidden(tap, nch, kv, wk_ref, bk_ref):
    x = jnp.concatenate([tap(i).astype(BF16) for i in range(CMP_STRIDE)], axis=1)
    pp = jnp.dot(x, wk_ref[kv], preferred_element_type=F32)
    pre = pp[:, 0:LANES] + pltpu.roll(pp[:, LANES:2 * LANES], nch - 1, axis=0) + bk_ref[kv:kv + 1, :]
    return jax.nn.gelu(pre).astype(BF16)


def _compress_prompt_body(x_ref, wk_ref, bk_ref, w2k_ref, w2kt_ref, ck_ref, cvt_ref):
    nch = x_ref.shape[1] // CMP_STRIDE
    hid = [_compress_hidden(lambda i, kv=kv: x_ref[kv, pl.ds(i, nch, stride=CMP_STRIDE), :], nch, kv, wk_ref, bk_ref)
           for kv in range(2)]
    ck_ref[0] = jnp.dot(hid[0], w2k_ref[0], preferred_element_type=F32).astype(BF16)
    cvt_ref[0] = _dot_t(w2kt_ref[1], hid[1]).astype(BF16)


def _compress_prompt(kvc2, cp, nseq, t):
    nch = t // CMP_STRIDE
    return pl.pallas_call(
        _compress_prompt_body,
        grid=(nseq,),
        in_specs=[pl.BlockSpec((2, t, LANES), lambda n: (0, n, 0)),
                  _full(cp["wk"].shape), _full(cp["bk"].shape), _full(cp["w2k"].shape), _full(cp["w2kt"].shape)],
        out_specs=[pl.BlockSpec((1, nch, LANES), lambda n: (n, 0, 0)),
                   pl.BlockSpec((1, LANES, nch), lambda n: (n, 0, 0))],
        out_shape=[jax.ShapeDtypeStruct((nseq, nch, LANES), BF16),
                   jax.ShapeDtypeStruct((nseq, LANES, nch), BF16)],
        compiler_params=_cparams(("arbitrary",), VMEM_SMALL_MIB),
        name="compress_prompt",
    )(kvc2, cp["wk"], cp["bk"], cp["w2k"], cp["w2kt"])


def _overlap_t(n_cmp_pad, n_slc_pad):
    j = np.arange(n_cmp_pad)[None, :]
    s = np.arange(n_slc_pad)[:, None]
    ov = (j * CMP_STRIDE <= s * SLC_BLOCK + SLC_BLOCK - 1) & (j * CMP_STRIDE + CMP_LEN - 1 >= s * SLC_BLOCK)
    return jnp.asarray(ov, dtype=BF16)


def _softmax_cols(s, valid):
    sm = jnp.where(valid, s, NEG_INF)
    mx = jnp.max(sm, axis=0, keepdims=True)
    e = jnp.where(valid, jnp.exp2(sm - mx), 0.0)
    l = jnp.sum(e, axis=0, keepdims=True)
    return e * (1.0 / jnp.maximum(l, 1e-30))


def _select_blocks(imp, blk, pos, nblk, axis=0):
    cur = pos // SLC_BLOCK
    forced = (blk == 0) | (blk == cur) | (blk == cur - 1)
    v = jnp.where(forced, imp + FORCE_BONUS, imp)
    v = jnp.where(blk * SLC_BLOCK <= pos, v, NEG_INF)
    v = jnp.where(blk < nblk, v, -3e38)
    blk_f = blk.astype(F32)
    neg = jnp.full(imp.shape, NEG_INF, F32)
    for _ in range(min(TOP_N, nblk)):
        mx = jnp.max(v, axis=axis, keepdims=True)
        first = jnp.min(jnp.where(v == mx, blk_f, float(imp.shape[axis])), axis=axis, keepdims=True)
        pick = blk_f == first
        neg = jnp.where(pick, 0.0, neg)
        v = jnp.where(pick, -3e38, v)
    return neg


CB = 2 * LANES


def _attn_prompt_body(q_ref, gn_ref, ck_ref, cvt_ref, ks_ref, vst_ref, kw_ref, vwt_ref, ovt_ref,
                      o_ref, kaug_ref, kwaug_ref, qaug_ref, acc_ref):
    i = pl.program_id(1)
    t = ks_ref.shape[1]
    nch = ck_ref.shape[1]
    nslc = t // SLC_BLOCK
    qb = Q_BLOCK
    ncol = N_Q_HEADS * qb
    ncb = ncol // CB
    q0 = i * qb
    one_row = 2 * LANES - HEAD_DIM
    hrows = [slice(HEAD_DIM * ((CB * cb // qb) // GQA), HEAD_DIM * ((CB * cb // qb) // GQA + 1)) for cb in range(ncb)]

    @pl.when(i == 0)
    def _():
        kaug_ref[:, 0:LANES] = ks_ref[0]
        blk = lax.broadcasted_iota(jnp.int32, (t, LANES), 0) // SLC_BLOCK
        col = lax.broadcasted_iota(jnp.int32, (t, LANES), 1)
        kaug_ref[:, LANES:2 * LANES] = jnp.where(blk == col, 1.0, 0.0).astype(BF16)
        padcol = lax.broadcasted_iota(jnp.int32, (WINDOW, 2 * LANES), 1)
        kwaug_ref[0:WINDOW, :] = jnp.where(padcol == one_row, NEG_INF, 0.0).astype(BF16)
        kwaug_ref[WINDOW:WINDOW + t, 0:LANES] = kw_ref[0]
        kwaug_ref[WINDOW:WINDOW + t, LANES:2 * LANES] = jnp.zeros((t, LANES), BF16)

    zeros64 = jnp.zeros((HEAD_DIM, qb), BF16)
    for j in range(N_Q_HEADS):
        dst = j // GQA
        qaug_ref[HEAD_DIM * dst:HEAD_DIM * (dst + 1), qb * j:qb * (j + 1)] = q_ref[0, HEAD_DIM * j:HEAD_DIM * (j + 1), :]
        qaug_ref[HEAD_DIM * (1 - dst):HEAD_DIM * (2 - dst), qb * j:qb * (j + 1)] = zeros64
    tail_row = lax.broadcasted_iota(jnp.int32, (HEAD_DIM, ncol), 0)
    qaug_ref[one_row:2 * LANES, :] = jnp.where(tail_row == 0, 1.0, 0.0).astype(BF16)

    pos_c = q0 + (lax.broadcasted_iota(jnp.int32, (nch, CB), 1) & (qb - 1))
    cvalid = lax.broadcasted_iota(jnp.int32, (nch, CB), 0) * CMP_STRIDE + (CMP_LEN - 1) <= pos_c
    scs = [jnp.dot(ck_ref[0], qaug_ref[0:LANES, CB * cb:CB * (cb + 1)], preferred_element_type=F32)
           for cb in range(ncb)]
    pcs = [_softmax_cols(sc, cvalid).astype(BF16) for sc in scs]
    oc = jnp.concatenate([jnp.dot(cvt_ref[0, hrows[cb], :], pc, preferred_element_type=F32)
                          for cb, pc in enumerate(pcs)], axis=1)
    imp = jnp.concatenate([jnp.dot(ovt_ref[...], pc, preferred_element_type=F32) for pc in pcs], axis=1)

    qaug_ref[LANES:one_row, :] = jnp.zeros((one_row - LANES, ncol), BF16)
    q_blocks = qb // TK_WIN
    npiece = (WINDOW + qb) // TK_WIN
    kws = [kwaug_ref[pl.ds(pl.multiple_of(q0 + w * TK_WIN, TK_WIN), TK_WIN), :] for w in range(npiece)]
    vwt = jnp.concatenate([vwt_ref[0, jnp.maximum(i * q_blocks + w - WINDOW // TK_WIN, 0)] for w in range(npiece)],
                          axis=1)
    wrow = lax.broadcasted_iota(jnp.int32, (TK_WIN, CB), 0)
    wcol = lax.broadcasted_iota(jnp.int32, (TK_WIN, CB), 1) & (qb - 1)
    wbias = []
    for w in range(npiece):
        lo, hi = w * TK_WIN - WINDOW, w * TK_WIN - WINDOW + TK_WIN - 1
        if hi <= 0 and qb - 1 - lo < WINDOW:
            wbias.append(None)
        else:
            dlt = wcol - wrow - lo
            wbias.append(jnp.where((dlt >= 0) & (dlt < WINDOW), 0.0, NEG_INF))
    kw_all = jnp.concatenate(kws, axis=0)
    sws = [jnp.dot(kw_all, qaug_ref[:, CB * cb:CB * (cb + 1)], preferred_element_type=F32) for cb in range(ncb)]
    es, rls = [], []
    for s in sws:
        s = jnp.concatenate([s[TK_WIN * w:TK_WIN * (w + 1)] if b is None else s[TK_WIN * w:TK_WIN * (w + 1)] + b
                             for w, b in enumerate(wbias)], axis=0)
        e = jnp.exp2(s - jnp.max(s, axis=0, keepdims=True))
        es.append(e.astype(BF16))
        rls.append(1.0 / jnp.sum(e, axis=0, keepdims=True))
    ow = jnp.concatenate([jnp.dot(vwt[hrows[cb]], e, preferred_element_type=F32) * rl
                          for cb, (e, rl) in enumerate(zip(es, rls))], axis=1)

    blk = lax.broadcasted_iota(jnp.int32, (nslc, N_KV_HEADS * qb), 0)
    pos_q = q0 + (lax.broadcasted_iota(jnp.int32, (nslc, N_KV_HEADS * qb), 1) & (qb - 1))
    vs = []
    for h in range(N_KV_HEADS):
        v = imp[0:nslc, qb * GQA * h:qb * GQA * h + qb]
        for g in range(1, GQA):
            v = v + imp[0:nslc, qb * (GQA * h + g):qb * (GQA * h + g + 1)]
        vs.append(v)
    neg = _select_blocks(jnp.concatenate(vs, axis=1), blk, pos_q, nslc).astype(BF16)
    for j in range(N_Q_HEADS):
        h = j // GQA
        qaug_ref[LANES:LANES + nslc, qb * j:qb * (j + 1)] = neg[:, qb * h:qb * (h + 1)]

    brow = lax.broadcasted_iota(jnp.int32, (qb, CB), 0)
    bcol = lax.broadcasted_iota(jnp.int32, (qb, CB), 1) & (qb - 1)
    tri_lo = jnp.where(brow <= bcol, 0.0, NEG_INF)

    acc_ref[...] = jnp.zeros_like(acc_ref)

    def sel_tile(k0, nk, vt, carry, bias):
        m, l = carry
        ka = kaug_ref[pl.ds(k0, nk), :]
        css = [slice(CB * cb, CB * (cb + 1)) for cb in range(ncb)]
        ss = [jnp.dot(ka, qaug_ref[:, cs], preferred_element_type=F32) for cs in css]
        ms, ls, ps, alphas = [], [], [], []
        for cs, s in zip(css, ss):
            if bias is not None:
                s = s + bias
            mn = jnp.maximum(m[:, cs], jnp.max(s, axis=0, keepdims=True))
            alpha = jnp.exp2(m[:, cs] - mn)
            p = jnp.exp2(s - mn)
            ms.append(mn)
            ls.append(alpha * l[:, cs] + jnp.sum(p, axis=0, keepdims=True))
            ps.append(p.astype(BF16))
            alphas.append(alpha)
        pvs = [jnp.dot(vt[hrows[cb]], p, preferred_element_type=F32) for cb, p in enumerate(ps)]
        for cs, alpha, pv in zip(css, alphas, pvs):
            acc_ref[:, cs] = alpha * acc_ref[:, cs] + pv
        return jnp.concatenate(ms, axis=1), jnp.concatenate(ls, axis=1)

    def vt_blocks(ref, b0, n):
        return jnp.concatenate([ref[0, b0 + j] for j in range(n)], axis=1) if n > 1 else ref[0, b0]

    big_blocks = TK_SLC // TK_WIN

    def big_tile(kt, carry):
        return sel_tile(pl.multiple_of(kt * TK_SLC, TK_SLC), TK_SLC, vt_blocks(vst_ref, kt * big_blocks, big_blocks),
                        carry, None)

    def small_tile(kb, carry):
        return sel_tile(pl.multiple_of(kb * qb, qb), qb, vt_blocks(vst_ref, kb * q_blocks, q_blocks), carry, None)

    carry = (jnp.full((1, ncol), NEG_INF, F32), jnp.zeros((1, ncol), F32))
    nbig = q0 // TK_SLC
    carry = lax.fori_loop(0, nbig // 2, lambda kp, c: big_tile(2 * kp + 1, big_tile(2 * kp, c)), carry)
    carry = lax.fori_loop(nbig - nbig % 2, nbig, big_tile, carry)
    carry = lax.fori_loop(nbig * (TK_SLC // qb), i, small_tile, carry)
    _, l = sel_tile(pl.multiple_of(q0, qb), qb, vt_blocks(vst_ref, i * q_blocks, q_blocks), carry, tri_lo)
    osel = acc_ref[...] * (1.0 / l)

    gt = gn_ref[0]
    for c in range(N_Q_HEADS // 2):
        rows = []
        for hh in range(2):
            j = 2 * c + hh
            cs = slice(qb * j, qb * (j + 1))
            rows.append(gt[3 * j:3 * j + 1, :] * oc[:, cs] + gt[3 * j + 1:3 * j + 2, :] * osel[:, cs]
                        + gt[3 * j + 2:3 * j + 3, :] * ow[:, cs])
        o_ref[:, LANES * c:LANES * (c + 1)] = jnp.concatenate(rows, axis=0).T.astype(o_ref.dtype)


def _attn_prompt(q, gn, ck, cvt, ksb, vst, kwb, vwt, nseq, t):
    nb = t // Q_BLOCK
    nch = t // CMP_STRIDE
    nslc = t // SLC_BLOCK
    ovt = _overlap_t(nch, max(nslc, SUBLANES))
    row = lambda n, i: (n * nb + i, 0)
    seq3 = lambda n, i: (n, 0, 0)
    seq4 = lambda n, i: (n, 0, 0, 0)
    col3 = lambda n, i: (n, 0, i)
    return pl.pallas_call(
        _attn_prompt_body,
        grid=(nseq, nb),
        in_specs=[pl.BlockSpec((1, Q_W, Q_BLOCK), col3), pl.BlockSpec((1, gn.shape[1], Q_BLOCK), col3),
                  pl.BlockSpec((1, nch, LANES), seq3), pl.BlockSpec((1, LANES, nch), seq3),
                  pl.BlockSpec((1, t, LANES), seq3), pl.BlockSpec((1, t // TK_WIN, LANES, TK_WIN), seq4),
                  pl.BlockSpec((1, t, LANES), seq3), pl.BlockSpec((1, t // TK_WIN, LANES, TK_WIN), seq4),
                  _full(ovt.shape)],
        out_specs=pl.BlockSpec((Q_BLOCK, Q_W), row),
        out_shape=jax.ShapeDtypeStruct((nseq * t, Q_W), BF16),
        scratch_shapes=[pltpu.VMEM((t, 2 * LANES), BF16),
                        pltpu.VMEM((WINDOW + t, 2 * LANES), BF16),
                        pltpu.VMEM((2 * LANES, N_Q_HEADS * Q_BLOCK), BF16),
                        pltpu.VMEM((HEAD_DIM, N_Q_HEADS * Q_BLOCK), F32)],
        compiler_params=_cparams(("arbitrary", "arbitrary"), VMEM_LARGE_MIB),
        name="attn_prompt",
    )(q, gn, ck, cvt, ksb.reshape(nseq, t, LANES), vst, kwb.reshape(nseq, t, LANES), vwt, ovt)


def _post_body(x_ref, abr_ref, on_ref, g1_ref, wgab_ref, wno_ref, wo_ref, g2_ref, wr_ref, br_ref,
               x1_ref, h2_ref, comb_ref):
    x = x_ref[...]
    inv1 = lax.rsqrt(jnp.mean(x * x, axis=-1, keepdims=True) + RMS_EPS)
    h = (x * inv1 * g1_ref[...]).astype(BF16)
    ga = jax.nn.sigmoid(jnp.dot(h, wgab_ref[:, 0:D_MODEL], preferred_element_type=F32))
    gb = jax.nn.sigmoid(jnp.dot(h, wgab_ref[:, D_MODEL:2 * D_MODEL], preferred_element_type=F32))
    bbr = jnp.dot(on_ref[...].astype(BF16), wno_ref[...], preferred_element_type=F32)
    merged = ga * abr_ref[...].astype(F32) + gb * bbr
    x1 = x + jnp.dot(merged.astype(BF16), wo_ref[...], preferred_element_type=F32)
    x1_ref[...] = x1
    inv = lax.rsqrt(jnp.mean(x1 * x1, axis=-1, keepdims=True) + RMS_EPS)
    h2 = (x1 * inv * g2_ref[...]).astype(BF16)
    h2_ref[...] = h2
    comb_ref[...] = _route(jnp.dot(h2, wr_ref[...], preferred_element_type=F32) + br_ref[...])


def _post(x2d, abr, abr_lay, onsa, g1, wgab, wno, wo, g2, wr, br, lay, out_lay):
    x_shape, x_spec = lay["a"](D_MODEL)
    abr_shape, abr_spec = lay[abr_lay](D_MODEL)
    on_shape, on_spec = lay["a"](Q_W)
    o_shape, o_spec = lay[out_lay](D_MODEL)
    c_shape, c_spec = lay[out_lay](LANES)
    return pl.pallas_call(
        _post_body,
        grid=lay["grid"],
        in_specs=[x_spec, abr_spec, on_spec, _full(g1.shape), _full(wgab.shape), _full(wno.shape), _full(wo.shape),
                  _full(g2.shape), _full(wr.shape), _full(br.shape)],
        out_specs=[o_spec, o_spec, c_spec],
        out_shape=[jax.ShapeDtypeStruct(o_shape, F32), jax.ShapeDtypeStruct(o_shape, BF16),
                   jax.ShapeDtypeStruct(c_shape, F32)],
        compiler_params=_cparams(("arbitrary",) * len(lay["grid"]), VMEM_SMALL_MIB),
        name="post",
    )(x2d.reshape(x_shape), abr.reshape(abr_shape), onsa.reshape(on_shape), g1, wgab, wno, wo, g2, wr, br)


def _route(logits):
    lane = lax.broadcasted_iota(jnp.int32, logits.shape, 1).astype(F32)
    big = float(LANES)
    glog = jnp.where(lane < N_EXPERT_GROUPS, logits, -jnp.inf)
    gmax = jnp.max(glog, axis=1, keepdims=True)
    gsel = jnp.min(jnp.where(glog == gmax, lane, big), axis=1, keepdims=True)
    gw = 1.0 / jnp.sum(jnp.exp(glog - gmax), axis=1, keepdims=True)
    lo = N_EXPERT_GROUPS + EXPERTS_PER_GROUP * gsel
    el = jnp.where((lane >= lo) & (lane < lo + EXPERTS_PER_GROUP), logits, -jnp.inf)
    v1 = jnp.max(el, axis=1, keepdims=True)
    i1 = jnp.min(jnp.where(el == v1, lane, big), axis=1, keepdims=True)
    el2 = jnp.where(lane == i1, -jnp.inf, el)
    v2 = jnp.max(el2, axis=1, keepdims=True)
    i2 = jnp.min(jnp.where(el2 == v2, lane, big), axis=1, keepdims=True)
    e2 = jnp.exp(v2 - v1)
    w1 = gw / (1.0 + e2)
    return jnp.where(lane == i1, w1, 0.0) + jnp.where(lane == i2, w1 * e2, 0.0)


def _moe_body(x1_ref, h2_ref, comb_ref, p_ref, wg_ref, wu_ref, wd_ref, wpg_ref, wp_ref, gf_ref,
              y_ref, acc_ref, *, tsplit):
    g = pl.program_id(1)
    h2 = h2_ref[...]

    @pl.when(g == 0)
    def _():
        acc_ref[...] = jnp.zeros_like(acc_ref)

    comb = comb_ref[...]
    lane = lax.broadcasted_iota(jnp.int32, comb.shape, 1)
    acc = acc_ref[...]
    for k in range(EXPERTS_PER_GROUP):
        e_lane = N_EXPERT_GROUPS + EXPERTS_PER_GROUP * g + k
        ce = jnp.sum(jnp.where(lane == e_lane, comb, 0.0), axis=1, keepdims=True)
        a = jnp.dot(h2, wg_ref[k], preferred_element_type=F32)
        b = jnp.dot(h2, wu_ref[k], preferred_element_type=F32)
        act = (jax.nn.silu(a) * b * ce).astype(BF16)
        acc = acc + jnp.dot(act, wd_ref[k], preferred_element_type=F32)
    acc_ref[...] = acc

    @pl.when(g == N_EXPERT_GROUPS - 1)
    def _():
        x2 = x1_ref[...] + acc_ref[...]
        rows = x2.shape[0] // tsplit
        if tsplit == 1:
            p = p_ref[...]
        else:
            p = jnp.concatenate([p_ref[:, PLE_DIM * t:PLE_DIM * (t + 1)] for t in range(tsplit)], axis=0)
        gate = jax.nn.sigmoid(jnp.dot(x2.astype(BF16), wpg_ref[...], preferred_element_type=F32))
        x3 = x2 + gate * jnp.dot(p.astype(BF16), wp_ref[...], preferred_element_type=F32)
        inv = lax.rsqrt(jnp.mean(x3 * x3, axis=-1, keepdims=True) + RMS_EPS)
        y = x3 * inv * gf_ref[...]
        if tsplit == 1:
            y_ref[...] = y
        else:
            for t in range(tsplit):
                y_ref[:, D_MODEL * t:D_MODEL * (t + 1)] = y[rows * t:rows * (t + 1)]


def _moe(x1, h2, comb, p, mp, tm, tsplit):
    rows = x1.shape[0]
    nrb = rows // tm
    rb = lambda r, g: (r, 0)
    grp = lambda r, g: (g, 0, 0)
    if tsplit == 1:
        p_spec = pl.BlockSpec((tm, PLE_DIM), rb)
        y_spec = pl.BlockSpec((tm, D_MODEL), rb)
        y_shape = (rows, D_MODEL)
    else:
        assert nrb == 1
        p_spec = _full(p.shape)
        y_shape = (rows // tsplit, tsplit * D_MODEL)
        y_spec = _full(y_shape)
    return pl.pallas_call(
        functools.partial(_moe_body, tsplit=tsplit),
        grid=(nrb, N_EXPERT_GROUPS),
        in_specs=[pl.BlockSpec((tm, D_MODEL), rb), pl.BlockSpec((tm, D_MODEL), rb), pl.BlockSpec((tm, LANES), rb), p_spec,
                  pl.BlockSpec((EXPERTS_PER_GROUP, D_MODEL, D_FF_EXPERT), grp),
                  pl.BlockSpec((EXPERTS_PER_GROUP, D_MODEL, D_FF_EXPERT), grp),
                  pl.BlockSpec((EXPERTS_PER_GROUP, D_FF_EXPERT, D_MODEL), grp),
                  _full(mp["wpg"].shape), _full(mp["wp"].shape), _full(mp["gf"].shape)],
        out_specs=y_spec,
        out_shape=jax.ShapeDtypeStruct(y_shape, F32),
        scratch_shapes=[pltpu.VMEM((tm, D_MODEL), F32)],
        compiler_params=_cparams(("arbitrary", "arbitrary"), VMEM_MOE_MIB),
        name="moe_ple",
    )(x1, h2, comb, p, mp["wg"], mp["wu"], mp["wd"], mp["wpg"], mp["wp"], mp["gf"])


def _s5_sample_body(u_ref, h0re_ref, h0im_ref, wre_ref, wim_ref, ar_ref, ai_ref, cre_ref, cim_ref, d_ref,
                    wglu_ref, bglu_ref, wso_ref, abr_ref, hre_out_ref, him_out_ref,
                    bure_ref, buim_ref, hre_ref, him_ref, *, nseq, nstep):
    u = u_ref[...]
    ub = u.astype(BF16)
    for j in range(4):
        lhs = ub[:, LANES * j:LANES * (j + 1)]
        bure_ref[:, 512 * j:512 * (j + 1)] = jnp.dot(lhs, wre_ref[j], preferred_element_type=F32)
        buim_ref[:, 512 * j:512 * (j + 1)] = jnp.dot(lhs, wim_ref[j], preferred_element_type=F32)
    for lc in range(4):
        sl = slice(512 * lc, 512 * (lc + 1))
        ar = jnp.broadcast_to(ar_ref[:, sl], (SUBLANES, 512))
        ai = jnp.broadcast_to(ai_ref[:, sl], (SUBLANES, 512))

        def body(rc, carry, sl=sl, ar=ar, ai=ai):
            r0 = pl.multiple_of(rc * SUBLANES, SUBLANES)
            hr = h0re_ref[pl.ds(r0, SUBLANES), sl]
            hi = h0im_ref[pl.ds(r0, SUBLANES), sl]
            for t in range(nstep):
                rr = pl.multiple_of(t * nseq + rc * SUBLANES, SUBLANES)
                hr, hi = (ar * hr - ai * hi + bure_ref[pl.ds(rr, SUBLANES), sl],
                          ar * hi + ai * hr + buim_ref[pl.ds(rr, SUBLANES), sl])
                hre_ref[pl.ds(rr, SUBLANES), sl] = hr
                him_ref[pl.ds(rr, SUBLANES), sl] = hi
            hre_out_ref[pl.ds(r0, SUBLANES), sl] = hr
            him_out_ref[pl.ds(r0, SUBLANES), sl] = hi
            return carry

        lax.fori_loop(0, nseq // SUBLANES, body, 0)
    parts = []
    for j in range(4):
        sl = slice(512 * j, 512 * (j + 1))
        parts.append(jnp.dot(hre_ref[:, sl].astype(BF16), cre_ref[j], preferred_element_type=F32)
                     - jnp.dot(him_ref[:, sl].astype(BF16), cim_ref[j], preferred_element_type=F32))
    y = jnp.concatenate(parts, axis=1) + d_ref[...] * u
    zg = jax.nn.gelu(y)
    gate = jnp.dot(zg.astype(BF16), wglu_ref[...], preferred_element_type=F32) + bglu_ref[...]
    glu = (zg * jax.nn.sigmoid(gate)).astype(BF16)
    abr_ref[...] = jnp.dot(glu, wso_ref[...], preferred_element_type=F32).astype(BF16)


def _s5_sample(u_ts, h0re, h0im, sp, wglu, bglu, wso, nseq, nstep):
    rows = nseq * nstep
    ops = [u_ts, h0re, h0im, sp["wre"], sp["wim"], sp["ar"], sp["ai"], sp["cre"], sp["cim"], sp["d"], wglu, bglu, wso]
    return pl.pallas_call(
        functools.partial(_s5_sample_body, nseq=nseq, nstep=nstep),
        grid=(1,),
        in_specs=[_full(o.shape) for o in ops],
        out_specs=[_full((rows, D_MODEL)), _full((nseq, N_STATE)), _full((nseq, N_STATE))],
        out_shape=[jax.ShapeDtypeStruct((rows, D_MODEL), BF16),
                   jax.ShapeDtypeStruct((nseq, N_STATE), F32), jax.ShapeDtypeStruct((nseq, N_STATE), F32)],
        scratch_shapes=[pltpu.VMEM((rows, N_STATE), F32) for _ in range(4)],
        compiler_params=_cparams(("arbitrary",), VMEM_LARGE_MIB),
        name="s5_sample",
    )(*ops)


def _softmax_rows(s, valid):
    sm = jnp.where(valid, s, NEG_INF)
    mx = jnp.max(sm, axis=1, keepdims=True)
    e = jnp.where(valid, jnp.exp2(sm - mx), 0.0)
    l = jnp.sum(e, axis=1, keepdims=True)
    return e * (1.0 / jnp.maximum(l, 1e-30))


SAMPLE_SEQS_PER_STEP = 4
CMP_PITCH = 24


def _attn_sample_body(pt_ref, q_ref, gn_ref, nks_ref, nkw_ref, wint_ref, wk_ref, bk_ref, w2k_ref, ov_ref, e_ref,
                      cmp_hbm, slc_hbm, o_ref, nwint_ref, xrow_ref, pages_ref, sem_ref, *, npage, past_len, nsub, tq):
    n = pl.program_id(0)
    nsteps = pl.num_programs(0)
    slot = lax.rem(n, 2)
    nrow = N_Q_HEADS * tq
    nwin = wint_ref.shape[2]
    nslc = -(-(past_len + tq) // SLC_BLOCK)
    nch = past_len // CMP_STRIDE
    per_page = PAGE_SIZE // CMP_STRIDE

    def page_copy(step, into, c, s, p):
        src = (cmp_hbm, slc_hbm)[c]
        return pltpu.make_async_copy(src.at[pt_ref[step * nsub + s, p]],
                                     pages_ref.at[into, (c * nsub + s) * npage + p], sem_ref.at[into])

    def all_pages(step, into, op):
        for c in range(2):
            for s in range(nsub):
                for p in range(npage):
                    op(page_copy(step, into, c, s, p))

    @pl.when(n == 0)
    def _():
        all_pages(0, 0, lambda cp: cp.start())

    all_pages(n, slot, lambda cp: cp.wait())
    nxt = jnp.minimum(n + 1, nsteps - 1)
    all_pages(nxt, 1 - slot, lambda cp: cp.start())

    def page(c, s, p):
        return pages_ref.at[slot, (c * nsub + s) * npage + p]

    cmp = []
    for kv in range(2):
        for s in range(nsub):
            for p in range(npage):
                rows = page(0, s, p)[LANES * kv:LANES * (kv + 1), :].T
                for c in range(per_page):
                    r0 = CMP_PITCH * (per_page * p + c)
                    xrow_ref[s, r0:r0 + CMP_STRIDE, :] = rows[CMP_STRIDE * c:CMP_STRIDE * (c + 1)]
        x = jnp.concatenate(
            [jnp.concatenate([xrow_ref[s, pl.ds(i, nch, stride=CMP_PITCH), :] for i in range(CMP_STRIDE)], axis=1)
             for s in range(nsub)], axis=0)
        pp = jnp.dot(x.astype(BF16), wk_ref[kv], preferred_element_type=F32)
        pre = pp[:, 0:LANES] + pltpu.roll(pp[:, LANES:2 * LANES], nsub * nch - 1, axis=0) + bk_ref[kv:kv + 1, :]
        cmp.append(jnp.dot(jax.nn.gelu(pre).astype(BF16), w2k_ref[kv], preferred_element_type=F32).astype(BF16))
    cks = [cmp[0][nch * s:nch * (s + 1)] for s in range(nsub)]
    cvs = [cmp[1][nch * s:nch * (s + 1)] for s in range(nsub)]
    seqs = range(nsub)
    rcat = lambda parts: jnp.concatenate(parts, axis=0)

    lane_w = lax.broadcasted_iota(jnp.int32, (KV_W, LANES), 1)
    lane8 = lax.broadcasted_iota(jnp.int32, (tq, LANES), 1)
    nks_l, nkw_l, wint_l, qs_l = [], [], [], []
    for s in seqs:
        rows_s = slice(tq * s, tq * (s + 1))
        nks_l.append(jnp.concatenate([nks_ref[rows_s, :], jnp.zeros((LANES - tq, KV_W), F32)], axis=0))
        nkw = jnp.concatenate([nkw_ref[rows_s, :], jnp.zeros((LANES - tq, KV_W), F32)], axis=0)
        nkw_l.append(nkw)
        wint = wint_ref[s]
        wint_l.append(wint)
        shifted = pltpu.roll(wint, nwin - tq, axis=1)
        new_t = pltpu.roll(nkw.T, LANES - tq, axis=1)
        nwint_ref[s, :, 0:nwin - LANES] = shifted[:, 0:nwin - LANES]
        nwint_ref[s, :, nwin - LANES:nwin] = jnp.where(lane_w >= LANES - tq, new_t, shifted[:, nwin - LANES:nwin])
        q = q_ref[rows_s, :]
        qrows = []
        for j in range(N_Q_HEADS):
            chunk = q[:, LANES * (j // 2):LANES * (j // 2 + 1)]
            dst = j // GQA
            if (j % 2) != dst:
                chunk = pltpu.roll(chunk, HEAD_DIM, axis=1)
            keep = (lane8 < HEAD_DIM) if dst == 0 else (lane8 >= HEAD_DIM)
            qrows.append(jnp.where(keep, chunk, 0.0))
        qs_l.append(jnp.concatenate(qrows, axis=0).astype(BF16))

    rtot = nsub * nrow
    seq_rows = [slice(nrow * s, nrow * (s + 1)) for s in seqs]
    pos = past_len + (lax.broadcasted_iota(jnp.int32, (rtot, LANES), 0) & (tq - 1))
    lane = lax.broadcasted_iota(jnp.int32, (rtot, LANES), 1)

    sc = rcat([_dot_t(qs_l[s], cks[s]) for s in seqs])
    pc = _softmax_rows(sc, lane * CMP_STRIDE + (CMP_LEN - 1) <= pos).astype(BF16)
    oc = rcat([jnp.dot(pc[seq_rows[s]], cvs[s], preferred_element_type=F32) for s in seqs])
    imp = jnp.dot(pc, ov_ref[...], preferred_element_type=F32)
    vs = []
    for s in seqs:
        for h in range(N_KV_HEADS):
            r0 = nrow * s + tq * GQA * h
            v = imp[r0:r0 + tq]
            for g in range(1, GQA):
                v = v + imp[r0 + tq * g:r0 + tq * (g + 1)]
            vs.append(v)
    nsel = len(vs) * tq
    vt = rcat(vs + [jnp.zeros((LANES - nsel, LANES), F32)]).T
    nblk_pad = -(-nslc // SUBLANES) * SUBLANES
    blk_t = lax.broadcasted_iota(jnp.int32, (nblk_pad, LANES), 0)
    pos_t = past_len + (lax.broadcasted_iota(jnp.int32, (nblk_pad, LANES), 1) & (tq - 1))
    neg_t = _select_blocks(vt[0:nblk_pad], blk_t, pos_t, nslc, axis=0)
    neg = rcat([neg_t, jnp.zeros((LANES - nblk_pad, LANES), F32)]).T
    negsel = rcat([neg[tq * (N_KV_HEADS * s + j // GQA):tq * (N_KV_HEADS * s + j // GQA + 1)]
                   for s in seqs for j in range(N_Q_HEADS)])
    negsel_b = negsel.astype(BF16)

    new_blk = past_len // SLC_BLOCK
    ss_l = []
    for s in seqs:
        qaug = jnp.concatenate([qs_l[s], negsel_b[seq_rows[s]]], axis=1)
        parts = []
        for p in range(0, npage, 2):
            kt = jnp.concatenate([page(1, s, p)[0:LANES, :], page(1, s, p + 1)[0:LANES, :]], axis=1).astype(BF16)
            et = jnp.concatenate([e_ref[p], e_ref[p + 1]], axis=1)
            parts.append(jnp.dot(qaug, jnp.concatenate([kt, et], axis=0), preferred_element_type=F32))
        parts.append(_dot_t(qs_l[s], nks_l[s][:, 0:LANES].astype(BF16)) + negsel[seq_rows[s], new_blk:new_blk + 1])
        ss_l.append(jnp.concatenate(parts, axis=1))
    ss = rcat(ss_l)
    nkeys = ss.shape[1]
    kpos = lax.broadcasted_iota(jnp.int32, (rtot, nkeys), 1)
    pos_k = past_len + (lax.broadcasted_iota(jnp.int32, (rtot, nkeys), 0) & (tq - 1))
    ps = _softmax_rows(ss, kpos <= pos_k).astype(BF16)
    osel_l = []
    for s in seqs:
        psq = ps[seq_rows[s]]
        o = jnp.dot(psq[:, past_len:nkeys], nks_l[s][:, LANES:2 * LANES].astype(BF16), preferred_element_type=F32)
        for p in range(0, npage, 2):
            vtp = jnp.concatenate([page(1, s, p)[LANES:2 * LANES, :], page(1, s, p + 1)[LANES:2 * LANES, :]],
                                  axis=1).astype(BF16)
            o = o + _dot_t(psq[:, PAGE_SIZE * p:PAGE_SIZE * (p + 2)], vtp)
        osel_l.append(o)
    osel = rcat(osel_l)

    sw = rcat([jnp.concatenate([jnp.dot(qs_l[s], wint_l[s][0:LANES].astype(BF16), preferred_element_type=F32),
                                _dot_t(qs_l[s], nkw_l[s][:, 0:LANES].astype(BF16))], axis=1) for s in seqs])
    nw = sw.shape[1]
    widx = lax.broadcasted_iota(jnp.int32, (rtot, nw), 1)
    pos_w = past_len + (lax.broadcasted_iota(jnp.int32, (rtot, nw), 0) & (tq - 1))
    dlt = pos_w - (past_len - nwin + widx)
    pw = _softmax_rows(sw, (dlt >= 0) & (dlt < WINDOW) & (widx < nwin + tq)).astype(BF16)
    ow = rcat([_dot_t(pw[seq_rows[s], 0:nwin], wint_l[s][LANES:2 * LANES].astype(BF16))
               + jnp.dot(pw[seq_rows[s], nwin:nw], nkw_l[s][:, LANES:2 * LANES].astype(BF16),
                         preferred_element_type=F32) for s in seqs])

    for s in seqs:
        rows_s = slice(tq * s, tq * (s + 1))
        gn = gn_ref[rows_s, :]
        for c in range(N_Q_HEADS // 2):
            halves = []
            for hh in range(2):
                j = 2 * c + hh
                rs = slice(nrow * s + tq * j, nrow * s + tq * (j + 1))
                oj = (gn[:, 3 * j:3 * j + 1] * oc[rs] + gn[:, 3 * j + 1:3 * j + 2] * osel[rs]
                      + gn[:, 3 * j + 2:3 * j + 3] * ow[rs])
                if (j // GQA) != hh:
                    oj = pltpu.roll(oj, HEAD_DIM, axis=1)
                halves.append(oj)
            o_ref[rows_s, LANES * c:LANES * (c + 1)] = jnp.where(lane8 < HEAD_DIM, halves[0], halves[1])

    @pl.when(n == nsteps - 1)
    def _():
        all_pages(nxt, 1 - slot, lambda cp: cp.wait())


def _attn_sample(q, gn, nks, nkw, cache_cmp, cache_slc, cache_win, page_table, cp, nseq, tq, past_len):
    assert tq <= CMP_STRIDE and past_len % PAGE_SIZE == 0
    npage = past_len // PAGE_SIZE
    assert npage % 2 == 0 and PAGE_SIZE == LANES
    n_pool = cache_cmp.shape[0]
    nwin = cache_win.shape[1]
    chunks = past_len // CMP_STRIDE
    ov = _overlap_t(chunks, LANES).T
    key = np.arange(past_len).reshape(npage, 1, PAGE_SIZE)
    e = jnp.asarray(np.arange(LANES).reshape(1, LANES, 1) == key // SLC_BLOCK, dtype=BF16)
    to_t = lambda c: jnp.transpose(c, (0, 2, 3, 4, 1)).reshape(c.shape[0], KV_W, c.shape[1])
    cmp_t, slc_t, win_t = to_t(cache_cmp), to_t(cache_slc), to_t(cache_win)
    nsub = SAMPLE_SEQS_PER_STEP
    assert nseq % nsub == 0
    row = lambda n, pt: (n, 0)
    seq3 = lambda n, pt: (n, 0, 0)
    consts = [cp["wk"], cp["bk"], cp["w2k"], ov, e]
    in_specs = [pl.BlockSpec((nsub * tq, Q_W), row), pl.BlockSpec((nsub * tq, LANES), row),
                pl.BlockSpec((nsub * tq, KV_W), row), pl.BlockSpec((nsub * tq, KV_W), row),
                pl.BlockSpec((nsub, KV_W, nwin), seq3)]
    in_specs += [pl.BlockSpec(c.shape, (lambda nd: lambda n, pt: (0,) * nd)(c.ndim)) for c in consts]
    in_specs += [pl.BlockSpec(memory_space=pl.ANY)] * 2
    grid_spec = pltpu.PrefetchScalarGridSpec(
        num_scalar_prefetch=1,
        grid=(nseq // nsub,),
        in_specs=in_specs,
        out_specs=[pl.BlockSpec((nsub * tq, Q_W), row), pl.BlockSpec((nsub, KV_W, nwin), seq3)],
        scratch_shapes=[pltpu.VMEM((nsub, chunks * CMP_PITCH, LANES), F32),
                        pltpu.VMEM((2, 2 * nsub * npage, KV_W, PAGE_SIZE), F32),
                        pltpu.SemaphoreType.DMA((2,))],
    )
    return pl.pallas_call(
        functools.partial(_attn_sample_body, npage=npage, past_len=past_len, nsub=nsub, tq=tq),
        grid_spec=grid_spec,
        out_shape=[jax.ShapeDtypeStruct((nseq * tq, Q_W), F32), jax.ShapeDtypeStruct((nseq, KV_W, nwin), F32)],
        compiler_params=_cparams(("arbitrary",), VMEM_LARGE_MIB),
        name="attn_sample",
    )(page_table, q, gn, nks, nkw, win_t, *consts, cmp_t, slc_t)


def _moe_params(w_rg, b_rg, w_re, b_re, w_gate, w_up, w_down, w_ple, w_ple_gate, gf):
    pad = LANES - N_EXPERT_GROUPS - N_EXPERTS
    return {"wr": jnp.pad(jnp.concatenate([w_rg, w_re], axis=1), ((0, 0), (0, pad))).astype(BF16),
            "br": jnp.pad(jnp.concatenate([b_rg, b_re]), (0, pad)).astype(F32).reshape(1, LANES),
            "wg": w_gate.astype(BF16), "wu": w_up.astype(BF16), "wd": w_down.astype(BF16),
            "wpg": w_ple_gate.astype(BF16), "wp": w_ple.astype(BF16), "gf": gf.astype(F32).reshape(1, D_MODEL)}


TM_PROMPT = 512
TM_MOE = 1024
TC_S5 = 128


def kernel(x_prompt, x_sample, p_prompt, p_sample, cache_cmp_kv, cache_slc_kv, cache_win_kv, state_ssm, page_table, norm1_g, w_in, ssm_lam_re, ssm_lam_im, ssm_log_dt, ssm_b_re, ssm_b_im, ssm_c_re, ssm_c_im, ssm_d, w_glu, b_glu, cmp_pe, cmp_w1, cmp_w2, w_ssm_out, w_nsa_out, w_o, norm2_g, w_route_group, b_route_group, w_route_expert, b_route_expert, w_exp_gate, w_exp_up, w_exp_down, w_ple, w_ple_gate, final_norm_g):
    assert w_in.shape[0] == 1, "one layer"
    l = 0
    nb, t = x_prompt.shape[:2]
    ns, ts = x_sample.shape[:2]
    past_len = page_table.shape[1] * PAGE_SIZE
    kvt = (2, N_KV_HEADS, HEAD_DIM)

    wi = _inproj_params(w_in[l])
    g1 = norm1_g[l].astype(F32).reshape(1, D_MODEL)
    g2 = norm2_g[l].astype(F32).reshape(1, D_MODEL)
    sp = _s5_params(ssm_lam_re[l], ssm_lam_im[l], ssm_log_dt[l], ssm_b_re[l], ssm_b_im[l], ssm_c_re[l], ssm_c_im[l],
                    ssm_d[l])
    cp = _cmp_params(cmp_pe[l], cmp_w1[l], cmp_w2[l])
    mp = _moe_params(w_route_group[l], b_route_group[l], w_route_expert[l], b_route_expert[l], w_exp_gate[l],
                     w_exp_up[l], w_exp_down[l], w_ple[l], w_ple_gate[l], final_norm_g)
    wglu = w_glu[l].astype(BF16)
    bglu = b_glu[l].astype(F32).reshape(1, SSM_WIDTH)
    wso = w_ssm_out[l].astype(BF16)
    wno = w_nsa_out[l].astype(BF16)
    wo = w_o[l].astype(BF16)

    lay = _prompt_layout(nb, t, TM_PROMPT)
    xp = x_prompt.reshape(nb * t, D_MODEL)
    r = _inproj_prompt(xp, lay, g1, wi)
    abr, hlast = _s5_prompt(r["u"], sp, wglu, bglu, wso, t, TC_S5)
    ck, cvt = _compress_prompt(r["kvc"], cp, nb, t)
    onsa = _attn_prompt(r["qt"], r["gnt"], ck, cvt, r["ksb"], r["vst"], r["kwb"], r["vwt"], nb, t)
    x1, h2, comb = _post(xp, abr, "a", onsa, g1, wi["wgab"], wno, wo, g2, mp["wr"], mp["br"],
                         _prompt_layout(nb, t, TM_MOE), "a")
    y_prompt = _moe(x1, h2, comb, p_prompt[l].reshape(nb * t, PLE_DIM), mp, TM_MOE, 1).reshape(nb, t, D_MODEL)
    keep = min(WINDOW, t)

    def rows_last(a):
        return jnp.transpose(a.reshape((a.shape[0],) + kvt + (a.shape[2],)), (0, 4, 1, 2, 3))[None]

    new_cmp_p = rows_last(r["kvct"])
    new_slc_p = rows_last(r["kvst"])
    new_win_p = rows_last(r["kvwt"][:, :, t - keep:])
    new_ssm_p = jnp.stack([hlast[0:nb], hlast[nb:2 * nb]], axis=-1).reshape(1, nb, N_SSM_GROUPS, SSM_STATE, 2)

    lays = _sample_layout(ns, ts)
    xs = x_sample.reshape(ns * ts, D_MODEL)
    rs = _inproj_sample(xs, lays, g1, wi)
    h0 = state_ssm[l].astype(F32).reshape(ns, N_STATE, 2)
    abr_s, hre, him = _s5_sample(rs["u"], h0[..., 0], h0[..., 1], sp, wglu, bglu, wso, ns, ts)
    onsa_s, new_win = _attn_sample(rs["q"].reshape(ns * ts, Q_W), rs["gn"].reshape(ns * ts, LANES),
                                   rs["kvs"].reshape(ns * ts, KV_W), rs["kvw"].reshape(ns * ts, KV_W),
                                   cache_cmp_kv[l], cache_slc_kv[l], cache_win_kv[l], page_table, cp, ns, ts, past_len)
    x1s, h2s, comb_s = _post(xs, abr_s, "b", onsa_s, g1, wi["wgab"], wno, wo, g2, mp["wr"], mp["br"], lays, "b")
    y_sample = _moe(x1s, h2s, comb_s, p_sample[l].reshape(ns, ts * PLE_DIM), mp, ns * ts, ts).reshape(ns, ts, D_MODEL)
    steps_first = lambda a: jnp.transpose(a.reshape((ts,) + kvt + (ns,)), (4, 0, 1, 2, 3))[None]
    new_cmp_s = steps_first(rs["kvct"])
    new_slc_s = steps_first(rs["kvst"])
    new_win_s = rows_last(new_win)
    new_ssm_s = jnp.stack([hre, him], axis=-1).reshape(1, ns, N_SSM_GROUPS, SSM_STATE, 2)
    return (y_prompt, y_sample, new_cmp_p, new_slc_p, new_win_p, new_ssm_p,
            new_cmp_s, new_slc_s, new_win_s, new_ssm_s)
```
